```python
import math
import jax, jax.numpy as jnp
from jax import lax
import numpy as np

D_MODEL = 1024
BATCH = 2
SEQ = 16384
DEPTH = 4

GRID_W = 64
CTX_LEN = 256
F32 = jnp.float32
EPS = 1e-6

HY_WIDTH = 256
POOL_WINDOWS = (2, 4, 8, 16)
POOL_WIDTH = 256
POOL_GROUP = POOL_WIDTH // len(POOL_WINDOWS)
MLA_HEADS = 8
QK_NOPE = 64
QK_ROPE = 32
V_DIM = 64
Q_LORA = 384
KV_LORA = 256
MLA_WIDTH = MLA_HEADS * V_DIM
D_MIX = HY_WIDTH + POOL_WIDTH + MLA_WIDTH
IN_WIDTH = 3 * HY_WIDTH + POOL_WIDTH + Q_LORA + KV_LORA + QK_ROPE
IN_SPLITS = (3 * HY_WIDTH,
             3 * HY_WIDTH + POOL_WIDTH,
             3 * HY_WIDTH + POOL_WIDTH + Q_LORA,
             3 * HY_WIDTH + POOL_WIDTH + Q_LORA + KV_LORA)
SM_SCALE = (QK_NOPE + QK_ROPE) ** -0.5
ROPE_THETA = 10000.0
Q_BLOCK = 128

SHORT_CONV = 3
FILT_EMB = 33
FILT_BANDS = (FILT_EMB - 1) // 2
FILT_HIDDEN = 64
DECAY_TARGET = 1e-2
FAST_DECAY_PCT = 0.3
SLOW_DECAY_PCT = 1.5

D_FF = 2816
N_EXPERTS = 8
TOP_K = 2
EXPERT_FF = 2816
N_DENSE = (DEPTH + 1) // 2
N_MOE = DEPTH // 2

kernel_name = 'hybrid_hyena_pool_mla_moe_dit'


def rmsnorm(x, g):
    x32 = x.astype(F32)
    y = x32 * lax.rsqrt(jnp.mean(x32 * x32, axis=-1, keepdims=True) + EPS)
    return (y * g.astype(F32)).astype(x.dtype)


def modulate(h, shift, scale):
    return h * (1 + scale) + shift


def short_conv(u, w, b):
    L = u.shape[1]
    p = SHORT_CONV // 2
    up = jnp.pad(u, ((0, 0), (p, p), (0, 0)))
    out = up[:, 0:L] * w[0]
    for j in range(1, SHORT_CONV):
        out = out + up[:, j:j + L] * w[j]
    return out + b


def implicit_filter(L, w1, b1, freq1, w2, b2, freq2, w3):
    pos = jnp.arange(L, dtype=F32)
    t = (pos / max(L - 1, 1))[:, None]
    w = 2.0 * math.pi * pos[:, None] / L
    f = jnp.linspace(1e-4, FILT_BANDS - 1, FILT_BANDS, dtype=F32)[None, :]
    feat = jnp.concatenate([t, jnp.cos(f * w), -jnp.sin(f * w)], axis=-1)
    hdn = jnp.sin(freq1.astype(F32) * (feat @ w1.astype(F32) + b1.astype(F32)))
    hdn = jnp.sin(freq2.astype(F32) * (hdn @ w2.astype(F32) + b2.astype(F32)))
    h = hdn @ w3.astype(F32)
    max_decay = math.log(DECAY_TARGET) / FAST_DECAY_PCT
    min_decay = math.log(DECAY_TARGET) / SLOW_DECAY_PCT
    deltas = jnp.linspace(min_decay, max_decay, HY_WIDTH, dtype=F32)
    decay = jnp.exp(-t * jnp.abs(deltas))
    h = h * jnp.concatenate([decay, decay], axis=-1)
    h_fwd, h_bwd = h[:, :HY_WIDTH], h[:, HY_WIDTH:]
    filt2 = jnp.concatenate([h_fwd, jnp.zeros((1, HY_WIDTH), F32), h_bwd[1:][::-1]], axis=0)
    return filt2 / jnp.sum(jnp.abs(filt2), axis=0, keepdims=True)


def fft_conv(z, filt2):
    L = z.shape[1]
    zf = jnp.fft.rfft(z, n=2 * L, axis=1)
    hf = jnp.fft.rfft(filt2, n=2 * L, axis=0)
    return jnp.fft.irfft(zf * hf[None], n=2 * L, axis=1)[:, :L]


def hyena_mix(u, sw, sb, filt, bias):
    uc = short_conv(u, sw, sb)
    x0, x1, v = jnp.split(uc, 3, axis=-1)
    z = (x1 * v).astype(F32)
    h = implicit_filter(u.shape[1], *filt)
    y = fft_conv(z, h) + bias.astype(F32) * z
    return (x0.astype(F32) * y).astype(u.dtype)


def pool_mix(p, pool_w, pool_scale):
    B, L, _ = p.shape
    p32 = p.astype(F32)
    cs = jnp.concatenate([jnp.zeros((B, 1, POOL_WIDTH), F32), jnp.cumsum(p32, axis=1)], axis=1)
    t = jnp.arange(L)
    parts = []
    for g, win in enumerate(POOL_WINDOWS):
        lo = jnp.clip(t - win // 2, 0, L)
        hi = jnp.clip(t + win // 2, 0, L)
        sl = slice(g * POOL_GROUP, (g + 1) * POOL_GROUP)
        cg = cs[..., sl]
        mean = (cg[:, hi] - cg[:, lo]) / (hi - lo).astype(F32)[None, :, None]
        parts.append(mean - p32[..., sl])
    d = jnp.stack(parts, axis=2)
    y = jnp.einsum('blgc,gcd->blgd', d, pool_w.astype(F32)).reshape(B, L, POOL_WIDTH)
    return (y * pool_scale.astype(F32)).astype(p.dtype)


def axial_rope(rows):
    n_freq = QK_ROPE // 4
    inv_freq = 1.0 / (ROPE_THETA ** (jnp.arange(n_freq, dtype=F32) / n_freq))
    row = jnp.repeat(jnp.arange(rows, dtype=F32), GRID_W)
    col = jnp.tile(jnp.arange(GRID_W, dtype=F32), rows)
    ang = jnp.concatenate([row[:, None] * inv_freq, col[:, None] * inv_freq], axis=-1)
    return jnp.cos(ang), jnp.sin(ang)


def apply_rope(x, cos, sin):
    x32 = x.astype(F32)
    half = x.shape[-1] // 2
    x1, x2 = x32[..., :half], x32[..., half:]
    return jnp.concatenate([x1 * cos - x2 * sin, x1 * sin + x2 * cos], axis=-1).astype(x.dtype)


def mla_q(cq, g, w_uq, rope):
    B, L, _ = cq.shape
    q = (rmsnorm(cq, g) @ w_uq).reshape(B, L, MLA_HEADS, QK_NOPE + QK_ROPE)
    q_nope, q_pe = q[..., :QK_NOPE], q[..., QK_NOPE:]
    if rope is not None:
        q_pe = apply_rope(q_pe, rope[0][:, None], rope[1][:, None])
    return jnp.concatenate([q_nope, q_pe], axis=-1)


def mla_kv(ckv, kpe, g, w_ukv, rope):
    B, L, _ = ckv.shape
    kv = (rmsnorm(ckv, g) @ w_ukv).reshape(B, L, MLA_HEADS, QK_NOPE + V_DIM)
    k_nope, v = kv[..., :QK_NOPE], kv[..., QK_NOPE:]
    if rope is not None:
        kpe = apply_rope(kpe, rope[0], rope[1])
    k_pe = jnp.broadcast_to(kpe[:, :, None, :], (B, L, MLA_HEADS, QK_ROPE))
    return jnp.concatenate([k_nope, k_pe], axis=-1), v


def attention(q, k, v):
    s = jnp.einsum('bqhd,bkhd->bhqk', q.astype(F32), k.astype(F32)) * SM_SCALE
    p = jax.nn.softmax(s, axis=-1)
    return jnp.einsum('bhqk,bkhd->bqhd', p, v.astype(F32)).astype(v.dtype)


def blocked_attention(q, k, v):
    B, L, H, Dq = q.shape
    nb = L // Q_BLOCK
    qb = q.reshape(B, nb, Q_BLOCK, H, Dq).swapaxes(0, 1)
    ob = lax.map(lambda qi: attention(qi, k, v), qb)
    return ob.swapaxes(0, 1).reshape(B, L, H, V_DIM)


def swiglu(h, wg, wu, wd):
    return (jax.nn.silu(h @ wg) * (h @ wu)) @ wd


def moe_swiglu(h, w_router, wg, wu, wd):
    logits = (h @ w_router).astype(F32)
    top_val, top_idx = lax.top_k(logits, TOP_K)
    gates = jax.nn.softmax(top_val, axis=-1)
    comb = jnp.sum(jax.nn.one_hot(top_idx, N_EXPERTS, dtype=F32) * gates[..., None], axis=-2)
    out = jnp.zeros(h.shape, F32)
    for e in range(N_EXPERTS):
        out = out + comb[..., e:e + 1] * swiglu(h, wg[e], wu[e], wd[e]).astype(F32)
    return out.astype(h.dtype)


def channel_mix(h, layer, ffn_wg, ffn_wu, ffn_wd, moe_router, moe_wg, moe_wu, moe_wd):
    i = layer // 2
    if layer % 2 == 0:
        return swiglu(h, ffn_wg[i], ffn_wu[i], ffn_wd[i])
    return moe_swiglu(h, moe_router[i], moe_wg[i], moe_wu[i], moe_wd[i])


def setup_inputs(seed: int = 0) -> dict:
    key = jax.random.key(seed)
    keys = iter(jax.random.split(key, 48))
    D = D_MODEL

    def normal(shape, scale=1.0):
        return jax.random.normal(next(keys), shape, F32) * scale

    def gain(shape):
        return 1.0 + normal(shape, 0.02)

    return {
        'x': normal((BATCH, SEQ, D)),
        'c': normal((BATCH, D)),
        'ctx': normal((BATCH, CTX_LEN, D)),
        'c_ctx': normal((D,)),
        'norm1_g': gain((DEPTH, D)),
        'norm2_g': gain((DEPTH, D)),
        'w_ada': normal((DEPTH, D, 6 * D), 0.5 * D ** -0.5),
        'b_ada': normal((DEPTH, 6 * D), 0.02),
        'w_in': normal((DEPTH, D, IN_WIDTH), D ** -0.5),
        'hy_short_w': normal((DEPTH, SHORT_CONV, 3 * HY_WIDTH), SHORT_CONV ** -0.5),
        'hy_short_b': normal((DEPTH, 3 * HY_WIDTH), 0.02),
        'filt_w1': normal((DEPTH, FILT_EMB, FILT_HIDDEN), FILT_EMB ** -0.5),
        'filt_b1': normal((DEPTH, FILT_HIDDEN), 0.02),
        'filt_freq1': gain((DEPTH, FILT_HIDDEN)),
        'filt_w2': normal((DEPTH, FILT_HIDDEN, FILT_HIDDEN), FILT_HIDDEN ** -0.5),
        'filt_b2': normal((DEPTH, FILT_HIDDEN), 0.02),
        'filt_freq2': gain((DEPTH, FILT_HIDDEN)),
        'filt_w3': normal((DEPTH, FILT_HIDDEN, 2 * HY_WIDTH), FILT_HIDDEN ** -0.5),
        'hy_bias': normal((DEPTH, HY_WIDTH), 0.5),
        'pool_w': normal((DEPTH, len(POOL_WINDOWS), POOL_GROUP, POOL_GROUP), POOL_GROUP ** -0.5),
        'pool_scale': gain((DEPTH, POOL_WIDTH)),
        'q_norm_g': gain((DEPTH, Q_LORA)),
        'w_uq': normal((DEPTH, Q_LORA, MLA_HEADS * (QK_NOPE + QK_ROPE)), Q_LORA ** -0.5),
        'kv_norm_g': gain((DEPTH, KV_LORA)),
        'w_ukv': normal((DEPTH, KV_LORA, MLA_HEADS * (QK_NOPE + V_DIM)), KV_LORA ** -0.5),
        'w_out': normal((DEPTH, D_MIX, D), D_MIX ** -0.5),
        'ffn_wg': normal((N_DENSE, D, D_FF), D ** -0.5),
        'ffn_wu': normal((N_DENSE, D, D_FF), D ** -0.5),
        'ffn_wd': normal((N_DENSE, D_FF, D), D_FF ** -0.5),
        'moe_router': normal((N_MOE, D, N_EXPERTS), D ** -0.5),
        'moe_wg': normal((N_MOE, N_EXPERTS, D, EXPERT_FF), D ** -0.5),
        'moe_wu': normal((N_MOE, N_EXPERTS, D, EXPERT_FF), D ** -0.5),
        'moe_wd': normal((N_MOE, N_EXPERTS, EXPERT_FF, D), EXPERT_FF ** -0.5),
        'final_g': gain((D,)),
    }


def reference(x, c, ctx, c_ctx, norm1_g, norm2_g, w_ada, b_ada, w_in, hy_short_w, hy_short_b,
              filt_w1, filt_b1, filt_freq1, filt_w2, filt_b2, filt_freq2, filt_w3, hy_bias,
              pool_w, pool_scale, q_norm_g, w_uq, kv_norm_g, w_ukv, w_out,
              ffn_wg, ffn_wu, ffn_wd, moe_router, moe_wg, moe_wu, moe_wd, final_g):
    B, L, _ = x.shape
    ROWS = L // GRID_W
    rope = axial_rope(ROWS)
    silu_c = jax.nn.silu(c)
    silu_ctx = jax.nn.silu(c_ctx)
    xl, xc = x, ctx
    for l in range(DEPTH):
        last = l == DEPTH - 1
        mod_l = jnp.split((silu_c @ w_ada[l] + b_ada[l])[:, None, :], 6, axis=-1)
        mod_c = jnp.split(silu_ctx @ w_ada[l] + b_ada[l], 6, axis=-1)
        filt = (filt_w1[l], filt_b1[l], filt_freq1[l], filt_w2[l], filt_b2[l], filt_freq2[l], filt_w3[l])

        def mix_heads(hy, pool, att):
            y_hy = hyena_mix(hy, hy_short_w[l], hy_short_b[l], filt, hy_bias[l])
            y_pool = pool_mix(pool, pool_w[l], pool_scale[l])
            y_att = att.reshape(att.shape[0], att.shape[1], MLA_WIDTH)
            return jnp.concatenate([y_hy, y_pool, y_att], axis=-1) @ w_out[l]

        hl = modulate(rmsnorm(xl, norm1_g[l]), mod_l[0], mod_l[1])
        hc = modulate(rmsnorm(xc, norm1_g[l]), mod_c[0], mod_c[1])
        hy_l, pool_l, cq_l, ckv_l, kpe_l = jnp.split(hl @ w_in[l], IN_SPLITS, axis=-1)
        hy_c, pool_c, cq_c, ckv_c, kpe_c = jnp.split(hc @ w_in[l], IN_SPLITS, axis=-1)
        k_c, v_c = mla_kv(ckv_c, kpe_c, kv_norm_g[l], w_ukv[l], None)
        k_l, v_l = mla_kv(ckv_l, kpe_l, kv_norm_g[l], w_ukv[l], rope)
        q_l = mla_q(cq_l, q_norm_g[l], w_uq[l], rope)
        att_l = blocked_attention(q_l, jnp.concatenate([k_c, k_l], axis=1),
                                  jnp.concatenate([v_c, v_l], axis=1))
        xl = xl + mod_l[2] * mix_heads(hy_l, pool_l, att_l)
        if not last:
            att_c = attention(mla_q(cq_c, q_norm_g[l], w_uq[l], None), k_c, v_c)
            xc = xc + mod_c[2] * mix_heads(hy_c, pool_c, att_c)
            hc = modulate(rmsnorm(xc, norm2_g[l]), mod_c[3], mod_c[4])
            xc = xc + mod_c[5] * channel_mix(hc, l, ffn_wg, ffn_wu, ffn_wd, moe_router, moe_wg, moe_wu, moe_wd)
        hl = modulate(rmsnorm(xl, norm2_g[l]), mod_l[3], mod_l[4])
        xl = xl + mod_l[5] * channel_mix(hl, l, ffn_wg, ffn_wu, ffn_wd, moe_router, moe_wg, moe_wu, moe_wd)
    return rmsnorm(xl, final_g)
```

```python
import functools
import math

import jax
import jax.numpy as jnp
import numpy as np
from jax import lax
from jax.experimental import pallas as pl
from jax.experimental.pallas import tpu as pltpu

F32 = jnp.float32
BF16 = jnp.bfloat16
HIGHEST = lax.Precision.HIGHEST

EPS = 1e-6
GRID_W = 64
HY_WIDTH = 256
POOL_WINDOWS = (2, 4, 8, 16)
POOL_WIDTH = 256
POOL_GROUP = 64
POOL_HALO = 8
MLA_HEADS = 8
QK_NOPE = 64
QK_ROPE = 32
V_DIM = 64
Q_LORA = 384
KV_LORA = 256
HEAD_PAD = 128
SM_SCALE = (QK_NOPE + QK_ROPE) ** -0.5
ROPE_THETA = 10000.0
FILT_EMB = 33
FILT_BANDS = 16
DECAY_TARGET = 1e-2
FAST_DECAY_PCT = 0.3
SLOW_DECAY_PCT = 1.5
N_EXPERTS = 8
LANES = 128
FFT_N2 = 128
VMEM_LIMIT = 56 * 2 ** 20


def _cp(*sem):
    return pltpu.CompilerParams(dimension_semantics=sem, vmem_limit_bytes=VMEM_LIMIT)


def _dot(a, b, **kw):
    return jnp.dot(a, b, preferred_element_type=F32, **kw)


def _normmod(x, g, shift, scale):
    ms = jnp.mean(x * x, axis=-1, keepdims=True)
    return x * lax.rsqrt(ms + EPS) * (g * (1.0 + scale)) + shift


def _ada_kernel(c_ref, w_ref, b_ref, o_ref):
    c = c_ref[...]
    s = c * jax.nn.sigmoid(c)
    o_ref[0] = _dot(s, w_ref[0], precision=HIGHEST) + b_ref[0]


def ada_vectors(cc, w_ada, b_ada):
    depth, d, n = w_ada.shape
    tn = 1536
    return pl.pallas_call(
        _ada_kernel,
        grid=(depth, n // tn),
        in_specs=[pl.BlockSpec((8, d), lambda l, j: (0, 0)),
                  pl.BlockSpec((1, d, tn), lambda l, j: (l, 0, j)),
                  pl.BlockSpec((1, 1, tn), lambda l, j: (l, 0, j))],
        out_specs=pl.BlockSpec((1, 8, tn), lambda l, j: (l, 0, j)),
        out_shape=jax.ShapeDtypeStruct((depth, 8, n), F32),
        compiler_params=_cp("parallel", "parallel"),
        name="ada_vectors",
    )(cc, w_ada, b_ada.reshape(depth, 1, n))


IN_HY, IN_POOL, IN_CQ, IN_CKV, IN_KPE = 768, 256, 384, 256, 32
IN_OFFS = (0, 768, 1024, 1408, 1664, 1696)


def _inproj_kernel(x_ref, g_ref, sh_ref, sc_ref, w_ref, hy_ref, pool_ref, cq_ref, ckv_ref, kpe_ref):
    h = _normmod(x_ref[0], g_ref[...], sh_ref[0], sc_ref[0]).astype(BF16)
    y = _dot(h, w_ref[...])
    o = IN_OFFS
    hy_ref[0] = y[:, o[0]:o[1]]
    pool_ref[0] = y[:, o[1]:o[2]]
    cq_ref[0] = y[:, o[2]:o[3]]
    ckv_ref[0] = y[:, o[3]:o[4]]
    kpe_ref[0] = y[:, o[4]:o[5]]


def in_projection(x, g, shift, scale, w_in):
    b, l, d = x.shape
    tm = min(512, l)
    widths = (IN_HY, IN_POOL, IN_CQ, IN_CKV, IN_KPE)
    row = lambda bi, i: (bi, i, 0)
    vec = lambda bi, i: (bi, 0, 0)
    return pl.pallas_call(
        _inproj_kernel,
        grid=(b, l // tm),
        in_specs=[pl.BlockSpec((1, tm, d), row),
                  pl.BlockSpec((1, d), lambda bi, i: (0, 0)),
                  pl.BlockSpec((1, 1, d), vec),
                  pl.BlockSpec((1, 1, d), vec),
                  pl.BlockSpec(w_in.shape, lambda bi, i: (0, 0))],
        out_specs=[pl.BlockSpec((1, tm, w), row) for w in widths],
        out_shape=[jax.ShapeDtypeStruct((b, l, w), F32) for w in widths],
        compiler_params=_cp("parallel", "parallel"),
        name="in_projection",
    )(x, g, shift, scale, w_in)


def _mla_kernel(cq_ref, ckv_ref, kpe_ref, c_ref, s_ref, wqa_ref, wqb_ref, wk_ref, wv_ref,
                pa_ref, pb_ref, q_ref, kt_ref, v_ref):
    cq = cq_ref[0]
    cqn = (cq * lax.rsqrt(jnp.mean(cq * cq, axis=-1, keepdims=True) + EPS)).astype(BF16)
    ckv = ckv_ref[0]
    ckvn = (ckv * lax.rsqrt(jnp.mean(ckv * ckv, axis=-1, keepdims=True) + EPS)).astype(BF16)
    kpe = kpe_ref[0].astype(BF16)
    cs, sn = c_ref[...], s_ref[...]
    qa = _dot(cqn, wqa_ref[...])
    qb = _dot(cqn, wqb_ref[...])
    kn = _dot(ckvn, wk_ref[...])
    vv = _dot(ckvn, wv_ref[...])
    kpa = _dot(kpe, pa_ref[...])
    kpb = _dot(kpe, pb_ref[...])
    kpe_rot = kpa * cs + kpb * sn
    lane = lax.broadcasted_iota(jnp.int32, (1, HEAD_PAD), 1)
    for h in range(MLA_HEADS):
        sl = slice(h * HEAD_PAD, (h + 1) * HEAD_PAD)
        q_ref[0, h] = (qa[:, sl] * cs + qb[:, sl] * sn).astype(BF16)
        kt_ref[0, h] = (kn[:, sl] + kpe_rot).T.astype(BF16)
        one_lane = V_DIM if h % 2 == 0 else 0
        v_ref[0, h] = jnp.where(lane == one_lane, 1.0, vv[:, sl]).astype(BF16)


def mla_project(cq, ckv, kpe, ctab, stab, wts):
    b, l, _ = cq.shape
    tm = min(512, l)
    row = lambda bi, i: (bi, i, 0)
    full = lambda a: pl.BlockSpec(a.shape, lambda bi, i: (0,) * a.ndim)
    hw = MLA_HEADS
    return pl.pallas_call(
        _mla_kernel,
        grid=(b, l // tm),
        in_specs=[pl.BlockSpec((1, tm, Q_LORA), row),
                  pl.BlockSpec((1, tm, KV_LORA), row),
                  pl.BlockSpec((1, tm, QK_ROPE), row),
                  pl.BlockSpec((tm, HEAD_PAD), lambda bi, i: (i, 0)),
                  pl.BlockSpec((tm, HEAD_PAD), lambda bi, i: (i, 0))] + [full(w) for w in wts],
        out_specs=[pl.BlockSpec((1, hw, tm, HEAD_PAD), lambda bi, i: (bi, 0, i, 0)),
                   pl.BlockSpec((1, hw, HEAD_PAD, tm), lambda bi, i: (bi, 0, 0, i)),
                   pl.BlockSpec((1, hw, tm, HEAD_PAD), lambda bi, i: (bi, 0, i, 0))],
        out_shape=[jax.ShapeDtypeStruct((b, hw, l, HEAD_PAD), BF16),
                   jax.ShapeDtypeStruct((b, hw, HEAD_PAD, l), BF16),
                   jax.ShapeDtypeStruct((b, hw, l, HEAD_PAD), BF16)],
        compiler_params=_cp("parallel", "parallel"),
        name="mla_project",
    )(cq, ckv, kpe, ctab, stab, *wts)


def mla_weights(q_norm_g, w_uq, kv_norm_g, w_ukv):
    hw, dq = MLA_HEADS, QK_NOPE + QK_ROPE
    half = QK_ROPE // 2
    wq = (w_uq * q_norm_g[:, None] * (SM_SCALE * math.log2(math.e))).reshape(Q_LORA, hw, dq)
    pad = jnp.zeros((Q_LORA, hw, HEAD_PAD - dq), F32)
    wqa = jnp.concatenate([wq, pad], -1)
    swap = jnp.concatenate([wq[..., QK_NOPE + half:], wq[..., QK_NOPE:QK_NOPE + half]], -1)
    wqb = jnp.concatenate([jnp.zeros((Q_LORA, hw, QK_NOPE), F32), swap, pad], -1)
    wkv = (w_ukv * kv_norm_g[:, None]).reshape(KV_LORA, hw, QK_NOPE + V_DIM)
    z64 = jnp.zeros((KV_LORA, hw, 64), F32)
    wk = jnp.concatenate([wkv[..., :QK_NOPE], z64], -1)
    v = wkv[..., QK_NOPE:]
    wv = jnp.where((jnp.arange(hw) % 2 == 0)[None, :, None],
                   jnp.concatenate([v, z64], -1), jnp.concatenate([z64, v], -1))
    eye = np.eye(QK_ROPE, dtype=np.float32)
    pa = np.zeros((QK_ROPE, HEAD_PAD), np.float32)
    pa[:, QK_NOPE:QK_NOPE + QK_ROPE] = eye
    pb = np.zeros((QK_ROPE, HEAD_PAD), np.float32)
    pb[:, QK_NOPE:QK_NOPE + QK_ROPE] = np.concatenate([eye[:, half:], eye[:, :half]], 1)
    flat = lambda w: w.reshape(w.shape[0], hw * HEAD_PAD).astype(BF16)
    return (flat(wqa), flat(wqb), flat(wk), flat(wv), jnp.asarray(pa, BF16), jnp.asarray(pb, BF16))


def rope_tables(l, with_rope):
    ctab = np.zeros((l, HEAD_PAD), np.float32)
    stab = np.zeros((l, HEAD_PAD), np.float32)
    ctab[:, :QK_NOPE + QK_ROPE] = 1.0
    if not with_rope:
        return jnp.asarray(ctab), jnp.asarray(stab)
    n_freq = QK_ROPE // 4
    inv_freq = 1.0 / (ROPE_THETA ** (jnp.arange(n_freq, dtype=F32) / n_freq))
    rows = l // GRID_W
    row = jnp.repeat(jnp.arange(rows, dtype=F32), GRID_W)
    col = jnp.tile(jnp.arange(GRID_W, dtype=F32), rows)
    ang = jnp.concatenate([row[:, None] * inv_freq, col[:, None] * inv_freq], axis=-1)
    cos, sin = jnp.cos(ang), jnp.sin(ang)
    one = jnp.ones((l, QK_NOPE), F32)
    zero = jnp.zeros((l, QK_NOPE), F32)
    tail = jnp.zeros((l, HEAD_PAD - QK_NOPE - QK_ROPE), F32)
    return (jnp.concatenate([one, cos, cos, tail], -1), jnp.concatenate([zero, -sin, sin, tail], -1))


ATT_TQ = 256
ATT_CK = 512


def _attn_kernel(q_ref, kct_ref, vc_ref, *rest, n_chunks):
    if n_chunks:
        kt_ref, v_ref, o_ref = rest
    else:
        (o_ref,) = rest
    lane = lax.broadcasted_iota(jnp.int32, (1, HEAD_PAD), 1)
    outs = []
    for hh in range(2):
        q = q_ref[0, hh]
        s = _dot(q, kct_ref[0, hh])
        m = jnp.max(s, axis=-1, keepdims=True)
        p = jnp.exp2(s - m).astype(BF16)
        acc = _dot(p, vc_ref[0, hh])
        if n_chunks:
            def body(c, carry, q=q, hh=hh):
                m, acc = carry
                off = pl.multiple_of(c * ATT_CK, ATT_CK)
                s = _dot(q, kt_ref[0, hh, :, pl.ds(off, ATT_CK)])
                m_new = jnp.maximum(m, jnp.max(s, axis=-1, keepdims=True))
                alpha = jnp.exp2(m - m_new)
                p = jnp.exp2(s - m_new).astype(BF16)
                acc = acc * alpha + _dot(p, v_ref[0, hh, pl.ds(off, ATT_CK), :])
                return m_new, acc
            m, acc = lax.fori_loop(0, n_chunks, body, (m, acc))
        one_lane = V_DIM if hh == 0 else 0
        denom = jnp.sum(jnp.where(lane == one_lane, acc, 0.0), axis=-1, keepdims=True)
        outs.append(acc / denom)
    o_ref[0] = jnp.where(lane < V_DIM, outs[0], outs[1]).astype(o_ref.dtype)


def attention(q, kct, vc, kt=None, v=None):
    b, hw, lq, _ = q.shape
    lc = vc.shape[2]
    tq = min(ATT_TQ, lq)
    n_chunks = 0 if kt is None else kt.shape[3] // ATT_CK
    in_specs = [pl.BlockSpec((1, 2, tq, HEAD_PAD), lambda bi, hp, i: (bi, hp, i, 0)),
                pl.BlockSpec((1, 2, HEAD_PAD, lc), lambda bi, hp, i: (bi, hp, 0, 0)),
                pl.BlockSpec((1, 2, lc, HEAD_PAD), lambda bi, hp, i: (bi, hp, 0, 0))]
    args = [q, kct, vc]
    if n_chunks:
        l = kt.shape[3]
        in_specs += [pl.BlockSpec((1, 2, HEAD_PAD, l), lambda bi, hp, i: (bi, hp, 0, 0),
                                  pipeline_mode=pl.Buffered(1)),
                     pl.BlockSpec((1, 2, l, HEAD_PAD), lambda bi, hp, i: (bi, hp, 0, 0),
                                  pipeline_mode=pl.Buffered(1))]
        args += [kt, v]
    return pl.pallas_call(
        functools.partial(_attn_kernel, n_chunks=n_chunks),
        grid=(b, hw // 2, lq // tq),
        in_specs=in_specs,
        out_specs=pl.BlockSpec((1, tq, HEAD_PAD), lambda bi, hp, i: (bi, i, hp)),
        out_shape=jax.ShapeDtypeStruct((b, lq, hw * V_DIM), BF16),
        compiler_params=_cp("parallel", "parallel", "arbitrary"),
        name="attention",
    )(*args)


def _halo_specs(tm, width, nblk):
    r = tm // POOL_HALO
    last = nblk * r - 1
    return [pl.BlockSpec((1, tm, width), lambda bi, i: (bi, i, 0)),
            pl.BlockSpec((1, POOL_HALO, width), lambda bi, i: (bi, jnp.maximum(i * r - 1, 0), 0)),
            pl.BlockSpec((1, POOL_HALO, width), lambda bi, i: (bi, jnp.minimum((i + 1) * r, last), 0))]


def _fill_padded(buf_ref, cur_ref, prev_ref, next_ref, tm):
    i = pl.program_id(1)
    first = i == 0
    last = i == pl.num_programs(1) - 1
    buf_ref[0:POOL_HALO] = jnp.where(first, 0.0, prev_ref[0])
    buf_ref[POOL_HALO:POOL_HALO + tm] = cur_ref[0]
    buf_ref[POOL_HALO + tm:] = jnp.where(last, 0.0, next_ref[0])


def _hyena_pre_kernel(u_ref, up_ref, un_ref, w_ref, b_ref, z_ref, x0_ref, buf_ref, *, tm):
    _fill_padded(buf_ref, u_ref, up_ref, un_ref, tm)
    w = w_ref[...]
    uc = (buf_ref[POOL_HALO - 1:POOL_HALO - 1 + tm] * w[0:1] + buf_ref[POOL_HALO:POOL_HALO + tm] * w[1:2]
          + buf_ref[POOL_HALO + 1:POOL_HALO + 1 + tm] * w[2:3] + b_ref[...])
    x0_ref[0] = uc[:, :HY_WIDTH]
    z_ref[0] = uc[:, HY_WIDTH:2 * HY_WIDTH] * uc[:, 2 * HY_WIDTH:]


def hyena_pre(u, sw, sb):
    b, l, wd = u.shape
    tm = min(512, l)
    nblk = l // tm
    return pl.pallas_call(
        functools.partial(_hyena_pre_kernel, tm=tm),
        grid=(b, nblk),
        in_specs=_halo_specs(tm, wd, nblk) + [pl.BlockSpec((3, wd), lambda bi, i: (0, 0)),
                                              pl.BlockSpec((1, wd), lambda bi, i: (0, 0))],
        out_specs=[pl.BlockSpec((1, tm, HY_WIDTH), lambda bi, i: (bi, i, 0))] * 2,
        out_shape=[jax.ShapeDtypeStruct((b, l, HY_WIDTH), F32)] * 2,
        scratch_shapes=[pltpu.VMEM((tm + 2 * POOL_HALO, wd), F32)],
        compiler_params=_cp("parallel", "parallel"),
        name="hyena_pre",
    )(u, u, u, sw, sb.reshape(1, wd))


def _filter_kernel(feat_ref, tdec_ref, w1_ref, b1_ref, f1_ref, w2_ref, b2_ref, f2_ref, w3_ref, dl_ref,
                   filt_ref, norm_ref, *, l, tr):
    i = pl.program_id(0)
    h = jnp.sin(f1_ref[...] * (_dot(feat_ref[...], w1_ref[...], precision=HIGHEST) + b1_ref[...]))
    h = jnp.sin(f2_ref[...] * (_dot(h, w2_ref[...], precision=HIGHEST) + b2_ref[...]))
    h = _dot(h, w3_ref[...], precision=HIGHEST)
    decay = jnp.exp(-tdec_ref[...] * jnp.abs(dl_ref[...]))
    n = i * tr + lax.broadcasted_iota(jnp.int32, (tr, 1), 0)
    f = jnp.where(n < l, h[:, :HY_WIDTH], h[:, HY_WIDTH:]) * decay
    f = jnp.where(n == l, 0.0, f)
    filt_ref[...] = f

    @pl.when(i == 0)
    def _():
        norm_ref[...] = jnp.zeros_like(norm_ref)
    norm_ref[...] += jnp.sum(jnp.abs(f), axis=0, keepdims=True)


def filter_tables(l):
    n = jnp.arange(2 * l)
    pos = jnp.where(n < l, n, 2 * l - n).astype(F32)[:, None]
    t = pos / max(l - 1, 1)
    w = 2.0 * math.pi * pos / l
    f = jnp.linspace(1e-4, FILT_BANDS - 1, FILT_BANDS, dtype=F32)[None, :]
    feat = jnp.concatenate([t, jnp.cos(f * w), -jnp.sin(f * w),
                            jnp.zeros((2 * l, LANES - FILT_EMB), F32)], axis=-1)
    return feat, t


def hyena_filter(l, tables, w1, b1, f1, w2, b2, f2, w3):
    feat, t = tables
    tr = min(512, 2 * l)
    hid = w2.shape[0]
    max_decay = math.log(DECAY_TARGET) / FAST_DECAY_PCT
    min_decay = math.log(DECAY_TARGET) / SLOW_DECAY_PCT
    deltas = jnp.linspace(min_decay, max_decay, HY_WIDTH, dtype=F32)[None, :]
    w1p = jnp.concatenate([w1, jnp.zeros((LANES - FILT_EMB, hid), F32)], 0)
    full = lambda a: pl.BlockSpec(a.shape, lambda i: (0,) * a.ndim)
    smalls = [w1p, b1.reshape(1, hid), f1.reshape(1, hid), w2, b2.reshape(1, hid), f2.reshape(1, hid), w3, deltas]
    return pl.pallas_call(
        functools.partial(_filter_kernel, l=l, tr=tr),
        grid=(2 * l // tr,),
        in_specs=[pl.BlockSpec((tr, LANES), lambda i: (i, 0)),
                  pl.BlockSpec((tr, 1), lambda i: (i, 0))] + [full(a) for a in smalls],
        out_specs=[pl.BlockSpec((tr, HY_WIDTH), lambda i: (i, 0)),
                   pl.BlockSpec((1, HY_WIDTH), lambda i: (0, 0))],
        out_shape=[jax.ShapeDtypeStruct((2 * l, HY_WIDTH), F32),
                   jax.ShapeDtypeStruct((1, HY_WIDTH), F32)],
        compiler_params=_cp("arbitrary"),
        name="hyena_filter",
    )(feat, t, *smalls)


def _dft_rows_kernel(f_ref, x_ref, o_ref):
    o_ref[0] = _dot(f_ref[...], x_ref[0].astype(BF16)).astype(o_ref.dtype)


def dft_rows(fmat, x, out_dtype, scale_cols=None):
    b, k, w = x.shape
    m = fmat.shape[0]
    tn = min(2048, w)
    return pl.pallas_call(
        _dft_rows_kernel,
        grid=(b, w // tn),
        in_specs=[pl.BlockSpec((m, k), lambda bi, j: (0, 0)),
                  pl.BlockSpec((1, k, tn), lambda bi, j: (bi, 0, j))],
        out_specs=pl.BlockSpec((1, m, tn), lambda bi, j: (bi, 0, j)),
        out_shape=jax.ShapeDtypeStruct((b, m, w), out_dtype),
        compiler_params=_cp("parallel", "parallel"),
        name="dft_rows",
    )(fmat, x)


def _spectrum_kernel(g_ref, a_ref, h_ref):
    n2 = a_ref.shape[3]
    g = g_ref[0]
    p1 = _dot(g, a_ref[0, 0, 0])
    p2 = _dot(g, a_ref[0, 1, 0])
    h_ref[0, 0] = p1[:n2] - p2[n2:]
    h_ref[1, 0] = p2[:n2] + p1[n2:]


def filter_spectrum(gs, a):
    _, _, n1, n2, c = a.shape
    return pl.pallas_call(
        _spectrum_kernel,
        grid=(n1,),
        in_specs=[pl.BlockSpec((1, 2 * n2, n2), lambda k: (k, 0, 0)),
                  pl.BlockSpec((1, 2, 1, n2, c), lambda k: (0, 0, k, 0, 0))],
        out_specs=pl.BlockSpec((2, 1, n2, c), lambda k: (0, k, 0, 0)),
        out_shape=jax.ShapeDtypeStruct((2, n1, n2, c), F32),
        compiler_params=_cp("parallel"),
        name="filter_spectrum",
    )(gs, a)


def _spectral_kernel(g_ref, gt_ref, h_ref, a_ref, b_ref):
    n2 = a_ref.shape[3]
    g = g_ref[0]
    p1 = _dot(g, a_ref[0, 0, 0])
    p2 = _dot(g, a_ref[0, 1, 0])
    xre = p1[:n2] - p2[n2:]
    xim = p2[:n2] + p1[n2:]
    hre, him = h_ref[0, 0], h_ref[1, 0]
    yre = (xre * hre - xim * him).astype(BF16)
    yim = (xre * him + xim * hre).astype(BF16)
    gt = gt_ref[0]
    q1 = _dot(gt, yre)
    q2 = _dot(gt, yim)
    b_ref[0, 0, 0] = (q1[:n2] + q2[n2:]).astype(b_ref.dtype)
    b_ref[0, 1, 0] = (q2[:n2] - q1[n2:]).astype(b_ref.dtype)


def spectral_multiply(gs, gts, hf, a):
    b, _, n1, n2, c = a.shape
    return pl.pallas_call(
        _spectral_kernel,
        grid=(n1, b),
        in_specs=[pl.BlockSpec((1, 2 * n2, n2), lambda k, bi: (k, 0, 0)),
                  pl.BlockSpec((1, 2 * n2, n2), lambda k, bi: (k, 0, 0)),
                  pl.BlockSpec((2, 1, n2, c), lambda k, bi: (0, k, 0, 0)),
                  pl.BlockSpec((1, 2, 1, n2, c), lambda k, bi: (bi, 0, k, 0, 0))],
        out_specs=pl.BlockSpec((1, 2, 1, n2, c), lambda k, bi: (bi, 0, k, 0, 0)),
        out_shape=jax.ShapeDtypeStruct(a.shape, BF16),
        compiler_params=_cp("parallel", "arbitrary"),
        name="spectral_multiply",
    )(gs, gts, hf, a)


def _idft_rows_kernel(f_ref, bm_ref, x0_ref, z_ref, inv_ref, bias_ref, o_ref, *, inv_n):
    y = _dot(f_ref[...], bm_ref[0]) * inv_n
    z = z_ref[0]
    o_ref[0] = (x0_ref[0] * (y * inv_ref[...] + bias_ref[...] * z)).astype(o_ref.dtype)


def idft_rows_gate(fmat, bm, x0, z, inv_norm, bias, n):
    b, k, w = bm.shape
    m = fmat.shape[0]
    tn = min(2048, w)
    col = lambda bi, j: (bi, 0, j)
    return pl.pallas_call(
        functools.partial(_idft_rows_kernel, inv_n=1.0 / n),
        grid=(b, w // tn),
        in_specs=[pl.BlockSpec((m, k), lambda bi, j: (0, 0)),
                  pl.BlockSpec((1, k, tn), col),
                  pl.BlockSpec((1, m, tn), col),
                  pl.BlockSpec((1, m, tn), col),
                  pl.BlockSpec((1, tn), lambda bi, j: (0, j)),
                  pl.BlockSpec((1, tn), lambda bi, j: (0, j))],
        out_specs=pl.BlockSpec((1, m, tn), col),
        out_shape=jax.ShapeDtypeStruct((b, m, w), BF16),
        compiler_params=_cp("parallel", "parallel"),
        name="idft_rows_gate",
    )(fmat, bm, x0, z, inv_norm, bias)


def dft_tables(l):
    n = 2 * l
    n2 = FFT_N2
    n1 = n // n2
    k1 = np.arange(n1)
    ang1 = 2.0 * np.pi * np.outer(k1, k1) / n1
    f1 = np.concatenate([np.cos(ang1), -np.sin(ang1)], 0)
    f3 = np.concatenate([np.cos(ang1), -np.sin(ang1)], 1)[: n1 // 2]
    kk = (jnp.arange(n1, dtype=jnp.int32)[:, None, None] + n1 * jnp.arange(n2, dtype=jnp.int32)[None, :, None])
    prod = (kk * jnp.arange(n2, dtype=jnp.int32)[None, None, :]) % n
    ang = prod.astype(F32) * (2.0 * math.pi / n)
    gre, gim = jnp.cos(ang), -jnp.sin(ang)
    gs = jnp.concatenate([gre, gim], 1).astype(BF16)
    gts = jnp.concatenate([gre.transpose(0, 2, 1), gim.transpose(0, 2, 1)], 1).astype(BF16)
    return dict(n=n, n1=n1, n2=n2, f1=jnp.asarray(f1, BF16), f3=jnp.asarray(f3, BF16), gs=gs, gts=gts)


def hyena_long_conv(z, x0, filt2, norm, bias, tabs):
    b, l, c = z.shape
    n, n1, n2 = tabs["n"], tabs["n1"], tabs["n2"]
    w = n2 * c
    fa = dft_rows(tabs["f1"], filt2.reshape(1, n1, w), BF16)
    hf = filter_spectrum(tabs["gs"], fa.reshape(1, 2, n1, n2, c))
    a = dft_rows(tabs["f1"][:, : n1 // 2], z.reshape(b, n1 // 2, w), BF16)
    bm = spectral_multiply(tabs["gs"], tabs["gts"], hf, a.reshape(b, 2, n1, n2, c))
    tile = lambda v: jnp.tile(v.reshape(1, c), (1, n2))
    y = idft_rows_gate(tabs["f3"], bm.reshape(b, 2 * n1, w), x0.reshape(b, n1 // 2, w),
                       z.reshape(b, n1 // 2, w), tile(1.0 / norm), tile(bias), n)
    return y.reshape(b, l, c)


def _small_conv_kernel(ff_ref, fi_ref, filt_ref, z_ref, x0_ref, inv_ref, bias_ref, o_ref, *, l):
    ff = ff_ref[...]
    hs = _dot(ff, filt_ref[...].astype(BF16))
    hre, him = hs[:2 * l], hs[2 * l:]
    z = z_ref[0]
    xs = _dot(ff[:, :l], z.astype(BF16))
    xre, xim = xs[:2 * l], xs[2 * l:]
    yre = (xre * hre - xim * him).astype(BF16)
    yim = (xre * him + xim * hre).astype(BF16)
    y = _dot(fi_ref[...], jnp.concatenate([yre, yim], 0)) * (0.5 / l)
    o_ref[0] = (x0_ref[0] * (y * inv_ref[...] + bias_ref[...] * z)).astype(o_ref.dtype)


def small_long_conv(z, x0, filt2, norm, bias):
    b, l, c = z.shape
    k = np.arange(2 * l)
    ang = 2.0 * np.pi * np.outer(k, k) / (2 * l)
    ff = jnp.asarray(np.concatenate([np.cos(ang), -np.sin(ang)], 0), BF16)
    fi = jnp.asarray(np.concatenate([np.cos(ang), -np.sin(ang)], 1)[:l], BF16)
    full = lambda a: pl.BlockSpec(a.shape, lambda bi: (0,) * a.ndim)
    row = pl.BlockSpec((1, l, c), lambda bi: (bi, 0, 0))
    inv = 1.0 / norm
    return pl.pallas_call(
        functools.partial(_small_conv_kernel, l=l),
        grid=(b,),
        in_specs=[full(ff), full(fi), full(filt2), row, row, full(inv), full(bias)],
        out_specs=row,
        out_shape=jax.ShapeDtypeStruct((b, l, c), BF16),
        compiler_params=_cp("parallel"),
        name="small_long_conv",
    )(ff, fi, filt2, z, x0, inv, bias)


def _pool_kernel(p_ref, pp_ref, pn_ref, w_ref, sc_ref, o_ref, buf_ref, *, tm, l):
    _fill_padded(buf_ref, p_ref, pp_ref, pn_ref, tm)
    t = pl.program_id(1) * tm + lax.broadcasted_iota(jnp.int32, (tm, LANES), 0)
    lane = lax.broadcasted_iota(jnp.int32, (tm, LANES), 1)
    first_group = lane < POOL_GROUP
    h0 = POOL_HALO

    def win(col, lo, hi):
        acc = buf_ref[h0 + lo:h0 + lo + tm, col]
        for j in range(lo + 1, hi):
            acc = acc + buf_ref[h0 + j:h0 + j + tm, col]
        return acc

    parts = []
    for half, (wa, wb) in enumerate(((2, 4), (8, 16))):
        col = slice(half * LANES, (half + 1) * LANES)
        sa = win(col, -wa // 2, wa // 2)
        sb = sa + win(col, -wb // 2, -wa // 2) + win(col, wa // 2, wb // 2)
        hw = jnp.where(first_group, wa // 2, wb // 2)
        cnt = jnp.minimum(t + hw, l) - jnp.maximum(t - hw, 0)
        mean = jnp.where(first_group, sa, sb) / cnt.astype(F32)
        parts.append(mean - buf_ref[h0:h0 + tm, col])
    d = jnp.concatenate(parts, axis=-1).astype(BF16)
    o_ref[0] = (_dot(d, w_ref[...]) * sc_ref[...]).astype(o_ref.dtype)


def pool_mix(p, pool_w, pool_scale):
    b, l, wd = p.shape
    tm = min(512, l)
    nblk = l // tm
    wblk = jax.scipy.linalg.block_diag(*[pool_w[g] for g in range(len(POOL_WINDOWS))]).astype(BF16)
    return pl.pallas_call(
        functools.partial(_pool_kernel, tm=tm, l=l),
        grid=(b, nblk),
        in_specs=_halo_specs(tm, wd, nblk) + [pl.BlockSpec((wd, wd), lambda bi, i: (0, 0)),
                                              pl.BlockSpec((1, wd), lambda bi, i: (0, 0))],
        out_specs=pl.BlockSpec((1, tm, wd), lambda bi, i: (bi, i, 0)),
        out_shape=jax.ShapeDtypeStruct((b, l, wd), BF16),
        scratch_shapes=[pltpu.VMEM((tm + 2 * POOL_HALO, wd), F32)],
        compiler_params=_cp("parallel", "parallel"),
        name="pool_mix",
    )(p, p, p, wblk, pool_scale.reshape(1, wd))


def _outproj_kernel(x_ref, hy_ref, pool_ref, att_ref, w_ref, gate_ref, o_ref):
    a, b2 = HY_WIDTH, HY_WIDTH + POOL_WIDTH
    y = (_dot(hy_ref[0], w_ref[0:a]) + _dot(pool_ref[0], w_ref[a:b2]) + _dot(att_ref[0], w_ref[b2:]))
    o_ref[0] = x_ref[0] + gate_ref[0] * y


def out_projection(x, y_hy, y_pool, y_att, w_out, gate):
    b, l, d = x.shape
    tm = min(512, l)
    row = lambda bi, i: (bi, i, 0)
    return pl.pallas_call(
        _outproj_kernel,
        grid=(b, l // tm),
        in_specs=[pl.BlockSpec((1, tm, d), row),
                  pl.BlockSpec((1, tm, y_hy.shape[2]), row),
                  pl.BlockSpec((1, tm, y_pool.shape[2]), row),
                  pl.BlockSpec((1, tm, y_att.shape[2]), row),
                  pl.BlockSpec(w_out.shape, lambda bi, i: (0, 0)),
                  pl.BlockSpec((1, 1, d), lambda bi, i: (bi, 0, 0))],
        out_specs=pl.BlockSpec((1, tm, d), row),
        out_shape=jax.ShapeDtypeStruct(x.shape, F32),
        compiler_params=_cp("parallel", "parallel"),
        name="out_projection",
    )(x, y_hy, y_pool, y_att, w_out, gate)


FF_CHUNK = 256


def _swiglu_acc(h, wg_ref, wu_ref, wd_ref, lead):
    ff = wg_ref.shape[-1]
    acc = None
    for f0 in range(0, ff, FF_CHUNK):
        cs = slice(f0, f0 + FF_CHUNK)
        g = _dot(h, wg_ref[lead + (slice(None), cs)])
        u = _dot(h, wu_ref[lead + (slice(None), cs)])
        a = (g * jax.nn.sigmoid(g) * u).astype(BF16)
        part = _dot(a, wd_ref[lead + (cs, slice(None))])
        acc = part if acc is None else acc + part
    return acc


def _ffn_kernel(x_ref, g_ref, sh_ref, sc_ref, gate_ref, wg_ref, wu_ref, wd_ref, o_ref):
    x = x_ref[0]
    h = _normmod(x, g_ref[...], sh_ref[0], sc_ref[0]).astype(BF16)
    o_ref[0] = x + gate_ref[0] * _swiglu_acc(h, wg_ref, wu_ref, wd_ref, ())


def ffn_dense(x, g, shift, scale, gate, wg, wu, wd):
    b, l, d = x.shape
    tm = min(512, l)
    row = lambda bi, i: (bi, i, 0)
    vec = lambda bi, i: (bi, 0, 0)
    wspec = lambda w: pl.BlockSpec(w.shape, lambda bi, i: (0, 0), pipeline_mode=pl.Buffered(1))
    return pl.pallas_call(
        _ffn_kernel,
        grid=(b, l // tm),
        in_specs=[pl.BlockSpec((1, tm, d), row),
                  pl.BlockSpec((1, d), lambda bi, i: (0, 0)),
                  pl.BlockSpec((1, 1, d), vec), pl.BlockSpec((1, 1, d), vec), pl.BlockSpec((1, 1, d), vec),
                  wspec(wg), wspec(wu), wspec(wd)],
        out_specs=pl.BlockSpec((1, tm, d), row),
        out_shape=jax.ShapeDtypeStruct(x.shape, F32),
        compiler_params=_cp("parallel", "parallel"),
        name="ffn_dense",
    )(x, g, shift, scale, gate, wg, wu, wd)


def _router_kernel(x_ref, g_ref, sh_ref, sc_ref, wr_ref, comb_ref):
    h = _normmod(x_ref[0], g_ref[...], sh_ref[0], sc_ref[0])
    logits = _dot(h, wr_ref[...], precision=HIGHEST)
    lane = lax.broadcasted_iota(jnp.int32, logits.shape, 1)
    neg = jnp.float32(-jnp.inf)
    lg = jnp.where(lane < N_EXPERTS, logits, neg)
    m1 = jnp.max(lg, axis=-1, keepdims=True)
    i1 = jnp.min(jnp.where(lg == m1, lane, LANES), axis=-1, keepdims=True)
    lg2 = jnp.where(lane == i1, neg, lg)
    m2 = jnp.max(lg2, axis=-1, keepdims=True)
    i2 = jnp.min(jnp.where(lg2 == m2, lane, LANES), axis=-1, keepdims=True)
    e = jnp.exp(m2 - m1)
    g1 = 1.0 / (1.0 + e)
    comb_ref[0] = jnp.where(lane == i1, g1, jnp.where(lane == i2, e * g1, 0.0))


def moe_route(x, g, shift, scale, w_router):
    b, l, d = x.shape
    tm = min(512, l)
    wr = jnp.concatenate([w_router, jnp.zeros((d, LANES - N_EXPERTS), F32)], -1)
    row = lambda bi, i: (bi, i, 0)
    vec = lambda bi, i: (bi, 0, 0)
    return pl.pallas_call(
        _router_kernel,
        grid=(b, l // tm),
        in_specs=[pl.BlockSpec((1, tm, d), row),
                  pl.BlockSpec((1, d), lambda bi, i: (0, 0)),
                  pl.BlockSpec((1, 1, d), vec), pl.BlockSpec((1, 1, d), vec),
                  pl.BlockSpec((d, LANES), lambda bi, i: (0, 0))],
        out_specs=pl.BlockSpec((1, tm, LANES), row),
        out_shape=jax.ShapeDtypeStruct((b, l, LANES), F32),
        compiler_params=_cp("parallel", "parallel"),
        name="moe_router",
    )(x, g, shift, scale, wr)


def _moe_kernel(x_ref, g_ref, sh_ref, sc_ref, gate_ref, comb_ref, wg_ref, wu_ref, wd_ref, o_ref, acc_ref):
    e = pl.program_id(2)
    x = x_ref[0]
    h = _normmod(x, g_ref[...], sh_ref[0], sc_ref[0]).astype(BF16)
    comb = comb_ref[0]
    lane = lax.broadcasted_iota(jnp.int32, comb.shape, 1)
    ce = jnp.sum(jnp.where(lane == e, comb, 0.0), axis=-1, keepdims=True)
    y = ce * _swiglu_acc(h, wg_ref, wu_ref, wd_ref, (0,))

    @pl.when(e == 0)
    def _():
        acc_ref[...] = y

    @pl.when(e > 0)
    def _():
        acc_ref[...] += y

    @pl.when(e == N_EXPERTS - 1)
    def _():
        o_ref[0] = x + gate_ref[0] * acc_ref[...]


def moe_dense(x, g, shift, scale, gate, comb, wg, wu, wd):
    b, l, d = x.shape
    tm = min(512, l)
    ff = wg.shape[-1]
    row = lambda bi, i, e: (bi, i, 0)
    vec = lambda bi, i, e: (bi, 0, 0)
    return pl.pallas_call(
        _moe_kernel,
        grid=(b, l // tm, N_EXPERTS),
        in_specs=[pl.BlockSpec((1, tm, d), row),
                  pl.BlockSpec((1, d), lambda bi, i, e: (0, 0)),
                  pl.BlockSpec((1, 1, d), vec), pl.BlockSpec((1, 1, d), vec), pl.BlockSpec((1, 1, d), vec),
                  pl.BlockSpec((1, tm, LANES), row),
                  pl.BlockSpec((1, d, ff), lambda bi, i, e: (e, 0, 0)),
                  pl.BlockSpec((1, d, ff), lambda bi, i, e: (e, 0, 0)),
                  pl.BlockSpec((1, ff, d), lambda bi, i, e: (e, 0, 0))],
        out_specs=pl.BlockSpec((1, tm, d), row),
        out_shape=jax.ShapeDtypeStruct(x.shape, F32),
        scratch_shapes=[pltpu.VMEM((tm, d), F32)],
        compiler_params=_cp("parallel", "parallel", "arbitrary"),
        name="moe_dense",
    )(x, g, shift, scale, gate, comb, wg, wu, wd)


def _final_kernel(x_ref, g_ref, o_ref):
    x = x_ref[0]
    o_ref[0] = x * lax.rsqrt(jnp.mean(x * x, axis=-1, keepdims=True) + EPS) * g_ref[...]


def final_norm(x, g):
    b, l, d = x.shape
    tm = min(1024, l)
    return pl.pallas_call(
        _final_kernel,
        grid=(b, l // tm),
        in_specs=[pl.BlockSpec((1, tm, d), lambda bi, i: (bi, i, 0)),
                  pl.BlockSpec((1, d), lambda bi, i: (0, 0))],
        out_specs=pl.BlockSpec((1, tm, d), lambda bi, i: (bi, i, 0)),
        out_shape=jax.ShapeDtypeStruct(x.shape, F32),
        compiler_params=_cp("parallel", "parallel"),
        name="final_norm",
    )(x, g.reshape(1, d))


def kernel(x, c, ctx, c_ctx, norm1_g, norm2_g, w_ada, b_ada, w_in, hy_short_w, hy_short_b, filt_w1, filt_b1, filt_freq1, filt_w2, filt_b2, filt_freq2, filt_w3, hy_bias, pool_w, pool_scale, q_norm_g, w_uq, kv_norm_g, w_ukv, w_out, ffn_wg, ffn_wu, ffn_wd, moe_router, moe_wg, moe_wu, moe_wd, final_g):
    b, l, d = x.shape
    lc = ctx.shape[1]
    depth = w_ada.shape[0]

    cc = jnp.zeros((8, d), F32).at[:b].set(c).at[b].set(c_ctx)
    mods = ada_vectors(cc, w_ada, b_ada)

    rope_l = rope_tables(l, True)
    rope_c = rope_tables(lc, False)
    ftab_l, ftab_c = filter_tables(l), filter_tables(lc)
    dtabs = dft_tables(l)

    xl, xc = x, ctx
    for layer in range(depth):
        last = layer == depth - 1
        mod = mods[layer].reshape(8, 6, 1, d)
        mod_l = [mod[:b, j] for j in range(6)]
        mod_c = [jnp.broadcast_to(mod[b:b + 1, j], (b, 1, d)) for j in range(6)]
        g1 = norm1_g[layer].reshape(1, d)
        g2 = norm2_g[layer].reshape(1, d)
        w_in_l = w_in[layer].astype(BF16)
        w_out_l = w_out[layer].astype(BF16)
        mw = mla_weights(q_norm_g[layer], w_uq[layer], kv_norm_g[layer], w_ukv[layer])
        fw = (filt_w1[layer], filt_b1[layer], filt_freq1[layer], filt_w2[layer], filt_b2[layer],
              filt_freq2[layer], filt_w3[layer])
        bias = hy_bias[layer].reshape(1, HY_WIDTH)

        def channel_mix(xs, md):
            i = layer // 2
            if layer % 2 == 0:
                return ffn_dense(xs, g2, md[3], md[4], md[5], ffn_wg[i].astype(BF16),
                                 ffn_wu[i].astype(BF16), ffn_wd[i].astype(BF16))
            comb = moe_route(xs, g2, md[3], md[4], moe_router[i])
            return moe_dense(xs, g2, md[3], md[4], md[5], comb, moe_wg[i].astype(BF16),
                             moe_wu[i].astype(BF16), moe_wd[i].astype(BF16))

        hy_l, pool_l, cq_l, ckv_l, kpe_l = in_projection(xl, g1, mod_l[0], mod_l[1], w_in_l)
        hy_c, pool_c, cq_c, ckv_c, kpe_c = in_projection(xc, g1, mod_c[0], mod_c[1], w_in_l)
        q_l, kt_l, v_l = mla_project(cq_l, ckv_l, kpe_l, *rope_l, mw)
        q_c, kt_c, v_c = mla_project(cq_c, ckv_c, kpe_c, *rope_c, mw)

        att_l = attention(q_l, kt_c, v_c, kt_l, v_l)
        z_l, x0_l = hyena_pre(hy_l, hy_short_w[layer], hy_short_b[layer])
        filt_l, nrm_l = hyena_filter(l, ftab_l, *fw)
        yhy_l = hyena_long_conv(z_l, x0_l, filt_l, nrm_l, bias, dtabs)
        ypool_l = pool_mix(pool_l, pool_w[layer], pool_scale[layer])
        xl = out_projection(xl, yhy_l, ypool_l, att_l, w_out_l, mod_l[2])

        if not last:
            att_c = attention(q_c, kt_c, v_c)
            z_c, x0_c = hyena_pre(hy_c, hy_short_w[layer], hy_short_b[layer])
            filt_c, nrm_c = hyena_filter(lc, ftab_c, *fw)
            yhy_c = small_long_conv(z_c, x0_c, filt_c, nrm_c, bias)
            ypool_c = pool_mix(pool_c, pool_w[layer], pool_scale[layer])
            xc = out_projection(xc, yhy_c, ypool_c, att_c, w_out_l, mod_c[2])
            xc = channel_mix(xc, mod_c)

        xl = channel_mix(xl, mod_l)
    return final_norm(xl, final_g)
```

```python
import functools
import math

import jax
import jax.numpy as jnp
import numpy as np
from jax import lax
from jax.experimental import pallas as pl
from jax.experimental.pallas import tpu as pltpu

F32 = jnp.float32
BF16 = jnp.bfloat16
HIGHEST = lax.Precision.HIGHEST

EPS = 1e-6
GRID_W = 64
HY_WIDTH = 256
POOL_WINDOWS = (2, 4, 8, 16)
POOL_WIDTH = 256
POOL_GROUP = 64
POOL_HALO = 8
MLA_HEADS = 8
QK_NOPE = 64
QK_ROPE = 32
V_DIM = 64
Q_LORA = 384
KV_LORA = 256
HEAD_PAD = 128
SM_SCALE = (QK_NOPE + QK_ROPE) ** -0.5
ROPE_THETA = 10000.0
FILT_EMB = 33
FILT_BANDS = 16
DECAY_TARGET = 1e-2
FAST_DECAY_PCT = 0.3
SLOW_DECAY_PCT = 1.5
N_EXPERTS = 8
LANES = 128
FFT_N2 = 128
VMEM_LIMIT = 56 * 2 ** 20


def _cp(*sem):
    return pltpu.CompilerParams(dimension_semantics=sem, vmem_limit_bytes=VMEM_LIMIT)


def _dot(a, b, **kw):
    return jnp.dot(a, b, preferred_element_type=F32, **kw)


def _normmod(x, g, shift, scale):
    ms = jnp.mean(x * x, axis=-1, keepdims=True)
    return x * lax.rsqrt(ms + EPS) * (g * (1.0 + scale)) + shift


def _ada_kernel(c_ref, w_ref, b_ref, o_ref):
    c = c_ref[...]
    s = c * jax.nn.sigmoid(c)
    o_ref[0] = _dot(s, w_ref[0], precision=HIGHEST) + b_ref[0]


def ada_vectors(cc, w_ada, b_ada):
    depth, d, n = w_ada.shape
    tn = 1536
    return pl.pallas_call(
        _ada_kernel,
        grid=(depth, n // tn),
        in_specs=[pl.BlockSpec((8, d), lambda l, j: (0, 0)),
                  pl.BlockSpec((1, d, tn), lambda l, j: (l, 0, j)),
                  pl.BlockSpec((1, 1, tn), lambda l, j: (l, 0, j))],
        out_specs=pl.BlockSpec((1, 8, tn), lambda l, j: (l, 0, j)),
        out_shape=jax.ShapeDtypeStruct((depth, 8, n), F32),
        compiler_params=_cp("parallel", "parallel"),
        name="ada_vectors",
    )(cc, w_ada, b_ada.reshape(depth, 1, n))


IN_HY, IN_POOL, IN_CQ, IN_CKV, IN_KPE = 768, 256, 384, 256, 32
IN_OFFS = (0, 768, 1024, 1408, 1664, 1696)


def _inproj_kernel(x_ref, g_ref, sh_ref, sc_ref, w_ref, hy_ref, pool_ref, cq_ref, ckv_ref, kpe_ref):
    h = _normmod(x_ref[0], g_ref[...], sh_ref[0], sc_ref[0]).astype(BF16)
    y = _dot(h, w_ref[...])
    o = IN_OFFS
    hy_ref[0] = y[:, o[0]:o[1]]
    pool_ref[0] = y[:, o[1]:o[2]]
    cq_ref[0] = y[:, o[2]:o[3]]
    ckv_ref[0] = y[:, o[3]:o[4]]
    kpe_ref[0] = y[:, o[4]:o[5]]


def in_projection(x, g, shift, scale, w_in):
    b, l, d = x.shape
    tm = min(512, l)
    widths = (IN_HY, IN_POOL, IN_CQ, IN_CKV, IN_KPE)
    row = lambda bi, i: (bi, i, 0)
    vec = lambda bi, i: (bi, 0, 0)
    return pl.pallas_call(
        _inproj_kernel,
        grid=(b, l // tm),
        in_specs=[pl.BlockSpec((1, tm, d), row),
                  pl.BlockSpec((1, d), lambda bi, i: (0, 0)),
                  pl.BlockSpec((1, 1, d), vec),
                  pl.BlockSpec((1, 1, d), vec),
                  pl.BlockSpec(w_in.shape, lambda bi, i: (0, 0))],
        out_specs=[pl.BlockSpec((1, tm, w), row) for w in widths],
        out_shape=[jax.ShapeDtypeStruct((b, l, w), F32) for w in widths],
        compiler_params=_cp("parallel", "parallel"),
        name="in_projection",
    )(x, g, shift, scale, w_in)


def _mla_kernel(cq_ref, ckv_ref, kpe_ref, c_ref, s_ref, wqa_ref, wqb_ref, wk_ref, wv_ref,
                pa_ref, pb_ref, q_ref, kt_ref, v_ref):
    cq = cq_ref[0]
    cqn = (cq * lax.rsqrt(jnp.mean(cq * cq, axis=-1, keepdims=True) + EPS)).astype(BF16)
    ckv = ckv_ref[0]
    ckvn = (ckv * lax.rsqrt(jnp.mean(ckv * ckv, axis=-1, keepdims=True) + EPS)).astype(BF16)
    kpe = kpe_ref[0].astype(BF16)
    cs, sn = c_ref[...], s_ref[...]
    qa = _dot(cqn, wqa_ref[...])
    qb = _dot(cqn, wqb_ref[...])
    kn = _dot(ckvn, wk_ref[...])
    vv = _dot(ckvn, wv_ref[...])
    kpa = _dot(kpe, pa_ref[...])
    kpb = _dot(kpe, pb_ref[...])
    kpe_rot = kpa * cs + kpb * sn
    lane = lax.broadcasted_iota(jnp.int32, (1, HEAD_PAD), 1)
    for h in range(MLA_HEADS):
        sl = slice(h * HEAD_PAD, (h + 1) * HEAD_PAD)
        q_ref[0, h] = (qa[:, sl] * cs + qb[:, sl] * sn).astype(BF16)
        kt_ref[0, h] = (kn[:, sl] + kpe_rot).T.astype(BF16)
        one_lane = V_DIM if h % 2 == 0 else 0
        v_ref[0, h] = jnp.where(lane == one_lane, 1.0, vv[:, sl]).astype(BF16)


def mla_project(cq, ckv, kpe, ctab, stab, wts):
    b, l, _ = cq.shape
    tm = min(512, l)
    row = lambda bi, i: (bi, i, 0)
    full = lambda a: pl.BlockSpec(a.shape, lambda bi, i: (0,) * a.ndim)
    hw = MLA_HEADS
    return pl.pallas_call(
        _mla_kernel,
        grid=(b, l // tm),
        in_specs=[pl.BlockSpec((1, tm, Q_LORA), row),
                  pl.BlockSpec((1, tm, KV_LORA), row),
                  pl.BlockSpec((1, tm, QK_ROPE), row),
                  pl.BlockSpec((tm, HEAD_PAD), lambda bi, i: (i, 0)),
                  pl.BlockSpec((tm, HEAD_PAD), lambda bi, i: (i, 0))] + [full(w) for w in wts],
        out_specs=[pl.BlockSpec((1, hw, tm, HEAD_PAD), lambda bi, i: (bi, 0, i, 0)),
                   pl.BlockSpec((1, hw, HEAD_PAD, tm), lambda bi, i: (bi, 0, 0, i)),
                   pl.BlockSpec((1, hw, tm, HEAD_PAD), lambda bi, i: (bi, 0, i, 0))],
        out_shape=[jax.ShapeDtypeStruct((b, hw, l, HEAD_PAD), BF16),
                   jax.ShapeDtypeStruct((b, hw, HEAD_PAD, l), BF16),
                   jax.ShapeDtypeStruct((b, hw, l, HEAD_PAD), BF16)],
        compiler_params=_cp("parallel", "parallel"),
        name="mla_project",
    )(cq, ckv, kpe, ctab, stab, *wts)


def mla_weights(q_norm_g, w_uq, kv_norm_g, w_ukv):
    hw, dq = MLA_HEADS, QK_NOPE + QK_ROPE
    half = QK_ROPE // 2
    wq = (w_uq * q_norm_g[:, None] * (SM_SCALE * math.log2(math.e))).reshape(Q_LORA, hw, dq)
    pad = jnp.zeros((Q_LORA, hw, HEAD_PAD - dq), F32)
    wqa = jnp.concatenate([wq, pad], -1)
    swap = jnp.concatenate([wq[..., QK_NOPE + half:], wq[..., QK_NOPE:QK_NOPE + half]], -1)
    wqb = jnp.concatenate([jnp.zeros((Q_LORA, hw, QK_NOPE), F32), swap, pad], -1)
    wkv = (w_ukv * kv_norm_g[:, None]).reshape(KV_LORA, hw, QK_NOPE + V_DIM)
    z64 = jnp.zeros((KV_LORA, hw, 64), F32)
    wk = jnp.concatenate([wkv[..., :QK_NOPE], z64], -1)
    v = wkv[..., QK_NOPE:]
    wv = jnp.where((jnp.arange(hw) % 2 == 0)[None, :, None],
                   jnp.concatenate([v, z64], -1), jnp.concatenate([z64, v], -1))
    eye = np.eye(QK_ROPE, dtype=np.float32)
    pa = np.zeros((QK_ROPE, HEAD_PAD), np.float32)
    pa[:, QK_NOPE:QK_NOPE + QK_ROPE] = eye
    pb = np.zeros((QK_ROPE, HEAD_PAD), np.float32)
    pb[:, QK_NOPE:QK_NOPE + QK_ROPE] = np.concatenate([eye[:, half:], eye[:, :half]], 1)
    flat = lambda w: w.reshape(w.shape[0], hw * HEAD_PAD).astype(BF16)
    return (flat(wqa), flat(wqb), flat(wk), flat(wv), jnp.asarray(pa, BF16), jnp.asarray(pb, BF16))


def rope_tables(l, with_rope):
    ctab = np.zeros((l, HEAD_PAD), np.float32)
    stab = np.zeros((l, HEAD_PAD), np.float32)
    ctab[:, :QK_NOPE + QK_ROPE] = 1.0
    if not with_rope:
        return jnp.asarray(ctab), jnp.asarray(stab)
    n_freq = QK_ROPE // 4
    inv_freq = 1.0 / (ROPE_THETA ** (jnp.arange(n_freq, dtype=F32) / n_freq))
    rows = l // GRID_W
    row = jnp.repeat(jnp.arange(rows, dtype=F32), GRID_W)
    col = jnp.tile(jnp.arange(GRID_W, dtype=F32), rows)
    ang = jnp.concatenate([row[:, None] * inv_freq, col[:, None] * inv_freq], axis=-1)
    cos, sin = jnp.cos(ang), jnp.sin(ang)
    one = jnp.ones((l, QK_NOPE), F32)
    zero = jnp.zeros((l, QK_NOPE), F32)
    tail = jnp.zeros((l, HEAD_PAD - QK_NOPE - QK_ROPE), F32)
    return (jnp.concatenate([one, cos, cos, tail], -1), jnp.concatenate([zero, -sin, sin, tail], -1))


ATT_TQ = 512
ATT_RB = 256
ATT_CK = 2048


def _row_max(s):
    mp = s[:, 0:LANES]
    for j in range(1, s.shape[1] // LANES):
        mp = jnp.maximum(mp, s[:, j * LANES:(j + 1) * LANES])
    return jnp.max(mp, axis=-1, keepdims=True)


def _attn_kernel(q_ref, kct_ref, vc_ref, *rest, n_chunks, ck, tq):
    if n_chunks:
        kt_ref, v_ref, o_ref = rest
    else:
        (o_ref,) = rest
    rb = min(ATT_RB, tq)
    chains = [(hh, r) for hh in range(2) for r in range(tq // rb)]
    lane = lax.broadcasted_iota(jnp.int32, (1, HEAD_PAD), 1)

    def q_rows(hh, r):
        return q_ref[0, hh, r * rb:(r + 1) * rb, :]

    state = []
    for hh, r in chains:
        s = _dot(q_rows(hh, r), kct_ref[0, hh])
        m = _row_max(s)
        p = jnp.exp2(s - m).astype(BF16)
        state += [m, _dot(p, vc_ref[0, hh])]

    if n_chunks:
        def body(c, state):
            off = pl.multiple_of(c * ck, ck)
            out = []
            for i, (hh, r) in enumerate(chains):
                m, acc = state[2 * i], state[2 * i + 1]
                s = _dot(q_rows(hh, r), kt_ref[0, hh, :, pl.ds(off, ck)])
                m_new = jnp.maximum(m, _row_max(s))
                alpha = jnp.exp2(m - m_new)
                p = jnp.exp2(s - m_new).astype(BF16)
                out += [m_new, acc * alpha + _dot(p, v_ref[0, hh, pl.ds(off, ck), :])]
            return tuple(out)
        state = lax.fori_loop(0, n_chunks, body, tuple(state))

    for r in range(tq // rb):
        outs = []
        for hh in range(2):
            acc = state[2 * chains.index((hh, r)) + 1]
            one_lane = V_DIM if hh == 0 else 0
            denom = jnp.sum(jnp.where(lane == one_lane, acc, 0.0), axis=-1, keepdims=True)
            outs.append(acc / denom)
        o_ref[0, r * rb:(r + 1) * rb, :] = jnp.where(lane < V_DIM, outs[0], outs[1]).astype(o_ref.dtype)


def attention(q, kct, vc, kt=None, v=None):
    b, hw, lq, _ = q.shape
    lc = vc.shape[2]
    tq = min(ATT_TQ, lq)
    ck = ATT_CK if kt is None else min(ATT_CK, kt.shape[3])
    n_chunks = 0 if kt is None else kt.shape[3] // ck
    in_specs = [pl.BlockSpec((1, 2, tq, HEAD_PAD), lambda bi, hp, i: (bi, hp, i, 0)),
                pl.BlockSpec((1, 2, HEAD_PAD, lc), lambda bi, hp, i: (bi, hp, 0, 0)),
                pl.BlockSpec((1, 2, lc, HEAD_PAD), lambda bi, hp, i: (bi, hp, 0, 0))]
    args = [q, kct, vc]
    if n_chunks:
        l = kt.shape[3]
        in_specs += [pl.BlockSpec((1, 2, HEAD_PAD, l), lambda bi, hp, i: (bi, hp, 0, 0),
                                  pipeline_mode=pl.Buffered(1)),
                     pl.BlockSpec((1, 2, l, HEAD_PAD), lambda bi, hp, i: (bi, hp, 0, 0),
                                  pipeline_mode=pl.Buffered(1))]
        args += [kt, v]
    return pl.pallas_call(
        functools.partial(_attn_kernel, n_chunks=n_chunks, ck=ck, tq=tq),
        grid=(b, hw // 2, lq // tq),
        in_specs=in_specs,
        out_specs=pl.BlockSpec((1, tq, HEAD_PAD), lambda bi, hp, i: (bi, i, hp)),
        out_shape=jax.ShapeDtypeStruct((b, lq, hw * V_DIM), BF16),
        compiler_params=_cp("parallel", "parallel", "arbitrary"),
        name="attention",
    )(*args)


def _halo_specs(tm, width, nblk):
    r = tm // POOL_HALO
    last = nblk * r - 1
    return [pl.BlockSpec((1, tm, width), lambda bi, i: (bi, i, 0)),
            pl.BlockSpec((1, POOL_HALO, width), lambda bi, i: (bi, jnp.maximum(i * r - 1, 0), 0)),
            pl.BlockSpec((1, POOL_HALO, width), lambda bi, i: (bi, jnp.minimum((i + 1) * r, last), 0))]


def _fill_padded(buf_ref, cur_ref, prev_ref, next_ref, tm):
    i = pl.program_id(1)
    first = i == 0
    last = i == pl.num_programs(1) - 1
    buf_ref[0:POOL_HALO] = jnp.where(first, 0.0, prev_ref[0])
    buf_ref[POOL_HALO:POOL_HALO + tm] = cur_ref[0]
    buf_ref[POOL_HALO + tm:] = jnp.where(last, 0.0, next_ref[0])


def _hyena_pre_kernel(u_ref, up_ref, un_ref, w_ref, b_ref, z_ref, x0_ref, buf_ref, *, tm):
    _fill_padded(buf_ref, u_ref, up_ref, un_ref, tm)
    w = w_ref[...]
    uc = (buf_ref[POOL_HALO - 1:POOL_HALO - 1 + tm] * w[0:1] + buf_ref[POOL_HALO:POOL_HALO + tm] * w[1:2]
          + buf_ref[POOL_HALO + 1:POOL_HALO + 1 + tm] * w[2:3] + b_ref[...])
    x0_ref[0] = uc[:, :HY_WIDTH]
    z_ref[0] = uc[:, HY_WIDTH:2 * HY_WIDTH] * uc[:, 2 * HY_WIDTH:]


def hyena_pre(u, sw, sb):
    b, l, wd = u.shape
    tm = min(512, l)
    nblk = l // tm
    return pl.pallas_call(
        functools.partial(_hyena_pre_kernel, tm=tm),
        grid=(b, nblk),
        in_specs=_halo_specs(tm, wd, nblk) + [pl.BlockSpec((3, wd), lambda bi, i: (0, 0)),
                                              pl.BlockSpec((1, wd), lambda bi, i: (0, 0))],
        out_specs=[pl.BlockSpec((1, tm, HY_WIDTH), lambda bi, i: (bi, i, 0))] * 2,
        out_shape=[jax.ShapeDtypeStruct((b, l, HY_WIDTH), F32)] * 2,
        scratch_shapes=[pltpu.VMEM((tm + 2 * POOL_HALO, wd), F32)],
        compiler_params=_cp("parallel", "parallel"),
        name="hyena_pre",
    )(u, u, u, sw, sb.reshape(1, wd))


def _filter_kernel(feat_ref, tdec_ref, w1_ref, b1_ref, f1_ref, w2_ref, b2_ref, f2_ref, w3_ref, dl_ref,
                   filt_ref, norm_ref, *, l, tr):
    i = pl.program_id(0)
    h = jnp.sin(f1_ref[...] * (_dot(feat_ref[...], w1_ref[...], precision=HIGHEST) + b1_ref[...]))
    h = jnp.sin(f2_ref[...] * (_dot(h, w2_ref[...], precision=HIGHEST) + b2_ref[...]))
    h = _dot(h, w3_ref[...], precision=HIGHEST)
    decay = jnp.exp(-tdec_ref[...] * jnp.abs(dl_ref[...]))
    n = i * tr + lax.broadcasted_iota(jnp.int32, (tr, 1), 0)
    f = jnp.where(n < l, h[:, :HY_WIDTH], h[:, HY_WIDTH:]) * decay
    f = jnp.where(n == l, 0.0, f)
    filt_ref[...] = f

    @pl.when(i == 0)
    def _():
        norm_ref[...] = jnp.zeros_like(norm_ref)
    norm_ref[...] += jnp.sum(jnp.abs(f), axis=0, keepdims=True)


def filter_tables(l):
    n = jnp.arange(2 * l)
    pos = jnp.where(n < l, n, 2 * l - n).astype(F32)[:, None]
    t = pos / max(l - 1, 1)
    w = 2.0 * math.pi * pos / l
    f = jnp.linspace(1e-4, FILT_BANDS - 1, FILT_BANDS, dtype=F32)[None, :]
    feat = jnp.concatenate([t, jnp.cos(f * w), -jnp.sin(f * w),
                            jnp.zeros((2 * l, LANES - FILT_EMB), F32)], axis=-1)
    return feat, t


def hyena_filter(l, tables, w1, b1, f1, w2, b2, f2, w3):
    feat, t = tables
    tr = min(512, 2 * l)
    hid = w2.shape[0]
    max_decay = math.log(DECAY_TARGET) / FAST_DECAY_PCT
    min_decay = math.log(DECAY_TARGET) / SLOW_DECAY_PCT
    deltas = jnp.linspace(min_decay, max_decay, HY_WIDTH, dtype=F32)[None, :]
    w1p = jnp.concatenate([w1, jnp.zeros((LANES - FILT_EMB, hid), F32)], 0)
    full = lambda a: pl.BlockSpec(a.shape, lambda i: (0,) * a.ndim)
    smalls = [w1p, b1.reshape(1, hid), f1.reshape(1, hid), w2, b2.reshape(1, hid), f2.reshape(1, hid), w3, deltas]
    return pl.pallas_call(
        functools.partial(_filter_kernel, l=l, tr=tr),
        grid=(2 * l // tr,),
        in_specs=[pl.BlockSpec((tr, LANES), lambda i: (i, 0)),
                  pl.BlockSpec((tr, 1), lambda i: (i, 0))] + [full(a) for a in smalls],
        out_specs=[pl.BlockSpec((tr, HY_WIDTH), lambda i: (i, 0)),
                   pl.BlockSpec((1, HY_WIDTH), lambda i: (0, 0))],
        out_shape=[jax.ShapeDtypeStruct((2 * l, HY_WIDTH), F32),
                   jax.ShapeDtypeStruct((1, HY_WIDTH), F32)],
        compiler_params=_cp("arbitrary"),
        name="hyena_filter",
    )(feat, t, *smalls)


def _dft_rows_kernel(f_ref, x_ref, o_ref):
    o_ref[0] = _dot(f_ref[...], x_ref[0].astype(BF16)).astype(o_ref.dtype)


def dft_rows(fmat, x, out_dtype, scale_cols=None):
    b, k, w = x.shape
    m = fmat.shape[0]
    tn = min(2048, w)
    return pl.pallas_call(
        _dft_rows_kernel,
        grid=(b, w // tn),
        in_specs=[pl.BlockSpec((m, k), lambda bi, j: (0, 0)),
                  pl.BlockSpec((1, k, tn), lambda bi, j: (bi, 0, j))],
        out_specs=pl.BlockSpec((1, m, tn), lambda bi, j: (bi, 0, j)),
        out_shape=jax.ShapeDtypeStruct((b, m, w), out_dtype),
        compiler_params=_cp("parallel", "parallel"),
        name="dft_rows",
    )(fmat, x)


def _spectrum_kernel(g_ref, a_ref, h_ref):
    n2 = a_ref.shape[3]
    g = g_ref[0]
    p1 = _dot(g, a_ref[0, 0, 0])
    p2 = _dot(g, a_ref[0, 1, 0])
    h_ref[0, 0] = p1[:n2] - p2[n2:]
    h_ref[1, 0] = p2[:n2] + p1[n2:]


def filter_spectrum(gs, a):
    _, _, n1, n2, c = a.shape
    return pl.pallas_call(
        _spectrum_kernel,
        grid=(n1,),
        in_specs=[pl.BlockSpec((1, 2 * n2, n2), lambda k: (k, 0, 0)),
                  pl.BlockSpec((1, 2, 1, n2, c), lambda k: (0, 0, k, 0, 0))],
        out_specs=pl.BlockSpec((2, 1, n2, c), lambda k: (0, k, 0, 0)),
        out_shape=jax.ShapeDtypeStruct((2, n1, n2, c), F32),
        compiler_params=_cp("parallel"),
        name="filter_spectrum",
    )(gs, a)


def _spectral_kernel(g_ref, gt_ref, h_ref, a_ref, b_ref):
    n2 = a_ref.shape[3]
    g = g_ref[0]
    p1 = _dot(g, a_ref[0, 0, 0])
    p2 = _dot(g, a_ref[0, 1, 0])
    xre = p1[:n2] - p2[n2:]
    xim = p2[:n2] + p1[n2:]
    hre, him = h_ref[0, 0], h_ref[1, 0]
    yre = (xre * hre - xim * him).astype(BF16)
    yim = (xre * him + xim * hre).astype(BF16)
    gt = gt_ref[0]
    q1 = _dot(gt, yre)
    q2 = _dot(gt, yim)
    b_ref[0, 0, 0] = (q1[:n2] + q2[n2:]).astype(b_ref.dtype)
    b_ref[0, 1, 0] = (q2[:n2] - q1[n2:]).astype(b_ref.dtype)


def spectral_multiply(gs, gts, hf, a):
    b, _, n1, n2, c = a.shape
    return pl.pallas_call(
        _spectral_kernel,
        grid=(n1, b),
        in_specs=[pl.BlockSpec((1, 2 * n2, n2), lambda k, bi: (k, 0, 0)),
                  pl.BlockSpec((1, 2 * n2, n2), lambda k, bi: (k, 0, 0)),
                  pl.BlockSpec((2, 1, n2, c), lambda k, bi: (0, k, 0, 0)),
                  pl.BlockSpec((1, 2, 1, n2, c), lambda k, bi: (bi, 0, k, 0, 0))],
        out_specs=pl.BlockSpec((1, 2, 1, n2, c), lambda k, bi: (bi, 0, k, 0, 0)),
        out_shape=jax.ShapeDtypeStruct(a.shape, BF16),
        compiler_params=_cp("parallel", "arbitrary"),
        name="spectral_multiply",
    )(gs, gts, hf, a)


def _idft_rows_kernel(f_ref, bm_ref, x0_ref, z_ref, inv_ref, bias_ref, o_ref, *, inv_n):
    y = _dot(f_ref[...], bm_ref[0]) * inv_n
    z = z_ref[0]
    o_ref[0] = (x0_ref[0] * (y * inv_ref[...] + bias_ref[...] * z)).astype(o_ref.dtype)


def idft_rows_gate(fmat, bm, x0, z, inv_norm, bias, n):
    b, k, w = bm.shape
    m = fmat.shape[0]
    tn = min(2048, w)
    col = lambda bi, j: (bi, 0, j)
    return pl.pallas_call(
        functools.partial(_idft_rows_kernel, inv_n=1.0 / n),
        grid=(b, w // tn),
        in_specs=[pl.BlockSpec((m, k), lambda bi, j: (0, 0)),
                  pl.BlockSpec((1, k, tn), col),
                  pl.BlockSpec((1, m, tn), col),
                  pl.BlockSpec((1, m, tn), col),
                  pl.BlockSpec((1, tn), lambda bi, j: (0, j)),
                  pl.BlockSpec((1, tn), lambda bi, j: (0, j))],
        out_specs=pl.BlockSpec((1, m, tn), col),
        out_shape=jax.ShapeDtypeStruct((b, m, w), BF16),
        compiler_params=_cp("parallel", "parallel"),
        name="idft_rows_gate",
    )(fmat, bm, x0, z, inv_norm, bias)


def dft_tables(l):
    n = 2 * l
    n2 = FFT_N2
    n1 = n // n2
    k1 = np.arange(n1)
    ang1 = 2.0 * np.pi * np.outer(k1, k1) / n1
    f1 = np.concatenate([np.cos(ang1), -np.sin(ang1)], 0)
    f3 = np.concatenate([np.cos(ang1), -np.sin(ang1)], 1)[: n1 // 2]
    kk = (jnp.arange(n1, dtype=jnp.int32)[:, None, None] + n1 * jnp.arange(n2, dtype=jnp.int32)[None, :, None])
    prod = (kk * jnp.arange(n2, dtype=jnp.int32)[None, None, :]) % n
    ang = prod.astype(F32) * (2.0 * math.pi / n)
    gre, gim = jnp.cos(ang), -jnp.sin(ang)
    gs = jnp.concatenate([gre, gim], 1).astype(BF16)
    gts = jnp.concatenate([gre.transpose(0, 2, 1), gim.transpose(0, 2, 1)], 1).astype(BF16)
    return dict(n=n, n1=n1, n2=n2, f1=jnp.asarray(f1, BF16), f3=jnp.asarray(f3, BF16), gs=gs, gts=gts)


def hyena_long_conv(z, x0, filt2, norm, bias, tabs):
    b, l, c = z.shape
    n, n1, n2 = tabs["n"], tabs["n1"], tabs["n2"]
    w = n2 * c
    fa = dft_rows(tabs["f1"], filt2.reshape(1, n1, w), BF16)
    hf = filter_spectrum(tabs["gs"], fa.reshape(1, 2, n1, n2, c))
    a = dft_rows(tabs["f1"][:, : n1 // 2], z.reshape(b, n1 // 2, w), BF16)
    bm = spectral_multiply(tabs["gs"], tabs["gts"], hf, a.reshape(b, 2, n1, n2, c))
    tile = lambda v: jnp.tile(v.reshape(1, c), (1, n2))
    y = idft_rows_gate(tabs["f3"], bm.reshape(b, 2 * n1, w), x0.reshape(b, n1 // 2, w),
                       z.reshape(b, n1 // 2, w), tile(1.0 / norm), tile(bias), n)
    return y.reshape(b, l, c)


def _small_conv_kernel(ff_ref, fi_ref, filt_ref, z_ref, x0_ref, inv_ref, bias_ref, o_ref, *, l):
    ff = ff_ref[...]
    hs = _dot(ff, filt_ref[...].astype(BF16))
    hre, him = hs[:2 * l], hs[2 * l:]
    z = z_ref[0]
    xs = _dot(ff[:, :l], z.astype(BF16))
    xre, xim = xs[:2 * l], xs[2 * l:]
    yre = (xre * hre - xim * him).astype(BF16)
    yim = (xre * him + xim * hre).astype(BF16)
    y = _dot(fi_ref[...], jnp.concatenate([yre, yim], 0)) * (0.5 / l)
    o_ref[0] = (x0_ref[0] * (y * inv_ref[...] + bias_ref[...] * z)).astype(o_ref.dtype)


def small_long_conv(z, x0, filt2, norm, bias):
    b, l, c = z.shape
    k = np.arange(2 * l)
    ang = 2.0 * np.pi * np.outer(k, k) / (2 * l)
    ff = jnp.asarray(np.concatenate([np.cos(ang), -np.sin(ang)], 0), BF16)
    fi = jnp.asarray(np.concatenate([np.cos(ang), -np.sin(ang)], 1)[:l], BF16)
    full = lambda a: pl.BlockSpec(a.shape, lambda bi: (0,) * a.ndim)
    row = pl.BlockSpec((1, l, c), lambda bi: (bi, 0, 0))
    inv = 1.0 / norm
    return pl.pallas_call(
        functools.partial(_small_conv_kernel, l=l),
        grid=(b,),
        in_specs=[full(ff), full(fi), full(filt2), row, row, full(inv), full(bias)],
        out_specs=row,
        out_shape=jax.ShapeDtypeStruct((b, l, c), BF16),
        compiler_params=_cp("parallel"),
        name="small_long_conv",
    )(ff, fi, filt2, z, x0, inv, bias)


def _pool_kernel(p_ref, pp_ref, pn_ref, w_ref, sc_ref, o_ref, buf_ref, *, tm, l):
    _fill_padded(buf_ref, p_ref, pp_ref, pn_ref, tm)
    t = pl.program_id(1) * tm + lax.broadcasted_iota(jnp.int32, (tm, LANES), 0)
    lane = lax.broadcasted_iota(jnp.int32, (tm, LANES), 1)
    first_group = lane < POOL_GROUP
    h0 = POOL_HALO

    def win(col, lo, hi):
        acc = buf_ref[h0 + lo:h0 + lo + tm, col]
        for j in range(lo + 1, hi):
            acc = acc + buf_ref[h0 + j:h0 + j + tm, col]
        return acc

    parts = []
    for half, (wa, wb) in enumerate(((2, 4), (8, 16))):
        col = slice(half * LANES, (half + 1) * LANES)
        sa = win(col, -wa // 2, wa // 2)
        sb = sa + win(col, -wb // 2, -wa // 2) + win(col, wa // 2, wb // 2)
        hw = jnp.where(first_group, wa // 2, wb // 2)
        cnt = jnp.minimum(t + hw, l) - jnp.maximum(t - hw, 0)
        mean = jnp.where(first_group, sa, sb) / cnt.astype(F32)
        parts.append(mean - buf_ref[h0:h0 + tm, col])
    d = jnp.concatenate(parts, axis=-1).astype(BF16)
    o_ref[0] = (_dot(d, w_ref[...]) * sc_ref[...]).astype(o_ref.dtype)


def pool_mix(p, pool_w, pool_scale):
    b, l, wd = p.shape
    tm = min(512, l)
    nblk = l // tm
    wblk = jax.scipy.linalg.block_diag(*[pool_w[g] for g in range(len(POOL_WINDOWS))]).astype(BF16)
    return pl.pallas_call(
        functools.partial(_pool_kernel, tm=tm, l=l),
        grid=(b, nblk),
        in_specs=_halo_specs(tm, wd, nblk) + [pl.BlockSpec((wd, wd), lambda bi, i: (0, 0)),
                                              pl.BlockSpec((1, wd), lambda bi, i: (0, 0))],
        out_specs=pl.BlockSpec((1, tm, wd), lambda bi, i: (bi, i, 0)),
        out_shape=jax.ShapeDtypeStruct((b, l, wd), BF16),
        scratch_shapes=[pltpu.VMEM((tm + 2 * POOL_HALO, wd), F32)],
        compiler_params=_cp("parallel", "parallel"),
        name="pool_mix",
    )(p, p, p, wblk, pool_scale.reshape(1, wd))


def _outproj_kernel(x_ref, hy_ref, pool_ref, att_ref, w_ref, gate_ref, o_ref):
    a, b2 = HY_WIDTH, HY_WIDTH + POOL_WIDTH
    y = (_dot(hy_ref[0], w_ref[0:a]) + _dot(pool_ref[0], w_ref[a:b2]) + _dot(att_ref[0], w_ref[b2:]))
    o_ref[0] = x_ref[0] + gate_ref[0] * y


def out_projection(x, y_hy, y_pool, y_att, w_out, gate):
    b, l, d = x.shape
    tm = min(512, l)
    row = lambda bi, i: (bi, i, 0)
    return pl.pallas_call(
        _outproj_kernel,
        grid=(b, l // tm),
        in_specs=[pl.BlockSpec((1, tm, d), row),
                  pl.BlockSpec((1, tm, y_hy.shape[2]), row),
                  pl.BlockSpec((1, tm, y_pool.shape[2]), row),
                  pl.BlockSpec((1, tm, y_att.shape[2]), row),
                  pl.BlockSpec(w_out.shape, lambda bi, i: (0, 0)),
                  pl.BlockSpec((1, 1, d), lambda bi, i: (bi, 0, 0))],
        out_specs=pl.BlockSpec((1, tm, d), row),
        out_shape=jax.ShapeDtypeStruct(x.shape, F32),
        compiler_params=_cp("parallel", "parallel"),
        name="out_projection",
    )(x, y_hy, y_pool, y_att, w_out, gate)


FF_CHUNK = 256


def _swiglu_acc(h, wg_ref, wu_ref, wd_ref, lead):
    ff = wg_ref.shape[-1]
    acc = None
    for f0 in range(0, ff, FF_CHUNK):
        cs = slice(f0, f0 + FF_CHUNK)
        g = _dot(h, wg_ref[lead + (slice(None), cs)])
        u = _dot(h, wu_ref[lead + (slice(None), cs)])
        a = (g * jax.nn.sigmoid(g) * u).astype(BF16)
        part = _dot(a, wd_ref[lead + (cs, slice(None))])
        acc = part if acc is None else acc + part
    return acc


def _ffn_kernel(x_ref, g_ref, sh_ref, sc_ref, gate_ref, wg_ref, wu_ref, wd_ref, o_ref):
    x = x_ref[0]
    h = _normmod(x, g_ref[...], sh_ref[0], sc_ref[0]).astype(BF16)
    o_ref[0] = x + gate_ref[0] * _swiglu_acc(h, wg_ref, wu_ref, wd_ref, ())


def ffn_dense(x, g, shift, scale, gate, wg, wu, wd):
    b, l, d = x.shape
    tm = min(512, l)
    row = lambda bi, i: (bi, i, 0)
    vec = lambda bi, i: (bi, 0, 0)
    wspec = lambda w: pl.BlockSpec(w.shape, lambda bi, i: (0, 0), pipeline_mode=pl.Buffered(1))
    return pl.pallas_call(
        _ffn_kernel,
        grid=(b, l // tm),
        in_specs=[pl.BlockSpec((1, tm, d), row),
                  pl.BlockSpec((1, d), lambda bi, i: (0, 0)),
                  pl.BlockSpec((1, 1, d), vec), pl.BlockSpec((1, 1, d), vec), pl.BlockSpec((1, 1, d), vec),
                  wspec(wg), wspec(wu), wspec(wd)],
        out_specs=pl.BlockSpec((1, tm, d), row),
        out_shape=jax.ShapeDtypeStruct(x.shape, F32),
        compiler_params=_cp("parallel", "parallel"),
        name="ffn_dense",
    )(x, g, shift, scale, gate, wg, wu, wd)


def _router_kernel(x_ref, g_ref, sh_ref, sc_ref, wr_ref, comb_ref):
    h = _normmod(x_ref[0], g_ref[...], sh_ref[0], sc_ref[0])
    logits = _dot(h, wr_ref[...], precision=HIGHEST)
    lane = lax.broadcasted_iota(jnp.int32, logits.shape, 1)
    neg = jnp.float32(-jnp.inf)
    lg = jnp.where(lane < N_EXPERTS, logits, neg)
    m1 = jnp.max(lg, axis=-1, keepdims=True)
    i1 = jnp.min(jnp.where(lg == m1, lane, LANES), axis=-1, keepdims=True)
    lg2 = jnp.where(lane == i1, neg, lg)
    m2 = jnp.max(lg2, axis=-1, keepdims=True)
    i2 = jnp.min(jnp.where(lg2 == m2, lane, LANES), axis=-1, keepdims=True)
    e = jnp.exp(m2 - m1)
    g1 = 1.0 / (1.0 + e)
    comb_ref[0] = jnp.where(lane == i1, g1, jnp.where(lane == i2, e * g1, 0.0))


def moe_route(x, g, shift, scale, w_router):
    b, l, d = x.shape
    tm = min(512, l)
    wr = jnp.concatenate([w_router, jnp.zeros((d, LANES - N_EXPERTS), F32)], -1)
    row = lambda bi, i: (bi, i, 0)
    vec = lambda bi, i: (bi, 0, 0)
    return pl.pallas_call(
        _router_kernel,
        grid=(b, l // tm),
        in_specs=[pl.BlockSpec((1, tm, d), row),
                  pl.BlockSpec((1, d), lambda bi, i: (0, 0)),
                  pl.BlockSpec((1, 1, d), vec), pl.BlockSpec((1, 1, d), vec),
                  pl.BlockSpec((d, LANES), lambda bi, i: (0, 0))],
        out_specs=pl.BlockSpec((1, tm, LANES), row),
        out_shape=jax.ShapeDtypeStruct((b, l, LANES), F32),
        compiler_params=_cp("parallel", "parallel"),
        name="moe_router",
    )(x, g, shift, scale, wr)


def _moe_kernel(x_ref, g_ref, sh_ref, sc_ref, gate_ref, comb_ref, wg_ref, wu_ref, wd_ref, o_ref, acc_ref):
    e = pl.program_id(2)
    x = x_ref[0]
    h = _normmod(x, g_ref[...], sh_ref[0], sc_ref[0]).astype(BF16)
    comb = comb_ref[0]
    lane = lax.broadcasted_iota(jnp.int32, comb.shape, 1)
    ce = jnp.sum(jnp.where(lane == e, comb, 0.0), axis=-1, keepdims=True)
    y = ce * _swiglu_acc(h, wg_ref, wu_ref, wd_ref, (0,))

    @pl.when(e == 0)
    def _():
        acc_ref[...] = y

    @pl.when(e > 0)
    def _():
        acc_ref[...] += y

    @pl.when(e == N_EXPERTS - 1)
    def _():
        o_ref[0] = x + gate_ref[0] * acc_ref[...]


def moe_dense(x, g, shift, scale, gate, comb, wg, wu, wd):
    b, l, d = x.shape
    tm = min(512, l)
    ff = wg.shape[-1]
    row = lambda bi, i, e: (bi, i, 0)
    vec = lambda bi, i, e: (bi, 0, 0)
    return pl.pallas_call(
        _moe_kernel,
        grid=(b, l // tm, N_EXPERTS),
        in_specs=[pl.BlockSpec((1, tm, d), row),
                  pl.BlockSpec((1, d), lambda bi, i, e: (0, 0)),
                  pl.BlockSpec((1, 1, d), vec), pl.BlockSpec((1, 1, d), vec), pl.BlockSpec((1, 1, d), vec),
                  pl.BlockSpec((1, tm, LANES), row),
                  pl.BlockSpec((1, d, ff), lambda bi, i, e: (e, 0, 0)),
                  pl.BlockSpec((1, d, ff), lambda bi, i, e: (e, 0, 0)),
                  pl.BlockSpec((1, ff, d), lambda bi, i, e: (e, 0, 0))],
        out_specs=pl.BlockSpec((1, tm, d), row),
        out_shape=jax.ShapeDtypeStruct(x.shape, F32),
        scratch_shapes=[pltpu.VMEM((tm, d), F32)],
        compiler_params=_cp("parallel", "parallel", "arbitrary"),
        name="moe_dense",
    )(x, g, shift, scale, gate, comb, wg, wu, wd)


def _final_kernel(x_ref, g_ref, o_ref):
    x = x_ref[0]
    o_ref[0] = x * lax.rsqrt(jnp.mean(x * x, axis=-1, keepdims=True) + EPS) * g_ref[...]


def final_norm(x, g):
    b, l, d = x.shape
    tm = min(1024, l)
    return pl.pallas_call(
        _final_kernel,
        grid=(b, l // tm),
        in_specs=[pl.BlockSpec((1, tm, d), lambda bi, i: (bi, i, 0)),
                  pl.BlockSpec((1, d), lambda bi, i: (0, 0))],
        out_specs=pl.BlockSpec((1, tm, d), lambda bi, i: (bi, i, 0)),
        out_shape=jax.ShapeDtypeStruct(x.shape, F32),
        compiler_params=_cp("parallel", "parallel"),
        name="final_norm",
    )(x, g.reshape(1, d))


def kernel(x, c, ctx, c_ctx, norm1_g, norm2_g, w_ada, b_ada, w_in, hy_short_w, hy_short_b, filt_w1, filt_b1, filt_freq1, filt_w2, filt_b2, filt_freq2, filt_w3, hy_bias, pool_w, pool_scale, q_norm_g, w_uq, kv_norm_g, w_ukv, w_out, ffn_wg, ffn_wu, ffn_wd, moe_router, moe_wg, moe_wu, moe_wd, final_g):
    b, l, d = x.shape
    lc = ctx.shape[1]
    depth = w_ada.shape[0]

    cc = jnp.zeros((8, d), F32).at[:b].set(c).at[b].set(c_ctx)
    mods = ada_vectors(cc, w_ada, b_ada)

    rope_l = rope_tables(l, True)
    rope_c = rope_tables(lc, False)
    ftab_l, ftab_c = filter_tables(l), filter_tables(lc)
    dtabs = dft_tables(l)

    xl, xc = x, ctx
    for layer in range(depth):
        last = layer == depth - 1
        mod = mods[layer].reshape(8, 6, 1, d)
        mod_l = [mod[:b, j] for j in range(6)]
        mod_c = [jnp.broadcast_to(mod[b:b + 1, j], (b, 1, d)) for j in range(6)]
        g1 = norm1_g[layer].reshape(1, d)
        g2 = norm2_g[layer].reshape(1, d)
        w_in_l = w_in[layer].astype(BF16)
        w_out_l = w_out[layer].astype(BF16)
        mw = mla_weights(q_norm_g[layer], w_uq[layer], kv_norm_g[layer], w_ukv[layer])
        fw = (filt_w1[layer], filt_b1[layer], filt_freq1[layer], filt_w2[layer], filt_b2[layer],
              filt_freq2[layer], filt_w3[layer])
        bias = hy_bias[layer].reshape(1, HY_WIDTH)

        def channel_mix(xs, md):
            i = layer // 2
            if layer % 2 == 0:
                return ffn_dense(xs, g2, md[3], md[4], md[5], ffn_wg[i].astype(BF16),
                                 ffn_wu[i].astype(BF16), ffn_wd[i].astype(BF16))
            comb = moe_route(xs, g2, md[3], md[4], moe_router[i])
            return moe_dense(xs, g2, md[3], md[4], md[5], comb, moe_wg[i].astype(BF16),
                             moe_wu[i].astype(BF16), moe_wd[i].astype(BF16))

        hy_l, pool_l, cq_l, ckv_l, kpe_l = in_projection(xl, g1, mod_l[0], mod_l[1], w_in_l)
        hy_c, pool_c, cq_c, ckv_c, kpe_c = in_projection(xc, g1, mod_c[0], mod_c[1], w_in_l)
        q_l, kt_l, v_l = mla_project(cq_l, ckv_l, kpe_l, *rope_l, mw)
        q_c, kt_c, v_c = mla_project(cq_c, ckv_c, kpe_c, *rope_c, mw)

        att_l = attention(q_l, kt_c, v_c, kt_l, v_l)
        z_l, x0_l = hyena_pre(hy_l, hy_short_w[layer], hy_short_b[layer])
        filt_l, nrm_l = hyena_filter(l, ftab_l, *fw)
        yhy_l = hyena_long_conv(z_l, x0_l, filt_l, nrm_l, bias, dtabs)
        ypool_l = pool_mix(pool_l, pool_w[layer], pool_scale[layer])
        xl = out_projection(xl, yhy_l, ypool_l, att_l, w_out_l, mod_l[2])

        if not last:
            att_c = attention(q_c, kt_c, v_c)
            z_c, x0_c = hyena_pre(hy_c, hy_short_w[layer], hy_short_b[layer])
            filt_c, nrm_c = hyena_filter(lc, ftab_c, *fw)
            yhy_c = small_long_conv(z_c, x0_c, filt_c, nrm_c, bias)
            ypool_c = pool_mix(pool_c, pool_w[layer], pool_scale[layer])
            xc = out_projection(xc, yhy_c, ypool_c, att_c, w_out_l, mod_c[2])
            xc = channel_mix(xc, mod_c)

        xl = channel_mix(xl, mod_l)
    return final_norm(xl, final_g)
```

```python
import functools
import math

import jax
import jax.numpy as jnp
import numpy as np
from jax import lax
from jax.experimental import pallas as pl
from jax.experimental.pallas import tpu as pltpu

F32 = jnp.float32
BF16 = jnp.bfloat16
HIGHEST = lax.Precision.HIGHEST

EPS = 1e-6
GRID_W = 64
HY_WIDTH = 256
POOL_WINDOWS = (2, 4, 8, 16)
POOL_WIDTH = 256
POOL_GROUP = 64
POOL_HALO = 8
MLA_HEADS = 8
QK_NOPE = 64
QK_ROPE = 32
V_DIM = 64
Q_LORA = 384
KV_LORA = 256
HEAD_PAD = 128
SM_SCALE = (QK_NOPE + QK_ROPE) ** -0.5
ROPE_THETA = 10000.0
FILT_EMB = 33
FILT_BANDS = 16
DECAY_TARGET = 1e-2
FAST_DECAY_PCT = 0.3
SLOW_DECAY_PCT = 1.5
N_EXPERTS = 8
LANES = 128
FFT_N2 = 128
VMEM_LIMIT = 56 * 2 ** 20


def _cp(*sem):
    return pltpu.CompilerParams(dimension_semantics=sem, vmem_limit_bytes=VMEM_LIMIT)


def _dot(a, b, **kw):
    return jnp.dot(a, b, preferred_element_type=F32, **kw)


def _normmod(x, g, shift, scale):
    ms = jnp.mean(x * x, axis=-1, keepdims=True)
    return x * lax.rsqrt(ms + EPS) * (g * (1.0 + scale)) + shift


def _ada_kernel(c_ref, w_ref, b_ref, o_ref):
    c = c_ref[...]
    s = c * jax.nn.sigmoid(c)
    o_ref[0] = _dot(s, w_ref[0], precision=HIGHEST) + b_ref[0]


def ada_vectors(cc, w_ada, b_ada):
    depth, d, n = w_ada.shape
    tn = 1536
    return pl.pallas_call(
        _ada_kernel,
        grid=(depth, n // tn),
        in_specs=[pl.BlockSpec((8, d), lambda l, j: (0, 0)),
                  pl.BlockSpec((1, d, tn), lambda l, j: (l, 0, j)),
                  pl.BlockSpec((1, 1, tn), lambda l, j: (l, 0, j))],
        out_specs=pl.BlockSpec((1, 8, tn), lambda l, j: (l, 0, j)),
        out_shape=jax.ShapeDtypeStruct((depth, 8, n), F32),
        compiler_params=_cp("parallel", "parallel"),
        name="ada_vectors",
    )(cc, w_ada, b_ada.reshape(depth, 1, n))


IN_HY, IN_POOL, IN_CQ, IN_CKV, IN_KPE = 768, 256, 384, 256, 32
IN_OFFS = (0, 768, 1024, 1408, 1664, 1696)


def _inproj_kernel(x_ref, g_ref, sh_ref, sc_ref, w_ref, c_ref, s_ref, wqa_ref, wqb_ref, wk_ref, wv_ref,
                   pa_ref, pb_ref, hy_ref, pool_ref, q_ref, kt_ref, v_ref):
    h = _normmod(x_ref[0], g_ref[...], sh_ref[0], sc_ref[0]).astype(BF16)
    y = _dot(h, w_ref[...])
    o = IN_OFFS
    hy_ref[0] = y[:, o[0]:o[1]]
    pool_ref[0] = y[:, o[1]:o[2]]
    cq, ckv = y[:, o[2]:o[3]], y[:, o[3]:o[4]]
    kpe = y[:, o[4]:o[5]].astype(BF16)
    cqn = (cq * lax.rsqrt(jnp.mean(cq * cq, axis=-1, keepdims=True) + EPS)).astype(BF16)
    ckvn = (ckv * lax.rsqrt(jnp.mean(ckv * ckv, axis=-1, keepdims=True) + EPS)).astype(BF16)
    cs, sn = c_ref[...], s_ref[...]
    qa = _dot(cqn, wqa_ref[...])
    qb = _dot(cqn, wqb_ref[...])
    kn = _dot(ckvn, wk_ref[...])
    vv = _dot(ckvn, wv_ref[...])
    kpa = _dot(kpe, pa_ref[...])
    kpb = _dot(kpe, pb_ref[...])
    kpe_rot = kpa * cs + kpb * sn
    lane = lax.broadcasted_iota(jnp.int32, (1, HEAD_PAD), 1)
    for hd in range(MLA_HEADS):
        sl = slice(hd * HEAD_PAD, (hd + 1) * HEAD_PAD)
        q_ref[0, hd] = (qa[:, sl] * cs + qb[:, sl] * sn).astype(BF16)
        kt_ref[0, hd] = (kn[:, sl] + kpe_rot).T.astype(BF16)
        one_lane = V_DIM if hd % 2 == 0 else 0
        v_ref[0, hd] = jnp.where(lane == one_lane, 1.0, vv[:, sl]).astype(BF16)


def in_projection(x, g, shift, scale, w_in, ctab, stab, wts):
    b, l, d = x.shape
    tm = min(512, l)
    hw = MLA_HEADS
    row = lambda bi, i: (bi, i, 0)
    vec = lambda bi, i: (bi, 0, 0)
    full = lambda a: pl.BlockSpec(a.shape, lambda bi, i: (0,) * a.ndim)
    return pl.pallas_call(
        _inproj_kernel,
        grid=(b, l // tm),
        in_specs=[pl.BlockSpec((1, tm, d), row),
                  pl.BlockSpec((1, d), lambda bi, i: (0, 0)),
                  pl.BlockSpec((1, 1, d), vec),
                  pl.BlockSpec((1, 1, d), vec),
                  full(w_in),
                  pl.BlockSpec((tm, HEAD_PAD), lambda bi, i: (i, 0)),
                  pl.BlockSpec((tm, HEAD_PAD), lambda bi, i: (i, 0))] + [full(w) for w in wts],
        out_specs=[pl.BlockSpec((1, tm, IN_HY), row),
                   pl.BlockSpec((1, tm, IN_POOL), row),
                   pl.BlockSpec((1, hw, tm, HEAD_PAD), lambda bi, i: (bi, 0, i, 0)),
                   pl.BlockSpec((1, hw, HEAD_PAD, tm), lambda bi, i: (bi, 0, 0, i)),
                   pl.BlockSpec((1, hw, tm, HEAD_PAD), lambda bi, i: (bi, 0, i, 0))],
        out_shape=[jax.ShapeDtypeStruct((b, l, IN_HY), F32),
                   jax.ShapeDtypeStruct((b, l, IN_POOL), F32),
                   jax.ShapeDtypeStruct((b, hw, l, HEAD_PAD), BF16),
                   jax.ShapeDtypeStruct((b, hw, HEAD_PAD, l), BF16),
                   jax.ShapeDtypeStruct((b, hw, l, HEAD_PAD), BF16)],
        compiler_params=_cp("parallel", "parallel"),
        name="in_projection",
    )(x, g, shift, scale, w_in, ctab, stab, *wts)


def mla_weights(q_norm_g, w_uq, kv_norm_g, w_ukv):
    hw, dq = MLA_HEADS, QK_NOPE + QK_ROPE
    half = QK_ROPE // 2
    wq = (w_uq * q_norm_g[:, None] * (SM_SCALE * math.log2(math.e))).reshape(Q_LORA, hw, dq)
    pad = jnp.zeros((Q_LORA, hw, HEAD_PAD - dq), F32)
    wqa = jnp.concatenate([wq, pad], -1)
    swap = jnp.concatenate([wq[..., QK_NOPE + half:], wq[..., QK_NOPE:QK_NOPE + half]], -1)
    wqb = jnp.concatenate([jnp.zeros((Q_LORA, hw, QK_NOPE), F32), swap, pad], -1)
    wkv = (w_ukv * kv_norm_g[:, None]).reshape(KV_LORA, hw, QK_NOPE + V_DIM)
    z64 = jnp.zeros((KV_LORA, hw, 64), F32)
    wk = jnp.concatenate([wkv[..., :QK_NOPE], z64], -1)
    v = wkv[..., QK_NOPE:]
    wv = jnp.where((jnp.arange(hw) % 2 == 0)[None, :, None],
                   jnp.concatenate([v, z64], -1), jnp.concatenate([z64, v], -1))
    eye = np.eye(QK_ROPE, dtype=np.float32)
    pa = np.zeros((QK_ROPE, HEAD_PAD), np.float32)
    pa[:, QK_NOPE:QK_NOPE + QK_ROPE] = eye
    pb = np.zeros((QK_ROPE, HEAD_PAD), np.float32)
    pb[:, QK_NOPE:QK_NOPE + QK_ROPE] = np.concatenate([eye[:, half:], eye[:, :half]], 1)
    flat = lambda w: w.reshape(w.shape[0], hw * HEAD_PAD).astype(BF16)
    return (flat(wqa), flat(wqb), flat(wk), flat(wv), jnp.asarray(pa, BF16), jnp.asarray(pb, BF16))


def rope_tables(l, with_rope):
    ctab = np.zeros((l, HEAD_PAD), np.float32)
    stab = np.zeros((l, HEAD_PAD), np.float32)
    ctab[:, :QK_NOPE + QK_ROPE] = 1.0
    if not with_rope:
        return jnp.asarray(ctab), jnp.asarray(stab)
    n_freq = QK_ROPE // 4
    inv_freq = 1.0 / (ROPE_THETA ** (jnp.arange(n_freq, dtype=F32) / n_freq))
    rows = l // GRID_W
    row = jnp.repeat(jnp.arange(rows, dtype=F32), GRID_W)
    col = jnp.tile(jnp.arange(GRID_W, dtype=F32), rows)
    ang = jnp.concatenate([row[:, None] * inv_freq, col[:, None] * inv_freq], axis=-1)
    cos, sin = jnp.cos(ang), jnp.sin(ang)
    one = jnp.ones((l, QK_NOPE), F32)
    zero = jnp.zeros((l, QK_NOPE), F32)
    tail = jnp.zeros((l, HEAD_PAD - QK_NOPE - QK_ROPE), F32)
    return (jnp.concatenate([one, cos, cos, tail], -1), jnp.concatenate([zero, -sin, sin, tail], -1))


ATT_TQ = 1024
ATT_RB = 1024
ATT_CK = 2048


def _row_max(s):
    mp = s[:, 0:LANES]
    for j in range(1, s.shape[1] // LANES):
        mp = jnp.maximum(mp, s[:, j * LANES:(j + 1) * LANES])
    return jnp.max(mp, axis=-1, keepdims=True)


def _attn_kernel(q_ref, kct_ref, vc_ref, *rest, n_chunks, ck, tq):
    if n_chunks:
        kt_ref, v_ref, o_ref = rest
    else:
        (o_ref,) = rest
    rb = min(ATT_RB, tq)
    chains = [(hh, r) for hh in range(2) for r in range(tq // rb)]
    lane = lax.broadcasted_iota(jnp.int32, (1, HEAD_PAD), 1)

    def q_rows(hh, r):
        return q_ref[0, hh, r * rb:(r + 1) * rb, :]

    state = []
    for hh, r in chains:
        s = _dot(q_rows(hh, r), kct_ref[0, hh])
        m = _row_max(s)
        p = jnp.exp2(s - m).astype(BF16)
        state += [m, _dot(p, vc_ref[0, hh])]

    if n_chunks:
        def body(c, state):
            off = pl.multiple_of(c * ck, ck)
            out = []
            for i, (hh, r) in enumerate(chains):
                m, acc = state[2 * i], state[2 * i + 1]
                s = _dot(q_rows(hh, r), kt_ref[0, hh, :, pl.ds(off, ck)])
                m_new = jnp.maximum(m, _row_max(s))
                alpha = jnp.exp2(m - m_new)
                p = jnp.exp2(s - m_new).astype(BF16)
                out += [m_new, acc * alpha + _dot(p, v_ref[0, hh, pl.ds(off, ck), :])]
            return tuple(out)
        state = lax.fori_loop(0, n_chunks, body, tuple(state))

    for r in range(tq // rb):
        outs = []
        for hh in range(2):
            acc = state[2 * chains.index((hh, r)) + 1]
            one_lane = V_DIM if hh == 0 else 0
            denom = jnp.sum(jnp.where(lane == one_lane, acc, 0.0), axis=-1, keepdims=True)
            outs.append(acc / denom)
        o_ref[0, r * rb:(r + 1) * rb, :] = jnp.where(lane < V_DIM, outs[0], outs[1]).astype(o_ref.dtype)


def attention(q, kct, vc, kt=None, v=None):
    b, hw, lq, _ = q.shape
    lc = vc.shape[2]
    tq = min(ATT_TQ, lq)
    ck = ATT_CK if kt is None else min(ATT_CK, kt.shape[3])
    n_chunks = 0 if kt is None else kt.shape[3] // ck
    in_specs = [pl.BlockSpec((1, 2, tq, HEAD_PAD), lambda bi, hp, i: (bi, hp, i, 0)),
                pl.BlockSpec((1, 2, HEAD_PAD, lc), lambda bi, hp, i: (bi, hp, 0, 0)),
                pl.BlockSpec((1, 2, lc, HEAD_PAD), lambda bi, hp, i: (bi, hp, 0, 0))]
    args = [q, kct, vc]
    if n_chunks:
        l = kt.shape[3]
        in_specs += [pl.BlockSpec((1, 2, HEAD_PAD, l), lambda bi, hp, i: (bi, hp, 0, 0),
                                  pipeline_mode=pl.Buffered(1)),
                     pl.BlockSpec((1, 2, l, HEAD_PAD), lambda bi, hp, i: (bi, hp, 0, 0),
                                  pipeline_mode=pl.Buffered(1))]
        args += [kt, v]
    return pl.pallas_call(
        functools.partial(_attn_kernel, n_chunks=n_chunks, ck=ck, tq=tq),
        grid=(b, hw // 2, lq // tq),
        in_specs=in_specs,
        out_specs=pl.BlockSpec((1, tq, HEAD_PAD), lambda bi, hp, i: (bi, i, hp)),
        out_shape=jax.ShapeDtypeStruct((b, lq, hw * V_DIM), BF16),
        compiler_params=_cp("parallel", "parallel", "arbitrary"),
        name="attention",
    )(*args)


def _halo_specs(tm, width, nblk):
    r = tm // POOL_HALO
    last = nblk * r - 1
    return [pl.BlockSpec((1, tm, width), lambda bi, i: (bi, i, 0)),
            pl.BlockSpec((1, POOL_HALO, width), lambda bi, i: (bi, jnp.maximum(i * r - 1, 0), 0)),
            pl.BlockSpec((1, POOL_HALO, width), lambda bi, i: (bi, jnp.minimum((i + 1) * r, last), 0))]


def _fill_padded(buf_ref, cur_ref, prev_ref, next_ref, tm):
    i = pl.program_id(1)
    first = i == 0
    last = i == pl.num_programs(1) - 1
    buf_ref[0:POOL_HALO] = jnp.where(first, 0.0, prev_ref[0])
    buf_ref[POOL_HALO:POOL_HALO + tm] = cur_ref[0]
    buf_ref[POOL_HALO + tm:] = jnp.where(last, 0.0, next_ref[0])


def _hyena_pre_kernel(u_ref, up_ref, un_ref, w_ref, b_ref, z_ref, x0_ref, buf_ref, *, tm):
    _fill_padded(buf_ref, u_ref, up_ref, un_ref, tm)
    w = w_ref[...]
    uc = (buf_ref[POOL_HALO - 1:POOL_HALO - 1 + tm] * w[0:1] + buf_ref[POOL_HALO:POOL_HALO + tm] * w[1:2]
          + buf_ref[POOL_HALO + 1:POOL_HALO + 1 + tm] * w[2:3] + b_ref[...])
    x0_ref[0] = uc[:, :HY_WIDTH]
    z_ref[0] = uc[:, HY_WIDTH:2 * HY_WIDTH] * uc[:, 2 * HY_WIDTH:]


def hyena_pre(u, sw, sb):
    b, l, wd = u.shape
    tm = min(512, l)
    nblk = l // tm
    return pl.pallas_call(
        functools.partial(_hyena_pre_kernel, tm=tm),
        grid=(b, nblk),
        in_specs=_halo_specs(tm, wd, nblk) + [pl.BlockSpec((3, wd), lambda bi, i: (0, 0)),
                                              pl.BlockSpec((1, wd), lambda bi, i: (0, 0))],
        out_specs=[pl.BlockSpec((1, tm, HY_WIDTH), lambda bi, i: (bi, i, 0))] * 2,
        out_shape=[jax.ShapeDtypeStruct((b, l, HY_WIDTH), F32)] * 2,
        scratch_shapes=[pltpu.VMEM((tm + 2 * POOL_HALO, wd), F32)],
        compiler_params=_cp("parallel", "parallel"),
        name="hyena_pre",
    )(u, u, u, sw, sb.reshape(1, wd))


def _filter_kernel(feat_ref, tdec_ref, w1_ref, b1_ref, f1_ref, w2_ref, b2_ref, f2_ref, w3_ref, dl_ref,
                   filt_ref, norm_ref, *, l, tr):
    i = pl.program_id(0)
    h = jnp.sin(f1_ref[...] * (_dot(feat_ref[...], w1_ref[...], precision=HIGHEST) + b1_ref[...]))
    h = jnp.sin(f2_ref[...] * (_dot(h, w2_ref[...], precision=HIGHEST) + b2_ref[...]))
    h = _dot(h, w3_ref[...], precision=HIGHEST)
    decay = jnp.exp(-tdec_ref[...] * jnp.abs(dl_ref[...]))
    n = i * tr + lax.broadcasted_iota(jnp.int32, (tr, 1), 0)
    f = jnp.where(n < l, h[:, :HY_WIDTH], h[:, HY_WIDTH:]) * decay
    f = jnp.where(n == l, 0.0, f)
    filt_ref[...] = f

    @pl.when(i == 0)
    def _():
        norm_ref[...] = jnp.zeros_like(norm_ref)
    norm_ref[...] += jnp.sum(jnp.abs(f), axis=0, keepdims=True)


def filter_tables(l):
    n = jnp.arange(2 * l)
    pos = jnp.where(n < l, n, 2 * l - n).astype(F32)[:, None]
    t = pos / max(l - 1, 1)
    w = 2.0 * math.pi * pos / l
    f = jnp.linspace(1e-4, FILT_BANDS - 1, FILT_BANDS, dtype=F32)[None, :]
    feat = jnp.concatenate([t, jnp.cos(f * w), -jnp.sin(f * w),
                            jnp.zeros((2 * l, LANES - FILT_EMB), F32)], axis=-1)
    return feat, t


def hyena_filter(l, tables, w1, b1, f1, w2, b2, f2, w3):
    feat, t = tables
    tr = min(512, 2 * l)
    hid = w2.shape[0]
    max_decay = math.log(DECAY_TARGET) / FAST_DECAY_PCT
    min_decay = math.log(DECAY_TARGET) / SLOW_DECAY_PCT
    deltas = jnp.linspace(min_decay, max_decay, HY_WIDTH, dtype=F32)[None, :]
    w1p = jnp.concatenate([w1, jnp.zeros((LANES - FILT_EMB, hid), F32)], 0)
    full = lambda a: pl.BlockSpec(a.shape, lambda i: (0,) * a.ndim)
    smalls = [w1p, b1.reshape(1, hid), f1.reshape(1, hid), w2, b2.reshape(1, hid), f2.reshape(1, hid), w3, deltas]
    return pl.pallas_call(
        functools.partial(_filter_kernel, l=l, tr=tr),
        grid=(2 * l // tr,),
        in_specs=[pl.BlockSpec((tr, LANES), lambda i: (i, 0)),
                  pl.BlockSpec((tr, 1), lambda i: (i, 0))] + [full(a) for a in smalls],
        out_specs=[pl.BlockSpec((tr, HY_WIDTH), lambda i: (i, 0)),
                   pl.BlockSpec((1, HY_WIDTH), lambda i: (0, 0))],
        out_shape=[jax.ShapeDtypeStruct((2 * l, HY_WIDTH), F32),
                   jax.ShapeDtypeStruct((1, HY_WIDTH), F32)],
        compiler_params=_cp("arbitrary"),
        name="hyena_filter",
    )(feat, t, *smalls)


def _dft_rows_kernel(f_ref, x_ref, o_ref):
    o_ref[0] = _dot(f_ref[...], x_ref[0].astype(BF16)).astype(o_ref.dtype)


def dft_rows(fmat, x, out_dtype, scale_cols=None):
    b, k, w = x.shape
    m = fmat.shape[0]
    tn = min(2048, w)
    return pl.pallas_call(
        _dft_rows_kernel,
        grid=(b, w // tn),
        in_specs=[pl.BlockSpec((m, k), lambda bi, j: (0, 0)),
                  pl.BlockSpec((1, k, tn), lambda bi, j: (bi, 0, j))],
        out_specs=pl.BlockSpec((1, m, tn), lambda bi, j: (bi, 0, j)),
        out_shape=jax.ShapeDtypeStruct((b, m, w), out_dtype),
        compiler_params=_cp("parallel", "parallel"),
        name="dft_rows",
    )(fmat, x)


SPEC_KB = 8


def _inner_dft(g, are, aim):
    n2 = are.shape[0]
    p1 = _dot(g, are)
    p2 = _dot(g, aim)
    return p1[:n2] - p2[n2:], p2[:n2] + p1[n2:]


def _spectrum_kernel(g_ref, a_ref, h_ref):
    for j in range(g_ref.shape[0]):
        h_ref[0, j], h_ref[1, j] = _inner_dft(g_ref[j], a_ref[0, 0, j], a_ref[0, 1, j])


def filter_spectrum(gs, a):
    _, _, n1, n2, c = a.shape
    kb = min(SPEC_KB, n1)
    return pl.pallas_call(
        _spectrum_kernel,
        grid=(n1 // kb,),
        in_specs=[pl.BlockSpec((kb, 2 * n2, n2), lambda k: (k, 0, 0)),
                  pl.BlockSpec((1, 2, kb, n2, c), lambda k: (0, 0, k, 0, 0))],
        out_specs=pl.BlockSpec((2, kb, n2, c), lambda k: (0, k, 0, 0)),
        out_shape=jax.ShapeDtypeStruct((2, n1, n2, c), F32),
        compiler_params=_cp("parallel"),
        name="filter_spectrum",
    )(gs, a)


def _spectral_kernel(g_ref, gt_ref, h_ref, a_ref, b_ref):
    n2 = a_ref.shape[3]
    for j in range(g_ref.shape[0]):
        xre, xim = _inner_dft(g_ref[j], a_ref[0, 0, j], a_ref[0, 1, j])
        hre, him = h_ref[0, j], h_ref[1, j]
        yre = (xre * hre - xim * him).astype(BF16)
        yim = (xre * him + xim * hre).astype(BF16)
        gt = gt_ref[j]
        q1 = _dot(gt, yre)
        q2 = _dot(gt, yim)
        b_ref[0, 0, j] = (q1[:n2] + q2[n2:]).astype(b_ref.dtype)
        b_ref[0, 1, j] = (q2[:n2] - q1[n2:]).astype(b_ref.dtype)


def spectral_multiply(gs, gts, hf, a):
    b, _, n1, n2, c = a.shape
    kb = min(SPEC_KB, n1)
    return pl.pallas_call(
        _spectral_kernel,
        grid=(n1 // kb, b),
        in_specs=[pl.BlockSpec((kb, 2 * n2, n2), lambda k, bi: (k, 0, 0)),
                  pl.BlockSpec((kb, 2 * n2, n2), lambda k, bi: (k, 0, 0)),
                  pl.BlockSpec((2, kb, n2, c), lambda k, bi: (0, k, 0, 0)),
                  pl.BlockSpec((1, 2, kb, n2, c), lambda k, bi: (bi, 0, k, 0, 0))],
        out_specs=pl.BlockSpec((1, 2, kb, n2, c), lambda k, bi: (bi, 0, k, 0, 0)),
        out_shape=jax.ShapeDtypeStruct(a.shape, BF16),
        compiler_params=_cp("parallel", "arbitrary"),
        name="spectral_multiply",
    )(gs, gts, hf, a)


def _idft_rows_kernel(f_ref, bm_ref, x0_ref, z_ref, inv_ref, bias_ref, o_ref, *, inv_n):
    y = _dot(f_ref[...], bm_ref[0]) * inv_n
    z = z_ref[0]
    o_ref[0] = (x0_ref[0] * (y * inv_ref[...] + bias_ref[...] * z)).astype(o_ref.dtype)


def idft_rows_gate(fmat, bm, x0, z, inv_norm, bias, n):
    b, k, w = bm.shape
    m = fmat.shape[0]
    tn = min(2048, w)
    col = lambda bi, j: (bi, 0, j)
    return pl.pallas_call(
        functools.partial(_idft_rows_kernel, inv_n=1.0 / n),
        grid=(b, w // tn),
        in_specs=[pl.BlockSpec((m, k), lambda bi, j: (0, 0)),
                  pl.BlockSpec((1, k, tn), col),
                  pl.BlockSpec((1, m, tn), col),
                  pl.BlockSpec((1, m, tn), col),
                  pl.BlockSpec((1, tn), lambda bi, j: (0, j)),
                  pl.BlockSpec((1, tn), lambda bi, j: (0, j))],
        out_specs=pl.BlockSpec((1, m, tn), col),
        out_shape=jax.ShapeDtypeStruct((b, m, w), BF16),
        compiler_params=_cp("parallel", "parallel"),
        name="idft_rows_gate",
    )(fmat, bm, x0, z, inv_norm, bias)


def dft_tables(l):
    n = 2 * l
    n2 = FFT_N2
    n1 = n // n2
    k1 = np.arange(n1)
    ang1 = 2.0 * np.pi * np.outer(k1, k1) / n1
    f1 = np.concatenate([np.cos(ang1), -np.sin(ang1)], 0)
    f3 = np.concatenate([np.cos(ang1), -np.sin(ang1)], 1)[: n1 // 2]
    kk = (jnp.arange(n1, dtype=jnp.int32)[:, None, None] + n1 * jnp.arange(n2, dtype=jnp.int32)[None, :, None])
    prod = (kk * jnp.arange(n2, dtype=jnp.int32)[None, None, :]) % n
    ang = prod.astype(F32) * (2.0 * math.pi / n)
    gre, gim = jnp.cos(ang), -jnp.sin(ang)
    gs = jnp.concatenate([gre, gim], 1).astype(BF16)
    gts = jnp.concatenate([gre.transpose(0, 2, 1), gim.transpose(0, 2, 1)], 1).astype(BF16)
    return dict(n=n, n1=n1, n2=n2, f1=jnp.asarray(f1, BF16), f3=jnp.asarray(f3, BF16), gs=gs, gts=gts)


def hyena_long_conv(z, x0, filt2, norm, bias, tabs):
    b, l, c = z.shape
    n, n1, n2 = tabs["n"], tabs["n1"], tabs["n2"]
    w = n2 * c
    fa = dft_rows(tabs["f1"], filt2.reshape(1, n1, w), BF16)
    hf = filter_spectrum(tabs["gs"], fa.reshape(1, 2, n1, n2, c))
    a = dft_rows(tabs["f1"][:, : n1 // 2], z.reshape(b, n1 // 2, w), BF16)
    bm = spectral_multiply(tabs["gs"], tabs["gts"], hf, a.reshape(b, 2, n1, n2, c))
    tile = lambda v: jnp.tile(v.reshape(1, c), (1, n2))
    y = idft_rows_gate(tabs["f3"], bm.reshape(b, 2 * n1, w), x0.reshape(b, n1 // 2, w),
                       z.reshape(b, n1 // 2, w), tile(1.0 / norm), tile(bias), n)
    return y.reshape(b, l, c)


def _small_conv_kernel(ff_ref, fi_ref, filt_ref, z_ref, x0_ref, inv_ref, bias_ref, o_ref, *, l):
    ff = ff_ref[...]
    hs = _dot(ff, filt_ref[...].astype(BF16))
    hre, him = hs[:2 * l], hs[2 * l:]
    z = z_ref[0]
    xs = _dot(ff[:, :l], z.astype(BF16))
    xre, xim = xs[:2 * l], xs[2 * l:]
    yre = (xre * hre - xim * him).astype(BF16)
    yim = (xre * him + xim * hre).astype(BF16)
    y = _dot(fi_ref[...], jnp.concatenate([yre, yim], 0)) * (0.5 / l)
    o_ref[0] = (x0_ref[0] * (y * inv_ref[...] + bias_ref[...] * z)).astype(o_ref.dtype)


def small_long_conv(z, x0, filt2, norm, bias):
    b, l, c = z.shape
    k = np.arange(2 * l)
    ang = 2.0 * np.pi * np.outer(k, k) / (2 * l)
    ff = jnp.asarray(np.concatenate([np.cos(ang), -np.sin(ang)], 0), BF16)
    fi = jnp.asarray(np.concatenate([np.cos(ang), -np.sin(ang)], 1)[:l], BF16)
    full = lambda a: pl.BlockSpec(a.shape, lambda bi: (0,) * a.ndim)
    row = pl.BlockSpec((1, l, c), lambda bi: (bi, 0, 0))
    inv = 1.0 / norm
    return pl.pallas_call(
        functools.partial(_small_conv_kernel, l=l),
        grid=(b,),
        in_specs=[full(ff), full(fi), full(filt2), row, row, full(inv), full(bias)],
        out_specs=row,
        out_shape=jax.ShapeDtypeStruct((b, l, c), BF16),
        compiler_params=_cp("parallel"),
        name="small_long_conv",
    )(ff, fi, filt2, z, x0, inv, bias)


def _pool_kernel(p_ref, pp_ref, pn_ref, w_ref, sc_ref, o_ref, buf_ref, *, tm, l):
    _fill_padded(buf_ref, p_ref, pp_ref, pn_ref, tm)
    t = pl.program_id(1) * tm + lax.broadcasted_iota(jnp.int32, (tm, LANES), 0)
    lane = lax.broadcasted_iota(jnp.int32, (tm, LANES), 1)
    first_group = lane < POOL_GROUP
    h0 = POOL_HALO

    def win(col, lo, hi):
        acc = buf_ref[h0 + lo:h0 + lo + tm, col]
        for j in range(lo + 1, hi):
            acc = acc + buf_ref[h0 + j:h0 + j + tm, col]
        return acc

    parts = []
    for half, (wa, wb) in enumerate(((2, 4), (8, 16))):
        col = slice(half * LANES, (half + 1) * LANES)
        sa = win(col, -wa // 2, wa // 2)
        sb = sa + win(col, -wb // 2, -wa // 2) + win(col, wa // 2, wb // 2)
        hw = jnp.where(first_group, wa // 2, wb // 2)
        cnt = jnp.minimum(t + hw, l) - jnp.maximum(t - hw, 0)
        mean = jnp.where(first_group, sa, sb) / cnt.astype(F32)
        parts.append(mean - buf_ref[h0:h0 + tm, col])
    d = jnp.concatenate(parts, axis=-1).astype(BF16)
    o_ref[0] = (_dot(d, w_ref[...]) * sc_ref[...]).astype(o_ref.dtype)


def pool_mix(p, pool_w, pool_scale):
    b, l, wd = p.shape
    tm = min(512, l)
    nblk = l // tm
    wblk = jax.scipy.linalg.block_diag(*[pool_w[g] for g in range(len(POOL_WINDOWS))]).astype(BF16)
    return pl.pallas_call(
        functools.partial(_pool_kernel, tm=tm, l=l),
        grid=(b, nblk),
        in_specs=_halo_specs(tm, wd, nblk) + [pl.BlockSpec((wd, wd), lambda bi, i: (0, 0)),
                                              pl.BlockSpec((1, wd), lambda bi, i: (0, 0))],
        out_specs=pl.BlockSpec((1, tm, wd), lambda bi, i: (bi, i, 0)),
        out_shape=jax.ShapeDtypeStruct((b, l, wd), BF16),
        scratch_shapes=[pltpu.VMEM((tm + 2 * POOL_HALO, wd), F32)],
        compiler_params=_cp("parallel", "parallel"),
        name="pool_mix",
    )(p, p, p, wblk, pool_scale.reshape(1, wd))


def _outproj_kernel(x_ref, hy_ref, pool_ref, att_ref, w_ref, gate_ref, o_ref):
    a, b2 = HY_WIDTH, HY_WIDTH + POOL_WIDTH
    y = (_dot(hy_ref[0], w_ref[0:a]) + _dot(pool_ref[0], w_ref[a:b2]) + _dot(att_ref[0], w_ref[b2:]))
    o_ref[0] = x_ref[0] + gate_ref[0] * y


def out_projection(x, y_hy, y_pool, y_att, w_out, gate):
    b, l, d = x.shape
    tm = min(512, l)
    row = lambda bi, i: (bi, i, 0)
    return pl.pallas_call(
        _outproj_kernel,
        grid=(b, l // tm),
        in_specs=[pl.BlockSpec((1, tm, d), row),
                  pl.BlockSpec((1, tm, y_hy.shape[2]), row),
                  pl.BlockSpec((1, tm, y_pool.shape[2]), row),
                  pl.BlockSpec((1, tm, y_att.shape[2]), row),
                  pl.BlockSpec(w_out.shape, lambda bi, i: (0, 0)),
                  pl.BlockSpec((1, 1, d), lambda bi, i: (bi, 0, 0))],
        out_specs=pl.BlockSpec((1, tm, d), row),
        out_shape=jax.ShapeDtypeStruct(x.shape, F32),
        compiler_params=_cp("parallel", "parallel"),
        name="out_projection",
    )(x, y_hy, y_pool, y_att, w_out, gate)


FF_CHUNK = 256
MOE_TILE = 1024
MOE_ROWS = 288
MOE_FSPLIT = 2


def _swiglu_acc(h, wg_ref, wu_ref, wd_ref, lead):
    ff = wg_ref.shape[-1]
    acc = None
    for f0 in range(0, ff, FF_CHUNK):
        cs = slice(f0, min(f0 + FF_CHUNK, ff))
        g = _dot(h, wg_ref[lead + (slice(None), cs)])
        u = _dot(h, wu_ref[lead + (slice(None), cs)])
        a = (g * jax.nn.sigmoid(g) * u).astype(BF16)
        part = _dot(a, wd_ref[lead + (cs, slice(None))])
        acc = part if acc is None else acc + part
    return acc


def _ffn_kernel(x_ref, g_ref, sh_ref, sc_ref, gate_ref, wg_ref, wu_ref, wd_ref, o_ref):
    x = x_ref[0]
    h = _normmod(x, g_ref[...], sh_ref[0], sc_ref[0]).astype(BF16)
    o_ref[0] = x + gate_ref[0] * _swiglu_acc(h, wg_ref, wu_ref, wd_ref, ())


def ffn_dense(x, g, shift, scale, gate, wg, wu, wd):
    b, l, d = x.shape
    tm = min(512, l)
    row = lambda bi, i: (bi, i, 0)
    vec = lambda bi, i: (bi, 0, 0)
    wspec = lambda w: pl.BlockSpec(w.shape, lambda bi, i: (0, 0), pipeline_mode=pl.Buffered(1))
    return pl.pallas_call(
        _ffn_kernel,
        grid=(b, l // tm),
        in_specs=[pl.BlockSpec((1, tm, d), row),
                  pl.BlockSpec((1, d), lambda bi, i: (0, 0)),
                  pl.BlockSpec((1, 1, d), vec), pl.BlockSpec((1, 1, d), vec), pl.BlockSpec((1, 1, d), vec),
                  wspec(wg), wspec(wu), wspec(wd)],
        out_specs=pl.BlockSpec((1, tm, d), row),
        out_shape=jax.ShapeDtypeStruct(x.shape, F32),
        compiler_params=_cp("parallel", "parallel"),
        name="ffn_dense",
    )(x, g, shift, scale, gate, wg, wu, wd)


def _router_kernel(x_ref, g_ref, sh_ref, sc_ref, wr_ref, tri_ref, comb_ref, rank_ref):
    h = _normmod(x_ref[0], g_ref[...], sh_ref[0], sc_ref[0])
    logits = _dot(h, wr_ref[...], precision=HIGHEST)
    lane = lax.broadcasted_iota(jnp.int32, logits.shape, 1)
    neg = jnp.float32(-jnp.inf)
    lg = jnp.where(lane < N_EXPERTS, logits, neg)
    m1 = jnp.max(lg, axis=-1, keepdims=True)
    i1 = jnp.min(jnp.where(lg == m1, lane, LANES), axis=-1, keepdims=True)
    lg2 = jnp.where(lane == i1, neg, lg)
    m2 = jnp.max(lg2, axis=-1, keepdims=True)
    i2 = jnp.min(jnp.where(lg2 == m2, lane, LANES), axis=-1, keepdims=True)
    e = jnp.exp(m2 - m1)
    g1 = 1.0 / (1.0 + e)
    comb_ref[0] = jnp.where(lane == i1, g1, jnp.where(lane == i2, e * g1, 0.0))
    routed = (lane == i1) | (lane == i2)
    before = _dot(tri_ref[...], jnp.where(routed, 1.0, 0.0).astype(BF16))
    rank_ref[0] = jnp.where(routed, before, -1.0)


def moe_route(x, g, shift, scale, w_router):
    b, l, d = x.shape
    tm = min(MOE_TILE, l)
    wr = jnp.concatenate([w_router, jnp.zeros((d, LANES - N_EXPERTS), F32)], -1)
    tri = jnp.asarray(np.tril(np.ones((tm, tm), np.float32), -1), BF16)
    row = lambda bi, i: (bi, i, 0)
    vec = lambda bi, i: (bi, 0, 0)
    return pl.pallas_call(
        _router_kernel,
        grid=(b, l // tm),
        in_specs=[pl.BlockSpec((1, tm, d), row),
                  pl.BlockSpec((1, d), lambda bi, i: (0, 0)),
                  pl.BlockSpec((1, 1, d), vec), pl.BlockSpec((1, 1, d), vec),
                  pl.BlockSpec((d, LANES), lambda bi, i: (0, 0)),
                  pl.BlockSpec((tm, tm), lambda bi, i: (0, 0))],
        out_specs=[pl.BlockSpec((1, tm, LANES), row)] * 2,
        out_shape=[jax.ShapeDtypeStruct((b, l, LANES), F32)] * 2,
        compiler_params=_cp("parallel", "parallel"),
        name="moe_router",
    )(x, g, shift, scale, wr, tri)


def _moe_kernel(x_ref, g_ref, sh_ref, sc_ref, gate_ref, comb_ref, wg_ref, wu_ref, wd_ref, o_ref, acc_ref):
    e = pl.program_id(2)
    x = x_ref[0]
    h = _normmod(x, g_ref[...], sh_ref[0], sc_ref[0]).astype(BF16)
    comb = comb_ref[0]
    lane = lax.broadcasted_iota(jnp.int32, comb.shape, 1)
    ce = jnp.sum(jnp.where(lane == e, comb, 0.0), axis=-1, keepdims=True)
    y = ce * _swiglu_acc(h, wg_ref, wu_ref, wd_ref, (0,))

    @pl.when(e == 0)
    def _():
        acc_ref[...] = y

    @pl.when(e > 0)
    def _():
        acc_ref[...] += y

    @pl.when(e == N_EXPERTS - 1)
    def _():
        o_ref[0] = x + gate_ref[0] * acc_ref[...]


def moe_dense(x, g, shift, scale, gate, comb, wg, wu, wd):
    b, l, d = x.shape
    tm = min(512, l)
    ff = wg.shape[-1]
    row = lambda bi, i, e: (bi, i, 0)
    vec = lambda bi, i, e: (bi, 0, 0)
    return pl.pallas_call(
        _moe_kernel,
        grid=(b, l // tm, N_EXPERTS),
        in_specs=[pl.BlockSpec((1, tm, d), row),
                  pl.BlockSpec((1, d), lambda bi, i, e: (0, 0)),
                  pl.BlockSpec((1, 1, d), vec), pl.BlockSpec((1, 1, d), vec), pl.BlockSpec((1, 1, d), vec),
                  pl.BlockSpec((1, tm, LANES), row),
                  pl.BlockSpec((1, d, ff), lambda bi, i, e: (e, 0, 0)),
                  pl.BlockSpec((1, d, ff), lambda bi, i, e: (e, 0, 0)),
                  pl.BlockSpec((1, ff, d), lambda bi, i, e: (e, 0, 0))],
        out_specs=pl.BlockSpec((1, tm, d), row),
        out_shape=jax.ShapeDtypeStruct(x.shape, F32),
        scratch_shapes=[pltpu.VMEM((tm, d), F32)],
        compiler_params=_cp("parallel", "parallel", "arbitrary"),
        name="moe_dense",
    )(x, g, shift, scale, gate, comb, wg, wu, wd)


def _moe_sparse_kernel(cnt_ref, x_ref, g_ref, sh_ref, sc_ref, gate_ref, comb_ref, rank_ref, rankt_ref,
                       wg_ref, wu_ref, wd_ref, o_ref, h_scr, xg_scr, y_scr, *, t, r):
    i, e, f = pl.program_id(0), pl.program_id(1), pl.program_id(2)
    nblk = (cnt_ref[i * N_EXPERTS + e] + r - 1) // r

    @pl.when((e == 0) & (f == 0))
    def _():
        x = x_ref[0]
        h_scr[...] = _normmod(x, g_ref[...], sh_ref[0], sc_ref[0]).astype(BF16)
        o_ref[0] = x

    @pl.when(f == 0)
    def _():
        rrow = rankt_ref[0, pl.ds(e, 1), :]

        def gather(k, carry):
            rid = (k * r + lax.broadcasted_iota(jnp.int32, (r, 1), 0)).astype(F32)
            onehot = jnp.where(rrow == rid, 1.0, 0.0).astype(BF16)
            xg_scr[k] = _dot(onehot, h_scr[...]).astype(BF16)
            return carry
        lax.fori_loop(0, nblk, gather, 0)

    def ffn(k, carry):
        part = _swiglu_acc(xg_scr[k], wg_ref, wu_ref, wd_ref, (0,))

        @pl.when(f == 0)
        def _():
            y_scr[k] = part

        @pl.when(f > 0)
        def _():
            y_scr[k] += part
        return carry
    lax.fori_loop(0, nblk, ffn, 0)

    @pl.when(f == pl.num_programs(2) - 1)
    def _():
        lane = lax.broadcasted_iota(jnp.int32, (t, LANES), 1)
        rcol = jnp.sum(jnp.where(lane == e, rank_ref[0], 0.0), axis=-1, keepdims=True)
        ccol = jnp.sum(jnp.where(lane == e, comb_ref[0], 0.0), axis=-1, keepdims=True)

        def scatter(k, carry):
            cid = (k * r + lax.broadcasted_iota(jnp.int32, (1, r), 1)).astype(F32)
            onehot_t = jnp.where(rcol == cid, 1.0, 0.0).astype(BF16)
            o_ref[0] += (gate_ref[0] * ccol) * _dot(onehot_t, y_scr[k].astype(BF16))
            return carry
        lax.fori_loop(0, nblk, scatter, 0)


def moe_sparse(x, g, shift, scale, gate, comb, rank, wg, wu, wd):
    b, l, d = x.shape
    t, r = MOE_TILE, MOE_ROWS
    tpb = l // t
    n_tiles = b * tpb
    ff = wg.shape[-1]
    fc = ff // MOE_FSPLIT
    max_blk = -(-t // r)
    rk = rank[..., :N_EXPERTS].reshape(n_tiles, t, N_EXPERTS)
    rank_t = rk.transpose(0, 2, 1)
    cnt = jnp.sum(rk >= 0, axis=1).astype(jnp.int32).reshape(n_tiles * N_EXPERTS)
    row = lambda i, e, f, c: (i // tpb, i % tpb, 0)
    vec = lambda i, e, f, c: (i // tpb, 0, 0)
    grid_spec = pltpu.PrefetchScalarGridSpec(
        num_scalar_prefetch=1,
        grid=(n_tiles, N_EXPERTS, MOE_FSPLIT),
        in_specs=[pl.BlockSpec((1, t, d), row, pipeline_mode=pl.Buffered(1)),
                  pl.BlockSpec((1, d), lambda i, e, f, c: (0, 0)),
                  pl.BlockSpec((1, 1, d), vec), pl.BlockSpec((1, 1, d), vec), pl.BlockSpec((1, 1, d), vec),
                  pl.BlockSpec((1, t, LANES), row, pipeline_mode=pl.Buffered(1)),
                  pl.BlockSpec((1, t, LANES), row, pipeline_mode=pl.Buffered(1)),
                  pl.BlockSpec((1, N_EXPERTS, t), lambda i, e, f, c: (i, 0, 0)),
                  pl.BlockSpec((1, d, fc), lambda i, e, f, c: (e, 0, f)),
                  pl.BlockSpec((1, d, fc), lambda i, e, f, c: (e, 0, f)),
                  pl.BlockSpec((1, fc, d), lambda i, e, f, c: (e, f, 0))],
        out_specs=pl.BlockSpec((1, t, d), row),
        scratch_shapes=[pltpu.VMEM((t, d), BF16),
                        pltpu.VMEM((max_blk, r, d), BF16),
                        pltpu.VMEM((max_blk, r, d), F32)],
    )
    return pl.pallas_call(
        functools.partial(_moe_sparse_kernel, t=t, r=r),
        grid_spec=grid_spec,
        out_shape=jax.ShapeDtypeStruct(x.shape, F32),
        compiler_params=_cp("parallel", "arbitrary", "arbitrary"),
        name="moe_sparse",
    )(cnt, x, g, shift, scale, gate, comb, rank, rank_t, wg, wu, wd)


def _final_kernel(x_ref, g_ref, o_ref):
    x = x_ref[0]
    o_ref[0] = x * lax.rsqrt(jnp.mean(x * x, axis=-1, keepdims=True) + EPS) * g_ref[...]


def final_norm(x, g):
    b, l, d = x.shape
    tm = min(1024, l)
    return pl.pallas_call(
        _final_kernel,
        grid=(b, l // tm),
        in_specs=[pl.BlockSpec((1, tm, d), lambda bi, i: (bi, i, 0)),
                  pl.BlockSpec((1, d), lambda bi, i: (0, 0))],
        out_specs=pl.BlockSpec((1, tm, d), lambda bi, i: (bi, i, 0)),
        out_shape=jax.ShapeDtypeStruct(x.shape, F32),
        compiler_params=_cp("parallel", "parallel"),
        name="final_norm",
    )(x, g.reshape(1, d))


def kernel(x, c, ctx, c_ctx, norm1_g, norm2_g, w_ada, b_ada, w_in, hy_short_w, hy_short_b, filt_w1, filt_b1, filt_freq1, filt_w2, filt_b2, filt_freq2, filt_w3, hy_bias, pool_w, pool_scale, q_norm_g, w_uq, kv_norm_g, w_ukv, w_out, ffn_wg, ffn_wu, ffn_wd, moe_router, moe_wg, moe_wu, moe_wd, final_g):
    b, l, d = x.shape
    lc = ctx.shape[1]
    depth = w_ada.shape[0]

    cc = jnp.zeros((8, d), F32).at[:b].set(c).at[b].set(c_ctx)
    mods = ada_vectors(cc, w_ada, b_ada)

    rope_l = rope_tables(l, True)
    rope_c = rope_tables(lc, False)
    ftab_l, ftab_c = filter_tables(l), filter_tables(lc)
    dtabs = dft_tables(l)

    xl, xc = x, ctx
    for layer in range(depth):
        last = layer == depth - 1
        mod = mods[layer].reshape(8, 6, 1, d)
        mod_l = [mod[:b, j] for j in range(6)]
        mod_c = [jnp.broadcast_to(mod[b:b + 1, j], (b, 1, d)) for j in range(6)]
        g1 = norm1_g[layer].reshape(1, d)
        g2 = norm2_g[layer].reshape(1, d)
        w_in_l = w_in[layer].astype(BF16)
        w_out_l = w_out[layer].astype(BF16)
        mw = mla_weights(q_norm_g[layer], w_uq[layer], kv_norm_g[layer], w_ukv[layer])
        fw = (filt_w1[layer], filt_b1[layer], filt_freq1[layer], filt_w2[layer], filt_b2[layer],
              filt_freq2[layer], filt_w3[layer])
        bias = hy_bias[layer].reshape(1, HY_WIDTH)

        def channel_mix(xs, md):
            i = layer // 2
            if layer % 2 == 0:
                return ffn_dense(xs, g2, md[3], md[4], md[5], ffn_wg[i].astype(BF16),
                                 ffn_wu[i].astype(BF16), ffn_wd[i].astype(BF16))
            comb, rank = moe_route(xs, g2, md[3], md[4], moe_router[i])
            ew = (moe_wg[i].astype(BF16), moe_wu[i].astype(BF16), moe_wd[i].astype(BF16))
            if xs.shape[1] % MOE_TILE:
                return moe_dense(xs, g2, md[3], md[4], md[5], comb, *ew)
            return moe_sparse(xs, g2, md[3], md[4], md[5], comb, rank, *ew)

        hy_l, pool_l, q_l, kt_l, v_l = in_projection(xl, g1, mod_l[0], mod_l[1], w_in_l, *rope_l, mw)
        hy_c, pool_c, q_c, kt_c, v_c = in_projection(xc, g1, mod_c[0], mod_c[1], w_in_l, *rope_c, mw)

        att_l = attention(q_l, kt_c, v_c, kt_l, v_l)
        z_l, x0_l = hyena_pre(hy_l, hy_short_w[layer], hy_short_b[layer])
        filt_l, nrm_l = hyena_filter(l, ftab_l, *fw)
        yhy_l = hyena_long_conv(z_l, x0_l, filt_l, nrm_l, bias, dtabs)
        ypool_l = pool_mix(pool_l, pool_w[layer], pool_scale[layer])
        xl = out_projection(xl, yhy_l, ypool_l, att_l, w_out_l, mod_l[2])

        if not last:
            att_c = attention(q_c, kt_c, v_c)
            z_c, x0_c = hyena_pre(hy_c, hy_short_w[layer], hy_short_b[layer])
            filt_c, nrm_c = hyena_filter(lc, ftab_c, *fw)
            yhy_c = small_long_conv(z_c, x0_c, filt_c, nrm_c, bias)
            ypool_c = pool_mix(pool_c, pool_w[layer], pool_scale[layer])
            xc = out_projection(xc, yhy_c, ypool_c, att_c, w_out_l, mod_c[2])
            xc = channel_mix(xc, mod_c)

        xl = channel_mix(xl, mod_l)
    return final_norm(xl, final_g)
```

```python
import functools
import math

import jax
import jax.numpy as jnp
import numpy as np
from jax import lax
from jax.experimental import pallas as pl
from jax.experimental.pallas import tpu as pltpu

F32 = jnp.float32
BF16 = jnp.bfloat16
HIGHEST = lax.Precision.HIGHEST

EPS = 1e-6
GRID_W = 64
HY_WIDTH = 256
POOL_WINDOWS = (2, 4, 8, 16)
POOL_WIDTH = 256
POOL_GROUP = 64
POOL_HALO = 8
MLA_HEADS = 8
QK_NOPE = 64
QK_ROPE = 32
V_DIM = 64
Q_LORA = 384
KV_LORA = 256
HEAD_PAD = 128
SM_SCALE = (QK_NOPE + QK_ROPE) ** -0.5
ROPE_THETA = 10000.0
FILT_EMB = 33
FILT_BANDS = 16
DECAY_TARGET = 1e-2
FAST_DECAY_PCT = 0.3
SLOW_DECAY_PCT = 1.5
N_EXPERTS = 8
LANES = 128
FFT_N2 = 128
VMEM_LIMIT = 56 * 2 ** 20


def _cp(*sem):
    return pltpu.CompilerParams(dimension_semantics=sem, vmem_limit_bytes=VMEM_LIMIT)


def _dot(a, b, **kw):
    return jnp.dot(a, b, preferred_element_type=F32, **kw)


def _normmod(x, g, shift, scale):
    ms = jnp.mean(x * x, axis=-1, keepdims=True)
    return x * lax.rsqrt(ms + EPS) * (g * (1.0 + scale)) + shift


def _ada_kernel(c_ref, w_ref, b_ref, o_ref):
    c = c_ref[...]
    s = c * jax.nn.sigmoid(c)
    o_ref[0] = _dot(s, w_ref[0], precision=HIGHEST) + b_ref[0]


def ada_vectors(cc, w_ada, b_ada):
    depth, d, n = w_ada.shape
    tn = 1536
    return pl.pallas_call(
        _ada_kernel,
        grid=(depth, n // tn),
        in_specs=[pl.BlockSpec((8, d), lambda l, j: (0, 0)),
                  pl.BlockSpec((1, d, tn), lambda l, j: (l, 0, j)),
                  pl.BlockSpec((1, 1, tn), lambda l, j: (l, 0, j))],
        out_specs=pl.BlockSpec((1, 8, tn), lambda l, j: (l, 0, j)),
        out_shape=jax.ShapeDtypeStruct((depth, 8, n), F32),
        compiler_params=_cp("parallel", "parallel"),
        name="ada_vectors",
    )(cc, w_ada, b_ada.reshape(depth, 1, n))


IN_HY, IN_POOL, IN_CQ, IN_CKV, IN_KPE = 768, 256, 384, 256, 32
IN_OFFS = (0, 768, 1024, 1408, 1664, 1696)


def _inproj_kernel(x_ref, g_ref, sh_ref, sc_ref, w_ref, c_ref, s_ref, wqa_ref, wqb_ref, wk_ref, wv_ref,
                   pa_ref, pb_ref, hy_ref, pool_ref, q_ref, kt_ref, v_ref):
    h = _normmod(x_ref[0], g_ref[...], sh_ref[0], sc_ref[0]).astype(BF16)
    y = _dot(h, w_ref[...])
    o = IN_OFFS
    hy_ref[0] = y[:, o[0]:o[1]]
    pool_ref[0] = y[:, o[1]:o[2]]
    cq, ckv = y[:, o[2]:o[3]], y[:, o[3]:o[4]]
    kpe = y[:, o[4]:o[5]].astype(BF16)
    cqn = (cq * lax.rsqrt(jnp.mean(cq * cq, axis=-1, keepdims=True) + EPS)).astype(BF16)
    ckvn = (ckv * lax.rsqrt(jnp.mean(ckv * ckv, axis=-1, keepdims=True) + EPS)).astype(BF16)
    cs, sn = c_ref[...], s_ref[...]
    qa = _dot(cqn, wqa_ref[...])
    qb = _dot(cqn, wqb_ref[...])
    kn = _dot(ckvn, wk_ref[...])
    vv = _dot(ckvn, wv_ref[...])
    kpa = _dot(kpe, pa_ref[...])
    kpb = _dot(kpe, pb_ref[...])
    kpe_rot = kpa * cs + kpb * sn
    lane = lax.broadcasted_iota(jnp.int32, (1, HEAD_PAD), 1)
    for hd in range(MLA_HEADS):
        sl = slice(hd * HEAD_PAD, (hd + 1) * HEAD_PAD)
        q_ref[0, hd] = (qa[:, sl] * cs + qb[:, sl] * sn).astype(BF16)
        kt_ref[0, hd] = (kn[:, sl] + kpe_rot).T.astype(BF16)
        one_lane = V_DIM if hd % 2 == 0 else 0
        v_ref[0, hd] = jnp.where(lane == one_lane, 1.0, vv[:, sl]).astype(BF16)


def in_projection(x, g, shift, scale, w_in, ctab, stab, wts):
    b, l, d = x.shape
    tm = min(512, l)
    hw = MLA_HEADS
    row = lambda bi, i: (bi, i, 0)
    vec = lambda bi, i: (bi, 0, 0)
    full = lambda a: pl.BlockSpec(a.shape, lambda bi, i: (0,) * a.ndim)
    return pl.pallas_call(
        _inproj_kernel,
        grid=(b, l // tm),
        in_specs=[pl.BlockSpec((1, tm, d), row),
                  pl.BlockSpec((1, d), lambda bi, i: (0, 0)),
                  pl.BlockSpec((1, 1, d), vec),
                  pl.BlockSpec((1, 1, d), vec),
                  full(w_in),
                  pl.BlockSpec((tm, HEAD_PAD), lambda bi, i: (i, 0)),
                  pl.BlockSpec((tm, HEAD_PAD), lambda bi, i: (i, 0))] + [full(w) for w in wts],
        out_specs=[pl.BlockSpec((1, tm, IN_HY), row),
                   pl.BlockSpec((1, tm, IN_POOL), row),
                   pl.BlockSpec((1, hw, tm, HEAD_PAD), lambda bi, i: (bi, 0, i, 0)),
                   pl.BlockSpec((1, hw, HEAD_PAD, tm), lambda bi, i: (bi, 0, 0, i)),
                   pl.BlockSpec((1, hw, tm, HEAD_PAD), lambda bi, i: (bi, 0, i, 0))],
        out_shape=[jax.ShapeDtypeStruct((b, l, IN_HY), F32),
                   jax.ShapeDtypeStruct((b, l, IN_POOL), F32),
                   jax.ShapeDtypeStruct((b, hw, l, HEAD_PAD), BF16),
                   jax.ShapeDtypeStruct((b, hw, HEAD_PAD, l), BF16),
                   jax.ShapeDtypeStruct((b, hw, l, HEAD_PAD), BF16)],
        compiler_params=_cp("parallel", "parallel"),
        name="in_projection",
    )(x, g, shift, scale, w_in, ctab, stab, *wts)


def mla_weights(q_norm_g, w_uq, kv_norm_g, w_ukv):
    hw, dq = MLA_HEADS, QK_NOPE + QK_ROPE
    half = QK_ROPE // 2
    wq = (w_uq * q_norm_g[:, None] * (SM_SCALE * math.log2(math.e))).reshape(Q_LORA, hw, dq)
    pad = jnp.zeros((Q_LORA, hw, HEAD_PAD - dq), F32)
    wqa = jnp.concatenate([wq, pad], -1)
    swap = jnp.concatenate([wq[..., QK_NOPE + half:], wq[..., QK_NOPE:QK_NOPE + half]], -1)
    wqb = jnp.concatenate([jnp.zeros((Q_LORA, hw, QK_NOPE), F32), swap, pad], -1)
    wkv = (w_ukv * kv_norm_g[:, None]).reshape(KV_LORA, hw, QK_NOPE + V_DIM)
    z64 = jnp.zeros((KV_LORA, hw, 64), F32)
    wk = jnp.concatenate([wkv[..., :QK_NOPE], z64], -1)
    v = wkv[..., QK_NOPE:]
    wv = jnp.where((jnp.arange(hw) % 2 == 0)[None, :, None],
                   jnp.concatenate([v, z64], -1), jnp.concatenate([z64, v], -1))
    eye = np.eye(QK_ROPE, dtype=np.float32)
    pa = np.zeros((QK_ROPE, HEAD_PAD), np.float32)
    pa[:, QK_NOPE:QK_NOPE + QK_ROPE] = eye
    pb = np.zeros((QK_ROPE, HEAD_PAD), np.float32)
    pb[:, QK_NOPE:QK_NOPE + QK_ROPE] = np.concatenate([eye[:, half:], eye[:, :half]], 1)
    flat = lambda w: w.reshape(w.shape[0], hw * HEAD_PAD).astype(BF16)
    return (flat(wqa), flat(wqb), flat(wk), flat(wv), jnp.asarray(pa, BF16), jnp.asarray(pb, BF16))


def rope_tables(l, with_rope):
    ctab = np.zeros((l, HEAD_PAD), np.float32)
    stab = np.zeros((l, HEAD_PAD), np.float32)
    ctab[:, :QK_NOPE + QK_ROPE] = 1.0
    if not with_rope:
        return jnp.asarray(ctab), jnp.asarray(stab)
    n_freq = QK_ROPE // 4
    inv_freq = 1.0 / (ROPE_THETA ** (jnp.arange(n_freq, dtype=F32) / n_freq))
    rows = l // GRID_W
    row = jnp.repeat(jnp.arange(rows, dtype=F32), GRID_W)
    col = jnp.tile(jnp.arange(GRID_W, dtype=F32), rows)
    ang = jnp.concatenate([row[:, None] * inv_freq, col[:, None] * inv_freq], axis=-1)
    cos, sin = jnp.cos(ang), jnp.sin(ang)
    one = jnp.ones((l, QK_NOPE), F32)
    zero = jnp.zeros((l, QK_NOPE), F32)
    tail = jnp.zeros((l, HEAD_PAD - QK_NOPE - QK_ROPE), F32)
    return (jnp.concatenate([one, cos, cos, tail], -1), jnp.concatenate([zero, -sin, sin, tail], -1))


ATT_TQ = 1024
ATT_RB = 1024
ATT_CK = 2048


def _row_max(s):
    mp = s[:, 0:LANES]
    for j in range(1, s.shape[1] // LANES):
        mp = jnp.maximum(mp, s[:, j * LANES:(j + 1) * LANES])
    return jnp.max(mp, axis=-1, keepdims=True)


def _attn_kernel(q_ref, kct_ref, vc_ref, *rest, n_chunks, ck, tq):
    if n_chunks:
        kt_ref, v_ref, o_ref = rest
    else:
        (o_ref,) = rest
    rb = min(ATT_RB, tq)
    chains = [(hh, r) for hh in range(2) for r in range(tq // rb)]
    lane = lax.broadcasted_iota(jnp.int32, (1, HEAD_PAD), 1)

    def q_rows(hh, r):
        return q_ref[0, hh, r * rb:(r + 1) * rb, :]

    state = []
    for hh, r in chains:
        s = _dot(q_rows(hh, r), kct_ref[0, hh])
        m = _row_max(s)
        p = jnp.exp2(s - m).astype(BF16)
        state += [m, _dot(p, vc_ref[0, hh])]

    if n_chunks:
        def body(c, state):
            off = pl.multiple_of(c * ck, ck)
            out = []
            for i, (hh, r) in enumerate(chains):
                m, acc = state[2 * i], state[2 * i + 1]
                s = _dot(q_rows(hh, r), kt_ref[0, hh, :, pl.ds(off, ck)])
                m_new = jnp.maximum(m, _row_max(s))
                alpha = jnp.exp2(m - m_new)
                p = jnp.exp2(s - m_new).astype(BF16)
                out += [m_new, acc * alpha + _dot(p, v_ref[0, hh, pl.ds(off, ck), :])]
            return tuple(out)
        state = lax.fori_loop(0, n_chunks, body, tuple(state), unroll=2 if n_chunks % 2 == 0 else 1)

    for r in range(tq // rb):
        outs = []
        for hh in range(2):
            acc = state[2 * chains.index((hh, r)) + 1]
            one_lane = V_DIM if hh == 0 else 0
            denom = jnp.sum(jnp.where(lane == one_lane, acc, 0.0), axis=-1, keepdims=True)
            outs.append(acc / denom)
        o_ref[0, r * rb:(r + 1) * rb, :] = jnp.where(lane < V_DIM, outs[0], outs[1]).astype(o_ref.dtype)


def attention(q, kct, vc, kt=None, v=None):
    b, hw, lq, _ = q.shape
    lc = vc.shape[2]
    tq = min(ATT_TQ, lq)
    ck = ATT_CK if kt is None else min(ATT_CK, kt.shape[3])
    n_chunks = 0 if kt is None else kt.shape[3] // ck
    in_specs = [pl.BlockSpec((1, 2, tq, HEAD_PAD), lambda bi, hp, i: (bi, hp, i, 0)),
                pl.BlockSpec((1, 2, HEAD_PAD, lc), lambda bi, hp, i: (bi, hp, 0, 0)),
                pl.BlockSpec((1, 2, lc, HEAD_PAD), lambda bi, hp, i: (bi, hp, 0, 0))]
    args = [q, kct, vc]
    if n_chunks:
        l = kt.shape[3]
        in_specs += [pl.BlockSpec((1, 2, HEAD_PAD, l), lambda bi, hp, i: (bi, hp, 0, 0),
                                  pipeline_mode=pl.Buffered(1)),
                     pl.BlockSpec((1, 2, l, HEAD_PAD), lambda bi, hp, i: (bi, hp, 0, 0),
                                  pipeline_mode=pl.Buffered(1))]
        args += [kt, v]
    return pl.pallas_call(
        functools.partial(_attn_kernel, n_chunks=n_chunks, ck=ck, tq=tq),
        grid=(b, hw // 2, lq // tq),
        in_specs=in_specs,
        out_specs=pl.BlockSpec((1, tq, HEAD_PAD), lambda bi, hp, i: (bi, i, hp)),
        out_shape=jax.ShapeDtypeStruct((b, lq, hw * V_DIM), BF16),
        compiler_params=_cp("parallel", "parallel", "arbitrary"),
        name="attention",
    )(*args)


def _halo_specs(tm, width, nblk):
    r = tm // POOL_HALO
    last = nblk * r - 1
    return [pl.BlockSpec((1, tm, width), lambda bi, i: (bi, i, 0)),
            pl.BlockSpec((1, POOL_HALO, width), lambda bi, i: (bi, jnp.maximum(i * r - 1, 0), 0)),
            pl.BlockSpec((1, POOL_HALO, width), lambda bi, i: (bi, jnp.minimum((i + 1) * r, last), 0))]


def _fill_padded(buf_ref, cur_ref, prev_ref, next_ref, tm):
    i = pl.program_id(1)
    first = i == 0
    last = i == pl.num_programs(1) - 1
    buf_ref[0:POOL_HALO] = jnp.where(first, 0.0, prev_ref[0])
    buf_ref[POOL_HALO:POOL_HALO + tm] = cur_ref[0]
    buf_ref[POOL_HALO + tm:] = jnp.where(last, 0.0, next_ref[0])


def _hyena_pre_kernel(u_ref, up_ref, un_ref, w_ref, b_ref, z_ref, x0_ref, buf_ref, *, tm):
    _fill_padded(buf_ref, u_ref, up_ref, un_ref, tm)
    w = w_ref[...]
    uc = (buf_ref[POOL_HALO - 1:POOL_HALO - 1 + tm] * w[0:1] + buf_ref[POOL_HALO:POOL_HALO + tm] * w[1:2]
          + buf_ref[POOL_HALO + 1:POOL_HALO + 1 + tm] * w[2:3] + b_ref[...])
    x0_ref[0] = uc[:, :HY_WIDTH]
    z_ref[0] = uc[:, HY_WIDTH:2 * HY_WIDTH] * uc[:, 2 * HY_WIDTH:]


def hyena_pre(u, sw, sb):
    b, l, wd = u.shape
    tm = min(512, l)
    nblk = l // tm
    return pl.pallas_call(
        functools.partial(_hyena_pre_kernel, tm=tm),
        grid=(b, nblk),
        in_specs=_halo_specs(tm, wd, nblk) + [pl.BlockSpec((3, wd), lambda bi, i: (0, 0)),
                                              pl.BlockSpec((1, wd), lambda bi, i: (0, 0))],
        out_specs=[pl.BlockSpec((1, tm, HY_WIDTH), lambda bi, i: (bi, i, 0))] * 2,
        out_shape=[jax.ShapeDtypeStruct((b, l, HY_WIDTH), F32)] * 2,
        scratch_shapes=[pltpu.VMEM((tm + 2 * POOL_HALO, wd), F32)],
        compiler_params=_cp("parallel", "parallel"),
        name="hyena_pre",
    )(u, u, u, sw, sb.reshape(1, wd))


def _filter_kernel(feat_ref, tdec_ref, w1_ref, b1_ref, f1_ref, w2_ref, b2_ref, f2_ref, w3_ref, dl_ref,
                   filt_ref, norm_ref, *, l, tr):
    i = pl.program_id(0)
    h = jnp.sin(f1_ref[...] * (_dot(feat_ref[...], w1_ref[...], precision=HIGHEST) + b1_ref[...]))
    h = jnp.sin(f2_ref[...] * (_dot(h, w2_ref[...], precision=HIGHEST) + b2_ref[...]))
    h = _dot(h, w3_ref[...], precision=HIGHEST)
    decay = jnp.exp(-tdec_ref[...] * jnp.abs(dl_ref[...]))
    n = i * tr + lax.broadcasted_iota(jnp.int32, (tr, 1), 0)
    f = jnp.where(n < l, h[:, :HY_WIDTH], h[:, HY_WIDTH:]) * decay
    f = jnp.where(n == l, 0.0, f)
    filt_ref[...] = f

    @pl.when(i == 0)
    def _():
        norm_ref[...] = jnp.zeros_like(norm_ref)
    norm_ref[...] += jnp.sum(jnp.abs(f), axis=0, keepdims=True)


def filter_tables(l):
    n = jnp.arange(2 * l)
    pos = jnp.where(n < l, n, 2 * l - n).astype(F32)[:, None]
    t = pos / max(l - 1, 1)
    w = 2.0 * math.pi * pos / l
    f = jnp.linspace(1e-4, FILT_BANDS - 1, FILT_BANDS, dtype=F32)[None, :]
    feat = jnp.concatenate([t, jnp.cos(f * w), -jnp.sin(f * w),
                            jnp.zeros((2 * l, LANES - FILT_EMB), F32)], axis=-1)
    return feat, t


def hyena_filter(l, tables, w1, b1, f1, w2, b2, f2, w3):
    feat, t = tables
    tr = min(512, 2 * l)
    hid = w2.shape[0]
    max_decay = math.log(DECAY_TARGET) / FAST_DECAY_PCT
    min_decay = math.log(DECAY_TARGET) / SLOW_DECAY_PCT
    deltas = jnp.linspace(min_decay, max_decay, HY_WIDTH, dtype=F32)[None, :]
    w1p = jnp.concatenate([w1, jnp.zeros((LANES - FILT_EMB, hid), F32)], 0)
    full = lambda a: pl.BlockSpec(a.shape, lambda i: (0,) * a.ndim)
    smalls = [w1p, b1.reshape(1, hid), f1.reshape(1, hid), w2, b2.reshape(1, hid), f2.reshape(1, hid), w3, deltas]
    return pl.pallas_call(
        functools.partial(_filter_kernel, l=l, tr=tr),
        grid=(2 * l // tr,),
        in_specs=[pl.BlockSpec((tr, LANES), lambda i: (i, 0)),
                  pl.BlockSpec((tr, 1), lambda i: (i, 0))] + [full(a) for a in smalls],
        out_specs=[pl.BlockSpec((tr, HY_WIDTH), lambda i: (i, 0)),
                   pl.BlockSpec((1, HY_WIDTH), lambda i: (0, 0))],
        out_shape=[jax.ShapeDtypeStruct((2 * l, HY_WIDTH), F32),
                   jax.ShapeDtypeStruct((1, HY_WIDTH), F32)],
        compiler_params=_cp("arbitrary"),
        name="hyena_filter",
    )(feat, t, *smalls)


def _dft_rows_kernel(f_ref, x_ref, o_ref):
    o_ref[0] = _dot(f_ref[...], x_ref[0].astype(BF16)).astype(o_ref.dtype)


def dft_rows(fmat, x, out_dtype, scale_cols=None):
    b, k, w = x.shape
    m = fmat.shape[0]
    tn = min(2048, w)
    return pl.pallas_call(
        _dft_rows_kernel,
        grid=(b, w // tn),
        in_specs=[pl.BlockSpec((m, k), lambda bi, j: (0, 0)),
                  pl.BlockSpec((1, k, tn), lambda bi, j: (bi, 0, j))],
        out_specs=pl.BlockSpec((1, m, tn), lambda bi, j: (bi, 0, j)),
        out_shape=jax.ShapeDtypeStruct((b, m, w), out_dtype),
        compiler_params=_cp("parallel", "parallel"),
        name="dft_rows",
    )(fmat, x)


SPEC_KB = 8


def _inner_dft(g, are, aim):
    n2 = are.shape[0]
    p1 = _dot(g, are)
    p2 = _dot(g, aim)
    return p1[:n2] - p2[n2:], p2[:n2] + p1[n2:]


def _spectrum_kernel(g_ref, a_ref, h_ref):
    for j in range(g_ref.shape[0]):
        h_ref[0, j], h_ref[1, j] = _inner_dft(g_ref[j], a_ref[0, 0, j], a_ref[0, 1, j])


def filter_spectrum(gs, a):
    _, _, n1, n2, c = a.shape
    kb = min(SPEC_KB, n1)
    return pl.pallas_call(
        _spectrum_kernel,
        grid=(n1 // kb,),
        in_specs=[pl.BlockSpec((kb, 2 * n2, n2), lambda k: (k, 0, 0)),
                  pl.BlockSpec((1, 2, kb, n2, c), lambda k: (0, 0, k, 0, 0))],
        out_specs=pl.BlockSpec((2, kb, n2, c), lambda k: (0, k, 0, 0)),
        out_shape=jax.ShapeDtypeStruct((2, n1, n2, c), F32),
        compiler_params=_cp("parallel"),
        name="filter_spectrum",
    )(gs, a)


def _spectral_kernel(g_ref, gt_ref, h_ref, a_ref, b_ref):
    n2 = a_ref.shape[3]
    for j in range(g_ref.shape[0]):
        xre, xim = _inner_dft(g_ref[j], a_ref[0, 0, j], a_ref[0, 1, j])
        hre, him = h_ref[0, j], h_ref[1, j]
        yre = (xre * hre - xim * him).astype(BF16)
        yim = (xre * him + xim * hre).astype(BF16)
        gt = gt_ref[j]
        q1 = _dot(gt, yre)
        q2 = _dot(gt, yim)
        b_ref[0, 0, j] = (q1[:n2] + q2[n2:]).astype(b_ref.dtype)
        b_ref[0, 1, j] = (q2[:n2] - q1[n2:]).astype(b_ref.dtype)


def spectral_multiply(gs, gts, hf, a):
    b, _, n1, n2, c = a.shape
    kb = min(SPEC_KB, n1)
    return pl.pallas_call(
        _spectral_kernel,
        grid=(n1 // kb, b),
        in_specs=[pl.BlockSpec((kb, 2 * n2, n2), lambda k, bi: (k, 0, 0)),
                  pl.BlockSpec((kb, 2 * n2, n2), lambda k, bi: (k, 0, 0)),
                  pl.BlockSpec((2, kb, n2, c), lambda k, bi: (0, k, 0, 0)),
                  pl.BlockSpec((1, 2, kb, n2, c), lambda k, bi: (bi, 0, k, 0, 0))],
        out_specs=pl.BlockSpec((1, 2, kb, n2, c), lambda k, bi: (bi, 0, k, 0, 0)),
        out_shape=jax.ShapeDtypeStruct(a.shape, BF16),
        compiler_params=_cp("parallel", "arbitrary"),
        name="spectral_multiply",
    )(gs, gts, hf, a)


def _idft_rows_kernel(f_ref, bm_ref, x0_ref, z_ref, inv_ref, bias_ref, o_ref, *, inv_n):
    y = _dot(f_ref[...], bm_ref[0]) * inv_n
    z = z_ref[0]
    o_ref[0] = (x0_ref[0] * (y * inv_ref[...] + bias_ref[...] * z)).astype(o_ref.dtype)


def idft_rows_gate(fmat, bm, x0, z, inv_norm, bias, n):
    b, k, w = bm.shape
    m = fmat.shape[0]
    tn = min(2048, w)
    col = lambda bi, j: (bi, 0, j)
    return pl.pallas_call(
        functools.partial(_idft_rows_kernel, inv_n=1.0 / n),
        grid=(b, w // tn),
        in_specs=[pl.BlockSpec((m, k), lambda bi, j: (0, 0)),
                  pl.BlockSpec((1, k, tn), col),
                  pl.BlockSpec((1, m, tn), col),
                  pl.BlockSpec((1, m, tn), col),
                  pl.BlockSpec((1, tn), lambda bi, j: (0, j)),
                  pl.BlockSpec((1, tn), lambda bi, j: (0, j))],
        out_specs=pl.BlockSpec((1, m, tn), col),
        out_shape=jax.ShapeDtypeStruct((b, m, w), BF16),
        compiler_params=_cp("parallel", "parallel"),
        name="idft_rows_gate",
    )(fmat, bm, x0, z, inv_norm, bias)


def dft_tables(l):
    n = 2 * l
    n2 = FFT_N2
    n1 = n // n2
    k1 = np.arange(n1)
    ang1 = 2.0 * np.pi * np.outer(k1, k1) / n1
    f1 = np.concatenate([np.cos(ang1), -np.sin(ang1)], 0)
    f3 = np.concatenate([np.cos(ang1), -np.sin(ang1)], 1)[: n1 // 2]
    kk = (jnp.arange(n1, dtype=jnp.int32)[:, None, None] + n1 * jnp.arange(n2, dtype=jnp.int32)[None, :, None])
    prod = (kk * jnp.arange(n2, dtype=jnp.int32)[None, None, :]) % n
    ang = prod.astype(F32) * (2.0 * math.pi / n)
    gre, gim = jnp.cos(ang), -jnp.sin(ang)
    gs = jnp.concatenate([gre, gim], 1).astype(BF16)
    gts = jnp.concatenate([gre.transpose(0, 2, 1), gim.transpose(0, 2, 1)], 1).astype(BF16)
    return dict(n=n, n1=n1, n2=n2, f1=jnp.asarray(f1, BF16), f3=jnp.asarray(f3, BF16), gs=gs, gts=gts)


def hyena_long_conv(z, x0, filt2, norm, bias, tabs):
    b, l, c = z.shape
    n, n1, n2 = tabs["n"], tabs["n1"], tabs["n2"]
    w = n2 * c
    fa = dft_rows(tabs["f1"], filt2.reshape(1, n1, w), BF16)
    hf = filter_spectrum(tabs["gs"], fa.reshape(1, 2, n1, n2, c))
    a = dft_rows(tabs["f1"][:, : n1 // 2], z.reshape(b, n1 // 2, w), BF16)
    bm = spectral_multiply(tabs["gs"], tabs["gts"], hf, a.reshape(b, 2, n1, n2, c))
    tile = lambda v: jnp.tile(v.reshape(1, c), (1, n2))
    y = idft_rows_gate(tabs["f3"], bm.reshape(b, 2 * n1, w), x0.reshape(b, n1 // 2, w),
                       z.reshape(b, n1 // 2, w), tile(1.0 / norm), tile(bias), n)
    return y.reshape(b, l, c)


def _small_conv_kernel(ff_ref, fi_ref, filt_ref, z_ref, x0_ref, inv_ref, bias_ref, o_ref, *, l):
    ff = ff_ref[...]
    hs = _dot(ff, filt_ref[...].astype(BF16))
    hre, him = hs[:2 * l], hs[2 * l:]
    z = z_ref[0]
    xs = _dot(ff[:, :l], z.astype(BF16))
    xre, xim = xs[:2 * l], xs[2 * l:]
    yre = (xre * hre - xim * him).astype(BF16)
    yim = (xre * him + xim * hre).astype(BF16)
    y = _dot(fi_ref[...], jnp.concatenate([yre, yim], 0)) * (0.5 / l)
    o_ref[0] = (x0_ref[0] * (y * inv_ref[...] + bias_ref[...] * z)).astype(o_ref.dtype)


def small_long_conv(z, x0, filt2, norm, bias):
    b, l, c = z.shape
    k = np.arange(2 * l)
    ang = 2.0 * np.pi * np.outer(k, k) / (2 * l)
    ff = jnp.asarray(np.concatenate([np.cos(ang), -np.sin(ang)], 0), BF16)
    fi = jnp.asarray(np.concatenate([np.cos(ang), -np.sin(ang)], 1)[:l], BF16)
    full = lambda a: pl.BlockSpec(a.shape, lambda bi: (0,) * a.ndim)
    row = pl.BlockSpec((1, l, c), lambda bi: (bi, 0, 0))
    inv = 1.0 / norm
    return pl.pallas_call(
        functools.partial(_small_conv_kernel, l=l),
        grid=(b,),
        in_specs=[full(ff), full(fi), full(filt2), row, row, full(inv), full(bias)],
        out_specs=row,
        out_shape=jax.ShapeDtypeStruct((b, l, c), BF16),
        compiler_params=_cp("parallel"),
        name="small_long_conv",
    )(ff, fi, filt2, z, x0, inv, bias)


def _pool_kernel(p_ref, pp_ref, pn_ref, w_ref, sc_ref, o_ref, buf_ref, *, tm, l):
    _fill_padded(buf_ref, p_ref, pp_ref, pn_ref, tm)
    t = pl.program_id(1) * tm + lax.broadcasted_iota(jnp.int32, (tm, LANES), 0)
    lane = lax.broadcasted_iota(jnp.int32, (tm, LANES), 1)
    first_group = lane < POOL_GROUP
    h0 = POOL_HALO

    def win(col, lo, hi):
        acc = buf_ref[h0 + lo:h0 + lo + tm, col]
        for j in range(lo + 1, hi):
            acc = acc + buf_ref[h0 + j:h0 + j + tm, col]
        return acc

    parts = []
    for half, (wa, wb) in enumerate(((2, 4), (8, 16))):
        col = slice(half * LANES, (half + 1) * LANES)
        sa = win(col, -wa // 2, wa // 2)
        sb = sa + win(col, -wb // 2, -wa // 2) + win(col, wa // 2, wb // 2)
        hw = jnp.where(first_group, wa // 2, wb // 2)
        cnt = jnp.minimum(t + hw, l) - jnp.maximum(t - hw, 0)
        mean = jnp.where(first_group, sa, sb) / cnt.astype(F32)
        parts.append(mean - buf_ref[h0:h0 + tm, col])
    d = jnp.concatenate(parts, axis=-1).astype(BF16)
    o_ref[0] = (_dot(d, w_ref[...]) * sc_ref[...]).astype(o_ref.dtype)


def pool_mix(p, pool_w, pool_scale):
    b, l, wd = p.shape
    tm = min(512, l)
    nblk = l // tm
    wblk = jax.scipy.linalg.block_diag(*[pool_w[g] for g in range(len(POOL_WINDOWS))]).astype(BF16)
    return pl.pallas_call(
        functools.partial(_pool_kernel, tm=tm, l=l),
        grid=(b, nblk),
        in_specs=_halo_specs(tm, wd, nblk) + [pl.BlockSpec((wd, wd), lambda bi, i: (0, 0)),
                                              pl.BlockSpec((1, wd), lambda bi, i: (0, 0))],
        out_specs=pl.BlockSpec((1, tm, wd), lambda bi, i: (bi, i, 0)),
        out_shape=jax.ShapeDtypeStruct((b, l, wd), BF16),
        scratch_shapes=[pltpu.VMEM((tm + 2 * POOL_HALO, wd), F32)],
        compiler_params=_cp("parallel", "parallel"),
        name="pool_mix",
    )(p, p, p, wblk, pool_scale.reshape(1, wd))


def _outproj_kernel(x_ref, hy_ref, pool_ref, att_ref, w_ref, gate_ref, o_ref):
    a, b2 = HY_WIDTH, HY_WIDTH + POOL_WIDTH
    y = (_dot(hy_ref[0], w_ref[0:a]) + _dot(pool_ref[0], w_ref[a:b2]) + _dot(att_ref[0], w_ref[b2:]))
    o_ref[0] = x_ref[0] + gate_ref[0] * y


def out_projection(x, y_hy, y_pool, y_att, w_out, gate):
    b, l, d = x.shape
    tm = min(512, l)
    row = lambda bi, i: (bi, i, 0)
    return pl.pallas_call(
        _outproj_kernel,
        grid=(b, l // tm),
        in_specs=[pl.BlockSpec((1, tm, d), row),
                  pl.BlockSpec((1, tm, y_hy.shape[2]), row),
                  pl.BlockSpec((1, tm, y_pool.shape[2]), row),
                  pl.BlockSpec((1, tm, y_att.shape[2]), row),
                  pl.BlockSpec(w_out.shape, lambda bi, i: (0, 0)),
                  pl.BlockSpec((1, 1, d), lambda bi, i: (bi, 0, 0))],
        out_specs=pl.BlockSpec((1, tm, d), row),
        out_shape=jax.ShapeDtypeStruct(x.shape, F32),
        compiler_params=_cp("parallel", "parallel"),
        name="out_projection",
    )(x, y_hy, y_pool, y_att, w_out, gate)


FF_CHUNK = 256
MOE_TILE = 1024
MOE_ROWS = 288


def _swiglu_acc(h, wg_ref, wu_ref, wd_ref, lead):
    ff = wg_ref.shape[-1]
    acc = None
    for f0 in range(0, ff, FF_CHUNK):
        cs = slice(f0, min(f0 + FF_CHUNK, ff))
        g = _dot(h, wg_ref[lead + (slice(None), cs)])
        u = _dot(h, wu_ref[lead + (slice(None), cs)])
        a = (g * jax.nn.sigmoid(g) * u).astype(BF16)
        part = _dot(a, wd_ref[lead + (cs, slice(None))])
        acc = part if acc is None else acc + part
    return acc


def _ffn_kernel(x_ref, g_ref, sh_ref, sc_ref, gate_ref, wg_ref, wu_ref, wd_ref, o_ref):
    x = x_ref[0]
    h = _normmod(x, g_ref[...], sh_ref[0], sc_ref[0]).astype(BF16)
    o_ref[0] = x + gate_ref[0] * _swiglu_acc(h, wg_ref, wu_ref, wd_ref, ())


def ffn_dense(x, g, shift, scale, gate, wg, wu, wd):
    b, l, d = x.shape
    tm = min(512, l)
    row = lambda bi, i: (bi, i, 0)
    vec = lambda bi, i: (bi, 0, 0)
    wspec = lambda w: pl.BlockSpec(w.shape, lambda bi, i: (0, 0), pipeline_mode=pl.Buffered(1))
    return pl.pallas_call(
        _ffn_kernel,
        grid=(b, l // tm),
        in_specs=[pl.BlockSpec((1, tm, d), row),
                  pl.BlockSpec((1, d), lambda bi, i: (0, 0)),
                  pl.BlockSpec((1, 1, d), vec), pl.BlockSpec((1, 1, d), vec), pl.BlockSpec((1, 1, d), vec),
                  wspec(wg), wspec(wu), wspec(wd)],
        out_specs=pl.BlockSpec((1, tm, d), row),
        out_shape=jax.ShapeDtypeStruct(x.shape, F32),
        compiler_params=_cp("parallel", "parallel"),
        name="ffn_dense",
    )(x, g, shift, scale, gate, wg, wu, wd)


def _router_kernel(x_ref, g_ref, sh_ref, sc_ref, wr_ref, tri_ref, comb_ref, rank_ref):
    h = _normmod(x_ref[0], g_ref[...], sh_ref[0], sc_ref[0])
    logits = _dot(h, wr_ref[...], precision=HIGHEST)
    lane = lax.broadcasted_iota(jnp.int32, logits.shape, 1)
    neg = jnp.float32(-jnp.inf)
    lg = jnp.where(lane < N_EXPERTS, logits, neg)
    m1 = jnp.max(lg, axis=-1, keepdims=True)
    i1 = jnp.min(jnp.where(lg == m1, lane, LANES), axis=-1, keepdims=True)
    lg2 = jnp.where(lane == i1, neg, lg)
    m2 = jnp.max(lg2, axis=-1, keepdims=True)
    i2 = jnp.min(jnp.where(lg2 == m2, lane, LANES), axis=-1, keepdims=True)
    e = jnp.exp(m2 - m1)
    g1 = 1.0 / (1.0 + e)
    comb_ref[0] = jnp.where(lane == i1, g1, jnp.where(lane == i2, e * g1, 0.0))
    routed = (lane == i1) | (lane == i2)
    before = _dot(tri_ref[...], jnp.where(routed, 1.0, 0.0).astype(BF16))
    rank_ref[0] = jnp.where(routed, before, -1.0)


def moe_route(x, g, shift, scale, w_router):
    b, l, d = x.shape
    tm = min(MOE_TILE, l)
    wr = jnp.concatenate([w_router, jnp.zeros((d, LANES - N_EXPERTS), F32)], -1)
    tri = jnp.asarray(np.tril(np.ones((tm, tm), np.float32), -1), BF16)
    row = lambda bi, i: (bi, i, 0)
    vec = lambda bi, i: (bi, 0, 0)
    return pl.pallas_call(
        _router_kernel,
        grid=(b, l // tm),
        in_specs=[pl.BlockSpec((1, tm, d), row),
                  pl.BlockSpec((1, d), lambda bi, i: (0, 0)),
                  pl.BlockSpec((1, 1, d), vec), pl.BlockSpec((1, 1, d), vec),
                  pl.BlockSpec((d, LANES), lambda bi, i: (0, 0)),
                  pl.BlockSpec((tm, tm), lambda bi, i: (0, 0))],
        out_specs=[pl.BlockSpec((1, tm, LANES), row)] * 2,
        out_shape=[jax.ShapeDtypeStruct((b, l, LANES), F32)] * 2,
        compiler_params=_cp("parallel", "parallel"),
        name="moe_router",
    )(x, g, shift, scale, wr, tri)


def _moe_kernel(x_ref, g_ref, sh_ref, sc_ref, gate_ref, comb_ref, wg_ref, wu_ref, wd_ref, o_ref, acc_ref):
    e = pl.program_id(2)
    x = x_ref[0]
    h = _normmod(x, g_ref[...], sh_ref[0], sc_ref[0]).astype(BF16)
    comb = comb_ref[0]
    lane = lax.broadcasted_iota(jnp.int32, comb.shape, 1)
    ce = jnp.sum(jnp.where(lane == e, comb, 0.0), axis=-1, keepdims=True)
    y = ce * _swiglu_acc(h, wg_ref, wu_ref, wd_ref, (0,))

    @pl.when(e == 0)
    def _():
        acc_ref[...] = y

    @pl.when(e > 0)
    def _():
        acc_ref[...] += y

    @pl.when(e == N_EXPERTS - 1)
    def _():
        o_ref[0] = x + gate_ref[0] * acc_ref[...]


def moe_dense(x, g, shift, scale, gate, comb, wg, wu, wd):
    b, l, d = x.shape
    tm = min(512, l)
    ff = wg.shape[-1]
    row = lambda bi, i, e: (bi, i, 0)
    vec = lambda bi, i, e: (bi, 0, 0)
    return pl.pallas_call(
        _moe_kernel,
        grid=(b, l // tm, N_EXPERTS),
        in_specs=[pl.BlockSpec((1, tm, d), row),
                  pl.BlockSpec((1, d), lambda bi, i, e: (0, 0)),
                  pl.BlockSpec((1, 1, d), vec), pl.BlockSpec((1, 1, d), vec), pl.BlockSpec((1, 1, d), vec),
                  pl.BlockSpec((1, tm, LANES), row),
                  pl.BlockSpec((1, d, ff), lambda bi, i, e: (e, 0, 0)),
                  pl.BlockSpec((1, d, ff), lambda bi, i, e: (e, 0, 0)),
                  pl.BlockSpec((1, ff, d), lambda bi, i, e: (e, 0, 0))],
        out_specs=pl.BlockSpec((1, tm, d), row),
        out_shape=jax.ShapeDtypeStruct(x.shape, F32),
        scratch_shapes=[pltpu.VMEM((tm, d), F32)],
        compiler_params=_cp("parallel", "parallel", "arbitrary"),
        name="moe_dense",
    )(x, g, shift, scale, gate, comb, wg, wu, wd)


def _moe_sparse_kernel(cnt_ref, x_ref, g_ref, sh_ref, sc_ref, gate_ref, comb_ref, rank_ref, rankt_ref,
                       wg_ref, wu_ref, wd_ref, o_ref, h_scr, *, t, r):
    i, e = pl.program_id(0), pl.program_id(1)
    nblk = (cnt_ref[i * N_EXPERTS + e] + r - 1) // r

    @pl.when(e == 0)
    def _():
        x = x_ref[0]
        h_scr[...] = _normmod(x, g_ref[...], sh_ref[0], sc_ref[0]).astype(BF16)
        o_ref[0] = x

    rrow = rankt_ref[0, pl.ds(e, 1), :]
    lane = lax.broadcasted_iota(jnp.int32, (t, LANES), 1)
    rcol = jnp.sum(jnp.where(lane == e, rank_ref[0], 0.0), axis=-1, keepdims=True)
    ccol = jnp.sum(jnp.where(lane == e, comb_ref[0], 0.0), axis=-1, keepdims=True)

    def block(k, carry):
        rid = (k * r + lax.broadcasted_iota(jnp.int32, (r, 1), 0)).astype(F32)
        onehot = jnp.where(rrow == rid, 1.0, 0.0).astype(BF16)
        xg = _dot(onehot, h_scr[...]).astype(BF16)
        y = _swiglu_acc(xg, wg_ref, wu_ref, wd_ref, (0,)).astype(BF16)
        cid = (k * r + lax.broadcasted_iota(jnp.int32, (1, r), 1)).astype(F32)
        onehot_t = jnp.where(rcol == cid, 1.0, 0.0).astype(BF16)
        o_ref[0] += (gate_ref[0] * ccol) * _dot(onehot_t, y)
        return carry
    lax.fori_loop(0, nblk, block, 0)


def moe_sparse(x, g, shift, scale, gate, comb, rank, wg, wu, wd):
    b, l, d = x.shape
    t, r = MOE_TILE, MOE_ROWS
    tpb = l // t
    n_tiles = b * tpb
    ff = wg.shape[-1]
    rk = rank[..., :N_EXPERTS].reshape(n_tiles, t, N_EXPERTS)
    rank_t = rk.transpose(0, 2, 1)
    cnt = jnp.sum(rk >= 0, axis=1).astype(jnp.int32).reshape(n_tiles * N_EXPERTS)
    row = lambda i, e, c: (i // tpb, i % tpb, 0)
    vec = lambda i, e, c: (i // tpb, 0, 0)
    grid_spec = pltpu.PrefetchScalarGridSpec(
        num_scalar_prefetch=1,
        grid=(n_tiles, N_EXPERTS),
        in_specs=[pl.BlockSpec((1, t, d), row, pipeline_mode=pl.Buffered(1)),
                  pl.BlockSpec((1, d), lambda i, e, c: (0, 0)),
                  pl.BlockSpec((1, 1, d), vec), pl.BlockSpec((1, 1, d), vec), pl.BlockSpec((1, 1, d), vec),
                  pl.BlockSpec((1, t, LANES), row, pipeline_mode=pl.Buffered(1)),
                  pl.BlockSpec((1, t, LANES), row, pipeline_mode=pl.Buffered(1)),
                  pl.BlockSpec((1, N_EXPERTS, t), lambda i, e, c: (i, 0, 0)),
                  pl.BlockSpec((1, d, ff), lambda i, e, c: (e, 0, 0)),
                  pl.BlockSpec((1, d, ff), lambda i, e, c: (e, 0, 0)),
                  pl.BlockSpec((1, ff, d), lambda i, e, c: (e, 0, 0))],
        out_specs=pl.BlockSpec((1, t, d), row),
        scratch_shapes=[pltpu.VMEM((t, d), BF16)],
    )
    return pl.pallas_call(
        functools.partial(_moe_sparse_kernel, t=t, r=r),
        grid_spec=grid_spec,
        out_shape=jax.ShapeDtypeStruct(x.shape, F32),
        compiler_params=_cp("parallel", "arbitrary"),
        name="moe_sparse",
    )(cnt, x, g, shift, scale, gate, comb, rank, rank_t, wg, wu, wd)


def _final_kernel(x_ref, g_ref, o_ref):
    x = x_ref[0]
    o_ref[0] = x * lax.rsqrt(jnp.mean(x * x, axis=-1, keepdims=True) + EPS) * g_ref[...]


def final_norm(x, g):
    b, l, d = x.shape
    tm = min(1024, l)
    return pl.pallas_call(
        _final_kernel,
        grid=(b, l // tm),
        in_specs=[pl.BlockSpec((1, tm, d), lambda bi, i: (bi, i, 0)),
                  pl.BlockSpec((1, d), lambda bi, i: (0, 0))],
        out_specs=pl.BlockSpec((1, tm, d), lambda bi, i: (bi, i, 0)),
        out_shape=jax.ShapeDtypeStruct(x.shape, F32),
        compiler_params=_cp("parallel", "parallel"),
        name="final_norm",
    )(x, g.reshape(1, d))


def kernel(x, c, ctx, c_ctx, norm1_g, norm2_g, w_ada, b_ada, w_in, hy_short_w, hy_short_b, filt_w1, filt_b1, filt_freq1, filt_w2, filt_b2, filt_freq2, filt_w3, hy_bias, pool_w, pool_scale, q_norm_g, w_uq, kv_norm_g, w_ukv, w_out, ffn_wg, ffn_wu, ffn_wd, moe_router, moe_wg, moe_wu, moe_wd, final_g):
    b, l, d = x.shape
    lc = ctx.shape[1]
    depth = w_ada.shape[0]

    cc = jnp.zeros((8, d), F32).at[:b].set(c).at[b].set(c_ctx)
    mods = ada_vectors(cc, w_ada, b_ada)

    rope_l = rope_tables(l, True)
    rope_c = rope_tables(lc, False)
    ftab_l, ftab_c = filter_tables(l), filter_tables(lc)
    dtabs = dft_tables(l)

    xl, xc = x, ctx
    for layer in range(depth):
        last = layer == depth - 1
        mod = mods[layer].reshape(8, 6, 1, d)
        mod_l = [mod[:b, j] for j in range(6)]
        mod_c = [jnp.broadcast_to(mod[b:b + 1, j], (b, 1, d)) for j in range(6)]
        g1 = norm1_g[layer].reshape(1, d)
        g2 = norm2_g[layer].reshape(1, d)
        w_in_l = w_in[layer].astype(BF16)
        w_out_l = w_out[layer].astype(BF16)
        mw = mla_weights(q_norm_g[layer], w_uq[layer], kv_norm_g[layer], w_ukv[layer])
        fw = (filt_w1[layer], filt_b1[layer], filt_freq1[layer], filt_w2[layer], filt_b2[layer],
              filt_freq2[layer], filt_w3[layer])
        bias = hy_bias[layer].reshape(1, HY_WIDTH)

        def channel_mix(xs, md):
            i = layer // 2
            if layer % 2 == 0:
                return ffn_dense(xs, g2, md[3], md[4], md[5], ffn_wg[i].astype(BF16),
                                 ffn_wu[i].astype(BF16), ffn_wd[i].astype(BF16))
            comb, rank = moe_route(xs, g2, md[3], md[4], moe_router[i])
            ew = (moe_wg[i].astype(BF16), moe_wu[i].astype(BF16), moe_wd[i].astype(BF16))
            if xs.shape[1] % MOE_TILE:
                return moe_dense(xs, g2, md[3], md[4], md[5], comb, *ew)
            return moe_sparse(xs, g2, md[3], md[4], md[5], comb, rank, *ew)

        hy_l, pool_l, q_l, kt_l, v_l = in_projection(xl, g1, mod_l[0], mod_l[1], w_in_l, *rope_l, mw)
        hy_c, pool_c, q_c, kt_c, v_c = in_projection(xc, g1, mod_c[0], mod_c[1], w_in_l, *rope_c, mw)

        att_l = attention(q_l, kt_c, v_c, kt_l, v_l)
        z_l, x0_l = hyena_pre(hy_l, hy_short_w[layer], hy_short_b[layer])
        filt_l, nrm_l = hyena_filter(l, ftab_l, *fw)
        yhy_l = hyena_long_conv(z_l, x0_l, filt_l, nrm_l, bias, dtabs)
        ypool_l = pool_mix(pool_l, pool_w[layer], pool_scale[layer])
        xl = out_projection(xl, yhy_l, ypool_l, att_l, w_out_l, mod_l[2])

        if not last:
            att_c = attention(q_c, kt_c, v_c)
            z_c, x0_c = hyena_pre(hy_c, hy_short_w[layer], hy_short_b[layer])
            filt_c, nrm_c = hyena_filter(lc, ftab_c, *fw)
            yhy_c = small_long_conv(z_c, x0_c, filt_c, nrm_c, bias)
            ypool_c = pool_mix(pool_c, pool_w[layer], pool_scale[layer])
            xc = out_projection(xc, yhy_c, ypool_c, att_c, w_out_l, mod_c[2])
            xc = channel_mix(xc, mod_c)

        xl = channel_mix(xl, mod_l)
    return final_norm(xl, final_g)
```

```python
import functools
import math

import jax
import jax.numpy as jnp
import numpy as np
from jax import lax
from jax.experimental import pallas as pl
from jax.experimental.pallas import tpu as pltpu

F32 = jnp.float32
BF16 = jnp.bfloat16
HIGHEST = lax.Precision.HIGHEST

EPS = 1e-6
GRID_W = 64
HY_WIDTH = 256
POOL_WINDOWS = (2, 4, 8, 16)
POOL_WIDTH = 256
POOL_GROUP = 64
POOL_HALO = 8
MLA_HEADS = 8
QK_NOPE = 64
QK_ROPE = 32
V_DIM = 64
Q_LORA = 384
KV_LORA = 256
HEAD_PAD = 128
SM_SCALE = (QK_NOPE + QK_ROPE) ** -0.5
ROPE_THETA = 10000.0
FILT_EMB = 33
FILT_BANDS = 16
DECAY_TARGET = 1e-2
FAST_DECAY_PCT = 0.3
SLOW_DECAY_PCT = 1.5
N_EXPERTS = 8
LANES = 128
FFT_N2 = 128
VMEM_LIMIT = 56 * 2 ** 20


def _cp(*sem):
    return pltpu.CompilerParams(dimension_semantics=sem, vmem_limit_bytes=VMEM_LIMIT)


def _dot(a, b, **kw):
    return jnp.dot(a, b, preferred_element_type=F32, **kw)


def _normmod(x, g, shift, scale):
    ms = jnp.mean(x * x, axis=-1, keepdims=True)
    return x * lax.rsqrt(ms + EPS) * (g * (1.0 + scale)) + shift


def _ada_kernel(c_ref, w_ref, b_ref, o_ref):
    c = c_ref[...]
    s = c * jax.nn.sigmoid(c)
    o_ref[0] = _dot(s, w_ref[0], precision=HIGHEST) + b_ref[0]


def ada_vectors(cc, w_ada, b_ada):
    depth, d, n = w_ada.shape
    tn = 1536
    return pl.pallas_call(
        _ada_kernel,
        grid=(depth, n // tn),
        in_specs=[pl.BlockSpec((8, d), lambda l, j: (0, 0)),
                  pl.BlockSpec((1, d, tn), lambda l, j: (l, 0, j)),
                  pl.BlockSpec((1, 1, tn), lambda l, j: (l, 0, j))],
        out_specs=pl.BlockSpec((1, 8, tn), lambda l, j: (l, 0, j)),
        out_shape=jax.ShapeDtypeStruct((depth, 8, n), F32),
        compiler_params=_cp("parallel", "parallel"),
        name="ada_vectors",
    )(cc, w_ada, b_ada.reshape(depth, 1, n))


IN_HY, IN_POOL, IN_CQ, IN_CKV, IN_KPE = 768, 256, 384, 256, 32
IN_OFFS = (0, 768, 1024, 1408, 1664, 1696)


def _inproj_kernel(x_ref, g_ref, sh_ref, sc_ref, w_ref, c_ref, s_ref, wqa_ref, wqb_ref, wk_ref, wv_ref,
                   pa_ref, pb_ref, hy_ref, pool_ref, q_ref, kt_ref, v_ref):
    h = _normmod(x_ref[0], g_ref[...], sh_ref[0], sc_ref[0]).astype(BF16)
    y = _dot(h, w_ref[...])
    o = IN_OFFS
    hy_ref[0] = y[:, o[0]:o[1]]
    pool_ref[0] = y[:, o[1]:o[2]]
    cq, ckv = y[:, o[2]:o[3]], y[:, o[3]:o[4]]
    kpe = y[:, o[4]:o[5]].astype(BF16)
    cqn = (cq * lax.rsqrt(jnp.mean(cq * cq, axis=-1, keepdims=True) + EPS)).astype(BF16)
    ckvn = (ckv * lax.rsqrt(jnp.mean(ckv * ckv, axis=-1, keepdims=True) + EPS)).astype(BF16)
    cs, sn = c_ref[...], s_ref[...]
    qa = _dot(cqn, wqa_ref[...])
    qb = _dot(cqn, wqb_ref[...])
    kn = _dot(ckvn, wk_ref[...])
    vv = _dot(ckvn, wv_ref[...])
    kpa = _dot(kpe, pa_ref[...])
    kpb = _dot(kpe, pb_ref[...])
    kpe_rot = kpa * cs + kpb * sn
    lane = lax.broadcasted_iota(jnp.int32, (1, HEAD_PAD), 1)
    for hd in range(MLA_HEADS):
        sl = slice(hd * HEAD_PAD, (hd + 1) * HEAD_PAD)
        q_ref[0, hd] = (qa[:, sl] * cs + qb[:, sl] * sn).astype(BF16)
        kt_ref[0, hd] = (kn[:, sl] + kpe_rot).T.astype(BF16)
        one_lane = V_DIM if hd % 2 == 0 else 0
        v_ref[0, hd] = jnp.where(lane == one_lane, 1.0, vv[:, sl]).astype(BF16)


def in_projection(x, g, shift, scale, w_in, ctab, stab, wts):
    b, l, d = x.shape
    tm = min(512, l)
    hw = MLA_HEADS
    row = lambda bi, i: (bi, i, 0)
    vec = lambda bi, i: (bi, 0, 0)
    full = lambda a: pl.BlockSpec(a.shape, lambda bi, i: (0,) * a.ndim)
    return pl.pallas_call(
        _inproj_kernel,
        grid=(b, l // tm),
        in_specs=[pl.BlockSpec((1, tm, d), row),
                  pl.BlockSpec((1, d), lambda bi, i: (0, 0)),
                  pl.BlockSpec((1, 1, d), vec),
                  pl.BlockSpec((1, 1, d), vec),
                  full(w_in),
                  pl.BlockSpec((tm, HEAD_PAD), lambda bi, i: (i, 0)),
                  pl.BlockSpec((tm, HEAD_PAD), lambda bi, i: (i, 0))] + [full(w) for w in wts],
        out_specs=[pl.BlockSpec((1, tm, IN_HY), row),
                   pl.BlockSpec((1, tm, IN_POOL), row),
                   pl.BlockSpec((1, hw, tm, HEAD_PAD), lambda bi, i: (bi, 0, i, 0)),
                   pl.BlockSpec((1, hw, HEAD_PAD, tm), lambda bi, i: (bi, 0, 0, i)),
                   pl.BlockSpec((1, hw, tm, HEAD_PAD), lambda bi, i: (bi, 0, i, 0))],
        out_shape=[jax.ShapeDtypeStruct((b, l, IN_HY), F32),
                   jax.ShapeDtypeStruct((b, l, IN_POOL), F32),
                   jax.ShapeDtypeStruct((b, hw, l, HEAD_PAD), BF16),
                   jax.ShapeDtypeStruct((b, hw, HEAD_PAD, l), BF16),
                   jax.ShapeDtypeStruct((b, hw, l, HEAD_PAD), BF16)],
        compiler_params=_cp("parallel", "parallel"),
        name="in_projection",
    )(x, g, shift, scale, w_in, ctab, stab, *wts)


def mla_weights(q_norm_g, w_uq, kv_norm_g, w_ukv):
    hw, dq = MLA_HEADS, QK_NOPE + QK_ROPE
    half = QK_ROPE // 2
    wq = (w_uq * q_norm_g[:, None] * (SM_SCALE * math.log2(math.e))).reshape(Q_LORA, hw, dq)
    pad = jnp.zeros((Q_LORA, hw, HEAD_PAD - dq), F32)
    wqa = jnp.concatenate([wq, pad], -1)
    swap = jnp.concatenate([wq[..., QK_NOPE + half:], wq[..., QK_NOPE:QK_NOPE + half]], -1)
    wqb = jnp.concatenate([jnp.zeros((Q_LORA, hw, QK_NOPE), F32), swap, pad], -1)
    wkv = (w_ukv * kv_norm_g[:, None]).reshape(KV_LORA, hw, QK_NOPE + V_DIM)
    z64 = jnp.zeros((KV_LORA, hw, 64), F32)
    wk = jnp.concatenate([wkv[..., :QK_NOPE], z64], -1)
    v = wkv[..., QK_NOPE:]
    wv = jnp.where((jnp.arange(hw) % 2 == 0)[None, :, None],
                   jnp.concatenate([v, z64], -1), jnp.concatenate([z64, v], -1))
    eye = np.eye(QK_ROPE, dtype=np.float32)
    pa = np.zeros((QK_ROPE, HEAD_PAD), np.float32)
    pa[:, QK_NOPE:QK_NOPE + QK_ROPE] = eye
    pb = np.zeros((QK_ROPE, HEAD_PAD), np.float32)
    pb[:, QK_NOPE:QK_NOPE + QK_ROPE] = np.concatenate([eye[:, half:], eye[:, :half]], 1)
    flat = lambda w: w.reshape(w.shape[0], hw * HEAD_PAD).astype(BF16)
    return (flat(wqa), flat(wqb), flat(wk), flat(wv), jnp.asarray(pa, BF16), jnp.asarray(pb, BF16))


def rope_tables(l, with_rope):
    ctab = np.zeros((l, HEAD_PAD), np.float32)
    stab = np.zeros((l, HEAD_PAD), np.float32)
    ctab[:, :QK_NOPE + QK_ROPE] = 1.0
    if not with_rope:
        return jnp.asarray(ctab), jnp.asarray(stab)
    n_freq = QK_ROPE // 4
    inv_freq = 1.0 / (ROPE_THETA ** (jnp.arange(n_freq, dtype=F32) / n_freq))
    rows = l // GRID_W
    row = jnp.repeat(jnp.arange(rows, dtype=F32), GRID_W)
    col = jnp.tile(jnp.arange(GRID_W, dtype=F32), rows)
    ang = jnp.concatenate([row[:, None] * inv_freq, col[:, None] * inv_freq], axis=-1)
    cos, sin = jnp.cos(ang), jnp.sin(ang)
    one = jnp.ones((l, QK_NOPE), F32)
    zero = jnp.zeros((l, QK_NOPE), F32)
    tail = jnp.zeros((l, HEAD_PAD - QK_NOPE - QK_ROPE), F32)
    return (jnp.concatenate([one, cos, cos, tail], -1), jnp.concatenate([zero, -sin, sin, tail], -1))


ATT_TQ = 1024
ATT_RB = 1024
ATT_CK = 2048


def _row_max(s):
    mp = s[:, 0:LANES]
    for j in range(1, s.shape[1] // LANES):
        mp = jnp.maximum(mp, s[:, j * LANES:(j + 1) * LANES])
    return jnp.max(mp, axis=-1, keepdims=True)


def _attn_kernel(q_ref, kct_ref, vc_ref, *rest, n_chunks, ck, tq):
    if n_chunks:
        kt0_ref, kt1_ref, v0_ref, v1_ref, o_ref = rest
        kt_refs, v_refs = (kt0_ref, kt1_ref), (v0_ref, v1_ref)
    else:
        (o_ref,) = rest
    rb = min(ATT_RB, tq)
    chains = [(hh, r) for hh in range(2) for r in range(tq // rb)]
    lane = lax.broadcasted_iota(jnp.int32, (1, HEAD_PAD), 1)

    def q_rows(hh, r):
        return q_ref[0, hh, r * rb:(r + 1) * rb, :]

    state = []
    for hh, r in chains:
        s = _dot(q_rows(hh, r), kct_ref[0, hh])
        m = _row_max(s)
        p = jnp.exp2(s - m).astype(BF16)
        state += [m, _dot(p, vc_ref[0, hh])]

    if n_chunks:
        def body(c, state):
            off = pl.multiple_of(c * ck, ck)
            out = []
            for i, (hh, r) in enumerate(chains):
                m, acc = state[2 * i], state[2 * i + 1]
                s = _dot(q_rows(hh, r), kt_refs[hh][0, 0, :, pl.ds(off, ck)])
                m_new = jnp.maximum(m, _row_max(s))
                alpha = jnp.exp2(m - m_new)
                p = jnp.exp2(s - m_new).astype(BF16)
                out += [m_new, acc * alpha + _dot(p, v_refs[hh][0, 0, pl.ds(off, ck), :])]
            return tuple(out)
        state = lax.fori_loop(0, n_chunks, body, tuple(state), unroll=2 if n_chunks % 2 == 0 else 1)

    for r in range(tq // rb):
        outs = []
        for hh in range(2):
            acc = state[2 * chains.index((hh, r)) + 1]
            one_lane = V_DIM if hh == 0 else 0
            denom = jnp.sum(jnp.where(lane == one_lane, acc, 0.0), axis=-1, keepdims=True)
            outs.append(acc / denom)
        o_ref[0, r * rb:(r + 1) * rb, :] = jnp.where(lane < V_DIM, outs[0], outs[1]).astype(o_ref.dtype)


def attention(q, kct, vc, kt=None, v=None):
    b, hw, lq, _ = q.shape
    lc = vc.shape[2]
    tq = min(ATT_TQ, lq)
    ck = ATT_CK if kt is None else min(ATT_CK, kt.shape[3])
    n_chunks = 0 if kt is None else kt.shape[3] // ck
    in_specs = [pl.BlockSpec((1, 2, tq, HEAD_PAD), lambda bi, hp, i: (bi, hp, i, 0)),
                pl.BlockSpec((1, 2, HEAD_PAD, lc), lambda bi, hp, i: (bi, hp, 0, 0)),
                pl.BlockSpec((1, 2, lc, HEAD_PAD), lambda bi, hp, i: (bi, hp, 0, 0))]
    args = [q, kct, vc]
    if n_chunks:
        l = kt.shape[3]
        head = lambda hh: (lambda bi, hp, i: (bi, 2 * hp + hh, 0, 0))
        in_specs += [pl.BlockSpec((1, 1, HEAD_PAD, l), head(hh), pipeline_mode=pl.Buffered(1)) for hh in range(2)]
        in_specs += [pl.BlockSpec((1, 1, l, HEAD_PAD), head(hh), pipeline_mode=pl.Buffered(1)) for hh in range(2)]
        args += [kt, kt, v, v]
    return pl.pallas_call(
        functools.partial(_attn_kernel, n_chunks=n_chunks, ck=ck, tq=tq),
        grid=(b, hw // 2, lq // tq),
        in_specs=in_specs,
        out_specs=pl.BlockSpec((1, tq, HEAD_PAD), lambda bi, hp, i: (bi, i, hp)),
        out_shape=jax.ShapeDtypeStruct((b, lq, hw * V_DIM), BF16),
        compiler_params=_cp("parallel", "parallel", "arbitrary"),
        name="attention",
    )(*args)


def _halo_specs(tm, width, nblk):
    r = tm // POOL_HALO
    last = nblk * r - 1
    return [pl.BlockSpec((1, tm, width), lambda bi, i: (bi, i, 0)),
            pl.BlockSpec((1, POOL_HALO, width), lambda bi, i: (bi, jnp.maximum(i * r - 1, 0), 0)),
            pl.BlockSpec((1, POOL_HALO, width), lambda bi, i: (bi, jnp.minimum((i + 1) * r, last), 0))]


def _fill_padded(buf_ref, cur_ref, prev_ref, next_ref, tm):
    i = pl.program_id(1)
    first = i == 0
    last = i == pl.num_programs(1) - 1
    buf_ref[0:POOL_HALO] = jnp.where(first, 0.0, prev_ref[0])
    buf_ref[POOL_HALO:POOL_HALO + tm] = cur_ref[0]
    buf_ref[POOL_HALO + tm:] = jnp.where(last, 0.0, next_ref[0])


def _hyena_pre_kernel(u_ref, up_ref, un_ref, w_ref, b_ref, z_ref, x0_ref, buf_ref, *, tm):
    _fill_padded(buf_ref, u_ref, up_ref, un_ref, tm)
    w = w_ref[...]
    uc = (buf_ref[POOL_HALO - 1:POOL_HALO - 1 + tm] * w[0:1] + buf_ref[POOL_HALO:POOL_HALO + tm] * w[1:2]
          + buf_ref[POOL_HALO + 1:POOL_HALO + 1 + tm] * w[2:3] + b_ref[...])
    x0_ref[0] = uc[:, :HY_WIDTH]
    z_ref[0] = uc[:, HY_WIDTH:2 * HY_WIDTH] * uc[:, 2 * HY_WIDTH:]


def hyena_pre(u, sw, sb):
    b, l, wd = u.shape
    tm = min(512, l)
    nblk = l // tm
    return pl.pallas_call(
        functools.partial(_hyena_pre_kernel, tm=tm),
        grid=(b, nblk),
        in_specs=_halo_specs(tm, wd, nblk) + [pl.BlockSpec((3, wd), lambda bi, i: (0, 0)),
                                              pl.BlockSpec((1, wd), lambda bi, i: (0, 0))],
        out_specs=[pl.BlockSpec((1, tm, HY_WIDTH), lambda bi, i: (bi, i, 0))] * 2,
        out_shape=[jax.ShapeDtypeStruct((b, l, HY_WIDTH), F32)] * 2,
        scratch_shapes=[pltpu.VMEM((tm + 2 * POOL_HALO, wd), F32)],
        compiler_params=_cp("parallel", "parallel"),
        name="hyena_pre",
    )(u, u, u, sw, sb.reshape(1, wd))


def _filter_kernel(feat_ref, tdec_ref, w1_ref, b1_ref, f1_ref, w2_ref, b2_ref, f2_ref, w3_ref, dl_ref,
                   filt_ref, norm_ref, *, l, tr):
    i = pl.program_id(0)
    h = jnp.sin(f1_ref[...] * (_dot(feat_ref[...], w1_ref[...], precision=HIGHEST) + b1_ref[...]))
    h = jnp.sin(f2_ref[...] * (_dot(h, w2_ref[...], precision=HIGHEST) + b2_ref[...]))
    h = _dot(h, w3_ref[...], precision=HIGHEST)
    decay = jnp.exp(-tdec_ref[...] * jnp.abs(dl_ref[...]))
    n = i * tr + lax.broadcasted_iota(jnp.int32, (tr, 1), 0)
    f = jnp.where(n < l, h[:, :HY_WIDTH], h[:, HY_WIDTH:]) * decay
    f = jnp.where(n == l, 0.0, f)
    filt_ref[...] = f

    @pl.when(i == 0)
    def _():
        norm_ref[...] = jnp.zeros_like(norm_ref)
    norm_ref[...] += jnp.sum(jnp.abs(f), axis=0, keepdims=True)


def filter_tables(l):
    n = jnp.arange(2 * l)
    pos = jnp.where(n < l, n, 2 * l - n).astype(F32)[:, None]
    t = pos / max(l - 1, 1)
    w = 2.0 * math.pi * pos / l
    f = jnp.linspace(1e-4, FILT_BANDS - 1, FILT_BANDS, dtype=F32)[None, :]
    feat = jnp.concatenate([t, jnp.cos(f * w), -jnp.sin(f * w),
                            jnp.zeros((2 * l, LANES - FILT_EMB), F32)], axis=-1)
    return feat, t


def hyena_filter(l, tables, w1, b1, f1, w2, b2, f2, w3):
    feat, t = tables
    tr = min(512, 2 * l)
    hid = w2.shape[0]
    max_decay = math.log(DECAY_TARGET) / FAST_DECAY_PCT
    min_decay = math.log(DECAY_TARGET) / SLOW_DECAY_PCT
    deltas = jnp.linspace(min_decay, max_decay, HY_WIDTH, dtype=F32)[None, :]
    w1p = jnp.concatenate([w1, jnp.zeros((LANES - FILT_EMB, hid), F32)], 0)
    full = lambda a: pl.BlockSpec(a.shape, lambda i: (0,) * a.ndim)
    smalls = [w1p, b1.reshape(1, hid), f1.reshape(1, hid), w2, b2.reshape(1, hid), f2.reshape(1, hid), w3, deltas]
    return pl.pallas_call(
        functools.partial(_filter_kernel, l=l, tr=tr),
        grid=(2 * l // tr,),
        in_specs=[pl.BlockSpec((tr, LANES), lambda i: (i, 0)),
                  pl.BlockSpec((tr, 1), lambda i: (i, 0))] + [full(a) for a in smalls],
        out_specs=[pl.BlockSpec((tr, HY_WIDTH), lambda i: (i, 0)),
                   pl.BlockSpec((1, HY_WIDTH), lambda i: (0, 0))],
        out_shape=[jax.ShapeDtypeStruct((2 * l, HY_WIDTH), F32),
                   jax.ShapeDtypeStruct((1, HY_WIDTH), F32)],
        compiler_params=_cp("arbitrary"),
        name="hyena_filter",
    )(feat, t, *smalls)


def _dft_rows_kernel(f_ref, x_ref, o_ref):
    o_ref[0] = _dot(f_ref[...], x_ref[0].astype(BF16)).astype(o_ref.dtype)


def dft_rows(fmat, x, out_dtype, scale_cols=None):
    b, k, w = x.shape
    m = fmat.shape[0]
    tn = min(2048, w)
    return pl.pallas_call(
        _dft_rows_kernel,
        grid=(b, w // tn),
        in_specs=[pl.BlockSpec((m, k), lambda bi, j: (0, 0)),
                  pl.BlockSpec((1, k, tn), lambda bi, j: (bi, 0, j))],
        out_specs=pl.BlockSpec((1, m, tn), lambda bi, j: (bi, 0, j)),
        out_shape=jax.ShapeDtypeStruct((b, m, w), out_dtype),
        compiler_params=_cp("parallel", "parallel"),
        name="dft_rows",
    )(fmat, x)


SPEC_KB = 8


def _inner_dft(g, are, aim):
    n2 = are.shape[0]
    p1 = _dot(g, are)
    p2 = _dot(g, aim)
    return p1[:n2] - p2[n2:], p2[:n2] + p1[n2:]


def _spectrum_kernel(g_ref, a_ref, h_ref):
    for j in range(g_ref.shape[0]):
        h_ref[0, j], h_ref[1, j] = _inner_dft(g_ref[j], a_ref[0, 0, j], a_ref[0, 1, j])


def filter_spectrum(gs, a):
    _, _, n1, n2, c = a.shape
    kb = min(SPEC_KB, n1)
    return pl.pallas_call(
        _spectrum_kernel,
        grid=(n1 // kb,),
        in_specs=[pl.BlockSpec((kb, 2 * n2, n2), lambda k: (k, 0, 0)),
                  pl.BlockSpec((1, 2, kb, n2, c), lambda k: (0, 0, k, 0, 0))],
        out_specs=pl.BlockSpec((2, kb, n2, c), lambda k: (0, k, 0, 0)),
        out_shape=jax.ShapeDtypeStruct((2, n1, n2, c), F32),
        compiler_params=_cp("parallel"),
        name="filter_spectrum",
    )(gs, a)


def _spectral_kernel(g_ref, gt_ref, h_ref, a_ref, b_ref):
    n2 = a_ref.shape[3]
    for j in range(g_ref.shape[0]):
        xre, xim = _inner_dft(g_ref[j], a_ref[0, 0, j], a_ref[0, 1, j])
        hre, him = h_ref[0, j], h_ref[1, j]
        yre = (xre * hre - xim * him).astype(BF16)
        yim = (xre * him + xim * hre).astype(BF16)
        gt = gt_ref[j]
        q1 = _dot(gt, yre)
        q2 = _dot(gt, yim)
        b_ref[0, 0, j] = (q1[:n2] + q2[n2:]).astype(b_ref.dtype)
        b_ref[0, 1, j] = (q2[:n2] - q1[n2:]).astype(b_ref.dtype)


def spectral_multiply(gs, gts, hf, a):
    b, _, n1, n2, c = a.shape
    kb = min(SPEC_KB, n1)
    return pl.pallas_call(
        _spectral_kernel,
        grid=(n1 // kb, b),
        in_specs=[pl.BlockSpec((kb, 2 * n2, n2), lambda k, bi: (k, 0, 0)),
                  pl.BlockSpec((kb, 2 * n2, n2), lambda k, bi: (k, 0, 0)),
                  pl.BlockSpec((2, kb, n2, c), lambda k, bi: (0, k, 0, 0)),
                  pl.BlockSpec((1, 2, kb, n2, c), lambda k, bi: (bi, 0, k, 0, 0))],
        out_specs=pl.BlockSpec((1, 2, kb, n2, c), lambda k, bi: (bi, 0, k, 0, 0)),
        out_shape=jax.ShapeDtypeStruct(a.shape, BF16),
        compiler_params=_cp("parallel", "arbitrary"),
        name="spectral_multiply",
    )(gs, gts, hf, a)


def _idft_rows_kernel(f_ref, bm_ref, x0_ref, z_ref, inv_ref, bias_ref, o_ref, *, inv_n):
    y = _dot(f_ref[...], bm_ref[0]) * inv_n
    z = z_ref[0]
    o_ref[0] = (x0_ref[0] * (y * inv_ref[...] + bias_ref[...] * z)).astype(o_ref.dtype)


def idft_rows_gate(fmat, bm, x0, z, inv_norm, bias, n):
    b, k, w = bm.shape
    m = fmat.shape[0]
    tn = min(2048, w)
    col = lambda bi, j: (bi, 0, j)
    return pl.pallas_call(
        functools.partial(_idft_rows_kernel, inv_n=1.0 / n),
        grid=(b, w // tn),
        in_specs=[pl.BlockSpec((m, k), lambda bi, j: (0, 0)),
                  pl.BlockSpec((1, k, tn), col),
                  pl.BlockSpec((1, m, tn), col),
                  pl.BlockSpec((1, m, tn), col),
                  pl.BlockSpec((1, tn), lambda bi, j: (0, j)),
                  pl.BlockSpec((1, tn), lambda bi, j: (0, j))],
        out_specs=pl.BlockSpec((1, m, tn), col),
        out_shape=jax.ShapeDtypeStruct((b, m, w), BF16),
        compiler_params=_cp("parallel", "parallel"),
        name="idft_rows_gate",
    )(fmat, bm, x0, z, inv_norm, bias)


def dft_tables(l):
    n = 2 * l
    n2 = FFT_N2
    n1 = n // n2
    k1 = np.arange(n1)
    ang1 = 2.0 * np.pi * np.outer(k1, k1) / n1
    f1 = np.concatenate([np.cos(ang1), -np.sin(ang1)], 0)
    f3 = np.concatenate([np.cos(ang1), -np.sin(ang1)], 1)[: n1 // 2]
    kk = (jnp.arange(n1, dtype=jnp.int32)[:, None, None] + n1 * jnp.arange(n2, dtype=jnp.int32)[None, :, None])
    prod = (kk * jnp.arange(n2, dtype=jnp.int32)[None, None, :]) % n
    ang = prod.astype(F32) * (2.0 * math.pi / n)
    gre, gim = jnp.cos(ang), -jnp.sin(ang)
    gs = jnp.concatenate([gre, gim], 1).astype(BF16)
    gts = jnp.concatenate([gre.transpose(0, 2, 1), gim.transpose(0, 2, 1)], 1).astype(BF16)
    return dict(n=n, n1=n1, n2=n2, f1=jnp.asarray(f1, BF16), f3=jnp.asarray(f3, BF16), gs=gs, gts=gts)


def hyena_long_conv(z, x0, filt2, norm, bias, tabs):
    b, l, c = z.shape
    n, n1, n2 = tabs["n"], tabs["n1"], tabs["n2"]
    w = n2 * c
    fa = dft_rows(tabs["f1"], filt2.reshape(1, n1, w), BF16)
    hf = filter_spectrum(tabs["gs"], fa.reshape(1, 2, n1, n2, c))
    a = dft_rows(tabs["f1"][:, : n1 // 2], z.reshape(b, n1 // 2, w), BF16)
    bm = spectral_multiply(tabs["gs"], tabs["gts"], hf, a.reshape(b, 2, n1, n2, c))
    tile = lambda v: jnp.tile(v.reshape(1, c), (1, n2))
    y = idft_rows_gate(tabs["f3"], bm.reshape(b, 2 * n1, w), x0.reshape(b, n1 // 2, w),
                       z.reshape(b, n1 // 2, w), tile(1.0 / norm), tile(bias), n)
    return y.reshape(b, l, c)


def _small_conv_kernel(ff_ref, fi_ref, filt_ref, z_ref, x0_ref, inv_ref, bias_ref, o_ref, *, l):
    ff = ff_ref[...]
    hs = _dot(ff, filt_ref[...].astype(BF16))
    hre, him = hs[:2 * l], hs[2 * l:]
    z = z_ref[0]
    xs = _dot(ff[:, :l], z.astype(BF16))
    xre, xim = xs[:2 * l], xs[2 * l:]
    yre = (xre * hre - xim * him).astype(BF16)
    yim = (xre * him + xim * hre).astype(BF16)
    y = _dot(fi_ref[...], jnp.concatenate([yre, yim], 0)) * (0.5 / l)
    o_ref[0] = (x0_ref[0] * (y * inv_ref[...] + bias_ref[...] * z)).astype(o_ref.dtype)


def small_long_conv(z, x0, filt2, norm, bias):
    b, l, c = z.shape
    k = np.arange(2 * l)
    ang = 2.0 * np.pi * np.outer(k, k) / (2 * l)
    ff = jnp.asarray(np.concatenate([np.cos(ang), -np.sin(ang)], 0), BF16)
    fi = jnp.asarray(np.concatenate([np.cos(ang), -np.sin(ang)], 1)[:l], BF16)
    full = lambda a: pl.BlockSpec(a.shape, lambda bi: (0,) * a.ndim)
    row = pl.BlockSpec((1, l, c), lambda bi: (bi, 0, 0))
    inv = 1.0 / norm
    return pl.pallas_call(
        functools.partial(_small_conv_kernel, l=l),
        grid=(b,),
        in_specs=[full(ff), full(fi), full(filt2), row, row, full(inv), full(bias)],
        out_specs=row,
        out_shape=jax.ShapeDtypeStruct((b, l, c), BF16),
        compiler_params=_cp("parallel"),
        name="small_long_conv",
    )(ff, fi, filt2, z, x0, inv, bias)


def _pool_kernel(p_ref, pp_ref, pn_ref, w_ref, sc_ref, o_ref, buf_ref, *, tm, l):
    _fill_padded(buf_ref, p_ref, pp_ref, pn_ref, tm)
    t = pl.program_id(1) * tm + lax.broadcasted_iota(jnp.int32, (tm, LANES), 0)
    lane = lax.broadcasted_iota(jnp.int32, (tm, LANES), 1)
    first_group = lane < POOL_GROUP
    h0 = POOL_HALO

    def win(col, lo, hi):
        acc = buf_ref[h0 + lo:h0 + lo + tm, col]
        for j in range(lo + 1, hi):
            acc = acc + buf_ref[h0 + j:h0 + j + tm, col]
        return acc

    parts = []
    for half, (wa, wb) in enumerate(((2, 4), (8, 16))):
        col = slice(half * LANES, (half + 1) * LANES)
        sa = win(col, -wa // 2, wa // 2)
        sb = sa + win(col, -wb // 2, -wa // 2) + win(col, wa // 2, wb // 2)
        hw = jnp.where(first_group, wa // 2, wb // 2)
        cnt = jnp.minimum(t + hw, l) - jnp.maximum(t - hw, 0)
        mean = jnp.where(first_group, sa, sb) / cnt.astype(F32)
        parts.append(mean - buf_ref[h0:h0 + tm, col])
    d = jnp.concatenate(parts, axis=-1).astype(BF16)
    o_ref[0] = (_dot(d, w_ref[...]) * sc_ref[...]).astype(o_ref.dtype)


def pool_mix(p, pool_w, pool_scale):
    b, l, wd = p.shape
    tm = min(512, l)
    nblk = l // tm
    wblk = jax.scipy.linalg.block_diag(*[pool_w[g] for g in range(len(POOL_WINDOWS))]).astype(BF16)
    return pl.pallas_call(
        functools.partial(_pool_kernel, tm=tm, l=l),
        grid=(b, nblk),
        in_specs=_halo_specs(tm, wd, nblk) + [pl.BlockSpec((wd, wd), lambda bi, i: (0, 0)),
                                              pl.BlockSpec((1, wd), lambda bi, i: (0, 0))],
        out_specs=pl.BlockSpec((1, tm, wd), lambda bi, i: (bi, i, 0)),
        out_shape=jax.ShapeDtypeStruct((b, l, wd), BF16),
        scratch_shapes=[pltpu.VMEM((tm + 2 * POOL_HALO, wd), F32)],
        compiler_params=_cp("parallel", "parallel"),
        name="pool_mix",
    )(p, p, p, wblk, pool_scale.reshape(1, wd))


def _outproj_kernel(x_ref, hy_ref, pool_ref, att_ref, w_ref, gate_ref, o_ref):
    a, b2 = HY_WIDTH, HY_WIDTH + POOL_WIDTH
    y = (_dot(hy_ref[0], w_ref[0:a]) + _dot(pool_ref[0], w_ref[a:b2]) + _dot(att_ref[0], w_ref[b2:]))
    o_ref[0] = x_ref[0] + gate_ref[0] * y


def out_projection(x, y_hy, y_pool, y_att, w_out, gate):
    b, l, d = x.shape
    tm = min(512, l)
    row = lambda bi, i: (bi, i, 0)
    return pl.pallas_call(
        _outproj_kernel,
        grid=(b, l // tm),
        in_specs=[pl.BlockSpec((1, tm, d), row),
                  pl.BlockSpec((1, tm, y_hy.shape[2]), row),
                  pl.BlockSpec((1, tm, y_pool.shape[2]), row),
                  pl.BlockSpec((1, tm, y_att.shape[2]), row),
                  pl.BlockSpec(w_out.shape, lambda bi, i: (0, 0)),
                  pl.BlockSpec((1, 1, d), lambda bi, i: (bi, 0, 0))],
        out_specs=pl.BlockSpec((1, tm, d), row),
        out_shape=jax.ShapeDtypeStruct(x.shape, F32),
        compiler_params=_cp("parallel", "parallel"),
        name="out_projection",
    )(x, y_hy, y_pool, y_att, w_out, gate)


FF_CHUNK = 256
MOE_TILE = 1024
MOE_ROWS = 288
MOE_FF_PIECES = 2
MOE_K_PIECES = 2


def _swiglu_acc(h, wg, wu, wd, lead):
    acc = None
    for wg_p, wu_p, wd_p in zip(wg, wu, wd):
        ffp = wd_p.shape[-2]
        kp = h.shape[1] // len(wg_p)
        for f0 in range(0, ffp, FF_CHUNK):
            cs = slice(f0, min(f0 + FF_CHUNK, ffp))
            g = u = None
            for j, (gr, ur) in enumerate(zip(wg_p, wu_p)):
                hj = h[:, j * kp:(j + 1) * kp]
                gj = _dot(hj, gr[lead + (slice(None), cs)])
                uj = _dot(hj, ur[lead + (slice(None), cs)])
                g, u = (gj, uj) if g is None else (g + gj, u + uj)
            a = (g * jax.nn.sigmoid(g) * u).astype(BF16)
            part = _dot(a, wd_p[lead + (cs, slice(None))])
            acc = part if acc is None else acc + part
    return acc


def _ffn_kernel(x_ref, g_ref, sh_ref, sc_ref, gate_ref, wg_ref, wu_ref, wd_ref, o_ref):
    x = x_ref[0]
    h = _normmod(x, g_ref[...], sh_ref[0], sc_ref[0]).astype(BF16)
    o_ref[0] = x + gate_ref[0] * _swiglu_acc(h, [[wg_ref]], [[wu_ref]], [wd_ref], ())


def ffn_dense(x, g, shift, scale, gate, wg, wu, wd):
    b, l, d = x.shape
    tm = min(512, l)
    row = lambda bi, i: (bi, i, 0)
    vec = lambda bi, i: (bi, 0, 0)
    wspec = lambda w: pl.BlockSpec(w.shape, lambda bi, i: (0, 0), pipeline_mode=pl.Buffered(1))
    return pl.pallas_call(
        _ffn_kernel,
        grid=(b, l // tm),
        in_specs=[pl.BlockSpec((1, tm, d), row),
                  pl.BlockSpec((1, d), lambda bi, i: (0, 0)),
                  pl.BlockSpec((1, 1, d), vec), pl.BlockSpec((1, 1, d), vec), pl.BlockSpec((1, 1, d), vec),
                  wspec(wg), wspec(wu), wspec(wd)],
        out_specs=pl.BlockSpec((1, tm, d), row),
        out_shape=jax.ShapeDtypeStruct(x.shape, F32),
        compiler_params=_cp("parallel", "parallel"),
        name="ffn_dense",
    )(x, g, shift, scale, gate, wg, wu, wd)


def _router_kernel(x_ref, g_ref, sh_ref, sc_ref, wr_ref, tri_ref, comb_ref, rank_ref):
    h = _normmod(x_ref[0], g_ref[...], sh_ref[0], sc_ref[0])
    logits = _dot(h, wr_ref[...], precision=HIGHEST)
    lane = lax.broadcasted_iota(jnp.int32, logits.shape, 1)
    neg = jnp.float32(-jnp.inf)
    lg = jnp.where(lane < N_EXPERTS, logits, neg)
    m1 = jnp.max(lg, axis=-1, keepdims=True)
    i1 = jnp.min(jnp.where(lg == m1, lane, LANES), axis=-1, keepdims=True)
    lg2 = jnp.where(lane == i1, neg, lg)
    m2 = jnp.max(lg2, axis=-1, keepdims=True)
    i2 = jnp.min(jnp.where(lg2 == m2, lane, LANES), axis=-1, keepdims=True)
    e = jnp.exp(m2 - m1)
    g1 = 1.0 / (1.0 + e)
    comb_ref[0] = jnp.where(lane == i1, g1, jnp.where(lane == i2, e * g1, 0.0))
    routed = (lane == i1) | (lane == i2)
    before = _dot(tri_ref[...], jnp.where(routed, 1.0, 0.0).astype(BF16))
    rank_ref[0] = jnp.where(routed, before, -1.0)


def moe_route(x, g, shift, scale, w_router):
    b, l, d = x.shape
    tm = min(MOE_TILE, l)
    wr = jnp.concatenate([w_router, jnp.zeros((d, LANES - N_EXPERTS), F32)], -1)
    tri = jnp.asarray(np.tril(np.ones((tm, tm), np.float32), -1), BF16)
    row = lambda bi, i: (bi, i, 0)
    vec = lambda bi, i: (bi, 0, 0)
    return pl.pallas_call(
        _router_kernel,
        grid=(b, l // tm),
        in_specs=[pl.BlockSpec((1, tm, d), row),
                  pl.BlockSpec((1, d), lambda bi, i: (0, 0)),
                  pl.BlockSpec((1, 1, d), vec), pl.BlockSpec((1, 1, d), vec),
                  pl.BlockSpec((d, LANES), lambda bi, i: (0, 0)),
                  pl.BlockSpec((tm, tm), lambda bi, i: (0, 0))],
        out_specs=[pl.BlockSpec((1, tm, LANES), row)] * 2,
        out_shape=[jax.ShapeDtypeStruct((b, l, LANES), F32)] * 2,
        compiler_params=_cp("parallel", "parallel"),
        name="moe_router",
    )(x, g, shift, scale, wr, tri)


def _moe_kernel(x_ref, g_ref, sh_ref, sc_ref, gate_ref, comb_ref, wg_ref, wu_ref, wd_ref, o_ref, acc_ref):
    e = pl.program_id(2)
    x = x_ref[0]
    h = _normmod(x, g_ref[...], sh_ref[0], sc_ref[0]).astype(BF16)
    comb = comb_ref[0]
    lane = lax.broadcasted_iota(jnp.int32, comb.shape, 1)
    ce = jnp.sum(jnp.where(lane == e, comb, 0.0), axis=-1, keepdims=True)
    y = ce * _swiglu_acc(h, [[wg_ref]], [[wu_ref]], [wd_ref], (0,))

    @pl.when(e == 0)
    def _():
        acc_ref[...] = y

    @pl.when(e > 0)
    def _():
        acc_ref[...] += y

    @pl.when(e == N_EXPERTS - 1)
    def _():
        o_ref[0] = x + gate_ref[0] * acc_ref[...]


def moe_dense(x, g, shift, scale, gate, comb, wg, wu, wd):
    b, l, d = x.shape
    tm = min(512, l)
    ff = wg.shape[-1]
    row = lambda bi, i, e: (bi, i, 0)
    vec = lambda bi, i, e: (bi, 0, 0)
    return pl.pallas_call(
        _moe_kernel,
        grid=(b, l // tm, N_EXPERTS),
        in_specs=[pl.BlockSpec((1, tm, d), row),
                  pl.BlockSpec((1, d), lambda bi, i, e: (0, 0)),
                  pl.BlockSpec((1, 1, d), vec), pl.BlockSpec((1, 1, d), vec), pl.BlockSpec((1, 1, d), vec),
                  pl.BlockSpec((1, tm, LANES), row),
                  pl.BlockSpec((1, d, ff), lambda bi, i, e: (e, 0, 0)),
                  pl.BlockSpec((1, d, ff), lambda bi, i, e: (e, 0, 0)),
                  pl.BlockSpec((1, ff, d), lambda bi, i, e: (e, 0, 0))],
        out_specs=pl.BlockSpec((1, tm, d), row),
        out_shape=jax.ShapeDtypeStruct(x.shape, F32),
        scratch_shapes=[pltpu.VMEM((tm, d), F32)],
        compiler_params=_cp("parallel", "parallel", "arbitrary"),
        name="moe_dense",
    )(x, g, shift, scale, gate, comb, wg, wu, wd)


def _moe_sparse_kernel(cnt_ref, x_ref, g_ref, sh_ref, sc_ref, gate_ref, comb_ref, rank_ref, rankt_ref,
                       *rest, t, r):
    nw = MOE_FF_PIECES * MOE_K_PIECES
    wg = [list(rest[p * MOE_K_PIECES:(p + 1) * MOE_K_PIECES]) for p in range(MOE_FF_PIECES)]
    wu = [list(rest[nw + p * MOE_K_PIECES:nw + (p + 1) * MOE_K_PIECES]) for p in range(MOE_FF_PIECES)]
    wd = list(rest[2 * nw:2 * nw + MOE_FF_PIECES])
    o_ref, h_scr = rest[2 * nw + MOE_FF_PIECES:]
    i, e = pl.program_id(0), pl.program_id(1)
    nblk = (cnt_ref[i * N_EXPERTS + e] + r - 1) // r

    @pl.when(e == 0)
    def _():
        x = x_ref[0]
        h_scr[...] = _normmod(x, g_ref[...], sh_ref[0], sc_ref[0]).astype(BF16)
        o_ref[0] = x

    rrow = rankt_ref[0, pl.ds(e, 1), :]
    lane = lax.broadcasted_iota(jnp.int32, (t, LANES), 1)
    rcol = jnp.sum(jnp.where(lane == e, rank_ref[0], 0.0), axis=-1, keepdims=True)
    ccol = jnp.sum(jnp.where(lane == e, comb_ref[0], 0.0), axis=-1, keepdims=True)

    def block(k, carry):
        rid = (k * r + lax.broadcasted_iota(jnp.int32, (r, 1), 0)).astype(F32)
        onehot = jnp.where(rrow == rid, 1.0, 0.0).astype(BF16)
        xg = _dot(onehot, h_scr[...]).astype(BF16)
        y = _swiglu_acc(xg, wg, wu, wd, (0,)).astype(BF16)
        cid = (k * r + lax.broadcasted_iota(jnp.int32, (1, r), 1)).astype(F32)
        onehot_t = jnp.where(rcol == cid, 1.0, 0.0).astype(BF16)
        o_ref[0] += (gate_ref[0] * ccol) * _dot(onehot_t, y)
        return carry
    lax.fori_loop(0, nblk, block, 0)


def moe_sparse(x, g, shift, scale, gate, comb, rank, wg, wu, wd):
    b, l, d = x.shape
    t, r = MOE_TILE, MOE_ROWS
    tpb = l // t
    n_tiles = b * tpb
    ff = wg.shape[-1]
    rk = rank[..., :N_EXPERTS].reshape(n_tiles, t, N_EXPERTS)
    rank_t = rk.transpose(0, 2, 1)
    cnt = jnp.sum(rk >= 0, axis=1).astype(jnp.int32).reshape(n_tiles * N_EXPERTS)
    row = lambda i, e, c: (i // tpb, i % tpb, 0)
    vec = lambda i, e, c: (i // tpb, 0, 0)
    fp, kp = ff // MOE_FF_PIECES, d // MOE_K_PIECES
    up_specs = [pl.BlockSpec((1, kp, fp), functools.partial(lambda i, e, c, p, j: (e, j, p), p=p, j=j))
                for p in range(MOE_FF_PIECES) for j in range(MOE_K_PIECES)]
    down_specs = [pl.BlockSpec((1, fp, d), functools.partial(lambda i, e, c, p: (e, p, 0), p=p))
                  for p in range(MOE_FF_PIECES)]
    n_up = len(up_specs)
    grid_spec = pltpu.PrefetchScalarGridSpec(
        num_scalar_prefetch=1,
        grid=(n_tiles, N_EXPERTS),
        in_specs=[pl.BlockSpec((1, t, d), row, pipeline_mode=pl.Buffered(1)),
                  pl.BlockSpec((1, d), lambda i, e, c: (0, 0)),
                  pl.BlockSpec((1, 1, d), vec), pl.BlockSpec((1, 1, d), vec), pl.BlockSpec((1, 1, d), vec),
                  pl.BlockSpec((1, t, LANES), row, pipeline_mode=pl.Buffered(1)),
                  pl.BlockSpec((1, t, LANES), row, pipeline_mode=pl.Buffered(1)),
                  pl.BlockSpec((1, N_EXPERTS, t), lambda i, e, c: (i, 0, 0))] + up_specs + up_specs + down_specs,
        out_specs=pl.BlockSpec((1, t, d), row),
        scratch_shapes=[pltpu.VMEM((t, d), BF16)],
    )
    return pl.pallas_call(
        functools.partial(_moe_sparse_kernel, t=t, r=r),
        grid_spec=grid_spec,
        out_shape=jax.ShapeDtypeStruct(x.shape, F32),
        compiler_params=_cp("parallel", "arbitrary"),
        name="moe_sparse",
    )(cnt, x, g, shift, scale, gate, comb, rank, rank_t, *([wg] * n_up), *([wu] * n_up), *([wd] * MOE_FF_PIECES))


def _final_kernel(x_ref, g_ref, o_ref):
    x = x_ref[0]
    o_ref[0] = x * lax.rsqrt(jnp.mean(x * x, axis=-1, keepdims=True) + EPS) * g_ref[...]


def final_norm(x, g):
    b, l, d = x.shape
    tm = min(1024, l)
    return pl.pallas_call(
        _final_kernel,
        grid=(b, l // tm),
        in_specs=[pl.BlockSpec((1, tm, d), lambda bi, i: (bi, i, 0)),
                  pl.BlockSpec((1, d), lambda bi, i: (0, 0))],
        out_specs=pl.BlockSpec((1, tm, d), lambda bi, i: (bi, i, 0)),
        out_shape=jax.ShapeDtypeStruct(x.shape, F32),
        compiler_params=_cp("parallel", "parallel"),
        name="final_norm",
    )(x, g.reshape(1, d))


def kernel(x, c, ctx, c_ctx, norm1_g, norm2_g, w_ada, b_ada, w_in, hy_short_w, hy_short_b, filt_w1, filt_b1, filt_freq1, filt_w2, filt_b2, filt_freq2, filt_w3, hy_bias, pool_w, pool_scale, q_norm_g, w_uq, kv_norm_g, w_ukv, w_out, ffn_wg, ffn_wu, ffn_wd, moe_router, moe_wg, moe_wu, moe_wd, final_g):
    b, l, d = x.shape
    lc = ctx.shape[1]
    depth = w_ada.shape[0]

    cc = jnp.zeros((8, d), F32).at[:b].set(c).at[b].set(c_ctx)
    mods = ada_vectors(cc, w_ada, b_ada)

    rope_l = rope_tables(l, True)
    rope_c = rope_tables(lc, False)
    ftab_l, ftab_c = filter_tables(l), filter_tables(lc)
    dtabs = dft_tables(l)

    xl, xc = x, ctx
    for layer in range(depth):
        last = layer == depth - 1
        mod = mods[layer].reshape(8, 6, 1, d)
        mod_l = [mod[:b, j] for j in range(6)]
        mod_c = [jnp.broadcast_to(mod[b:b + 1, j], (b, 1, d)) for j in range(6)]
        g1 = norm1_g[layer].reshape(1, d)
        g2 = norm2_g[layer].reshape(1, d)
        w_in_l = w_in[layer].astype(BF16)
        w_out_l = w_out[layer].astype(BF16)
        mw = mla_weights(q_norm_g[layer], w_uq[layer], kv_norm_g[layer], w_ukv[layer])
        fw = (filt_w1[layer], filt_b1[layer], filt_freq1[layer], filt_w2[layer], filt_b2[layer],
              filt_freq2[layer], filt_w3[layer])
        bias = hy_bias[layer].reshape(1, HY_WIDTH)

        def channel_mix(xs, md):
            i = layer // 2
            if layer % 2 == 0:
                return ffn_dense(xs, g2, md[3], md[4], md[5], ffn_wg[i].astype(BF16),
                                 ffn_wu[i].astype(BF16), ffn_wd[i].astype(BF16))
            comb, rank = moe_route(xs, g2, md[3], md[4], moe_router[i])
            ew = (moe_wg[i].astype(BF16), moe_wu[i].astype(BF16), moe_wd[i].astype(BF16))
            if xs.shape[1] % MOE_TILE:
                return moe_dense(xs, g2, md[3], md[4], md[5], comb, *ew)
            return moe_sparse(xs, g2, md[3], md[4], md[5], comb, rank, *ew)

        hy_l, pool_l, q_l, kt_l, v_l = in_projection(xl, g1, mod_l[0], mod_l[1], w_in_l, *rope_l, mw)
        hy_c, pool_c, q_c, kt_c, v_c = in_projection(xc, g1, mod_c[0], mod_c[1], w_in_l, *rope_c, mw)

        att_l = attention(q_l, kt_c, v_c, kt_l, v_l)
        z_l, x0_l = hyena_pre(hy_l, hy_short_w[layer], hy_short_b[layer])
        filt_l, nrm_l = hyena_filter(l, ftab_l, *fw)
        yhy_l = hyena_long_conv(z_l, x0_l, filt_l, nrm_l, bias, dtabs)
        ypool_l = pool_mix(pool_l, pool_w[layer], pool_scale[layer])
        xl = out_projection(xl, yhy_l, ypool_l, att_l, w_out_l, mod_l[2])

        if not last:
            att_c = attention(q_c, kt_c, v_c)
            z_c, x0_c = hyena_pre(hy_c, hy_short_w[layer], hy_short_b[layer])
            filt_c, nrm_c = hyena_filter(lc, ftab_c, *fw)
            yhy_c = small_long_conv(z_c, x0_c, filt_c, nrm_c, bias)
            ypool_c = pool_mix(pool_c, pool_w[layer], pool_scale[layer])
            xc = out_projection(xc, yhy_c, ypool_c, att_c, w_out_l, mod_c[2])
            xc = channel_mix(xc, mod_c)

        xl = channel_mix(xl, mod_l)
    return final_norm(xl, final_g)
```

```python
import functools
import math

import jax
import jax.numpy as jnp
import numpy as np
from jax import lax
from jax.experimental import pallas as pl
from jax.experimental.pallas import tpu as pltpu

F32 = jnp.float32
BF16 = jnp.bfloat16
HIGHEST = lax.Precision.HIGHEST

EPS = 1e-6
GRID_W = 64
HY_WIDTH = 256
POOL_WINDOWS = (2, 4, 8, 16)
POOL_WIDTH = 256
POOL_GROUP = 64
POOL_HALO = 8
MLA_HEADS = 8
QK_NOPE = 64
QK_ROPE = 32
V_DIM = 64
Q_LORA = 384
KV_LORA = 256
HEAD_PAD = 128
SM_SCALE = (QK_NOPE + QK_ROPE) ** -0.5
ROPE_THETA = 10000.0
FILT_EMB = 33
FILT_BANDS = 16
DECAY_TARGET = 1e-2
FAST_DECAY_PCT = 0.3
SLOW_DECAY_PCT = 1.5
N_EXPERTS = 8
LANES = 128
FFT_N2 = 128
VMEM_LIMIT = 56 * 2 ** 20


def _cp(*sem):
    return pltpu.CompilerParams(dimension_semantics=sem, vmem_limit_bytes=VMEM_LIMIT)


def _dot(a, b, **kw):
    return jnp.dot(a, b, preferred_element_type=F32, **kw)


def _normmod(x, g, shift, scale):
    ms = jnp.mean(x * x, axis=-1, keepdims=True)
    return x * lax.rsqrt(ms + EPS) * (g * (1.0 + scale)) + shift


def _ada_kernel(c_ref, w_ref, b_ref, o_ref):
    c = c_ref[...]
    s = c * jax.nn.sigmoid(c)
    o_ref[0] = _dot(s, w_ref[0], precision=HIGHEST) + b_ref[0]


def ada_vectors(cc, w_ada, b_ada):
    depth, d, n = w_ada.shape
    tn = 1536
    return pl.pallas_call(
        _ada_kernel,
        grid=(depth, n // tn),
        in_specs=[pl.BlockSpec((8, d), lambda l, j: (0, 0)),
                  pl.BlockSpec((1, d, tn), lambda l, j: (l, 0, j)),
                  pl.BlockSpec((1, 1, tn), lambda l, j: (l, 0, j))],
        out_specs=pl.BlockSpec((1, 8, tn), lambda l, j: (l, 0, j)),
        out_shape=jax.ShapeDtypeStruct((depth, 8, n), F32),
        compiler_params=_cp("parallel", "parallel"),
        name="ada_vectors",
    )(cc, w_ada, b_ada.reshape(depth, 1, n))


IN_HY, IN_POOL, IN_CQ, IN_CKV, IN_KPE = 768, 256, 384, 256, 32
IN_OFFS = (0, 768, 1024, 1408, 1664, 1696)


def _inproj_kernel(x_ref, g_ref, sh_ref, sc_ref, w_ref, c_ref, s_ref, wqa_ref, wqb_ref, wk_ref, wv_ref,
                   pa_ref, pb_ref, hy_ref, pool_ref, q_ref, kt_ref, v_ref):
    h = _normmod(x_ref[0], g_ref[...], sh_ref[0], sc_ref[0]).astype(BF16)
    y = _dot(h, w_ref[...])
    o = IN_OFFS
    hy_ref[0] = y[:, o[0]:o[1]]
    pool_ref[0] = y[:, o[1]:o[2]]
    cq, ckv = y[:, o[2]:o[3]], y[:, o[3]:o[4]]
    kpe = y[:, o[4]:o[5]].astype(BF16)
    cqn = (cq * lax.rsqrt(jnp.mean(cq * cq, axis=-1, keepdims=True) + EPS)).astype(BF16)
    ckvn = (ckv * lax.rsqrt(jnp.mean(ckv * ckv, axis=-1, keepdims=True) + EPS)).astype(BF16)
    cs, sn = c_ref[...], s_ref[...]
    qa = _dot(cqn, wqa_ref[...])
    qb = _dot(cqn, wqb_ref[...])
    kn = _dot(ckvn, wk_ref[...])
    vv = _dot(ckvn, wv_ref[...])
    kpa = _dot(kpe, pa_ref[...])
    kpb = _dot(kpe, pb_ref[...])
    kpe_rot = kpa * cs + kpb * sn
    lane = lax.broadcasted_iota(jnp.int32, (1, HEAD_PAD), 1)
    for hd in range(MLA_HEADS):
        sl = slice(hd * HEAD_PAD, (hd + 1) * HEAD_PAD)
        q_ref[0, hd] = (qa[:, sl] * cs + qb[:, sl] * sn).astype(BF16)
        kt_ref[0, hd] = (kn[:, sl] + kpe_rot).T.astype(BF16)
        one_lane = V_DIM if hd % 2 == 0 else 0
        v_ref[0, hd] = jnp.where(lane == one_lane, 1.0, vv[:, sl]).astype(BF16)


def in_projection(x, g, shift, scale, w_in, ctab, stab, wts):
    b, l, d = x.shape
    tm = min(512, l)
    hw = MLA_HEADS
    row = lambda bi, i: (bi, i, 0)
    vec = lambda bi, i: (bi, 0, 0)
    full = lambda a: pl.BlockSpec(a.shape, lambda bi, i: (0,) * a.ndim)
    return pl.pallas_call(
        _inproj_kernel,
        grid=(b, l // tm),
        in_specs=[pl.BlockSpec((1, tm, d), row),
                  pl.BlockSpec((1, d), lambda bi, i: (0, 0)),
                  pl.BlockSpec((1, 1, d), vec),
                  pl.BlockSpec((1, 1, d), vec),
                  full(w_in),
                  pl.BlockSpec((tm, HEAD_PAD), lambda bi, i: (i, 0)),
                  pl.BlockSpec((tm, HEAD_PAD), lambda bi, i: (i, 0))] + [full(w) for w in wts],
        out_specs=[pl.BlockSpec((1, tm, IN_HY), row),
                   pl.BlockSpec((1, tm, IN_POOL), row),
                   pl.BlockSpec((1, hw, tm, HEAD_PAD), lambda bi, i: (bi, 0, i, 0)),
                   pl.BlockSpec((1, hw, HEAD_PAD, tm), lambda bi, i: (bi, 0, 0, i)),
                   pl.BlockSpec((1, hw, tm, HEAD_PAD), lambda bi, i: (bi, 0, i, 0))],
        out_shape=[jax.ShapeDtypeStruct((b, l, IN_HY), F32),
                   jax.ShapeDtypeStruct((b, l, IN_POOL), F32),
                   jax.ShapeDtypeStruct((b, hw, l, HEAD_PAD), BF16),
                   jax.ShapeDtypeStruct((b, hw, HEAD_PAD, l), BF16),
                   jax.ShapeDtypeStruct((b, hw, l, HEAD_PAD), BF16)],
        compiler_params=_cp("parallel", "parallel"),
        name="in_projection",
    )(x, g, shift, scale, w_in, ctab, stab, *wts)


def mla_weights(q_norm_g, w_uq, kv_norm_g, w_ukv):
    hw, dq = MLA_HEADS, QK_NOPE + QK_ROPE
    half = QK_ROPE // 2
    wq = (w_uq * q_norm_g[:, None] * (SM_SCALE * math.log2(math.e))).reshape(Q_LORA, hw, dq)
    pad = jnp.zeros((Q_LORA, hw, HEAD_PAD - dq), F32)
    wqa = jnp.concatenate([wq, pad], -1)
    swap = jnp.concatenate([wq[..., QK_NOPE + half:], wq[..., QK_NOPE:QK_NOPE + half]], -1)
    wqb = jnp.concatenate([jnp.zeros((Q_LORA, hw, QK_NOPE), F32), swap, pad], -1)
    wkv = (w_ukv * kv_norm_g[:, None]).reshape(KV_LORA, hw, QK_NOPE + V_DIM)
    z64 = jnp.zeros((KV_LORA, hw, 64), F32)
    wk = jnp.concatenate([wkv[..., :QK_NOPE], z64], -1)
    v = wkv[..., QK_NOPE:]
    wv = jnp.where((jnp.arange(hw) % 2 == 0)[None, :, None],
                   jnp.concatenate([v, z64], -1), jnp.concatenate([z64, v], -1))
    eye = np.eye(QK_ROPE, dtype=np.float32)
    pa = np.zeros((QK_ROPE, HEAD_PAD), np.float32)
    pa[:, QK_NOPE:QK_NOPE + QK_ROPE] = eye
    pb = np.zeros((QK_ROPE, HEAD_PAD), np.float32)
    pb[:, QK_NOPE:QK_NOPE + QK_ROPE] = np.concatenate([eye[:, half:], eye[:, :half]], 1)
    flat = lambda w: w.reshape(w.shape[0], hw * HEAD_PAD).astype(BF16)
    return (flat(wqa), flat(wqb), flat(wk), flat(wv), jnp.asarray(pa, BF16), jnp.asarray(pb, BF16))


def rope_tables(l, with_rope):
    ctab = np.zeros((l, HEAD_PAD), np.float32)
    stab = np.zeros((l, HEAD_PAD), np.float32)
    ctab[:, :QK_NOPE + QK_ROPE] = 1.0
    if not with_rope:
        return jnp.asarray(ctab), jnp.asarray(stab)
    n_freq = QK_ROPE // 4
    inv_freq = 1.0 / (ROPE_THETA ** (jnp.arange(n_freq, dtype=F32) / n_freq))
    rows = l // GRID_W
    row = jnp.repeat(jnp.arange(rows, dtype=F32), GRID_W)
    col = jnp.tile(jnp.arange(GRID_W, dtype=F32), rows)
    ang = jnp.concatenate([row[:, None] * inv_freq, col[:, None] * inv_freq], axis=-1)
    cos, sin = jnp.cos(ang), jnp.sin(ang)
    one = jnp.ones((l, QK_NOPE), F32)
    zero = jnp.zeros((l, QK_NOPE), F32)
    tail = jnp.zeros((l, HEAD_PAD - QK_NOPE - QK_ROPE), F32)
    return (jnp.concatenate([one, cos, cos, tail], -1), jnp.concatenate([zero, -sin, sin, tail], -1))


ATT_TQ = 1024
ATT_RB = 1024
ATT_CK = 2048


def _row_max(s):
    mp = s[:, 0:LANES]
    for j in range(1, s.shape[1] // LANES):
        mp = jnp.maximum(mp, s[:, j * LANES:(j + 1) * LANES])
    return jnp.max(mp, axis=-1, keepdims=True)


def _attn_kernel(q_ref, kct_ref, vc_ref, *rest, n_chunks, ck, tq):
    if n_chunks:
        kt_ref, v_ref, o_ref = rest
    else:
        (o_ref,) = rest
    rb = min(ATT_RB, tq)
    chains = [(hh, r) for hh in range(2) for r in range(tq // rb)]
    lane = lax.broadcasted_iota(jnp.int32, (1, HEAD_PAD), 1)

    def q_rows(hh, r):
        return q_ref[0, hh, r * rb:(r + 1) * rb, :]

    state = []
    for hh, r in chains:
        s = _dot(q_rows(hh, r), kct_ref[0, hh])
        m = _row_max(s)
        p = jnp.exp2(s - m).astype(BF16)
        state += [m, _dot(p, vc_ref[0, hh])]

    if n_chunks:
        def body(c, state):
            off = pl.multiple_of(c * ck, ck)
            out = []
            for i, (hh, r) in enumerate(chains):
                m, acc = state[2 * i], state[2 * i + 1]
                s = _dot(q_rows(hh, r), kt_ref[0, hh, :, pl.ds(off, ck)])
                m_new = jnp.maximum(m, _row_max(s))
                alpha = jnp.exp2(m - m_new)
                p = jnp.exp2(s - m_new).astype(BF16)
                out += [m_new, acc * alpha + _dot(p, v_ref[0, hh, pl.ds(off, ck), :])]
            return tuple(out)
        state = lax.fori_loop(0, n_chunks, body, tuple(state), unroll=2 if n_chunks % 2 == 0 else 1)

    for r in range(tq // rb):
        outs = []
        for hh in range(2):
            acc = state[2 * chains.index((hh, r)) + 1]
            one_lane = V_DIM if hh == 0 else 0
            denom = jnp.sum(jnp.where(lane == one_lane, acc, 0.0), axis=-1, keepdims=True)
            outs.append(acc / denom)
        o_ref[0, r * rb:(r + 1) * rb, :] = jnp.where(lane < V_DIM, outs[0], outs[1]).astype(o_ref.dtype)


def attention(q, kct, vc, kt=None, v=None):
    b, hw, lq, _ = q.shape
    lc = vc.shape[2]
    tq = min(ATT_TQ, lq)
    ck = ATT_CK if kt is None else min(ATT_CK, kt.shape[3])
    n_chunks = 0 if kt is None else kt.shape[3] // ck
    in_specs = [pl.BlockSpec((1, 2, tq, HEAD_PAD), lambda bi, hp, i: (bi, hp, i, 0)),
                pl.BlockSpec((1, 2, HEAD_PAD, lc), lambda bi, hp, i: (bi, hp, 0, 0)),
                pl.BlockSpec((1, 2, lc, HEAD_PAD), lambda bi, hp, i: (bi, hp, 0, 0))]
    args = [q, kct, vc]
    if n_chunks:
        l = kt.shape[3]
        in_specs += [pl.BlockSpec((1, 2, HEAD_PAD, l), lambda bi, hp, i: (bi, hp, 0, 0),
                                  pipeline_mode=pl.Buffered(1)),
                     pl.BlockSpec((1, 2, l, HEAD_PAD), lambda bi, hp, i: (bi, hp, 0, 0),
                                  pipeline_mode=pl.Buffered(1))]
        args += [kt, v]
    return pl.pallas_call(
        functools.partial(_attn_kernel, n_chunks=n_chunks, ck=ck, tq=tq),
        grid=(b, hw // 2, lq // tq),
        in_specs=in_specs,
        out_specs=pl.BlockSpec((1, tq, HEAD_PAD), lambda bi, hp, i: (bi, i, hp)),
        out_shape=jax.ShapeDtypeStruct((b, lq, hw * V_DIM), BF16),
        compiler_params=_cp("parallel", "parallel", "arbitrary"),
        name="attention",
    )(*args)


def _halo_specs(tm, width, nblk):
    r = tm // POOL_HALO
    last = nblk * r - 1
    return [pl.BlockSpec((1, tm, width), lambda bi, i: (bi, i, 0)),
            pl.BlockSpec((1, POOL_HALO, width), lambda bi, i: (bi, jnp.maximum(i * r - 1, 0), 0)),
            pl.BlockSpec((1, POOL_HALO, width), lambda bi, i: (bi, jnp.minimum((i + 1) * r, last), 0))]


def _fill_padded(buf_ref, cur_ref, prev_ref, next_ref, tm):
    i = pl.program_id(1)
    first = i == 0
    last = i == pl.num_programs(1) - 1
    buf_ref[0:POOL_HALO] = jnp.where(first, 0.0, prev_ref[0])
    buf_ref[POOL_HALO:POOL_HALO + tm] = cur_ref[0]
    buf_ref[POOL_HALO + tm:] = jnp.where(last, 0.0, next_ref[0])


def _hyena_pre_kernel(u_ref, up_ref, un_ref, w_ref, b_ref, z_ref, x0_ref, buf_ref, *, tm):
    _fill_padded(buf_ref, u_ref, up_ref, un_ref, tm)
    w = w_ref[...]
    uc = (buf_ref[POOL_HALO - 1:POOL_HALO - 1 + tm] * w[0:1] + buf_ref[POOL_HALO:POOL_HALO + tm] * w[1:2]
          + buf_ref[POOL_HALO + 1:POOL_HALO + 1 + tm] * w[2:3] + b_ref[...])
    x0_ref[0] = uc[:, :HY_WIDTH]
    z_ref[0] = uc[:, HY_WIDTH:2 * HY_WIDTH] * uc[:, 2 * HY_WIDTH:]


def hyena_pre(u, sw, sb):
    b, l, wd = u.shape
    tm = min(512, l)
    nblk = l // tm
    return pl.pallas_call(
        functools.partial(_hyena_pre_kernel, tm=tm),
        grid=(b, nblk),
        in_specs=_halo_specs(tm, wd, nblk) + [pl.BlockSpec((3, wd), lambda bi, i: (0, 0)),
                                              pl.BlockSpec((1, wd), lambda bi, i: (0, 0))],
        out_specs=[pl.BlockSpec((1, tm, HY_WIDTH), lambda bi, i: (bi, i, 0))] * 2,
        out_shape=[jax.ShapeDtypeStruct((b, l, HY_WIDTH), F32)] * 2,
        scratch_shapes=[pltpu.VMEM((tm + 2 * POOL_HALO, wd), F32)],
        compiler_params=_cp("parallel", "parallel"),
        name="hyena_pre",
    )(u, u, u, sw, sb.reshape(1, wd))


def _filter_kernel(feat_ref, tdec_ref, w1_ref, b1_ref, f1_ref, w2_ref, b2_ref, f2_ref, w3_ref, dl_ref,
                   filt_ref, norm_ref, *, l, tr):
    i = pl.program_id(0)
    h = jnp.sin(f1_ref[...] * (_dot(feat_ref[...], w1_ref[...], precision=HIGHEST) + b1_ref[...]))
    h = jnp.sin(f2_ref[...] * (_dot(h, w2_ref[...], precision=HIGHEST) + b2_ref[...]))
    h = _dot(h, w3_ref[...], precision=HIGHEST)
    decay = jnp.exp(-tdec_ref[...] * jnp.abs(dl_ref[...]))
    n = i * tr + lax.broadcasted_iota(jnp.int32, (tr, 1), 0)
    f = jnp.where(n < l, h[:, :HY_WIDTH], h[:, HY_WIDTH:]) * decay
    f = jnp.where(n == l, 0.0, f)
    filt_ref[...] = f

    @pl.when(i == 0)
    def _():
        norm_ref[...] = jnp.zeros_like(norm_ref)
    norm_ref[...] += jnp.sum(jnp.abs(f), axis=0, keepdims=True)


def filter_tables(l):
    n = jnp.arange(2 * l)
    pos = jnp.where(n < l, n, 2 * l - n).astype(F32)[:, None]
    t = pos / max(l - 1, 1)
    w = 2.0 * math.pi * pos / l
    f = jnp.linspace(1e-4, FILT_BANDS - 1, FILT_BANDS, dtype=F32)[None, :]
    feat = jnp.concatenate([t, jnp.cos(f * w), -jnp.sin(f * w),
                            jnp.zeros((2 * l, LANES - FILT_EMB), F32)], axis=-1)
    return feat, t


def hyena_filter(l, tables, w1, b1, f1, w2, b2, f2, w3):
    feat, t = tables
    tr = min(512, 2 * l)
    hid = w2.shape[0]
    max_decay = math.log(DECAY_TARGET) / FAST_DECAY_PCT
    min_decay = math.log(DECAY_TARGET) / SLOW_DECAY_PCT
    deltas = jnp.linspace(min_decay, max_decay, HY_WIDTH, dtype=F32)[None, :]
    w1p = jnp.concatenate([w1, jnp.zeros((LANES - FILT_EMB, hid), F32)], 0)
    full = lambda a: pl.BlockSpec(a.shape, lambda i: (0,) * a.ndim)
    smalls = [w1p, b1.reshape(1, hid), f1.reshape(1, hid), w2, b2.reshape(1, hid), f2.reshape(1, hid), w3, deltas]
    return pl.pallas_call(
        functools.partial(_filter_kernel, l=l, tr=tr),
        grid=(2 * l // tr,),
        in_specs=[pl.BlockSpec((tr, LANES), lambda i: (i, 0)),
                  pl.BlockSpec((tr, 1), lambda i: (i, 0))] + [full(a) for a in smalls],
        out_specs=[pl.BlockSpec((tr, HY_WIDTH), lambda i: (i, 0)),
                   pl.BlockSpec((1, HY_WIDTH), lambda i: (0, 0))],
        out_shape=[jax.ShapeDtypeStruct((2 * l, HY_WIDTH), F32),
                   jax.ShapeDtypeStruct((1, HY_WIDTH), F32)],
        compiler_params=_cp("arbitrary"),
        name="hyena_filter",
    )(feat, t, *smalls)


def _dft_rows_kernel(f_ref, x_ref, o_ref):
    o_ref[0] = _dot(f_ref[...], x_ref[0].astype(BF16)).astype(o_ref.dtype)


def dft_rows(fmat, x, out_dtype):
    b, k, w = x.shape
    m = fmat.shape[0]
    tn = min(2048, w)
    return pl.pallas_call(
        _dft_rows_kernel,
        grid=(b, w // tn),
        in_specs=[pl.BlockSpec((m, k), lambda bi, j: (0, 0)),
                  pl.BlockSpec((1, k, tn), lambda bi, j: (bi, 0, j))],
        out_specs=pl.BlockSpec((1, m, tn), lambda bi, j: (bi, 0, j)),
        out_shape=jax.ShapeDtypeStruct((b, m, w), out_dtype),
        compiler_params=_cp("parallel", "parallel"),
        name="dft_rows",
    )(fmat, x)


SPEC_KB = 8


def _inner_dft(g, are, aim):
    n2 = are.shape[0]
    p1 = _dot(g, are)
    p2 = _dot(g, aim)
    return p1[:n2] - p2[n2:], p2[:n2] + p1[n2:]


def _spectrum_kernel(g_ref, a_ref, h_ref):
    for j in range(g_ref.shape[0]):
        h_ref[0, j], h_ref[1, j] = _inner_dft(g_ref[j], a_ref[0, 0, j], a_ref[0, 1, j])


def filter_spectrum(gs, a):
    _, _, n1, n2, c = a.shape
    kb = min(SPEC_KB, n1)
    return pl.pallas_call(
        _spectrum_kernel,
        grid=(n1 // kb,),
        in_specs=[pl.BlockSpec((kb, 2 * n2, n2), lambda k: (k, 0, 0)),
                  pl.BlockSpec((1, 2, kb, n2, c), lambda k: (0, 0, k, 0, 0))],
        out_specs=pl.BlockSpec((2, kb, n2, c), lambda k: (0, k, 0, 0)),
        out_shape=jax.ShapeDtypeStruct((2, n1, n2, c), F32),
        compiler_params=_cp("parallel"),
        name="filter_spectrum",
    )(gs, a)


def _spectral_kernel(g_ref, gt_ref, h_ref, a_ref, b_ref):
    n2 = a_ref.shape[3]
    for j in range(g_ref.shape[0]):
        xre, xim = _inner_dft(g_ref[j], a_ref[0, 0, j], a_ref[0, 1, j])
        hre, him = h_ref[0, j], h_ref[1, j]
        yre = (xre * hre - xim * him).astype(BF16)
        yim = (xre * him + xim * hre).astype(BF16)
        gt = gt_ref[j]
        q1 = _dot(gt, yre)
        q2 = _dot(gt, yim)
        b_ref[0, 0, j] = (q1[:n2] + q2[n2:]).astype(b_ref.dtype)
        b_ref[0, 1, j] = (q2[:n2] - q1[n2:]).astype(b_ref.dtype)


def spectral_multiply(gs, gts, hf, a):
    b, _, n1, n2, c = a.shape
    kb = min(SPEC_KB, n1)
    return pl.pallas_call(
        _spectral_kernel,
        grid=(n1 // kb, b),
        in_specs=[pl.BlockSpec((kb, 2 * n2, n2), lambda k, bi: (k, 0, 0)),
                  pl.BlockSpec((kb, 2 * n2, n2), lambda k, bi: (k, 0, 0)),
                  pl.BlockSpec((2, kb, n2, c), lambda k, bi: (0, k, 0, 0)),
                  pl.BlockSpec((1, 2, kb, n2, c), lambda k, bi: (bi, 0, k, 0, 0))],
        out_specs=pl.BlockSpec((1, 2, kb, n2, c), lambda k, bi: (bi, 0, k, 0, 0)),
        out_shape=jax.ShapeDtypeStruct(a.shape, BF16),
        compiler_params=_cp("parallel", "arbitrary"),
        name="spectral_multiply",
    )(gs, gts, hf, a)


def _idft_rows_kernel(f_ref, bm_ref, x0_ref, z_ref, inv_ref, bias_ref, o_ref, *, inv_n):
    y = _dot(f_ref[...], bm_ref[0]) * inv_n
    z = z_ref[0]
    o_ref[0] = (x0_ref[0] * (y * inv_ref[...] + bias_ref[...] * z)).astype(o_ref.dtype)


def idft_rows_gate(fmat, bm, x0, z, inv_norm, bias, n):
    b, k, w = bm.shape
    m = fmat.shape[0]
    tn = min(2048, w)
    col = lambda bi, j: (bi, 0, j)
    return pl.pallas_call(
        functools.partial(_idft_rows_kernel, inv_n=1.0 / n),
        grid=(b, w // tn),
        in_specs=[pl.BlockSpec((m, k), lambda bi, j: (0, 0)),
                  pl.BlockSpec((1, k, tn), col),
                  pl.BlockSpec((1, m, tn), col),
                  pl.BlockSpec((1, m, tn), col),
                  pl.BlockSpec((1, tn), lambda bi, j: (0, j)),
                  pl.BlockSpec((1, tn), lambda bi, j: (0, j))],
        out_specs=pl.BlockSpec((1, m, tn), col),
        out_shape=jax.ShapeDtypeStruct((b, m, w), BF16),
        compiler_params=_cp("parallel", "parallel"),
        name="idft_rows_gate",
    )(fmat, bm, x0, z, inv_norm, bias)


def dft_tables(l):
    n = 2 * l
    n2 = FFT_N2
    n1 = n // n2
    k1 = np.arange(n1)
    ang1 = 2.0 * np.pi * np.outer(k1, k1) / n1
    f1 = np.concatenate([np.cos(ang1), -np.sin(ang1)], 0)
    f3 = np.concatenate([np.cos(ang1), -np.sin(ang1)], 1)[: n1 // 2]
    kk = (jnp.arange(n1, dtype=jnp.int32)[:, None, None] + n1 * jnp.arange(n2, dtype=jnp.int32)[None, :, None])
    prod = (kk * jnp.arange(n2, dtype=jnp.int32)[None, None, :]) % n
    ang = prod.astype(F32) * (2.0 * math.pi / n)
    gre, gim = jnp.cos(ang), -jnp.sin(ang)
    gs = jnp.concatenate([gre, gim], 1).astype(BF16)
    gts = jnp.concatenate([gre.transpose(0, 2, 1), gim.transpose(0, 2, 1)], 1).astype(BF16)
    return dict(n=n, n1=n1, n2=n2, f1=jnp.asarray(f1, BF16), f3=jnp.asarray(f3, BF16), gs=gs, gts=gts)


def hyena_long_conv(z, x0, filt2, norm, bias, tabs):
    b, l, c = z.shape
    n, n1, n2 = tabs["n"], tabs["n1"], tabs["n2"]
    w = n2 * c
    fa = dft_rows(tabs["f1"], filt2.reshape(1, n1, w), BF16)
    hf = filter_spectrum(tabs["gs"], fa.reshape(1, 2, n1, n2, c))
    a = dft_rows(tabs["f1"][:, : n1 // 2], z.reshape(b, n1 // 2, w), BF16)
    bm = spectral_multiply(tabs["gs"], tabs["gts"], hf, a.reshape(b, 2, n1, n2, c))
    tile = lambda v: jnp.tile(v.reshape(1, c), (1, n2))
    y = idft_rows_gate(tabs["f3"], bm.reshape(b, 2 * n1, w), x0.reshape(b, n1 // 2, w),
                       z.reshape(b, n1 // 2, w), tile(1.0 / norm), tile(bias), n)
    return y.reshape(b, l, c)


def _small_conv_kernel(ff_ref, fi_ref, filt_ref, z_ref, x0_ref, inv_ref, bias_ref, o_ref, *, l):
    ff = ff_ref[...]
    hs = _dot(ff, filt_ref[...].astype(BF16))
    hre, him = hs[:2 * l], hs[2 * l:]
    z = z_ref[0]
    xs = _dot(ff[:, :l], z.astype(BF16))
    xre, xim = xs[:2 * l], xs[2 * l:]
    yre = (xre * hre - xim * him).astype(BF16)
    yim = (xre * him + xim * hre).astype(BF16)
    y = _dot(fi_ref[...], jnp.concatenate([yre, yim], 0)) * (0.5 / l)
    o_ref[0] = (x0_ref[0] * (y * inv_ref[...] + bias_ref[...] * z)).astype(o_ref.dtype)


def small_long_conv(z, x0, filt2, norm, bias):
    b, l, c = z.shape
    k = np.arange(2 * l)
    ang = 2.0 * np.pi * np.outer(k, k) / (2 * l)
    ff = jnp.asarray(np.concatenate([np.cos(ang), -np.sin(ang)], 0), BF16)
    fi = jnp.asarray(np.concatenate([np.cos(ang), -np.sin(ang)], 1)[:l], BF16)
    full = lambda a: pl.BlockSpec(a.shape, lambda bi: (0,) * a.ndim)
    row = pl.BlockSpec((1, l, c), lambda bi: (bi, 0, 0))
    inv = 1.0 / norm
    return pl.pallas_call(
        functools.partial(_small_conv_kernel, l=l),
        grid=(b,),
        in_specs=[full(ff), full(fi), full(filt2), row, row, full(inv), full(bias)],
        out_specs=row,
        out_shape=jax.ShapeDtypeStruct((b, l, c), BF16),
        compiler_params=_cp("parallel"),
        name="small_long_conv",
    )(ff, fi, filt2, z, x0, inv, bias)


def _pool_kernel(p_ref, pp_ref, pn_ref, w_ref, sc_ref, o_ref, buf_ref, *, tm, l):
    _fill_padded(buf_ref, p_ref, pp_ref, pn_ref, tm)
    t = pl.program_id(1) * tm + lax.broadcasted_iota(jnp.int32, (tm, LANES), 0)
    lane = lax.broadcasted_iota(jnp.int32, (tm, LANES), 1)
    first_group = lane < POOL_GROUP
    h0 = POOL_HALO

    def win(col, lo, hi):
        acc = buf_ref[h0 + lo:h0 + lo + tm, col]
        for j in range(lo + 1, hi):
            acc = acc + buf_ref[h0 + j:h0 + j + tm, col]
        return acc

    parts = []
    for half, (wa, wb) in enumerate(((2, 4), (8, 16))):
        col = slice(half * LANES, (half + 1) * LANES)
        sa = win(col, -wa // 2, wa // 2)
        sb = sa + win(col, -wb // 2, -wa // 2) + win(col, wa // 2, wb // 2)
        hw = jnp.where(first_group, wa // 2, wb // 2)
        cnt = jnp.minimum(t + hw, l) - jnp.maximum(t - hw, 0)
        mean = jnp.where(first_group, sa, sb) / cnt.astype(F32)
        parts.append(mean - buf_ref[h0:h0 + tm, col])
    d = jnp.concatenate(parts, axis=-1).astype(BF16)
    o_ref[0] = (_dot(d, w_ref[...]) * sc_ref[...]).astype(o_ref.dtype)


def pool_mix(p, pool_w, pool_scale):
    b, l, wd = p.shape
    tm = min(512, l)
    nblk = l // tm
    wblk = jax.scipy.linalg.block_diag(*[pool_w[g] for g in range(len(POOL_WINDOWS))]).astype(BF16)
    return pl.pallas_call(
        functools.partial(_pool_kernel, tm=tm, l=l),
        grid=(b, nblk),
        in_specs=_halo_specs(tm, wd, nblk) + [pl.BlockSpec((wd, wd), lambda bi, i: (0, 0)),
                                              pl.BlockSpec((1, wd), lambda bi, i: (0, 0))],
        out_specs=pl.BlockSpec((1, tm, wd), lambda bi, i: (bi, i, 0)),
        out_shape=jax.ShapeDtypeStruct((b, l, wd), BF16),
        scratch_shapes=[pltpu.VMEM((tm + 2 * POOL_HALO, wd), F32)],
        compiler_params=_cp("parallel", "parallel"),
        name="pool_mix",
    )(p, p, p, wblk, pool_scale.reshape(1, wd))


def _outproj_kernel(x_ref, hy_ref, pool_ref, att_ref, w_ref, gate_ref, o_ref):
    a, b2 = HY_WIDTH, HY_WIDTH + POOL_WIDTH
    y = (_dot(hy_ref[0], w_ref[0:a]) + _dot(pool_ref[0], w_ref[a:b2]) + _dot(att_ref[0], w_ref[b2:]))
    o_ref[0] = x_ref[0] + gate_ref[0] * y


def out_projection(x, y_hy, y_pool, y_att, w_out, gate):
    b, l, d = x.shape
    tm = min(512, l)
    row = lambda bi, i: (bi, i, 0)
    return pl.pallas_call(
        _outproj_kernel,
        grid=(b, l // tm),
        in_specs=[pl.BlockSpec((1, tm, d), row),
                  pl.BlockSpec((1, tm, y_hy.shape[2]), row),
                  pl.BlockSpec((1, tm, y_pool.shape[2]), row),
                  pl.BlockSpec((1, tm, y_att.shape[2]), row),
                  pl.BlockSpec(w_out.shape, lambda bi, i: (0, 0)),
                  pl.BlockSpec((1, 1, d), lambda bi, i: (bi, 0, 0))],
        out_specs=pl.BlockSpec((1, tm, d), row),
        out_shape=jax.ShapeDtypeStruct(x.shape, F32),
        compiler_params=_cp("parallel", "parallel"),
        name="out_projection",
    )(x, y_hy, y_pool, y_att, w_out, gate)


FF_CHUNK = 256
MOE_TILE = 1024
MOE_ROWS = (128, 256, 384)


def _swiglu_acc(h, wg_ref, wu_ref, wd_ref, lead):
    ff = wg_ref.shape[-1]
    acc = None
    for f0 in range(0, ff, FF_CHUNK):
        cs = slice(f0, min(f0 + FF_CHUNK, ff))
        g = _dot(h, wg_ref[lead + (slice(None), cs)])
        u = _dot(h, wu_ref[lead + (slice(None), cs)])
        a = (g * jax.nn.sigmoid(g) * u).astype(BF16)
        part = _dot(a, wd_ref[lead + (cs, slice(None))])
        acc = part if acc is None else acc + part
    return acc


def _ffn_kernel(x_ref, g_ref, sh_ref, sc_ref, gate_ref, wg_ref, wu_ref, wd_ref, o_ref):
    x = x_ref[0]
    h = _normmod(x, g_ref[...], sh_ref[0], sc_ref[0]).astype(BF16)
    o_ref[0] = x + gate_ref[0] * _swiglu_acc(h, wg_ref, wu_ref, wd_ref, ())


def ffn_dense(x, g, shift, scale, gate, wg, wu, wd):
    b, l, d = x.shape
    tm = min(512, l)
    row = lambda bi, i: (bi, i, 0)
    vec = lambda bi, i: (bi, 0, 0)
    wspec = lambda w: pl.BlockSpec(w.shape, lambda bi, i: (0, 0), pipeline_mode=pl.Buffered(1))
    return pl.pallas_call(
        _ffn_kernel,
        grid=(b, l // tm),
        in_specs=[pl.BlockSpec((1, tm, d), row),
                  pl.BlockSpec((1, d), lambda bi, i: (0, 0)),
                  pl.BlockSpec((1, 1, d), vec), pl.BlockSpec((1, 1, d), vec), pl.BlockSpec((1, 1, d), vec),
                  wspec(wg), wspec(wu), wspec(wd)],
        out_specs=pl.BlockSpec((1, tm, d), row),
        out_shape=jax.ShapeDtypeStruct(x.shape, F32),
        compiler_params=_cp("parallel", "parallel"),
        name="ffn_dense",
    )(x, g, shift, scale, gate, wg, wu, wd)


def _router_kernel(x_ref, g_ref, sh_ref, sc_ref, wr_ref, tri_ref, comb_ref, rank_ref):
    h = _normmod(x_ref[0], g_ref[...], sh_ref[0], sc_ref[0])
    logits = _dot(h, wr_ref[...], precision=HIGHEST)
    lane = lax.broadcasted_iota(jnp.int32, logits.shape, 1)
    neg = jnp.float32(-jnp.inf)
    lg = jnp.where(lane < N_EXPERTS, logits, neg)
    m1 = jnp.max(lg, axis=-1, keepdims=True)
    i1 = jnp.min(jnp.where(lg == m1, lane, LANES), axis=-1, keepdims=True)
    lg2 = jnp.where(lane == i1, neg, lg)
    m2 = jnp.max(lg2, axis=-1, keepdims=True)
    i2 = jnp.min(jnp.where(lg2 == m2, lane, LANES), axis=-1, keepdims=True)
    e = jnp.exp(m2 - m1)
    g1 = 1.0 / (1.0 + e)
    comb_ref[0] = jnp.where(lane == i1, g1, jnp.where(lane == i2, e * g1, 0.0))
    routed = (lane == i1) | (lane == i2)
    before = _dot(tri_ref[...], jnp.where(routed, 1.0, 0.0).astype(BF16))
    rank_ref[0] = jnp.where(routed, before, -1.0)


def moe_route(x, g, shift, scale, w_router):
    b, l, d = x.shape
    tm = min(MOE_TILE, l)
    wr = jnp.concatenate([w_router, jnp.zeros((d, LANES - N_EXPERTS), F32)], -1)
    tri = jnp.asarray(np.tril(np.ones((tm, tm), np.float32), -1), BF16)
    row = lambda bi, i: (bi, i, 0)
    vec = lambda bi, i: (bi, 0, 0)
    return pl.pallas_call(
        _router_kernel,
        grid=(b, l // tm),
        in_specs=[pl.BlockSpec((1, tm, d), row),
                  pl.BlockSpec((1, d), lambda bi, i: (0, 0)),
                  pl.BlockSpec((1, 1, d), vec), pl.BlockSpec((1, 1, d), vec),
                  pl.BlockSpec((d, LANES), lambda bi, i: (0, 0)),
                  pl.BlockSpec((tm, tm), lambda bi, i: (0, 0))],
        out_specs=[pl.BlockSpec((1, tm, LANES), row)] * 2,
        out_shape=[jax.ShapeDtypeStruct((b, l, LANES), F32)] * 2,
        compiler_params=_cp("parallel", "parallel"),
        name="moe_router",
    )(x, g, shift, scale, wr, tri)


def _moe_kernel(x_ref, g_ref, sh_ref, sc_ref, gate_ref, comb_ref, wg_ref, wu_ref, wd_ref, o_ref, acc_ref):
    e = pl.program_id(2)
    x = x_ref[0]
    h = _normmod(x, g_ref[...], sh_ref[0], sc_ref[0]).astype(BF16)
    comb = comb_ref[0]
    lane = lax.broadcasted_iota(jnp.int32, comb.shape, 1)
    ce = jnp.sum(jnp.where(lane == e, comb, 0.0), axis=-1, keepdims=True)
    y = ce * _swiglu_acc(h, wg_ref, wu_ref, wd_ref, (0,))

    @pl.when(e == 0)
    def _():
        acc_ref[...] = y

    @pl.when(e > 0)
    def _():
        acc_ref[...] += y

    @pl.when(e == N_EXPERTS - 1)
    def _():
        o_ref[0] = x + gate_ref[0] * acc_ref[...]


def moe_dense(x, g, shift, scale, gate, comb, wg, wu, wd):
    b, l, d = x.shape
    tm = min(512, l)
    ff = wg.shape[-1]
    row = lambda bi, i, e: (bi, i, 0)
    vec = lambda bi, i, e: (bi, 0, 0)
    return pl.pallas_call(
        _moe_kernel,
        grid=(b, l // tm, N_EXPERTS),
        in_specs=[pl.BlockSpec((1, tm, d), row),
                  pl.BlockSpec((1, d), lambda bi, i, e: (0, 0)),
                  pl.BlockSpec((1, 1, d), vec), pl.BlockSpec((1, 1, d), vec), pl.BlockSpec((1, 1, d), vec),
                  pl.BlockSpec((1, tm, LANES), row),
                  pl.BlockSpec((1, d, ff), lambda bi, i, e: (e, 0, 0)),
                  pl.BlockSpec((1, d, ff), lambda bi, i, e: (e, 0, 0)),
                  pl.BlockSpec((1, ff, d), lambda bi, i, e: (e, 0, 0))],
        out_specs=pl.BlockSpec((1, tm, d), row),
        out_shape=jax.ShapeDtypeStruct(x.shape, F32),
        scratch_shapes=[pltpu.VMEM((tm, d), F32)],
        compiler_params=_cp("parallel", "parallel", "arbitrary"),
        name="moe_dense",
    )(x, g, shift, scale, gate, comb, wg, wu, wd)


def _moe_sparse_kernel(cnt_ref, x_ref, g_ref, sh_ref, sc_ref, gate_ref, comb_ref, rank_ref, rankt_ref,
                       wg_ref, wu_ref, wd_ref, o_ref, h_scr, *, t, sizes):
    i, e = pl.program_id(0), pl.program_id(1)
    cnt = cnt_ref[i * N_EXPERTS + e]

    @pl.when(e == 0)
    def _():
        x = x_ref[0]
        h_scr[...] = _normmod(x, g_ref[...], sh_ref[0], sc_ref[0]).astype(BF16)
        o_ref[0] = x

    rrow = rankt_ref[0, pl.ds(e, 1), :]
    lane = lax.broadcasted_iota(jnp.int32, (t, LANES), 1)
    rcol = jnp.sum(jnp.where(lane == e, rank_ref[0], 0.0), axis=-1, keepdims=True)
    ccol = jnp.sum(jnp.where(lane == e, comb_ref[0], 0.0), axis=-1, keepdims=True)

    def block(base, r):
        rid = (base + lax.broadcasted_iota(jnp.int32, (r, 1), 0)).astype(F32)
        onehot = jnp.where(rrow == rid, 1.0, 0.0).astype(BF16)
        xg = _dot(onehot, h_scr[...]).astype(BF16)
        y = _swiglu_acc(xg, wg_ref, wu_ref, wd_ref, (0,)).astype(BF16)
        cid = (base + lax.broadcasted_iota(jnp.int32, (1, r), 1)).astype(F32)
        onehot_t = jnp.where(rcol == cid, 1.0, 0.0).astype(BF16)
        o_ref[0] += (gate_ref[0] * ccol) * _dot(onehot_t, y)

    big = sizes[-1]
    nfull = cnt // big

    def full(k, carry):
        block(k * big, big)
        return carry
    lax.fori_loop(0, nfull, full, 0)
    base = nfull * big
    rem = cnt - base
    for lo, r in zip((0,) + sizes[:-1], sizes):
        @pl.when((rem > lo) & (rem <= r))
        def _(r=r):
            block(base, r)


def moe_sparse(x, g, shift, scale, gate, comb, rank, wg, wu, wd):
    b, l, d = x.shape
    t = MOE_TILE
    tpb = l // t
    n_tiles = b * tpb
    ff = wg.shape[-1]
    rk = rank[..., :N_EXPERTS].reshape(n_tiles, t, N_EXPERTS)
    rank_t = rk.transpose(0, 2, 1)
    cnt = jnp.sum(rk >= 0, axis=1).astype(jnp.int32).reshape(n_tiles * N_EXPERTS)
    row = lambda i, e, c: (i // tpb, i % tpb, 0)
    vec = lambda i, e, c: (i // tpb, 0, 0)
    grid_spec = pltpu.PrefetchScalarGridSpec(
        num_scalar_prefetch=1,
        grid=(n_tiles, N_EXPERTS),
        in_specs=[pl.BlockSpec((1, t, d), row, pipeline_mode=pl.Buffered(1)),
                  pl.BlockSpec((1, d), lambda i, e, c: (0, 0)),
                  pl.BlockSpec((1, 1, d), vec), pl.BlockSpec((1, 1, d), vec), pl.BlockSpec((1, 1, d), vec),
                  pl.BlockSpec((1, t, LANES), row, pipeline_mode=pl.Buffered(1)),
                  pl.BlockSpec((1, t, LANES), row, pipeline_mode=pl.Buffered(1)),
                  pl.BlockSpec((1, N_EXPERTS, t), lambda i, e, c: (i, 0, 0)),
                  pl.BlockSpec((1, d, ff), lambda i, e, c: (e, 0, 0)),
                  pl.BlockSpec((1, d, ff), lambda i, e, c: (e, 0, 0)),
                  pl.BlockSpec((1, ff, d), lambda i, e, c: (e, 0, 0))],
        out_specs=pl.BlockSpec((1, t, d), row),
        scratch_shapes=[pltpu.VMEM((t, d), BF16)],
    )
    return pl.pallas_call(
        functools.partial(_moe_sparse_kernel, t=t, sizes=MOE_ROWS),
        grid_spec=grid_spec,
        out_shape=jax.ShapeDtypeStruct(x.shape, F32),
        compiler_params=_cp("parallel", "arbitrary"),
        name="moe_sparse",
    )(cnt, x, g, shift, scale, gate, comb, rank, rank_t, wg, wu, wd)


def _final_kernel(x_ref, g_ref, o_ref):
    x = x_ref[0]
    o_ref[0] = x * lax.rsqrt(jnp.mean(x * x, axis=-1, keepdims=True) + EPS) * g_ref[...]


def final_norm(x, g):
    b, l, d = x.shape
    tm = min(1024, l)
    return pl.pallas_call(
        _final_kernel,
        grid=(b, l // tm),
        in_specs=[pl.BlockSpec((1, tm, d), lambda bi, i: (bi, i, 0)),
                  pl.BlockSpec((1, d), lambda bi, i: (0, 0))],
        out_specs=pl.BlockSpec((1, tm, d), lambda bi, i: (bi, i, 0)),
        out_shape=jax.ShapeDtypeStruct(x.shape, F32),
        compiler_params=_cp("parallel", "parallel"),
        name="final_norm",
    )(x, g.reshape(1, d))


def kernel(x, c, ctx, c_ctx, norm1_g, norm2_g, w_ada, b_ada, w_in, hy_short_w, hy_short_b, filt_w1, filt_b1, filt_freq1, filt_w2, filt_b2, filt_freq2, filt_w3, hy_bias, pool_w, pool_scale, q_norm_g, w_uq, kv_norm_g, w_ukv, w_out, ffn_wg, ffn_wu, ffn_wd, moe_router, moe_wg, moe_wu, moe_wd, final_g):
    b, l, d = x.shape
    lc = ctx.shape[1]
    depth = w_ada.shape[0]

    cc = jnp.zeros((8, d), F32).at[:b].set(c).at[b].set(c_ctx)
    mods = ada_vectors(cc, w_ada, b_ada)

    rope_l = rope_tables(l, True)
    rope_c = rope_tables(lc, False)
    ftab_l, ftab_c = filter_tables(l), filter_tables(lc)
    dtabs = dft_tables(l)

    xl, xc = x, ctx
    for layer in range(depth):
        last = layer == depth - 1
        mod = mods[layer].reshape(8, 6, 1, d)
        mod_l = [mod[:b, j] for j in range(6)]
        mod_c = [jnp.broadcast_to(mod[b:b + 1, j], (b, 1, d)) for j in range(6)]
        g1 = norm1_g[layer].reshape(1, d)
        g2 = norm2_g[layer].reshape(1, d)
        w_in_l = w_in[layer].astype(BF16)
        w_out_l = w_out[layer].astype(BF16)
        mw = mla_weights(q_norm_g[layer], w_uq[layer], kv_norm_g[layer], w_ukv[layer])
        fw = (filt_w1[layer], filt_b1[layer], filt_freq1[layer], filt_w2[layer], filt_b2[layer],
              filt_freq2[layer], filt_w3[layer])
        bias = hy_bias[layer].reshape(1, HY_WIDTH)

        def channel_mix(xs, md):
            i = layer // 2
            if layer % 2 == 0:
                return ffn_dense(xs, g2, md[3], md[4], md[5], ffn_wg[i].astype(BF16),
                                 ffn_wu[i].astype(BF16), ffn_wd[i].astype(BF16))
            comb, rank = moe_route(xs, g2, md[3], md[4], moe_router[i])
            ew = (moe_wg[i].astype(BF16), moe_wu[i].astype(BF16), moe_wd[i].astype(BF16))
            if xs.shape[1] % MOE_TILE:
                return moe_dense(xs, g2, md[3], md[4], md[5], comb, *ew)
            return moe_sparse(xs, g2, md[3], md[4], md[5], comb, rank, *ew)

        hy_l, pool_l, q_l, kt_l, v_l = in_projection(xl, g1, mod_l[0], mod_l[1], w_in_l, *rope_l, mw)
        hy_c, pool_c, q_c, kt_c, v_c = in_projection(xc, g1, mod_c[0], mod_c[1], w_in_l, *rope_c, mw)

        att_l = attention(q_l, kt_c, v_c, kt_l, v_l)
        z_l, x0_l = hyena_pre(hy_l, hy_short_w[layer], hy_short_b[layer])
        filt_l, nrm_l = hyena_filter(l, ftab_l, *fw)
        yhy_l = hyena_long_conv(z_l, x0_l, filt_l, nrm_l, bias, dtabs)
        ypool_l = pool_mix(pool_l, pool_w[layer], pool_scale[layer])
        xl = out_projection(xl, yhy_l, ypool_l, att_l, w_out_l, mod_l[2])

        if not last:
            att_c = attention(q_c, kt_c, v_c)
            z_c, x0_c = hyena_pre(hy_c, hy_short_w[layer], hy_short_b[layer])
            filt_c, nrm_c = hyena_filter(lc, ftab_c, *fw)
            yhy_c = small_long_conv(z_c, x0_c, filt_c, nrm_c, bias)
            ypool_c = pool_mix(pool_c, pool_w[layer], pool_scale[layer])
            xc = out_projection(xc, yhy_c, ypool_c, att_c, w_out_l, mod_c[2])
            xc = channel_mix(xc, mod_c)

        xl = channel_mix(xl, mod_l)
    return final_norm(xl, final_g)
```

```python
import functools
import math

import jax
import jax.numpy as jnp
import numpy as np
from jax import lax
from jax.experimental import pallas as pl
from jax.experimental.pallas import tpu as pltpu

F32 = jnp.float32
BF16 = jnp.bfloat16
HIGHEST = lax.Precision.HIGHEST

EPS = 1e-6
GRID_W = 64
HY_WIDTH = 256
POOL_WINDOWS = (2, 4, 8, 16)
POOL_WIDTH = 256
POOL_GROUP = 64
POOL_HALO = 8
MLA_HEADS = 8
QK_NOPE = 64
QK_ROPE = 32
V_DIM = 64
Q_LORA = 384
KV_LORA = 256
HEAD_PAD = 128
SM_SCALE = (QK_NOPE + QK_ROPE) ** -0.5
ROPE_THETA = 10000.0
FILT_EMB = 33
FILT_BANDS = 16
DECAY_TARGET = 1e-2
FAST_DECAY_PCT = 0.3
SLOW_DECAY_PCT = 1.5
N_EXPERTS = 8
LANES = 128
FFT_N2 = 128
VMEM_LIMIT = 56 * 2 ** 20


def _cp(*sem):
    return pltpu.CompilerParams(dimension_semantics=sem, vmem_limit_bytes=VMEM_LIMIT)


def _dot(a, b, **kw):
    return jnp.dot(a, b, preferred_element_type=F32, **kw)


def _normmod(x, g, shift, scale):
    ms = jnp.mean(x * x, axis=-1, keepdims=True)
    return x * lax.rsqrt(ms + EPS) * (g * (1.0 + scale)) + shift


def _ada_kernel(c_ref, w_ref, b_ref, o_ref):
    c = c_ref[...]
    s = c * jax.nn.sigmoid(c)
    o_ref[0] = _dot(s, w_ref[0], precision=HIGHEST) + b_ref[0]


def ada_vectors(cc, w_ada, b_ada):
    depth, d, n = w_ada.shape
    tn = 1536
    return pl.pallas_call(
        _ada_kernel,
        grid=(depth, n // tn),
        in_specs=[pl.BlockSpec((8, d), lambda l, j: (0, 0)),
                  pl.BlockSpec((1, d, tn), lambda l, j: (l, 0, j)),
                  pl.BlockSpec((1, 1, tn), lambda l, j: (l, 0, j))],
        out_specs=pl.BlockSpec((1, 8, tn), lambda l, j: (l, 0, j)),
        out_shape=jax.ShapeDtypeStruct((depth, 8, n), F32),
        compiler_params=_cp("parallel", "parallel"),
        name="ada_vectors",
    )(cc, w_ada, b_ada.reshape(depth, 1, n))


IN_HY, IN_POOL, IN_CQ, IN_CKV, IN_KPE = 768, 256, 384, 256, 32
IN_OFFS = (0, 768, 1024, 1408, 1664, 1696)


def _inproj_kernel(x_ref, g_ref, sh_ref, sc_ref, w_ref, c_ref, s_ref, wqa_ref, wqb_ref, wk_ref, wv_ref,
                   pa_ref, pb_ref, hy_ref, pool_ref, q_ref, kt_ref, v_ref):
    h = _normmod(x_ref[0], g_ref[...], sh_ref[0], sc_ref[0]).astype(BF16)
    y = _dot(h, w_ref[...])
    o = IN_OFFS
    hy_ref[0] = y[:, o[0]:o[1]]
    pool_ref[0] = y[:, o[1]:o[2]]
    cq, ckv = y[:, o[2]:o[3]], y[:, o[3]:o[4]]
    kpe = y[:, o[4]:o[5]].astype(BF16)
    cqn = (cq * lax.rsqrt(jnp.mean(cq * cq, axis=-1, keepdims=True) + EPS)).astype(BF16)
    ckvn = (ckv * lax.rsqrt(jnp.mean(ckv * ckv, axis=-1, keepdims=True) + EPS)).astype(BF16)
    cs, sn = c_ref[...], s_ref[...]
    qa = _dot(cqn, wqa_ref[...])
    qb = _dot(cqn, wqb_ref[...])
    kn = _dot(ckvn, wk_ref[...])
    vv = _dot(ckvn, wv_ref[...])
    kpa = _dot(kpe, pa_ref[...])
    kpb = _dot(kpe, pb_ref[...])
    kpe_rot = kpa * cs + kpb * sn
    lane = lax.broadcasted_iota(jnp.int32, (1, HEAD_PAD), 1)
    for hd in range(MLA_HEADS):
        sl = slice(hd * HEAD_PAD, (hd + 1) * HEAD_PAD)
        q_ref[0, hd] = (qa[:, sl] * cs + qb[:, sl] * sn).astype(BF16)
        kt_ref[0, hd] = (kn[:, sl] + kpe_rot).T.astype(BF16)
        one_lane = V_DIM if hd % 2 == 0 else 0
        v_ref[0, hd] = jnp.where(lane == one_lane, 1.0, vv[:, sl]).astype(BF16)


def in_projection(x, g, shift, scale, w_in, ctab, stab, wts):
    b, l, d = x.shape
    tm = min(512, l)
    hw = MLA_HEADS
    row = lambda bi, i: (bi, i, 0)
    vec = lambda bi, i: (bi, 0, 0)
    full = lambda a: pl.BlockSpec(a.shape, lambda bi, i: (0,) * a.ndim)
    return pl.pallas_call(
        _inproj_kernel,
        grid=(b, l // tm),
        in_specs=[pl.BlockSpec((1, tm, d), row),
                  pl.BlockSpec((1, d), lambda bi, i: (0, 0)),
                  pl.BlockSpec((1, 1, d), vec),
                  pl.BlockSpec((1, 1, d), vec),
                  full(w_in),
                  pl.BlockSpec((tm, HEAD_PAD), lambda bi, i: (i, 0)),
                  pl.BlockSpec((tm, HEAD_PAD), lambda bi, i: (i, 0))] + [full(w) for w in wts],
        out_specs=[pl.BlockSpec((1, tm, IN_HY), row),
                   pl.BlockSpec((1, tm, IN_POOL), row),
                   pl.BlockSpec((1, hw, tm, HEAD_PAD), lambda bi, i: (bi, 0, i, 0)),
                   pl.BlockSpec((1, hw, HEAD_PAD, tm), lambda bi, i: (bi, 0, 0, i)),
                   pl.BlockSpec((1, hw, tm, HEAD_PAD), lambda bi, i: (bi, 0, i, 0))],
        out_shape=[jax.ShapeDtypeStruct((b, l, IN_HY), F32),
                   jax.ShapeDtypeStruct((b, l, IN_POOL), F32),
                   jax.ShapeDtypeStruct((b, hw, l, HEAD_PAD), BF16),
                   jax.ShapeDtypeStruct((b, hw, HEAD_PAD, l), BF16),
                   jax.ShapeDtypeStruct((b, hw, l, HEAD_PAD), BF16)],
        compiler_params=_cp("parallel", "parallel"),
        name="in_projection",
    )(x, g, shift, scale, w_in, ctab, stab, *wts)


def mla_weights(q_norm_g, w_uq, kv_norm_g, w_ukv):
    hw, dq = MLA_HEADS, QK_NOPE + QK_ROPE
    half = QK_ROPE // 2
    wq = (w_uq * q_norm_g[:, None] * (SM_SCALE * math.log2(math.e))).reshape(Q_LORA, hw, dq)
    pad = jnp.zeros((Q_LORA, hw, HEAD_PAD - dq), F32)
    wqa = jnp.concatenate([wq, pad], -1)
    swap = jnp.concatenate([wq[..., QK_NOPE + half:], wq[..., QK_NOPE:QK_NOPE + half]], -1)
    wqb = jnp.concatenate([jnp.zeros((Q_LORA, hw, QK_NOPE), F32), swap, pad], -1)
    wkv = (w_ukv * kv_norm_g[:, None]).reshape(KV_LORA, hw, QK_NOPE + V_DIM)
    z64 = jnp.zeros((KV_LORA, hw, 64), F32)
    wk = jnp.concatenate([wkv[..., :QK_NOPE], z64], -1)
    v = wkv[..., QK_NOPE:]
    wv = jnp.where((jnp.arange(hw) % 2 == 0)[None, :, None],
                   jnp.concatenate([v, z64], -1), jnp.concatenate([z64, v], -1))
    eye = np.eye(QK_ROPE, dtype=np.float32)
    pa = np.zeros((QK_ROPE, HEAD_PAD), np.float32)
    pa[:, QK_NOPE:QK_NOPE + QK_ROPE] = eye
    pb = np.zeros((QK_ROPE, HEAD_PAD), np.float32)
    pb[:, QK_NOPE:QK_NOPE + QK_ROPE] = np.concatenate([eye[:, half:], eye[:, :half]], 1)
    flat = lambda w: w.reshape(w.shape[0], hw * HEAD_PAD).astype(BF16)
    return (flat(wqa), flat(wqb), flat(wk), flat(wv), jnp.asarray(pa, BF16), jnp.asarray(pb, BF16))


def rope_tables(l, with_rope):
    ctab = np.zeros((l, HEAD_PAD), np.float32)
    stab = np.zeros((l, HEAD_PAD), np.float32)
    ctab[:, :QK_NOPE + QK_ROPE] = 1.0
    if not with_rope:
        return jnp.asarray(ctab), jnp.asarray(stab)
    n_freq = QK_ROPE // 4
    inv_freq = 1.0 / (ROPE_THETA ** (jnp.arange(n_freq, dtype=F32) / n_freq))
    rows = l // GRID_W
    row = jnp.repeat(jnp.arange(rows, dtype=F32), GRID_W)
    col = jnp.tile(jnp.arange(GRID_W, dtype=F32), rows)
    ang = jnp.concatenate([row[:, None] * inv_freq, col[:, None] * inv_freq], axis=-1)
    cos, sin = jnp.cos(ang), jnp.sin(ang)
    one = jnp.ones((l, QK_NOPE), F32)
    zero = jnp.zeros((l, QK_NOPE), F32)
    tail = jnp.zeros((l, HEAD_PAD - QK_NOPE - QK_ROPE), F32)
    return (jnp.concatenate([one, cos, cos, tail], -1), jnp.concatenate([zero, -sin, sin, tail], -1))


ATT_TQ = 1024
ATT_RB = 1024
ATT_CK = 2048


def _row_max(s):
    mp = s[:, 0:LANES]
    for j in range(1, s.shape[1] // LANES):
        mp = jnp.maximum(mp, s[:, j * LANES:(j + 1) * LANES])
    return jnp.max(mp, axis=-1, keepdims=True)


def _attn_kernel(q_ref, kct_ref, vc_ref, *rest, n_chunks, ck, tq):
    if n_chunks:
        kt_ref, v_ref, o_ref = rest
    else:
        (o_ref,) = rest
    rb = min(ATT_RB, tq)
    chains = [(hh, r) for hh in range(2) for r in range(tq // rb)]
    lane = lax.broadcasted_iota(jnp.int32, (1, HEAD_PAD), 1)

    def q_rows(hh, r):
        return q_ref[0, hh, r * rb:(r + 1) * rb, :]

    state = []
    for hh, r in chains:
        s = _dot(q_rows(hh, r), kct_ref[0, hh])
        m = _row_max(s)
        p = jnp.exp2(s - m).astype(BF16)
        state += [m, _dot(p, vc_ref[0, hh])]

    if n_chunks:
        def body(c, state):
            off = pl.multiple_of(c * ck, ck)
            out = []
            for i, (hh, r) in enumerate(chains):
                m, acc = state[2 * i], state[2 * i + 1]
                s = _dot(q_rows(hh, r), kt_ref[0, hh, :, pl.ds(off, ck)])
                m_new = jnp.maximum(m, _row_max(s))
                alpha = jnp.exp2(m - m_new)
                p = jnp.exp2(s - m_new).astype(BF16)
                out += [m_new, acc * alpha + _dot(p, v_ref[0, hh, pl.ds(off, ck), :])]
            return tuple(out)
        state = lax.fori_loop(0, n_chunks, body, tuple(state), unroll=2 if n_chunks % 2 == 0 else 1)

    for r in range(tq // rb):
        outs = []
        for hh in range(2):
            acc = state[2 * chains.index((hh, r)) + 1]
            one_lane = V_DIM if hh == 0 else 0
            denom = jnp.sum(jnp.where(lane == one_lane, acc, 0.0), axis=-1, keepdims=True)
            outs.append(acc / denom)
        o_ref[0, r * rb:(r + 1) * rb, :] = jnp.where(lane < V_DIM, outs[0], outs[1]).astype(o_ref.dtype)


def attention(q, kct, vc, kt=None, v=None):
    b, hw, lq, _ = q.shape
    lc = vc.shape[2]
    tq = min(ATT_TQ, lq)
    ck = ATT_CK if kt is None else min(ATT_CK, kt.shape[3])
    n_chunks = 0 if kt is None else kt.shape[3] // ck
    in_specs = [pl.BlockSpec((1, 2, tq, HEAD_PAD), lambda bi, hp, i: (bi, hp, i, 0)),
                pl.BlockSpec((1, 2, HEAD_PAD, lc), lambda bi, hp, i: (bi, hp, 0, 0)),
                pl.BlockSpec((1, 2, lc, HEAD_PAD), lambda bi, hp, i: (bi, hp, 0, 0))]
    args = [q, kct, vc]
    if n_chunks:
        l = kt.shape[3]
        in_specs += [pl.BlockSpec((1, 2, HEAD_PAD, l), lambda bi, hp, i: (bi, hp, 0, 0),
                                  pipeline_mode=pl.Buffered(1)),
                     pl.BlockSpec((1, 2, l, HEAD_PAD), lambda bi, hp, i: (bi, hp, 0, 0),
                                  pipeline_mode=pl.Buffered(1))]
        args += [kt, v]
    return pl.pallas_call(
        functools.partial(_attn_kernel, n_chunks=n_chunks, ck=ck, tq=tq),
        grid=(b, hw // 2, lq // tq),
        in_specs=in_specs,
        out_specs=pl.BlockSpec((1, tq, HEAD_PAD), lambda bi, hp, i: (bi, i, hp)),
        out_shape=jax.ShapeDtypeStruct((b, lq, hw * V_DIM), BF16),
        compiler_params=_cp("parallel", "parallel", "arbitrary"),
        name="attention",
    )(*args)


def _halo_specs(tm, width, nblk):
    r = tm // POOL_HALO
    last = nblk * r - 1
    return [pl.BlockSpec((1, tm, width), lambda bi, i: (bi, i, 0)),
            pl.BlockSpec((1, POOL_HALO, width), lambda bi, i: (bi, jnp.maximum(i * r - 1, 0), 0)),
            pl.BlockSpec((1, POOL_HALO, width), lambda bi, i: (bi, jnp.minimum((i + 1) * r, last), 0))]


def _fill_padded(buf_ref, cur_ref, prev_ref, next_ref, tm):
    i = pl.program_id(1)
    first = i == 0
    last = i == pl.num_programs(1) - 1
    buf_ref[0:POOL_HALO] = jnp.where(first, 0.0, prev_ref[0])
    buf_ref[POOL_HALO:POOL_HALO + tm] = cur_ref[0]
    buf_ref[POOL_HALO + tm:] = jnp.where(last, 0.0, next_ref[0])


def _hyena_pre_kernel(u_ref, up_ref, un_ref, w_ref, b_ref, z_ref, x0_ref, buf_ref, *, tm):
    _fill_padded(buf_ref, u_ref, up_ref, un_ref, tm)
    w = w_ref[...]
    uc = (buf_ref[POOL_HALO - 1:POOL_HALO - 1 + tm] * w[0:1] + buf_ref[POOL_HALO:POOL_HALO + tm] * w[1:2]
          + buf_ref[POOL_HALO + 1:POOL_HALO + 1 + tm] * w[2:3] + b_ref[...])
    x0_ref[0] = uc[:, :HY_WIDTH]
    z_ref[0] = uc[:, HY_WIDTH:2 * HY_WIDTH] * uc[:, 2 * HY_WIDTH:]


def hyena_pre(u, sw, sb):
    b, l, wd = u.shape
    tm = min(512, l)
    nblk = l // tm
    return pl.pallas_call(
        functools.partial(_hyena_pre_kernel, tm=tm),
        grid=(b, nblk),
        in_specs=_halo_specs(tm, wd, nblk) + [pl.BlockSpec((3, wd), lambda bi, i: (0, 0)),
                                              pl.BlockSpec((1, wd), lambda bi, i: (0, 0))],
        out_specs=[pl.BlockSpec((1, tm, HY_WIDTH), lambda bi, i: (bi, i, 0))] * 2,
        out_shape=[jax.ShapeDtypeStruct((b, l, HY_WIDTH), F32)] * 2,
        scratch_shapes=[pltpu.VMEM((tm + 2 * POOL_HALO, wd), F32)],
        compiler_params=_cp("parallel", "parallel"),
        name="hyena_pre",
    )(u, u, u, sw, sb.reshape(1, wd))


def _filter_kernel(feat_ref, tdec_ref, w1_ref, b1_ref, f1_ref, w2_ref, b2_ref, f2_ref, w3_ref, dl_ref,
                   filt_ref, norm_ref, *, l, tr):
    i = pl.program_id(0)
    h = jnp.sin(f1_ref[...] * (_dot(feat_ref[...], w1_ref[...], precision=HIGHEST) + b1_ref[...]))
    h = jnp.sin(f2_ref[...] * (_dot(h, w2_ref[...], precision=HIGHEST) + b2_ref[...]))
    h = _dot(h, w3_ref[...], precision=HIGHEST)
    decay = jnp.exp(-tdec_ref[...] * jnp.abs(dl_ref[...]))
    n = i * tr + lax.broadcasted_iota(jnp.int32, (tr, 1), 0)
    f = jnp.where(n < l, h[:, :HY_WIDTH], h[:, HY_WIDTH:]) * decay
    f = jnp.where(n == l, 0.0, f)
    filt_ref[...] = f

    @pl.when(i == 0)
    def _():
        norm_ref[...] = jnp.zeros_like(norm_ref)
    norm_ref[...] += jnp.sum(jnp.abs(f), axis=0, keepdims=True)


def filter_tables(l):
    n = jnp.arange(2 * l)
    pos = jnp.where(n < l, n, 2 * l - n).astype(F32)[:, None]
    t = pos / max(l - 1, 1)
    w = 2.0 * math.pi * pos / l
    f = jnp.linspace(1e-4, FILT_BANDS - 1, FILT_BANDS, dtype=F32)[None, :]
    feat = jnp.concatenate([t, jnp.cos(f * w), -jnp.sin(f * w),
                            jnp.zeros((2 * l, LANES - FILT_EMB), F32)], axis=-1)
    return feat, t


def hyena_filter(l, tables, w1, b1, f1, w2, b2, f2, w3):
    feat, t = tables
    tr = min(512, 2 * l)
    hid = w2.shape[0]
    max_decay = math.log(DECAY_TARGET) / FAST_DECAY_PCT
    min_decay = math.log(DECAY_TARGET) / SLOW_DECAY_PCT
    deltas = jnp.linspace(min_decay, max_decay, HY_WIDTH, dtype=F32)[None, :]
    w1p = jnp.concatenate([w1, jnp.zeros((LANES - FILT_EMB, hid), F32)], 0)
    full = lambda a: pl.BlockSpec(a.shape, lambda i: (0,) * a.ndim)
    smalls = [w1p, b1.reshape(1, hid), f1.reshape(1, hid), w2, b2.reshape(1, hid), f2.reshape(1, hid), w3, deltas]
    return pl.pallas_call(
        functools.partial(_filter_kernel, l=l, tr=tr),
        grid=(2 * l // tr,),
        in_specs=[pl.BlockSpec((tr, LANES), lambda i: (i, 0)),
                  pl.BlockSpec((tr, 1), lambda i: (i, 0))] + [full(a) for a in smalls],
        out_specs=[pl.BlockSpec((tr, HY_WIDTH), lambda i: (i, 0)),
                   pl.BlockSpec((1, HY_WIDTH), lambda i: (0, 0))],
        out_shape=[jax.ShapeDtypeStruct((2 * l, HY_WIDTH), F32),
                   jax.ShapeDtypeStruct((1, HY_WIDTH), F32)],
        compiler_params=_cp("arbitrary"),
        name="hyena_filter",
    )(feat, t, *smalls)


def _dft_rows_kernel(f_ref, x_ref, o_ref):
    o_ref[0] = _dot(f_ref[...], x_ref[0].astype(BF16)).astype(o_ref.dtype)


def dft_rows(fmat, x, out_dtype):
    b, k, w = x.shape
    m = fmat.shape[0]
    tn = min(2048, w)
    return pl.pallas_call(
        _dft_rows_kernel,
        grid=(b, w // tn),
        in_specs=[pl.BlockSpec((m, k), lambda bi, j: (0, 0)),
                  pl.BlockSpec((1, k, tn), lambda bi, j: (bi, 0, j))],
        out_specs=pl.BlockSpec((1, m, tn), lambda bi, j: (bi, 0, j)),
        out_shape=jax.ShapeDtypeStruct((b, m, w), out_dtype),
        compiler_params=_cp("parallel", "parallel"),
        name="dft_rows",
    )(fmat, x)


SPEC_KB = 8


def _inner_dft(g, are, aim):
    n2 = are.shape[0]
    p1 = _dot(g, are)
    p2 = _dot(g, aim)
    return p1[:n2] - p2[n2:], p2[:n2] + p1[n2:]


def _spectrum_kernel(g_ref, a_ref, h_ref):
    for j in range(g_ref.shape[0]):
        h_ref[0, j], h_ref[1, j] = _inner_dft(g_ref[j], a_ref[0, 0, j], a_ref[0, 1, j])


def filter_spectrum(gs, a):
    _, _, n1, n2, c = a.shape
    kb = min(SPEC_KB, n1)
    return pl.pallas_call(
        _spectrum_kernel,
        grid=(n1 // kb,),
        in_specs=[pl.BlockSpec((kb, 2 * n2, n2), lambda k: (k, 0, 0)),
                  pl.BlockSpec((1, 2, kb, n2, c), lambda k: (0, 0, k, 0, 0))],
        out_specs=pl.BlockSpec((2, kb, n2, c), lambda k: (0, k, 0, 0)),
        out_shape=jax.ShapeDtypeStruct((2, n1, n2, c), F32),
        compiler_params=_cp("parallel"),
        name="filter_spectrum",
    )(gs, a)


def _spectral_kernel(g_ref, gt_ref, h_ref, a_ref, b_ref):
    n2 = a_ref.shape[3]
    for j in range(g_ref.shape[0]):
        xre, xim = _inner_dft(g_ref[j], a_ref[0, 0, j], a_ref[0, 1, j])
        hre, him = h_ref[0, j], h_ref[1, j]
        yre = (xre * hre - xim * him).astype(BF16)
        yim = (xre * him + xim * hre).astype(BF16)
        gt = gt_ref[j]
        q1 = _dot(gt, yre)
        q2 = _dot(gt, yim)
        b_ref[0, 0, j] = (q1[:n2] + q2[n2:]).astype(b_ref.dtype)
        b_ref[0, 1, j] = (q2[:n2] - q1[n2:]).astype(b_ref.dtype)


def spectral_multiply(gs, gts, hf, a):
    b, _, n1, n2, c = a.shape
    kb = min(SPEC_KB, n1)
    return pl.pallas_call(
        _spectral_kernel,
        grid=(n1 // kb, b),
        in_specs=[pl.BlockSpec((kb, 2 * n2, n2), lambda k, bi: (k, 0, 0)),
                  pl.BlockSpec((kb, 2 * n2, n2), lambda k, bi: (k, 0, 0)),
                  pl.BlockSpec((2, kb, n2, c), lambda k, bi: (0, k, 0, 0)),
                  pl.BlockSpec((1, 2, kb, n2, c), lambda k, bi: (bi, 0, k, 0, 0))],
        out_specs=pl.BlockSpec((1, 2, kb, n2, c), lambda k, bi: (bi, 0, k, 0, 0)),
        out_shape=jax.ShapeDtypeStruct(a.shape, BF16),
        compiler_params=_cp("parallel", "arbitrary"),
        name="spectral_multiply",
    )(gs, gts, hf, a)


def _idft_rows_kernel(f_ref, bm_ref, x0_ref, z_ref, inv_ref, bias_ref, o_ref, *, inv_n):
    y = _dot(f_ref[...], bm_ref[0]) * inv_n
    z = z_ref[0]
    o_ref[0] = (x0_ref[0] * (y * inv_ref[...] + bias_ref[...] * z)).astype(o_ref.dtype)


def idft_rows_gate(fmat, bm, x0, z, inv_norm, bias, n):
    b, k, w = bm.shape
    m = fmat.shape[0]
    tn = min(2048, w)
    col = lambda bi, j: (bi, 0, j)
    return pl.pallas_call(
        functools.partial(_idft_rows_kernel, inv_n=1.0 / n),
        grid=(b, w // tn),
        in_specs=[pl.BlockSpec((m, k), lambda bi, j: (0, 0)),
                  pl.BlockSpec((1, k, tn), col),
                  pl.BlockSpec((1, m, tn), col),
                  pl.BlockSpec((1, m, tn), col),
                  pl.BlockSpec((1, tn), lambda bi, j: (0, j)),
                  pl.BlockSpec((1, tn), lambda bi, j: (0, j))],
        out_specs=pl.BlockSpec((1, m, tn), col),
        out_shape=jax.ShapeDtypeStruct((b, m, w), BF16),
        compiler_params=_cp("parallel", "parallel"),
        name="idft_rows_gate",
    )(fmat, bm, x0, z, inv_norm, bias)


def dft_tables(l):
    n = 2 * l
    n2 = FFT_N2
    n1 = n // n2
    k1 = np.arange(n1)
    ang1 = 2.0 * np.pi * np.outer(k1, k1) / n1
    f1 = np.concatenate([np.cos(ang1), -np.sin(ang1)], 0)
    f3 = np.concatenate([np.cos(ang1), -np.sin(ang1)], 1)[: n1 // 2]
    m2 = np.arange(n2)
    tw = 2.0 * np.pi * np.outer(k1, m2) / n
    ang2 = 2.0 * np.pi * np.outer(m2, m2) / n2
    twr, twi = jnp.asarray(np.cos(tw), F32)[:, None, :], jnp.asarray(-np.sin(tw), F32)[:, None, :]
    f2r, f2i = jnp.asarray(np.cos(ang2), F32)[None], jnp.asarray(-np.sin(ang2), F32)[None]
    gre, gim = twr * f2r - twi * f2i, twr * f2i + twi * f2r
    gs = jnp.concatenate([gre, gim], 1).astype(BF16)
    gts = jnp.concatenate([gre.transpose(0, 2, 1), gim.transpose(0, 2, 1)], 1).astype(BF16)
    return dict(n=n, n1=n1, n2=n2, f1=jnp.asarray(f1, BF16), f3=jnp.asarray(f3, BF16), gs=gs, gts=gts)


def hyena_long_conv(z, x0, filt2, norm, bias, tabs):
    b, l, c = z.shape
    n, n1, n2 = tabs["n"], tabs["n1"], tabs["n2"]
    w = n2 * c
    fa = dft_rows(tabs["f1"], filt2.reshape(1, n1, w), BF16)
    hf = filter_spectrum(tabs["gs"], fa.reshape(1, 2, n1, n2, c))
    a = dft_rows(tabs["f1"][:, : n1 // 2], z.reshape(b, n1 // 2, w), BF16)
    bm = spectral_multiply(tabs["gs"], tabs["gts"], hf, a.reshape(b, 2, n1, n2, c))
    tile = lambda v: jnp.tile(v.reshape(1, c), (1, n2))
    y = idft_rows_gate(tabs["f3"], bm.reshape(b, 2 * n1, w), x0.reshape(b, n1 // 2, w),
                       z.reshape(b, n1 // 2, w), tile(1.0 / norm), tile(bias), n)
    return y.reshape(b, l, c)


def _small_conv_kernel(ff_ref, fi_ref, filt_ref, z_ref, x0_ref, inv_ref, bias_ref, o_ref, *, l):
    ff = ff_ref[...]
    hs = _dot(ff, filt_ref[...].astype(BF16))
    hre, him = hs[:2 * l], hs[2 * l:]
    z = z_ref[0]
    xs = _dot(ff[:, :l], z.astype(BF16))
    xre, xim = xs[:2 * l], xs[2 * l:]
    yre = (xre * hre - xim * him).astype(BF16)
    yim = (xre * him + xim * hre).astype(BF16)
    y = _dot(fi_ref[...], jnp.concatenate([yre, yim], 0)) * (0.5 / l)
    o_ref[0] = (x0_ref[0] * (y * inv_ref[...] + bias_ref[...] * z)).astype(o_ref.dtype)


def small_long_conv(z, x0, filt2, norm, bias):
    b, l, c = z.shape
    k = np.arange(2 * l)
    ang = 2.0 * np.pi * np.outer(k, k) / (2 * l)
    ff = jnp.asarray(np.concatenate([np.cos(ang), -np.sin(ang)], 0), BF16)
    fi = jnp.asarray(np.concatenate([np.cos(ang), -np.sin(ang)], 1)[:l], BF16)
    full = lambda a: pl.BlockSpec(a.shape, lambda bi: (0,) * a.ndim)
    row = pl.BlockSpec((1, l, c), lambda bi: (bi, 0, 0))
    inv = 1.0 / norm
    return pl.pallas_call(
        functools.partial(_small_conv_kernel, l=l),
        grid=(b,),
        in_specs=[full(ff), full(fi), full(filt2), row, row, full(inv), full(bias)],
        out_specs=row,
        out_shape=jax.ShapeDtypeStruct((b, l, c), BF16),
        compiler_params=_cp("parallel"),
        name="small_long_conv",
    )(ff, fi, filt2, z, x0, inv, bias)


def _pool_kernel(p_ref, pp_ref, pn_ref, w_ref, sc_ref, o_ref, buf_ref, *, tm, l):
    _fill_padded(buf_ref, p_ref, pp_ref, pn_ref, tm)
    t = pl.program_id(1) * tm + lax.broadcasted_iota(jnp.int32, (tm, LANES), 0)
    lane = lax.broadcasted_iota(jnp.int32, (tm, LANES), 1)
    first_group = lane < POOL_GROUP
    h0 = POOL_HALO

    def win(col, lo, hi):
        acc = buf_ref[h0 + lo:h0 + lo + tm, col]
        for j in range(lo + 1, hi):
            acc = acc + buf_ref[h0 + j:h0 + j + tm, col]
        return acc

    parts = []
    for half, (wa, wb) in enumerate(((2, 4), (8, 16))):
        col = slice(half * LANES, (half + 1) * LANES)
        sa = win(col, -wa // 2, wa // 2)
        sb = sa + win(col, -wb // 2, -wa // 2) + win(col, wa // 2, wb // 2)
        hw = jnp.where(first_group, wa // 2, wb // 2)
        cnt = jnp.minimum(t + hw, l) - jnp.maximum(t - hw, 0)
        mean = jnp.where(first_group, sa, sb) / cnt.astype(F32)
        parts.append(mean - buf_ref[h0:h0 + tm, col])
    d = jnp.concatenate(parts, axis=-1).astype(BF16)
    o_ref[0] = (_dot(d, w_ref[...]) * sc_ref[...]).astype(o_ref.dtype)


def pool_mix(p, pool_w, pool_scale):
    b, l, wd = p.shape
    tm = min(512, l)
    nblk = l // tm
    wblk = jax.scipy.linalg.block_diag(*[pool_w[g] for g in range(len(POOL_WINDOWS))]).astype(BF16)
    return pl.pallas_call(
        functools.partial(_pool_kernel, tm=tm, l=l),
        grid=(b, nblk),
        in_specs=_halo_specs(tm, wd, nblk) + [pl.BlockSpec((wd, wd), lambda bi, i: (0, 0)),
                                              pl.BlockSpec((1, wd), lambda bi, i: (0, 0))],
        out_specs=pl.BlockSpec((1, tm, wd), lambda bi, i: (bi, i, 0)),
        out_shape=jax.ShapeDtypeStruct((b, l, wd), BF16),
        scratch_shapes=[pltpu.VMEM((tm + 2 * POOL_HALO, wd), F32)],
        compiler_params=_cp("parallel", "parallel"),
        name="pool_mix",
    )(p, p, p, wblk, pool_scale.reshape(1, wd))


def _outproj_kernel(x_ref, hy_ref, pool_ref, att_ref, w_ref, gate_ref, o_ref):
    a, b2 = HY_WIDTH, HY_WIDTH + POOL_WIDTH
    y = (_dot(hy_ref[0], w_ref[0:a]) + _dot(pool_ref[0], w_ref[a:b2]) + _dot(att_ref[0], w_ref[b2:]))
    o_ref[0] = x_ref[0] + gate_ref[0] * y


def out_projection(x, y_hy, y_pool, y_att, w_out, gate):
    b, l, d = x.shape
    tm = min(512, l)
    row = lambda bi, i: (bi, i, 0)
    return pl.pallas_call(
        _outproj_kernel,
        grid=(b, l // tm),
        in_specs=[pl.BlockSpec((1, tm, d), row),
                  pl.BlockSpec((1, tm, y_hy.shape[2]), row),
                  pl.BlockSpec((1, tm, y_pool.shape[2]), row),
                  pl.BlockSpec((1, tm, y_att.shape[2]), row),
                  pl.BlockSpec(w_out.shape, lambda bi, i: (0, 0)),
                  pl.BlockSpec((1, 1, d), lambda bi, i: (bi, 0, 0))],
        out_specs=pl.BlockSpec((1, tm, d), row),
        out_shape=jax.ShapeDtypeStruct(x.shape, F32),
        compiler_params=_cp("parallel", "parallel"),
        name="out_projection",
    )(x, y_hy, y_pool, y_att, w_out, gate)


FF_CHUNK = 256
MOE_TILE = 1024
MOE_ROWS = (128, 192, 256, 320, 384)


def _swiglu_acc(h, wg_ref, wu_ref, wd_ref, lead):
    ff = wg_ref.shape[-1]
    acc = None
    for f0 in range(0, ff, FF_CHUNK):
        cs = slice(f0, min(f0 + FF_CHUNK, ff))
        g = _dot(h, wg_ref[lead + (slice(None), cs)])
        u = _dot(h, wu_ref[lead + (slice(None), cs)])
        a = (g * jax.nn.sigmoid(g) * u).astype(BF16)
        part = _dot(a, wd_ref[lead + (cs, slice(None))])
        acc = part if acc is None else acc + part
    return acc


def _ffn_kernel(x_ref, g_ref, sh_ref, sc_ref, gate_ref, wg_ref, wu_ref, wd_ref, o_ref):
    x = x_ref[0]
    h = _normmod(x, g_ref[...], sh_ref[0], sc_ref[0]).astype(BF16)
    o_ref[0] = x + gate_ref[0] * _swiglu_acc(h, wg_ref, wu_ref, wd_ref, ())


def ffn_dense(x, g, shift, scale, gate, wg, wu, wd):
    b, l, d = x.shape
    tm = min(512, l)
    row = lambda bi, i: (bi, i, 0)
    vec = lambda bi, i: (bi, 0, 0)
    wspec = lambda w: pl.BlockSpec(w.shape, lambda bi, i: (0, 0), pipeline_mode=pl.Buffered(1))
    return pl.pallas_call(
        _ffn_kernel,
        grid=(b, l // tm),
        in_specs=[pl.BlockSpec((1, tm, d), row),
                  pl.BlockSpec((1, d), lambda bi, i: (0, 0)),
                  pl.BlockSpec((1, 1, d), vec), pl.BlockSpec((1, 1, d), vec), pl.BlockSpec((1, 1, d), vec),
                  wspec(wg), wspec(wu), wspec(wd)],
        out_specs=pl.BlockSpec((1, tm, d), row),
        out_shape=jax.ShapeDtypeStruct(x.shape, F32),
        compiler_params=_cp("parallel", "parallel"),
        name="ffn_dense",
    )(x, g, shift, scale, gate, wg, wu, wd)


def _router_kernel(x_ref, g_ref, sh_ref, sc_ref, wr_ref, tri_ref, comb_ref, rank_ref):
    h = _normmod(x_ref[0], g_ref[...], sh_ref[0], sc_ref[0])
    logits = _dot(h, wr_ref[...], precision=HIGHEST)
    lane = lax.broadcasted_iota(jnp.int32, logits.shape, 1)
    neg = jnp.float32(-jnp.inf)
    lg = jnp.where(lane < N_EXPERTS, logits, neg)
    m1 = jnp.max(lg, axis=-1, keepdims=True)
    i1 = jnp.min(jnp.where(lg == m1, lane, LANES), axis=-1, keepdims=True)
    lg2 = jnp.where(lane == i1, neg, lg)
    m2 = jnp.max(lg2, axis=-1, keepdims=True)
    i2 = jnp.min(jnp.where(lg2 == m2, lane, LANES), axis=-1, keepdims=True)
    e = jnp.exp(m2 - m1)
    g1 = 1.0 / (1.0 + e)
    comb_ref[0] = jnp.where(lane == i1, g1, jnp.where(lane == i2, e * g1, 0.0))
    routed = (lane == i1) | (lane == i2)
    before = _dot(tri_ref[...], jnp.where(routed, 1.0, 0.0).astype(BF16))
    rank_ref[0] = jnp.where(routed, before, -1.0)


def moe_route(x, g, shift, scale, w_router):
    b, l, d = x.shape
    tm = min(MOE_TILE, l)
    wr = jnp.concatenate([w_router, jnp.zeros((d, LANES - N_EXPERTS), F32)], -1)
    tri = jnp.asarray(np.tril(np.ones((tm, tm), np.float32), -1), BF16)
    row = lambda bi, i: (bi, i, 0)
    vec = lambda bi, i: (bi, 0, 0)
    return pl.pallas_call(
        _router_kernel,
        grid=(b, l // tm),
        in_specs=[pl.BlockSpec((1, tm, d), row),
                  pl.BlockSpec((1, d), lambda bi, i: (0, 0)),
                  pl.BlockSpec((1, 1, d), vec), pl.BlockSpec((1, 1, d), vec),
                  pl.BlockSpec((d, LANES), lambda bi, i: (0, 0)),
                  pl.BlockSpec((tm, tm), lambda bi, i: (0, 0))],
        out_specs=[pl.BlockSpec((1, tm, LANES), row)] * 2,
        out_shape=[jax.ShapeDtypeStruct((b, l, LANES), F32)] * 2,
        compiler_params=_cp("parallel", "parallel"),
        name="moe_router",
    )(x, g, shift, scale, wr, tri)


def _moe_kernel(x_ref, g_ref, sh_ref, sc_ref, gate_ref, comb_ref, wg_ref, wu_ref, wd_ref, o_ref, acc_ref):
    e = pl.program_id(2)
    x = x_ref[0]
    h = _normmod(x, g_ref[...], sh_ref[0], sc_ref[0]).astype(BF16)
    comb = comb_ref[0]
    lane = lax.broadcasted_iota(jnp.int32, comb.shape, 1)
    ce = jnp.sum(jnp.where(lane == e, comb, 0.0), axis=-1, keepdims=True)
    y = ce * _swiglu_acc(h, wg_ref, wu_ref, wd_ref, (0,))

    @pl.when(e == 0)
    def _():
        acc_ref[...] = y

    @pl.when(e > 0)
    def _():
        acc_ref[...] += y

    @pl.when(e == N_EXPERTS - 1)
    def _():
        o_ref[0] = x + gate_ref[0] * acc_ref[...]


def moe_dense(x, g, shift, scale, gate, comb, wg, wu, wd):
    b, l, d = x.shape
    tm = min(512, l)
    ff = wg.shape[-1]
    row = lambda bi, i, e: (bi, i, 0)
    vec = lambda bi, i, e: (bi, 0, 0)
    return pl.pallas_call(
        _moe_kernel,
        grid=(b, l // tm, N_EXPERTS),
        in_specs=[pl.BlockSpec((1, tm, d), row),
                  pl.BlockSpec((1, d), lambda bi, i, e: (0, 0)),
                  pl.BlockSpec((1, 1, d), vec), pl.BlockSpec((1, 1, d), vec), pl.BlockSpec((1, 1, d), vec),
                  pl.BlockSpec((1, tm, LANES), row),
                  pl.BlockSpec((1, d, ff), lambda bi, i, e: (e, 0, 0)),
                  pl.BlockSpec((1, d, ff), lambda bi, i, e: (e, 0, 0)),
                  pl.BlockSpec((1, ff, d), lambda bi, i, e: (e, 0, 0))],
        out_specs=pl.BlockSpec((1, tm, d), row),
        out_shape=jax.ShapeDtypeStruct(x.shape, F32),
        scratch_shapes=[pltpu.VMEM((tm, d), F32)],
        compiler_params=_cp("parallel", "parallel", "arbitrary"),
        name="moe_dense",
    )(x, g, shift, scale, gate, comb, wg, wu, wd)


def _moe_sparse_kernel(cnt_ref, x_ref, g_ref, sh_ref, sc_ref, gate_ref, comb_ref, rank_ref, rankt_ref,
                       wg_ref, wu_ref, wd_ref, *rest, t, sizes):
    fg_ref = rest[0] if len(rest) == 3 else None
    o_ref, h_scr = rest[-2:]
    i, e = pl.program_id(0), pl.program_id(1)
    cnt = cnt_ref[i * N_EXPERTS + e]

    @pl.when(e == 0)
    def _():
        x = x_ref[0]
        h_scr[...] = _normmod(x, g_ref[...], sh_ref[0], sc_ref[0]).astype(BF16)
        o_ref[0] = x

    rrow = rankt_ref[0, pl.ds(e, 1), :]
    lane = lax.broadcasted_iota(jnp.int32, (t, LANES), 1)
    rcol = jnp.sum(jnp.where(lane == e, rank_ref[0], 0.0), axis=-1, keepdims=True)
    ccol = jnp.sum(jnp.where(lane == e, comb_ref[0], 0.0), axis=-1, keepdims=True)

    def block(base, r):
        rid = (base + lax.broadcasted_iota(jnp.int32, (r, 1), 0)).astype(F32)
        onehot = jnp.where(rrow == rid, 1.0, 0.0).astype(BF16)
        xg = _dot(onehot, h_scr[...]).astype(BF16)
        y = _swiglu_acc(xg, wg_ref, wu_ref, wd_ref, (0,)).astype(BF16)
        cid = (base + lax.broadcasted_iota(jnp.int32, (1, r), 1)).astype(F32)
        onehot_t = jnp.where(rcol == cid, 1.0, 0.0).astype(BF16)
        o_ref[0] += (gate_ref[0] * ccol) * _dot(onehot_t, y)

    big = sizes[-1]
    nfull = cnt // big

    def full(k, carry):
        block(k * big, big)
        return carry
    lax.fori_loop(0, nfull, full, 0)
    base = nfull * big
    rem = cnt - base
    for lo, r in zip((0,) + sizes[:-1], sizes):
        @pl.when((rem > lo) & (rem <= r))
        def _(r=r):
            block(base, r)

    if fg_ref is not None:
        @pl.when(e == N_EXPERTS - 1)
        def _():
            o = o_ref[0]
            o_ref[0] = o * lax.rsqrt(jnp.mean(o * o, axis=-1, keepdims=True) + EPS) * fg_ref[...]


def moe_sparse(x, g, shift, scale, gate, comb, rank, wg, wu, wd, final_g=None):
    b, l, d = x.shape
    t = MOE_TILE
    tpb = l // t
    n_tiles = b * tpb
    ff = wg.shape[-1]
    rk = rank[..., :N_EXPERTS].reshape(n_tiles, t, N_EXPERTS)
    rank_t = rk.transpose(0, 2, 1)
    cnt = jnp.sum(rk >= 0, axis=1).astype(jnp.int32).reshape(n_tiles * N_EXPERTS)
    row = lambda i, e, c: (i // tpb, i % tpb, 0)
    vec = lambda i, e, c: (i // tpb, 0, 0)
    grid_spec = pltpu.PrefetchScalarGridSpec(
        num_scalar_prefetch=1,
        grid=(n_tiles, N_EXPERTS),
        in_specs=[pl.BlockSpec((1, t, d), row, pipeline_mode=pl.Buffered(1)),
                  pl.BlockSpec((1, d), lambda i, e, c: (0, 0)),
                  pl.BlockSpec((1, 1, d), vec), pl.BlockSpec((1, 1, d), vec), pl.BlockSpec((1, 1, d), vec),
                  pl.BlockSpec((1, t, LANES), row, pipeline_mode=pl.Buffered(1)),
                  pl.BlockSpec((1, t, LANES), row, pipeline_mode=pl.Buffered(1)),
                  pl.BlockSpec((1, N_EXPERTS, t), lambda i, e, c: (i, 0, 0)),
                  pl.BlockSpec((1, d, ff), lambda i, e, c: (e, 0, 0)),
                  pl.BlockSpec((1, d, ff), lambda i, e, c: (e, 0, 0)),
                  pl.BlockSpec((1, ff, d), lambda i, e, c: (e, 0, 0))]
        + ([] if final_g is None else [pl.BlockSpec((1, d), lambda i, e, c: (0, 0))]),
        out_specs=pl.BlockSpec((1, t, d), row),
        scratch_shapes=[pltpu.VMEM((t, d), BF16)],
    )
    return pl.pallas_call(
        functools.partial(_moe_sparse_kernel, t=t, sizes=MOE_ROWS),
        grid_spec=grid_spec,
        out_shape=jax.ShapeDtypeStruct(x.shape, F32),
        compiler_params=_cp("parallel", "arbitrary"),
        name="moe_sparse",
    )(cnt, x, g, shift, scale, gate, comb, rank, rank_t, wg, wu, wd, *(() if final_g is None else (final_g,)))


def _final_kernel(x_ref, g_ref, o_ref):
    x = x_ref[0]
    o_ref[0] = x * lax.rsqrt(jnp.mean(x * x, axis=-1, keepdims=True) + EPS) * g_ref[...]


def final_norm(x, g):
    b, l, d = x.shape
    tm = min(1024, l)
    return pl.pallas_call(
        _final_kernel,
        grid=(b, l // tm),
        in_specs=[pl.BlockSpec((1, tm, d), lambda bi, i: (bi, i, 0)),
                  pl.BlockSpec((1, d), lambda bi, i: (0, 0))],
        out_specs=pl.BlockSpec((1, tm, d), lambda bi, i: (bi, i, 0)),
        out_shape=jax.ShapeDtypeStruct(x.shape, F32),
        compiler_params=_cp("parallel", "parallel"),
        name="final_norm",
    )(x, g)


def kernel(x, c, ctx, c_ctx, norm1_g, norm2_g, w_ada, b_ada, w_in, hy_short_w, hy_short_b, filt_w1, filt_b1, filt_freq1, filt_w2, filt_b2, filt_freq2, filt_w3, hy_bias, pool_w, pool_scale, q_norm_g, w_uq, kv_norm_g, w_ukv, w_out, ffn_wg, ffn_wu, ffn_wd, moe_router, moe_wg, moe_wu, moe_wd, final_g):
    b, l, d = x.shape
    lc = ctx.shape[1]
    depth = w_ada.shape[0]

    cc = jnp.zeros((8, d), F32).at[:b].set(c).at[b].set(c_ctx)
    mods = ada_vectors(cc, w_ada, b_ada)

    rope_l = rope_tables(l, True)
    rope_c = rope_tables(lc, False)
    ftab_l, ftab_c = filter_tables(l), filter_tables(lc)
    dtabs = dft_tables(l)

    xl, xc = x, ctx
    for layer in range(depth):
        last = layer == depth - 1
        mod = mods[layer].reshape(8, 6, 1, d)
        mod_l = [mod[:b, j] for j in range(6)]
        mod_c = [jnp.broadcast_to(mod[b:b + 1, j], (b, 1, d)) for j in range(6)]
        g1 = norm1_g[layer].reshape(1, d)
        g2 = norm2_g[layer].reshape(1, d)
        w_in_l = w_in[layer].astype(BF16)
        w_out_l = w_out[layer].astype(BF16)
        mw = mla_weights(q_norm_g[layer], w_uq[layer], kv_norm_g[layer], w_ukv[layer])
        fw = (filt_w1[layer], filt_b1[layer], filt_freq1[layer], filt_w2[layer], filt_b2[layer],
              filt_freq2[layer], filt_w3[layer])
        bias = hy_bias[layer].reshape(1, HY_WIDTH)

        def channel_mix(xs, md, final=False):
            i = layer // 2
            fg = final_g.reshape(1, d)
            if layer % 2 == 1 and xs.shape[1] % MOE_TILE == 0:
                comb, rank = moe_route(xs, g2, md[3], md[4], moe_router[i])
                return moe_sparse(xs, g2, md[3], md[4], md[5], comb, rank, moe_wg[i].astype(BF16),
                                  moe_wu[i].astype(BF16), moe_wd[i].astype(BF16), fg if final else None)
            if layer % 2 == 0:
                out = ffn_dense(xs, g2, md[3], md[4], md[5], ffn_wg[i].astype(BF16),
                                ffn_wu[i].astype(BF16), ffn_wd[i].astype(BF16))
            else:
                comb, _ = moe_route(xs, g2, md[3], md[4], moe_router[i])
                out = moe_dense(xs, g2, md[3], md[4], md[5], comb, moe_wg[i].astype(BF16),
                                moe_wu[i].astype(BF16), moe_wd[i].astype(BF16))
            return final_norm(out, fg) if final else out

        hy_l, pool_l, q_l, kt_l, v_l = in_projection(xl, g1, mod_l[0], mod_l[1], w_in_l, *rope_l, mw)
        hy_c, pool_c, q_c, kt_c, v_c = in_projection(xc, g1, mod_c[0], mod_c[1], w_in_l, *rope_c, mw)

        att_l = attention(q_l, kt_c, v_c, kt_l, v_l)
        z_l, x0_l = hyena_pre(hy_l, hy_short_w[layer], hy_short_b[layer])
        filt_l, nrm_l = hyena_filter(l, ftab_l, *fw)
        yhy_l = hyena_long_conv(z_l, x0_l, filt_l, nrm_l, bias, dtabs)
        ypool_l = pool_mix(pool_l, pool_w[layer], pool_scale[layer])
        xl = out_projection(xl, yhy_l, ypool_l, att_l, w_out_l, mod_l[2])

        if not last:
            att_c = attention(q_c, kt_c, v_c)
            z_c, x0_c = hyena_pre(hy_c, hy_short_w[layer], hy_short_b[layer])
            filt_c, nrm_c = hyena_filter(lc, ftab_c, *fw)
            yhy_c = small_long_conv(z_c, x0_c, filt_c, nrm_c, bias)
            ypool_c = pool_mix(pool_c, pool_w[layer], pool_scale[layer])
            xc = out_projection(xc, yhy_c, ypool_c, att_c, w_out_l, mod_c[2])
            xc = channel_mix(xc, mod_c)

        xl = channel_mix(xl, mod_l, final=last)
    return xl
```

```python
import functools
import math

import jax
import jax.numpy as jnp
import numpy as np
from jax import lax
from jax.experimental import pallas as pl
from jax.experimental.pallas import tpu as pltpu

F32 = jnp.float32
BF16 = jnp.bfloat16
HIGHEST = lax.Precision.HIGHEST

EPS = 1e-6
GRID_W = 64
HY_WIDTH = 256
POOL_WINDOWS = (2, 4, 8, 16)
POOL_WIDTH = 256
POOL_GROUP = 64
POOL_HALO = 8
MLA_HEADS = 8
QK_NOPE = 64
QK_ROPE = 32
V_DIM = 64
Q_LORA = 384
KV_LORA = 256
HEAD_PAD = 128
SM_SCALE = (QK_NOPE + QK_ROPE) ** -0.5
ROPE_THETA = 10000.0
FILT_EMB = 33
FILT_BANDS = 16
DECAY_TARGET = 1e-2
FAST_DECAY_PCT = 0.3
SLOW_DECAY_PCT = 1.5
N_EXPERTS = 8
LANES = 128
FFT_N2 = 128
VMEM_LIMIT = 56 * 2 ** 20


def _cp(*sem):
    return pltpu.CompilerParams(dimension_semantics=sem, vmem_limit_bytes=VMEM_LIMIT)


def _dot(a, b, **kw):
    return jnp.dot(a, b, preferred_element_type=F32, **kw)


def _normmod(x, g, shift, scale):
    ms = jnp.mean(x * x, axis=-1, keepdims=True)
    return x * lax.rsqrt(ms + EPS) * (g * (1.0 + scale)) + shift


def _ada_kernel(c_ref, w_ref, b_ref, o_ref):
    c = c_ref[...]
    s = c * jax.nn.sigmoid(c)
    o_ref[0] = _dot(s, w_ref[0], precision=HIGHEST) + b_ref[0]


def ada_vectors(cc, w_ada, b_ada):
    depth, d, n = w_ada.shape
    tn = 1536
    return pl.pallas_call(
        _ada_kernel,
        grid=(depth, n // tn),
        in_specs=[pl.BlockSpec((8, d), lambda l, j: (0, 0)),
                  pl.BlockSpec((1, d, tn), lambda l, j: (l, 0, j)),
                  pl.BlockSpec((1, 1, tn), lambda l, j: (l, 0, j))],
        out_specs=pl.BlockSpec((1, 8, tn), lambda l, j: (l, 0, j)),
        out_shape=jax.ShapeDtypeStruct((depth, 8, n), F32),
        compiler_params=_cp("parallel", "parallel"),
        name="ada_vectors",
    )(cc, w_ada, b_ada.reshape(depth, 1, n))


IN_HY, IN_POOL, IN_CQ, IN_CKV, IN_KPE = 768, 256, 384, 256, 32
IN_OFFS = (0, 768, 1024, 1408, 1664, 1696)


def _inproj_kernel(x_ref, g_ref, sh_ref, sc_ref, w_ref, c_ref, s_ref, wqa_ref, wqb_ref, wk_ref, wv_ref,
                   pa_ref, pb_ref, hy_ref, pool_ref, q_ref, kt_ref, v_ref):
    h = _normmod(x_ref[0], g_ref[...], sh_ref[0], sc_ref[0]).astype(BF16)
    y = _dot(h, w_ref[...])
    o = IN_OFFS
    hy_ref[0] = y[:, o[0]:o[1]]
    pool_ref[0] = y[:, o[1]:o[2]]
    cq, ckv = y[:, o[2]:o[3]], y[:, o[3]:o[4]]
    kpe = y[:, o[4]:o[5]].astype(BF16)
    cqn = (cq * lax.rsqrt(jnp.mean(cq * cq, axis=-1, keepdims=True) + EPS)).astype(BF16)
    ckvn = (ckv * lax.rsqrt(jnp.mean(ckv * ckv, axis=-1, keepdims=True) + EPS)).astype(BF16)
    cs, sn = c_ref[...], s_ref[...]
    qa = _dot(cqn, wqa_ref[...])
    qb = _dot(cqn, wqb_ref[...])
    kn = _dot(ckvn, wk_ref[...])
    vv = _dot(ckvn, wv_ref[...])
    kpa = _dot(kpe, pa_ref[...])
    kpb = _dot(kpe, pb_ref[...])
    kpe_rot = kpa * cs + kpb * sn
    lane = lax.broadcasted_iota(jnp.int32, (1, HEAD_PAD), 1)
    for hd in range(MLA_HEADS):
        sl = slice(hd * HEAD_PAD, (hd + 1) * HEAD_PAD)
        q_ref[0, hd] = (qa[:, sl] * cs + qb[:, sl] * sn).astype(BF16)
        kt_ref[0, hd] = (kn[:, sl] + kpe_rot).T.astype(BF16)
        one_lane = V_DIM if hd % 2 == 0 else 0
        v_ref[0, hd] = jnp.where(lane == one_lane, 1.0, vv[:, sl]).astype(BF16)


def in_projection(x, g, shift, scale, w_in, ctab, stab, wts):
    b, l, d = x.shape
    tm = min(512, l)
    hw = MLA_HEADS
    row = lambda bi, i: (bi, i, 0)
    vec = lambda bi, i: (bi, 0, 0)
    full = lambda a: pl.BlockSpec(a.shape, lambda bi, i: (0,) * a.ndim)
    return pl.pallas_call(
        _inproj_kernel,
        grid=(b, l // tm),
        in_specs=[pl.BlockSpec((1, tm, d), row),
                  pl.BlockSpec((1, d), lambda bi, i: (0, 0)),
                  pl.BlockSpec((1, 1, d), vec),
                  pl.BlockSpec((1, 1, d), vec),
                  full(w_in),
                  pl.BlockSpec((tm, HEAD_PAD), lambda bi, i: (i, 0)),
                  pl.BlockSpec((tm, HEAD_PAD), lambda bi, i: (i, 0))] + [full(w) for w in wts],
        out_specs=[pl.BlockSpec((1, tm, IN_HY), row),
                   pl.BlockSpec((1, tm, IN_POOL), row),
                   pl.BlockSpec((1, hw, tm, HEAD_PAD), lambda bi, i: (bi, 0, i, 0)),
                   pl.BlockSpec((1, hw, HEAD_PAD, tm), lambda bi, i: (bi, 0, 0, i)),
                   pl.BlockSpec((1, hw, tm, HEAD_PAD), lambda bi, i: (bi, 0, i, 0))],
        out_shape=[jax.ShapeDtypeStruct((b, l, IN_HY), F32),
                   jax.ShapeDtypeStruct((b, l, IN_POOL), F32),
                   jax.ShapeDtypeStruct((b, hw, l, HEAD_PAD), BF16),
                   jax.ShapeDtypeStruct((b, hw, HEAD_PAD, l), BF16),
                   jax.ShapeDtypeStruct((b, hw, l, HEAD_PAD), BF16)],
        compiler_params=_cp("parallel", "parallel"),
        name="in_projection",
    )(x, g, shift, scale, w_in, ctab, stab, *wts)


def mla_weights(q_norm_g, w_uq, kv_norm_g, w_ukv):
    hw, dq = MLA_HEADS, QK_NOPE + QK_ROPE
    half = QK_ROPE // 2
    wq = (w_uq * q_norm_g[:, None] * (SM_SCALE * math.log2(math.e))).reshape(Q_LORA, hw, dq)
    pad = jnp.zeros((Q_LORA, hw, HEAD_PAD - dq), F32)
    wqa = jnp.concatenate([wq, pad], -1)
    swap = jnp.concatenate([wq[..., QK_NOPE + half:], wq[..., QK_NOPE:QK_NOPE + half]], -1)
    wqb = jnp.concatenate([jnp.zeros((Q_LORA, hw, QK_NOPE), F32), swap, pad], -1)
    wkv = (w_ukv * kv_norm_g[:, None]).reshape(KV_LORA, hw, QK_NOPE + V_DIM)
    z64 = jnp.zeros((KV_LORA, hw, 64), F32)
    wk = jnp.concatenate([wkv[..., :QK_NOPE], z64], -1)
    v = wkv[..., QK_NOPE:]
    wv = jnp.where((jnp.arange(hw) % 2 == 0)[None, :, None],
                   jnp.concatenate([v, z64], -1), jnp.concatenate([z64, v], -1))
    eye = np.eye(QK_ROPE, dtype=np.float32)
    pa = np.zeros((QK_ROPE, HEAD_PAD), np.float32)
    pa[:, QK_NOPE:QK_NOPE + QK_ROPE] = eye
    pb = np.zeros((QK_ROPE, HEAD_PAD), np.float32)
    pb[:, QK_NOPE:QK_NOPE + QK_ROPE] = np.concatenate([eye[:, half:], eye[:, :half]], 1)
    flat = lambda w: w.reshape(w.shape[0], hw * HEAD_PAD).astype(BF16)
    return (flat(wqa), flat(wqb), flat(wk), flat(wv), jnp.asarray(pa, BF16), jnp.asarray(pb, BF16))


def rope_tables(l, with_rope):
    ctab = np.zeros((l, HEAD_PAD), np.float32)
    stab = np.zeros((l, HEAD_PAD), np.float32)
    ctab[:, :QK_NOPE + QK_ROPE] = 1.0
    if not with_rope:
        return jnp.asarray(ctab), jnp.asarray(stab)
    n_freq = QK_ROPE // 4
    inv_freq = 1.0 / (ROPE_THETA ** (jnp.arange(n_freq, dtype=F32) / n_freq))
    rows = l // GRID_W
    row = jnp.repeat(jnp.arange(rows, dtype=F32), GRID_W)
    col = jnp.tile(jnp.arange(GRID_W, dtype=F32), rows)
    ang = jnp.concatenate([row[:, None] * inv_freq, col[:, None] * inv_freq], axis=-1)
    cos, sin = jnp.cos(ang), jnp.sin(ang)
    one = jnp.ones((l, QK_NOPE), F32)
    zero = jnp.zeros((l, QK_NOPE), F32)
    tail = jnp.zeros((l, HEAD_PAD - QK_NOPE - QK_ROPE), F32)
    return (jnp.concatenate([one, cos, cos, tail], -1), jnp.concatenate([zero, -sin, sin, tail], -1))


ATT_TQ = 1024
ATT_RB = 1024
ATT_CK = 2048


def _row_max(s):
    mp = s[:, 0:LANES]
    for j in range(1, s.shape[1] // LANES):
        mp = jnp.maximum(mp, s[:, j * LANES:(j + 1) * LANES])
    return jnp.max(mp, axis=-1, keepdims=True)


def _attn_kernel(q_ref, kct_ref, vc_ref, *rest, n_chunks, ck, tq):
    if n_chunks:
        kt_ref, v_ref, o_ref = rest
    else:
        (o_ref,) = rest
    rb = min(ATT_RB, tq)
    chains = [(hh, r) for hh in range(2) for r in range(tq // rb)]
    lane = lax.broadcasted_iota(jnp.int32, (1, HEAD_PAD), 1)

    def q_rows(hh, r):
        return q_ref[0, hh, r * rb:(r + 1) * rb, :]

    state = []
    for hh, r in chains:
        s = _dot(q_rows(hh, r), kct_ref[0, hh])
        m = _row_max(s)
        p = jnp.exp2(s - m).astype(BF16)
        state += [m, _dot(p, vc_ref[0, hh])]

    if n_chunks:
        def body(c, state):
            off = pl.multiple_of(c * ck, ck)
            out = []
            for i, (hh, r) in enumerate(chains):
                m, acc = state[2 * i], state[2 * i + 1]
                s = _dot(q_rows(hh, r), kt_ref[0, hh, :, pl.ds(off, ck)])
                m_new = jnp.maximum(m, _row_max(s))
                alpha = jnp.exp2(m - m_new)
                p = jnp.exp2(s - m_new).astype(BF16)
                out += [m_new, acc * alpha + _dot(p, v_ref[0, hh, pl.ds(off, ck), :])]
            return tuple(out)
        state = lax.fori_loop(0, n_chunks, body, tuple(state), unroll=2 if n_chunks % 2 == 0 else 1)

    for r in range(tq // rb):
        outs = []
        for hh in range(2):
            acc = state[2 * chains.index((hh, r)) + 1]
            one_lane = V_DIM if hh == 0 else 0
            denom = jnp.sum(jnp.where(lane == one_lane, acc, 0.0), axis=-1, keepdims=True)
            outs.append(acc / denom)
        o_ref[0, r * rb:(r + 1) * rb, :] = jnp.where(lane < V_DIM, outs[0], outs[1]).astype(o_ref.dtype)


def attention(q, kct, vc, kt=None, v=None):
    b, hw, lq, _ = q.shape
    lc = vc.shape[2]
    tq = min(ATT_TQ, lq)
    ck = ATT_CK if kt is None else min(ATT_CK, kt.shape[3])
    n_chunks = 0 if kt is None else kt.shape[3] // ck
    in_specs = [pl.BlockSpec((1, 2, tq, HEAD_PAD), lambda bi, hp, i: (bi, hp, i, 0)),
                pl.BlockSpec((1, 2, HEAD_PAD, lc), lambda bi, hp, i: (bi, hp, 0, 0)),
                pl.BlockSpec((1, 2, lc, HEAD_PAD), lambda bi, hp, i: (bi, hp, 0, 0))]
    args = [q, kct, vc]
    if n_chunks:
        l = kt.shape[3]
        in_specs += [pl.BlockSpec((1, 2, HEAD_PAD, l), lambda bi, hp, i: (bi, hp, 0, 0),
                                  pipeline_mode=pl.Buffered(1)),
                     pl.BlockSpec((1, 2, l, HEAD_PAD), lambda bi, hp, i: (bi, hp, 0, 0),
                                  pipeline_mode=pl.Buffered(1))]
        args += [kt, v]
    return pl.pallas_call(
        functools.partial(_attn_kernel, n_chunks=n_chunks, ck=ck, tq=tq),
        grid=(b, hw // 2, lq // tq),
        in_specs=in_specs,
        out_specs=pl.BlockSpec((1, tq, HEAD_PAD), lambda bi, hp, i: (bi, i, hp)),
        out_shape=jax.ShapeDtypeStruct((b, lq, hw * V_DIM), BF16),
        compiler_params=_cp("parallel", "parallel", "arbitrary"),
        name="attention",
    )(*args)


def _halo_specs(tm, width, nblk):
    r = tm // POOL_HALO
    last = nblk * r - 1
    return [pl.BlockSpec((1, tm, width), lambda bi, i: (bi, i, 0)),
            pl.BlockSpec((1, POOL_HALO, width), lambda bi, i: (bi, jnp.maximum(i * r - 1, 0), 0)),
            pl.BlockSpec((1, POOL_HALO, width), lambda bi, i: (bi, jnp.minimum((i + 1) * r, last), 0))]


def _fill_padded(buf_ref, cur_ref, prev_ref, next_ref, tm):
    i = pl.program_id(1)
    first = i == 0
    last = i == pl.num_programs(1) - 1
    buf_ref[0:POOL_HALO] = jnp.where(first, 0.0, prev_ref[0])
    buf_ref[POOL_HALO:POOL_HALO + tm] = cur_ref[0]
    buf_ref[POOL_HALO + tm:] = jnp.where(last, 0.0, next_ref[0])


def _hyena_pre_kernel(u_ref, up_ref, un_ref, w_ref, b_ref, z_ref, x0_ref, buf_ref, *, tm):
    _fill_padded(buf_ref, u_ref, up_ref, un_ref, tm)
    w = w_ref[...]
    uc = (buf_ref[POOL_HALO - 1:POOL_HALO - 1 + tm] * w[0:1] + buf_ref[POOL_HALO:POOL_HALO + tm] * w[1:2]
          + buf_ref[POOL_HALO + 1:POOL_HALO + 1 + tm] * w[2:3] + b_ref[...])
    x0_ref[0] = uc[:, :HY_WIDTH]
    z_ref[0] = uc[:, HY_WIDTH:2 * HY_WIDTH] * uc[:, 2 * HY_WIDTH:]


def hyena_pre(u, sw, sb):
    b, l, wd = u.shape
    tm = min(512, l)
    nblk = l // tm
    return pl.pallas_call(
        functools.partial(_hyena_pre_kernel, tm=tm),
        grid=(b, nblk),
        in_specs=_halo_specs(tm, wd, nblk) + [pl.BlockSpec((3, wd), lambda bi, i: (0, 0)),
                                              pl.BlockSpec((1, wd), lambda bi, i: (0, 0))],
        out_specs=[pl.BlockSpec((1, tm, HY_WIDTH), lambda bi, i: (bi, i, 0))] * 2,
        out_shape=[jax.ShapeDtypeStruct((b, l, HY_WIDTH), F32)] * 2,
        scratch_shapes=[pltpu.VMEM((tm + 2 * POOL_HALO, wd), F32)],
        compiler_params=_cp("parallel", "parallel"),
        name="hyena_pre",
    )(u, u, u, sw, sb.reshape(1, wd))


FILT_TILE = 512


def _filter_kernel(feat_ref, tdec_ref, w1_ref, b1_ref, f1_ref, w2_ref, b2_ref, f2_ref, w3_ref, dl_ref,
                   filt_ref, norm_ref, *, l, tr):
    i = pl.program_id(0)
    half = tr // 2
    h = jnp.sin(f1_ref[...] * (_dot(feat_ref[...], w1_ref[...], precision=HIGHEST) + b1_ref[...]))
    h = jnp.sin(f2_ref[...] * (_dot(h, w2_ref[...], precision=HIGHEST) + b2_ref[...]))
    h = _dot(h, w3_ref[...], precision=HIGHEST)
    dl = jnp.abs(dl_ref[...])
    total = jnp.zeros((1, HY_WIDTH), F32)
    for p in range(2):
        rows = slice(p * half, (p + 1) * half)
        hp = h[:, p * 2 * HY_WIDTH:(p + 1) * 2 * HY_WIDTH]
        decay = jnp.exp(-tdec_ref[rows] * dl)
        n = i * tr + p * half + lax.broadcasted_iota(jnp.int32, (half, 1), 0)
        f = jnp.where(n < l, hp[:, :HY_WIDTH], hp[:, HY_WIDTH:]) * decay
        f = jnp.where(n == l, 0.0, f)
        filt_ref[rows] = f
        total = total + jnp.sum(jnp.abs(f), axis=0, keepdims=True)

    @pl.when(i == 0)
    def _():
        norm_ref[...] = jnp.zeros_like(norm_ref)
    norm_ref[...] += total


def filter_tables(l):
    n = jnp.arange(2 * l)
    pos = jnp.where(n < l, n, 2 * l - n).astype(F32)[:, None]
    t = pos / max(l - 1, 1)
    w = 2.0 * math.pi * pos / l
    f = jnp.linspace(1e-4, FILT_BANDS - 1, FILT_BANDS, dtype=F32)[None, :]
    feat = jnp.concatenate([t, jnp.cos(f * w), -jnp.sin(f * w),
                            jnp.zeros((2 * l, LANES - FILT_EMB), F32)], axis=-1)
    half = min(FILT_TILE, 2 * l) // 2
    feat = feat.reshape(-1, 2, half, LANES).transpose(0, 2, 1, 3).reshape(l, 2 * LANES)
    return feat, t


def hyena_filter(l, tables, w1, b1, f1, w2, b2, f2, w3):
    feat, t = tables
    tr = min(FILT_TILE, 2 * l)
    hid = w2.shape[0]
    max_decay = math.log(DECAY_TARGET) / FAST_DECAY_PCT
    min_decay = math.log(DECAY_TARGET) / SLOW_DECAY_PCT
    deltas = jnp.linspace(min_decay, max_decay, HY_WIDTH, dtype=F32)[None, :]
    w1p = jnp.concatenate([w1, jnp.zeros((LANES - FILT_EMB, hid), F32)], 0)
    twice = lambda m: jax.scipy.linalg.block_diag(m, m)
    pair = lambda v: jnp.tile(v.reshape(1, hid), (1, 2))
    full = lambda a: pl.BlockSpec(a.shape, lambda i: (0,) * a.ndim)
    smalls = [twice(w1p), pair(b1), pair(f1), twice(w2), pair(b2), pair(f2), twice(w3), deltas]
    return pl.pallas_call(
        functools.partial(_filter_kernel, l=l, tr=tr),
        grid=(2 * l // tr,),
        in_specs=[pl.BlockSpec((tr // 2, 2 * LANES), lambda i: (i, 0)),
                  pl.BlockSpec((tr, 1), lambda i: (i, 0))] + [full(a) for a in smalls],
        out_specs=[pl.BlockSpec((tr, HY_WIDTH), lambda i: (i, 0)),
                   pl.BlockSpec((1, HY_WIDTH), lambda i: (0, 0))],
        out_shape=[jax.ShapeDtypeStruct((2 * l, HY_WIDTH), F32),
                   jax.ShapeDtypeStruct((1, HY_WIDTH), F32)],
        compiler_params=_cp("arbitrary"),
        name="hyena_filter",
    )(feat, t, *smalls)


def _dft_rows_kernel(f_ref, x_ref, o_ref):
    o_ref[0] = _dot(f_ref[...], x_ref[0].astype(BF16)).astype(o_ref.dtype)


def dft_rows(fmat, x, out_dtype):
    b, k, w = x.shape
    m = fmat.shape[0]
    tn = min(2048, w)
    return pl.pallas_call(
        _dft_rows_kernel,
        grid=(b, w // tn),
        in_specs=[pl.BlockSpec((m, k), lambda bi, j: (0, 0)),
                  pl.BlockSpec((1, k, tn), lambda bi, j: (bi, 0, j))],
        out_specs=pl.BlockSpec((1, m, tn), lambda bi, j: (bi, 0, j)),
        out_shape=jax.ShapeDtypeStruct((b, m, w), out_dtype),
        compiler_params=_cp("parallel", "parallel"),
        name="dft_rows",
    )(fmat, x)


SPEC_KB = 8


def _inner_dft(g, are, aim):
    n2 = are.shape[0]
    p1 = _dot(g, are)
    p2 = _dot(g, aim)
    return p1[:n2] - p2[n2:], p2[:n2] + p1[n2:]


def _spectrum_kernel(g_ref, a_ref, h_ref):
    for j in range(g_ref.shape[0]):
        h_ref[0, j], h_ref[1, j] = _inner_dft(g_ref[j], a_ref[0, 0, j], a_ref[0, 1, j])


def filter_spectrum(gs, a):
    _, _, n1, n2, c = a.shape
    kb = min(SPEC_KB, n1)
    return pl.pallas_call(
        _spectrum_kernel,
        grid=(n1 // kb,),
        in_specs=[pl.BlockSpec((kb, 2 * n2, n2), lambda k: (k, 0, 0)),
                  pl.BlockSpec((1, 2, kb, n2, c), lambda k: (0, 0, k, 0, 0))],
        out_specs=pl.BlockSpec((2, kb, n2, c), lambda k: (0, k, 0, 0)),
        out_shape=jax.ShapeDtypeStruct((2, n1, n2, c), F32),
        compiler_params=_cp("parallel"),
        name="filter_spectrum",
    )(gs, a)


def _spectral_kernel(g_ref, gt_ref, h_ref, a_ref, b_ref):
    n2 = a_ref.shape[3]
    for j in range(g_ref.shape[0]):
        xre, xim = _inner_dft(g_ref[j], a_ref[0, 0, j], a_ref[0, 1, j])
        hre, him = h_ref[0, j], h_ref[1, j]
        yre = (xre * hre - xim * him).astype(BF16)
        yim = (xre * him + xim * hre).astype(BF16)
        gt = gt_ref[j]
        q1 = _dot(gt, yre)
        q2 = _dot(gt, yim)
        b_ref[0, 0, j] = (q1[:n2] + q2[n2:]).astype(b_ref.dtype)
        b_ref[0, 1, j] = (q2[:n2] - q1[n2:]).astype(b_ref.dtype)


def spectral_multiply(gs, gts, hf, a):
    b, _, n1, n2, c = a.shape
    kb = min(SPEC_KB, n1)
    return pl.pallas_call(
        _spectral_kernel,
        grid=(n1 // kb, b),
        in_specs=[pl.BlockSpec((kb, 2 * n2, n2), lambda k, bi: (k, 0, 0)),
                  pl.BlockSpec((kb, 2 * n2, n2), lambda k, bi: (k, 0, 0)),
                  pl.BlockSpec((2, kb, n2, c), lambda k, bi: (0, k, 0, 0)),
                  pl.BlockSpec((1, 2, kb, n2, c), lambda k, bi: (bi, 0, k, 0, 0))],
        out_specs=pl.BlockSpec((1, 2, kb, n2, c), lambda k, bi: (bi, 0, k, 0, 0)),
        out_shape=jax.ShapeDtypeStruct(a.shape, BF16),
        compiler_params=_cp("parallel", "arbitrary"),
        name="spectral_multiply",
    )(gs, gts, hf, a)


def _idft_rows_kernel(f_ref, bm_ref, x0_ref, z_ref, inv_ref, bias_ref, o_ref, *, inv_n):
    y = _dot(f_ref[...], bm_ref[0]) * inv_n
    z = z_ref[0]
    o_ref[0] = (x0_ref[0] * (y * inv_ref[...] + bias_ref[...] * z)).astype(o_ref.dtype)


def idft_rows_gate(fmat, bm, x0, z, inv_norm, bias, n):
    b, k, w = bm.shape
    m = fmat.shape[0]
    tn = min(2048, w)
    col = lambda bi, j: (bi, 0, j)
    return pl.pallas_call(
        functools.partial(_idft_rows_kernel, inv_n=1.0 / n),
        grid=(b, w // tn),
        in_specs=[pl.BlockSpec((m, k), lambda bi, j: (0, 0)),
                  pl.BlockSpec((1, k, tn), col),
                  pl.BlockSpec((1, m, tn), col),
                  pl.BlockSpec((1, m, tn), col),
                  pl.BlockSpec((1, tn), lambda bi, j: (0, j)),
                  pl.BlockSpec((1, tn), lambda bi, j: (0, j))],
        out_specs=pl.BlockSpec((1, m, tn), col),
        out_shape=jax.ShapeDtypeStruct((b, m, w), BF16),
        compiler_params=_cp("parallel", "parallel"),
        name="idft_rows_gate",
    )(fmat, bm, x0, z, inv_norm, bias)


def dft_tables(l):
    n = 2 * l
    n2 = FFT_N2
    n1 = n // n2
    k1 = np.arange(n1)
    ang1 = 2.0 * np.pi * np.outer(k1, k1) / n1
    f1 = np.concatenate([np.cos(ang1), -np.sin(ang1)], 0)
    f3 = np.concatenate([np.cos(ang1), -np.sin(ang1)], 1)[: n1 // 2]
    m2 = np.arange(n2)
    tw = 2.0 * np.pi * np.outer(k1, m2) / n
    ang2 = 2.0 * np.pi * np.outer(m2, m2) / n2
    twr, twi = jnp.asarray(np.cos(tw), F32)[:, None, :], jnp.asarray(-np.sin(tw), F32)[:, None, :]
    f2r, f2i = jnp.asarray(np.cos(ang2), F32)[None], jnp.asarray(-np.sin(ang2), F32)[None]
    gre, gim = twr * f2r - twi * f2i, twr * f2i + twi * f2r
    gs = jnp.concatenate([gre, gim], 1).astype(BF16)
    gts = jnp.concatenate([gre.transpose(0, 2, 1), gim.transpose(0, 2, 1)], 1).astype(BF16)
    return dict(n=n, n1=n1, n2=n2, f1=jnp.asarray(f1, BF16), f3=jnp.asarray(f3, BF16), gs=gs, gts=gts)


def hyena_long_conv(z, x0, filt2, norm, bias, tabs):
    b, l, c = z.shape
    n, n1, n2 = tabs["n"], tabs["n1"], tabs["n2"]
    w = n2 * c
    fa = dft_rows(tabs["f1"], filt2.reshape(1, n1, w), BF16)
    hf = filter_spectrum(tabs["gs"], fa.reshape(1, 2, n1, n2, c))
    a = dft_rows(tabs["f1"][:, : n1 // 2], z.reshape(b, n1 // 2, w), BF16)
    bm = spectral_multiply(tabs["gs"], tabs["gts"], hf, a.reshape(b, 2, n1, n2, c))
    tile = lambda v: jnp.tile(v.reshape(1, c), (1, n2))
    y = idft_rows_gate(tabs["f3"], bm.reshape(b, 2 * n1, w), x0.reshape(b, n1 // 2, w),
                       z.reshape(b, n1 // 2, w), tile(1.0 / norm), tile(bias), n)
    return y.reshape(b, l, c)


def _small_conv_kernel(ff_ref, fi_ref, filt_ref, z_ref, x0_ref, inv_ref, bias_ref, o_ref, *, l):
    ff = ff_ref[...]
    hs = _dot(ff, filt_ref[...].astype(BF16))
    hre, him = hs[:2 * l], hs[2 * l:]
    z = z_ref[0]
    xs = _dot(ff[:, :l], z.astype(BF16))
    xre, xim = xs[:2 * l], xs[2 * l:]
    yre = (xre * hre - xim * him).astype(BF16)
    yim = (xre * him + xim * hre).astype(BF16)
    y = _dot(fi_ref[...], jnp.concatenate([yre, yim], 0)) * (0.5 / l)
    o_ref[0] = (x0_ref[0] * (y * inv_ref[...] + bias_ref[...] * z)).astype(o_ref.dtype)


def small_long_conv(z, x0, filt2, norm, bias):
    b, l, c = z.shape
    k = np.arange(2 * l)
    ang = 2.0 * np.pi * np.outer(k, k) / (2 * l)
    ff = jnp.asarray(np.concatenate([np.cos(ang), -np.sin(ang)], 0), BF16)
    fi = jnp.asarray(np.concatenate([np.cos(ang), -np.sin(ang)], 1)[:l], BF16)
    full = lambda a: pl.BlockSpec(a.shape, lambda bi: (0,) * a.ndim)
    row = pl.BlockSpec((1, l, c), lambda bi: (bi, 0, 0))
    inv = 1.0 / norm
    return pl.pallas_call(
        functools.partial(_small_conv_kernel, l=l),
        grid=(b,),
        in_specs=[full(ff), full(fi), full(filt2), row, row, full(inv), full(bias)],
        out_specs=row,
        out_shape=jax.ShapeDtypeStruct((b, l, c), BF16),
        compiler_params=_cp("parallel"),
        name="small_long_conv",
    )(ff, fi, filt2, z, x0, inv, bias)


def _pool_kernel(p_ref, pp_ref, pn_ref, w_ref, sc_ref, o_ref, buf_ref, *, tm, l):
    _fill_padded(buf_ref, p_ref, pp_ref, pn_ref, tm)
    t = pl.program_id(1) * tm + lax.broadcasted_iota(jnp.int32, (tm, LANES), 0)
    lane = lax.broadcasted_iota(jnp.int32, (tm, LANES), 1)
    first_group = lane < POOL_GROUP
    h0 = POOL_HALO

    def win(col, lo, hi):
        acc = buf_ref[h0 + lo:h0 + lo + tm, col]
        for j in range(lo + 1, hi):
            acc = acc + buf_ref[h0 + j:h0 + j + tm, col]
        return acc

    parts = []
    for half, (wa, wb) in enumerate(((2, 4), (8, 16))):
        col = slice(half * LANES, (half + 1) * LANES)
        sa = win(col, -wa // 2, wa // 2)
        sb = sa + win(col, -wb // 2, -wa // 2) + win(col, wa // 2, wb // 2)
        hw = jnp.where(first_group, wa // 2, wb // 2)
        cnt = jnp.minimum(t + hw, l) - jnp.maximum(t - hw, 0)
        mean = jnp.where(first_group, sa, sb) / cnt.astype(F32)
        parts.append(mean - buf_ref[h0:h0 + tm, col])
    d = jnp.concatenate(parts, axis=-1).astype(BF16)
    o_ref[0] = (_dot(d, w_ref[...]) * sc_ref[...]).astype(o_ref.dtype)


def pool_mix(p, pool_w, pool_scale):
    b, l, wd = p.shape
    tm = min(512, l)
    nblk = l // tm
    wblk = jax.scipy.linalg.block_diag(*[pool_w[g] for g in range(len(POOL_WINDOWS))]).astype(BF16)
    return pl.pallas_call(
        functools.partial(_pool_kernel, tm=tm, l=l),
        grid=(b, nblk),
        in_specs=_halo_specs(tm, wd, nblk) + [pl.BlockSpec((wd, wd), lambda bi, i: (0, 0)),
                                              pl.BlockSpec((1, wd), lambda bi, i: (0, 0))],
        out_specs=pl.BlockSpec((1, tm, wd), lambda bi, i: (bi, i, 0)),
        out_shape=jax.ShapeDtypeStruct((b, l, wd), BF16),
        scratch_shapes=[pltpu.VMEM((tm + 2 * POOL_HALO, wd), F32)],
        compiler_params=_cp("parallel", "parallel"),
        name="pool_mix",
    )(p, p, p, wblk, pool_scale.reshape(1, wd))


def _outproj_kernel(x_ref, hy_ref, pool_ref, att_ref, w_ref, gate_ref, o_ref):
    a, b2 = HY_WIDTH, HY_WIDTH + POOL_WIDTH
    y = (_dot(hy_ref[0], w_ref[0:a]) + _dot(pool_ref[0], w_ref[a:b2]) + _dot(att_ref[0], w_ref[b2:]))
    o_ref[0] = x_ref[0] + gate_ref[0] * y


def out_projection(x, y_hy, y_pool, y_att, w_out, gate):
    b, l, d = x.shape
    tm = min(512, l)
    row = lambda bi, i: (bi, i, 0)
    return pl.pallas_call(
        _outproj_kernel,
        grid=(b, l // tm),
        in_specs=[pl.BlockSpec((1, tm, d), row),
                  pl.BlockSpec((1, tm, y_hy.shape[2]), row),
                  pl.BlockSpec((1, tm, y_pool.shape[2]), row),
                  pl.BlockSpec((1, tm, y_att.shape[2]), row),
                  pl.BlockSpec(w_out.shape, lambda bi, i: (0, 0)),
                  pl.BlockSpec((1, 1, d), lambda bi, i: (bi, 0, 0))],
        out_specs=pl.BlockSpec((1, tm, d), row),
        out_shape=jax.ShapeDtypeStruct(x.shape, F32),
        compiler_params=_cp("parallel", "parallel"),
        name="out_projection",
    )(x, y_hy, y_pool, y_att, w_out, gate)


FF_CHUNK = 256
MOE_TILE = 1024
MOE_ROWS = (128, 192, 256, 320, 384)


def _swiglu_acc(h, wg_ref, wu_ref, wd_ref, lead):
    ff = wg_ref.shape[-1]
    acc = None
    for f0 in range(0, ff, FF_CHUNK):
        cs = slice(f0, min(f0 + FF_CHUNK, ff))
        g = _dot(h, wg_ref[lead + (slice(None), cs)])
        u = _dot(h, wu_ref[lead + (slice(None), cs)])
        a = (g * jax.nn.sigmoid(g) * u).astype(BF16)
        part = _dot(a, wd_ref[lead + (cs, slice(None))])
        acc = part if acc is None else acc + part
    return acc


def _ffn_kernel(x_ref, g_ref, sh_ref, sc_ref, gate_ref, wg_ref, wu_ref, wd_ref, o_ref):
    x = x_ref[0]
    h = _normmod(x, g_ref[...], sh_ref[0], sc_ref[0]).astype(BF16)
    o_ref[0] = x + gate_ref[0] * _swiglu_acc(h, wg_ref, wu_ref, wd_ref, ())


def ffn_dense(x, g, shift, scale, gate, wg, wu, wd):
    b, l, d = x.shape
    tm = min(512, l)
    row = lambda bi, i: (bi, i, 0)
    vec = lambda bi, i: (bi, 0, 0)
    wspec = lambda w: pl.BlockSpec(w.shape, lambda bi, i: (0, 0), pipeline_mode=pl.Buffered(1))
    return pl.pallas_call(
        _ffn_kernel,
        grid=(b, l // tm),
        in_specs=[pl.BlockSpec((1, tm, d), row),
                  pl.BlockSpec((1, d), lambda bi, i: (0, 0)),
                  pl.BlockSpec((1, 1, d), vec), pl.BlockSpec((1, 1, d), vec), pl.BlockSpec((1, 1, d), vec),
                  wspec(wg), wspec(wu), wspec(wd)],
        out_specs=pl.BlockSpec((1, tm, d), row),
        out_shape=jax.ShapeDtypeStruct(x.shape, F32),
        compiler_params=_cp("parallel", "parallel"),
        name="ffn_dense",
    )(x, g, shift, scale, gate, wg, wu, wd)


def _router_kernel(x_ref, g_ref, sh_ref, sc_ref, wr_ref, tri_ref, comb_ref, rank_ref):
    h = _normmod(x_ref[0], g_ref[...], sh_ref[0], sc_ref[0])
    logits = _dot(h, wr_ref[...], precision=HIGHEST)
    lane = lax.broadcasted_iota(jnp.int32, logits.shape, 1)
    neg = jnp.float32(-jnp.inf)
    lg = jnp.where(lane < N_EXPERTS, logits, neg)
    m1 = jnp.max(lg, axis=-1, keepdims=True)
    i1 = jnp.min(jnp.where(lg == m1, lane, LANES), axis=-1, keepdims=True)
    lg2 = jnp.where(lane == i1, neg, lg)
    m2 = jnp.max(lg2, axis=-1, keepdims=True)
    i2 = jnp.min(jnp.where(lg2 == m2, lane, LANES), axis=-1, keepdims=True)
    e = jnp.exp(m2 - m1)
    g1 = 1.0 / (1.0 + e)
    comb_ref[0] = jnp.where(lane == i1, g1, jnp.where(lane == i2, e * g1, 0.0))
    routed = (lane == i1) | (lane == i2)
    before = _dot(tri_ref[...], jnp.where(routed, 1.0, 0.0).astype(BF16))
    rank_ref[0] = jnp.where(routed, before, -1.0)


def moe_route(x, g, shift, scale, w_router):
    b, l, d = x.shape
    tm = min(MOE_TILE, l)
    wr = jnp.concatenate([w_router, jnp.zeros((d, LANES - N_EXPERTS), F32)], -1)
    tri = jnp.asarray(np.tril(np.ones((tm, tm), np.float32), -1), BF16)
    row = lambda bi, i: (bi, i, 0)
    vec = lambda bi, i: (bi, 0, 0)
    return pl.pallas_call(
        _router_kernel,
        grid=(b, l // tm),
        in_specs=[pl.BlockSpec((1, tm, d), row),
                  pl.BlockSpec((1, d), lambda bi, i: (0, 0)),
                  pl.BlockSpec((1, 1, d), vec), pl.BlockSpec((1, 1, d), vec),
                  pl.BlockSpec((d, LANES), lambda bi, i: (0, 0)),
                  pl.BlockSpec((tm, tm), lambda bi, i: (0, 0))],
        out_specs=[pl.BlockSpec((1, tm, LANES), row)] * 2,
        out_shape=[jax.ShapeDtypeStruct((b, l, LANES), F32)] * 2,
        compiler_params=_cp("parallel", "parallel"),
        name="moe_router",
    )(x, g, shift, scale, wr, tri)


def _moe_kernel(x_ref, g_ref, sh_ref, sc_ref, gate_ref, comb_ref, wg_ref, wu_ref, wd_ref, o_ref, acc_ref):
    e = pl.program_id(2)
    x = x_ref[0]
    h = _normmod(x, g_ref[...], sh_ref[0], sc_ref[0]).astype(BF16)
    comb = comb_ref[0]
    lane = lax.broadcasted_iota(jnp.int32, comb.shape, 1)
    ce = jnp.sum(jnp.where(lane == e, comb, 0.0), axis=-1, keepdims=True)
    y = ce * _swiglu_acc(h, wg_ref, wu_ref, wd_ref, (0,))

    @pl.when(e == 0)
    def _():
        acc_ref[...] = y

    @pl.when(e > 0)
    def _():
        acc_ref[...] += y

    @pl.when(e == N_EXPERTS - 1)
    def _():
        o_ref[0] = x + gate_ref[0] * acc_ref[...]


def moe_dense(x, g, shift, scale, gate, comb, wg, wu, wd):
    b, l, d = x.shape
    tm = min(512, l)
    ff = wg.shape[-1]
    row = lambda bi, i, e: (bi, i, 0)
    vec = lambda bi, i, e: (bi, 0, 0)
    return pl.pallas_call(
        _moe_kernel,
        grid=(b, l // tm, N_EXPERTS),
        in_specs=[pl.BlockSpec((1, tm, d), row),
                  pl.BlockSpec((1, d), lambda bi, i, e: (0, 0)),
                  pl.BlockSpec((1, 1, d), vec), pl.BlockSpec((1, 1, d), vec), pl.BlockSpec((1, 1, d), vec),
                  pl.BlockSpec((1, tm, LANES), row),
                  pl.BlockSpec((1, d, ff), lambda bi, i, e: (e, 0, 0)),
                  pl.BlockSpec((1, d, ff), lambda bi, i, e: (e, 0, 0)),
                  pl.BlockSpec((1, ff, d), lambda bi, i, e: (e, 0, 0))],
        out_specs=pl.BlockSpec((1, tm, d), row),
        out_shape=jax.ShapeDtypeStruct(x.shape, F32),
        scratch_shapes=[pltpu.VMEM((tm, d), F32)],
        compiler_params=_cp("parallel", "parallel", "arbitrary"),
        name="moe_dense",
    )(x, g, shift, scale, gate, comb, wg, wu, wd)


def _moe_sparse_kernel(cnt_ref, x_ref, g_ref, sh_ref, sc_ref, gate_ref, comb_ref, rank_ref, rankt_ref,
                       wg_ref, wu_ref, wd_ref, *rest, t, sizes):
    fg_ref = rest[0] if len(rest) == 3 else None
    o_ref, h_scr = rest[-2:]
    i, e = pl.program_id(0), pl.program_id(1)
    cnt = cnt_ref[i * N_EXPERTS + e]

    @pl.when(e == 0)
    def _():
        x = x_ref[0]
        h_scr[...] = _normmod(x, g_ref[...], sh_ref[0], sc_ref[0]).astype(BF16)
        o_ref[0] = x

    rrow = rankt_ref[0, pl.ds(e, 1), :]
    lane = lax.broadcasted_iota(jnp.int32, (t, LANES), 1)
    rcol = jnp.sum(jnp.where(lane == e, rank_ref[0], 0.0), axis=-1, keepdims=True)
    ccol = jnp.sum(jnp.where(lane == e, comb_ref[0], 0.0), axis=-1, keepdims=True)

    def block(base, r):
        rid = (base + lax.broadcasted_iota(jnp.int32, (r, 1), 0)).astype(F32)
        onehot = jnp.where(rrow == rid, 1.0, 0.0).astype(BF16)
        xg = _dot(onehot, h_scr[...]).astype(BF16)
        y = _swiglu_acc(xg, wg_ref, wu_ref, wd_ref, (0,)).astype(BF16)
        cid = (base + lax.broadcasted_iota(jnp.int32, (1, r), 1)).astype(F32)
        onehot_t = jnp.where(rcol == cid, 1.0, 0.0).astype(BF16)
        o_ref[0] += (gate_ref[0] * ccol) * _dot(onehot_t, y)

    big = sizes[-1]
    nfull = cnt // big

    def full(k, carry):
        block(k * big, big)
        return carry
    lax.fori_loop(0, nfull, full, 0)
    base = nfull * big
    rem = cnt - base
    for lo, r in zip((0,) + sizes[:-1], sizes):
        @pl.when((rem > lo) & (rem <= r))
        def _(r=r):
            block(base, r)

    if fg_ref is not None:
        @pl.when(e == N_EXPERTS - 1)
        def _():
            o = o_ref[0]
            o_ref[0] = o * lax.rsqrt(jnp.mean(o * o, axis=-1, keepdims=True) + EPS) * fg_ref[...]


def moe_sparse(x, g, shift, scale, gate, comb, rank, wg, wu, wd, final_g=None):
    b, l, d = x.shape
    t = MOE_TILE
    tpb = l // t
    n_tiles = b * tpb
    ff = wg.shape[-1]
    rk = rank[..., :N_EXPERTS].reshape(n_tiles, t, N_EXPERTS)
    rank_t = rk.transpose(0, 2, 1)
    cnt = jnp.sum(rk >= 0, axis=1).astype(jnp.int32).reshape(n_tiles * N_EXPERTS)
    row = lambda i, e, c: (i // tpb, i % tpb, 0)
    vec = lambda i, e, c: (i // tpb, 0, 0)
    grid_spec = pltpu.PrefetchScalarGridSpec(
        num_scalar_prefetch=1,
        grid=(n_tiles, N_EXPERTS),
        in_specs=[pl.BlockSpec((1, t, d), row, pipeline_mode=pl.Buffered(1)),
                  pl.BlockSpec((1, d), lambda i, e, c: (0, 0)),
                  pl.BlockSpec((1, 1, d), vec), pl.BlockSpec((1, 1, d), vec), pl.BlockSpec((1, 1, d), vec),
                  pl.BlockSpec((1, t, LANES), row, pipeline_mode=pl.Buffered(1)),
                  pl.BlockSpec((1, t, LANES), row, pipeline_mode=pl.Buffered(1)),
                  pl.BlockSpec((1, N_EXPERTS, t), lambda i, e, c: (i, 0, 0)),
                  pl.BlockSpec((1, d, ff), lambda i, e, c: (e, 0, 0)),
                  pl.BlockSpec((1, d, ff), lambda i, e, c: (e, 0, 0)),
                  pl.BlockSpec((1, ff, d), lambda i, e, c: (e, 0, 0))]
        + ([] if final_g is None else [pl.BlockSpec((1, d), lambda i, e, c: (0, 0))]),
        out_specs=pl.BlockSpec((1, t, d), row),
        scratch_shapes=[pltpu.VMEM((t, d), BF16)],
    )
    return pl.pallas_call(
        functools.partial(_moe_sparse_kernel, t=t, sizes=MOE_ROWS),
        grid_spec=grid_spec,
        out_shape=jax.ShapeDtypeStruct(x.shape, F32),
        compiler_params=_cp("parallel", "arbitrary"),
        name="moe_sparse",
    )(cnt, x, g, shift, scale, gate, comb, rank, rank_t, wg, wu, wd, *(() if final_g is None else (final_g,)))


def _final_kernel(x_ref, g_ref, o_ref):
    x = x_ref[0]
    o_ref[0] = x * lax.rsqrt(jnp.mean(x * x, axis=-1, keepdims=True) + EPS) * g_ref[...]


def final_norm(x, g):
    b, l, d = x.shape
    tm = min(1024, l)
    return pl.pallas_call(
        _final_kernel,
        grid=(b, l // tm),
        in_specs=[pl.BlockSpec((1, tm, d), lambda bi, i: (bi, i, 0)),
                  pl.BlockSpec((1, d), lambda bi, i: (0, 0))],
        out_specs=pl.BlockSpec((1, tm, d), lambda bi, i: (bi, i, 0)),
        out_shape=jax.ShapeDtypeStruct(x.shape, F32),
        compiler_params=_cp("parallel", "parallel"),
        name="final_norm",
    )(x, g)


def kernel(x, c, ctx, c_ctx, norm1_g, norm2_g, w_ada, b_ada, w_in, hy_short_w, hy_short_b, filt_w1, filt_b1, filt_freq1, filt_w2, filt_b2, filt_freq2, filt_w3, hy_bias, pool_w, pool_scale, q_norm_g, w_uq, kv_norm_g, w_ukv, w_out, ffn_wg, ffn_wu, ffn_wd, moe_router, moe_wg, moe_wu, moe_wd, final_g):
    b, l, d = x.shape
    lc = ctx.shape[1]
    depth = w_ada.shape[0]

    cc = jnp.zeros((8, d), F32).at[:b].set(c).at[b].set(c_ctx)
    mods = ada_vectors(cc, w_ada, b_ada)

    rope_l = rope_tables(l, True)
    rope_c = rope_tables(lc, False)
    ftab_l, ftab_c = filter_tables(l), filter_tables(lc)
    dtabs = dft_tables(l)

    xl, xc = x, ctx
    for layer in range(depth):
        last = layer == depth - 1
        mod = mods[layer].reshape(8, 6, 1, d)
        mod_l = [mod[:b, j] for j in range(6)]
        mod_c = [jnp.broadcast_to(mod[b:b + 1, j], (b, 1, d)) for j in range(6)]
        g1 = norm1_g[layer].reshape(1, d)
        g2 = norm2_g[layer].reshape(1, d)
        w_in_l = w_in[layer].astype(BF16)
        w_out_l = w_out[layer].astype(BF16)
        mw = mla_weights(q_norm_g[layer], w_uq[layer], kv_norm_g[layer], w_ukv[layer])
        fw = (filt_w1[layer], filt_b1[layer], filt_freq1[layer], filt_w2[layer], filt_b2[layer],
              filt_freq2[layer], filt_w3[layer])
        bias = hy_bias[layer].reshape(1, HY_WIDTH)

        def channel_mix(xs, md, final=False):
            i = layer // 2
            fg = final_g.reshape(1, d)
            if layer % 2 == 1 and xs.shape[1] % MOE_TILE == 0:
                comb, rank = moe_route(xs, g2, md[3], md[4], moe_router[i])
                return moe_sparse(xs, g2, md[3], md[4], md[5], comb, rank, moe_wg[i].astype(BF16),
                                  moe_wu[i].astype(BF16), moe_wd[i].astype(BF16), fg if final else None)
            if layer % 2 == 0:
                out = ffn_dense(xs, g2, md[3], md[4], md[5], ffn_wg[i].astype(BF16),
                                ffn_wu[i].astype(BF16), ffn_wd[i].astype(BF16))
            else:
                comb, _ = moe_route(xs, g2, md[3], md[4], moe_router[i])
                out = moe_dense(xs, g2, md[3], md[4], md[5], comb, moe_wg[i].astype(BF16),
                                moe_wu[i].astype(BF16), moe_wd[i].astype(BF16))
            return final_norm(out, fg) if final else out

        hy_l, pool_l, q_l, kt_l, v_l = in_projection(xl, g1, mod_l[0], mod_l[1], w_in_l, *rope_l, mw)
        hy_c, pool_c, q_c, kt_c, v_c = in_projection(xc, g1, mod_c[0], mod_c[1], w_in_l, *rope_c, mw)

        att_l = attention(q_l, kt_c, v_c, kt_l, v_l)
        z_l, x0_l = hyena_pre(hy_l, hy_short_w[layer], hy_short_b[layer])
        filt_l, nrm_l = hyena_filter(l, ftab_l, *fw)
        yhy_l = hyena_long_conv(z_l, x0_l, filt_l, nrm_l, bias, dtabs)
        ypool_l = pool_mix(pool_l, pool_w[layer], pool_scale[layer])
        xl = out_projection(xl, yhy_l, ypool_l, att_l, w_out_l, mod_l[2])

        if not last:
            att_c = attention(q_c, kt_c, v_c)
            z_c, x0_c = hyena_pre(hy_c, hy_short_w[layer], hy_short_b[layer])
            filt_c, nrm_c = hyena_filter(lc, ftab_c, *fw)
            yhy_c = small_long_conv(z_c, x0_c, filt_c, nrm_c, bias)
            ypool_c = pool_mix(pool_c, pool_w[layer], pool_scale[layer])
            xc = out_projection(xc, yhy_c, ypool_c, att_c, w_out_l, mod_c[2])
            xc = channel_mix(xc, mod_c)

        xl = channel_mix(xl, mod_l, final=last)
    return xl
```

```python
import functools
import math

import jax
import jax.numpy as jnp
import numpy as np
from jax import lax
from jax.experimental import pallas as pl
from jax.experimental.pallas import tpu as pltpu

F32 = jnp.float32
BF16 = jnp.bfloat16
HIGHEST = lax.Precision.HIGHEST

EPS = 1e-6
GRID_W = 64
HY_WIDTH = 256
POOL_WINDOWS = (2, 4, 8, 16)
POOL_WIDTH = 256
POOL_GROUP = 64
POOL_HALO = 8
MLA_HEADS = 8
QK_NOPE = 64
QK_ROPE = 32
V_DIM = 64
Q_LORA = 384
KV_LORA = 256
HEAD_PAD = 128
SM_SCALE = (QK_NOPE + QK_ROPE) ** -0.5
ROPE_THETA = 10000.0
FILT_EMB = 33
FILT_BANDS = 16
DECAY_TARGET = 1e-2
FAST_DECAY_PCT = 0.3
SLOW_DECAY_PCT = 1.5
N_EXPERTS = 8
LANES = 128
FFT_N2 = 128
VMEM_LIMIT = 56 * 2 ** 20
ROW_TILE = 512
DFT_COLS = 2048
ADA_COLS = 1536


def _cp(*sem):
    return pltpu.CompilerParams(dimension_semantics=sem, vmem_limit_bytes=VMEM_LIMIT)


def _dot(a, b, **kw):
    return jnp.dot(a, b, preferred_element_type=F32, **kw)


def _normmod(x, g, shift, scale):
    ms = jnp.mean(x * x, axis=-1, keepdims=True)
    return x * lax.rsqrt(ms + EPS) * (g * (1.0 + scale)) + shift


def _ada_kernel(c_ref, w_ref, b_ref, o_ref):
    c = c_ref[...]
    s = c * jax.nn.sigmoid(c)
    o_ref[0] = _dot(s, w_ref[0], precision=HIGHEST) + b_ref[0]


def ada_vectors(cc, w_ada, b_ada):
    depth, d, n = w_ada.shape
    tn = ADA_COLS
    return pl.pallas_call(
        _ada_kernel,
        grid=(depth, n // tn),
        in_specs=[pl.BlockSpec((8, d), lambda l, j: (0, 0)),
                  pl.BlockSpec((1, d, tn), lambda l, j: (l, 0, j)),
                  pl.BlockSpec((1, 1, tn), lambda l, j: (l, 0, j))],
        out_specs=pl.BlockSpec((1, 8, tn), lambda l, j: (l, 0, j)),
        out_shape=jax.ShapeDtypeStruct((depth, 8, n), F32),
        compiler_params=_cp("parallel", "parallel"),
        name="ada_vectors",
    )(cc, w_ada, b_ada.reshape(depth, 1, n))


IN_HY, IN_POOL = 768, 256
IN_OFFS = (0, 768, 1024, 1408, 1664, 1696)


def _inproj_kernel(x_ref, g_ref, sh_ref, sc_ref, w_ref, c_ref, s_ref, wqa_ref, wqb_ref, wk_ref, wv_ref,
                   pa_ref, pb_ref, hy_ref, pool_ref, q_ref, kt_ref, v_ref):
    h = _normmod(x_ref[0], g_ref[...], sh_ref[0], sc_ref[0]).astype(BF16)
    y = _dot(h, w_ref[...])
    o = IN_OFFS
    hy_ref[0] = y[:, o[0]:o[1]]
    pool_ref[0] = y[:, o[1]:o[2]]
    cq, ckv = y[:, o[2]:o[3]], y[:, o[3]:o[4]]
    kpe = y[:, o[4]:o[5]].astype(BF16)
    cqn = (cq * lax.rsqrt(jnp.mean(cq * cq, axis=-1, keepdims=True) + EPS)).astype(BF16)
    ckvn = (ckv * lax.rsqrt(jnp.mean(ckv * ckv, axis=-1, keepdims=True) + EPS)).astype(BF16)
    cs, sn = c_ref[...], s_ref[...]
    qa = _dot(cqn, wqa_ref[...])
    qb = _dot(cqn, wqb_ref[...])
    kn = _dot(ckvn, wk_ref[...])
    vv = _dot(ckvn, wv_ref[...])
    kpa = _dot(kpe, pa_ref[...])
    kpb = _dot(kpe, pb_ref[...])
    kpe_rot = kpa * cs + kpb * sn
    lane = lax.broadcasted_iota(jnp.int32, (1, HEAD_PAD), 1)
    for hd in range(MLA_HEADS):
        sl = slice(hd * HEAD_PAD, (hd + 1) * HEAD_PAD)
        q_ref[0, hd] = (qa[:, sl] * cs + qb[:, sl] * sn).astype(BF16)
        kt_ref[0, hd] = (kn[:, sl] + kpe_rot).T.astype(BF16)
        one_lane = V_DIM if hd % 2 == 0 else 0
        v_ref[0, hd] = jnp.where(lane == one_lane, 1.0, vv[:, sl]).astype(BF16)


def in_projection(x, g, shift, scale, w_in, ctab, stab, wts):
    b, l, d = x.shape
    tm = min(ROW_TILE, l)
    hw = MLA_HEADS
    row = lambda bi, i: (bi, i, 0)
    vec = lambda bi, i: (bi, 0, 0)
    full = lambda a: pl.BlockSpec(a.shape, lambda bi, i: (0,) * a.ndim)
    return pl.pallas_call(
        _inproj_kernel,
        grid=(b, l // tm),
        in_specs=[pl.BlockSpec((1, tm, d), row),
                  pl.BlockSpec((1, d), lambda bi, i: (0, 0)),
                  pl.BlockSpec((1, 1, d), vec),
                  pl.BlockSpec((1, 1, d), vec),
                  full(w_in),
                  pl.BlockSpec((tm, HEAD_PAD), lambda bi, i: (i, 0)),
                  pl.BlockSpec((tm, HEAD_PAD), lambda bi, i: (i, 0))] + [full(w) for w in wts],
        out_specs=[pl.BlockSpec((1, tm, IN_HY), row),
                   pl.BlockSpec((1, tm, IN_POOL), row),
                   pl.BlockSpec((1, hw, tm, HEAD_PAD), lambda bi, i: (bi, 0, i, 0)),
                   pl.BlockSpec((1, hw, HEAD_PAD, tm), lambda bi, i: (bi, 0, 0, i)),
                   pl.BlockSpec((1, hw, tm, HEAD_PAD), lambda bi, i: (bi, 0, i, 0))],
        out_shape=[jax.ShapeDtypeStruct((b, l, IN_HY), F32),
                   jax.ShapeDtypeStruct((b, l, IN_POOL), F32),
                   jax.ShapeDtypeStruct((b, hw, l, HEAD_PAD), BF16),
                   jax.ShapeDtypeStruct((b, hw, HEAD_PAD, l), BF16),
                   jax.ShapeDtypeStruct((b, hw, l, HEAD_PAD), BF16)],
        compiler_params=_cp("parallel", "parallel"),
        name="in_projection",
    )(x, g, shift, scale, w_in, ctab, stab, *wts)


def mla_weights(q_norm_g, w_uq, kv_norm_g, w_ukv):
    hw, dq = MLA_HEADS, QK_NOPE + QK_ROPE
    half = QK_ROPE // 2
    wq = (w_uq * q_norm_g[:, None] * (SM_SCALE * math.log2(math.e))).reshape(Q_LORA, hw, dq)
    pad = jnp.zeros((Q_LORA, hw, HEAD_PAD - dq), F32)
    wqa = jnp.concatenate([wq, pad], -1)
    swap = jnp.concatenate([wq[..., QK_NOPE + half:], wq[..., QK_NOPE:QK_NOPE + half]], -1)
    wqb = jnp.concatenate([jnp.zeros((Q_LORA, hw, QK_NOPE), F32), swap, pad], -1)
    wkv = (w_ukv * kv_norm_g[:, None]).reshape(KV_LORA, hw, QK_NOPE + V_DIM)
    z64 = jnp.zeros((KV_LORA, hw, 64), F32)
    wk = jnp.concatenate([wkv[..., :QK_NOPE], z64], -1)
    v = wkv[..., QK_NOPE:]
    wv = jnp.where((jnp.arange(hw) % 2 == 0)[None, :, None],
                   jnp.concatenate([v, z64], -1), jnp.concatenate([z64, v], -1))
    eye = np.eye(QK_ROPE, dtype=np.float32)
    pa = np.zeros((QK_ROPE, HEAD_PAD), np.float32)
    pa[:, QK_NOPE:QK_NOPE + QK_ROPE] = eye
    pb = np.zeros((QK_ROPE, HEAD_PAD), np.float32)
    pb[:, QK_NOPE:QK_NOPE + QK_ROPE] = np.concatenate([eye[:, half:], eye[:, :half]], 1)
    flat = lambda w: w.reshape(w.shape[0], hw * HEAD_PAD).astype(BF16)
    return (flat(wqa), flat(wqb), flat(wk), flat(wv), jnp.asarray(pa, BF16), jnp.asarray(pb, BF16))


def rope_tables(l, with_rope):
    ctab = np.zeros((l, HEAD_PAD), np.float32)
    stab = np.zeros((l, HEAD_PAD), np.float32)
    ctab[:, :QK_NOPE + QK_ROPE] = 1.0
    if not with_rope:
        return jnp.asarray(ctab), jnp.asarray(stab)
    n_freq = QK_ROPE // 4
    inv_freq = 1.0 / (ROPE_THETA ** (jnp.arange(n_freq, dtype=F32) / n_freq))
    rows = l // GRID_W
    row = jnp.repeat(jnp.arange(rows, dtype=F32), GRID_W)
    col = jnp.tile(jnp.arange(GRID_W, dtype=F32), rows)
    ang = jnp.concatenate([row[:, None] * inv_freq, col[:, None] * inv_freq], axis=-1)
    cos, sin = jnp.cos(ang), jnp.sin(ang)
    one = jnp.ones((l, QK_NOPE), F32)
    zero = jnp.zeros((l, QK_NOPE), F32)
    tail = jnp.zeros((l, HEAD_PAD - QK_NOPE - QK_ROPE), F32)
    return (jnp.concatenate([one, cos, cos, tail], -1), jnp.concatenate([zero, -sin, sin, tail], -1))


ATT_TQ = 1024
ATT_RB = 1024
ATT_CK = 2048


def _row_max(s):
    mp = s[:, 0:LANES]
    for j in range(1, s.shape[1] // LANES):
        mp = jnp.maximum(mp, s[:, j * LANES:(j + 1) * LANES])
    return jnp.max(mp, axis=-1, keepdims=True)


def _attn_kernel(q_ref, kct_ref, vc_ref, *rest, n_chunks, ck, tq):
    if n_chunks:
        kt_ref, v_ref, o_ref = rest
    else:
        (o_ref,) = rest
    rb = min(ATT_RB, tq)
    chains = [(hh, r) for hh in range(2) for r in range(tq // rb)]
    lane = lax.broadcasted_iota(jnp.int32, (1, HEAD_PAD), 1)

    def q_rows(hh, r):
        return q_ref[0, hh, r * rb:(r + 1) * rb, :]

    state = []
    for hh, r in chains:
        s = _dot(q_rows(hh, r), kct_ref[0, hh])
        m = _row_max(s)
        p = jnp.exp2(s - m).astype(BF16)
        state += [m, _dot(p, vc_ref[0, hh])]

    if n_chunks:
        def body(c, state):
            off = pl.multiple_of(c * ck, ck)
            out = []
            for i, (hh, r) in enumerate(chains):
                m, acc = state[2 * i], state[2 * i + 1]
                s = _dot(q_rows(hh, r), kt_ref[0, hh, :, pl.ds(off, ck)])
                m_new = jnp.maximum(m, _row_max(s))
                alpha = jnp.exp2(m - m_new)
                p = jnp.exp2(s - m_new).astype(BF16)
                out += [m_new, acc * alpha + _dot(p, v_ref[0, hh, pl.ds(off, ck), :])]
            return tuple(out)
        state = lax.fori_loop(0, n_chunks, body, tuple(state), unroll=2 if n_chunks % 2 == 0 else 1)

    for r in range(tq // rb):
        outs = []
        for hh in range(2):
            acc = state[2 * chains.index((hh, r)) + 1]
            one_lane = V_DIM if hh == 0 else 0
            denom = jnp.sum(jnp.where(lane == one_lane, acc, 0.0), axis=-1, keepdims=True)
            outs.append(acc / denom)
        o_ref[0, r * rb:(r + 1) * rb, :] = jnp.where(lane < V_DIM, outs[0], outs[1]).astype(o_ref.dtype)


def attention(q, kct, vc, kt=None, v=None):
    b, hw, lq, _ = q.shape
    lc = vc.shape[2]
    tq = min(ATT_TQ, lq)
    ck = ATT_CK if kt is None else min(ATT_CK, kt.shape[3])
    n_chunks = 0 if kt is None else kt.shape[3] // ck
    in_specs = [pl.BlockSpec((1, 2, tq, HEAD_PAD), lambda bi, hp, i: (bi, hp, i, 0)),
                pl.BlockSpec((1, 2, HEAD_PAD, lc), lambda bi, hp, i: (bi, hp, 0, 0)),
                pl.BlockSpec((1, 2, lc, HEAD_PAD), lambda bi, hp, i: (bi, hp, 0, 0))]
    args = [q, kct, vc]
    if n_chunks:
        l = kt.shape[3]
        in_specs += [pl.BlockSpec((1, 2, HEAD_PAD, l), lambda bi, hp, i: (bi, hp, 0, 0)),
                     pl.BlockSpec((1, 2, l, HEAD_PAD), lambda bi, hp, i: (bi, hp, 0, 0))]
        args += [kt, v]
    return pl.pallas_call(
        functools.partial(_attn_kernel, n_chunks=n_chunks, ck=ck, tq=tq),
        grid=(b, hw // 2, lq // tq),
        in_specs=in_specs,
        out_specs=pl.BlockSpec((1, tq, HEAD_PAD), lambda bi, hp, i: (bi, i, hp)),
        out_shape=jax.ShapeDtypeStruct((b, lq, hw * V_DIM), BF16),
        compiler_params=_cp("parallel", "parallel", "arbitrary"),
        name="attention",
    )(*args)


def _halo_specs(tm, width, nblk):
    r = tm // POOL_HALO
    last = nblk * r - 1
    return [pl.BlockSpec((1, tm, width), lambda bi, i: (bi, i, 0)),
            pl.BlockSpec((1, POOL_HALO, width), lambda bi, i: (bi, jnp.maximum(i * r - 1, 0), 0)),
            pl.BlockSpec((1, POOL_HALO, width), lambda bi, i: (bi, jnp.minimum((i + 1) * r, last), 0))]


def _fill_padded(buf_ref, cur_ref, prev_ref, next_ref, tm):
    i = pl.program_id(1)
    first = i == 0
    last = i == pl.num_programs(1) - 1
    buf_ref[0:POOL_HALO] = jnp.where(first, 0.0, prev_ref[0])
    buf_ref[POOL_HALO:POOL_HALO + tm] = cur_ref[0]
    buf_ref[POOL_HALO + tm:] = jnp.where(last, 0.0, next_ref[0])


def _hyena_pre_kernel(u_ref, up_ref, un_ref, w_ref, b_ref, z_ref, x0_ref, buf_ref, *, tm):
    _fill_padded(buf_ref, u_ref, up_ref, un_ref, tm)
    w = w_ref[...]
    uc = (buf_ref[POOL_HALO - 1:POOL_HALO - 1 + tm] * w[0:1] + buf_ref[POOL_HALO:POOL_HALO + tm] * w[1:2]
          + buf_ref[POOL_HALO + 1:POOL_HALO + 1 + tm] * w[2:3] + b_ref[...])
    x0_ref[0] = uc[:, :HY_WIDTH]
    z_ref[0] = uc[:, HY_WIDTH:2 * HY_WIDTH] * uc[:, 2 * HY_WIDTH:]


def hyena_pre(u, sw, sb):
    b, l, wd = u.shape
    tm = min(ROW_TILE, l)
    nblk = l // tm
    return pl.pallas_call(
        functools.partial(_hyena_pre_kernel, tm=tm),
        grid=(b, nblk),
        in_specs=_halo_specs(tm, wd, nblk) + [pl.BlockSpec((3, wd), lambda bi, i: (0, 0)),
                                              pl.BlockSpec((1, wd), lambda bi, i: (0, 0))],
        out_specs=[pl.BlockSpec((1, tm, HY_WIDTH), lambda bi, i: (bi, i, 0))] * 2,
        out_shape=[jax.ShapeDtypeStruct((b, l, HY_WIDTH), F32)] * 2,
        scratch_shapes=[pltpu.VMEM((tm + 2 * POOL_HALO, wd), F32)],
        compiler_params=_cp("parallel", "parallel"),
        name="hyena_pre",
    )(u, u, u, sw, sb.reshape(1, wd))


FILT_TILE = 512


def _filter_kernel(feat_ref, tdec_ref, w1_ref, b1_ref, f1_ref, w2_ref, b2_ref, f2_ref, w3_ref, dl_ref,
                   filt_ref, norm_ref, *, l, tr):
    i = pl.program_id(0)
    half = tr // 2
    h = jnp.sin(f1_ref[...] * (_dot(feat_ref[...], w1_ref[...], precision=HIGHEST) + b1_ref[...]))
    h = jnp.sin(f2_ref[...] * (_dot(h, w2_ref[...], precision=HIGHEST) + b2_ref[...]))
    h = _dot(h, w3_ref[...], precision=HIGHEST)
    dl = jnp.abs(dl_ref[...])
    total = jnp.zeros((1, HY_WIDTH), F32)
    for p in range(2):
        rows = slice(p * half, (p + 1) * half)
        hp = h[:, p * 2 * HY_WIDTH:(p + 1) * 2 * HY_WIDTH]
        decay = jnp.exp(-tdec_ref[rows] * dl)
        n = i * tr + p * half + lax.broadcasted_iota(jnp.int32, (half, 1), 0)
        f = jnp.where(n < l, hp[:, :HY_WIDTH], hp[:, HY_WIDTH:]) * decay
        f = jnp.where(n == l, 0.0, f)
        filt_ref[rows] = f
        total = total + jnp.sum(jnp.abs(f), axis=0, keepdims=True)

    @pl.when(i == 0)
    def _():
        norm_ref[...] = jnp.zeros_like(norm_ref)
    norm_ref[...] += total


def filter_tables(l):
    n = jnp.arange(2 * l)
    pos = jnp.where(n < l, n, 2 * l - n).astype(F32)[:, None]
    t = pos / max(l - 1, 1)
    w = 2.0 * math.pi * pos / l
    f = jnp.linspace(1e-4, FILT_BANDS - 1, FILT_BANDS, dtype=F32)[None, :]
    feat = jnp.concatenate([t, jnp.cos(f * w), -jnp.sin(f * w),
                            jnp.zeros((2 * l, LANES - FILT_EMB), F32)], axis=-1)
    half = min(FILT_TILE, 2 * l) // 2
    feat = feat.reshape(-1, 2, half, LANES).transpose(0, 2, 1, 3).reshape(l, 2 * LANES)
    return feat, t


def hyena_filter(l, tables, w1, b1, f1, w2, b2, f2, w3):
    feat, t = tables
    tr = min(FILT_TILE, 2 * l)
    hid = w2.shape[0]
    max_decay = math.log(DECAY_TARGET) / FAST_DECAY_PCT
    min_decay = math.log(DECAY_TARGET) / SLOW_DECAY_PCT
    deltas = jnp.linspace(min_decay, max_decay, HY_WIDTH, dtype=F32)[None, :]
    w1p = jnp.concatenate([w1, jnp.zeros((LANES - FILT_EMB, hid), F32)], 0)
    twice = lambda m: jax.scipy.linalg.block_diag(m, m)
    pair = lambda v: jnp.tile(v.reshape(1, hid), (1, 2))
    full = lambda a: pl.BlockSpec(a.shape, lambda i: (0,) * a.ndim)
    smalls = [twice(w1p), pair(b1), pair(f1), twice(w2), pair(b2), pair(f2), twice(w3), deltas]
    return pl.pallas_call(
        functools.partial(_filter_kernel, l=l, tr=tr),
        grid=(2 * l // tr,),
        in_specs=[pl.BlockSpec((tr // 2, 2 * LANES), lambda i: (i, 0)),
                  pl.BlockSpec((tr, 1), lambda i: (i, 0))] + [full(a) for a in smalls],
        out_specs=[pl.BlockSpec((tr, HY_WIDTH), lambda i: (i, 0)),
                   pl.BlockSpec((1, HY_WIDTH), lambda i: (0, 0))],
        out_shape=[jax.ShapeDtypeStruct((2 * l, HY_WIDTH), F32),
                   jax.ShapeDtypeStruct((1, HY_WIDTH), F32)],
        compiler_params=_cp("arbitrary"),
        name="hyena_filter",
    )(feat, t, *smalls)


def _dft_rows_kernel(f_ref, x_ref, o_ref):
    o_ref[0] = _dot(f_ref[...], x_ref[0].astype(BF16)).astype(o_ref.dtype)


def dft_rows(fmat, x, out_dtype):
    b, k, w = x.shape
    m = fmat.shape[0]
    tn = min(DFT_COLS, w)
    return pl.pallas_call(
        _dft_rows_kernel,
        grid=(b, w // tn),
        in_specs=[pl.BlockSpec((m, k), lambda bi, j: (0, 0)),
                  pl.BlockSpec((1, k, tn), lambda bi, j: (bi, 0, j))],
        out_specs=pl.BlockSpec((1, m, tn), lambda bi, j: (bi, 0, j)),
        out_shape=jax.ShapeDtypeStruct((b, m, w), out_dtype),
        compiler_params=_cp("parallel", "parallel"),
        name="dft_rows",
    )(fmat, x)


SPEC_KB = 8


def _inner_dft(g, are, aim):
    n2 = are.shape[0]
    p1 = _dot(g, are)
    p2 = _dot(g, aim)
    return p1[:n2] - p2[n2:], p2[:n2] + p1[n2:]


def _spectrum_kernel(g_ref, a_ref, h_ref):
    for j in range(g_ref.shape[0]):
        h_ref[0, j], h_ref[1, j] = _inner_dft(g_ref[j], a_ref[0, 0, j], a_ref[0, 1, j])


def filter_spectrum(gs, a):
    _, _, n1, n2, c = a.shape
    kb = min(SPEC_KB, n1)
    return pl.pallas_call(
        _spectrum_kernel,
        grid=(n1 // kb,),
        in_specs=[pl.BlockSpec((kb, 2 * n2, n2), lambda k: (k, 0, 0)),
                  pl.BlockSpec((1, 2, kb, n2, c), lambda k: (0, 0, k, 0, 0))],
        out_specs=pl.BlockSpec((2, kb, n2, c), lambda k: (0, k, 0, 0)),
        out_shape=jax.ShapeDtypeStruct((2, n1, n2, c), F32),
        compiler_params=_cp("parallel"),
        name="filter_spectrum",
    )(gs, a)


def _spectral_kernel(g_ref, gt_ref, h_ref, a_ref, b_ref):
    n2 = a_ref.shape[3]
    for j in range(g_ref.shape[0]):
        xre, xim = _inner_dft(g_ref[j], a_ref[0, 0, j], a_ref[0, 1, j])
        hre, him = h_ref[0, j], h_ref[1, j]
        yre = (xre * hre - xim * him).astype(BF16)
        yim = (xre * him + xim * hre).astype(BF16)
        gt = gt_ref[j]
        q1 = _dot(gt, yre)
        q2 = _dot(gt, yim)
        b_ref[0, 0, j] = (q1[:n2] + q2[n2:]).astype(b_ref.dtype)
        b_ref[0, 1, j] = (q2[:n2] - q1[n2:]).astype(b_ref.dtype)


def spectral_multiply(gs, gts, hf, a):
    b, _, n1, n2, c = a.shape
    kb = min(SPEC_KB, n1)
    return pl.pallas_call(
        _spectral_kernel,
        grid=(n1 // kb, b),
        in_specs=[pl.BlockSpec((kb, 2 * n2, n2), lambda k, bi: (k, 0, 0)),
                  pl.BlockSpec((kb, 2 * n2, n2), lambda k, bi: (k, 0, 0)),
                  pl.BlockSpec((2, kb, n2, c), lambda k, bi: (0, k, 0, 0)),
                  pl.BlockSpec((1, 2, kb, n2, c), lambda k, bi: (bi, 0, k, 0, 0))],
        out_specs=pl.BlockSpec((1, 2, kb, n2, c), lambda k, bi: (bi, 0, k, 0, 0)),
        out_shape=jax.ShapeDtypeStruct(a.shape, BF16),
        compiler_params=_cp("parallel", "arbitrary"),
        name="spectral_multiply",
    )(gs, gts, hf, a)


def _idft_rows_kernel(f_ref, bm_ref, x0_ref, z_ref, inv_ref, bias_ref, o_ref, *, inv_n):
    y = _dot(f_ref[...], bm_ref[0]) * inv_n
    z = z_ref[0]
    o_ref[0] = (x0_ref[0] * (y * inv_ref[...] + bias_ref[...] * z)).astype(o_ref.dtype)


def idft_rows_gate(fmat, bm, x0, z, inv_norm, bias, n):
    b, k, w = bm.shape
    m = fmat.shape[0]
    tn = min(DFT_COLS, w)
    col = lambda bi, j: (bi, 0, j)
    return pl.pallas_call(
        functools.partial(_idft_rows_kernel, inv_n=1.0 / n),
        grid=(b, w // tn),
        in_specs=[pl.BlockSpec((m, k), lambda bi, j: (0, 0)),
                  pl.BlockSpec((1, k, tn), col),
                  pl.BlockSpec((1, m, tn), col),
                  pl.BlockSpec((1, m, tn), col),
                  pl.BlockSpec((1, tn), lambda bi, j: (0, j)),
                  pl.BlockSpec((1, tn), lambda bi, j: (0, j))],
        out_specs=pl.BlockSpec((1, m, tn), col),
        out_shape=jax.ShapeDtypeStruct((b, m, w), BF16),
        compiler_params=_cp("parallel", "parallel"),
        name="idft_rows_gate",
    )(fmat, bm, x0, z, inv_norm, bias)


def dft_tables(l):
    n = 2 * l
    n2 = FFT_N2
    n1 = n // n2
    k1 = np.arange(n1)
    ang1 = 2.0 * np.pi * np.outer(k1, k1) / n1
    f1 = np.concatenate([np.cos(ang1), -np.sin(ang1)], 0)
    f3 = np.concatenate([np.cos(ang1), -np.sin(ang1)], 1)[: n1 // 2]
    m2 = np.arange(n2)
    tw = 2.0 * np.pi * np.outer(k1, m2) / n
    ang2 = 2.0 * np.pi * np.outer(m2, m2) / n2
    twr, twi = jnp.asarray(np.cos(tw), F32)[:, None, :], jnp.asarray(-np.sin(tw), F32)[:, None, :]
    f2r, f2i = jnp.asarray(np.cos(ang2), F32)[None], jnp.asarray(-np.sin(ang2), F32)[None]
    gre, gim = twr * f2r - twi * f2i, twr * f2i + twi * f2r
    gs = jnp.concatenate([gre, gim], 1).astype(BF16)
    gts = jnp.concatenate([gre.transpose(0, 2, 1), gim.transpose(0, 2, 1)], 1).astype(BF16)
    return dict(n=n, n1=n1, n2=n2, f1=jnp.asarray(f1, BF16), f3=jnp.asarray(f3, BF16), gs=gs, gts=gts)


def hyena_long_conv(z, x0, filt2, norm, bias, tabs):
    b, l, c = z.shape
    n, n1, n2 = tabs["n"], tabs["n1"], tabs["n2"]
    w = n2 * c
    fa = dft_rows(tabs["f1"], filt2.reshape(1, n1, w), BF16)
    hf = filter_spectrum(tabs["gs"], fa.reshape(1, 2, n1, n2, c))
    a = dft_rows(tabs["f1"][:, : n1 // 2], z.reshape(b, n1 // 2, w), BF16)
    bm = spectral_multiply(tabs["gs"], tabs["gts"], hf, a.reshape(b, 2, n1, n2, c))
    tile = lambda v: jnp.tile(v.reshape(1, c), (1, n2))
    y = idft_rows_gate(tabs["f3"], bm.reshape(b, 2 * n1, w), x0.reshape(b, n1 // 2, w),
                       z.reshape(b, n1 // 2, w), tile(1.0 / norm), tile(bias), n)
    return y.reshape(b, l, c)


def _small_conv_kernel(ff_ref, fi_ref, filt_ref, z_ref, x0_ref, inv_ref, bias_ref, o_ref, *, l):
    ff = ff_ref[...]
    hs = _dot(ff, filt_ref[...].astype(BF16))
    hre, him = hs[:2 * l], hs[2 * l:]
    z = z_ref[0]
    xs = _dot(ff[:, :l], z.astype(BF16))
    xre, xim = xs[:2 * l], xs[2 * l:]
    yre = (xre * hre - xim * him).astype(BF16)
    yim = (xre * him + xim * hre).astype(BF16)
    y = _dot(fi_ref[...], jnp.concatenate([yre, yim], 0)) * (0.5 / l)
    o_ref[0] = (x0_ref[0] * (y * inv_ref[...] + bias_ref[...] * z)).astype(o_ref.dtype)


def small_long_conv(z, x0, filt2, norm, bias):
    b, l, c = z.shape
    k = np.arange(2 * l)
    ang = 2.0 * np.pi * np.outer(k, k) / (2 * l)
    ff = jnp.asarray(np.concatenate([np.cos(ang), -np.sin(ang)], 0), BF16)
    fi = jnp.asarray(np.concatenate([np.cos(ang), -np.sin(ang)], 1)[:l], BF16)
    full = lambda a: pl.BlockSpec(a.shape, lambda bi: (0,) * a.ndim)
    row = pl.BlockSpec((1, l, c), lambda bi: (bi, 0, 0))
    inv = 1.0 / norm
    return pl.pallas_call(
        functools.partial(_small_conv_kernel, l=l),
        grid=(b,),
        in_specs=[full(ff), full(fi), full(filt2), row, row, full(inv), full(bias)],
        out_specs=row,
        out_shape=jax.ShapeDtypeStruct((b, l, c), BF16),
        compiler_params=_cp("parallel"),
        name="small_long_conv",
    )(ff, fi, filt2, z, x0, inv, bias)


def _pool_kernel(p_ref, pp_ref, pn_ref, w_ref, sc_ref, o_ref, buf_ref, *, tm, l):
    _fill_padded(buf_ref, p_ref, pp_ref, pn_ref, tm)
    t = pl.program_id(1) * tm + lax.broadcasted_iota(jnp.int32, (tm, LANES), 0)
    lane = lax.broadcasted_iota(jnp.int32, (tm, LANES), 1)
    first_group = lane < POOL_GROUP
    h0 = POOL_HALO

    def win(col, lo, hi):
        acc = buf_ref[h0 + lo:h0 + lo + tm, col]
        for j in range(lo + 1, hi):
            acc = acc + buf_ref[h0 + j:h0 + j + tm, col]
        return acc

    parts = []
    for half, (wa, wb) in enumerate(((2, 4), (8, 16))):
        col = slice(half * LANES, (half + 1) * LANES)
        sa = win(col, -wa // 2, wa // 2)
        sb = sa + win(col, -wb // 2, -wa // 2) + win(col, wa // 2, wb // 2)
        hw = jnp.where(first_group, wa // 2, wb // 2)
        cnt = jnp.minimum(t + hw, l) - jnp.maximum(t - hw, 0)
        mean = jnp.where(first_group, sa, sb) / cnt.astype(F32)
        parts.append(mean - buf_ref[h0:h0 + tm, col])
    d = jnp.concatenate(parts, axis=-1).astype(BF16)
    o_ref[0] = (_dot(d, w_ref[...]) * sc_ref[...]).astype(o_ref.dtype)


def pool_mix(p, pool_w, pool_scale):
    b, l, wd = p.shape
    tm = min(ROW_TILE, l)
    nblk = l // tm
    wblk = jax.scipy.linalg.block_diag(*[pool_w[g] for g in range(len(POOL_WINDOWS))]).astype(BF16)
    return pl.pallas_call(
        functools.partial(_pool_kernel, tm=tm, l=l),
        grid=(b, nblk),
        in_specs=_halo_specs(tm, wd, nblk) + [pl.BlockSpec((wd, wd), lambda bi, i: (0, 0)),
                                              pl.BlockSpec((1, wd), lambda bi, i: (0, 0))],
        out_specs=pl.BlockSpec((1, tm, wd), lambda bi, i: (bi, i, 0)),
        out_shape=jax.ShapeDtypeStruct((b, l, wd), BF16),
        scratch_shapes=[pltpu.VMEM((tm + 2 * POOL_HALO, wd), F32)],
        compiler_params=_cp("parallel", "parallel"),
        name="pool_mix",
    )(p, p, p, wblk, pool_scale.reshape(1, wd))


def _outproj_kernel(x_ref, hy_ref, pool_ref, att_ref, w_ref, gate_ref, o_ref):
    a, b2 = HY_WIDTH, HY_WIDTH + POOL_WIDTH
    y = (_dot(hy_ref[0], w_ref[0:a]) + _dot(pool_ref[0], w_ref[a:b2]) + _dot(att_ref[0], w_ref[b2:]))
    o_ref[0] = x_ref[0] + gate_ref[0] * y


def out_projection(x, y_hy, y_pool, y_att, w_out, gate):
    b, l, d = x.shape
    tm = min(ROW_TILE, l)
    row = lambda bi, i: (bi, i, 0)
    return pl.pallas_call(
        _outproj_kernel,
        grid=(b, l // tm),
        in_specs=[pl.BlockSpec((1, tm, d), row),
                  pl.BlockSpec((1, tm, y_hy.shape[2]), row),
                  pl.BlockSpec((1, tm, y_pool.shape[2]), row),
                  pl.BlockSpec((1, tm, y_att.shape[2]), row),
                  pl.BlockSpec(w_out.shape, lambda bi, i: (0, 0)),
                  pl.BlockSpec((1, 1, d), lambda bi, i: (bi, 0, 0))],
        out_specs=pl.BlockSpec((1, tm, d), row),
        out_shape=jax.ShapeDtypeStruct(x.shape, F32),
        compiler_params=_cp("parallel", "parallel"),
        name="out_projection",
    )(x, y_hy, y_pool, y_att, w_out, gate)


FF_CHUNK = 256
MOE_TILE = 1024
MOE_ROWS = (128, 192, 256, 320, 384)


def _swiglu_acc(h, wg_ref, wu_ref, wd_ref, lead):
    ff = wg_ref.shape[-1]
    acc = None
    for f0 in range(0, ff, FF_CHUNK):
        cs = slice(f0, min(f0 + FF_CHUNK, ff))
        g = _dot(h, wg_ref[lead + (slice(None), cs)])
        u = _dot(h, wu_ref[lead + (slice(None), cs)])
        a = (g * jax.nn.sigmoid(g) * u).astype(BF16)
        part = _dot(a, wd_ref[lead + (cs, slice(None))])
        acc = part if acc is None else acc + part
    return acc


def _ffn_kernel(x_ref, g_ref, sh_ref, sc_ref, gate_ref, wg_ref, wu_ref, wd_ref, o_ref):
    x = x_ref[0]
    h = _normmod(x, g_ref[...], sh_ref[0], sc_ref[0]).astype(BF16)
    o_ref[0] = x + gate_ref[0] * _swiglu_acc(h, wg_ref, wu_ref, wd_ref, ())


def ffn_dense(x, g, shift, scale, gate, wg, wu, wd):
    b, l, d = x.shape
    tm = min(ROW_TILE, l)
    row = lambda bi, i: (bi, i, 0)
    vec = lambda bi, i: (bi, 0, 0)
    wspec = lambda w: pl.BlockSpec(w.shape, lambda bi, i: (0, 0), pipeline_mode=pl.Buffered(1))
    return pl.pallas_call(
        _ffn_kernel,
        grid=(b, l // tm),
        in_specs=[pl.BlockSpec((1, tm, d), row),
                  pl.BlockSpec((1, d), lambda bi, i: (0, 0)),
                  pl.BlockSpec((1, 1, d), vec), pl.BlockSpec((1, 1, d), vec), pl.BlockSpec((1, 1, d), vec),
                  wspec(wg), wspec(wu), wspec(wd)],
        out_specs=pl.BlockSpec((1, tm, d), row),
        out_shape=jax.ShapeDtypeStruct(x.shape, F32),
        compiler_params=_cp("parallel", "parallel"),
        name="ffn_dense",
    )(x, g, shift, scale, gate, wg, wu, wd)


def _router_kernel(x_ref, g_ref, sh_ref, sc_ref, wr_ref, tri_ref, comb_ref, rank_ref):
    h = _normmod(x_ref[0], g_ref[...], sh_ref[0], sc_ref[0])
    logits = _dot(h, wr_ref[...], precision=HIGHEST)
    lane = lax.broadcasted_iota(jnp.int32, logits.shape, 1)
    neg = jnp.float32(-jnp.inf)
    lg = jnp.where(lane < N_EXPERTS, logits, neg)
    m1 = jnp.max(lg, axis=-1, keepdims=True)
    i1 = jnp.min(jnp.where(lg == m1, lane, LANES), axis=-1, keepdims=True)
    lg2 = jnp.where(lane == i1, neg, lg)
    m2 = jnp.max(lg2, axis=-1, keepdims=True)
    i2 = jnp.min(jnp.where(lg2 == m2, lane, LANES), axis=-1, keepdims=True)
    e = jnp.exp(m2 - m1)
    g1 = 1.0 / (1.0 + e)
    comb_ref[0] = jnp.where(lane == i1, g1, jnp.where(lane == i2, e * g1, 0.0))
    routed = (lane == i1) | (lane == i2)
    before = _dot(tri_ref[...], jnp.where(routed, 1.0, 0.0).astype(BF16))
    rank_ref[0] = jnp.where(routed, before, -1.0)


def moe_route(x, g, shift, scale, w_router):
    b, l, d = x.shape
    tm = min(MOE_TILE, l)
    wr = jnp.concatenate([w_router, jnp.zeros((d, LANES - N_EXPERTS), F32)], -1)
    tri = jnp.asarray(np.tril(np.ones((tm, tm), np.float32), -1), BF16)
    row = lambda bi, i: (bi, i, 0)
    vec = lambda bi, i: (bi, 0, 0)
    return pl.pallas_call(
        _router_kernel,
        grid=(b, l // tm),
        in_specs=[pl.BlockSpec((1, tm, d), row),
                  pl.BlockSpec((1, d), lambda bi, i: (0, 0)),
                  pl.BlockSpec((1, 1, d), vec), pl.BlockSpec((1, 1, d), vec),
                  pl.BlockSpec((d, LANES), lambda bi, i: (0, 0)),
                  pl.BlockSpec((tm, tm), lambda bi, i: (0, 0))],
        out_specs=[pl.BlockSpec((1, tm, LANES), row)] * 2,
        out_shape=[jax.ShapeDtypeStruct((b, l, LANES), F32)] * 2,
        compiler_params=_cp("parallel", "parallel"),
        name="moe_router",
    )(x, g, shift, scale, wr, tri)


def _moe_kernel(x_ref, g_ref, sh_ref, sc_ref, gate_ref, comb_ref, wg_ref, wu_ref, wd_ref, o_ref, acc_ref):
    e = pl.program_id(2)
    x = x_ref[0]
    h = _normmod(x, g_ref[...], sh_ref[0], sc_ref[0]).astype(BF16)
    comb = comb_ref[0]
    lane = lax.broadcasted_iota(jnp.int32, comb.shape, 1)
    ce = jnp.sum(jnp.where(lane == e, comb, 0.0), axis=-1, keepdims=True)
    y = ce * _swiglu_acc(h, wg_ref, wu_ref, wd_ref, (0, 0))

    @pl.when(e == 0)
    def _():
        acc_ref[...] = y

    @pl.when(e > 0)
    def _():
        acc_ref[...] += y

    @pl.when(e == N_EXPERTS - 1)
    def _():
        o_ref[0] = x + gate_ref[0] * acc_ref[...]


def moe_dense(x, g, shift, scale, gate, comb, wg, wu, wd, li):
    b, l, d = x.shape
    tm = min(ROW_TILE, l)
    ff = wg.shape[-1]
    row = lambda bi, i, e: (bi, i, 0)
    vec = lambda bi, i, e: (bi, 0, 0)
    return pl.pallas_call(
        _moe_kernel,
        grid=(b, l // tm, N_EXPERTS),
        in_specs=[pl.BlockSpec((1, tm, d), row),
                  pl.BlockSpec((1, d), lambda bi, i, e: (0, 0)),
                  pl.BlockSpec((1, 1, d), vec), pl.BlockSpec((1, 1, d), vec), pl.BlockSpec((1, 1, d), vec),
                  pl.BlockSpec((1, tm, LANES), row),
                  pl.BlockSpec((1, 1, d, ff), lambda bi, i, e: (li, e, 0, 0)),
                  pl.BlockSpec((1, 1, d, ff), lambda bi, i, e: (li, e, 0, 0)),
                  pl.BlockSpec((1, 1, ff, d), lambda bi, i, e: (li, e, 0, 0))],
        out_specs=pl.BlockSpec((1, tm, d), row),
        out_shape=jax.ShapeDtypeStruct(x.shape, F32),
        scratch_shapes=[pltpu.VMEM((tm, d), F32)],
        compiler_params=_cp("parallel", "parallel", "arbitrary"),
        name="moe_dense",
    )(x, g, shift, scale, gate, comb, wg, wu, wd)


def _moe_sparse_kernel(cnt_ref, x_ref, g_ref, sh_ref, sc_ref, gate_ref, comb_ref, rank_ref, rankt_ref,
                       wg_ref, wu_ref, wd_ref, *rest, t, sizes):
    fg_ref = rest[0] if len(rest) == 3 else None
    o_ref, h_scr = rest[-2:]
    i, e = pl.program_id(0), pl.program_id(1)
    cnt = cnt_ref[i * N_EXPERTS + e]

    @pl.when(e == 0)
    def _():
        x = x_ref[0]
        h_scr[...] = _normmod(x, g_ref[...], sh_ref[0], sc_ref[0]).astype(BF16)
        o_ref[0] = x

    rrow = rankt_ref[0, pl.ds(e, 1), :]
    lane = lax.broadcasted_iota(jnp.int32, (t, LANES), 1)
    rcol = jnp.sum(jnp.where(lane == e, rank_ref[0], 0.0), axis=-1, keepdims=True)
    ccol = jnp.sum(jnp.where(lane == e, comb_ref[0], 0.0), axis=-1, keepdims=True)

    def block(base, r):
        rid = (base + lax.broadcasted_iota(jnp.int32, (r, 1), 0)).astype(F32)
        onehot = jnp.where(rrow == rid, 1.0, 0.0).astype(BF16)
        xg = _dot(onehot, h_scr[...]).astype(BF16)
        y = _swiglu_acc(xg, wg_ref, wu_ref, wd_ref, (0, 0)).astype(BF16)
        cid = (base + lax.broadcasted_iota(jnp.int32, (1, r), 1)).astype(F32)
        onehot_t = jnp.where(rcol == cid, 1.0, 0.0).astype(BF16)
        o_ref[0] += (gate_ref[0] * ccol) * _dot(onehot_t, y)

    big = sizes[-1]
    nfull = cnt // big

    def full(k, carry):
        block(k * big, big)
        return carry
    lax.fori_loop(0, nfull, full, 0)
    base = nfull * big
    rem = cnt - base
    for lo, r in zip((0,) + sizes[:-1], sizes):
        @pl.when((rem > lo) & (rem <= r))
        def _(r=r):
            block(base, r)

    if fg_ref is not None:
        @pl.when(e == N_EXPERTS - 1)
        def _():
            o = o_ref[0]
            o_ref[0] = o * lax.rsqrt(jnp.mean(o * o, axis=-1, keepdims=True) + EPS) * fg_ref[...]


def moe_sparse(x, g, shift, scale, gate, comb, rank, wg, wu, wd, li, final_g=None):
    b, l, d = x.shape
    t = MOE_TILE
    tpb = l // t
    n_tiles = b * tpb
    ff = wg.shape[-1]
    rk = rank[..., :N_EXPERTS].reshape(n_tiles, t, N_EXPERTS)
    rank_t = rk.transpose(0, 2, 1)
    cnt = jnp.sum(rk >= 0, axis=1).astype(jnp.int32).reshape(n_tiles * N_EXPERTS)
    row = lambda i, e, c: (i // tpb, i % tpb, 0)
    vec = lambda i, e, c: (i // tpb, 0, 0)
    grid_spec = pltpu.PrefetchScalarGridSpec(
        num_scalar_prefetch=1,
        grid=(n_tiles, N_EXPERTS),
        in_specs=[pl.BlockSpec((1, t, d), row, pipeline_mode=pl.Buffered(1)),
                  pl.BlockSpec((1, d), lambda i, e, c: (0, 0)),
                  pl.BlockSpec((1, 1, d), vec), pl.BlockSpec((1, 1, d), vec), pl.BlockSpec((1, 1, d), vec),
                  pl.BlockSpec((1, t, LANES), row, pipeline_mode=pl.Buffered(1)),
                  pl.BlockSpec((1, t, LANES), row, pipeline_mode=pl.Buffered(1)),
                  pl.BlockSpec((1, N_EXPERTS, t), lambda i, e, c: (i, 0, 0)),
                  pl.BlockSpec((1, 1, d, ff), lambda i, e, c: (li, e, 0, 0)),
                  pl.BlockSpec((1, 1, d, ff), lambda i, e, c: (li, e, 0, 0)),
                  pl.BlockSpec((1, 1, ff, d), lambda i, e, c: (li, e, 0, 0))]
        + ([] if final_g is None else [pl.BlockSpec((1, d), lambda i, e, c: (0, 0))]),
        out_specs=pl.BlockSpec((1, t, d), row),
        scratch_shapes=[pltpu.VMEM((t, d), BF16)],
    )
    return pl.pallas_call(
        functools.partial(_moe_sparse_kernel, t=t, sizes=MOE_ROWS),
        grid_spec=grid_spec,
        out_shape=jax.ShapeDtypeStruct(x.shape, F32),
        compiler_params=_cp("parallel", "arbitrary"),
        name="moe_sparse",
    )(cnt, x, g, shift, scale, gate, comb, rank, rank_t, wg, wu, wd, *(() if final_g is None else (final_g,)))


def _final_kernel(x_ref, g_ref, o_ref):
    x = x_ref[0]
    o_ref[0] = x * lax.rsqrt(jnp.mean(x * x, axis=-1, keepdims=True) + EPS) * g_ref[...]


def final_norm(x, g):
    b, l, d = x.shape
    tm = min(2 * ROW_TILE, l)
    return pl.pallas_call(
        _final_kernel,
        grid=(b, l // tm),
        in_specs=[pl.BlockSpec((1, tm, d), lambda bi, i: (bi, i, 0)),
                  pl.BlockSpec((1, d), lambda bi, i: (0, 0))],
        out_specs=pl.BlockSpec((1, tm, d), lambda bi, i: (bi, i, 0)),
        out_shape=jax.ShapeDtypeStruct(x.shape, F32),
        compiler_params=_cp("parallel", "parallel"),
        name="final_norm",
    )(x, g)


def kernel(x, c, ctx, c_ctx, norm1_g, norm2_g, w_ada, b_ada, w_in, hy_short_w, hy_short_b, filt_w1, filt_b1, filt_freq1, filt_w2, filt_b2, filt_freq2, filt_w3, hy_bias, pool_w, pool_scale, q_norm_g, w_uq, kv_norm_g, w_ukv, w_out, ffn_wg, ffn_wu, ffn_wd, moe_router, moe_wg, moe_wu, moe_wd, final_g):
    b, l, d = x.shape
    lc = ctx.shape[1]
    depth = w_ada.shape[0]

    cc = jnp.zeros((8, d), F32).at[:b].set(c).at[b].set(c_ctx)
    mods = ada_vectors(cc, w_ada, b_ada)

    rope_l = rope_tables(l, True)
    rope_c = rope_tables(lc, False)
    ftab_l, ftab_c = filter_tables(l), filter_tables(lc)
    dtabs = dft_tables(l)
    moe_w = (moe_wg.astype(BF16), moe_wu.astype(BF16), moe_wd.astype(BF16))

    xl, xc = x, ctx
    for layer in range(depth):
        last = layer == depth - 1
        mod = mods[layer].reshape(8, 6, 1, d)
        mod_l = [mod[:b, j] for j in range(6)]
        mod_c = [jnp.broadcast_to(mod[b:b + 1, j], (b, 1, d)) for j in range(6)]
        g1 = norm1_g[layer].reshape(1, d)
        g2 = norm2_g[layer].reshape(1, d)
        w_in_l = w_in[layer].astype(BF16)
        w_out_l = w_out[layer].astype(BF16)
        mw = mla_weights(q_norm_g[layer], w_uq[layer], kv_norm_g[layer], w_ukv[layer])
        fw = (filt_w1[layer], filt_b1[layer], filt_freq1[layer], filt_w2[layer], filt_b2[layer],
              filt_freq2[layer], filt_w3[layer])
        bias = hy_bias[layer].reshape(1, HY_WIDTH)

        def channel_mix(xs, md, final=False):
            i = layer // 2
            fg = final_g.reshape(1, d)
            if layer % 2 == 1 and xs.shape[1] % MOE_TILE == 0:
                comb, rank = moe_route(xs, g2, md[3], md[4], moe_router[i])
                return moe_sparse(xs, g2, md[3], md[4], md[5], comb, rank, *moe_w, i, fg if final else None)
            if layer % 2 == 0:
                out = ffn_dense(xs, g2, md[3], md[4], md[5], ffn_wg[i].astype(BF16),
                                ffn_wu[i].astype(BF16), ffn_wd[i].astype(BF16))
            else:
                comb, _ = moe_route(xs, g2, md[3], md[4], moe_router[i])
                out = moe_dense(xs, g2, md[3], md[4], md[5], comb, *moe_w, i)
            return final_norm(out, fg) if final else out

        hy_l, pool_l, q_l, kt_l, v_l = in_projection(xl, g1, mod_l[0], mod_l[1], w_in_l, *rope_l, mw)
        hy_c, pool_c, q_c, kt_c, v_c = in_projection(xc, g1, mod_c[0], mod_c[1], w_in_l, *rope_c, mw)

        att_l = attention(q_l, kt_c, v_c, kt_l, v_l)
        z_l, x0_l = hyena_pre(hy_l, hy_short_w[layer], hy_short_b[layer])
        filt_l, nrm_l = hyena_filter(l, ftab_l, *fw)
        yhy_l = hyena_long_conv(z_l, x0_l, filt_l, nrm_l, bias, dtabs)
        ypool_l = pool_mix(pool_l, pool_w[layer], pool_scale[layer])
        xl = out_projection(xl, yhy_l, ypool_l, att_l, w_out_l, mod_l[2])

        if not last:
            att_c = attention(q_c, kt_c, v_c)
            z_c, x0_c = hyena_pre(hy_c, hy_short_w[layer], hy_short_b[layer])
            filt_c, nrm_c = hyena_filter(lc, ftab_c, *fw)
            yhy_c = small_long_conv(z_c, x0_c, filt_c, nrm_c, bias)
            ypool_c = pool_mix(pool_c, pool_w[layer], pool_scale[layer])
            xc = out_projection(xc, yhy_c, ypool_c, att_c, w_out_l, mod_c[2])
            xc = channel_mix(xc, mod_c)

        xl = channel_mix(xl, mod_l, final=last)
    return xl
```

```python
import functools
import math

import jax
import jax.numpy as jnp
import numpy as np
from jax import lax
from jax.experimental import pallas as pl
from jax.experimental.pallas import tpu as pltpu

F32 = jnp.float32
BF16 = jnp.bfloat16
HIGHEST = lax.Precision.HIGHEST

EPS = 1e-6
GRID_W = 64
HY_WIDTH = 256
POOL_WINDOWS = (2, 4, 8, 16)
POOL_WIDTH = 256
POOL_GROUP = 64
POOL_HALO = 8
MLA_HEADS = 8
QK_NOPE = 64
QK_ROPE = 32
V_DIM = 64
Q_LORA = 384
KV_LORA = 256
HEAD_PAD = 128
SM_SCALE = (QK_NOPE + QK_ROPE) ** -0.5
ROPE_THETA = 10000.0
FILT_EMB = 33
FILT_BANDS = 16
DECAY_TARGET = 1e-2
FAST_DECAY_PCT = 0.3
SLOW_DECAY_PCT = 1.5
N_EXPERTS = 8
LANES = 128
FFT_N2 = 128
VMEM_LIMIT = 56 * 2 ** 20
ROW_TILE = 512
DFT_COLS = 2048
ADA_COLS = 1536


def _cp(*sem):
    return pltpu.CompilerParams(dimension_semantics=sem, vmem_limit_bytes=VMEM_LIMIT)


def _dot(a, b, **kw):
    return jnp.dot(a, b, preferred_element_type=F32, **kw)


def _normmod(x, g, shift, scale):
    ms = jnp.mean(x * x, axis=-1, keepdims=True)
    return x * lax.rsqrt(ms + EPS) * (g * (1.0 + scale)) + shift


def _ada_kernel(c_ref, w_ref, b_ref, o_ref):
    c = c_ref[...]
    s = c * jax.nn.sigmoid(c)
    o_ref[0] = _dot(s, w_ref[0], precision=HIGHEST) + b_ref[0]


def ada_vectors(cc, w_ada, b_ada):
    depth, d, n = w_ada.shape
    tn = ADA_COLS
    return pl.pallas_call(
        _ada_kernel,
        grid=(depth, n // tn),
        in_specs=[pl.BlockSpec((8, d), lambda l, j: (0, 0)),
                  pl.BlockSpec((1, d, tn), lambda l, j: (l, 0, j)),
                  pl.BlockSpec((1, 1, tn), lambda l, j: (l, 0, j))],
        out_specs=pl.BlockSpec((1, 8, tn), lambda l, j: (l, 0, j)),
        out_shape=jax.ShapeDtypeStruct((depth, 8, n), F32),
        compiler_params=_cp("parallel", "parallel"),
        name="ada_vectors",
    )(cc, w_ada, b_ada.reshape(depth, 1, n))


IN_HY, IN_POOL = 768, 256
IN_OFFS = (0, 768, 1024, 1408, 1664, 1696)


def _inproj_kernel(x_ref, g_ref, sh_ref, sc_ref, w_ref, c_ref, s_ref, wqa_ref, wqb_ref, wk_ref, wv_ref,
                   pa_ref, pb_ref, hy_ref, pool_ref, q_ref, kt_ref, v_ref):
    h = _normmod(x_ref[0], g_ref[...], sh_ref[0], sc_ref[0]).astype(BF16)
    y = _dot(h, w_ref[...])
    o = IN_OFFS
    hy_ref[0] = y[:, o[0]:o[1]]
    pool_ref[0] = y[:, o[1]:o[2]]
    cq, ckv = y[:, o[2]:o[3]], y[:, o[3]:o[4]]
    kpe = y[:, o[4]:o[5]].astype(BF16)
    cqn = (cq * lax.rsqrt(jnp.mean(cq * cq, axis=-1, keepdims=True) + EPS)).astype(BF16)
    ckvn = (ckv * lax.rsqrt(jnp.mean(ckv * ckv, axis=-1, keepdims=True) + EPS)).astype(BF16)
    cs, sn = c_ref[...], s_ref[...]
    qa = _dot(cqn, wqa_ref[...])
    qb = _dot(cqn, wqb_ref[...])
    kn = _dot(ckvn, wk_ref[...])
    vv = _dot(ckvn, wv_ref[...])
    kpa = _dot(kpe, pa_ref[...])
    kpb = _dot(kpe, pb_ref[...])
    kpe_rot = kpa * cs + kpb * sn
    lane = lax.broadcasted_iota(jnp.int32, (1, HEAD_PAD), 1)
    for hd in range(MLA_HEADS):
        sl = slice(hd * HEAD_PAD, (hd + 1) * HEAD_PAD)
        q_ref[0, hd] = (qa[:, sl] * cs + qb[:, sl] * sn).astype(BF16)
        kt_ref[0, hd] = (kn[:, sl] + kpe_rot).T.astype(BF16)
        one_lane = V_DIM if hd % 2 == 0 else 0
        v_ref[0, hd] = jnp.where(lane == one_lane, 1.0, vv[:, sl]).astype(BF16)


def in_projection(x, g, shift, scale, w_in, ctab, stab, wts):
    b, l, d = x.shape
    tm = min(ROW_TILE, l)
    hw = MLA_HEADS
    row = lambda bi, i: (bi, i, 0)
    vec = lambda bi, i: (bi, 0, 0)
    full = lambda a: pl.BlockSpec(a.shape, lambda bi, i: (0,) * a.ndim)
    return pl.pallas_call(
        _inproj_kernel,
        grid=(b, l // tm),
        in_specs=[pl.BlockSpec((1, tm, d), row),
                  pl.BlockSpec((1, d), lambda bi, i: (0, 0)),
                  pl.BlockSpec((1, 1, d), vec),
                  pl.BlockSpec((1, 1, d), vec),
                  full(w_in),
                  pl.BlockSpec((tm, HEAD_PAD), lambda bi, i: (i, 0)),
                  pl.BlockSpec((tm, HEAD_PAD), lambda bi, i: (i, 0))] + [full(w) for w in wts],
        out_specs=[pl.BlockSpec((1, tm, IN_HY), row),
                   pl.BlockSpec((1, tm, IN_POOL), row),
                   pl.BlockSpec((1, hw, tm, HEAD_PAD), lambda bi, i: (bi, 0, i, 0)),
                   pl.BlockSpec((1, hw, HEAD_PAD, tm), lambda bi, i: (bi, 0, 0, i)),
                   pl.BlockSpec((1, hw, tm, HEAD_PAD), lambda bi, i: (bi, 0, i, 0))],
        out_shape=[jax.ShapeDtypeStruct((b, l, IN_HY), F32),
                   jax.ShapeDtypeStruct((b, l, IN_POOL), F32),
                   jax.ShapeDtypeStruct((b, hw, l, HEAD_PAD), BF16),
                   jax.ShapeDtypeStruct((b, hw, HEAD_PAD, l), BF16),
                   jax.ShapeDtypeStruct((b, hw, l, HEAD_PAD), BF16)],
        compiler_params=_cp("parallel", "parallel"),
        name="in_projection",
    )(x, g, shift, scale, w_in, ctab, stab, *wts)


def mla_weights(q_norm_g, w_uq, kv_norm_g, w_ukv):
    hw, dq = MLA_HEADS, QK_NOPE + QK_ROPE
    half = QK_ROPE // 2
    wq = (w_uq * q_norm_g[:, None] * (SM_SCALE * math.log2(math.e))).reshape(Q_LORA, hw, dq)
    pad = jnp.zeros((Q_LORA, hw, HEAD_PAD - dq), F32)
    wqa = jnp.concatenate([wq, pad], -1)
    swap = jnp.concatenate([wq[..., QK_NOPE + half:], wq[..., QK_NOPE:QK_NOPE + half]], -1)
    wqb = jnp.concatenate([jnp.zeros((Q_LORA, hw, QK_NOPE), F32), swap, pad], -1)
    wkv = (w_ukv * kv_norm_g[:, None]).reshape(KV_LORA, hw, QK_NOPE + V_DIM)
    z64 = jnp.zeros((KV_LORA, hw, 64), F32)
    wk = jnp.concatenate([wkv[..., :QK_NOPE], z64], -1)
    v = wkv[..., QK_NOPE:]
    wv = jnp.where((jnp.arange(hw) % 2 == 0)[None, :, None],
                   jnp.concatenate([v, z64], -1), jnp.concatenate([z64, v], -1))
    eye = np.eye(QK_ROPE, dtype=np.float32)
    pa = np.zeros((QK_ROPE, HEAD_PAD), np.float32)
    pa[:, QK_NOPE:QK_NOPE + QK_ROPE] = eye
    pb = np.zeros((QK_ROPE, HEAD_PAD), np.float32)
    pb[:, QK_NOPE:QK_NOPE + QK_ROPE] = np.concatenate([eye[:, half:], eye[:, :half]], 1)
    flat = lambda w: w.reshape(w.shape[0], hw * HEAD_PAD).astype(BF16)
    return (flat(wqa), flat(wqb), flat(wk), flat(wv), jnp.asarray(pa, BF16), jnp.asarray(pb, BF16))


def rope_tables(l, with_rope):
    ctab = np.zeros((l, HEAD_PAD), np.float32)
    stab = np.zeros((l, HEAD_PAD), np.float32)
    ctab[:, :QK_NOPE + QK_ROPE] = 1.0
    if not with_rope:
        return jnp.asarray(ctab), jnp.asarray(stab)
    n_freq = QK_ROPE // 4
    inv_freq = 1.0 / (ROPE_THETA ** (jnp.arange(n_freq, dtype=F32) / n_freq))
    rows = l // GRID_W
    row = jnp.repeat(jnp.arange(rows, dtype=F32), GRID_W)
    col = jnp.tile(jnp.arange(GRID_W, dtype=F32), rows)
    ang = jnp.concatenate([row[:, None] * inv_freq, col[:, None] * inv_freq], axis=-1)
    cos, sin = jnp.cos(ang), jnp.sin(ang)
    one = jnp.ones((l, QK_NOPE), F32)
    zero = jnp.zeros((l, QK_NOPE), F32)
    tail = jnp.zeros((l, HEAD_PAD - QK_NOPE - QK_ROPE), F32)
    return (jnp.concatenate([one, cos, cos, tail], -1), jnp.concatenate([zero, -sin, sin, tail], -1))


ATT_TQ = 1024
ATT_RB = 1024
ATT_CK = 2048


def _row_max(s):
    mp = s[:, 0:LANES]
    for j in range(1, s.shape[1] // LANES):
        mp = jnp.maximum(mp, s[:, j * LANES:(j + 1) * LANES])
    return jnp.max(mp, axis=-1, keepdims=True)


def _attn_kernel(q_ref, kct_ref, vc_ref, *rest, n_chunks, ck, tq):
    if n_chunks:
        kt_ref, v_ref, o_ref = rest
    else:
        (o_ref,) = rest
    rb = min(ATT_RB, tq)
    chains = [(hh, r) for hh in range(2) for r in range(tq // rb)]
    lane = lax.broadcasted_iota(jnp.int32, (1, HEAD_PAD), 1)

    def q_rows(hh, r):
        return q_ref[0, hh, r * rb:(r + 1) * rb, :]

    state = []
    for hh, r in chains:
        s = _dot(q_rows(hh, r), kct_ref[0, hh])
        m = _row_max(s)
        p = jnp.exp2(s - m).astype(BF16)
        state += [m, _dot(p, vc_ref[0, hh])]

    if n_chunks:
        def body(c, state):
            off = pl.multiple_of(c * ck, ck)
            out = []
            for i, (hh, r) in enumerate(chains):
                m, acc = state[2 * i], state[2 * i + 1]
                s = _dot(q_rows(hh, r), kt_ref[0, hh, :, pl.ds(off, ck)])
                m_new = jnp.maximum(m, _row_max(s))
                alpha = jnp.exp2(m - m_new)
                p = jnp.exp2(s - m_new).astype(BF16)
                out += [m_new, acc * alpha + _dot(p, v_ref[0, hh, pl.ds(off, ck), :])]
            return tuple(out)
        state = lax.fori_loop(0, n_chunks, body, tuple(state), unroll=2 if n_chunks % 2 == 0 else 1)

    for r in range(tq // rb):
        outs = []
        for hh in range(2):
            acc = state[2 * chains.index((hh, r)) + 1]
            one_lane = V_DIM if hh == 0 else 0
            denom = jnp.sum(jnp.where(lane == one_lane, acc, 0.0), axis=-1, keepdims=True)
            outs.append(acc / denom)
        o_ref[0, r * rb:(r + 1) * rb, :] = jnp.where(lane < V_DIM, outs[0], outs[1]).astype(o_ref.dtype)


def attention(q, kct, vc, kt=None, v=None):
    b, hw, lq, _ = q.shape
    lc = vc.shape[2]
    tq = min(ATT_TQ, lq)
    ck = ATT_CK if kt is None else min(ATT_CK, kt.shape[3])
    n_chunks = 0 if kt is None else kt.shape[3] // ck
    in_specs = [pl.BlockSpec((1, 2, tq, HEAD_PAD), lambda bi, hp, i: (bi, hp, i, 0)),
                pl.BlockSpec((1, 2, HEAD_PAD, lc), lambda bi, hp, i: (bi, hp, 0, 0)),
                pl.BlockSpec((1, 2, lc, HEAD_PAD), lambda bi, hp, i: (bi, hp, 0, 0))]
    args = [q, kct, vc]
    if n_chunks:
        l = kt.shape[3]
        in_specs += [pl.BlockSpec((1, 2, HEAD_PAD, l), lambda bi, hp, i: (bi, hp, 0, 0)),
                     pl.BlockSpec((1, 2, l, HEAD_PAD), lambda bi, hp, i: (bi, hp, 0, 0))]
        args += [kt, v]
    return pl.pallas_call(
        functools.partial(_attn_kernel, n_chunks=n_chunks, ck=ck, tq=tq),
        grid=(b, hw // 2, lq // tq),
        in_specs=in_specs,
        out_specs=pl.BlockSpec((1, tq, HEAD_PAD), lambda bi, hp, i: (bi, i, hp)),
        out_shape=jax.ShapeDtypeStruct((b, lq, hw * V_DIM), BF16),
        compiler_params=_cp("parallel", "parallel", "arbitrary"),
        name="attention",
    )(*args)


def _halo_specs(tm, width, nblk):
    r = tm // POOL_HALO
    last = nblk * r - 1
    return [pl.BlockSpec((1, tm, width), lambda bi, i: (bi, i, 0)),
            pl.BlockSpec((1, POOL_HALO, width), lambda bi, i: (bi, jnp.maximum(i * r - 1, 0), 0)),
            pl.BlockSpec((1, POOL_HALO, width), lambda bi, i: (bi, jnp.minimum((i + 1) * r, last), 0))]


def _fill_padded(buf_ref, cur_ref, prev_ref, next_ref, tm):
    i = pl.program_id(1)
    first = i == 0
    last = i == pl.num_programs(1) - 1
    buf_ref[0:POOL_HALO] = jnp.where(first, 0.0, prev_ref[0])
    buf_ref[POOL_HALO:POOL_HALO + tm] = cur_ref[0]
    buf_ref[POOL_HALO + tm:] = jnp.where(last, 0.0, next_ref[0])


def _hyena_pre_kernel(u_ref, up_ref, un_ref, w_ref, b_ref, z_ref, x0_ref, buf_ref, *, tm):
    _fill_padded(buf_ref, u_ref, up_ref, un_ref, tm)
    w = w_ref[...]
    uc = (buf_ref[POOL_HALO - 1:POOL_HALO - 1 + tm] * w[0:1] + buf_ref[POOL_HALO:POOL_HALO + tm] * w[1:2]
          + buf_ref[POOL_HALO + 1:POOL_HALO + 1 + tm] * w[2:3] + b_ref[...])
    x0_ref[0] = uc[:, :HY_WIDTH]
    z_ref[0] = uc[:, HY_WIDTH:2 * HY_WIDTH] * uc[:, 2 * HY_WIDTH:]


def hyena_pre(u, sw, sb):
    b, l, wd = u.shape
    tm = min(ROW_TILE, l)
    nblk = l // tm
    return pl.pallas_call(
        functools.partial(_hyena_pre_kernel, tm=tm),
        grid=(b, nblk),
        in_specs=_halo_specs(tm, wd, nblk) + [pl.BlockSpec((3, wd), lambda bi, i: (0, 0)),
                                              pl.BlockSpec((1, wd), lambda bi, i: (0, 0))],
        out_specs=[pl.BlockSpec((1, tm, HY_WIDTH), lambda bi, i: (bi, i, 0))] * 2,
        out_shape=[jax.ShapeDtypeStruct((b, l, HY_WIDTH), F32)] * 2,
        scratch_shapes=[pltpu.VMEM((tm + 2 * POOL_HALO, wd), F32)],
        compiler_params=_cp("parallel", "parallel"),
        name="hyena_pre",
    )(u, u, u, sw, sb.reshape(1, wd))


FILT_TILE = 512


def _filter_kernel(feat_ref, tdec_ref, w1_ref, b1_ref, f1_ref, w2_ref, b2_ref, f2_ref, w3_ref, dl_ref,
                   filt_ref, norm_ref, *, l, tr):
    i = pl.program_id(0)
    half = tr // 2
    h = jnp.sin(f1_ref[...] * (_dot(feat_ref[...], w1_ref[...], precision=HIGHEST) + b1_ref[...]))
    h = jnp.sin(f2_ref[...] * (_dot(h, w2_ref[...], precision=HIGHEST) + b2_ref[...]))
    h = _dot(h, w3_ref[...], precision=HIGHEST)
    dl = jnp.abs(dl_ref[...])
    total = jnp.zeros((1, HY_WIDTH), F32)
    for p in range(2):
        rows = slice(p * half, (p + 1) * half)
        hp = h[:, p * 2 * HY_WIDTH:(p + 1) * 2 * HY_WIDTH]
        decay = jnp.exp(-tdec_ref[rows] * dl)
        n = i * tr + p * half + lax.broadcasted_iota(jnp.int32, (half, 1), 0)
        f = jnp.where(n < l, hp[:, :HY_WIDTH], hp[:, HY_WIDTH:]) * decay
        f = jnp.where(n == l, 0.0, f)
        filt_ref[rows] = f
        total = total + jnp.sum(jnp.abs(f), axis=0, keepdims=True)

    @pl.when(i == 0)
    def _():
        norm_ref[...] = jnp.zeros_like(norm_ref)
    norm_ref[...] += total


def filter_tables(l):
    n = jnp.arange(2 * l)
    pos = jnp.where(n < l, n, 2 * l - n).astype(F32)[:, None]
    t = pos / max(l - 1, 1)
    w = 2.0 * math.pi * pos / l
    f = jnp.linspace(1e-4, FILT_BANDS - 1, FILT_BANDS, dtype=F32)[None, :]
    feat = jnp.concatenate([t, jnp.cos(f * w), -jnp.sin(f * w),
                            jnp.zeros((2 * l, LANES - FILT_EMB), F32)], axis=-1)
    half = min(FILT_TILE, 2 * l) // 2
    feat = feat.reshape(-1, 2, half, LANES).transpose(0, 2, 1, 3).reshape(l, 2 * LANES)
    return feat, t


def hyena_filter(l, tables, w1, b1, f1, w2, b2, f2, w3):
    feat, t = tables
    tr = min(FILT_TILE, 2 * l)
    hid = w2.shape[0]
    max_decay = math.log(DECAY_TARGET) / FAST_DECAY_PCT
    min_decay = math.log(DECAY_TARGET) / SLOW_DECAY_PCT
    deltas = jnp.linspace(min_decay, max_decay, HY_WIDTH, dtype=F32)[None, :]
    w1p = jnp.concatenate([w1, jnp.zeros((LANES - FILT_EMB, hid), F32)], 0)
    twice = lambda m: jax.scipy.linalg.block_diag(m, m)
    pair = lambda v: jnp.tile(v.reshape(1, hid), (1, 2))
    full = lambda a: pl.BlockSpec(a.shape, lambda i: (0,) * a.ndim)
    smalls = [twice(w1p), pair(b1), pair(f1), twice(w2), pair(b2), pair(f2), twice(w3), deltas]
    return pl.pallas_call(
        functools.partial(_filter_kernel, l=l, tr=tr),
        grid=(2 * l // tr,),
        in_specs=[pl.BlockSpec((tr // 2, 2 * LANES), lambda i: (i, 0)),
                  pl.BlockSpec((tr, 1), lambda i: (i, 0))] + [full(a) for a in smalls],
        out_specs=[pl.BlockSpec((tr, HY_WIDTH), lambda i: (i, 0)),
                   pl.BlockSpec((1, HY_WIDTH), lambda i: (0, 0))],
        out_shape=[jax.ShapeDtypeStruct((2 * l, HY_WIDTH), F32),
                   jax.ShapeDtypeStruct((1, HY_WIDTH), F32)],
        compiler_params=_cp("arbitrary"),
        name="hyena_filter",
    )(feat, t, *smalls)


def _dft_rows_kernel(f_ref, x_ref, o_ref):
    n1, nb, c = o_ref.shape[2:]
    xs = jnp.concatenate([x_ref[0, :, j, :] for j in range(nb)], axis=-1).astype(BF16)
    res = _dot(f_ref[...], xs)
    for p in range(2):
        for j in range(nb):
            o_ref[0, p, :, j, :] = res[p * n1:(p + 1) * n1, j * c:(j + 1) * c]


def dft_rows(fmat, x):
    b, k, n2, c = x.shape
    n1 = fmat.shape[0] // 2
    nb = min(DFT_COLS // c, n2)
    return pl.pallas_call(
        _dft_rows_kernel,
        grid=(b, n2 // nb),
        in_specs=[pl.BlockSpec((2 * n1, k), lambda bi, j: (0, 0)),
                  pl.BlockSpec((1, k, nb, c), lambda bi, j: (bi, 0, j, 0))],
        out_specs=pl.BlockSpec((1, 2, n1, nb, c), lambda bi, j: (bi, 0, 0, j, 0)),
        out_shape=jax.ShapeDtypeStruct((b, 2, n1, n2, c), F32),
        compiler_params=_cp("parallel", "parallel"),
        name="dft_rows",
    )(fmat, x)


SPEC_KB = 8


def _inner_dft(g, are, aim):
    n2 = are.shape[0]
    p1 = _dot(g, are)
    p2 = _dot(g, aim)
    return p1[:n2] - p2[n2:], p2[:n2] + p1[n2:]


def _spectrum_kernel(g_ref, a_ref, h_ref):
    for j in range(g_ref.shape[0]):
        h_ref[0, j], h_ref[1, j] = _inner_dft(g_ref[j], a_ref[0, 0, j].astype(BF16), a_ref[0, 1, j].astype(BF16))


def filter_spectrum(gs, a):
    _, _, n1, n2, c = a.shape
    kb = min(SPEC_KB, n1)
    return pl.pallas_call(
        _spectrum_kernel,
        grid=(n1 // kb,),
        in_specs=[pl.BlockSpec((kb, 2 * n2, n2), lambda k: (k, 0, 0)),
                  pl.BlockSpec((1, 2, kb, n2, c), lambda k: (0, 0, k, 0, 0))],
        out_specs=pl.BlockSpec((2, kb, n2, c), lambda k: (0, k, 0, 0)),
        out_shape=jax.ShapeDtypeStruct((2, n1, n2, c), F32),
        compiler_params=_cp("parallel"),
        name="filter_spectrum",
    )(gs, a)


def _spectral_kernel(g_ref, gt_ref, h_ref, a_ref, b_ref):
    n2 = a_ref.shape[3]
    for j in range(g_ref.shape[0]):
        xre, xim = _inner_dft(g_ref[j], a_ref[0, 0, j].astype(BF16), a_ref[0, 1, j].astype(BF16))
        hre, him = h_ref[0, j], h_ref[1, j]
        yre = (xre * hre - xim * him).astype(BF16)
        yim = (xre * him + xim * hre).astype(BF16)
        gt = gt_ref[j]
        q1 = _dot(gt, yre)
        q2 = _dot(gt, yim)
        b_ref[0, 0, j] = q1[:n2] + q2[n2:]
        b_ref[0, 1, j] = q2[:n2] - q1[n2:]


def spectral_multiply(gs, gts, hf, a):
    b, _, n1, n2, c = a.shape
    kb = min(SPEC_KB, n1)
    return pl.pallas_call(
        _spectral_kernel,
        grid=(n1 // kb, b),
        in_specs=[pl.BlockSpec((kb, 2 * n2, n2), lambda k, bi: (k, 0, 0)),
                  pl.BlockSpec((kb, 2 * n2, n2), lambda k, bi: (k, 0, 0)),
                  pl.BlockSpec((2, kb, n2, c), lambda k, bi: (0, k, 0, 0)),
                  pl.BlockSpec((1, 2, kb, n2, c), lambda k, bi: (bi, 0, k, 0, 0))],
        out_specs=pl.BlockSpec((1, 2, kb, n2, c), lambda k, bi: (bi, 0, k, 0, 0)),
        out_shape=jax.ShapeDtypeStruct(a.shape, F32),
        compiler_params=_cp("parallel", "arbitrary"),
        name="spectral_multiply",
    )(gs, gts, hf, a)


def _idft_rows_kernel(f_ref, bm_ref, x0_ref, z_ref, inv_ref, bias_ref, o_ref, *, inv_n):
    k, nb = f_ref.shape[1], o_ref.shape[2]
    f = f_ref[...]
    for j in range(nb):
        bm = bm_ref[0, :, :, j, :].reshape(k, bm_ref.shape[-1]).astype(BF16)
        y = _dot(f, bm) * inv_n
        z = z_ref[0, :, j, :]
        o_ref[0, :, j, :] = x0_ref[0, :, j, :] * (y * inv_ref[...] + bias_ref[...] * z)


def idft_rows_gate(fmat, bm, x0, z, inv_norm, bias, n):
    b, _, n1, n2, c = bm.shape
    m = fmat.shape[0]
    nb = min(DFT_COLS // c, n2)
    blk = pl.BlockSpec((1, m, nb, c), lambda bi, j: (bi, 0, j, 0))
    return pl.pallas_call(
        functools.partial(_idft_rows_kernel, inv_n=1.0 / n),
        grid=(b, n2 // nb),
        in_specs=[pl.BlockSpec((m, 2 * n1), lambda bi, j: (0, 0)),
                  pl.BlockSpec((1, 2, n1, nb, c), lambda bi, j: (bi, 0, 0, j, 0)),
                  blk, blk,
                  pl.BlockSpec((1, c), lambda bi, j: (0, 0)),
                  pl.BlockSpec((1, c), lambda bi, j: (0, 0))],
        out_specs=blk,
        out_shape=jax.ShapeDtypeStruct((b, m, n2, c), F32),
        compiler_params=_cp("parallel", "parallel"),
        name="idft_rows_gate",
    )(fmat, bm, x0, z, inv_norm, bias)


def dft_tables(l):
    n = 2 * l
    n2 = FFT_N2
    n1 = n // n2
    k1 = np.arange(n1)
    ang1 = 2.0 * np.pi * np.outer(k1, k1) / n1
    f1 = np.concatenate([np.cos(ang1), -np.sin(ang1)], 0)
    f3 = np.concatenate([np.cos(ang1), -np.sin(ang1)], 1)[: n1 // 2]
    m2 = np.arange(n2)
    tw = 2.0 * np.pi * np.outer(k1, m2) / n
    ang2 = 2.0 * np.pi * np.outer(m2, m2) / n2
    twr, twi = jnp.asarray(np.cos(tw), F32)[:, None, :], jnp.asarray(-np.sin(tw), F32)[:, None, :]
    f2r, f2i = jnp.asarray(np.cos(ang2), F32)[None], jnp.asarray(-np.sin(ang2), F32)[None]
    gre, gim = twr * f2r - twi * f2i, twr * f2i + twi * f2r
    gs = jnp.concatenate([gre, gim], 1).astype(BF16)
    gts = jnp.concatenate([gre.transpose(0, 2, 1), gim.transpose(0, 2, 1)], 1).astype(BF16)
    return dict(n=n, n1=n1, n2=n2, f1=jnp.asarray(f1, BF16), f3=jnp.asarray(f3, BF16), gs=gs, gts=gts)


def hyena_long_conv(z, x0, filt2, norm, bias, tabs):
    b, l, c = z.shape
    n, n1, n2 = tabs["n"], tabs["n1"], tabs["n2"]
    hf = filter_spectrum(tabs["gs"], dft_rows(tabs["f1"], filt2.reshape(1, n1, n2, c)))
    a = dft_rows(tabs["f1"][:, : n1 // 2], z.reshape(b, n1 // 2, n2, c))
    bm = spectral_multiply(tabs["gs"], tabs["gts"], hf, a)
    y = idft_rows_gate(tabs["f3"], bm, x0.reshape(b, n1 // 2, n2, c), z.reshape(b, n1 // 2, n2, c),
                       1.0 / norm, bias, n)
    return y.reshape(b, l, c)


def _small_conv_kernel(ff_ref, fi_ref, filt_ref, z_ref, x0_ref, inv_ref, bias_ref, o_ref, *, l):
    ff = ff_ref[...]
    hs = _dot(ff, filt_ref[...].astype(BF16))
    hre, him = hs[:2 * l], hs[2 * l:]
    z = z_ref[0]
    xs = _dot(ff[:, :l], z.astype(BF16))
    xre, xim = xs[:2 * l], xs[2 * l:]
    yre = (xre * hre - xim * him).astype(BF16)
    yim = (xre * him + xim * hre).astype(BF16)
    y = _dot(fi_ref[...], jnp.concatenate([yre, yim], 0)) * (0.5 / l)
    o_ref[0] = (x0_ref[0] * (y * inv_ref[...] + bias_ref[...] * z)).astype(o_ref.dtype)


def small_long_conv(z, x0, filt2, norm, bias):
    b, l, c = z.shape
    k = np.arange(2 * l)
    ang = 2.0 * np.pi * np.outer(k, k) / (2 * l)
    ff = jnp.asarray(np.concatenate([np.cos(ang), -np.sin(ang)], 0), BF16)
    fi = jnp.asarray(np.concatenate([np.cos(ang), -np.sin(ang)], 1)[:l], BF16)
    full = lambda a: pl.BlockSpec(a.shape, lambda bi: (0,) * a.ndim)
    row = pl.BlockSpec((1, l, c), lambda bi: (bi, 0, 0))
    inv = 1.0 / norm
    return pl.pallas_call(
        functools.partial(_small_conv_kernel, l=l),
        grid=(b,),
        in_specs=[full(ff), full(fi), full(filt2), row, row, full(inv), full(bias)],
        out_specs=row,
        out_shape=jax.ShapeDtypeStruct((b, l, c), BF16),
        compiler_params=_cp("parallel"),
        name="small_long_conv",
    )(ff, fi, filt2, z, x0, inv, bias)


def _pool_kernel(p_ref, pp_ref, pn_ref, w_ref, sc_ref, o_ref, buf_ref, *, tm, l):
    _fill_padded(buf_ref, p_ref, pp_ref, pn_ref, tm)
    t = pl.program_id(1) * tm + lax.broadcasted_iota(jnp.int32, (tm, LANES), 0)
    lane = lax.broadcasted_iota(jnp.int32, (tm, LANES), 1)
    first_group = lane < POOL_GROUP
    h0 = POOL_HALO

    def win(col, lo, hi):
        acc = buf_ref[h0 + lo:h0 + lo + tm, col]
        for j in range(lo + 1, hi):
            acc = acc + buf_ref[h0 + j:h0 + j + tm, col]
        return acc

    parts = []
    for half, (wa, wb) in enumerate(((2, 4), (8, 16))):
        col = slice(half * LANES, (half + 1) * LANES)
        sa = win(col, -wa // 2, wa // 2)
        sb = sa + win(col, -wb // 2, -wa // 2) + win(col, wa // 2, wb // 2)
        hw = jnp.where(first_group, wa // 2, wb // 2)
        cnt = jnp.minimum(t + hw, l) - jnp.maximum(t - hw, 0)
        mean = jnp.where(first_group, sa, sb) / cnt.astype(F32)
        parts.append(mean - buf_ref[h0:h0 + tm, col])
    d = jnp.concatenate(parts, axis=-1).astype(BF16)
    o_ref[0] = (_dot(d, w_ref[...]) * sc_ref[...]).astype(o_ref.dtype)


def pool_mix(p, pool_w, pool_scale):
    b, l, wd = p.shape
    tm = min(ROW_TILE, l)
    nblk = l // tm
    wblk = jax.scipy.linalg.block_diag(*[pool_w[g] for g in range(len(POOL_WINDOWS))]).astype(BF16)
    return pl.pallas_call(
        functools.partial(_pool_kernel, tm=tm, l=l),
        grid=(b, nblk),
        in_specs=_halo_specs(tm, wd, nblk) + [pl.BlockSpec((wd, wd), lambda bi, i: (0, 0)),
                                              pl.BlockSpec((1, wd), lambda bi, i: (0, 0))],
        out_specs=pl.BlockSpec((1, tm, wd), lambda bi, i: (bi, i, 0)),
        out_shape=jax.ShapeDtypeStruct((b, l, wd), BF16),
        scratch_shapes=[pltpu.VMEM((tm + 2 * POOL_HALO, wd), F32)],
        compiler_params=_cp("parallel", "parallel"),
        name="pool_mix",
    )(p, p, p, wblk, pool_scale.reshape(1, wd))


def _outproj_kernel(x_ref, hy_ref, pool_ref, att_ref, w_ref, gate_ref, o_ref):
    a, b2 = HY_WIDTH, HY_WIDTH + POOL_WIDTH
    y = (_dot(hy_ref[0].astype(BF16), w_ref[0:a]) + _dot(pool_ref[0], w_ref[a:b2]) + _dot(att_ref[0], w_ref[b2:]))
    o_ref[0] = x_ref[0] + gate_ref[0] * y


def out_projection(x, y_hy, y_pool, y_att, w_out, gate):
    b, l, d = x.shape
    tm = min(ROW_TILE, l)
    row = lambda bi, i: (bi, i, 0)
    return pl.pallas_call(
        _outproj_kernel,
        grid=(b, l // tm),
        in_specs=[pl.BlockSpec((1, tm, d), row),
                  pl.BlockSpec((1, tm, y_hy.shape[2]), row),
                  pl.BlockSpec((1, tm, y_pool.shape[2]), row),
                  pl.BlockSpec((1, tm, y_att.shape[2]), row),
                  pl.BlockSpec(w_out.shape, lambda bi, i: (0, 0)),
                  pl.BlockSpec((1, 1, d), lambda bi, i: (bi, 0, 0))],
        out_specs=pl.BlockSpec((1, tm, d), row),
        out_shape=jax.ShapeDtypeStruct(x.shape, F32),
        compiler_params=_cp("parallel", "parallel"),
        name="out_projection",
    )(x, y_hy, y_pool, y_att, w_out, gate)


FF_CHUNK = 256
MOE_TILE = 1024
MOE_ROWS = (128, 192, 256, 320, 384)


def _swiglu_acc(h, wg_ref, wu_ref, wd_ref, lead):
    ff = wg_ref.shape[-1]
    acc = None
    for f0 in range(0, ff, FF_CHUNK):
        cs = slice(f0, min(f0 + FF_CHUNK, ff))
        g = _dot(h, wg_ref[lead + (slice(None), cs)])
        u = _dot(h, wu_ref[lead + (slice(None), cs)])
        a = (g * jax.nn.sigmoid(g) * u).astype(BF16)
        part = _dot(a, wd_ref[lead + (cs, slice(None))])
        acc = part if acc is None else acc + part
    return acc


def _ffn_kernel(x_ref, g_ref, sh_ref, sc_ref, gate_ref, wg_ref, wu_ref, wd_ref, o_ref):
    x = x_ref[0]
    h = _normmod(x, g_ref[...], sh_ref[0], sc_ref[0]).astype(BF16)
    o_ref[0] = x + gate_ref[0] * _swiglu_acc(h, wg_ref, wu_ref, wd_ref, ())


def ffn_dense(x, g, shift, scale, gate, wg, wu, wd):
    b, l, d = x.shape
    tm = min(ROW_TILE, l)
    row = lambda bi, i: (bi, i, 0)
    vec = lambda bi, i: (bi, 0, 0)
    wspec = lambda w: pl.BlockSpec(w.shape, lambda bi, i: (0, 0), pipeline_mode=pl.Buffered(1))
    return pl.pallas_call(
        _ffn_kernel,
        grid=(b, l // tm),
        in_specs=[pl.BlockSpec((1, tm, d), row),
                  pl.BlockSpec((1, d), lambda bi, i: (0, 0)),
                  pl.BlockSpec((1, 1, d), vec), pl.BlockSpec((1, 1, d), vec), pl.BlockSpec((1, 1, d), vec),
                  wspec(wg), wspec(wu), wspec(wd)],
        out_specs=pl.BlockSpec((1, tm, d), row),
        out_shape=jax.ShapeDtypeStruct(x.shape, F32),
        compiler_params=_cp("parallel", "parallel"),
        name="ffn_dense",
    )(x, g, shift, scale, gate, wg, wu, wd)


def _router_kernel(x_ref, g_ref, sh_ref, sc_ref, wr_ref, tri_ref, comb_ref, rank_ref):
    h = _normmod(x_ref[0], g_ref[...], sh_ref[0], sc_ref[0])
    logits = _dot(h, wr_ref[...], precision=HIGHEST)
    lane = lax.broadcasted_iota(jnp.int32, logits.shape, 1)
    neg = jnp.float32(-jnp.inf)
    lg = jnp.where(lane < N_EXPERTS, logits, neg)
    m1 = jnp.max(lg, axis=-1, keepdims=True)
    i1 = jnp.min(jnp.where(lg == m1, lane, LANES), axis=-1, keepdims=True)
    lg2 = jnp.where(lane == i1, neg, lg)
    m2 = jnp.max(lg2, axis=-1, keepdims=True)
    i2 = jnp.min(jnp.where(lg2 == m2, lane, LANES), axis=-1, keepdims=True)
    e = jnp.exp(m2 - m1)
    g1 = 1.0 / (1.0 + e)
    comb_ref[0] = jnp.where(lane == i1, g1, jnp.where(lane == i2, e * g1, 0.0))
    routed = (lane == i1) | (lane == i2)
    before = _dot(tri_ref[...], jnp.where(routed, 1.0, 0.0).astype(BF16))
    rank_ref[0] = jnp.where(routed, before, -1.0)


def moe_route(x, g, shift, scale, w_router):
    b, l, d = x.shape
    tm = min(MOE_TILE, l)
    wr = jnp.concatenate([w_router, jnp.zeros((d, LANES - N_EXPERTS), F32)], -1)
    tri = jnp.asarray(np.tril(np.ones((tm, tm), np.float32), -1), BF16)
    row = lambda bi, i: (bi, i, 0)
    vec = lambda bi, i: (bi, 0, 0)
    return pl.pallas_call(
        _router_kernel,
        grid=(b, l // tm),
        in_specs=[pl.BlockSpec((1, tm, d), row),
                  pl.BlockSpec((1, d), lambda bi, i: (0, 0)),
                  pl.BlockSpec((1, 1, d), vec), pl.BlockSpec((1, 1, d), vec),
                  pl.BlockSpec((d, LANES), lambda bi, i: (0, 0)),
                  pl.BlockSpec((tm, tm), lambda bi, i: (0, 0))],
        out_specs=[pl.BlockSpec((1, tm, LANES), row)] * 2,
        out_shape=[jax.ShapeDtypeStruct((b, l, LANES), F32)] * 2,
        compiler_params=_cp("parallel", "parallel"),
        name="moe_router",
    )(x, g, shift, scale, wr, tri)


def _moe_kernel(x_ref, g_ref, sh_ref, sc_ref, gate_ref, comb_ref, wg_ref, wu_ref, wd_ref, o_ref, acc_ref):
    e = pl.program_id(2)
    x = x_ref[0]
    h = _normmod(x, g_ref[...], sh_ref[0], sc_ref[0]).astype(BF16)
    comb = comb_ref[0]
    lane = lax.broadcasted_iota(jnp.int32, comb.shape, 1)
    ce = jnp.sum(jnp.where(lane == e, comb, 0.0), axis=-1, keepdims=True)
    y = ce * _swiglu_acc(h, wg_ref, wu_ref, wd_ref, (0, 0))

    @pl.when(e == 0)
    def _():
        acc_ref[...] = y

    @pl.when(e > 0)
    def _():
        acc_ref[...] += y

    @pl.when(e == N_EXPERTS - 1)
    def _():
        o_ref[0] = x + gate_ref[0] * acc_ref[...]


def moe_dense(x, g, shift, scale, gate, comb, wg, wu, wd, li):
    b, l, d = x.shape
    tm = min(ROW_TILE, l)
    ff = wg.shape[-1]
    row = lambda bi, i, e: (bi, i, 0)
    vec = lambda bi, i, e: (bi, 0, 0)
    return pl.pallas_call(
        _moe_kernel,
        grid=(b, l // tm, N_EXPERTS),
        in_specs=[pl.BlockSpec((1, tm, d), row),
                  pl.BlockSpec((1, d), lambda bi, i, e: (0, 0)),
                  pl.BlockSpec((1, 1, d), vec), pl.BlockSpec((1, 1, d), vec), pl.BlockSpec((1, 1, d), vec),
                  pl.BlockSpec((1, tm, LANES), row),
                  pl.BlockSpec((1, 1, d, ff), lambda bi, i, e: (li, e, 0, 0)),
                  pl.BlockSpec((1, 1, d, ff), lambda bi, i, e: (li, e, 0, 0)),
                  pl.BlockSpec((1, 1, ff, d), lambda bi, i, e: (li, e, 0, 0))],
        out_specs=pl.BlockSpec((1, tm, d), row),
        out_shape=jax.ShapeDtypeStruct(x.shape, F32),
        scratch_shapes=[pltpu.VMEM((tm, d), F32)],
        compiler_params=_cp("parallel", "parallel", "arbitrary"),
        name="moe_dense",
    )(x, g, shift, scale, gate, comb, wg, wu, wd)


def _moe_sparse_kernel(cnt_ref, x_ref, g_ref, sh_ref, sc_ref, gate_ref, comb_ref, rank_ref, rankt_ref,
                       wg_ref, wu_ref, wd_ref, *rest, t, sizes):
    fg_ref = rest[0] if len(rest) == 3 else None
    o_ref, h_scr = rest[-2:]
    i, e = pl.program_id(0), pl.program_id(1)
    cnt = cnt_ref[i * N_EXPERTS + e]

    @pl.when(e == 0)
    def _():
        x = x_ref[0]
        h_scr[...] = _normmod(x, g_ref[...], sh_ref[0], sc_ref[0]).astype(BF16)
        o_ref[0] = x

    rrow = rankt_ref[0, pl.ds(e, 1), :]
    lane = lax.broadcasted_iota(jnp.int32, (t, LANES), 1)
    rcol = jnp.sum(jnp.where(lane == e, rank_ref[0], 0.0), axis=-1, keepdims=True)
    ccol = jnp.sum(jnp.where(lane == e, comb_ref[0], 0.0), axis=-1, keepdims=True)

    def block(base, r):
        rid = (base + lax.broadcasted_iota(jnp.int32, (r, 1), 0)).astype(F32)
        onehot = jnp.where(rrow == rid, 1.0, 0.0).astype(BF16)
        xg = _dot(onehot, h_scr[...]).astype(BF16)
        y = _swiglu_acc(xg, wg_ref, wu_ref, wd_ref, (0, 0)).astype(BF16)
        cid = (base + lax.broadcasted_iota(jnp.int32, (1, r), 1)).astype(F32)
        onehot_t = jnp.where(rcol == cid, 1.0, 0.0).astype(BF16)
        o_ref[0] += (gate_ref[0] * ccol) * _dot(onehot_t, y)

    big = sizes[-1]
    nfull = cnt // big

    def full(k, carry):
        block(k * big, big)
        return carry
    lax.fori_loop(0, nfull, full, 0)
    base = nfull * big
    rem = cnt - base
    for lo, r in zip((0,) + sizes[:-1], sizes):
        @pl.when((rem > lo) & (rem <= r))
        def _(r=r):
            block(base, r)

    if fg_ref is not None:
        @pl.when(e == N_EXPERTS - 1)
        def _():
            o = o_ref[0]
            o_ref[0] = o * lax.rsqrt(jnp.mean(o * o, axis=-1, keepdims=True) + EPS) * fg_ref[...]


def moe_sparse(x, g, shift, scale, gate, comb, rank, wg, wu, wd, li, final_g=None):
    b, l, d = x.shape
    t = MOE_TILE
    tpb = l // t
    n_tiles = b * tpb
    ff = wg.shape[-1]
    rk = rank[..., :N_EXPERTS].reshape(n_tiles, t, N_EXPERTS)
    rank_t = rk.transpose(0, 2, 1)
    cnt = jnp.sum(rk >= 0, axis=1).astype(jnp.int32).reshape(n_tiles * N_EXPERTS)
    row = lambda i, e, c: (i // tpb, i % tpb, 0)
    vec = lambda i, e, c: (i // tpb, 0, 0)
    grid_spec = pltpu.PrefetchScalarGridSpec(
        num_scalar_prefetch=1,
        grid=(n_tiles, N_EXPERTS),
        in_specs=[pl.BlockSpec((1, t, d), row, pipeline_mode=pl.Buffered(1)),
                  pl.BlockSpec((1, d), lambda i, e, c: (0, 0)),
                  pl.BlockSpec((1, 1, d), vec), pl.BlockSpec((1, 1, d), vec), pl.BlockSpec((1, 1, d), vec),
                  pl.BlockSpec((1, t, LANES), row, pipeline_mode=pl.Buffered(1)),
                  pl.BlockSpec((1, t, LANES), row, pipeline_mode=pl.Buffered(1)),
                  pl.BlockSpec((1, N_EXPERTS, t), lambda i, e, c: (i, 0, 0)),
                  pl.BlockSpec((1, 1, d, ff), lambda i, e, c: (li, e, 0, 0)),
                  pl.BlockSpec((1, 1, d, ff), lambda i, e, c: (li, e, 0, 0)),
                  pl.BlockSpec((1, 1, ff, d), lambda i, e, c: (li, e, 0, 0))]
        + ([] if final_g is None else [pl.BlockSpec((1, d), lambda i, e, c: (0, 0))]),
        out_specs=pl.BlockSpec((1, t, d), row),
        scratch_shapes=[pltpu.VMEM((t, d), BF16)],
    )
    return pl.pallas_call(
        functools.partial(_moe_sparse_kernel, t=t, sizes=MOE_ROWS),
        grid_spec=grid_spec,
        out_shape=jax.ShapeDtypeStruct(x.shape, F32),
        compiler_params=_cp("parallel", "arbitrary"),
        name="moe_sparse",
    )(cnt, x, g, shift, scale, gate, comb, rank, rank_t, wg, wu, wd, *(() if final_g is None else (final_g,)))


def _final_kernel(x_ref, g_ref, o_ref):
    x = x_ref[0]
    o_ref[0] = x * lax.rsqrt(jnp.mean(x * x, axis=-1, keepdims=True) + EPS) * g_ref[...]


def final_norm(x, g):
    b, l, d = x.shape
    tm = min(2 * ROW_TILE, l)
    return pl.pallas_call(
        _final_kernel,
        grid=(b, l // tm),
        in_specs=[pl.BlockSpec((1, tm, d), lambda bi, i: (bi, i, 0)),
                  pl.BlockSpec((1, d), lambda bi, i: (0, 0))],
        out_specs=pl.BlockSpec((1, tm, d), lambda bi, i: (bi, i, 0)),
        out_shape=jax.ShapeDtypeStruct(x.shape, F32),
        compiler_params=_cp("parallel", "parallel"),
        name="final_norm",
    )(x, g)


def kernel(x, c, ctx, c_ctx, norm1_g, norm2_g, w_ada, b_ada, w_in, hy_short_w, hy_short_b, filt_w1, filt_b1, filt_freq1, filt_w2, filt_b2, filt_freq2, filt_w3, hy_bias, pool_w, pool_scale, q_norm_g, w_uq, kv_norm_g, w_ukv, w_out, ffn_wg, ffn_wu, ffn_wd, moe_router, moe_wg, moe_wu, moe_wd, final_g):
    b, l, d = x.shape
    lc = ctx.shape[1]
    depth = w_ada.shape[0]

    cc = jnp.zeros((8, d), F32).at[:b].set(c).at[b].set(c_ctx)
    mods = ada_vectors(cc, w_ada, b_ada)

    rope_l = rope_tables(l, True)
    rope_c = rope_tables(lc, False)
    ftab_l, ftab_c = filter_tables(l), filter_tables(lc)
    dtabs = dft_tables(l)
    moe_w = (moe_wg.astype(BF16), moe_wu.astype(BF16), moe_wd.astype(BF16))

    xl, xc = x, ctx
    for layer in range(depth):
        last = layer == depth - 1
        mod = mods[layer].reshape(8, 6, 1, d)
        mod_l = [mod[:b, j] for j in range(6)]
        mod_c = [jnp.broadcast_to(mod[b:b + 1, j], (b, 1, d)) for j in range(6)]
        g1 = norm1_g[layer].reshape(1, d)
        g2 = norm2_g[layer].reshape(1, d)
        w_in_l = w_in[layer].astype(BF16)
        w_out_l = w_out[layer].astype(BF16)
        mw = mla_weights(q_norm_g[layer], w_uq[layer], kv_norm_g[layer], w_ukv[layer])
        fw = (filt_w1[layer], filt_b1[layer], filt_freq1[layer], filt_w2[layer], filt_b2[layer],
              filt_freq2[layer], filt_w3[layer])
        bias = hy_bias[layer].reshape(1, HY_WIDTH)

        def channel_mix(xs, md, final=False):
            i = layer // 2
            fg = final_g.reshape(1, d)
            if layer % 2 == 1 and xs.shape[1] % MOE_TILE == 0:
                comb, rank = moe_route(xs, g2, md[3], md[4], moe_router[i])
                return moe_sparse(xs, g2, md[3], md[4], md[5], comb, rank, *moe_w, i, fg if final else None)
            if layer % 2 == 0:
                out = ffn_dense(xs, g2, md[3], md[4], md[5], ffn_wg[i].astype(BF16),
                                ffn_wu[i].astype(BF16), ffn_wd[i].astype(BF16))
            else:
                comb, _ = moe_route(xs, g2, md[3], md[4], moe_router[i])
                out = moe_dense(xs, g2, md[3], md[4], md[5], comb, *moe_w, i)
            return final_norm(out, fg) if final else out

        hy_l, pool_l, q_l, kt_l, v_l = in_projection(xl, g1, mod_l[0], mod_l[1], w_in_l, *rope_l, mw)
        hy_c, pool_c, q_c, kt_c, v_c = in_projection(xc, g1, mod_c[0], mod_c[1], w_in_l, *rope_c, mw)

        att_l = attention(q_l, kt_c, v_c, kt_l, v_l)
        z_l, x0_l = hyena_pre(hy_l, hy_short_w[layer], hy_short_b[layer])
        filt_l, nrm_l = hyena_filter(l, ftab_l, *fw)
        yhy_l = hyena_long_conv(z_l, x0_l, filt_l, nrm_l, bias, dtabs)
        ypool_l = pool_mix(pool_l, pool_w[layer], pool_scale[layer])
        xl = out_projection(xl, yhy_l, ypool_l, att_l, w_out_l, mod_l[2])

        if not last:
            att_c = attention(q_c, kt_c, v_c)
            z_c, x0_c = hyena_pre(hy_c, hy_short_w[layer], hy_short_b[layer])
            filt_c, nrm_c = hyena_filter(lc, ftab_c, *fw)
            yhy_c = small_long_conv(z_c, x0_c, filt_c, nrm_c, bias)
            ypool_c = pool_mix(pool_c, pool_w[layer], pool_scale[layer])
            xc = out_projection(xc, yhy_c, ypool_c, att_c, w_out_l, mod_c[2])
            xc = channel_mix(xc, mod_c)

        xl = channel_mix(xl, mod_l, final=last)
    return xl
```

```python
import functools
import math

import jax
import jax.numpy as jnp
import numpy as np
from jax import lax
from jax.experimental import pallas as pl
from jax.experimental.pallas import tpu as pltpu

F32 = jnp.float32
BF16 = jnp.bfloat16
HIGHEST = lax.Precision.HIGHEST

EPS = 1e-6
GRID_W = 64
HY_WIDTH = 256
POOL_WINDOWS = (2, 4, 8, 16)
POOL_WIDTH = 256
POOL_GROUP = 64
POOL_HALO = 8
MLA_HEADS = 8
QK_NOPE = 64
QK_ROPE = 32
V_DIM = 64
Q_LORA = 384
KV_LORA = 256
HEAD_PAD = 128
SM_SCALE = (QK_NOPE + QK_ROPE) ** -0.5
ROPE_THETA = 10000.0
FILT_EMB = 33
FILT_BANDS = 16
DECAY_TARGET = 1e-2
FAST_DECAY_PCT = 0.3
SLOW_DECAY_PCT = 1.5
N_EXPERTS = 8
LANES = 128
FFT_N2 = 128
VMEM_LIMIT = 56 * 2 ** 20
ROW_TILE = 512
DFT_COLS = 2048
ADA_COLS = 1536


def _cp(*sem):
    return pltpu.CompilerParams(dimension_semantics=sem, vmem_limit_bytes=VMEM_LIMIT)


def _dot(a, b, **kw):
    return jnp.dot(a, b, preferred_element_type=F32, **kw)


def _normmod(x, g, shift, scale):
    ms = jnp.mean(x * x, axis=-1, keepdims=True)
    return x * lax.rsqrt(ms + EPS) * (g * (1.0 + scale)) + shift


def _ada_kernel(c_ref, w_ref, b_ref, o_ref):
    c = c_ref[...]
    s = c * jax.nn.sigmoid(c)
    o_ref[0] = _dot(s, w_ref[0], precision=HIGHEST) + b_ref[0]


def ada_vectors(cc, w_ada, b_ada):
    depth, d, n = w_ada.shape
    tn = ADA_COLS
    return pl.pallas_call(
        _ada_kernel,
        grid=(depth, n // tn),
        in_specs=[pl.BlockSpec((8, d), lambda l, j: (0, 0)),
                  pl.BlockSpec((1, d, tn), lambda l, j: (l, 0, j)),
                  pl.BlockSpec((1, 1, tn), lambda l, j: (l, 0, j))],
        out_specs=pl.BlockSpec((1, 8, tn), lambda l, j: (l, 0, j)),
        out_shape=jax.ShapeDtypeStruct((depth, 8, n), F32),
        compiler_params=_cp("parallel", "parallel"),
        name="ada_vectors",
    )(cc, w_ada, b_ada.reshape(depth, 1, n))


IN_HY, IN_POOL = 768, 256
IN_OFFS = (0, 768, 1024, 1408, 1664, 1696)


def _inproj_kernel(x_ref, g_ref, sh_ref, sc_ref, w_ref, c_ref, s_ref, wqa_ref, wqb_ref, wk_ref, wv_ref,
                   pa_ref, pb_ref, hy_ref, pool_ref, q_ref, kt_ref, v_ref):
    h = _normmod(x_ref[0], g_ref[...], sh_ref[0], sc_ref[0]).astype(BF16)
    y = _dot(h, w_ref[...])
    o = IN_OFFS
    hy_ref[0] = y[:, o[0]:o[1]]
    pool_ref[0] = y[:, o[1]:o[2]]
    cq, ckv = y[:, o[2]:o[3]], y[:, o[3]:o[4]]
    kpe = y[:, o[4]:o[5]].astype(BF16)
    cqn = (cq * lax.rsqrt(jnp.mean(cq * cq, axis=-1, keepdims=True) + EPS)).astype(BF16)
    ckvn = (ckv * lax.rsqrt(jnp.mean(ckv * ckv, axis=-1, keepdims=True) + EPS)).astype(BF16)
    cs, sn = c_ref[...], s_ref[...]
    qa = _dot(cqn, wqa_ref[...])
    qb = _dot(cqn, wqb_ref[...])
    kn = _dot(ckvn, wk_ref[...])
    vv = _dot(ckvn, wv_ref[...])
    kpa = _dot(kpe, pa_ref[...])
    kpb = _dot(kpe, pb_ref[...])
    kpe_rot = kpa * cs + kpb * sn
    lane = lax.broadcasted_iota(jnp.int32, (1, HEAD_PAD), 1)
    for hd in range(MLA_HEADS):
        sl = slice(hd * HEAD_PAD, (hd + 1) * HEAD_PAD)
        q_ref[0, hd] = (qa[:, sl] * cs + qb[:, sl] * sn).astype(BF16)
        kt_ref[0, hd] = (kn[:, sl] + kpe_rot).T.astype(BF16)
        one_lane = V_DIM if hd % 2 == 0 else 0
        v_ref[0, hd] = jnp.where(lane == one_lane, 1.0, vv[:, sl]).astype(BF16)


def in_projection(x, g, shift, scale, w_in, ctab, stab, wts):
    b, l, d = x.shape
    tm = min(ROW_TILE, l)
    hw = MLA_HEADS
    row = lambda bi, i: (bi, i, 0)
    vec = lambda bi, i: (bi, 0, 0)
    full = lambda a: pl.BlockSpec(a.shape, lambda bi, i: (0,) * a.ndim)
    return pl.pallas_call(
        _inproj_kernel,
        grid=(b, l // tm),
        in_specs=[pl.BlockSpec((1, tm, d), row),
                  pl.BlockSpec((1, d), lambda bi, i: (0, 0)),
                  pl.BlockSpec((1, 1, d), vec),
                  pl.BlockSpec((1, 1, d), vec),
                  full(w_in),
                  pl.BlockSpec((tm, HEAD_PAD), lambda bi, i: (i, 0)),
                  pl.BlockSpec((tm, HEAD_PAD), lambda bi, i: (i, 0))] + [full(w) for w in wts],
        out_specs=[pl.BlockSpec((1, tm, IN_HY), row),
                   pl.BlockSpec((1, tm, IN_POOL), row),
                   pl.BlockSpec((1, hw, tm, HEAD_PAD), lambda bi, i: (bi, 0, i, 0)),
                   pl.BlockSpec((1, hw, HEAD_PAD, tm), lambda bi, i: (bi, 0, 0, i)),
                   pl.BlockSpec((1, hw, tm, HEAD_PAD), lambda bi, i: (bi, 0, i, 0))],
        out_shape=[jax.ShapeDtypeStruct((b, l, IN_HY), F32),
                   jax.ShapeDtypeStruct((b, l, IN_POOL), F32),
                   jax.ShapeDtypeStruct((b, hw, l, HEAD_PAD), BF16),
                   jax.ShapeDtypeStruct((b, hw, HEAD_PAD, l), BF16),
                   jax.ShapeDtypeStruct((b, hw, l, HEAD_PAD), BF16)],
        compiler_params=_cp("parallel", "parallel"),
        name="in_projection",
    )(x, g, shift, scale, w_in, ctab, stab, *wts)


def mla_weights(q_norm_g, w_uq, kv_norm_g, w_ukv):
    hw, dq = MLA_HEADS, QK_NOPE + QK_ROPE
    half = QK_ROPE // 2
    wq = (w_uq * q_norm_g[:, None] * (SM_SCALE * math.log2(math.e))).reshape(Q_LORA, hw, dq)
    pad = jnp.zeros((Q_LORA, hw, HEAD_PAD - dq), F32)
    wqa = jnp.concatenate([wq, pad], -1)
    swap = jnp.concatenate([wq[..., QK_NOPE + half:], wq[..., QK_NOPE:QK_NOPE + half]], -1)
    wqb = jnp.concatenate([jnp.zeros((Q_LORA, hw, QK_NOPE), F32), swap, pad], -1)
    wkv = (w_ukv * kv_norm_g[:, None]).reshape(KV_LORA, hw, QK_NOPE + V_DIM)
    z64 = jnp.zeros((KV_LORA, hw, 64), F32)
    wk = jnp.concatenate([wkv[..., :QK_NOPE], z64], -1)
    v = wkv[..., QK_NOPE:]
    wv = jnp.where((jnp.arange(hw) % 2 == 0)[None, :, None],
                   jnp.concatenate([v, z64], -1), jnp.concatenate([z64, v], -1))
    eye = np.eye(QK_ROPE, dtype=np.float32)
    pa = np.zeros((QK_ROPE, HEAD_PAD), np.float32)
    pa[:, QK_NOPE:QK_NOPE + QK_ROPE] = eye
    pb = np.zeros((QK_ROPE, HEAD_PAD), np.float32)
    pb[:, QK_NOPE:QK_NOPE + QK_ROPE] = np.concatenate([eye[:, half:], eye[:, :half]], 1)
    flat = lambda w: w.reshape(w.shape[0], hw * HEAD_PAD).astype(BF16)
    return (flat(wqa), flat(wqb), flat(wk), flat(wv), jnp.asarray(pa, BF16), jnp.asarray(pb, BF16))


def rope_tables(l, with_rope):
    ctab = np.zeros((l, HEAD_PAD), np.float32)
    stab = np.zeros((l, HEAD_PAD), np.float32)
    ctab[:, :QK_NOPE + QK_ROPE] = 1.0
    if not with_rope:
        return jnp.asarray(ctab), jnp.asarray(stab)
    n_freq = QK_ROPE // 4
    inv_freq = 1.0 / (ROPE_THETA ** (jnp.arange(n_freq, dtype=F32) / n_freq))
    rows = l // GRID_W
    row = jnp.repeat(jnp.arange(rows, dtype=F32), GRID_W)
    col = jnp.tile(jnp.arange(GRID_W, dtype=F32), rows)
    ang = jnp.concatenate([row[:, None] * inv_freq, col[:, None] * inv_freq], axis=-1)
    cos, sin = jnp.cos(ang), jnp.sin(ang)
    one = jnp.ones((l, QK_NOPE), F32)
    zero = jnp.zeros((l, QK_NOPE), F32)
    tail = jnp.zeros((l, HEAD_PAD - QK_NOPE - QK_ROPE), F32)
    return (jnp.concatenate([one, cos, cos, tail], -1), jnp.concatenate([zero, -sin, sin, tail], -1))


ATT_TQ = 1024
ATT_RB = 1024
ATT_CK = 2048


def _row_max(s):
    mp = s[:, 0:LANES]
    for j in range(1, s.shape[1] // LANES):
        mp = jnp.maximum(mp, s[:, j * LANES:(j + 1) * LANES])
    return jnp.max(mp, axis=-1, keepdims=True)


def _attn_kernel(q_ref, kct_ref, vc_ref, *rest, n_chunks, ck, tq):
    if n_chunks:
        kt_ref, v_ref, o_ref = rest
    else:
        (o_ref,) = rest
    rb = min(ATT_RB, tq)
    chains = [(hh, r) for hh in range(2) for r in range(tq // rb)]
    lane = lax.broadcasted_iota(jnp.int32, (1, HEAD_PAD), 1)

    def q_rows(hh, r):
        return q_ref[0, hh, r * rb:(r + 1) * rb, :]

    state = []
    for hh, r in chains:
        s = _dot(q_rows(hh, r), kct_ref[0, hh])
        m = _row_max(s)
        p = jnp.exp2(s - m).astype(BF16)
        state += [m, _dot(p, vc_ref[0, hh])]

    if n_chunks:
        def body(c, state):
            off = pl.multiple_of(c * ck, ck)
            out = []
            for i, (hh, r) in enumerate(chains):
                m, acc = state[2 * i], state[2 * i + 1]
                s = _dot(q_rows(hh, r), kt_ref[0, hh, :, pl.ds(off, ck)])
                m_new = jnp.maximum(m, _row_max(s))
                alpha = jnp.exp2(m - m_new)
                p = jnp.exp2(s - m_new).astype(BF16)
                out += [m_new, acc * alpha + _dot(p, v_ref[0, hh, pl.ds(off, ck), :])]
            return tuple(out)
        state = lax.fori_loop(0, n_chunks, body, tuple(state), unroll=2 if n_chunks % 2 == 0 else 1)

    for r in range(tq // rb):
        outs = []
        for hh in range(2):
            acc = state[2 * chains.index((hh, r)) + 1]
            one_lane = V_DIM if hh == 0 else 0
            denom = jnp.sum(jnp.where(lane == one_lane, acc, 0.0), axis=-1, keepdims=True)
            outs.append(acc / denom)
        o_ref[0, r * rb:(r + 1) * rb, :] = jnp.where(lane < V_DIM, outs[0], outs[1]).astype(o_ref.dtype)


def attention(q, kct, vc, kt=None, v=None):
    b, hw, lq, _ = q.shape
    lc = vc.shape[2]
    tq = min(ATT_TQ, lq)
    ck = ATT_CK if kt is None else min(ATT_CK, kt.shape[3])
    n_chunks = 0 if kt is None else kt.shape[3] // ck
    in_specs = [pl.BlockSpec((1, 2, tq, HEAD_PAD), lambda bi, hp, i: (bi, hp, i, 0)),
                pl.BlockSpec((1, 2, HEAD_PAD, lc), lambda bi, hp, i: (bi, hp, 0, 0)),
                pl.BlockSpec((1, 2, lc, HEAD_PAD), lambda bi, hp, i: (bi, hp, 0, 0))]
    args = [q, kct, vc]
    if n_chunks:
        l = kt.shape[3]
        in_specs += [pl.BlockSpec((1, 2, HEAD_PAD, l), lambda bi, hp, i: (bi, hp, 0, 0)),
                     pl.BlockSpec((1, 2, l, HEAD_PAD), lambda bi, hp, i: (bi, hp, 0, 0))]
        args += [kt, v]
    return pl.pallas_call(
        functools.partial(_attn_kernel, n_chunks=n_chunks, ck=ck, tq=tq),
        grid=(b, hw // 2, lq // tq),
        in_specs=in_specs,
        out_specs=pl.BlockSpec((1, tq, HEAD_PAD), lambda bi, hp, i: (bi, i, hp)),
        out_shape=jax.ShapeDtypeStruct((b, lq, hw * V_DIM), BF16),
        compiler_params=_cp("parallel", "parallel", "arbitrary"),
        name="attention",
    )(*args)


def _halo_specs(tm, width, nblk):
    r = tm // POOL_HALO
    last = nblk * r - 1
    return [pl.BlockSpec((1, tm, width), lambda bi, i: (bi, i, 0)),
            pl.BlockSpec((1, POOL_HALO, width), lambda bi, i: (bi, jnp.maximum(i * r - 1, 0), 0)),
            pl.BlockSpec((1, POOL_HALO, width), lambda bi, i: (bi, jnp.minimum((i + 1) * r, last), 0))]


def _fill_padded(buf_ref, cur_ref, prev_ref, next_ref, tm):
    i = pl.program_id(1)
    first = i == 0
    last = i == pl.num_programs(1) - 1
    buf_ref[0:POOL_HALO] = jnp.where(first, 0.0, prev_ref[0])
    buf_ref[POOL_HALO:POOL_HALO + tm] = cur_ref[0]
    buf_ref[POOL_HALO + tm:] = jnp.where(last, 0.0, next_ref[0])


def _hyena_pre_kernel(u_ref, up_ref, un_ref, w_ref, b_ref, z_ref, x0_ref, buf_ref, *, tm):
    _fill_padded(buf_ref, u_ref, up_ref, un_ref, tm)
    w = w_ref[...]
    uc = (buf_ref[POOL_HALO - 1:POOL_HALO - 1 + tm] * w[0:1] + buf_ref[POOL_HALO:POOL_HALO + tm] * w[1:2]
          + buf_ref[POOL_HALO + 1:POOL_HALO + 1 + tm] * w[2:3] + b_ref[...])
    x0_ref[0] = uc[:, :HY_WIDTH]
    z_ref[0] = uc[:, HY_WIDTH:2 * HY_WIDTH] * uc[:, 2 * HY_WIDTH:]


def hyena_pre(u, sw, sb):
    b, l, wd = u.shape
    tm = min(ROW_TILE, l)
    nblk = l // tm
    return pl.pallas_call(
        functools.partial(_hyena_pre_kernel, tm=tm),
        grid=(b, nblk),
        in_specs=_halo_specs(tm, wd, nblk) + [pl.BlockSpec((3, wd), lambda bi, i: (0, 0)),
                                              pl.BlockSpec((1, wd), lambda bi, i: (0, 0))],
        out_specs=[pl.BlockSpec((1, tm, HY_WIDTH), lambda bi, i: (bi, i, 0))] * 2,
        out_shape=[jax.ShapeDtypeStruct((b, l, HY_WIDTH), F32)] * 2,
        scratch_shapes=[pltpu.VMEM((tm + 2 * POOL_HALO, wd), F32)],
        compiler_params=_cp("parallel", "parallel"),
        name="hyena_pre",
    )(u, u, u, sw, sb.reshape(1, wd))


FILT_TILE = 512


def _filter_kernel(feat_ref, tdec_ref, w1_ref, b1_ref, f1_ref, w2_ref, b2_ref, f2_ref, w3_ref, dl_ref,
                   filt_ref, norm_ref, *, l, tr):
    i = pl.program_id(0)
    half = tr // 2
    h = jnp.sin(f1_ref[...] * (_dot(feat_ref[...], w1_ref[...], precision=HIGHEST) + b1_ref[...]))
    h = jnp.sin(f2_ref[...] * (_dot(h, w2_ref[...], precision=HIGHEST) + b2_ref[...]))
    h = _dot(h, w3_ref[...], precision=HIGHEST)
    dl = jnp.abs(dl_ref[...])
    total = jnp.zeros((1, HY_WIDTH), F32)
    for p in range(2):
        rows = slice(p * half, (p + 1) * half)
        hp = h[:, p * 2 * HY_WIDTH:(p + 1) * 2 * HY_WIDTH]
        decay = jnp.exp(-tdec_ref[rows] * dl)
        n = i * tr + p * half + lax.broadcasted_iota(jnp.int32, (half, 1), 0)
        f = jnp.where(n < l, hp[:, :HY_WIDTH], hp[:, HY_WIDTH:]) * decay
        f = jnp.where(n == l, 0.0, f)
        filt_ref[rows] = f
        total = total + jnp.sum(jnp.abs(f), axis=0, keepdims=True)

    @pl.when(i == 0)
    def _():
        norm_ref[...] = jnp.zeros_like(norm_ref)
    norm_ref[...] += total


def filter_tables(l):
    n = jnp.arange(2 * l)
    pos = jnp.where(n < l, n, 2 * l - n).astype(F32)[:, None]
    t = pos / max(l - 1, 1)
    w = 2.0 * math.pi * pos / l
    f = jnp.linspace(1e-4, FILT_BANDS - 1, FILT_BANDS, dtype=F32)[None, :]
    feat = jnp.concatenate([t, jnp.cos(f * w), -jnp.sin(f * w),
                            jnp.zeros((2 * l, LANES - FILT_EMB), F32)], axis=-1)
    half = min(FILT_TILE, 2 * l) // 2
    feat = feat.reshape(-1, 2, half, LANES).transpose(0, 2, 1, 3).reshape(l, 2 * LANES)
    return feat, t


def hyena_filter(l, tables, w1, b1, f1, w2, b2, f2, w3):
    feat, t = tables
    tr = min(FILT_TILE, 2 * l)
    hid = w2.shape[0]
    max_decay = math.log(DECAY_TARGET) / FAST_DECAY_PCT
    min_decay = math.log(DECAY_TARGET) / SLOW_DECAY_PCT
    deltas = jnp.linspace(min_decay, max_decay, HY_WIDTH, dtype=F32)[None, :]
    w1p = jnp.concatenate([w1, jnp.zeros((LANES - FILT_EMB, hid), F32)], 0)
    twice = lambda m: jax.scipy.linalg.block_diag(m, m)
    pair = lambda v: jnp.tile(v.reshape(1, hid), (1, 2))
    full = lambda a: pl.BlockSpec(a.shape, lambda i: (0,) * a.ndim)
    smalls = [twice(w1p), pair(b1), pair(f1), twice(w2), pair(b2), pair(f2), twice(w3), deltas]
    return pl.pallas_call(
        functools.partial(_filter_kernel, l=l, tr=tr),
        grid=(2 * l // tr,),
        in_specs=[pl.BlockSpec((tr // 2, 2 * LANES), lambda i: (i, 0)),
                  pl.BlockSpec((tr, 1), lambda i: (i, 0))] + [full(a) for a in smalls],
        out_specs=[pl.BlockSpec((tr, HY_WIDTH), lambda i: (i, 0)),
                   pl.BlockSpec((1, HY_WIDTH), lambda i: (0, 0))],
        out_shape=[jax.ShapeDtypeStruct((2 * l, HY_WIDTH), F32),
                   jax.ShapeDtypeStruct((1, HY_WIDTH), F32)],
        compiler_params=_cp("arbitrary"),
        name="hyena_filter",
    )(feat, t, *smalls)


def _dft_rows_kernel(f_ref, x_ref, o_ref):
    n1, nb, c = o_ref.shape[2:]
    xs = jnp.concatenate([x_ref[0, :, j, :] for j in range(nb)], axis=-1).astype(BF16)
    res = _dot(f_ref[...], xs)
    for p in range(2):
        for j in range(nb):
            o_ref[0, p, :, j, :] = res[p * n1:(p + 1) * n1, j * c:(j + 1) * c]


def dft_rows(fmat, x):
    b, k, n2, c = x.shape
    n1 = fmat.shape[0] // 2
    nb = min(DFT_COLS // c, n2)
    return pl.pallas_call(
        _dft_rows_kernel,
        grid=(b, n2 // nb),
        in_specs=[pl.BlockSpec((2 * n1, k), lambda bi, j: (0, 0)),
                  pl.BlockSpec((1, k, nb, c), lambda bi, j: (bi, 0, j, 0))],
        out_specs=pl.BlockSpec((1, 2, n1, nb, c), lambda bi, j: (bi, 0, 0, j, 0)),
        out_shape=jax.ShapeDtypeStruct((b, 2, n1, n2, c), F32),
        compiler_params=_cp("parallel", "parallel"),
        name="dft_rows",
    )(fmat, x)


SPEC_KB = 8


def _inner_dft(g, are, aim):
    n2 = are.shape[0]
    p1 = _dot(g, are)
    p2 = _dot(g, aim)
    return p1[:n2] - p2[n2:], p2[:n2] + p1[n2:]


def _spectrum_kernel(g_ref, a_ref, h_ref):
    for j in range(g_ref.shape[0]):
        h_ref[0, j], h_ref[1, j] = _inner_dft(g_ref[j], a_ref[0, 0, j].astype(BF16), a_ref[0, 1, j].astype(BF16))


def filter_spectrum(gs, a):
    _, _, n1, n2, c = a.shape
    kb = min(SPEC_KB, n1)
    return pl.pallas_call(
        _spectrum_kernel,
        grid=(n1 // kb,),
        in_specs=[pl.BlockSpec((kb, 2 * n2, n2), lambda k: (k, 0, 0)),
                  pl.BlockSpec((1, 2, kb, n2, c), lambda k: (0, 0, k, 0, 0))],
        out_specs=pl.BlockSpec((2, kb, n2, c), lambda k: (0, k, 0, 0)),
        out_shape=jax.ShapeDtypeStruct((2, n1, n2, c), F32),
        compiler_params=_cp("parallel"),
        name="filter_spectrum",
    )(gs, a)


def _spectral_kernel(g_ref, gt_ref, h_ref, a_ref, b_ref):
    n2 = a_ref.shape[3]
    for j in range(g_ref.shape[0]):
        xre, xim = _inner_dft(g_ref[j], a_ref[0, 0, j].astype(BF16), a_ref[0, 1, j].astype(BF16))
        hre, him = h_ref[0, j], h_ref[1, j]
        yre = (xre * hre - xim * him).astype(BF16)
        yim = (xre * him + xim * hre).astype(BF16)
        gt = gt_ref[j]
        q1 = _dot(gt, yre)
        q2 = _dot(gt, yim)
        b_ref[0, 0, j] = q1[:n2] + q2[n2:]
        b_ref[0, 1, j] = q2[:n2] - q1[n2:]


def spectral_multiply(gs, gts, hf, a):
    b, _, n1, n2, c = a.shape
    kb = min(SPEC_KB, n1)
    return pl.pallas_call(
        _spectral_kernel,
        grid=(n1 // kb, b),
        in_specs=[pl.BlockSpec((kb, 2 * n2, n2), lambda k, bi: (k, 0, 0)),
                  pl.BlockSpec((kb, 2 * n2, n2), lambda k, bi: (k, 0, 0)),
                  pl.BlockSpec((2, kb, n2, c), lambda k, bi: (0, k, 0, 0)),
                  pl.BlockSpec((1, 2, kb, n2, c), lambda k, bi: (bi, 0, k, 0, 0))],
        out_specs=pl.BlockSpec((1, 2, kb, n2, c), lambda k, bi: (bi, 0, k, 0, 0)),
        out_shape=jax.ShapeDtypeStruct(a.shape, F32),
        compiler_params=_cp("parallel", "arbitrary"),
        name="spectral_multiply",
    )(gs, gts, hf, a)


def _idft_rows_kernel(f_ref, bm_ref, x0_ref, z_ref, inv_ref, bias_ref, o_ref, *, inv_n):
    m, nb, c = o_ref.shape[1:]
    bflat = bm_ref[0].reshape(-1, c).astype(BF16)
    y = _dot(f_ref[...], bflat) * inv_n
    for j in range(nb):
        z = z_ref[0, :, j, :]
        o_ref[0, :, j, :] = x0_ref[0, :, j, :] * (y[j * m:(j + 1) * m] * inv_ref[...] + bias_ref[...] * z)


def idft_rows_gate(fmat, bm, x0, z, inv_norm, bias, n):
    b, _, n1, n2, c = bm.shape
    m = fmat.shape[0]
    nb = min(DFT_COLS // c, n2)
    spread = jnp.kron(fmat, jnp.eye(nb, dtype=fmat.dtype)).reshape(m, nb, 2 * n1 * nb)
    spread = spread.transpose(1, 0, 2).reshape(nb * m, 2 * n1 * nb)
    blk = pl.BlockSpec((1, m, nb, c), lambda bi, j: (bi, 0, j, 0))
    return pl.pallas_call(
        functools.partial(_idft_rows_kernel, inv_n=1.0 / n),
        grid=(b, n2 // nb),
        in_specs=[pl.BlockSpec(spread.shape, lambda bi, j: (0, 0), pipeline_mode=pl.Buffered(1)),
                  pl.BlockSpec((1, 2, n1, nb, c), lambda bi, j: (bi, 0, 0, j, 0)),
                  blk, blk,
                  pl.BlockSpec((1, c), lambda bi, j: (0, 0)),
                  pl.BlockSpec((1, c), lambda bi, j: (0, 0))],
        out_specs=blk,
        out_shape=jax.ShapeDtypeStruct((b, m, n2, c), F32),
        compiler_params=_cp("parallel", "parallel"),
        name="idft_rows_gate",
    )(spread, bm, x0, z, inv_norm, bias)


def dft_tables(l):
    n = 2 * l
    n2 = FFT_N2
    n1 = n // n2
    k1 = np.arange(n1)
    ang1 = 2.0 * np.pi * np.outer(k1, k1) / n1
    f1 = np.concatenate([np.cos(ang1), -np.sin(ang1)], 0)
    f3 = np.concatenate([np.cos(ang1), -np.sin(ang1)], 1)[: n1 // 2]
    m2 = np.arange(n2)
    tw = 2.0 * np.pi * np.outer(k1, m2) / n
    ang2 = 2.0 * np.pi * np.outer(m2, m2) / n2
    twr, twi = jnp.asarray(np.cos(tw), F32)[:, None, :], jnp.asarray(-np.sin(tw), F32)[:, None, :]
    f2r, f2i = jnp.asarray(np.cos(ang2), F32)[None], jnp.asarray(-np.sin(ang2), F32)[None]
    gre, gim = twr * f2r - twi * f2i, twr * f2i + twi * f2r
    gs = jnp.concatenate([gre, gim], 1).astype(BF16)
    gts = jnp.concatenate([gre.transpose(0, 2, 1), gim.transpose(0, 2, 1)], 1).astype(BF16)
    return dict(n=n, n1=n1, n2=n2, f1=jnp.asarray(f1, BF16), f3=jnp.asarray(f3, BF16), gs=gs, gts=gts)


def hyena_long_conv(z, x0, filt2, norm, bias, tabs):
    b, l, c = z.shape
    n, n1, n2 = tabs["n"], tabs["n1"], tabs["n2"]
    hf = filter_spectrum(tabs["gs"], dft_rows(tabs["f1"], filt2.reshape(1, n1, n2, c)))
    a = dft_rows(tabs["f1"][:, : n1 // 2], z.reshape(b, n1 // 2, n2, c))
    bm = spectral_multiply(tabs["gs"], tabs["gts"], hf, a)
    y = idft_rows_gate(tabs["f3"], bm, x0.reshape(b, n1 // 2, n2, c), z.reshape(b, n1 // 2, n2, c),
                       1.0 / norm, bias, n)
    return y.reshape(b, l, c)


def _small_conv_kernel(ff_ref, fi_ref, filt_ref, z_ref, x0_ref, inv_ref, bias_ref, o_ref, *, l):
    ff = ff_ref[...]
    hs = _dot(ff, filt_ref[...].astype(BF16))
    hre, him = hs[:2 * l], hs[2 * l:]
    z = z_ref[0]
    xs = _dot(ff[:, :l], z.astype(BF16))
    xre, xim = xs[:2 * l], xs[2 * l:]
    yre = (xre * hre - xim * him).astype(BF16)
    yim = (xre * him + xim * hre).astype(BF16)
    y = _dot(fi_ref[...], jnp.concatenate([yre, yim], 0)) * (0.5 / l)
    o_ref[0] = (x0_ref[0] * (y * inv_ref[...] + bias_ref[...] * z)).astype(o_ref.dtype)


def small_long_conv(z, x0, filt2, norm, bias):
    b, l, c = z.shape
    k = np.arange(2 * l)
    ang = 2.0 * np.pi * np.outer(k, k) / (2 * l)
    ff = jnp.asarray(np.concatenate([np.cos(ang), -np.sin(ang)], 0), BF16)
    fi = jnp.asarray(np.concatenate([np.cos(ang), -np.sin(ang)], 1)[:l], BF16)
    full = lambda a: pl.BlockSpec(a.shape, lambda bi: (0,) * a.ndim)
    row = pl.BlockSpec((1, l, c), lambda bi: (bi, 0, 0))
    inv = 1.0 / norm
    return pl.pallas_call(
        functools.partial(_small_conv_kernel, l=l),
        grid=(b,),
        in_specs=[full(ff), full(fi), full(filt2), row, row, full(inv), full(bias)],
        out_specs=row,
        out_shape=jax.ShapeDtypeStruct((b, l, c), BF16),
        compiler_params=_cp("parallel"),
        name="small_long_conv",
    )(ff, fi, filt2, z, x0, inv, bias)


def _pool_kernel(p_ref, pp_ref, pn_ref, w_ref, sc_ref, o_ref, buf_ref, *, tm, l):
    _fill_padded(buf_ref, p_ref, pp_ref, pn_ref, tm)
    t = pl.program_id(1) * tm + lax.broadcasted_iota(jnp.int32, (tm, LANES), 0)
    lane = lax.broadcasted_iota(jnp.int32, (tm, LANES), 1)
    first_group = lane < POOL_GROUP
    h0 = POOL_HALO

    def win(col, lo, hi):
        acc = buf_ref[h0 + lo:h0 + lo + tm, col]
        for j in range(lo + 1, hi):
            acc = acc + buf_ref[h0 + j:h0 + j + tm, col]
        return acc

    parts = []
    for half, (wa, wb) in enumerate(((2, 4), (8, 16))):
        col = slice(half * LANES, (half + 1) * LANES)
        sa = win(col, -wa // 2, wa // 2)
        sb = sa + win(col, -wb // 2, -wa // 2) + win(col, wa // 2, wb // 2)
        hw = jnp.where(first_group, wa // 2, wb // 2)
        cnt = jnp.minimum(t + hw, l) - jnp.maximum(t - hw, 0)
        mean = jnp.where(first_group, sa, sb) / cnt.astype(F32)
        parts.append(mean - buf_ref[h0:h0 + tm, col])
    d = jnp.concatenate(parts, axis=-1).astype(BF16)
    o_ref[0] = (_dot(d, w_ref[...]) * sc_ref[...]).astype(o_ref.dtype)


def pool_mix(p, pool_w, pool_scale):
    b, l, wd = p.shape
    tm = min(ROW_TILE, l)
    nblk = l // tm
    wblk = jax.scipy.linalg.block_diag(*[pool_w[g] for g in range(len(POOL_WINDOWS))]).astype(BF16)
    return pl.pallas_call(
        functools.partial(_pool_kernel, tm=tm, l=l),
        grid=(b, nblk),
        in_specs=_halo_specs(tm, wd, nblk) + [pl.BlockSpec((wd, wd), lambda bi, i: (0, 0)),
                                              pl.BlockSpec((1, wd), lambda bi, i: (0, 0))],
        out_specs=pl.BlockSpec((1, tm, wd), lambda bi, i: (bi, i, 0)),
        out_shape=jax.ShapeDtypeStruct((b, l, wd), BF16),
        scratch_shapes=[pltpu.VMEM((tm + 2 * POOL_HALO, wd), F32)],
        compiler_params=_cp("parallel", "parallel"),
        name="pool_mix",
    )(p, p, p, wblk, pool_scale.reshape(1, wd))


def _outproj_kernel(x_ref, hy_ref, pool_ref, att_ref, w_ref, gate_ref, o_ref):
    a, b2 = HY_WIDTH, HY_WIDTH + POOL_WIDTH
    y = (_dot(hy_ref[0].astype(BF16), w_ref[0:a]) + _dot(pool_ref[0], w_ref[a:b2]) + _dot(att_ref[0], w_ref[b2:]))
    o_ref[0] = x_ref[0] + gate_ref[0] * y


def out_projection(x, y_hy, y_pool, y_att, w_out, gate):
    b, l, d = x.shape
    tm = min(ROW_TILE, l)
    row = lambda bi, i: (bi, i, 0)
    return pl.pallas_call(
        _outproj_kernel,
        grid=(b, l // tm),
        in_specs=[pl.BlockSpec((1, tm, d), row),
                  pl.BlockSpec((1, tm, y_hy.shape[2]), row),
                  pl.BlockSpec((1, tm, y_pool.shape[2]), row),
                  pl.BlockSpec((1, tm, y_att.shape[2]), row),
                  pl.BlockSpec(w_out.shape, lambda bi, i: (0, 0)),
                  pl.BlockSpec((1, 1, d), lambda bi, i: (bi, 0, 0))],
        out_specs=pl.BlockSpec((1, tm, d), row),
        out_shape=jax.ShapeDtypeStruct(x.shape, F32),
        compiler_params=_cp("parallel", "parallel"),
        name="out_projection",
    )(x, y_hy, y_pool, y_att, w_out, gate)


FF_CHUNK = 256
MOE_TILE = 1024
MOE_ROWS = (128, 192, 256, 320, 384)


def _swiglu_acc(h, wg_ref, wu_ref, wd_ref, lead):
    ff = wg_ref.shape[-1]
    acc = None
    for f0 in range(0, ff, FF_CHUNK):
        cs = slice(f0, min(f0 + FF_CHUNK, ff))
        g = _dot(h, wg_ref[lead + (slice(None), cs)])
        u = _dot(h, wu_ref[lead + (slice(None), cs)])
        a = (g * jax.nn.sigmoid(g) * u).astype(BF16)
        part = _dot(a, wd_ref[lead + (cs, slice(None))])
        acc = part if acc is None else acc + part
    return acc


def _ffn_kernel(x_ref, g_ref, sh_ref, sc_ref, gate_ref, wg_ref, wu_ref, wd_ref, o_ref):
    x = x_ref[0]
    h = _normmod(x, g_ref[...], sh_ref[0], sc_ref[0]).astype(BF16)
    o_ref[0] = x + gate_ref[0] * _swiglu_acc(h, wg_ref, wu_ref, wd_ref, ())


def ffn_dense(x, g, shift, scale, gate, wg, wu, wd):
    b, l, d = x.shape
    tm = min(ROW_TILE, l)
    row = lambda bi, i: (bi, i, 0)
    vec = lambda bi, i: (bi, 0, 0)
    wspec = lambda w: pl.BlockSpec(w.shape, lambda bi, i: (0, 0), pipeline_mode=pl.Buffered(1))
    return pl.pallas_call(
        _ffn_kernel,
        grid=(b, l // tm),
        in_specs=[pl.BlockSpec((1, tm, d), row),
                  pl.BlockSpec((1, d), lambda bi, i: (0, 0)),
                  pl.BlockSpec((1, 1, d), vec), pl.BlockSpec((1, 1, d), vec), pl.BlockSpec((1, 1, d), vec),
                  wspec(wg), wspec(wu), wspec(wd)],
        out_specs=pl.BlockSpec((1, tm, d), row),
        out_shape=jax.ShapeDtypeStruct(x.shape, F32),
        compiler_params=_cp("parallel", "parallel"),
        name="ffn_dense",
    )(x, g, shift, scale, gate, wg, wu, wd)


def _router_kernel(x_ref, g_ref, sh_ref, sc_ref, wr_ref, tri_ref, comb_ref, rank_ref):
    h = _normmod(x_ref[0], g_ref[...], sh_ref[0], sc_ref[0])
    logits = _dot(h, wr_ref[...], precision=HIGHEST)
    lane = lax.broadcasted_iota(jnp.int32, logits.shape, 1)
    neg = jnp.float32(-jnp.inf)
    lg = jnp.where(lane < N_EXPERTS, logits, neg)
    m1 = jnp.max(lg, axis=-1, keepdims=True)
    i1 = jnp.min(jnp.where(lg == m1, lane, LANES), axis=-1, keepdims=True)
    lg2 = jnp.where(lane == i1, neg, lg)
    m2 = jnp.max(lg2, axis=-1, keepdims=True)
    i2 = jnp.min(jnp.where(lg2 == m2, lane, LANES), axis=-1, keepdims=True)
    e = jnp.exp(m2 - m1)
    g1 = 1.0 / (1.0 + e)
    comb_ref[0] = jnp.where(lane == i1, g1, jnp.where(lane == i2, e * g1, 0.0))
    routed = (lane == i1) | (lane == i2)
    before = _dot(tri_ref[...], jnp.where(routed, 1.0, 0.0).astype(BF16))
    rank_ref[0] = jnp.where(routed, before, -1.0)


def moe_route(x, g, shift, scale, w_router):
    b, l, d = x.shape
    tm = min(MOE_TILE, l)
    wr = jnp.concatenate([w_router, jnp.zeros((d, LANES - N_EXPERTS), F32)], -1)
    tri = jnp.asarray(np.tril(np.ones((tm, tm), np.float32), -1), BF16)
    row = lambda bi, i: (bi, i, 0)
    vec = lambda bi, i: (bi, 0, 0)
    return pl.pallas_call(
        _router_kernel,
        grid=(b, l // tm),
        in_specs=[pl.BlockSpec((1, tm, d), row),
                  pl.BlockSpec((1, d), lambda bi, i: (0, 0)),
                  pl.BlockSpec((1, 1, d), vec), pl.BlockSpec((1, 1, d), vec),
                  pl.BlockSpec((d, LANES), lambda bi, i: (0, 0)),
                  pl.BlockSpec((tm, tm), lambda bi, i: (0, 0))],
        out_specs=[pl.BlockSpec((1, tm, LANES), row)] * 2,
        out_shape=[jax.ShapeDtypeStruct((b, l, LANES), F32)] * 2,
        compiler_params=_cp("parallel", "parallel"),
        name="moe_router",
    )(x, g, shift, scale, wr, tri)


def _moe_kernel(x_ref, g_ref, sh_ref, sc_ref, gate_ref, comb_ref, wg_ref, wu_ref, wd_ref, o_ref, acc_ref):
    e = pl.program_id(2)
    x = x_ref[0]
    h = _normmod(x, g_ref[...], sh_ref[0], sc_ref[0]).astype(BF16)
    comb = comb_ref[0]
    lane = lax.broadcasted_iota(jnp.int32, comb.shape, 1)
    ce = jnp.sum(jnp.where(lane == e, comb, 0.0), axis=-1, keepdims=True)
    y = ce * _swiglu_acc(h, wg_ref, wu_ref, wd_ref, (0, 0))

    @pl.when(e == 0)
    def _():
        acc_ref[...] = y

    @pl.when(e > 0)
    def _():
        acc_ref[...] += y

    @pl.when(e == N_EXPERTS - 1)
    def _():
        o_ref[0] = x + gate_ref[0] * acc_ref[...]


def moe_dense(x, g, shift, scale, gate, comb, wg, wu, wd, li):
    b, l, d = x.shape
    tm = min(ROW_TILE, l)
    ff = wg.shape[-1]
    row = lambda bi, i, e: (bi, i, 0)
    vec = lambda bi, i, e: (bi, 0, 0)
    return pl.pallas_call(
        _moe_kernel,
        grid=(b, l // tm, N_EXPERTS),
        in_specs=[pl.BlockSpec((1, tm, d), row),
                  pl.BlockSpec((1, d), lambda bi, i, e: (0, 0)),
                  pl.BlockSpec((1, 1, d), vec), pl.BlockSpec((1, 1, d), vec), pl.BlockSpec((1, 1, d), vec),
                  pl.BlockSpec((1, tm, LANES), row),
                  pl.BlockSpec((1, 1, d, ff), lambda bi, i, e: (li, e, 0, 0)),
                  pl.BlockSpec((1, 1, d, ff), lambda bi, i, e: (li, e, 0, 0)),
                  pl.BlockSpec((1, 1, ff, d), lambda bi, i, e: (li, e, 0, 0))],
        out_specs=pl.BlockSpec((1, tm, d), row),
        out_shape=jax.ShapeDtypeStruct(x.shape, F32),
        scratch_shapes=[pltpu.VMEM((tm, d), F32)],
        compiler_params=_cp("parallel", "parallel", "arbitrary"),
        name="moe_dense",
    )(x, g, shift, scale, gate, comb, wg, wu, wd)


def _moe_sparse_kernel(cnt_ref, x_ref, g_ref, sh_ref, sc_ref, gate_ref, comb_ref, rank_ref, rankt_ref,
                       wg_ref, wu_ref, wd_ref, *rest, t, sizes):
    fg_ref = rest[0] if len(rest) == 3 else None
    o_ref, h_scr = rest[-2:]
    i, e = pl.program_id(0), pl.program_id(1)
    cnt = cnt_ref[i * N_EXPERTS + e]

    @pl.when(e == 0)
    def _():
        x = x_ref[0]
        h_scr[...] = _normmod(x, g_ref[...], sh_ref[0], sc_ref[0]).astype(BF16)
        o_ref[0] = x

    rrow = rankt_ref[0, pl.ds(e, 1), :]
    lane = lax.broadcasted_iota(jnp.int32, (t, LANES), 1)
    rcol = jnp.sum(jnp.where(lane == e, rank_ref[0], 0.0), axis=-1, keepdims=True)
    ccol = jnp.sum(jnp.where(lane == e, comb_ref[0], 0.0), axis=-1, keepdims=True)

    def block(base, r):
        rid = (base + lax.broadcasted_iota(jnp.int32, (r, 1), 0)).astype(F32)
        onehot = jnp.where(rrow == rid, 1.0, 0.0).astype(BF16)
        xg = _dot(onehot, h_scr[...]).astype(BF16)
        y = _swiglu_acc(xg, wg_ref, wu_ref, wd_ref, (0, 0)).astype(BF16)
        cid = (base + lax.broadcasted_iota(jnp.int32, (1, r), 1)).astype(F32)
        onehot_t = jnp.where(rcol == cid, 1.0, 0.0).astype(BF16)
        o_ref[0] += (gate_ref[0] * ccol) * _dot(onehot_t, y)

    big = sizes[-1]
    nfull = cnt // big

    def full(k, carry):
        block(k * big, big)
        return carry
    lax.fori_loop(0, nfull, full, 0)
    base = nfull * big
    rem = cnt - base
    for lo, r in zip((0,) + sizes[:-1], sizes):
        @pl.when((rem > lo) & (rem <= r))
        def _(r=r):
            block(base, r)

    if fg_ref is not None:
        @pl.when(e == N_EXPERTS - 1)
        def _():
            o = o_ref[0]
            o_ref[0] = o * lax.rsqrt(jnp.mean(o * o, axis=-1, keepdims=True) + EPS) * fg_ref[...]


def moe_sparse(x, g, shift, scale, gate, comb, rank, wg, wu, wd, li, final_g=None):
    b, l, d = x.shape
    t = MOE_TILE
    tpb = l // t
    n_tiles = b * tpb
    ff = wg.shape[-1]
    rk = rank[..., :N_EXPERTS].reshape(n_tiles, t, N_EXPERTS)
    rank_t = rk.transpose(0, 2, 1)
    cnt = jnp.sum(rk >= 0, axis=1).astype(jnp.int32).reshape(n_tiles * N_EXPERTS)
    row = lambda i, e, c: (i // tpb, i % tpb, 0)
    vec = lambda i, e, c: (i // tpb, 0, 0)
    grid_spec = pltpu.PrefetchScalarGridSpec(
        num_scalar_prefetch=1,
        grid=(n_tiles, N_EXPERTS),
        in_specs=[pl.BlockSpec((1, t, d), row, pipeline_mode=pl.Buffered(1)),
                  pl.BlockSpec((1, d), lambda i, e, c: (0, 0)),
                  pl.BlockSpec((1, 1, d), vec), pl.BlockSpec((1, 1, d), vec), pl.BlockSpec((1, 1, d), vec),
                  pl.BlockSpec((1, t, LANES), row, pipeline_mode=pl.Buffered(1)),
                  pl.BlockSpec((1, t, LANES), row, pipeline_mode=pl.Buffered(1)),
                  pl.BlockSpec((1, N_EXPERTS, t), lambda i, e, c: (i, 0, 0)),
                  pl.BlockSpec((1, 1, d, ff), lambda i, e, c: (li, e, 0, 0)),
                  pl.BlockSpec((1, 1, d, ff), lambda i, e, c: (li, e, 0, 0)),
                  pl.BlockSpec((1, 1, ff, d), lambda i, e, c: (li, e, 0, 0))]
        + ([] if final_g is None else [pl.BlockSpec((1, d), lambda i, e, c: (0, 0))]),
        out_specs=pl.BlockSpec((1, t, d), row),
        scratch_shapes=[pltpu.VMEM((t, d), BF16)],
    )
    return pl.pallas_call(
        functools.partial(_moe_sparse_kernel, t=t, sizes=MOE_ROWS),
        grid_spec=grid_spec,
        out_shape=jax.ShapeDtypeStruct(x.shape, F32),
        compiler_params=_cp("parallel", "arbitrary"),
        name="moe_sparse",
    )(cnt, x, g, shift, scale, gate, comb, rank, rank_t, wg, wu, wd, *(() if final_g is None else (final_g,)))


def _final_kernel(x_ref, g_ref, o_ref):
    x = x_ref[0]
    o_ref[0] = x * lax.rsqrt(jnp.mean(x * x, axis=-1, keepdims=True) + EPS) * g_ref[...]


def final_norm(x, g):
    b, l, d = x.shape
    tm = min(2 * ROW_TILE, l)
    return pl.pallas_call(
        _final_kernel,
        grid=(b, l // tm),
        in_specs=[pl.BlockSpec((1, tm, d), lambda bi, i: (bi, i, 0)),
                  pl.BlockSpec((1, d), lambda bi, i: (0, 0))],
        out_specs=pl.BlockSpec((1, tm, d), lambda bi, i: (bi, i, 0)),
        out_shape=jax.ShapeDtypeStruct(x.shape, F32),
        compiler_params=_cp("parallel", "parallel"),
        name="final_norm",
    )(x, g)


def kernel(x, c, ctx, c_ctx, norm1_g, norm2_g, w_ada, b_ada, w_in, hy_short_w, hy_short_b, filt_w1, filt_b1, filt_freq1, filt_w2, filt_b2, filt_freq2, filt_w3, hy_bias, pool_w, pool_scale, q_norm_g, w_uq, kv_norm_g, w_ukv, w_out, ffn_wg, ffn_wu, ffn_wd, moe_router, moe_wg, moe_wu, moe_wd, final_g):
    b, l, d = x.shape
    lc = ctx.shape[1]
    depth = w_ada.shape[0]

    cc = jnp.zeros((8, d), F32).at[:b].set(c).at[b].set(c_ctx)
    mods = ada_vectors(cc, w_ada, b_ada)

    rope_l = rope_tables(l, True)
    rope_c = rope_tables(lc, False)
    ftab_l, ftab_c = filter_tables(l), filter_tables(lc)
    dtabs = dft_tables(l)
    moe_w = (moe_wg.astype(BF16), moe_wu.astype(BF16), moe_wd.astype(BF16))

    xl, xc = x, ctx
    for layer in range(depth):
        last = layer == depth - 1
        mod = mods[layer].reshape(8, 6, 1, d)
        mod_l = [mod[:b, j] for j in range(6)]
        mod_c = [jnp.broadcast_to(mod[b:b + 1, j], (b, 1, d)) for j in range(6)]
        g1 = norm1_g[layer].reshape(1, d)
        g2 = norm2_g[layer].reshape(1, d)
        w_in_l = w_in[layer].astype(BF16)
        w_out_l = w_out[layer].astype(BF16)
        mw = mla_weights(q_norm_g[layer], w_uq[layer], kv_norm_g[layer], w_ukv[layer])
        fw = (filt_w1[layer], filt_b1[layer], filt_freq1[layer], filt_w2[layer], filt_b2[layer],
              filt_freq2[layer], filt_w3[layer])
        bias = hy_bias[layer].reshape(1, HY_WIDTH)

        def channel_mix(xs, md, final=False):
            i = layer // 2
            fg = final_g.reshape(1, d)
            if layer % 2 == 1 and xs.shape[1] % MOE_TILE == 0:
                comb, rank = moe_route(xs, g2, md[3], md[4], moe_router[i])
                return moe_sparse(xs, g2, md[3], md[4], md[5], comb, rank, *moe_w, i, fg if final else None)
            if layer % 2 == 0:
                out = ffn_dense(xs, g2, md[3], md[4], md[5], ffn_wg[i].astype(BF16),
                                ffn_wu[i].astype(BF16), ffn_wd[i].astype(BF16))
            else:
                comb, _ = moe_route(xs, g2, md[3], md[4], moe_router[i])
                out = moe_dense(xs, g2, md[3], md[4], md[5], comb, *moe_w, i)
            return final_norm(out, fg) if final else out

        hy_l, pool_l, q_l, kt_l, v_l = in_projection(xl, g1, mod_l[0], mod_l[1], w_in_l, *rope_l, mw)
        hy_c, pool_c, q_c, kt_c, v_c = in_projection(xc, g1, mod_c[0], mod_c[1], w_in_l, *rope_c, mw)

        att_l = attention(q_l, kt_c, v_c, kt_l, v_l)
        z_l, x0_l = hyena_pre(hy_l, hy_short_w[layer], hy_short_b[layer])
        filt_l, nrm_l = hyena_filter(l, ftab_l, *fw)
        yhy_l = hyena_long_conv(z_l, x0_l, filt_l, nrm_l, bias, dtabs)
        ypool_l = pool_mix(pool_l, pool_w[layer], pool_scale[layer])
        xl = out_projection(xl, yhy_l, ypool_l, att_l, w_out_l, mod_l[2])

        if not last:
            att_c = attention(q_c, kt_c, v_c)
            z_c, x0_c = hyena_pre(hy_c, hy_short_w[layer], hy_short_b[layer])
            filt_c, nrm_c = hyena_filter(lc, ftab_c, *fw)
            yhy_c = small_long_conv(z_c, x0_c, filt_c, nrm_c, bias)
            ypool_c = pool_mix(pool_c, pool_w[layer], pool_scale[layer])
            xc = out_projection(xc, yhy_c, ypool_c, att_c, w_out_l, mod_c[2])
            xc = channel_mix(xc, mod_c)

        xl = channel_mix(xl, mod_l, final=last)
    return xl
```

```python
import functools
import math

import jax
import jax.numpy as jnp
import numpy as np
from jax import lax
from jax.experimental import pallas as pl
from jax.experimental.pallas import tpu as pltpu

F32 = jnp.float32
BF16 = jnp.bfloat16
HIGHEST = lax.Precision.HIGHEST

EPS = 1e-6
GRID_W = 64
HY_WIDTH = 256
POOL_WINDOWS = (2, 4, 8, 16)
POOL_WIDTH = 256
POOL_GROUP = 64
POOL_HALO = 8
MLA_HEADS = 8
QK_NOPE = 64
QK_ROPE = 32
V_DIM = 64
Q_LORA = 384
KV_LORA = 256
HEAD_PAD = 128
SM_SCALE = (QK_NOPE + QK_ROPE) ** -0.5
ROPE_THETA = 10000.0
FILT_EMB = 33
FILT_BANDS = 16
DECAY_TARGET = 1e-2
FAST_DECAY_PCT = 0.3
SLOW_DECAY_PCT = 1.5
N_EXPERTS = 8
LANES = 128
FFT_N2 = 128
VMEM_LIMIT = 56 * 2 ** 20
ROW_TILE = 512
DFT_COLS = 2048
ADA_COLS = 1536


def _cp(*sem):
    return pltpu.CompilerParams(dimension_semantics=sem, vmem_limit_bytes=VMEM_LIMIT)


def _dot(a, b, **kw):
    return jnp.dot(a, b, preferred_element_type=F32, **kw)


def _normmod(x, g, shift, scale):
    ms = jnp.mean(x * x, axis=-1, keepdims=True)
    return x * lax.rsqrt(ms + EPS) * (g * (1.0 + scale)) + shift


def _ada_kernel(c_ref, w_ref, b_ref, o_ref):
    c = c_ref[...]
    s = c * jax.nn.sigmoid(c)
    o_ref[0] = _dot(s, w_ref[0], precision=HIGHEST) + b_ref[0]


def ada_vectors(cc, w_ada, b_ada):
    depth, d, n = w_ada.shape
    tn = ADA_COLS
    return pl.pallas_call(
        _ada_kernel,
        grid=(depth, n // tn),
        in_specs=[pl.BlockSpec((8, d), lambda l, j: (0, 0)),
                  pl.BlockSpec((1, d, tn), lambda l, j: (l, 0, j)),
                  pl.BlockSpec((1, 1, tn), lambda l, j: (l, 0, j))],
        out_specs=pl.BlockSpec((1, 8, tn), lambda l, j: (l, 0, j)),
        out_shape=jax.ShapeDtypeStruct((depth, 8, n), F32),
        compiler_params=_cp("parallel", "parallel"),
        name="ada_vectors",
    )(cc, w_ada, b_ada.reshape(depth, 1, n))


IN_HY, IN_POOL = 3 * HY_WIDTH, POOL_WIDTH
IN_OFFS = tuple(int(v) for v in np.cumsum((0, IN_HY, IN_POOL, Q_LORA, KV_LORA, QK_ROPE)))


def _inproj_kernel(x_ref, g_ref, sh_ref, sc_ref, w_ref, c_ref, s_ref, wqa_ref, wqb_ref, wk_ref, wv_ref,
                   pa_ref, pb_ref, hy_ref, pool_ref, q_ref, kt_ref, v_ref):
    h = _normmod(x_ref[0], g_ref[...], sh_ref[0], sc_ref[0]).astype(BF16)
    y = _dot(h, w_ref[...])
    o = IN_OFFS
    hy_ref[0] = y[:, o[0]:o[1]]
    pool_ref[0] = y[:, o[1]:o[2]]
    cq, ckv = y[:, o[2]:o[3]], y[:, o[3]:o[4]]
    kpe = y[:, o[4]:o[5]].astype(BF16)
    cqn = (cq * lax.rsqrt(jnp.mean(cq * cq, axis=-1, keepdims=True) + EPS)).astype(BF16)
    ckvn = (ckv * lax.rsqrt(jnp.mean(ckv * ckv, axis=-1, keepdims=True) + EPS)).astype(BF16)
    cs, sn = c_ref[...], s_ref[...]
    qa = _dot(cqn, wqa_ref[...])
    qb = _dot(cqn, wqb_ref[...])
    kn = _dot(ckvn, wk_ref[...])
    vv = _dot(ckvn, wv_ref[...])
    kpa = _dot(kpe, pa_ref[...])
    kpb = _dot(kpe, pb_ref[...])
    kpe_rot = kpa * cs + kpb * sn
    lane = lax.broadcasted_iota(jnp.int32, (1, HEAD_PAD), 1)
    for hd in range(MLA_HEADS):
        sl = slice(hd * HEAD_PAD, (hd + 1) * HEAD_PAD)
        q_ref[0, hd] = (qa[:, sl] * cs + qb[:, sl] * sn).astype(BF16)
        kt_ref[0, hd] = (kn[:, sl] + kpe_rot).T.astype(BF16)
        one_lane = V_DIM if hd % 2 == 0 else 0
        v_ref[0, hd] = jnp.where(lane == one_lane, 1.0, vv[:, sl]).astype(BF16)


def in_projection(x, g, shift, scale, w_in, ctab, stab, wts):
    b, l, d = x.shape
    tm = min(ROW_TILE, l)
    hw = MLA_HEADS
    row = lambda bi, i: (bi, i, 0)
    vec = lambda bi, i: (bi, 0, 0)
    full = lambda a: pl.BlockSpec(a.shape, lambda bi, i: (0,) * a.ndim)
    return pl.pallas_call(
        _inproj_kernel,
        grid=(b, l // tm),
        in_specs=[pl.BlockSpec((1, tm, d), row),
                  pl.BlockSpec((1, d), lambda bi, i: (0, 0)),
                  pl.BlockSpec((1, 1, d), vec),
                  pl.BlockSpec((1, 1, d), vec),
                  full(w_in),
                  pl.BlockSpec((tm, HEAD_PAD), lambda bi, i: (i, 0)),
                  pl.BlockSpec((tm, HEAD_PAD), lambda bi, i: (i, 0))] + [full(w) for w in wts],
        out_specs=[pl.BlockSpec((1, tm, IN_HY), row),
                   pl.BlockSpec((1, tm, IN_POOL), row),
                   pl.BlockSpec((1, hw, tm, HEAD_PAD), lambda bi, i: (bi, 0, i, 0)),
                   pl.BlockSpec((1, hw, HEAD_PAD, tm), lambda bi, i: (bi, 0, 0, i)),
                   pl.BlockSpec((1, hw, tm, HEAD_PAD), lambda bi, i: (bi, 0, i, 0))],
        out_shape=[jax.ShapeDtypeStruct((b, l, IN_HY), F32),
                   jax.ShapeDtypeStruct((b, l, IN_POOL), F32),
                   jax.ShapeDtypeStruct((b, hw, l, HEAD_PAD), BF16),
                   jax.ShapeDtypeStruct((b, hw, HEAD_PAD, l), BF16),
                   jax.ShapeDtypeStruct((b, hw, l, HEAD_PAD), BF16)],
        compiler_params=_cp("parallel", "parallel"),
        name="in_projection",
    )(x, g, shift, scale, w_in, ctab, stab, *wts)


def mla_weights(q_norm_g, w_uq, kv_norm_g, w_ukv):
    hw, dq = MLA_HEADS, QK_NOPE + QK_ROPE
    half = QK_ROPE // 2
    wq = (w_uq * q_norm_g[:, None] * (SM_SCALE * math.log2(math.e))).reshape(Q_LORA, hw, dq)
    pad = jnp.zeros((Q_LORA, hw, HEAD_PAD - dq), F32)
    wqa = jnp.concatenate([wq, pad], -1)
    swap = jnp.concatenate([wq[..., QK_NOPE + half:], wq[..., QK_NOPE:QK_NOPE + half]], -1)
    wqb = jnp.concatenate([jnp.zeros((Q_LORA, hw, QK_NOPE), F32), swap, pad], -1)
    wkv = (w_ukv * kv_norm_g[:, None]).reshape(KV_LORA, hw, QK_NOPE + V_DIM)
    z64 = jnp.zeros((KV_LORA, hw, 64), F32)
    wk = jnp.concatenate([wkv[..., :QK_NOPE], z64], -1)
    v = wkv[..., QK_NOPE:]
    wv = jnp.where((jnp.arange(hw) % 2 == 0)[None, :, None],
                   jnp.concatenate([v, z64], -1), jnp.concatenate([z64, v], -1))
    eye = np.eye(QK_ROPE, dtype=np.float32)
    pa = np.zeros((QK_ROPE, HEAD_PAD), np.float32)
    pa[:, QK_NOPE:QK_NOPE + QK_ROPE] = eye
    pb = np.zeros((QK_ROPE, HEAD_PAD), np.float32)
    pb[:, QK_NOPE:QK_NOPE + QK_ROPE] = np.concatenate([eye[:, half:], eye[:, :half]], 1)
    flat = lambda w: w.reshape(w.shape[0], hw * HEAD_PAD).astype(BF16)
    return (flat(wqa), flat(wqb), flat(wk), flat(wv), jnp.asarray(pa, BF16), jnp.asarray(pb, BF16))


def rope_tables(l, with_rope):
    ctab = np.zeros((l, HEAD_PAD), np.float32)
    stab = np.zeros((l, HEAD_PAD), np.float32)
    ctab[:, :QK_NOPE + QK_ROPE] = 1.0
    if not with_rope:
        return jnp.asarray(ctab), jnp.asarray(stab)
    n_freq = QK_ROPE // 4
    inv_freq = 1.0 / (ROPE_THETA ** (jnp.arange(n_freq, dtype=F32) / n_freq))
    rows = l // GRID_W
    row = jnp.repeat(jnp.arange(rows, dtype=F32), GRID_W)
    col = jnp.tile(jnp.arange(GRID_W, dtype=F32), rows)
    ang = jnp.concatenate([row[:, None] * inv_freq, col[:, None] * inv_freq], axis=-1)
    cos, sin = jnp.cos(ang), jnp.sin(ang)
    one = jnp.ones((l, QK_NOPE), F32)
    zero = jnp.zeros((l, QK_NOPE), F32)
    tail = jnp.zeros((l, HEAD_PAD - QK_NOPE - QK_ROPE), F32)
    return (jnp.concatenate([one, cos, cos, tail], -1), jnp.concatenate([zero, -sin, sin, tail], -1))


ATT_TQ = 1024
ATT_RB = 1024
ATT_CK = 2048


def _row_max(s):
    mp = s[:, 0:LANES]
    for j in range(1, s.shape[1] // LANES):
        mp = jnp.maximum(mp, s[:, j * LANES:(j + 1) * LANES])
    return jnp.max(mp, axis=-1, keepdims=True)


def _attn_kernel(q_ref, kct_ref, vc_ref, *rest, n_chunks, ck, tq):
    if n_chunks:
        kt_ref, v_ref, o_ref = rest
    else:
        (o_ref,) = rest
    rb = min(ATT_RB, tq)
    chains = [(hh, r) for hh in range(2) for r in range(tq // rb)]
    lane = lax.broadcasted_iota(jnp.int32, (1, HEAD_PAD), 1)

    def q_rows(hh, r):
        return q_ref[0, hh, r * rb:(r + 1) * rb, :]

    state = []
    for hh, r in chains:
        s = _dot(q_rows(hh, r), kct_ref[0, hh])
        m = _row_max(s)
        p = jnp.exp2(s - m).astype(BF16)
        state += [m, _dot(p, vc_ref[0, hh])]

    if n_chunks:
        def body(c, state):
            off = pl.multiple_of(c * ck, ck)
            out = []
            for i, (hh, r) in enumerate(chains):
                m, acc = state[2 * i], state[2 * i + 1]
                s = _dot(q_rows(hh, r), kt_ref[0, hh, :, pl.ds(off, ck)])
                m_new = jnp.maximum(m, _row_max(s))
                alpha = jnp.exp2(m - m_new)
                p = jnp.exp2(s - m_new).astype(BF16)
                out += [m_new, acc * alpha + _dot(p, v_ref[0, hh, pl.ds(off, ck), :])]
            return tuple(out)
        state = lax.fori_loop(0, n_chunks, body, tuple(state), unroll=2 if n_chunks % 2 == 0 else 1)

    for r in range(tq // rb):
        outs = []
        for hh in range(2):
            acc = state[2 * chains.index((hh, r)) + 1]
            one_lane = V_DIM if hh == 0 else 0
            denom = jnp.sum(jnp.where(lane == one_lane, acc, 0.0), axis=-1, keepdims=True)
            outs.append(acc / denom)
        o_ref[0, r * rb:(r + 1) * rb, :] = jnp.where(lane < V_DIM, outs[0], outs[1]).astype(o_ref.dtype)


def attention(q, kct, vc, kt=None, v=None):
    b, hw, lq, _ = q.shape
    lc = vc.shape[2]
    tq = min(ATT_TQ, lq)
    ck = ATT_CK if kt is None else min(ATT_CK, kt.shape[3])
    n_chunks = 0 if kt is None else kt.shape[3] // ck
    in_specs = [pl.BlockSpec((1, 2, tq, HEAD_PAD), lambda bi, hp, i: (bi, hp, i, 0)),
                pl.BlockSpec((1, 2, HEAD_PAD, lc), lambda bi, hp, i: (bi, hp, 0, 0)),
                pl.BlockSpec((1, 2, lc, HEAD_PAD), lambda bi, hp, i: (bi, hp, 0, 0))]
    args = [q, kct, vc]
    if n_chunks:
        l = kt.shape[3]
        in_specs += [pl.BlockSpec((1, 2, HEAD_PAD, l), lambda bi, hp, i: (bi, hp, 0, 0)),
                     pl.BlockSpec((1, 2, l, HEAD_PAD), lambda bi, hp, i: (bi, hp, 0, 0))]
        args += [kt, v]
    return pl.pallas_call(
        functools.partial(_attn_kernel, n_chunks=n_chunks, ck=ck, tq=tq),
        grid=(b, hw // 2, lq // tq),
        in_specs=in_specs,
        out_specs=pl.BlockSpec((1, tq, HEAD_PAD), lambda bi, hp, i: (bi, i, hp)),
        out_shape=jax.ShapeDtypeStruct((b, lq, hw * V_DIM), BF16),
        compiler_params=_cp("parallel", "parallel", "arbitrary"),
        name="attention",
    )(*args)


def _halo_specs(tm, width, nblk):
    r = tm // POOL_HALO
    last = nblk * r - 1
    return [pl.BlockSpec((1, tm, width), lambda bi, i: (bi, i, 0)),
            pl.BlockSpec((1, POOL_HALO, width), lambda bi, i: (bi, jnp.maximum(i * r - 1, 0), 0)),
            pl.BlockSpec((1, POOL_HALO, width), lambda bi, i: (bi, jnp.minimum((i + 1) * r, last), 0))]


def _fill_padded(buf_ref, cur_ref, prev_ref, next_ref, tm):
    i = pl.program_id(1)
    first = i == 0
    last = i == pl.num_programs(1) - 1
    buf_ref[0:POOL_HALO] = jnp.where(first, 0.0, prev_ref[0])
    buf_ref[POOL_HALO:POOL_HALO + tm] = cur_ref[0]
    buf_ref[POOL_HALO + tm:] = jnp.where(last, 0.0, next_ref[0])


def _hyena_pre_kernel(u_ref, up_ref, un_ref, w_ref, b_ref, z_ref, x0_ref, buf_ref, *, tm):
    _fill_padded(buf_ref, u_ref, up_ref, un_ref, tm)
    w = w_ref[...]
    uc = (buf_ref[POOL_HALO - 1:POOL_HALO - 1 + tm] * w[0:1] + buf_ref[POOL_HALO:POOL_HALO + tm] * w[1:2]
          + buf_ref[POOL_HALO + 1:POOL_HALO + 1 + tm] * w[2:3] + b_ref[...])
    x0_ref[0] = uc[:, :HY_WIDTH]
    z_ref[0] = uc[:, HY_WIDTH:2 * HY_WIDTH] * uc[:, 2 * HY_WIDTH:]


def hyena_pre(u, sw, sb):
    b, l, wd = u.shape
    tm = min(ROW_TILE, l)
    nblk = l // tm
    return pl.pallas_call(
        functools.partial(_hyena_pre_kernel, tm=tm),
        grid=(b, nblk),
        in_specs=_halo_specs(tm, wd, nblk) + [pl.BlockSpec((3, wd), lambda bi, i: (0, 0)),
                                              pl.BlockSpec((1, wd), lambda bi, i: (0, 0))],
        out_specs=[pl.BlockSpec((1, tm, HY_WIDTH), lambda bi, i: (bi, i, 0))] * 2,
        out_shape=[jax.ShapeDtypeStruct((b, l, HY_WIDTH), F32)] * 2,
        scratch_shapes=[pltpu.VMEM((tm + 2 * POOL_HALO, wd), F32)],
        compiler_params=_cp("parallel", "parallel"),
        name="hyena_pre",
    )(u, u, u, sw, sb.reshape(1, wd))


FILT_TILE = 512


def _filter_kernel(feat_ref, tdec_ref, w1_ref, b1_ref, f1_ref, w2_ref, b2_ref, f2_ref, w3_ref, dl_ref,
                   filt_ref, norm_ref, *, l, tr):
    i = pl.program_id(0)
    half = tr // 2
    h = jnp.sin(f1_ref[...] * (_dot(feat_ref[...], w1_ref[...], precision=HIGHEST) + b1_ref[...]))
    h = jnp.sin(f2_ref[...] * (_dot(h, w2_ref[...], precision=HIGHEST) + b2_ref[...]))
    h = _dot(h, w3_ref[...], precision=HIGHEST)
    dl = jnp.abs(dl_ref[...])
    total = jnp.zeros((1, HY_WIDTH), F32)
    for p in range(2):
        rows = slice(p * half, (p + 1) * half)
        hp = h[:, p * 2 * HY_WIDTH:(p + 1) * 2 * HY_WIDTH]
        decay = jnp.exp(-tdec_ref[rows] * dl)
        n = i * tr + p * half + lax.broadcasted_iota(jnp.int32, (half, 1), 0)
        f = jnp.where(n < l, hp[:, :HY_WIDTH], hp[:, HY_WIDTH:]) * decay
        f = jnp.where(n == l, 0.0, f)
        filt_ref[rows] = f
        total = total + jnp.sum(jnp.abs(f), axis=0, keepdims=True)

    @pl.when(i == 0)
    def _():
        norm_ref[...] = jnp.zeros_like(norm_ref)
    norm_ref[...] += total


def filter_tables(l):
    n = jnp.arange(2 * l)
    pos = jnp.where(n < l, n, 2 * l - n).astype(F32)[:, None]
    t = pos / max(l - 1, 1)
    w = 2.0 * math.pi * pos / l
    f = jnp.linspace(1e-4, FILT_BANDS - 1, FILT_BANDS, dtype=F32)[None, :]
    feat = jnp.concatenate([t, jnp.cos(f * w), -jnp.sin(f * w),
                            jnp.zeros((2 * l, LANES - FILT_EMB), F32)], axis=-1)
    half = min(FILT_TILE, 2 * l) // 2
    feat = feat.reshape(-1, 2, half, LANES).transpose(0, 2, 1, 3).reshape(l, 2 * LANES)
    return feat, t


def hyena_filter(l, tables, w1, b1, f1, w2, b2, f2, w3):
    feat, t = tables
    tr = min(FILT_TILE, 2 * l)
    hid = w2.shape[0]
    max_decay = math.log(DECAY_TARGET) / FAST_DECAY_PCT
    min_decay = math.log(DECAY_TARGET) / SLOW_DECAY_PCT
    deltas = jnp.linspace(min_decay, max_decay, HY_WIDTH, dtype=F32)[None, :]
    w1p = jnp.concatenate([w1, jnp.zeros((LANES - FILT_EMB, hid), F32)], 0)
    twice = lambda m: jax.scipy.linalg.block_diag(m, m)
    pair = lambda v: jnp.tile(v.reshape(1, hid), (1, 2))
    full = lambda a: pl.BlockSpec(a.shape, lambda i: (0,) * a.ndim)
    smalls = [twice(w1p), pair(b1), pair(f1), twice(w2), pair(b2), pair(f2), twice(w3), deltas]
    return pl.pallas_call(
        functools.partial(_filter_kernel, l=l, tr=tr),
        grid=(2 * l // tr,),
        in_specs=[pl.BlockSpec((tr // 2, 2 * LANES), lambda i: (i, 0)),
                  pl.BlockSpec((tr, 1), lambda i: (i, 0))] + [full(a) for a in smalls],
        out_specs=[pl.BlockSpec((tr, HY_WIDTH), lambda i: (i, 0)),
                   pl.BlockSpec((1, HY_WIDTH), lambda i: (0, 0))],
        out_shape=[jax.ShapeDtypeStruct((2 * l, HY_WIDTH), F32),
                   jax.ShapeDtypeStruct((1, HY_WIDTH), F32)],
        compiler_params=_cp("arbitrary"),
        name="hyena_filter",
    )(feat, t, *smalls)


def _dft_rows_kernel(f_ref, x_ref, o_ref):
    n1, nb, c = o_ref.shape[2:]
    xs = jnp.concatenate([x_ref[0, :, j, :] for j in range(nb)], axis=-1).astype(BF16)
    res = _dot(f_ref[...], xs)
    for p in range(2):
        for j in range(nb):
            o_ref[0, p, :, j, :] = res[p * n1:(p + 1) * n1, j * c:(j + 1) * c]


def dft_rows(fmat, x):
    b, k, n2, c = x.shape
    n1 = fmat.shape[0] // 2
    nb = min(DFT_COLS // c, n2)
    return pl.pallas_call(
        _dft_rows_kernel,
        grid=(b, n2 // nb),
        in_specs=[pl.BlockSpec((2 * n1, k), lambda bi, j: (0, 0)),
                  pl.BlockSpec((1, k, nb, c), lambda bi, j: (bi, 0, j, 0))],
        out_specs=pl.BlockSpec((1, 2, n1, nb, c), lambda bi, j: (bi, 0, 0, j, 0)),
        out_shape=jax.ShapeDtypeStruct((b, 2, n1, n2, c), F32),
        compiler_params=_cp("parallel", "parallel"),
        name="dft_rows",
    )(fmat, x)


SPEC_KB = 8


def _inner_dft(g, are, aim):
    n2 = are.shape[0]
    p1 = _dot(g, are)
    p2 = _dot(g, aim)
    return p1[:n2] - p2[n2:], p2[:n2] + p1[n2:]


def _spectrum_kernel(g_ref, a_ref, h_ref):
    for j in range(g_ref.shape[0]):
        h_ref[0, j], h_ref[1, j] = _inner_dft(g_ref[j], a_ref[0, 0, j].astype(BF16), a_ref[0, 1, j].astype(BF16))


def filter_spectrum(gs, a):
    _, _, n1, n2, c = a.shape
    kb = min(SPEC_KB, n1)
    return pl.pallas_call(
        _spectrum_kernel,
        grid=(n1 // kb,),
        in_specs=[pl.BlockSpec((kb, 2 * n2, n2), lambda k: (k, 0, 0)),
                  pl.BlockSpec((1, 2, kb, n2, c), lambda k: (0, 0, k, 0, 0))],
        out_specs=pl.BlockSpec((2, kb, n2, c), lambda k: (0, k, 0, 0)),
        out_shape=jax.ShapeDtypeStruct((2, n1, n2, c), F32),
        compiler_params=_cp("parallel"),
        name="filter_spectrum",
    )(gs, a)


def _spectral_kernel(g_ref, gt_ref, h_ref, a_ref, b_ref):
    n2 = a_ref.shape[3]
    for j in range(g_ref.shape[0]):
        xre, xim = _inner_dft(g_ref[j], a_ref[0, 0, j].astype(BF16), a_ref[0, 1, j].astype(BF16))
        hre, him = h_ref[0, j], h_ref[1, j]
        yre = (xre * hre - xim * him).astype(BF16)
        yim = (xre * him + xim * hre).astype(BF16)
        gt = gt_ref[j]
        q1 = _dot(gt, yre)
        q2 = _dot(gt, yim)
        b_ref[0, 0, j] = q1[:n2] + q2[n2:]
        b_ref[0, 1, j] = q2[:n2] - q1[n2:]


def spectral_multiply(gs, gts, hf, a):
    b, _, n1, n2, c = a.shape
    kb = min(SPEC_KB, n1)
    return pl.pallas_call(
        _spectral_kernel,
        grid=(n1 // kb, b),
        in_specs=[pl.BlockSpec((kb, 2 * n2, n2), lambda k, bi: (k, 0, 0)),
                  pl.BlockSpec((kb, 2 * n2, n2), lambda k, bi: (k, 0, 0)),
                  pl.BlockSpec((2, kb, n2, c), lambda k, bi: (0, k, 0, 0)),
                  pl.BlockSpec((1, 2, kb, n2, c), lambda k, bi: (bi, 0, k, 0, 0))],
        out_specs=pl.BlockSpec((1, 2, kb, n2, c), lambda k, bi: (bi, 0, k, 0, 0)),
        out_shape=jax.ShapeDtypeStruct(a.shape, F32),
        compiler_params=_cp("parallel", "arbitrary"),
        name="spectral_multiply",
    )(gs, gts, hf, a)


def _idft_rows_kernel(f_ref, bm_ref, x0_ref, z_ref, inv_ref, bias_ref, o_ref, *, inv_n):
    m, nb, c = o_ref.shape[1:]
    bflat = bm_ref[0].reshape(-1, c).astype(BF16)
    y = _dot(f_ref[...], bflat) * inv_n
    for j in range(nb):
        z = z_ref[0, :, j, :]
        o_ref[0, :, j, :] = x0_ref[0, :, j, :] * (y[j * m:(j + 1) * m] * inv_ref[...] + bias_ref[...] * z)


def idft_rows_gate(fmat, bm, x0, z, inv_norm, bias, n):
    b, _, n1, n2, c = bm.shape
    m = fmat.shape[0]
    nb = min(DFT_COLS // c, n2)
    spread = jnp.kron(fmat, jnp.eye(nb, dtype=fmat.dtype)).reshape(m, nb, 2 * n1 * nb)
    spread = spread.transpose(1, 0, 2).reshape(nb * m, 2 * n1 * nb)
    blk = pl.BlockSpec((1, m, nb, c), lambda bi, j: (bi, 0, j, 0))
    return pl.pallas_call(
        functools.partial(_idft_rows_kernel, inv_n=1.0 / n),
        grid=(b, n2 // nb),
        in_specs=[pl.BlockSpec(spread.shape, lambda bi, j: (0, 0), pipeline_mode=pl.Buffered(1)),
                  pl.BlockSpec((1, 2, n1, nb, c), lambda bi, j: (bi, 0, 0, j, 0)),
                  blk, blk,
                  pl.BlockSpec((1, c), lambda bi, j: (0, 0)),
                  pl.BlockSpec((1, c), lambda bi, j: (0, 0))],
        out_specs=blk,
        out_shape=jax.ShapeDtypeStruct((b, m, n2, c), F32),
        compiler_params=_cp("parallel", "parallel"),
        name="idft_rows_gate",
    )(spread, bm, x0, z, inv_norm, bias)


def dft_tables(l):
    n = 2 * l
    n2 = FFT_N2
    n1 = n // n2
    k1 = np.arange(n1)
    ang1 = 2.0 * np.pi * np.outer(k1, k1) / n1
    f1 = np.concatenate([np.cos(ang1), -np.sin(ang1)], 0)
    f3 = np.concatenate([np.cos(ang1), -np.sin(ang1)], 1)[: n1 // 2]
    m2 = np.arange(n2)
    tw = 2.0 * np.pi * np.outer(k1, m2) / n
    ang2 = 2.0 * np.pi * np.outer(m2, m2) / n2
    twr, twi = jnp.asarray(np.cos(tw), F32)[:, None, :], jnp.asarray(-np.sin(tw), F32)[:, None, :]
    f2r, f2i = jnp.asarray(np.cos(ang2), F32)[None], jnp.asarray(-np.sin(ang2), F32)[None]
    gre, gim = twr * f2r - twi * f2i, twr * f2i + twi * f2r
    gs = jnp.concatenate([gre, gim], 1).astype(BF16)
    gts = jnp.concatenate([gre.transpose(0, 2, 1), gim.transpose(0, 2, 1)], 1).astype(BF16)
    return dict(n=n, n1=n1, n2=n2, f1=jnp.asarray(f1, BF16), f3=jnp.asarray(f3, BF16), gs=gs, gts=gts)


def hyena_long_conv(z, x0, filt2, norm, bias, tabs):
    b, l, c = z.shape
    n, n1, n2 = tabs["n"], tabs["n1"], tabs["n2"]
    hf = filter_spectrum(tabs["gs"], dft_rows(tabs["f1"], filt2.reshape(1, n1, n2, c)))
    a = dft_rows(tabs["f1"][:, : n1 // 2], z.reshape(b, n1 // 2, n2, c))
    bm = spectral_multiply(tabs["gs"], tabs["gts"], hf, a)
    y = idft_rows_gate(tabs["f3"], bm, x0.reshape(b, n1 // 2, n2, c), z.reshape(b, n1 // 2, n2, c),
                       1.0 / norm, bias, n)
    return y.reshape(b, l, c)


def _small_conv_kernel(ff_ref, fi_ref, filt_ref, z_ref, x0_ref, inv_ref, bias_ref, o_ref, *, l):
    ff = ff_ref[...]
    hs = _dot(ff, filt_ref[...].astype(BF16))
    hre, him = hs[:2 * l], hs[2 * l:]
    z = z_ref[0]
    xs = _dot(ff[:, :l], z.astype(BF16))
    xre, xim = xs[:2 * l], xs[2 * l:]
    yre = (xre * hre - xim * him).astype(BF16)
    yim = (xre * him + xim * hre).astype(BF16)
    y = _dot(fi_ref[...], jnp.concatenate([yre, yim], 0)) * (0.5 / l)
    o_ref[0] = (x0_ref[0] * (y * inv_ref[...] + bias_ref[...] * z)).astype(o_ref.dtype)


def small_long_conv(z, x0, filt2, norm, bias):
    b, l, c = z.shape
    k = np.arange(2 * l)
    ang = 2.0 * np.pi * np.outer(k, k) / (2 * l)
    ff = jnp.asarray(np.concatenate([np.cos(ang), -np.sin(ang)], 0), BF16)
    fi = jnp.asarray(np.concatenate([np.cos(ang), -np.sin(ang)], 1)[:l], BF16)
    full = lambda a: pl.BlockSpec(a.shape, lambda bi: (0,) * a.ndim)
    row = pl.BlockSpec((1, l, c), lambda bi: (bi, 0, 0))
    inv = 1.0 / norm
    return pl.pallas_call(
        functools.partial(_small_conv_kernel, l=l),
        grid=(b,),
        in_specs=[full(ff), full(fi), full(filt2), row, row, full(inv), full(bias)],
        out_specs=row,
        out_shape=jax.ShapeDtypeStruct((b, l, c), BF16),
        compiler_params=_cp("parallel"),
        name="small_long_conv",
    )(ff, fi, filt2, z, x0, inv, bias)


def _pool_kernel(p_ref, pp_ref, pn_ref, w_ref, sc_ref, o_ref, buf_ref, *, tm, l):
    _fill_padded(buf_ref, p_ref, pp_ref, pn_ref, tm)
    t = pl.program_id(1) * tm + lax.broadcasted_iota(jnp.int32, (tm, LANES), 0)
    lane = lax.broadcasted_iota(jnp.int32, (tm, LANES), 1)
    first_group = lane < POOL_GROUP
    h0 = POOL_HALO

    def win(col, lo, hi):
        acc = buf_ref[h0 + lo:h0 + lo + tm, col]
        for j in range(lo + 1, hi):
            acc = acc + buf_ref[h0 + j:h0 + j + tm, col]
        return acc

    parts = []
    for half, (wa, wb) in enumerate(((2, 4), (8, 16))):
        col = slice(half * LANES, (half + 1) * LANES)
        sa = win(col, -wa // 2, wa // 2)
        sb = sa + win(col, -wb // 2, -wa // 2) + win(col, wa // 2, wb // 2)
        hw = jnp.where(first_group, wa // 2, wb // 2)
        cnt = jnp.minimum(t + hw, l) - jnp.maximum(t - hw, 0)
        mean = jnp.where(first_group, sa, sb) / cnt.astype(F32)
        parts.append(mean - buf_ref[h0:h0 + tm, col])
    d = jnp.concatenate(parts, axis=-1).astype(BF16)
    o_ref[0] = (_dot(d, w_ref[...]) * sc_ref[...]).astype(o_ref.dtype)


def pool_mix(p, pool_w, pool_scale):
    b, l, wd = p.shape
    tm = min(ROW_TILE, l)
    nblk = l // tm
    wblk = jax.scipy.linalg.block_diag(*[pool_w[g] for g in range(len(POOL_WINDOWS))]).astype(BF16)
    return pl.pallas_call(
        functools.partial(_pool_kernel, tm=tm, l=l),
        grid=(b, nblk),
        in_specs=_halo_specs(tm, wd, nblk) + [pl.BlockSpec((wd, wd), lambda bi, i: (0, 0)),
                                              pl.BlockSpec((1, wd), lambda bi, i: (0, 0))],
        out_specs=pl.BlockSpec((1, tm, wd), lambda bi, i: (bi, i, 0)),
        out_shape=jax.ShapeDtypeStruct((b, l, wd), BF16),
        scratch_shapes=[pltpu.VMEM((tm + 2 * POOL_HALO, wd), F32)],
        compiler_params=_cp("parallel", "parallel"),
        name="pool_mix",
    )(p, p, p, wblk, pool_scale.reshape(1, wd))


def _outproj_kernel(x_ref, hy_ref, pool_ref, att_ref, w_ref, gate_ref, o_ref):
    a, b2 = HY_WIDTH, HY_WIDTH + POOL_WIDTH
    y = (_dot(hy_ref[0].astype(BF16), w_ref[0:a]) + _dot(pool_ref[0], w_ref[a:b2]) + _dot(att_ref[0], w_ref[b2:]))
    o_ref[0] = x_ref[0] + gate_ref[0] * y


def out_projection(x, y_hy, y_pool, y_att, w_out, gate):
    b, l, d = x.shape
    tm = min(ROW_TILE, l)
    row = lambda bi, i: (bi, i, 0)
    return pl.pallas_call(
        _outproj_kernel,
        grid=(b, l // tm),
        in_specs=[pl.BlockSpec((1, tm, d), row),
                  pl.BlockSpec((1, tm, y_hy.shape[2]), row),
                  pl.BlockSpec((1, tm, y_pool.shape[2]), row),
                  pl.BlockSpec((1, tm, y_att.shape[2]), row),
                  pl.BlockSpec(w_out.shape, lambda bi, i: (0, 0)),
                  pl.BlockSpec((1, 1, d), lambda bi, i: (bi, 0, 0))],
        out_specs=pl.BlockSpec((1, tm, d), row),
        out_shape=jax.ShapeDtypeStruct(x.shape, F32),
        compiler_params=_cp("parallel", "parallel"),
        name="out_projection",
    )(x, y_hy, y_pool, y_att, w_out, gate)


FF_CHUNK = 256
MOE_TILE = 1024
MOE_ROWS = (128, 192, 256, 320, 384)


def _swiglu_acc(h, wg_ref, wu_ref, wd_ref, lead):
    ff = wg_ref.shape[-1]
    acc = None
    for f0 in range(0, ff, FF_CHUNK):
        cs = slice(f0, min(f0 + FF_CHUNK, ff))
        g = _dot(h, wg_ref[lead + (slice(None), cs)])
        u = _dot(h, wu_ref[lead + (slice(None), cs)])
        a = (g * jax.nn.sigmoid(g) * u).astype(BF16)
        part = _dot(a, wd_ref[lead + (cs, slice(None))])
        acc = part if acc is None else acc + part
    return acc


def _ffn_kernel(x_ref, g_ref, sh_ref, sc_ref, gate_ref, wg_ref, wu_ref, wd_ref, o_ref):
    x = x_ref[0]
    h = _normmod(x, g_ref[...], sh_ref[0], sc_ref[0]).astype(BF16)
    o_ref[0] = x + gate_ref[0] * _swiglu_acc(h, wg_ref, wu_ref, wd_ref, ())


def ffn_dense(x, g, shift, scale, gate, wg, wu, wd):
    b, l, d = x.shape
    tm = min(ROW_TILE, l)
    row = lambda bi, i: (bi, i, 0)
    vec = lambda bi, i: (bi, 0, 0)
    wspec = lambda w: pl.BlockSpec(w.shape, lambda bi, i: (0, 0), pipeline_mode=pl.Buffered(1))
    return pl.pallas_call(
        _ffn_kernel,
        grid=(b, l // tm),
        in_specs=[pl.BlockSpec((1, tm, d), row),
                  pl.BlockSpec((1, d), lambda bi, i: (0, 0)),
                  pl.BlockSpec((1, 1, d), vec), pl.BlockSpec((1, 1, d), vec), pl.BlockSpec((1, 1, d), vec),
                  wspec(wg), wspec(wu), wspec(wd)],
        out_specs=pl.BlockSpec((1, tm, d), row),
        out_shape=jax.ShapeDtypeStruct(x.shape, F32),
        compiler_params=_cp("parallel", "parallel"),
        name="ffn_dense",
    )(x, g, shift, scale, gate, wg, wu, wd)


def _router_kernel(x_ref, g_ref, sh_ref, sc_ref, wr_ref, tri_ref, comb_ref, rank_ref):
    h = _normmod(x_ref[0], g_ref[...], sh_ref[0], sc_ref[0])
    logits = _dot(h, wr_ref[...], precision=HIGHEST)
    lane = lax.broadcasted_iota(jnp.int32, logits.shape, 1)
    neg = jnp.float32(-jnp.inf)
    lg = jnp.where(lane < N_EXPERTS, logits, neg)
    m1 = jnp.max(lg, axis=-1, keepdims=True)
    i1 = jnp.min(jnp.where(lg == m1, lane, LANES), axis=-1, keepdims=True)
    lg2 = jnp.where(lane == i1, neg, lg)
    m2 = jnp.max(lg2, axis=-1, keepdims=True)
    i2 = jnp.min(jnp.where(lg2 == m2, lane, LANES), axis=-1, keepdims=True)
    e = jnp.exp(m2 - m1)
    g1 = 1.0 / (1.0 + e)
    comb_ref[0] = jnp.where(lane == i1, g1, jnp.where(lane == i2, e * g1, 0.0))
    routed = (lane == i1) | (lane == i2)
    before = _dot(tri_ref[...], jnp.where(routed, 1.0, 0.0).astype(BF16))
    rank_ref[0] = jnp.where(routed, before, -1.0)


def moe_route(x, g, shift, scale, w_router):
    b, l, d = x.shape
    tm = min(MOE_TILE, l)
    wr = jnp.concatenate([w_router, jnp.zeros((d, LANES - N_EXPERTS), F32)], -1)
    tri = jnp.asarray(np.tril(np.ones((tm, tm), np.float32), -1), BF16)
    row = lambda bi, i: (bi, i, 0)
    vec = lambda bi, i: (bi, 0, 0)
    return pl.pallas_call(
        _router_kernel,
        grid=(b, l // tm),
        in_specs=[pl.BlockSpec((1, tm, d), row),
                  pl.BlockSpec((1, d), lambda bi, i: (0, 0)),
                  pl.BlockSpec((1, 1, d), vec), pl.BlockSpec((1, 1, d), vec),
                  pl.BlockSpec((d, LANES), lambda bi, i: (0, 0)),
                  pl.BlockSpec((tm, tm), lambda bi, i: (0, 0))],
        out_specs=[pl.BlockSpec((1, tm, LANES), row)] * 2,
        out_shape=[jax.ShapeDtypeStruct((b, l, LANES), F32)] * 2,
        compiler_params=_cp("parallel", "parallel"),
        name="moe_router",
    )(x, g, shift, scale, wr, tri)


def _moe_kernel(x_ref, g_ref, sh_ref, sc_ref, gate_ref, comb_ref, wg_ref, wu_ref, wd_ref, o_ref, acc_ref):
    e = pl.program_id(2)
    x = x_ref[0]
    h = _normmod(x, g_ref[...], sh_ref[0], sc_ref[0]).astype(BF16)
    comb = comb_ref[0]
    lane = lax.broadcasted_iota(jnp.int32, comb.shape, 1)
    ce = jnp.sum(jnp.where(lane == e, comb, 0.0), axis=-1, keepdims=True)
    y = ce * _swiglu_acc(h, wg_ref, wu_ref, wd_ref, (0, 0))

    @pl.when(e == 0)
    def _():
        acc_ref[...] = y

    @pl.when(e > 0)
    def _():
        acc_ref[...] += y

    @pl.when(e == N_EXPERTS - 1)
    def _():
        o_ref[0] = x + gate_ref[0] * acc_ref[...]


def moe_dense(x, g, shift, scale, gate, comb, wg, wu, wd, li):
    b, l, d = x.shape
    tm = min(ROW_TILE, l)
    ff = wg.shape[-1]
    row = lambda bi, i, e: (bi, i, 0)
    vec = lambda bi, i, e: (bi, 0, 0)
    return pl.pallas_call(
        _moe_kernel,
        grid=(b, l // tm, N_EXPERTS),
        in_specs=[pl.BlockSpec((1, tm, d), row),
                  pl.BlockSpec((1, d), lambda bi, i, e: (0, 0)),
                  pl.BlockSpec((1, 1, d), vec), pl.BlockSpec((1, 1, d), vec), pl.BlockSpec((1, 1, d), vec),
                  pl.BlockSpec((1, tm, LANES), row),
                  pl.BlockSpec((1, 1, d, ff), lambda bi, i, e: (li, e, 0, 0)),
                  pl.BlockSpec((1, 1, d, ff), lambda bi, i, e: (li, e, 0, 0)),
                  pl.BlockSpec((1, 1, ff, d), lambda bi, i, e: (li, e, 0, 0))],
        out_specs=pl.BlockSpec((1, tm, d), row),
        out_shape=jax.ShapeDtypeStruct(x.shape, F32),
        scratch_shapes=[pltpu.VMEM((tm, d), F32)],
        compiler_params=_cp("parallel", "parallel", "arbitrary"),
        name="moe_dense",
    )(x, g, shift, scale, gate, comb, wg, wu, wd)


def _moe_sparse_kernel(cnt_ref, x_ref, g_ref, sh_ref, sc_ref, gate_ref, comb_ref, rank_ref, rankt_ref,
                       wg_ref, wu_ref, wd_ref, *rest, t, sizes):
    fg_ref = rest[0] if len(rest) == 3 else None
    o_ref, h_scr = rest[-2:]
    i, e = pl.program_id(0), pl.program_id(1)
    cnt = cnt_ref[i * N_EXPERTS + e]

    @pl.when(e == 0)
    def _():
        x = x_ref[0]
        h_scr[...] = _normmod(x, g_ref[...], sh_ref[0], sc_ref[0]).astype(BF16)
        o_ref[0] = x

    rrow = rankt_ref[0, pl.ds(e, 1), :]
    lane = lax.broadcasted_iota(jnp.int32, (t, LANES), 1)
    rcol = jnp.sum(jnp.where(lane == e, rank_ref[0], 0.0), axis=-1, keepdims=True)
    ccol = jnp.sum(jnp.where(lane == e, comb_ref[0], 0.0), axis=-1, keepdims=True)

    def block(base, r):
        rid = (base + lax.broadcasted_iota(jnp.int32, (r, 1), 0)).astype(F32)
        onehot = jnp.where(rrow == rid, 1.0, 0.0).astype(BF16)
        xg = _dot(onehot, h_scr[...]).astype(BF16)
        y = _swiglu_acc(xg, wg_ref, wu_ref, wd_ref, (0, 0)).astype(BF16)
        cid = (base + lax.broadcasted_iota(jnp.int32, (1, r), 1)).astype(F32)
        onehot_t = jnp.where(rcol == cid, 1.0, 0.0).astype(BF16)
        o_ref[0] += (gate_ref[0] * ccol) * _dot(onehot_t, y)

    big = sizes[-1]
    nfull = cnt // big

    def full(k, carry):
        block(k * big, big)
        return carry
    lax.fori_loop(0, nfull, full, 0)
    base = nfull * big
    rem = cnt - base
    for lo, r in zip((0,) + sizes[:-1], sizes):
        @pl.when((rem > lo) & (rem <= r))
        def _(r=r):
            block(base, r)

    if fg_ref is not None:
        @pl.when(e == N_EXPERTS - 1)
        def _():
            o = o_ref[0]
            o_ref[0] = o * lax.rsqrt(jnp.mean(o * o, axis=-1, keepdims=True) + EPS) * fg_ref[...]


def moe_sparse(x, g, shift, scale, gate, comb, rank, wg, wu, wd, li, final_g=None):
    b, l, d = x.shape
    t = MOE_TILE
    tpb = l // t
    n_tiles = b * tpb
    ff = wg.shape[-1]
    rk = rank[..., :N_EXPERTS].reshape(n_tiles, t, N_EXPERTS)
    rank_t = rk.transpose(0, 2, 1)
    cnt = jnp.sum(rk >= 0, axis=1).astype(jnp.int32).reshape(n_tiles * N_EXPERTS)
    row = lambda i, e, c: (i // tpb, i % tpb, 0)
    vec = lambda i, e, c: (i // tpb, 0, 0)
    grid_spec = pltpu.PrefetchScalarGridSpec(
        num_scalar_prefetch=1,
        grid=(n_tiles, N_EXPERTS),
        in_specs=[pl.BlockSpec((1, t, d), row, pipeline_mode=pl.Buffered(1)),
                  pl.BlockSpec((1, d), lambda i, e, c: (0, 0)),
                  pl.BlockSpec((1, 1, d), vec), pl.BlockSpec((1, 1, d), vec), pl.BlockSpec((1, 1, d), vec),
                  pl.BlockSpec((1, t, LANES), row, pipeline_mode=pl.Buffered(1)),
                  pl.BlockSpec((1, t, LANES), row, pipeline_mode=pl.Buffered(1)),
                  pl.BlockSpec((1, N_EXPERTS, t), lambda i, e, c: (i, 0, 0)),
                  pl.BlockSpec((1, 1, d, ff), lambda i, e, c: (li, e, 0, 0)),
                  pl.BlockSpec((1, 1, d, ff), lambda i, e, c: (li, e, 0, 0)),
                  pl.BlockSpec((1, 1, ff, d), lambda i, e, c: (li, e, 0, 0))]
        + ([] if final_g is None else [pl.BlockSpec((1, d), lambda i, e, c: (0, 0))]),
        out_specs=pl.BlockSpec((1, t, d), row),
        scratch_shapes=[pltpu.VMEM((t, d), BF16)],
    )
    return pl.pallas_call(
        functools.partial(_moe_sparse_kernel, t=t, sizes=MOE_ROWS),
        grid_spec=grid_spec,
        out_shape=jax.ShapeDtypeStruct(x.shape, F32),
        compiler_params=_cp("parallel", "arbitrary"),
        name="moe_sparse",
    )(cnt, x, g, shift, scale, gate, comb, rank, rank_t, wg, wu, wd, *(() if final_g is None else (final_g,)))


def _final_kernel(x_ref, g_ref, o_ref):
    x = x_ref[0]
    o_ref[0] = x * lax.rsqrt(jnp.mean(x * x, axis=-1, keepdims=True) + EPS) * g_ref[...]


def final_norm(x, g):
    b, l, d = x.shape
    tm = min(2 * ROW_TILE, l)
    return pl.pallas_call(
        _final_kernel,
        grid=(b, l // tm),
        in_specs=[pl.BlockSpec((1, tm, d), lambda bi, i: (bi, i, 0)),
                  pl.BlockSpec((1, d), lambda bi, i: (0, 0))],
        out_specs=pl.BlockSpec((1, tm, d), lambda bi, i: (bi, i, 0)),
        out_shape=jax.ShapeDtypeStruct(x.shape, F32),
        compiler_params=_cp("parallel", "parallel"),
        name="final_norm",
    )(x, g)


def kernel(x, c, ctx, c_ctx, norm1_g, norm2_g, w_ada, b_ada, w_in, hy_short_w, hy_short_b, filt_w1, filt_b1, filt_freq1, filt_w2, filt_b2, filt_freq2, filt_w3, hy_bias, pool_w, pool_scale, q_norm_g, w_uq, kv_norm_g, w_ukv, w_out, ffn_wg, ffn_wu, ffn_wd, moe_router, moe_wg, moe_wu, moe_wd, final_g):
    b, l, d = x.shape
    lc = ctx.shape[1]
    depth = w_ada.shape[0]

    cc = jnp.zeros((8, d), F32).at[:b].set(c).at[b].set(c_ctx)
    mods = ada_vectors(cc, w_ada, b_ada)

    rope_l = rope_tables(l, True)
    rope_c = rope_tables(lc, False)
    ftab_l, ftab_c = filter_tables(l), filter_tables(lc)
    dtabs = dft_tables(l)
    moe_w = (moe_wg.astype(BF16), moe_wu.astype(BF16), moe_wd.astype(BF16))

    xl, xc = x, ctx
    for layer in range(depth):
        last = layer == depth - 1
        mod = mods[layer].reshape(8, 6, 1, d)
        mod_l = [mod[:b, j] for j in range(6)]
        mod_c = [jnp.broadcast_to(mod[b:b + 1, j], (b, 1, d)) for j in range(6)]
        g1 = norm1_g[layer].reshape(1, d)
        g2 = norm2_g[layer].reshape(1, d)
        w_in_l = w_in[layer].astype(BF16)
        w_out_l = w_out[layer].astype(BF16)
        mw = mla_weights(q_norm_g[layer], w_uq[layer], kv_norm_g[layer], w_ukv[layer])
        fw = (filt_w1[layer], filt_b1[layer], filt_freq1[layer], filt_w2[layer], filt_b2[layer],
              filt_freq2[layer], filt_w3[layer])
        bias = hy_bias[layer].reshape(1, HY_WIDTH)

        def channel_mix(xs, md, final=False):
            i = layer // 2
            fg = final_g.reshape(1, d)
            if layer % 2 == 1 and xs.shape[1] % MOE_TILE == 0:
                comb, rank = moe_route(xs, g2, md[3], md[4], moe_router[i])
                return moe_sparse(xs, g2, md[3], md[4], md[5], comb, rank, *moe_w, i, fg if final else None)
            if layer % 2 == 0:
                out = ffn_dense(xs, g2, md[3], md[4], md[5], ffn_wg[i].astype(BF16),
                                ffn_wu[i].astype(BF16), ffn_wd[i].astype(BF16))
            else:
                comb, _ = moe_route(xs, g2, md[3], md[4], moe_router[i])
                out = moe_dense(xs, g2, md[3], md[4], md[5], comb, *moe_w, i)
            return final_norm(out, fg) if final else out

        hy_l, pool_l, q_l, kt_l, v_l = in_projection(xl, g1, mod_l[0], mod_l[1], w_in_l, *rope_l, mw)
        hy_c, pool_c, q_c, kt_c, v_c = in_projection(xc, g1, mod_c[0], mod_c[1], w_in_l, *rope_c, mw)

        att_l = attention(q_l, kt_c, v_c, kt_l, v_l)
        z_l, x0_l = hyena_pre(hy_l, hy_short_w[layer], hy_short_b[layer])
        filt_l, nrm_l = hyena_filter(l, ftab_l, *fw)
        yhy_l = hyena_long_conv(z_l, x0_l, filt_l, nrm_l, bias, dtabs)
        ypool_l = pool_mix(pool_l, pool_w[layer], pool_scale[layer])
        xl = out_projection(xl, yhy_l, ypool_l, att_l, w_out_l, mod_l[2])

        if not last:
            att_c = attention(q_c, kt_c, v_c)
            z_c, x0_c = hyena_pre(hy_c, hy_short_w[layer], hy_short_b[layer])
            filt_c, nrm_c = hyena_filter(lc, ftab_c, *fw)
            yhy_c = small_long_conv(z_c, x0_c, filt_c, nrm_c, bias)
            ypool_c = pool_mix(pool_c, pool_w[layer], pool_scale[layer])
            xc = out_projection(xc, yhy_c, ypool_c, att_c, w_out_l, mod_c[2])
            xc = channel_mix(xc, mod_c)

        xl = channel_mix(xl, mod_l, final=last)
    return xl
```

```python
import functools
import math

import jax
import jax.numpy as jnp
import numpy as np
from jax import lax
from jax.experimental import pallas as pl
from jax.experimental.pallas import tpu as pltpu

F32 = jnp.float32
BF16 = jnp.bfloat16
HIGHEST = lax.Precision.HIGHEST

EPS = 1e-6
GRID_W = 64
HY_WIDTH = 256
POOL_WINDOWS = (2, 4, 8, 16)
POOL_WIDTH = 256
POOL_GROUP = 64
POOL_HALO = 8
MLA_HEADS = 8
QK_NOPE = 64
QK_ROPE = 32
V_DIM = 64
Q_LORA = 384
KV_LORA = 256
HEAD_PAD = 128
SM_SCALE = (QK_NOPE + QK_ROPE) ** -0.5
ROPE_THETA = 10000.0
FILT_EMB = 33
FILT_BANDS = 16
DECAY_TARGET = 1e-2
FAST_DECAY_PCT = 0.3
SLOW_DECAY_PCT = 1.5
N_EXPERTS = 8
LANES = 128
FFT_N2 = 128
VMEM_LIMIT = 56 * 2 ** 20
ROW_TILE = 1024
DFT_COLS = 2048
ADA_COLS = 1536


def _cp(*sem):
    return pltpu.CompilerParams(dimension_semantics=sem, vmem_limit_bytes=VMEM_LIMIT)


def _dot(a, b, **kw):
    return jnp.dot(a, b, preferred_element_type=F32, **kw)


def _normmod(x, g, shift, scale):
    ms = jnp.mean(x * x, axis=-1, keepdims=True)
    return x * lax.rsqrt(ms + EPS) * (g * (1.0 + scale)) + shift


def _ada_kernel(c_ref, w_ref, b_ref, o_ref):
    c = c_ref[...]
    s = c * jax.nn.sigmoid(c)
    o_ref[0] = _dot(s, w_ref[0], precision=HIGHEST) + b_ref[0]


def ada_vectors(cc, w_ada, b_ada):
    depth, d, n = w_ada.shape
    tn = ADA_COLS
    return pl.pallas_call(
        _ada_kernel,
        grid=(depth, n // tn),
        in_specs=[pl.BlockSpec((8, d), lambda l, j: (0, 0)),
                  pl.BlockSpec((1, d, tn), lambda l, j: (l, 0, j)),
                  pl.BlockSpec((1, 1, tn), lambda l, j: (l, 0, j))],
        out_specs=pl.BlockSpec((1, 8, tn), lambda l, j: (l, 0, j)),
        out_shape=jax.ShapeDtypeStruct((depth, 8, n), F32),
        compiler_params=_cp("parallel", "parallel"),
        name="ada_vectors",
    )(cc, w_ada, b_ada.reshape(depth, 1, n))


IN_HY, IN_POOL = 3 * HY_WIDTH, POOL_WIDTH
IN_OFFS = tuple(int(v) for v in np.cumsum((0, IN_HY, IN_POOL, Q_LORA, KV_LORA, QK_ROPE)))


def _inproj_kernel(x_ref, g_ref, sh_ref, sc_ref, w_ref, c_ref, s_ref, wqa_ref, wqb_ref, wk_ref, wv_ref,
                   pa_ref, pb_ref, hy_ref, pool_ref, q_ref, kt_ref, v_ref):
    h = _normmod(x_ref[0], g_ref[...], sh_ref[0], sc_ref[0]).astype(BF16)
    y = _dot(h, w_ref[...])
    o = IN_OFFS
    hy_ref[0] = y[:, o[0]:o[1]]
    pool_ref[0] = y[:, o[1]:o[2]]
    cq, ckv = y[:, o[2]:o[3]], y[:, o[3]:o[4]]
    kpe = y[:, o[4]:o[5]].astype(BF16)
    cqn = (cq * lax.rsqrt(jnp.mean(cq * cq, axis=-1, keepdims=True) + EPS)).astype(BF16)
    ckvn = (ckv * lax.rsqrt(jnp.mean(ckv * ckv, axis=-1, keepdims=True) + EPS)).astype(BF16)
    cs, sn = c_ref[...], s_ref[...]
    qa = _dot(cqn, wqa_ref[...])
    qb = _dot(cqn, wqb_ref[...])
    kn = _dot(ckvn, wk_ref[...])
    vv = _dot(ckvn, wv_ref[...])
    kpa = _dot(kpe, pa_ref[...])
    kpb = _dot(kpe, pb_ref[...])
    kpe_rot = kpa * cs + kpb * sn
    lane = lax.broadcasted_iota(jnp.int32, (1, HEAD_PAD), 1)
    for hd in range(MLA_HEADS):
        sl = slice(hd * HEAD_PAD, (hd + 1) * HEAD_PAD)
        q_ref[0, hd] = (qa[:, sl] * cs + qb[:, sl] * sn).astype(BF16)
        kt_ref[0, hd] = (kn[:, sl] + kpe_rot).T.astype(BF16)
        one_lane = V_DIM if hd % 2 == 0 else 0
        v_ref[0, hd] = jnp.where(lane == one_lane, 1.0, vv[:, sl]).astype(BF16)


def in_projection(x, g, shift, scale, w_in, ctab, stab, wts):
    b, l, d = x.shape
    tm = min(ROW_TILE, l)
    hw = MLA_HEADS
    row = lambda bi, i: (bi, i, 0)
    vec = lambda bi, i: (bi, 0, 0)
    full = lambda a: pl.BlockSpec(a.shape, lambda bi, i: (0,) * a.ndim)
    return pl.pallas_call(
        _inproj_kernel,
        grid=(b, l // tm),
        in_specs=[pl.BlockSpec((1, tm, d), row),
                  pl.BlockSpec((1, d), lambda bi, i: (0, 0)),
                  pl.BlockSpec((1, 1, d), vec),
                  pl.BlockSpec((1, 1, d), vec),
                  full(w_in),
                  pl.BlockSpec((tm, HEAD_PAD), lambda bi, i: (i, 0)),
                  pl.BlockSpec((tm, HEAD_PAD), lambda bi, i: (i, 0))] + [full(w) for w in wts],
        out_specs=[pl.BlockSpec((1, tm, IN_HY), row),
                   pl.BlockSpec((1, tm, IN_POOL), row),
                   pl.BlockSpec((1, hw, tm, HEAD_PAD), lambda bi, i: (bi, 0, i, 0)),
                   pl.BlockSpec((1, hw, HEAD_PAD, tm), lambda bi, i: (bi, 0, 0, i)),
                   pl.BlockSpec((1, hw, tm, HEAD_PAD), lambda bi, i: (bi, 0, i, 0))],
        out_shape=[jax.ShapeDtypeStruct((b, l, IN_HY), F32),
                   jax.ShapeDtypeStruct((b, l, IN_POOL), F32),
                   jax.ShapeDtypeStruct((b, hw, l, HEAD_PAD), BF16),
                   jax.ShapeDtypeStruct((b, hw, HEAD_PAD, l), BF16),
                   jax.ShapeDtypeStruct((b, hw, l, HEAD_PAD), BF16)],
        compiler_params=_cp("parallel", "parallel"),
        name="in_projection",
    )(x, g, shift, scale, w_in, ctab, stab, *wts)


def mla_weights(q_norm_g, w_uq, kv_norm_g, w_ukv):
    hw, dq = MLA_HEADS, QK_NOPE + QK_ROPE
    half = QK_ROPE // 2
    wq = (w_uq * q_norm_g[:, None] * (SM_SCALE * math.log2(math.e))).reshape(Q_LORA, hw, dq)
    pad = jnp.zeros((Q_LORA, hw, HEAD_PAD - dq), F32)
    wqa = jnp.concatenate([wq, pad], -1)
    swap = jnp.concatenate([wq[..., QK_NOPE + half:], wq[..., QK_NOPE:QK_NOPE + half]], -1)
    wqb = jnp.concatenate([jnp.zeros((Q_LORA, hw, QK_NOPE), F32), swap, pad], -1)
    wkv = (w_ukv * kv_norm_g[:, None]).reshape(KV_LORA, hw, QK_NOPE + V_DIM)
    z64 = jnp.zeros((KV_LORA, hw, 64), F32)
    wk = jnp.concatenate([wkv[..., :QK_NOPE], z64], -1)
    v = wkv[..., QK_NOPE:]
    wv = jnp.where((jnp.arange(hw) % 2 == 0)[None, :, None],
                   jnp.concatenate([v, z64], -1), jnp.concatenate([z64, v], -1))
    eye = np.eye(QK_ROPE, dtype=np.float32)
    pa = np.zeros((QK_ROPE, HEAD_PAD), np.float32)
    pa[:, QK_NOPE:QK_NOPE + QK_ROPE] = eye
    pb = np.zeros((QK_ROPE, HEAD_PAD), np.float32)
    pb[:, QK_NOPE:QK_NOPE + QK_ROPE] = np.concatenate([eye[:, half:], eye[:, :half]], 1)
    flat = lambda w: w.reshape(w.shape[0], hw * HEAD_PAD).astype(BF16)
    return (flat(wqa), flat(wqb), flat(wk), flat(wv), jnp.asarray(pa, BF16), jnp.asarray(pb, BF16))


def rope_tables(l, with_rope):
    ctab = np.zeros((l, HEAD_PAD), np.float32)
    stab = np.zeros((l, HEAD_PAD), np.float32)
    ctab[:, :QK_NOPE + QK_ROPE] = 1.0
    if not with_rope:
        return jnp.asarray(ctab), jnp.asarray(stab)
    n_freq = QK_ROPE // 4
    inv_freq = 1.0 / (ROPE_THETA ** (jnp.arange(n_freq, dtype=F32) / n_freq))
    rows = l // GRID_W
    row = jnp.repeat(jnp.arange(rows, dtype=F32), GRID_W)
    col = jnp.tile(jnp.arange(GRID_W, dtype=F32), rows)
    ang = jnp.concatenate([row[:, None] * inv_freq, col[:, None] * inv_freq], axis=-1)
    cos, sin = jnp.cos(ang), jnp.sin(ang)
    one = jnp.ones((l, QK_NOPE), F32)
    zero = jnp.zeros((l, QK_NOPE), F32)
    tail = jnp.zeros((l, HEAD_PAD - QK_NOPE - QK_ROPE), F32)
    return (jnp.concatenate([one, cos, cos, tail], -1), jnp.concatenate([zero, -sin, sin, tail], -1))


ATT_TQ = 1024
ATT_RB = 1024
ATT_CK = 2048


def _row_max(s):
    mp = s[:, 0:LANES]
    for j in range(1, s.shape[1] // LANES):
        mp = jnp.maximum(mp, s[:, j * LANES:(j + 1) * LANES])
    return jnp.max(mp, axis=-1, keepdims=True)


def _attn_kernel(q_ref, kct_ref, vc_ref, *rest, n_chunks, ck, tq):
    if n_chunks:
        kt_ref, v_ref, o_ref = rest
    else:
        (o_ref,) = rest
    rb = min(ATT_RB, tq)
    chains = [(hh, r) for hh in range(2) for r in range(tq // rb)]
    lane = lax.broadcasted_iota(jnp.int32, (1, HEAD_PAD), 1)

    def q_rows(hh, r):
        return q_ref[0, hh, r * rb:(r + 1) * rb, :]

    state = []
    for hh, r in chains:
        s = _dot(q_rows(hh, r), kct_ref[0, hh])
        m = _row_max(s)
        p = jnp.exp2(s - m).astype(BF16)
        state += [m, _dot(p, vc_ref[0, hh])]

    if n_chunks:
        def body(c, state):
            off = pl.multiple_of(c * ck, ck)
            out = []
            for i, (hh, r) in enumerate(chains):
                m, acc = state[2 * i], state[2 * i + 1]
                s = _dot(q_rows(hh, r), kt_ref[0, hh, :, pl.ds(off, ck)])
                m_new = jnp.maximum(m, _row_max(s))
                alpha = jnp.exp2(m - m_new)
                p = jnp.exp2(s - m_new).astype(BF16)
                out += [m_new, acc * alpha + _dot(p, v_ref[0, hh, pl.ds(off, ck), :])]
            return tuple(out)
        state = lax.fori_loop(0, n_chunks, body, tuple(state), unroll=2 if n_chunks % 2 == 0 else 1)

    for r in range(tq // rb):
        outs = []
        for hh in range(2):
            acc = state[2 * chains.index((hh, r)) + 1]
            one_lane = V_DIM if hh == 0 else 0
            denom = jnp.sum(jnp.where(lane == one_lane, acc, 0.0), axis=-1, keepdims=True)
            outs.append(acc / denom)
        o_ref[0, r * rb:(r + 1) * rb, :] = jnp.where(lane < V_DIM, outs[0], outs[1]).astype(o_ref.dtype)


def attention(q, kct, vc, kt=None, v=None):
    b, hw, lq, _ = q.shape
    lc = vc.shape[2]
    tq = min(ATT_TQ, lq)
    ck = ATT_CK if kt is None else min(ATT_CK, kt.shape[3])
    n_chunks = 0 if kt is None else kt.shape[3] // ck
    in_specs = [pl.BlockSpec((1, 2, tq, HEAD_PAD), lambda bi, hp, i: (bi, hp, i, 0)),
                pl.BlockSpec((1, 2, HEAD_PAD, lc), lambda bi, hp, i: (bi, hp, 0, 0)),
                pl.BlockSpec((1, 2, lc, HEAD_PAD), lambda bi, hp, i: (bi, hp, 0, 0))]
    args = [q, kct, vc]
    if n_chunks:
        l = kt.shape[3]
        in_specs += [pl.BlockSpec((1, 2, HEAD_PAD, l), lambda bi, hp, i: (bi, hp, 0, 0)),
                     pl.BlockSpec((1, 2, l, HEAD_PAD), lambda bi, hp, i: (bi, hp, 0, 0))]
        args += [kt, v]
    return pl.pallas_call(
        functools.partial(_attn_kernel, n_chunks=n_chunks, ck=ck, tq=tq),
        grid=(b, hw // 2, lq // tq),
        in_specs=in_specs,
        out_specs=pl.BlockSpec((1, tq, HEAD_PAD), lambda bi, hp, i: (bi, i, hp)),
        out_shape=jax.ShapeDtypeStruct((b, lq, hw * V_DIM), BF16),
        compiler_params=_cp("parallel", "parallel", "arbitrary"),
        name="attention",
    )(*args)


def _halo_specs(tm, width, nblk):
    r = tm // POOL_HALO
    last = nblk * r - 1
    return [pl.BlockSpec((1, tm, width), lambda bi, i: (bi, i, 0)),
            pl.BlockSpec((1, POOL_HALO, width), lambda bi, i: (bi, jnp.maximum(i * r - 1, 0), 0)),
            pl.BlockSpec((1, POOL_HALO, width), lambda bi, i: (bi, jnp.minimum((i + 1) * r, last), 0))]


def _fill_padded(buf_ref, cur_ref, prev_ref, next_ref, tm):
    i = pl.program_id(1)
    first = i == 0
    last = i == pl.num_programs(1) - 1
    buf_ref[0:POOL_HALO] = jnp.where(first, 0.0, prev_ref[0])
    buf_ref[POOL_HALO:POOL_HALO + tm] = cur_ref[0]
    buf_ref[POOL_HALO + tm:] = jnp.where(last, 0.0, next_ref[0])


def _hyena_pre_kernel(u_ref, up_ref, un_ref, w_ref, b_ref, z_ref, x0_ref, buf_ref, *, tm):
    _fill_padded(buf_ref, u_ref, up_ref, un_ref, tm)
    w = w_ref[...]
    uc = (buf_ref[POOL_HALO - 1:POOL_HALO - 1 + tm] * w[0:1] + buf_ref[POOL_HALO:POOL_HALO + tm] * w[1:2]
          + buf_ref[POOL_HALO + 1:POOL_HALO + 1 + tm] * w[2:3] + b_ref[...])
    x0_ref[0] = uc[:, :HY_WIDTH]
    z_ref[0] = uc[:, HY_WIDTH:2 * HY_WIDTH] * uc[:, 2 * HY_WIDTH:]


def hyena_pre(u, sw, sb):
    b, l, wd = u.shape
    tm = min(ROW_TILE, l)
    nblk = l // tm
    return pl.pallas_call(
        functools.partial(_hyena_pre_kernel, tm=tm),
        grid=(b, nblk),
        in_specs=_halo_specs(tm, wd, nblk) + [pl.BlockSpec((3, wd), lambda bi, i: (0, 0)),
                                              pl.BlockSpec((1, wd), lambda bi, i: (0, 0))],
        out_specs=[pl.BlockSpec((1, tm, HY_WIDTH), lambda bi, i: (bi, i, 0))] * 2,
        out_shape=[jax.ShapeDtypeStruct((b, l, HY_WIDTH), F32)] * 2,
        scratch_shapes=[pltpu.VMEM((tm + 2 * POOL_HALO, wd), F32)],
        compiler_params=_cp("parallel", "parallel"),
        name="hyena_pre",
    )(u, u, u, sw, sb.reshape(1, wd))


FILT_TILE = 1024


def _filter_kernel(feat_ref, tdec_ref, w1_ref, b1_ref, f1_ref, w2_ref, b2_ref, f2_ref, w3_ref, dl_ref,
                   filt_ref, norm_ref, *, l, tr):
    i = pl.program_id(0)
    half = tr // 2
    h = jnp.sin(f1_ref[...] * (_dot(feat_ref[...], w1_ref[...], precision=HIGHEST) + b1_ref[...]))
    h = jnp.sin(f2_ref[...] * (_dot(h, w2_ref[...], precision=HIGHEST) + b2_ref[...]))
    h = _dot(h, w3_ref[...], precision=HIGHEST)
    dl = jnp.abs(dl_ref[...])
    total = jnp.zeros((1, HY_WIDTH), F32)
    for p in range(2):
        rows = slice(p * half, (p + 1) * half)
        hp = h[:, p * 2 * HY_WIDTH:(p + 1) * 2 * HY_WIDTH]
        decay = jnp.exp(-tdec_ref[rows] * dl)
        n = i * tr + p * half + lax.broadcasted_iota(jnp.int32, (half, 1), 0)
        f = jnp.where(n < l, hp[:, :HY_WIDTH], hp[:, HY_WIDTH:]) * decay
        f = jnp.where(n == l, 0.0, f)
        filt_ref[rows] = f
        total = total + jnp.sum(jnp.abs(f), axis=0, keepdims=True)

    @pl.when(i == 0)
    def _():
        norm_ref[...] = jnp.zeros_like(norm_ref)
    norm_ref[...] += total


def filter_tables(l):
    n = jnp.arange(2 * l)
    pos = jnp.where(n < l, n, 2 * l - n).astype(F32)[:, None]
    t = pos / max(l - 1, 1)
    w = 2.0 * math.pi * pos / l
    f = jnp.linspace(1e-4, FILT_BANDS - 1, FILT_BANDS, dtype=F32)[None, :]
    feat = jnp.concatenate([t, jnp.cos(f * w), -jnp.sin(f * w),
                            jnp.zeros((2 * l, LANES - FILT_EMB), F32)], axis=-1)
    half = min(FILT_TILE, 2 * l) // 2
    feat = feat.reshape(-1, 2, half, LANES).transpose(0, 2, 1, 3).reshape(l, 2 * LANES)
    return feat, t


def hyena_filter(l, tables, w1, b1, f1, w2, b2, f2, w3):
    feat, t = tables
    tr = min(FILT_TILE, 2 * l)
    hid = w2.shape[0]
    max_decay = math.log(DECAY_TARGET) / FAST_DECAY_PCT
    min_decay = math.log(DECAY_TARGET) / SLOW_DECAY_PCT
    deltas = jnp.linspace(min_decay, max_decay, HY_WIDTH, dtype=F32)[None, :]
    w1p = jnp.concatenate([w1, jnp.zeros((LANES - FILT_EMB, hid), F32)], 0)
    twice = lambda m: jax.scipy.linalg.block_diag(m, m)
    pair = lambda v: jnp.tile(v.reshape(1, hid), (1, 2))
    full = lambda a: pl.BlockSpec(a.shape, lambda i: (0,) * a.ndim)
    smalls = [twice(w1p), pair(b1), pair(f1), twice(w2), pair(b2), pair(f2), twice(w3), deltas]
    return pl.pallas_call(
        functools.partial(_filter_kernel, l=l, tr=tr),
        grid=(2 * l // tr,),
        in_specs=[pl.BlockSpec((tr // 2, 2 * LANES), lambda i: (i, 0)),
                  pl.BlockSpec((tr, 1), lambda i: (i, 0))] + [full(a) for a in smalls],
        out_specs=[pl.BlockSpec((tr, HY_WIDTH), lambda i: (i, 0)),
                   pl.BlockSpec((1, HY_WIDTH), lambda i: (0, 0))],
        out_shape=[jax.ShapeDtypeStruct((2 * l, HY_WIDTH), F32),
                   jax.ShapeDtypeStruct((1, HY_WIDTH), F32)],
        compiler_params=_cp("arbitrary"),
        name="hyena_filter",
    )(feat, t, *smalls)


def _dft_rows_kernel(f_ref, x_ref, o_ref):
    n1, nb, c = o_ref.shape[2:]
    xs = jnp.concatenate([x_ref[0, :, j, :] for j in range(nb)], axis=-1).astype(BF16)
    res = _dot(f_ref[...], xs)
    for p in range(2):
        for j in range(nb):
            o_ref[0, p, :, j, :] = res[p * n1:(p + 1) * n1, j * c:(j + 1) * c]


def dft_rows(fmat, x):
    b, k, n2, c = x.shape
    n1 = fmat.shape[0] // 2
    nb = min(DFT_COLS // c, n2)
    return pl.pallas_call(
        _dft_rows_kernel,
        grid=(b, n2 // nb),
        in_specs=[pl.BlockSpec((2 * n1, k), lambda bi, j: (0, 0)),
                  pl.BlockSpec((1, k, nb, c), lambda bi, j: (bi, 0, j, 0))],
        out_specs=pl.BlockSpec((1, 2, n1, nb, c), lambda bi, j: (bi, 0, 0, j, 0)),
        out_shape=jax.ShapeDtypeStruct((b, 2, n1, n2, c), F32),
        compiler_params=_cp("parallel", "parallel"),
        name="dft_rows",
    )(fmat, x)


SPEC_KB = 16


def _inner_dft(g, are, aim):
    n2 = are.shape[0]
    p1 = _dot(g, are)
    p2 = _dot(g, aim)
    return p1[:n2] - p2[n2:], p2[:n2] + p1[n2:]


def _spectrum_kernel(g_ref, a_ref, h_ref):
    for j in range(g_ref.shape[0]):
        h_ref[0, j], h_ref[1, j] = _inner_dft(g_ref[j], a_ref[0, 0, j].astype(BF16), a_ref[0, 1, j].astype(BF16))


def filter_spectrum(gs, a):
    _, _, n1, n2, c = a.shape
    kb = min(SPEC_KB, n1)
    return pl.pallas_call(
        _spectrum_kernel,
        grid=(n1 // kb,),
        in_specs=[pl.BlockSpec((kb, 2 * n2, n2), lambda k: (k, 0, 0)),
                  pl.BlockSpec((1, 2, kb, n2, c), lambda k: (0, 0, k, 0, 0))],
        out_specs=pl.BlockSpec((2, kb, n2, c), lambda k: (0, k, 0, 0)),
        out_shape=jax.ShapeDtypeStruct((2, n1, n2, c), F32),
        compiler_params=_cp("parallel"),
        name="filter_spectrum",
    )(gs, a)


def _spectral_kernel(g_ref, gt_ref, h_ref, a_ref, b_ref):
    n2 = a_ref.shape[3]
    for j in range(g_ref.shape[0]):
        xre, xim = _inner_dft(g_ref[j], a_ref[0, 0, j].astype(BF16), a_ref[0, 1, j].astype(BF16))
        hre, him = h_ref[0, j], h_ref[1, j]
        yre = (xre * hre - xim * him).astype(BF16)
        yim = (xre * him + xim * hre).astype(BF16)
        gt = gt_ref[j]
        q1 = _dot(gt, yre)
        q2 = _dot(gt, yim)
        b_ref[0, 0, j] = q1[:n2] + q2[n2:]
        b_ref[0, 1, j] = q2[:n2] - q1[n2:]


def spectral_multiply(gs, gts, hf, a):
    b, _, n1, n2, c = a.shape
    kb = min(SPEC_KB, n1)
    return pl.pallas_call(
        _spectral_kernel,
        grid=(n1 // kb, b),
        in_specs=[pl.BlockSpec((kb, 2 * n2, n2), lambda k, bi: (k, 0, 0)),
                  pl.BlockSpec((kb, 2 * n2, n2), lambda k, bi: (k, 0, 0)),
                  pl.BlockSpec((2, kb, n2, c), lambda k, bi: (0, k, 0, 0)),
                  pl.BlockSpec((1, 2, kb, n2, c), lambda k, bi: (bi, 0, k, 0, 0))],
        out_specs=pl.BlockSpec((1, 2, kb, n2, c), lambda k, bi: (bi, 0, k, 0, 0)),
        out_shape=jax.ShapeDtypeStruct(a.shape, F32),
        compiler_params=_cp("parallel", "arbitrary"),
        name="spectral_multiply",
    )(gs, gts, hf, a)


def _idft_rows_kernel(f_ref, bm_ref, x0_ref, z_ref, inv_ref, bias_ref, o_ref, *, inv_n):
    m, nb, c = o_ref.shape[1:]
    bflat = bm_ref[0].reshape(-1, c).astype(BF16)
    y = _dot(f_ref[...], bflat) * inv_n
    for j in range(nb):
        z = z_ref[0, :, j, :]
        o_ref[0, :, j, :] = x0_ref[0, :, j, :] * (y[j * m:(j + 1) * m] * inv_ref[...] + bias_ref[...] * z)


def idft_rows_gate(fmat, bm, x0, z, inv_norm, bias, n):
    b, _, n1, n2, c = bm.shape
    m = fmat.shape[0]
    nb = min(DFT_COLS // c, n2)
    spread = jnp.kron(fmat, jnp.eye(nb, dtype=fmat.dtype)).reshape(m, nb, 2 * n1 * nb)
    spread = spread.transpose(1, 0, 2).reshape(nb * m, 2 * n1 * nb)
    blk = pl.BlockSpec((1, m, nb, c), lambda bi, j: (bi, 0, j, 0))
    return pl.pallas_call(
        functools.partial(_idft_rows_kernel, inv_n=1.0 / n),
        grid=(b, n2 // nb),
        in_specs=[pl.BlockSpec(spread.shape, lambda bi, j: (0, 0), pipeline_mode=pl.Buffered(1)),
                  pl.BlockSpec((1, 2, n1, nb, c), lambda bi, j: (bi, 0, 0, j, 0)),
                  blk, blk,
                  pl.BlockSpec((1, c), lambda bi, j: (0, 0)),
                  pl.BlockSpec((1, c), lambda bi, j: (0, 0))],
        out_specs=blk,
        out_shape=jax.ShapeDtypeStruct((b, m, n2, c), F32),
        compiler_params=_cp("parallel", "parallel"),
        name="idft_rows_gate",
    )(spread, bm, x0, z, inv_norm, bias)


def dft_tables(l):
    n = 2 * l
    n2 = FFT_N2
    n1 = n // n2
    k1 = np.arange(n1)
    ang1 = 2.0 * np.pi * np.outer(k1, k1) / n1
    f1 = np.concatenate([np.cos(ang1), -np.sin(ang1)], 0)
    f3 = np.concatenate([np.cos(ang1), -np.sin(ang1)], 1)[: n1 // 2]
    m2 = np.arange(n2)
    tw = 2.0 * np.pi * np.outer(k1, m2) / n
    ang2 = 2.0 * np.pi * np.outer(m2, m2) / n2
    twr, twi = jnp.asarray(np.cos(tw), F32)[:, None, :], jnp.asarray(-np.sin(tw), F32)[:, None, :]
    f2r, f2i = jnp.asarray(np.cos(ang2), F32)[None], jnp.asarray(-np.sin(ang2), F32)[None]
    gre, gim = twr * f2r - twi * f2i, twr * f2i + twi * f2r
    gs = jnp.concatenate([gre, gim], 1).astype(BF16)
    gts = jnp.concatenate([gre.transpose(0, 2, 1), gim.transpose(0, 2, 1)], 1).astype(BF16)
    return dict(n=n, n1=n1, n2=n2, f1=jnp.asarray(f1, BF16), f3=jnp.asarray(f3, BF16), gs=gs, gts=gts)


def hyena_long_conv(z, x0, filt2, norm, bias, tabs):
    b, l, c = z.shape
    n, n1, n2 = tabs["n"], tabs["n1"], tabs["n2"]
    hf = filter_spectrum(tabs["gs"], dft_rows(tabs["f1"], filt2.reshape(1, n1, n2, c)))
    a = dft_rows(tabs["f1"][:, : n1 // 2], z.reshape(b, n1 // 2, n2, c))
    bm = spectral_multiply(tabs["gs"], tabs["gts"], hf, a)
    y = idft_rows_gate(tabs["f3"], bm, x0.reshape(b, n1 // 2, n2, c), z.reshape(b, n1 // 2, n2, c),
                       1.0 / norm, bias, n)
    return y.reshape(b, l, c)


def _small_conv_kernel(ff_ref, fi_ref, filt_ref, z_ref, x0_ref, inv_ref, bias_ref, o_ref, *, l):
    ff = ff_ref[...]
    hs = _dot(ff, filt_ref[...].astype(BF16))
    hre, him = hs[:2 * l], hs[2 * l:]
    z = z_ref[0]
    xs = _dot(ff[:, :l], z.astype(BF16))
    xre, xim = xs[:2 * l], xs[2 * l:]
    yre = (xre * hre - xim * him).astype(BF16)
    yim = (xre * him + xim * hre).astype(BF16)
    y = _dot(fi_ref[...], jnp.concatenate([yre, yim], 0)) * (0.5 / l)
    o_ref[0] = (x0_ref[0] * (y * inv_ref[...] + bias_ref[...] * z)).astype(o_ref.dtype)


def small_long_conv(z, x0, filt2, norm, bias):
    b, l, c = z.shape
    k = np.arange(2 * l)
    ang = 2.0 * np.pi * np.outer(k, k) / (2 * l)
    ff = jnp.asarray(np.concatenate([np.cos(ang), -np.sin(ang)], 0), BF16)
    fi = jnp.asarray(np.concatenate([np.cos(ang), -np.sin(ang)], 1)[:l], BF16)
    full = lambda a: pl.BlockSpec(a.shape, lambda bi: (0,) * a.ndim)
    row = pl.BlockSpec((1, l, c), lambda bi: (bi, 0, 0))
    inv = 1.0 / norm
    return pl.pallas_call(
        functools.partial(_small_conv_kernel, l=l),
        grid=(b,),
        in_specs=[full(ff), full(fi), full(filt2), row, row, full(inv), full(bias)],
        out_specs=row,
        out_shape=jax.ShapeDtypeStruct((b, l, c), BF16),
        compiler_params=_cp("parallel"),
        name="small_long_conv",
    )(ff, fi, filt2, z, x0, inv, bias)


def _pool_kernel(p_ref, pp_ref, pn_ref, w_ref, sc_ref, o_ref, buf_ref, *, tm, l):
    _fill_padded(buf_ref, p_ref, pp_ref, pn_ref, tm)
    t = pl.program_id(1) * tm + lax.broadcasted_iota(jnp.int32, (tm, LANES), 0)
    lane = lax.broadcasted_iota(jnp.int32, (tm, LANES), 1)
    first_group = lane < POOL_GROUP
    h0 = POOL_HALO

    def win(col, lo, hi):
        acc = buf_ref[h0 + lo:h0 + lo + tm, col]
        for j in range(lo + 1, hi):
            acc = acc + buf_ref[h0 + j:h0 + j + tm, col]
        return acc

    parts = []
    for half, (wa, wb) in enumerate(((2, 4), (8, 16))):
        col = slice(half * LANES, (half + 1) * LANES)
        sa = win(col, -wa // 2, wa // 2)
        sb = sa + win(col, -wb // 2, -wa // 2) + win(col, wa // 2, wb // 2)
        hw = jnp.where(first_group, wa // 2, wb // 2)
        cnt = jnp.minimum(t + hw, l) - jnp.maximum(t - hw, 0)
        mean = jnp.where(first_group, sa, sb) / cnt.astype(F32)
        parts.append(mean - buf_ref[h0:h0 + tm, col])
    d = jnp.concatenate(parts, axis=-1).astype(BF16)
    o_ref[0] = (_dot(d, w_ref[...]) * sc_ref[...]).astype(o_ref.dtype)


def pool_mix(p, pool_w, pool_scale):
    b, l, wd = p.shape
    tm = min(ROW_TILE, l)
    nblk = l // tm
    wblk = jax.scipy.linalg.block_diag(*[pool_w[g] for g in range(len(POOL_WINDOWS))]).astype(BF16)
    return pl.pallas_call(
        functools.partial(_pool_kernel, tm=tm, l=l),
        grid=(b, nblk),
        in_specs=_halo_specs(tm, wd, nblk) + [pl.BlockSpec((wd, wd), lambda bi, i: (0, 0)),
                                              pl.BlockSpec((1, wd), lambda bi, i: (0, 0))],
        out_specs=pl.BlockSpec((1, tm, wd), lambda bi, i: (bi, i, 0)),
        out_shape=jax.ShapeDtypeStruct((b, l, wd), BF16),
        scratch_shapes=[pltpu.VMEM((tm + 2 * POOL_HALO, wd), F32)],
        compiler_params=_cp("parallel", "parallel"),
        name="pool_mix",
    )(p, p, p, wblk, pool_scale.reshape(1, wd))


def _outproj_kernel(x_ref, hy_ref, pool_ref, att_ref, w_ref, gate_ref, o_ref):
    a, b2 = HY_WIDTH, HY_WIDTH + POOL_WIDTH
    y = (_dot(hy_ref[0].astype(BF16), w_ref[0:a]) + _dot(pool_ref[0], w_ref[a:b2]) + _dot(att_ref[0], w_ref[b2:]))
    o_ref[0] = x_ref[0] + gate_ref[0] * y


def out_projection(x, y_hy, y_pool, y_att, w_out, gate):
    b, l, d = x.shape
    tm = min(ROW_TILE, l)
    row = lambda bi, i: (bi, i, 0)
    return pl.pallas_call(
        _outproj_kernel,
        grid=(b, l // tm),
        in_specs=[pl.BlockSpec((1, tm, d), row),
                  pl.BlockSpec((1, tm, y_hy.shape[2]), row),
                  pl.BlockSpec((1, tm, y_pool.shape[2]), row),
                  pl.BlockSpec((1, tm, y_att.shape[2]), row),
                  pl.BlockSpec(w_out.shape, lambda bi, i: (0, 0)),
                  pl.BlockSpec((1, 1, d), lambda bi, i: (bi, 0, 0))],
        out_specs=pl.BlockSpec((1, tm, d), row),
        out_shape=jax.ShapeDtypeStruct(x.shape, F32),
        compiler_params=_cp("parallel", "parallel"),
        name="out_projection",
    )(x, y_hy, y_pool, y_att, w_out, gate)


FF_CHUNK = 256
MOE_TILE = 1024
MOE_ROWS = (128, 192, 256, 320, 384)


def _swiglu_acc(h, wg_ref, wu_ref, wd_ref, lead):
    ff = wg_ref.shape[-1]
    acc = None
    for f0 in range(0, ff, FF_CHUNK):
        cs = slice(f0, min(f0 + FF_CHUNK, ff))
        g = _dot(h, wg_ref[lead + (slice(None), cs)])
        u = _dot(h, wu_ref[lead + (slice(None), cs)])
        a = (g * jax.nn.sigmoid(g) * u).astype(BF16)
        part = _dot(a, wd_ref[lead + (cs, slice(None))])
        acc = part if acc is None else acc + part
    return acc


def _ffn_kernel(x_ref, g_ref, sh_ref, sc_ref, gate_ref, wg_ref, wu_ref, wd_ref, o_ref):
    x = x_ref[0]
    h = _normmod(x, g_ref[...], sh_ref[0], sc_ref[0]).astype(BF16)
    o_ref[0] = x + gate_ref[0] * _swiglu_acc(h, wg_ref, wu_ref, wd_ref, ())


def ffn_dense(x, g, shift, scale, gate, wg, wu, wd):
    b, l, d = x.shape
    tm = min(ROW_TILE, l)
    row = lambda bi, i: (bi, i, 0)
    vec = lambda bi, i: (bi, 0, 0)
    wspec = lambda w: pl.BlockSpec(w.shape, lambda bi, i: (0, 0), pipeline_mode=pl.Buffered(1))
    return pl.pallas_call(
        _ffn_kernel,
        grid=(b, l // tm),
        in_specs=[pl.BlockSpec((1, tm, d), row),
                  pl.BlockSpec((1, d), lambda bi, i: (0, 0)),
                  pl.BlockSpec((1, 1, d), vec), pl.BlockSpec((1, 1, d), vec), pl.BlockSpec((1, 1, d), vec),
                  wspec(wg), wspec(wu), wspec(wd)],
        out_specs=pl.BlockSpec((1, tm, d), row),
        out_shape=jax.ShapeDtypeStruct(x.shape, F32),
        compiler_params=_cp("parallel", "parallel"),
        name="ffn_dense",
    )(x, g, shift, scale, gate, wg, wu, wd)


def _router_kernel(x_ref, g_ref, sh_ref, sc_ref, wr_ref, tri_ref, comb_ref, rank_ref):
    h = _normmod(x_ref[0], g_ref[...], sh_ref[0], sc_ref[0])
    logits = _dot(h, wr_ref[...], precision=HIGHEST)
    lane = lax.broadcasted_iota(jnp.int32, logits.shape, 1)
    neg = jnp.float32(-jnp.inf)
    lg = jnp.where(lane < N_EXPERTS, logits, neg)
    m1 = jnp.max(lg, axis=-1, keepdims=True)
    i1 = jnp.min(jnp.where(lg == m1, lane, LANES), axis=-1, keepdims=True)
    lg2 = jnp.where(lane == i1, neg, lg)
    m2 = jnp.max(lg2, axis=-1, keepdims=True)
    i2 = jnp.min(jnp.where(lg2 == m2, lane, LANES), axis=-1, keepdims=True)
    e = jnp.exp(m2 - m1)
    g1 = 1.0 / (1.0 + e)
    comb_ref[0] = jnp.where(lane == i1, g1, jnp.where(lane == i2, e * g1, 0.0))
    routed = (lane == i1) | (lane == i2)
    before = _dot(tri_ref[...], jnp.where(routed, 1.0, 0.0).astype(BF16))
    rank_ref[0] = jnp.where(routed, before, -1.0)


def moe_route(x, g, shift, scale, w_router):
    b, l, d = x.shape
    tm = min(MOE_TILE, l)
    wr = jnp.concatenate([w_router, jnp.zeros((d, LANES - N_EXPERTS), F32)], -1)
    tri = jnp.asarray(np.tril(np.ones((tm, tm), np.float32), -1), BF16)
    row = lambda bi, i: (bi, i, 0)
    vec = lambda bi, i: (bi, 0, 0)
    return pl.pallas_call(
        _router_kernel,
        grid=(b, l // tm),
        in_specs=[pl.BlockSpec((1, tm, d), row),
                  pl.BlockSpec((1, d), lambda bi, i: (0, 0)),
                  pl.BlockSpec((1, 1, d), vec), pl.BlockSpec((1, 1, d), vec),
                  pl.BlockSpec((d, LANES), lambda bi, i: (0, 0)),
                  pl.BlockSpec((tm, tm), lambda bi, i: (0, 0))],
        out_specs=[pl.BlockSpec((1, tm, LANES), row)] * 2,
        out_shape=[jax.ShapeDtypeStruct((b, l, LANES), F32)] * 2,
        compiler_params=_cp("parallel", "parallel"),
        name="moe_router",
    )(x, g, shift, scale, wr, tri)


def _moe_kernel(x_ref, g_ref, sh_ref, sc_ref, gate_ref, comb_ref, wg_ref, wu_ref, wd_ref, o_ref, acc_ref):
    e = pl.program_id(2)
    x = x_ref[0]
    h = _normmod(x, g_ref[...], sh_ref[0], sc_ref[0]).astype(BF16)
    comb = comb_ref[0]
    lane = lax.broadcasted_iota(jnp.int32, comb.shape, 1)
    ce = jnp.sum(jnp.where(lane == e, comb, 0.0), axis=-1, keepdims=True)
    y = ce * _swiglu_acc(h, wg_ref, wu_ref, wd_ref, (0, 0))

    @pl.when(e == 0)
    def _():
        acc_ref[...] = y

    @pl.when(e > 0)
    def _():
        acc_ref[...] += y

    @pl.when(e == N_EXPERTS - 1)
    def _():
        o_ref[0] = x + gate_ref[0] * acc_ref[...]


def moe_dense(x, g, shift, scale, gate, comb, wg, wu, wd, li):
    b, l, d = x.shape
    tm = min(ROW_TILE, l)
    ff = wg.shape[-1]
    row = lambda bi, i, e: (bi, i, 0)
    vec = lambda bi, i, e: (bi, 0, 0)
    return pl.pallas_call(
        _moe_kernel,
        grid=(b, l // tm, N_EXPERTS),
        in_specs=[pl.BlockSpec((1, tm, d), row),
                  pl.BlockSpec((1, d), lambda bi, i, e: (0, 0)),
                  pl.BlockSpec((1, 1, d), vec), pl.BlockSpec((1, 1, d), vec), pl.BlockSpec((1, 1, d), vec),
                  pl.BlockSpec((1, tm, LANES), row),
                  pl.BlockSpec((1, 1, d, ff), lambda bi, i, e: (li, e, 0, 0)),
                  pl.BlockSpec((1, 1, d, ff), lambda bi, i, e: (li, e, 0, 0)),
                  pl.BlockSpec((1, 1, ff, d), lambda bi, i, e: (li, e, 0, 0))],
        out_specs=pl.BlockSpec((1, tm, d), row),
        out_shape=jax.ShapeDtypeStruct(x.shape, F32),
        scratch_shapes=[pltpu.VMEM((tm, d), F32)],
        compiler_params=_cp("parallel", "parallel", "arbitrary"),
        name="moe_dense",
    )(x, g, shift, scale, gate, comb, wg, wu, wd)


def _moe_sparse_kernel(cnt_ref, x_ref, g_ref, sh_ref, sc_ref, gate_ref, comb_ref, rank_ref, rankt_ref,
                       wg_ref, wu_ref, wd_ref, *rest, t, sizes):
    fg_ref = rest[0] if len(rest) == 3 else None
    o_ref, h_scr = rest[-2:]
    i, e = pl.program_id(0), pl.program_id(1)
    cnt = cnt_ref[i * N_EXPERTS + e]

    @pl.when(e == 0)
    def _():
        x = x_ref[0]
        h_scr[...] = _normmod(x, g_ref[...], sh_ref[0], sc_ref[0]).astype(BF16)
        o_ref[0] = x

    rrow = rankt_ref[0, pl.ds(e, 1), :]
    lane = lax.broadcasted_iota(jnp.int32, (t, LANES), 1)
    rcol = jnp.sum(jnp.where(lane == e, rank_ref[0], 0.0), axis=-1, keepdims=True)
    ccol = jnp.sum(jnp.where(lane == e, comb_ref[0], 0.0), axis=-1, keepdims=True)

    def block(base, r):
        rid = (base + lax.broadcasted_iota(jnp.int32, (r, 1), 0)).astype(F32)
        onehot = jnp.where(rrow == rid, 1.0, 0.0).astype(BF16)
        xg = _dot(onehot, h_scr[...]).astype(BF16)
        y = _swiglu_acc(xg, wg_ref, wu_ref, wd_ref, (0, 0)).astype(BF16)
        cid = (base + lax.broadcasted_iota(jnp.int32, (1, r), 1)).astype(F32)
        onehot_t = jnp.where(rcol == cid, 1.0, 0.0).astype(BF16)
        o_ref[0] += (gate_ref[0] * ccol) * _dot(onehot_t, y)

    big = sizes[-1]
    nfull = cnt // big

    def full(k, carry):
        block(k * big, big)
        return carry
    lax.fori_loop(0, nfull, full, 0)
    base = nfull * big
    rem = cnt - base
    for lo, r in zip((0,) + sizes[:-1], sizes):
        @pl.when((rem > lo) & (rem <= r))
        def _(r=r):
            block(base, r)

    if fg_ref is not None:
        @pl.when(e == N_EXPERTS - 1)
        def _():
            o = o_ref[0]
            o_ref[0] = o * lax.rsqrt(jnp.mean(o * o, axis=-1, keepdims=True) + EPS) * fg_ref[...]


def moe_sparse(x, g, shift, scale, gate, comb, rank, wg, wu, wd, li, final_g=None):
    b, l, d = x.shape
    t = MOE_TILE
    tpb = l // t
    n_tiles = b * tpb
    ff = wg.shape[-1]
    rk = rank[..., :N_EXPERTS].reshape(n_tiles, t, N_EXPERTS)
    rank_t = rk.transpose(0, 2, 1)
    cnt = jnp.sum(rk >= 0, axis=1).astype(jnp.int32).reshape(n_tiles * N_EXPERTS)
    row = lambda i, e, c: (i // tpb, i % tpb, 0)
    vec = lambda i, e, c: (i // tpb, 0, 0)
    grid_spec = pltpu.PrefetchScalarGridSpec(
        num_scalar_prefetch=1,
        grid=(n_tiles, N_EXPERTS),
        in_specs=[pl.BlockSpec((1, t, d), row, pipeline_mode=pl.Buffered(1)),
                  pl.BlockSpec((1, d), lambda i, e, c: (0, 0)),
                  pl.BlockSpec((1, 1, d), vec), pl.BlockSpec((1, 1, d), vec), pl.BlockSpec((1, 1, d), vec),
                  pl.BlockSpec((1, t, LANES), row, pipeline_mode=pl.Buffered(1)),
                  pl.BlockSpec((1, t, LANES), row, pipeline_mode=pl.Buffered(1)),
                  pl.BlockSpec((1, N_EXPERTS, t), lambda i, e, c: (i, 0, 0)),
                  pl.BlockSpec((1, 1, d, ff), lambda i, e, c: (li, e, 0, 0)),
                  pl.BlockSpec((1, 1, d, ff), lambda i, e, c: (li, e, 0, 0)),
                  pl.BlockSpec((1, 1, ff, d), lambda i, e, c: (li, e, 0, 0))]
        + ([] if final_g is None else [pl.BlockSpec((1, d), lambda i, e, c: (0, 0))]),
        out_specs=pl.BlockSpec((1, t, d), row),
        scratch_shapes=[pltpu.VMEM((t, d), BF16)],
    )
    return pl.pallas_call(
        functools.partial(_moe_sparse_kernel, t=t, sizes=MOE_ROWS),
        grid_spec=grid_spec,
        out_shape=jax.ShapeDtypeStruct(x.shape, F32),
        compiler_params=_cp("parallel", "arbitrary"),
        name="moe_sparse",
    )(cnt, x, g, shift, scale, gate, comb, rank, rank_t, wg, wu, wd, *(() if final_g is None else (final_g,)))


def _final_kernel(x_ref, g_ref, o_ref):
    x = x_ref[0]
    o_ref[0] = x * lax.rsqrt(jnp.mean(x * x, axis=-1, keepdims=True) + EPS) * g_ref[...]


def final_norm(x, g):
    b, l, d = x.shape
    tm = min(2 * ROW_TILE, l)
    return pl.pallas_call(
        _final_kernel,
        grid=(b, l // tm),
        in_specs=[pl.BlockSpec((1, tm, d), lambda bi, i: (bi, i, 0)),
                  pl.BlockSpec((1, d), lambda bi, i: (0, 0))],
        out_specs=pl.BlockSpec((1, tm, d), lambda bi, i: (bi, i, 0)),
        out_shape=jax.ShapeDtypeStruct(x.shape, F32),
        compiler_params=_cp("parallel", "parallel"),
        name="final_norm",
    )(x, g)


def kernel(x, c, ctx, c_ctx, norm1_g, norm2_g, w_ada, b_ada, w_in, hy_short_w, hy_short_b, filt_w1, filt_b1, filt_freq1, filt_w2, filt_b2, filt_freq2, filt_w3, hy_bias, pool_w, pool_scale, q_norm_g, w_uq, kv_norm_g, w_ukv, w_out, ffn_wg, ffn_wu, ffn_wd, moe_router, moe_wg, moe_wu, moe_wd, final_g):
    b, l, d = x.shape
    lc = ctx.shape[1]
    depth = w_ada.shape[0]

    cc = jnp.zeros((8, d), F32).at[:b].set(c).at[b].set(c_ctx)
    mods = ada_vectors(cc, w_ada, b_ada)

    rope_l = rope_tables(l, True)
    rope_c = rope_tables(lc, False)
    ftab_l, ftab_c = filter_tables(l), filter_tables(lc)
    dtabs = dft_tables(l)
    moe_w = (moe_wg.astype(BF16), moe_wu.astype(BF16), moe_wd.astype(BF16))

    xl, xc = x, ctx
    for layer in range(depth):
        last = layer == depth - 1
        mod = mods[layer].reshape(8, 6, 1, d)
        mod_l = [mod[:b, j] for j in range(6)]
        mod_c = [jnp.broadcast_to(mod[b:b + 1, j], (b, 1, d)) for j in range(6)]
        g1 = norm1_g[layer].reshape(1, d)
        g2 = norm2_g[layer].reshape(1, d)
        w_in_l = w_in[layer].astype(BF16)
        w_out_l = w_out[layer].astype(BF16)
        mw = mla_weights(q_norm_g[layer], w_uq[layer], kv_norm_g[layer], w_ukv[layer])
        fw = (filt_w1[layer], filt_b1[layer], filt_freq1[layer], filt_w2[layer], filt_b2[layer],
              filt_freq2[layer], filt_w3[layer])
        bias = hy_bias[layer].reshape(1, HY_WIDTH)

        def channel_mix(xs, md, final=False):
            i = layer // 2
            fg = final_g.reshape(1, d)
            if layer % 2 == 1 and xs.shape[1] % MOE_TILE == 0:
                comb, rank = moe_route(xs, g2, md[3], md[4], moe_router[i])
                return moe_sparse(xs, g2, md[3], md[4], md[5], comb, rank, *moe_w, i, fg if final else None)
            if layer % 2 == 0:
                out = ffn_dense(xs, g2, md[3], md[4], md[5], ffn_wg[i].astype(BF16),
                                ffn_wu[i].astype(BF16), ffn_wd[i].astype(BF16))
            else:
                comb, _ = moe_route(xs, g2, md[3], md[4], moe_router[i])
                out = moe_dense(xs, g2, md[3], md[4], md[5], comb, *moe_w, i)
            return final_norm(out, fg) if final else out

        hy_l, pool_l, q_l, kt_l, v_l = in_projection(xl, g1, mod_l[0], mod_l[1], w_in_l, *rope_l, mw)
        hy_c, pool_c, q_c, kt_c, v_c = in_projection(xc, g1, mod_c[0], mod_c[1], w_in_l, *rope_c, mw)

        att_l = attention(q_l, kt_c, v_c, kt_l, v_l)
        z_l, x0_l = hyena_pre(hy_l, hy_short_w[layer], hy_short_b[layer])
        filt_l, nrm_l = hyena_filter(l, ftab_l, *fw)
        yhy_l = hyena_long_conv(z_l, x0_l, filt_l, nrm_l, bias, dtabs)
        ypool_l = pool_mix(pool_l, pool_w[layer], pool_scale[layer])
        xl = out_projection(xl, yhy_l, ypool_l, att_l, w_out_l, mod_l[2])

        if not last:
            att_c = attention(q_c, kt_c, v_c)
            z_c, x0_c = hyena_pre(hy_c, hy_short_w[layer], hy_short_b[layer])
            filt_c, nrm_c = hyena_filter(lc, ftab_c, *fw)
            yhy_c = small_long_conv(z_c, x0_c, filt_c, nrm_c, bias)
            ypool_c = pool_mix(pool_c, pool_w[layer], pool_scale[layer])
            xc = out_projection(xc, yhy_c, ypool_c, att_c, w_out_l, mod_c[2])
            xc = channel_mix(xc, mod_c)

        xl = channel_mix(xl, mod_l, final=last)
    return xl
```

```python
import functools
import math

import jax
import jax.numpy as jnp
import numpy as np
from jax import lax
from jax.experimental import pallas as pl
from jax.experimental.pallas import tpu as pltpu

F32 = jnp.float32
BF16 = jnp.bfloat16
HIGHEST = lax.Precision.HIGHEST

EPS = 1e-6
GRID_W = 64
HY_WIDTH = 256
POOL_WINDOWS = (2, 4, 8, 16)
POOL_WIDTH = 256
POOL_GROUP = 64
POOL_HALO = 8
MLA_HEADS = 8
QK_NOPE = 64
QK_ROPE = 32
V_DIM = 64
Q_LORA = 384
KV_LORA = 256
HEAD_PAD = 128
SM_SCALE = (QK_NOPE + QK_ROPE) ** -0.5
ROPE_THETA = 10000.0
FILT_EMB = 33
FILT_BANDS = 16
DECAY_TARGET = 1e-2
FAST_DECAY_PCT = 0.3
SLOW_DECAY_PCT = 1.5
N_EXPERTS = 8
LANES = 128
FFT_N2 = 128
VMEM_LIMIT = 56 * 2 ** 20
ROW_TILE = 1024
DFT_COLS = 2048
ADA_COLS = 1536


def _cp(*sem):
    return pltpu.CompilerParams(dimension_semantics=sem, vmem_limit_bytes=VMEM_LIMIT)


def _dot(a, b, **kw):
    return jnp.dot(a, b, preferred_element_type=F32, **kw)


def _normmod(x, g, shift, scale):
    ms = jnp.mean(x * x, axis=-1, keepdims=True)
    return x * lax.rsqrt(ms + EPS) * (g * (1.0 + scale)) + shift


def _ada_kernel(c_ref, w_ref, b_ref, o_ref):
    c = c_ref[...]
    s = c * jax.nn.sigmoid(c)
    o_ref[0] = _dot(s, w_ref[0], precision=HIGHEST) + b_ref[0]


def ada_vectors(cc, w_ada, b_ada):
    depth, d, n = w_ada.shape
    tn = ADA_COLS
    return pl.pallas_call(
        _ada_kernel,
        grid=(depth, n // tn),
        in_specs=[pl.BlockSpec((8, d), lambda l, j: (0, 0)),
                  pl.BlockSpec((1, d, tn), lambda l, j: (l, 0, j)),
                  pl.BlockSpec((1, 1, tn), lambda l, j: (l, 0, j))],
        out_specs=pl.BlockSpec((1, 8, tn), lambda l, j: (l, 0, j)),
        out_shape=jax.ShapeDtypeStruct((depth, 8, n), F32),
        compiler_params=_cp("parallel", "parallel"),
        name="ada_vectors",
    )(cc, w_ada, b_ada.reshape(depth, 1, n))


IN_HY, IN_POOL = 3 * HY_WIDTH, POOL_WIDTH
IN_OFFS = tuple(int(v) for v in np.cumsum((0, IN_HY, IN_POOL, Q_LORA, KV_LORA, QK_ROPE)))


def _inproj_kernel(x_ref, g_ref, sh_ref, sc_ref, w_ref, c_ref, s_ref, wqa_ref, wqb_ref, wk_ref, wv_ref,
                   pa_ref, pb_ref, hy_ref, pool_ref, q_ref, kt_ref, v_ref):
    h = _normmod(x_ref[0], g_ref[...], sh_ref[0], sc_ref[0]).astype(BF16)
    y = _dot(h, w_ref[...])
    o = IN_OFFS
    hy_ref[0] = y[:, o[0]:o[1]]
    pool_ref[0] = y[:, o[1]:o[2]]
    cq, ckv = y[:, o[2]:o[3]], y[:, o[3]:o[4]]
    kpe = y[:, o[4]:o[5]].astype(BF16)
    cqn = (cq * lax.rsqrt(jnp.mean(cq * cq, axis=-1, keepdims=True) + EPS)).astype(BF16)
    ckvn = (ckv * lax.rsqrt(jnp.mean(ckv * ckv, axis=-1, keepdims=True) + EPS)).astype(BF16)
    cs, sn = c_ref[...], s_ref[...]
    qa = _dot(cqn, wqa_ref[...])
    qb = _dot(cqn, wqb_ref[...])
    kn = _dot(ckvn, wk_ref[...])
    vv = _dot(ckvn, wv_ref[...])
    kpa = _dot(kpe, pa_ref[...])
    kpb = _dot(kpe, pb_ref[...])
    kpe_rot = kpa * cs + kpb * sn
    lane = lax.broadcasted_iota(jnp.int32, (1, HEAD_PAD), 1)
    for hd in range(MLA_HEADS):
        sl = slice(hd * HEAD_PAD, (hd + 1) * HEAD_PAD)
        q_ref[0, hd] = (qa[:, sl] * cs + qb[:, sl] * sn).astype(BF16)
        kt_ref[0, hd] = (kn[:, sl] + kpe_rot).T.astype(BF16)
        one_lane = V_DIM if hd % 2 == 0 else 0
        v_ref[0, hd] = jnp.where(lane == one_lane, 1.0, vv[:, sl]).astype(BF16)


def in_projection(x, g, shift, scale, w_in, ctab, stab, wts):
    b, l, d = x.shape
    tm = min(ROW_TILE, l)
    hw = MLA_HEADS
    row = lambda bi, i: (bi, i, 0)
    vec = lambda bi, i: (bi, 0, 0)
    full = lambda a: pl.BlockSpec(a.shape, lambda bi, i: (0,) * a.ndim)
    return pl.pallas_call(
        _inproj_kernel,
        grid=(b, l // tm),
        in_specs=[pl.BlockSpec((1, tm, d), row),
                  pl.BlockSpec((1, d), lambda bi, i: (0, 0)),
                  pl.BlockSpec((1, 1, d), vec),
                  pl.BlockSpec((1, 1, d), vec),
                  full(w_in),
                  pl.BlockSpec((tm, HEAD_PAD), lambda bi, i: (i, 0)),
                  pl.BlockSpec((tm, HEAD_PAD), lambda bi, i: (i, 0))] + [full(w) for w in wts],
        out_specs=[pl.BlockSpec((1, tm, IN_HY), row),
                   pl.BlockSpec((1, tm, IN_POOL), row),
                   pl.BlockSpec((1, hw, tm, HEAD_PAD), lambda bi, i: (bi, 0, i, 0)),
                   pl.BlockSpec((1, hw, HEAD_PAD, tm), lambda bi, i: (bi, 0, 0, i)),
                   pl.BlockSpec((1, hw, tm, HEAD_PAD), lambda bi, i: (bi, 0, i, 0))],
        out_shape=[jax.ShapeDtypeStruct((b, l, IN_HY), F32),
                   jax.ShapeDtypeStruct((b, l, IN_POOL), F32),
                   jax.ShapeDtypeStruct((b, hw, l, HEAD_PAD), BF16),
                   jax.ShapeDtypeStruct((b, hw, HEAD_PAD, l), BF16),
                   jax.ShapeDtypeStruct((b, hw, l, HEAD_PAD), BF16)],
        compiler_params=_cp("parallel", "parallel"),
        name="in_projection",
    )(x, g, shift, scale, w_in, ctab, stab, *wts)


def mla_weights(q_norm_g, w_uq, kv_norm_g, w_ukv):
    hw, dq = MLA_HEADS, QK_NOPE + QK_ROPE
    half = QK_ROPE // 2
    wq = (w_uq * q_norm_g[:, None] * (SM_SCALE * math.log2(math.e))).reshape(Q_LORA, hw, dq)
    pad = jnp.zeros((Q_LORA, hw, HEAD_PAD - dq), F32)
    wqa = jnp.concatenate([wq, pad], -1)
    swap = jnp.concatenate([wq[..., QK_NOPE + half:], wq[..., QK_NOPE:QK_NOPE + half]], -1)
    wqb = jnp.concatenate([jnp.zeros((Q_LORA, hw, QK_NOPE), F32), swap, pad], -1)
    wkv = (w_ukv * kv_norm_g[:, None]).reshape(KV_LORA, hw, QK_NOPE + V_DIM)
    z64 = jnp.zeros((KV_LORA, hw, 64), F32)
    wk = jnp.concatenate([wkv[..., :QK_NOPE], z64], -1)
    v = wkv[..., QK_NOPE:]
    wv = jnp.where((jnp.arange(hw) % 2 == 0)[None, :, None],
                   jnp.concatenate([v, z64], -1), jnp.concatenate([z64, v], -1))
    eye = np.eye(QK_ROPE, dtype=np.float32)
    pa = np.zeros((QK_ROPE, HEAD_PAD), np.float32)
    pa[:, QK_NOPE:QK_NOPE + QK_ROPE] = eye
    pb = np.zeros((QK_ROPE, HEAD_PAD), np.float32)
    pb[:, QK_NOPE:QK_NOPE + QK_ROPE] = np.concatenate([eye[:, half:], eye[:, :half]], 1)
    flat = lambda w: w.reshape(w.shape[0], hw * HEAD_PAD).astype(BF16)
    return (flat(wqa), flat(wqb), flat(wk), flat(wv), jnp.asarray(pa, BF16), jnp.asarray(pb, BF16))


def rope_tables(l, with_rope):
    ctab = np.zeros((l, HEAD_PAD), np.float32)
    stab = np.zeros((l, HEAD_PAD), np.float32)
    ctab[:, :QK_NOPE + QK_ROPE] = 1.0
    if not with_rope:
        return jnp.asarray(ctab), jnp.asarray(stab)
    n_freq = QK_ROPE // 4
    inv_freq = 1.0 / (ROPE_THETA ** (jnp.arange(n_freq, dtype=F32) / n_freq))
    rows = l // GRID_W
    row = jnp.repeat(jnp.arange(rows, dtype=F32), GRID_W)
    col = jnp.tile(jnp.arange(GRID_W, dtype=F32), rows)
    ang = jnp.concatenate([row[:, None] * inv_freq, col[:, None] * inv_freq], axis=-1)
    cos, sin = jnp.cos(ang), jnp.sin(ang)
    one = jnp.ones((l, QK_NOPE), F32)
    zero = jnp.zeros((l, QK_NOPE), F32)
    tail = jnp.zeros((l, HEAD_PAD - QK_NOPE - QK_ROPE), F32)
    return (jnp.concatenate([one, cos, cos, tail], -1), jnp.concatenate([zero, -sin, sin, tail], -1))


ATT_TQ = 1024
ATT_RB = 1024
ATT_CK = 2048


def _row_max(s):
    mp = s[:, 0:LANES]
    for j in range(1, s.shape[1] // LANES):
        mp = jnp.maximum(mp, s[:, j * LANES:(j + 1) * LANES])
    return jnp.max(mp, axis=-1, keepdims=True)


def _attn_kernel(q_ref, kct_ref, vc_ref, *rest, n_chunks, ck, tq):
    if n_chunks:
        kt_ref, v_ref, o_ref = rest
    else:
        (o_ref,) = rest
    rb = min(ATT_RB, tq)
    chains = [(hh, r) for hh in range(2) for r in range(tq // rb)]
    lane = lax.broadcasted_iota(jnp.int32, (1, HEAD_PAD), 1)

    def q_rows(hh, r):
        return q_ref[0, hh, r * rb:(r + 1) * rb, :]

    state = []
    for hh, r in chains:
        s = _dot(q_rows(hh, r), kct_ref[0, hh])
        m = _row_max(s)
        p = jnp.exp2(s - m).astype(BF16)
        state += [m, _dot(p, vc_ref[0, hh])]

    if n_chunks:
        def body(c, state):
            off = pl.multiple_of(c * ck, ck)
            out = []
            for i, (hh, r) in enumerate(chains):
                m, acc = state[2 * i], state[2 * i + 1]
                s = _dot(q_rows(hh, r), kt_ref[0, hh, :, pl.ds(off, ck)])
                m_new = jnp.maximum(m, _row_max(s))
                alpha = jnp.exp2(m - m_new)
                p = jnp.exp2(s - m_new).astype(BF16)
                out += [m_new, acc * alpha + _dot(p, v_ref[0, hh, pl.ds(off, ck), :])]
            return tuple(out)
        state = lax.fori_loop(0, n_chunks, body, tuple(state), unroll=4 if n_chunks % 4 == 0 else 1)

    for r in range(tq // rb):
        outs = []
        for hh in range(2):
            acc = state[2 * chains.index((hh, r)) + 1]
            one_lane = V_DIM if hh == 0 else 0
            denom = jnp.sum(jnp.where(lane == one_lane, acc, 0.0), axis=-1, keepdims=True)
            outs.append(acc / denom)
        o_ref[0, r * rb:(r + 1) * rb, :] = jnp.where(lane < V_DIM, outs[0], outs[1]).astype(o_ref.dtype)


def attention(q, kct, vc, kt=None, v=None):
    b, hw, lq, _ = q.shape
    lc = vc.shape[2]
    tq = min(ATT_TQ, lq)
    ck = ATT_CK if kt is None else min(ATT_CK, kt.shape[3])
    n_chunks = 0 if kt is None else kt.shape[3] // ck
    in_specs = [pl.BlockSpec((1, 2, tq, HEAD_PAD), lambda bi, hp, i: (bi, hp, i, 0)),
                pl.BlockSpec((1, 2, HEAD_PAD, lc), lambda bi, hp, i: (bi, hp, 0, 0)),
                pl.BlockSpec((1, 2, lc, HEAD_PAD), lambda bi, hp, i: (bi, hp, 0, 0))]
    args = [q, kct, vc]
    if n_chunks:
        l = kt.shape[3]
        in_specs += [pl.BlockSpec((1, 2, HEAD_PAD, l), lambda bi, hp, i: (bi, hp, 0, 0)),
                     pl.BlockSpec((1, 2, l, HEAD_PAD), lambda bi, hp, i: (bi, hp, 0, 0))]
        args += [kt, v]
    return pl.pallas_call(
        functools.partial(_attn_kernel, n_chunks=n_chunks, ck=ck, tq=tq),
        grid=(b, hw // 2, lq // tq),
        in_specs=in_specs,
        out_specs=pl.BlockSpec((1, tq, HEAD_PAD), lambda bi, hp, i: (bi, i, hp)),
        out_shape=jax.ShapeDtypeStruct((b, lq, hw * V_DIM), BF16),
        compiler_params=_cp("parallel", "parallel", "arbitrary"),
        name="attention",
    )(*args)


def _halo_specs(tm, width, nblk):
    r = tm // POOL_HALO
    last = nblk * r - 1
    return [pl.BlockSpec((1, tm, width), lambda bi, i: (bi, i, 0)),
            pl.BlockSpec((1, POOL_HALO, width), lambda bi, i: (bi, jnp.maximum(i * r - 1, 0), 0)),
            pl.BlockSpec((1, POOL_HALO, width), lambda bi, i: (bi, jnp.minimum((i + 1) * r, last), 0))]


def _fill_padded(buf_ref, cur_ref, prev_ref, next_ref, tm):
    i = pl.program_id(1)
    first = i == 0
    last = i == pl.num_programs(1) - 1
    buf_ref[0:POOL_HALO] = jnp.where(first, 0.0, prev_ref[0])
    buf_ref[POOL_HALO:POOL_HALO + tm] = cur_ref[0]
    buf_ref[POOL_HALO + tm:] = jnp.where(last, 0.0, next_ref[0])


def _hyena_pre_kernel(u_ref, up_ref, un_ref, w_ref, b_ref, z_ref, x0_ref, buf_ref, *, tm):
    _fill_padded(buf_ref, u_ref, up_ref, un_ref, tm)
    w = w_ref[...]
    uc = (buf_ref[POOL_HALO - 1:POOL_HALO - 1 + tm] * w[0:1] + buf_ref[POOL_HALO:POOL_HALO + tm] * w[1:2]
          + buf_ref[POOL_HALO + 1:POOL_HALO + 1 + tm] * w[2:3] + b_ref[...])
    x0_ref[0] = uc[:, :HY_WIDTH]
    z_ref[0] = uc[:, HY_WIDTH:2 * HY_WIDTH] * uc[:, 2 * HY_WIDTH:]


def hyena_pre(u, sw, sb):
    b, l, wd = u.shape
    tm = min(ROW_TILE, l)
    nblk = l // tm
    return pl.pallas_call(
        functools.partial(_hyena_pre_kernel, tm=tm),
        grid=(b, nblk),
        in_specs=_halo_specs(tm, wd, nblk) + [pl.BlockSpec((3, wd), lambda bi, i: (0, 0)),
                                              pl.BlockSpec((1, wd), lambda bi, i: (0, 0))],
        out_specs=[pl.BlockSpec((1, tm, HY_WIDTH), lambda bi, i: (bi, i, 0))] * 2,
        out_shape=[jax.ShapeDtypeStruct((b, l, HY_WIDTH), F32)] * 2,
        scratch_shapes=[pltpu.VMEM((tm + 2 * POOL_HALO, wd), F32)],
        compiler_params=_cp("parallel", "parallel"),
        name="hyena_pre",
    )(u, u, u, sw, sb.reshape(1, wd))


FILT_TILE = 1024


def _filter_kernel(feat_ref, tdec_ref, w1_ref, b1_ref, f1_ref, w2_ref, b2_ref, f2_ref, w3_ref, dl_ref,
                   filt_ref, norm_ref, *, l, tr):
    i = pl.program_id(0)
    half = tr // 2
    h = jnp.sin(f1_ref[...] * (_dot(feat_ref[...], w1_ref[...], precision=HIGHEST) + b1_ref[...]))
    h = jnp.sin(f2_ref[...] * (_dot(h, w2_ref[...], precision=HIGHEST) + b2_ref[...]))
    h = _dot(h, w3_ref[...], precision=HIGHEST)
    dl = jnp.abs(dl_ref[...])
    total = jnp.zeros((1, HY_WIDTH), F32)
    for p in range(2):
        rows = slice(p * half, (p + 1) * half)
        hp = h[:, p * 2 * HY_WIDTH:(p + 1) * 2 * HY_WIDTH]
        decay = jnp.exp(-tdec_ref[rows] * dl)
        n = i * tr + p * half + lax.broadcasted_iota(jnp.int32, (half, 1), 0)
        f = jnp.where(n < l, hp[:, :HY_WIDTH], hp[:, HY_WIDTH:]) * decay
        f = jnp.where(n == l, 0.0, f)
        filt_ref[rows] = f
        total = total + jnp.sum(jnp.abs(f), axis=0, keepdims=True)

    @pl.when(i == 0)
    def _():
        norm_ref[...] = jnp.zeros_like(norm_ref)
    norm_ref[...] += total


def filter_tables(l):
    n = jnp.arange(2 * l)
    pos = jnp.where(n < l, n, 2 * l - n).astype(F32)[:, None]
    t = pos / max(l - 1, 1)
    w = 2.0 * math.pi * pos / l
    f = jnp.linspace(1e-4, FILT_BANDS - 1, FILT_BANDS, dtype=F32)[None, :]
    feat = jnp.concatenate([t, jnp.cos(f * w), -jnp.sin(f * w),
                            jnp.zeros((2 * l, LANES - FILT_EMB), F32)], axis=-1)
    half = min(FILT_TILE, 2 * l) // 2
    feat = feat.reshape(-1, 2, half, LANES).transpose(0, 2, 1, 3).reshape(l, 2 * LANES)
    return feat, t


def hyena_filter(l, tables, w1, b1, f1, w2, b2, f2, w3):
    feat, t = tables
    tr = min(FILT_TILE, 2 * l)
    hid = w2.shape[0]
    max_decay = math.log(DECAY_TARGET) / FAST_DECAY_PCT
    min_decay = math.log(DECAY_TARGET) / SLOW_DECAY_PCT
    deltas = jnp.linspace(min_decay, max_decay, HY_WIDTH, dtype=F32)[None, :]
    w1p = jnp.concatenate([w1, jnp.zeros((LANES - FILT_EMB, hid), F32)], 0)
    twice = lambda m: jax.scipy.linalg.block_diag(m, m)
    pair = lambda v: jnp.tile(v.reshape(1, hid), (1, 2))
    full = lambda a: pl.BlockSpec(a.shape, lambda i: (0,) * a.ndim)
    smalls = [twice(w1p), pair(b1), pair(f1), twice(w2), pair(b2), pair(f2), twice(w3), deltas]
    return pl.pallas_call(
        functools.partial(_filter_kernel, l=l, tr=tr),
        grid=(2 * l // tr,),
        in_specs=[pl.BlockSpec((tr // 2, 2 * LANES), lambda i: (i, 0)),
                  pl.BlockSpec((tr, 1), lambda i: (i, 0))] + [full(a) for a in smalls],
        out_specs=[pl.BlockSpec((tr, HY_WIDTH), lambda i: (i, 0)),
                   pl.BlockSpec((1, HY_WIDTH), lambda i: (0, 0))],
        out_shape=[jax.ShapeDtypeStruct((2 * l, HY_WIDTH), F32),
                   jax.ShapeDtypeStruct((1, HY_WIDTH), F32)],
        compiler_params=_cp("arbitrary"),
        name="hyena_filter",
    )(feat, t, *smalls)


def _dft_rows_kernel(f_ref, x_ref, o_ref):
    n1, nb, c = o_ref.shape[2:]
    xs = jnp.concatenate([x_ref[0, :, j, :] for j in range(nb)], axis=-1).astype(BF16)
    res = _dot(f_ref[...], xs)
    for p in range(2):
        for j in range(nb):
            o_ref[0, p, :, j, :] = res[p * n1:(p + 1) * n1, j * c:(j + 1) * c]


def dft_rows(fmat, x):
    b, k, n2, c = x.shape
    n1 = fmat.shape[0] // 2
    nb = min(DFT_COLS // c, n2)
    return pl.pallas_call(
        _dft_rows_kernel,
        grid=(b, n2 // nb),
        in_specs=[pl.BlockSpec((2 * n1, k), lambda bi, j: (0, 0)),
                  pl.BlockSpec((1, k, nb, c), lambda bi, j: (bi, 0, j, 0))],
        out_specs=pl.BlockSpec((1, 2, n1, nb, c), lambda bi, j: (bi, 0, 0, j, 0)),
        out_shape=jax.ShapeDtypeStruct((b, 2, n1, n2, c), F32),
        compiler_params=_cp("parallel", "parallel"),
        name="dft_rows",
    )(fmat, x)


SPEC_KB = 16


def _inner_dft(g, are, aim):
    n2 = are.shape[0]
    p1 = _dot(g, are)
    p2 = _dot(g, aim)
    return p1[:n2] - p2[n2:], p2[:n2] + p1[n2:]


def _spectrum_kernel(g_ref, a_ref, h_ref):
    for j in range(g_ref.shape[0]):
        h_ref[0, j], h_ref[1, j] = _inner_dft(g_ref[j], a_ref[0, 0, j].astype(BF16), a_ref[0, 1, j].astype(BF16))


def filter_spectrum(gs, a):
    _, _, n1, n2, c = a.shape
    kb = min(SPEC_KB, n1)
    return pl.pallas_call(
        _spectrum_kernel,
        grid=(n1 // kb,),
        in_specs=[pl.BlockSpec((kb, 2 * n2, n2), lambda k: (k, 0, 0)),
                  pl.BlockSpec((1, 2, kb, n2, c), lambda k: (0, 0, k, 0, 0))],
        out_specs=pl.BlockSpec((2, kb, n2, c), lambda k: (0, k, 0, 0)),
        out_shape=jax.ShapeDtypeStruct((2, n1, n2, c), F32),
        compiler_params=_cp("parallel"),
        name="filter_spectrum",
    )(gs, a)


def _spectral_kernel(g_ref, gt_ref, h_ref, a_ref, b_ref):
    n2 = a_ref.shape[3]
    for j in range(g_ref.shape[0]):
        xre, xim = _inner_dft(g_ref[j], a_ref[0, 0, j].astype(BF16), a_ref[0, 1, j].astype(BF16))
        hre, him = h_ref[0, j], h_ref[1, j]
        yre = (xre * hre - xim * him).astype(BF16)
        yim = (xre * him + xim * hre).astype(BF16)
        gt = gt_ref[j]
        q1 = _dot(gt, yre)
        q2 = _dot(gt, yim)
        b_ref[0, 0, j] = q1[:n2] + q2[n2:]
        b_ref[0, 1, j] = q2[:n2] - q1[n2:]


def spectral_multiply(gs, gts, hf, a):
    b, _, n1, n2, c = a.shape
    kb = min(SPEC_KB, n1)
    return pl.pallas_call(
        _spectral_kernel,
        grid=(n1 // kb, b),
        in_specs=[pl.BlockSpec((kb, 2 * n2, n2), lambda k, bi: (k, 0, 0)),
                  pl.BlockSpec((kb, 2 * n2, n2), lambda k, bi: (k, 0, 0)),
                  pl.BlockSpec((2, kb, n2, c), lambda k, bi: (0, k, 0, 0)),
                  pl.BlockSpec((1, 2, kb, n2, c), lambda k, bi: (bi, 0, k, 0, 0))],
        out_specs=pl.BlockSpec((1, 2, kb, n2, c), lambda k, bi: (bi, 0, k, 0, 0)),
        out_shape=jax.ShapeDtypeStruct(a.shape, F32),
        compiler_params=_cp("parallel", "arbitrary"),
        name="spectral_multiply",
    )(gs, gts, hf, a)


def _idft_rows_kernel(f_ref, bm_ref, x0_ref, z_ref, inv_ref, bias_ref, o_ref, *, inv_n):
    m, nb, c = o_ref.shape[1:]
    bflat = bm_ref[0].reshape(-1, c).astype(BF16)
    y = _dot(f_ref[...], bflat) * inv_n
    for j in range(nb):
        z = z_ref[0, :, j, :]
        o_ref[0, :, j, :] = x0_ref[0, :, j, :] * (y[j * m:(j + 1) * m] * inv_ref[...] + bias_ref[...] * z)


def idft_rows_gate(fmat, bm, x0, z, inv_norm, bias, n):
    b, _, n1, n2, c = bm.shape
    m = fmat.shape[0]
    nb = min(DFT_COLS // c, n2)
    spread = jnp.kron(fmat, jnp.eye(nb, dtype=fmat.dtype)).reshape(m, nb, 2 * n1 * nb)
    spread = spread.transpose(1, 0, 2).reshape(nb * m, 2 * n1 * nb)
    blk = pl.BlockSpec((1, m, nb, c), lambda bi, j: (bi, 0, j, 0))
    return pl.pallas_call(
        functools.partial(_idft_rows_kernel, inv_n=1.0 / n),
        grid=(b, n2 // nb),
        in_specs=[pl.BlockSpec(spread.shape, lambda bi, j: (0, 0), pipeline_mode=pl.Buffered(1)),
                  pl.BlockSpec((1, 2, n1, nb, c), lambda bi, j: (bi, 0, 0, j, 0)),
                  blk, blk,
                  pl.BlockSpec((1, c), lambda bi, j: (0, 0)),
                  pl.BlockSpec((1, c), lambda bi, j: (0, 0))],
        out_specs=blk,
        out_shape=jax.ShapeDtypeStruct((b, m, n2, c), F32),
        compiler_params=_cp("parallel", "parallel"),
        name="idft_rows_gate",
    )(spread, bm, x0, z, inv_norm, bias)


def dft_tables(l):
    n = 2 * l
    n2 = FFT_N2
    n1 = n // n2
    k1 = np.arange(n1)
    ang1 = 2.0 * np.pi * np.outer(k1, k1) / n1
    f1 = np.concatenate([np.cos(ang1), -np.sin(ang1)], 0)
    f3 = np.concatenate([np.cos(ang1), -np.sin(ang1)], 1)[: n1 // 2]
    m2 = np.arange(n2)
    tw = 2.0 * np.pi * np.outer(k1, m2) / n
    ang2 = 2.0 * np.pi * np.outer(m2, m2) / n2
    twr, twi = jnp.asarray(np.cos(tw), F32)[:, None, :], jnp.asarray(-np.sin(tw), F32)[:, None, :]
    f2r, f2i = jnp.asarray(np.cos(ang2), F32)[None], jnp.asarray(-np.sin(ang2), F32)[None]
    gre, gim = twr * f2r - twi * f2i, twr * f2i + twi * f2r
    gs = jnp.concatenate([gre, gim], 1).astype(BF16)
    gts = jnp.concatenate([gre.transpose(0, 2, 1), gim.transpose(0, 2, 1)], 1).astype(BF16)
    return dict(n=n, n1=n1, n2=n2, f1=jnp.asarray(f1, BF16), f3=jnp.asarray(f3, BF16), gs=gs, gts=gts)


def hyena_long_conv(z, x0, filt2, norm, bias, tabs):
    b, l, c = z.shape
    n, n1, n2 = tabs["n"], tabs["n1"], tabs["n2"]
    hf = filter_spectrum(tabs["gs"], dft_rows(tabs["f1"], filt2.reshape(1, n1, n2, c)))
    a = dft_rows(tabs["f1"][:, : n1 // 2], z.reshape(b, n1 // 2, n2, c))
    bm = spectral_multiply(tabs["gs"], tabs["gts"], hf, a)
    y = idft_rows_gate(tabs["f3"], bm, x0.reshape(b, n1 // 2, n2, c), z.reshape(b, n1 // 2, n2, c),
                       1.0 / norm, bias, n)
    return y.reshape(b, l, c)


def _small_conv_kernel(ff_ref, fi_ref, filt_ref, z_ref, x0_ref, inv_ref, bias_ref, o_ref, *, l):
    ff = ff_ref[...]
    hs = _dot(ff, filt_ref[...].astype(BF16))
    hre, him = hs[:2 * l], hs[2 * l:]
    z = z_ref[0]
    xs = _dot(ff[:, :l], z.astype(BF16))
    xre, xim = xs[:2 * l], xs[2 * l:]
    yre = (xre * hre - xim * him).astype(BF16)
    yim = (xre * him + xim * hre).astype(BF16)
    y = _dot(fi_ref[...], jnp.concatenate([yre, yim], 0)) * (0.5 / l)
    o_ref[0] = (x0_ref[0] * (y * inv_ref[...] + bias_ref[...] * z)).astype(o_ref.dtype)


def small_long_conv(z, x0, filt2, norm, bias):
    b, l, c = z.shape
    k = np.arange(2 * l)
    ang = 2.0 * np.pi * np.outer(k, k) / (2 * l)
    ff = jnp.asarray(np.concatenate([np.cos(ang), -np.sin(ang)], 0), BF16)
    fi = jnp.asarray(np.concatenate([np.cos(ang), -np.sin(ang)], 1)[:l], BF16)
    full = lambda a: pl.BlockSpec(a.shape, lambda bi: (0,) * a.ndim)
    row = pl.BlockSpec((1, l, c), lambda bi: (bi, 0, 0))
    inv = 1.0 / norm
    return pl.pallas_call(
        functools.partial(_small_conv_kernel, l=l),
        grid=(b,),
        in_specs=[full(ff), full(fi), full(filt2), row, row, full(inv), full(bias)],
        out_specs=row,
        out_shape=jax.ShapeDtypeStruct((b, l, c), BF16),
        compiler_params=_cp("parallel"),
        name="small_long_conv",
    )(ff, fi, filt2, z, x0, inv, bias)


def _pool_kernel(p_ref, pp_ref, pn_ref, w_ref, sc_ref, o_ref, buf_ref, *, tm, l):
    _fill_padded(buf_ref, p_ref, pp_ref, pn_ref, tm)
    t = pl.program_id(1) * tm + lax.broadcasted_iota(jnp.int32, (tm, LANES), 0)
    lane = lax.broadcasted_iota(jnp.int32, (tm, LANES), 1)
    first_group = lane < POOL_GROUP
    h0 = POOL_HALO

    def win(col, lo, hi):
        acc = buf_ref[h0 + lo:h0 + lo + tm, col]
        for j in range(lo + 1, hi):
            acc = acc + buf_ref[h0 + j:h0 + j + tm, col]
        return acc

    parts = []
    for half, (wa, wb) in enumerate(((2, 4), (8, 16))):
        col = slice(half * LANES, (half + 1) * LANES)
        sa = win(col, -wa // 2, wa // 2)
        sb = sa + win(col, -wb // 2, -wa // 2) + win(col, wa // 2, wb // 2)
        hw = jnp.where(first_group, wa // 2, wb // 2)
        cnt = jnp.minimum(t + hw, l) - jnp.maximum(t - hw, 0)
        mean = jnp.where(first_group, sa, sb) / cnt.astype(F32)
        parts.append(mean - buf_ref[h0:h0 + tm, col])
    d = jnp.concatenate(parts, axis=-1).astype(BF16)
    o_ref[0] = (_dot(d, w_ref[...]) * sc_ref[...]).astype(o_ref.dtype)


def pool_mix(p, pool_w, pool_scale):
    b, l, wd = p.shape
    tm = min(ROW_TILE, l)
    nblk = l // tm
    wblk = jax.scipy.linalg.block_diag(*[pool_w[g] for g in range(len(POOL_WINDOWS))]).astype(BF16)
    return pl.pallas_call(
        functools.partial(_pool_kernel, tm=tm, l=l),
        grid=(b, nblk),
        in_specs=_halo_specs(tm, wd, nblk) + [pl.BlockSpec((wd, wd), lambda bi, i: (0, 0)),
                                              pl.BlockSpec((1, wd), lambda bi, i: (0, 0))],
        out_specs=pl.BlockSpec((1, tm, wd), lambda bi, i: (bi, i, 0)),
        out_shape=jax.ShapeDtypeStruct((b, l, wd), BF16),
        scratch_shapes=[pltpu.VMEM((tm + 2 * POOL_HALO, wd), F32)],
        compiler_params=_cp("parallel", "parallel"),
        name="pool_mix",
    )(p, p, p, wblk, pool_scale.reshape(1, wd))


def _outproj_kernel(x_ref, hy_ref, pool_ref, att_ref, w_ref, gate_ref, o_ref):
    a, b2 = HY_WIDTH, HY_WIDTH + POOL_WIDTH
    y = (_dot(hy_ref[0].astype(BF16), w_ref[0:a]) + _dot(pool_ref[0], w_ref[a:b2]) + _dot(att_ref[0], w_ref[b2:]))
    o_ref[0] = x_ref[0] + gate_ref[0] * y


def out_projection(x, y_hy, y_pool, y_att, w_out, gate):
    b, l, d = x.shape
    tm = min(ROW_TILE, l)
    row = lambda bi, i: (bi, i, 0)
    return pl.pallas_call(
        _outproj_kernel,
        grid=(b, l // tm),
        in_specs=[pl.BlockSpec((1, tm, d), row),
                  pl.BlockSpec((1, tm, y_hy.shape[2]), row),
                  pl.BlockSpec((1, tm, y_pool.shape[2]), row),
                  pl.BlockSpec((1, tm, y_att.shape[2]), row),
                  pl.BlockSpec(w_out.shape, lambda bi, i: (0, 0)),
                  pl.BlockSpec((1, 1, d), lambda bi, i: (bi, 0, 0))],
        out_specs=pl.BlockSpec((1, tm, d), row),
        out_shape=jax.ShapeDtypeStruct(x.shape, F32),
        compiler_params=_cp("parallel", "parallel"),
        name="out_projection",
    )(x, y_hy, y_pool, y_att, w_out, gate)


FF_CHUNK = 256
MOE_TILE = 1024
MOE_ROWS = (128, 192, 256, 320, 384)


def _swiglu_acc(h, wg_ref, wu_ref, wd_ref, lead):
    ff = wg_ref.shape[-1]
    acc = None
    for f0 in range(0, ff, FF_CHUNK):
        cs = slice(f0, min(f0 + FF_CHUNK, ff))
        g = _dot(h, wg_ref[lead + (slice(None), cs)])
        u = _dot(h, wu_ref[lead + (slice(None), cs)])
        a = (g * jax.nn.sigmoid(g) * u).astype(BF16)
        part = _dot(a, wd_ref[lead + (cs, slice(None))])
        acc = part if acc is None else acc + part
    return acc


def _ffn_kernel(x_ref, g_ref, sh_ref, sc_ref, gate_ref, wg_ref, wu_ref, wd_ref, o_ref):
    x = x_ref[0]
    h = _normmod(x, g_ref[...], sh_ref[0], sc_ref[0]).astype(BF16)
    o_ref[0] = x + gate_ref[0] * _swiglu_acc(h, wg_ref, wu_ref, wd_ref, ())


def ffn_dense(x, g, shift, scale, gate, wg, wu, wd):
    b, l, d = x.shape
    tm = min(ROW_TILE, l)
    row = lambda bi, i: (bi, i, 0)
    vec = lambda bi, i: (bi, 0, 0)
    wspec = lambda w: pl.BlockSpec(w.shape, lambda bi, i: (0, 0), pipeline_mode=pl.Buffered(1))
    return pl.pallas_call(
        _ffn_kernel,
        grid=(b, l // tm),
        in_specs=[pl.BlockSpec((1, tm, d), row),
                  pl.BlockSpec((1, d), lambda bi, i: (0, 0)),
                  pl.BlockSpec((1, 1, d), vec), pl.BlockSpec((1, 1, d), vec), pl.BlockSpec((1, 1, d), vec),
                  wspec(wg), wspec(wu), wspec(wd)],
        out_specs=pl.BlockSpec((1, tm, d), row),
        out_shape=jax.ShapeDtypeStruct(x.shape, F32),
        compiler_params=_cp("parallel", "parallel"),
        name="ffn_dense",
    )(x, g, shift, scale, gate, wg, wu, wd)


def _router_kernel(x_ref, g_ref, sh_ref, sc_ref, wr_ref, tri_ref, comb_ref, rank_ref):
    h = _normmod(x_ref[0], g_ref[...], sh_ref[0], sc_ref[0])
    logits = _dot(h, wr_ref[...], precision=HIGHEST)
    lane = lax.broadcasted_iota(jnp.int32, logits.shape, 1)
    neg = jnp.float32(-jnp.inf)
    lg = jnp.where(lane < N_EXPERTS, logits, neg)
    m1 = jnp.max(lg, axis=-1, keepdims=True)
    i1 = jnp.min(jnp.where(lg == m1, lane, LANES), axis=-1, keepdims=True)
    lg2 = jnp.where(lane == i1, neg, lg)
    m2 = jnp.max(lg2, axis=-1, keepdims=True)
    i2 = jnp.min(jnp.where(lg2 == m2, lane, LANES), axis=-1, keepdims=True)
    e = jnp.exp(m2 - m1)
    g1 = 1.0 / (1.0 + e)
    comb_ref[0] = jnp.where(lane == i1, g1, jnp.where(lane == i2, e * g1, 0.0))
    routed = (lane == i1) | (lane == i2)
    before = _dot(tri_ref[...], jnp.where(routed, 1.0, 0.0).astype(BF16))
    rank_ref[0] = jnp.where(routed, before, -1.0)


def moe_route(x, g, shift, scale, w_router):
    b, l, d = x.shape
    tm = min(MOE_TILE, l)
    wr = jnp.concatenate([w_router, jnp.zeros((d, LANES - N_EXPERTS), F32)], -1)
    tri = jnp.asarray(np.tril(np.ones((tm, tm), np.float32), -1), BF16)
    row = lambda bi, i: (bi, i, 0)
    vec = lambda bi, i: (bi, 0, 0)
    return pl.pallas_call(
        _router_kernel,
        grid=(b, l // tm),
        in_specs=[pl.BlockSpec((1, tm, d), row),
                  pl.BlockSpec((1, d), lambda bi, i: (0, 0)),
                  pl.BlockSpec((1, 1, d), vec), pl.BlockSpec((1, 1, d), vec),
                  pl.BlockSpec((d, LANES), lambda bi, i: (0, 0)),
                  pl.BlockSpec((tm, tm), lambda bi, i: (0, 0))],
        out_specs=[pl.BlockSpec((1, tm, LANES), row)] * 2,
        out_shape=[jax.ShapeDtypeStruct((b, l, LANES), F32)] * 2,
        compiler_params=_cp("parallel", "parallel"),
        name="moe_router",
    )(x, g, shift, scale, wr, tri)


def _moe_kernel(x_ref, g_ref, sh_ref, sc_ref, gate_ref, comb_ref, wg_ref, wu_ref, wd_ref, o_ref, acc_ref):
    e = pl.program_id(2)
    x = x_ref[0]
    h = _normmod(x, g_ref[...], sh_ref[0], sc_ref[0]).astype(BF16)
    comb = comb_ref[0]
    lane = lax.broadcasted_iota(jnp.int32, comb.shape, 1)
    ce = jnp.sum(jnp.where(lane == e, comb, 0.0), axis=-1, keepdims=True)
    y = ce * _swiglu_acc(h, wg_ref, wu_ref, wd_ref, (0, 0))

    @pl.when(e == 0)
    def _():
        acc_ref[...] = y

    @pl.when(e > 0)
    def _():
        acc_ref[...] += y

    @pl.when(e == N_EXPERTS - 1)
    def _():
        o_ref[0] = x + gate_ref[0] * acc_ref[...]


def moe_dense(x, g, shift, scale, gate, comb, wg, wu, wd, li):
    b, l, d = x.shape
    tm = min(ROW_TILE, l)
    ff = wg.shape[-1]
    row = lambda bi, i, e: (bi, i, 0)
    vec = lambda bi, i, e: (bi, 0, 0)
    return pl.pallas_call(
        _moe_kernel,
        grid=(b, l // tm, N_EXPERTS),
        in_specs=[pl.BlockSpec((1, tm, d), row),
                  pl.BlockSpec((1, d), lambda bi, i, e: (0, 0)),
                  pl.BlockSpec((1, 1, d), vec), pl.BlockSpec((1, 1, d), vec), pl.BlockSpec((1, 1, d), vec),
                  pl.BlockSpec((1, tm, LANES), row),
                  pl.BlockSpec((1, 1, d, ff), lambda bi, i, e: (li, e, 0, 0)),
                  pl.BlockSpec((1, 1, d, ff), lambda bi, i, e: (li, e, 0, 0)),
                  pl.BlockSpec((1, 1, ff, d), lambda bi, i, e: (li, e, 0, 0))],
        out_specs=pl.BlockSpec((1, tm, d), row),
        out_shape=jax.ShapeDtypeStruct(x.shape, F32),
        scratch_shapes=[pltpu.VMEM((tm, d), F32)],
        compiler_params=_cp("parallel", "parallel", "arbitrary"),
        name="moe_dense",
    )(x, g, shift, scale, gate, comb, wg, wu, wd)


def _moe_sparse_kernel(cnt_ref, x_ref, g_ref, sh_ref, sc_ref, gate_ref, comb_ref, rank_ref, rankt_ref,
                       wg_ref, wu_ref, wd_ref, *rest, t, sizes):
    fg_ref = rest[0] if len(rest) == 3 else None
    o_ref, h_scr = rest[-2:]
    i, e = pl.program_id(0), pl.program_id(1)
    cnt = cnt_ref[i * N_EXPERTS + e]

    @pl.when(e == 0)
    def _():
        x = x_ref[0]
        h_scr[...] = _normmod(x, g_ref[...], sh_ref[0], sc_ref[0]).astype(BF16)
        o_ref[0] = x

    rrow = rankt_ref[0, pl.ds(e, 1), :]
    lane = lax.broadcasted_iota(jnp.int32, (t, LANES), 1)
    rcol = jnp.sum(jnp.where(lane == e, rank_ref[0], 0.0), axis=-1, keepdims=True)
    ccol = jnp.sum(jnp.where(lane == e, comb_ref[0], 0.0), axis=-1, keepdims=True)

    def block(base, r):
        rid = (base + lax.broadcasted_iota(jnp.int32, (r, 1), 0)).astype(F32)
        onehot = jnp.where(rrow == rid, 1.0, 0.0).astype(BF16)
        xg = _dot(onehot, h_scr[...]).astype(BF16)
        y = _swiglu_acc(xg, wg_ref, wu_ref, wd_ref, (0, 0)).astype(BF16)
        cid = (base + lax.broadcasted_iota(jnp.int32, (1, r), 1)).astype(F32)
        onehot_t = jnp.where(rcol == cid, 1.0, 0.0).astype(BF16)
        o_ref[0] += (gate_ref[0] * ccol) * _dot(onehot_t, y)

    big = sizes[-1]
    nfull = cnt // big

    def full(k, carry):
        block(k * big, big)
        return carry
    lax.fori_loop(0, nfull, full, 0)
    base = nfull * big
    rem = cnt - base
    for lo, r in zip((0,) + sizes[:-1], sizes):
        @pl.when((rem > lo) & (rem <= r))
        def _(r=r):
            block(base, r)

    if fg_ref is not None:
        @pl.when(e == N_EXPERTS - 1)
        def _():
            o = o_ref[0]
            o_ref[0] = o * lax.rsqrt(jnp.mean(o * o, axis=-1, keepdims=True) + EPS) * fg_ref[...]


def moe_sparse(x, g, shift, scale, gate, comb, rank, wg, wu, wd, li, final_g=None):
    b, l, d = x.shape
    t = MOE_TILE
    tpb = l // t
    n_tiles = b * tpb
    ff = wg.shape[-1]
    rk = rank[..., :N_EXPERTS].reshape(n_tiles, t, N_EXPERTS)
    rank_t = rk.transpose(0, 2, 1)
    cnt = jnp.sum(rk >= 0, axis=1).astype(jnp.int32).reshape(n_tiles * N_EXPERTS)
    row = lambda i, e, c: (i // tpb, i % tpb, 0)
    vec = lambda i, e, c: (i // tpb, 0, 0)
    grid_spec = pltpu.PrefetchScalarGridSpec(
        num_scalar_prefetch=1,
        grid=(n_tiles, N_EXPERTS),
        in_specs=[pl.BlockSpec((1, t, d), row, pipeline_mode=pl.Buffered(1)),
                  pl.BlockSpec((1, d), lambda i, e, c: (0, 0)),
                  pl.BlockSpec((1, 1, d), vec), pl.BlockSpec((1, 1, d), vec), pl.BlockSpec((1, 1, d), vec),
                  pl.BlockSpec((1, t, LANES), row, pipeline_mode=pl.Buffered(1)),
                  pl.BlockSpec((1, t, LANES), row, pipeline_mode=pl.Buffered(1)),
                  pl.BlockSpec((1, N_EXPERTS, t), lambda i, e, c: (i, 0, 0)),
                  pl.BlockSpec((1, 1, d, ff), lambda i, e, c: (li, e, 0, 0)),
                  pl.BlockSpec((1, 1, d, ff), lambda i, e, c: (li, e, 0, 0)),
                  pl.BlockSpec((1, 1, ff, d), lambda i, e, c: (li, e, 0, 0))]
        + ([] if final_g is None else [pl.BlockSpec((1, d), lambda i, e, c: (0, 0))]),
        out_specs=pl.BlockSpec((1, t, d), row),
        scratch_shapes=[pltpu.VMEM((t, d), BF16)],
    )
    return pl.pallas_call(
        functools.partial(_moe_sparse_kernel, t=t, sizes=MOE_ROWS),
        grid_spec=grid_spec,
        out_shape=jax.ShapeDtypeStruct(x.shape, F32),
        compiler_params=_cp("parallel", "arbitrary"),
        name="moe_sparse",
    )(cnt, x, g, shift, scale, gate, comb, rank, rank_t, wg, wu, wd, *(() if final_g is None else (final_g,)))


def _final_kernel(x_ref, g_ref, o_ref):
    x = x_ref[0]
    o_ref[0] = x * lax.rsqrt(jnp.mean(x * x, axis=-1, keepdims=True) + EPS) * g_ref[...]


def final_norm(x, g):
    b, l, d = x.shape
    tm = min(2 * ROW_TILE, l)
    return pl.pallas_call(
        _final_kernel,
        grid=(b, l // tm),
        in_specs=[pl.BlockSpec((1, tm, d), lambda bi, i: (bi, i, 0)),
                  pl.BlockSpec((1, d), lambda bi, i: (0, 0))],
        out_specs=pl.BlockSpec((1, tm, d), lambda bi, i: (bi, i, 0)),
        out_shape=jax.ShapeDtypeStruct(x.shape, F32),
        compiler_params=_cp("parallel", "parallel"),
        name="final_norm",
    )(x, g)


def kernel(x, c, ctx, c_ctx, norm1_g, norm2_g, w_ada, b_ada, w_in, hy_short_w, hy_short_b, filt_w1, filt_b1, filt_freq1, filt_w2, filt_b2, filt_freq2, filt_w3, hy_bias, pool_w, pool_scale, q_norm_g, w_uq, kv_norm_g, w_ukv, w_out, ffn_wg, ffn_wu, ffn_wd, moe_router, moe_wg, moe_wu, moe_wd, final_g):
    b, l, d = x.shape
    lc = ctx.shape[1]
    depth = w_ada.shape[0]

    cc = jnp.zeros((8, d), F32).at[:b].set(c).at[b].set(c_ctx)
    mods = ada_vectors(cc, w_ada, b_ada)

    rope_l = rope_tables(l, True)
    rope_c = rope_tables(lc, False)
    ftab_l, ftab_c = filter_tables(l), filter_tables(lc)
    dtabs = dft_tables(l)
    moe_w = (moe_wg.astype(BF16), moe_wu.astype(BF16), moe_wd.astype(BF16))

    xl, xc = x, ctx
    for layer in range(depth):
        last = layer == depth - 1
        mod = mods[layer].reshape(8, 6, 1, d)
        mod_l = [mod[:b, j] for j in range(6)]
        mod_c = [jnp.broadcast_to(mod[b:b + 1, j], (b, 1, d)) for j in range(6)]
        g1 = norm1_g[layer].reshape(1, d)
        g2 = norm2_g[layer].reshape(1, d)
        w_in_l = w_in[layer].astype(BF16)
        w_out_l = w_out[layer].astype(BF16)
        mw = mla_weights(q_norm_g[layer], w_uq[layer], kv_norm_g[layer], w_ukv[layer])
        fw = (filt_w1[layer], filt_b1[layer], filt_freq1[layer], filt_w2[layer], filt_b2[layer],
              filt_freq2[layer], filt_w3[layer])
        bias = hy_bias[layer].reshape(1, HY_WIDTH)

        def channel_mix(xs, md, final=False):
            i = layer // 2
            fg = final_g.reshape(1, d)
            if layer % 2 == 1 and xs.shape[1] % MOE_TILE == 0:
                comb, rank = moe_route(xs, g2, md[3], md[4], moe_router[i])
                return moe_sparse(xs, g2, md[3], md[4], md[5], comb, rank, *moe_w, i, fg if final else None)
            if layer % 2 == 0:
                out = ffn_dense(xs, g2, md[3], md[4], md[5], ffn_wg[i].astype(BF16),
                                ffn_wu[i].astype(BF16), ffn_wd[i].astype(BF16))
            else:
                comb, _ = moe_route(xs, g2, md[3], md[4], moe_router[i])
                out = moe_dense(xs, g2, md[3], md[4], md[5], comb, *moe_w, i)
            return final_norm(out, fg) if final else out

        hy_l, pool_l, q_l, kt_l, v_l = in_projection(xl, g1, mod_l[0], mod_l[1], w_in_l, *rope_l, mw)
        hy_c, pool_c, q_c, kt_c, v_c = in_projection(xc, g1, mod_c[0], mod_c[1], w_in_l, *rope_c, mw)

        att_l = attention(q_l, kt_c, v_c, kt_l, v_l)
        z_l, x0_l = hyena_pre(hy_l, hy_short_w[layer], hy_short_b[layer])
        filt_l, nrm_l = hyena_filter(l, ftab_l, *fw)
        yhy_l = hyena_long_conv(z_l, x0_l, filt_l, nrm_l, bias, dtabs)
        ypool_l = pool_mix(pool_l, pool_w[layer], pool_scale[layer])
        xl = out_projection(xl, yhy_l, ypool_l, att_l, w_out_l, mod_l[2])

        if not last:
            att_c = attention(q_c, kt_c, v_c)
            z_c, x0_c = hyena_pre(hy_c, hy_short_w[layer], hy_short_b[layer])
            filt_c, nrm_c = hyena_filter(lc, ftab_c, *fw)
            yhy_c = small_long_conv(z_c, x0_c, filt_c, nrm_c, bias)
            ypool_c = pool_mix(pool_c, pool_w[layer], pool_scale[layer])
            xc = out_projection(xc, yhy_c, ypool_c, att_c, w_out_l, mod_c[2])
            xc = channel_mix(xc, mod_c)

        xl = channel_mix(xl, mod_l, final=last)
    return xl
```

```python
import functools
import math

import jax
import jax.numpy as jnp
import numpy as np
from jax import lax
from jax.experimental import pallas as pl
from jax.experimental.pallas import tpu as pltpu

F32 = jnp.float32
BF16 = jnp.bfloat16
HIGHEST = lax.Precision.HIGHEST

EPS = 1e-6
GRID_W = 64
HY_WIDTH = 256
POOL_WINDOWS = (2, 4, 8, 16)
POOL_WIDTH = 256
POOL_GROUP = 64
POOL_HALO = 8
MLA_HEADS = 8
QK_NOPE = 64
QK_ROPE = 32
V_DIM = 64
Q_LORA = 384
KV_LORA = 256
HEAD_PAD = 128
SM_SCALE = (QK_NOPE + QK_ROPE) ** -0.5
ROPE_THETA = 10000.0
FILT_EMB = 33
FILT_BANDS = 16
DECAY_TARGET = 1e-2
FAST_DECAY_PCT = 0.3
SLOW_DECAY_PCT = 1.5
N_EXPERTS = 8
LANES = 128
FFT_N2 = 128
VMEM_LIMIT = 56 * 2 ** 20
ROW_TILE = 1024
DFT_COLS = 2048
ADA_COLS = 1536


def _cp(*sem):
    return pltpu.CompilerParams(dimension_semantics=sem, vmem_limit_bytes=VMEM_LIMIT)


def _dot(a, b, **kw):
    return jnp.dot(a, b, preferred_element_type=F32, **kw)


def _normmod(x, g, shift, scale):
    ms = jnp.mean(x * x, axis=-1, keepdims=True)
    return x * lax.rsqrt(ms + EPS) * (g * (1.0 + scale)) + shift


def _ada_kernel(c_ref, w_ref, b_ref, o_ref):
    c = c_ref[...]
    s = c * jax.nn.sigmoid(c)
    o_ref[0] = _dot(s, w_ref[0], precision=HIGHEST) + b_ref[0]


def ada_vectors(cc, w_ada, b_ada):
    depth, d, n = w_ada.shape
    tn = ADA_COLS
    return pl.pallas_call(
        _ada_kernel,
        grid=(depth, n // tn),
        in_specs=[pl.BlockSpec((8, d), lambda l, j: (0, 0)),
                  pl.BlockSpec((1, d, tn), lambda l, j: (l, 0, j)),
                  pl.BlockSpec((1, 1, tn), lambda l, j: (l, 0, j))],
        out_specs=pl.BlockSpec((1, 8, tn), lambda l, j: (l, 0, j)),
        out_shape=jax.ShapeDtypeStruct((depth, 8, n), F32),
        compiler_params=_cp("parallel", "parallel"),
        name="ada_vectors",
    )(cc, w_ada, b_ada.reshape(depth, 1, n))


IN_HY, IN_POOL = 3 * HY_WIDTH, POOL_WIDTH
IN_OFFS = tuple(int(v) for v in np.cumsum((0, IN_HY, IN_POOL, Q_LORA, KV_LORA, QK_ROPE)))


def _inproj_kernel(x_ref, g_ref, sh_ref, sc_ref, w_ref, c_ref, s_ref, wqa_ref, wqb_ref, wk_ref, wv_ref,
                   pa_ref, pb_ref, hy_ref, pool_ref, q_ref, kt_ref, v_ref):
    h = _normmod(x_ref[0], g_ref[...], sh_ref[0], sc_ref[0]).astype(BF16)
    y = _dot(h, w_ref[...])
    o = IN_OFFS
    hy_ref[0] = y[:, o[0]:o[1]]
    pool_ref[0] = y[:, o[1]:o[2]]
    cq, ckv = y[:, o[2]:o[3]], y[:, o[3]:o[4]]
    kpe = y[:, o[4]:o[5]].astype(BF16)
    cqn = (cq * lax.rsqrt(jnp.mean(cq * cq, axis=-1, keepdims=True) + EPS)).astype(BF16)
    ckvn = (ckv * lax.rsqrt(jnp.mean(ckv * ckv, axis=-1, keepdims=True) + EPS)).astype(BF16)
    cs, sn = c_ref[...], s_ref[...]
    qa = _dot(cqn, wqa_ref[...])
    qb = _dot(cqn, wqb_ref[...])
    kn = _dot(ckvn, wk_ref[...])
    vv = _dot(ckvn, wv_ref[...])
    kpa = _dot(kpe, pa_ref[...])
    kpb = _dot(kpe, pb_ref[...])
    kpe_rot = kpa * cs + kpb * sn
    lane = lax.broadcasted_iota(jnp.int32, (1, HEAD_PAD), 1)
    for hd in range(MLA_HEADS):
        sl = slice(hd * HEAD_PAD, (hd + 1) * HEAD_PAD)
        q_ref[0, hd] = (qa[:, sl] * cs + qb[:, sl] * sn).astype(BF16)
        kt_ref[0, hd] = (kn[:, sl] + kpe_rot).T.astype(BF16)
        one_lane = V_DIM if hd % 2 == 0 else 0
        v_ref[0, hd] = jnp.where(lane == one_lane, 1.0, vv[:, sl]).astype(BF16)


def in_projection(x, g, shift, scale, w_in, ctab, stab, wts):
    b, l, d = x.shape
    tm = min(ROW_TILE, l)
    hw = MLA_HEADS
    row = lambda bi, i: (bi, i, 0)
    vec = lambda bi, i: (bi, 0, 0)
    full = lambda a: pl.BlockSpec(a.shape, lambda bi, i: (0,) * a.ndim)
    return pl.pallas_call(
        _inproj_kernel,
        grid=(b, l // tm),
        in_specs=[pl.BlockSpec((1, tm, d), row),
                  pl.BlockSpec((1, d), lambda bi, i: (0, 0)),
                  pl.BlockSpec((1, 1, d), vec),
                  pl.BlockSpec((1, 1, d), vec),
                  full(w_in),
                  pl.BlockSpec((tm, HEAD_PAD), lambda bi, i: (i, 0)),
                  pl.BlockSpec((tm, HEAD_PAD), lambda bi, i: (i, 0))] + [full(w) for w in wts],
        out_specs=[pl.BlockSpec((1, tm, IN_HY), row),
                   pl.BlockSpec((1, tm, IN_POOL), row),
                   pl.BlockSpec((1, hw, tm, HEAD_PAD), lambda bi, i: (bi, 0, i, 0)),
                   pl.BlockSpec((1, hw, HEAD_PAD, tm), lambda bi, i: (bi, 0, 0, i)),
                   pl.BlockSpec((1, hw, tm, HEAD_PAD), lambda bi, i: (bi, 0, i, 0))],
        out_shape=[jax.ShapeDtypeStruct((b, l, IN_HY), F32),
                   jax.ShapeDtypeStruct((b, l, IN_POOL), F32),
                   jax.ShapeDtypeStruct((b, hw, l, HEAD_PAD), BF16),
                   jax.ShapeDtypeStruct((b, hw, HEAD_PAD, l), BF16),
                   jax.ShapeDtypeStruct((b, hw, l, HEAD_PAD), BF16)],
        compiler_params=_cp("parallel", "parallel"),
        name="in_projection",
    )(x, g, shift, scale, w_in, ctab, stab, *wts)


def mla_weights(q_norm_g, w_uq, kv_norm_g, w_ukv):
    hw, dq = MLA_HEADS, QK_NOPE + QK_ROPE
    half = QK_ROPE // 2
    wq = (w_uq * q_norm_g[:, None] * (SM_SCALE * math.log2(math.e))).reshape(Q_LORA, hw, dq)
    pad = jnp.zeros((Q_LORA, hw, HEAD_PAD - dq), F32)
    wqa = jnp.concatenate([wq, pad], -1)
    swap = jnp.concatenate([wq[..., QK_NOPE + half:], wq[..., QK_NOPE:QK_NOPE + half]], -1)
    wqb = jnp.concatenate([jnp.zeros((Q_LORA, hw, QK_NOPE), F32), swap, pad], -1)
    wkv = (w_ukv * kv_norm_g[:, None]).reshape(KV_LORA, hw, QK_NOPE + V_DIM)
    z64 = jnp.zeros((KV_LORA, hw, 64), F32)
    wk = jnp.concatenate([wkv[..., :QK_NOPE], z64], -1)
    v = wkv[..., QK_NOPE:]
    wv = jnp.where((jnp.arange(hw) % 2 == 0)[None, :, None],
                   jnp.concatenate([v, z64], -1), jnp.concatenate([z64, v], -1))
    eye = np.eye(QK_ROPE, dtype=np.float32)
    pa = np.zeros((QK_ROPE, HEAD_PAD), np.float32)
    pa[:, QK_NOPE:QK_NOPE + QK_ROPE] = eye
    pb = np.zeros((QK_ROPE, HEAD_PAD), np.float32)
    pb[:, QK_NOPE:QK_NOPE + QK_ROPE] = np.concatenate([eye[:, half:], eye[:, :half]], 1)
    flat = lambda w: w.reshape(w.shape[0], hw * HEAD_PAD).astype(BF16)
    return (flat(wqa), flat(wqb), flat(wk), flat(wv), jnp.asarray(pa, BF16), jnp.asarray(pb, BF16))


def rope_tables(l, with_rope):
    ctab = np.zeros((l, HEAD_PAD), np.float32)
    stab = np.zeros((l, HEAD_PAD), np.float32)
    ctab[:, :QK_NOPE + QK_ROPE] = 1.0
    if not with_rope:
        return jnp.asarray(ctab), jnp.asarray(stab)
    n_freq = QK_ROPE // 4
    inv_freq = 1.0 / (ROPE_THETA ** (jnp.arange(n_freq, dtype=F32) / n_freq))
    rows = l // GRID_W
    row = jnp.repeat(jnp.arange(rows, dtype=F32), GRID_W)
    col = jnp.tile(jnp.arange(GRID_W, dtype=F32), rows)
    ang = jnp.concatenate([row[:, None] * inv_freq, col[:, None] * inv_freq], axis=-1)
    cos, sin = jnp.cos(ang), jnp.sin(ang)
    one = jnp.ones((l, QK_NOPE), F32)
    zero = jnp.zeros((l, QK_NOPE), F32)
    tail = jnp.zeros((l, HEAD_PAD - QK_NOPE - QK_ROPE), F32)
    return (jnp.concatenate([one, cos, cos, tail], -1), jnp.concatenate([zero, -sin, sin, tail], -1))


ATT_TQ = 1024
ATT_RB = 1024
ATT_CK = 2048


def _row_max(s):
    mp = s[:, 0:LANES]
    for j in range(1, s.shape[1] // LANES):
        mp = jnp.maximum(mp, s[:, j * LANES:(j + 1) * LANES])
    return jnp.max(mp, axis=-1, keepdims=True)


def _attn_kernel(q_ref, kct_ref, vc_ref, *rest, n_chunks, ck, tq):
    if n_chunks:
        kt_ref, v_ref, o_ref = rest
    else:
        (o_ref,) = rest
    rb = min(ATT_RB, tq)
    chains = [(hh, r) for hh in range(2) for r in range(tq // rb)]
    lane = lax.broadcasted_iota(jnp.int32, (1, HEAD_PAD), 1)

    def q_rows(hh, r):
        return q_ref[0, hh, r * rb:(r + 1) * rb, :]

    state = []
    for hh, r in chains:
        s = _dot(q_rows(hh, r), kct_ref[0, hh])
        m = _row_max(s)
        p = jnp.exp2(s - m).astype(BF16)
        state += [m, _dot(p, vc_ref[0, hh])]

    if n_chunks:
        def body(c, state):
            off = pl.multiple_of(c * ck, ck)
            out = []
            for i, (hh, r) in enumerate(chains):
                m, acc = state[2 * i], state[2 * i + 1]
                s = _dot(q_rows(hh, r), kt_ref[0, hh, :, pl.ds(off, ck)])
                m_new = jnp.maximum(m, _row_max(s))
                alpha = jnp.exp2(m - m_new)
                p = jnp.exp2(s - m_new).astype(BF16)
                out += [m_new, acc * alpha + _dot(p, v_ref[0, hh, pl.ds(off, ck), :])]
            return tuple(out)
        state = lax.fori_loop(0, n_chunks, body, tuple(state), unroll=8 if n_chunks % 8 == 0 else 1)

    for r in range(tq // rb):
        outs = []
        for hh in range(2):
            acc = state[2 * chains.index((hh, r)) + 1]
            one_lane = V_DIM if hh == 0 else 0
            denom = jnp.sum(jnp.where(lane == one_lane, acc, 0.0), axis=-1, keepdims=True)
            outs.append(acc / denom)
        o_ref[0, r * rb:(r + 1) * rb, :] = jnp.where(lane < V_DIM, outs[0], outs[1]).astype(o_ref.dtype)


def attention(q, kct, vc, kt=None, v=None):
    b, hw, lq, _ = q.shape
    lc = vc.shape[2]
    tq = min(ATT_TQ, lq)
    ck = ATT_CK if kt is None else min(ATT_CK, kt.shape[3])
    n_chunks = 0 if kt is None else kt.shape[3] // ck
    in_specs = [pl.BlockSpec((1, 2, tq, HEAD_PAD), lambda bi, hp, i: (bi, hp, i, 0)),
                pl.BlockSpec((1, 2, HEAD_PAD, lc), lambda bi, hp, i: (bi, hp, 0, 0)),
                pl.BlockSpec((1, 2, lc, HEAD_PAD), lambda bi, hp, i: (bi, hp, 0, 0))]
    args = [q, kct, vc]
    if n_chunks:
        l = kt.shape[3]
        in_specs += [pl.BlockSpec((1, 2, HEAD_PAD, l), lambda bi, hp, i: (bi, hp, 0, 0)),
                     pl.BlockSpec((1, 2, l, HEAD_PAD), lambda bi, hp, i: (bi, hp, 0, 0))]
        args += [kt, v]
    return pl.pallas_call(
        functools.partial(_attn_kernel, n_chunks=n_chunks, ck=ck, tq=tq),
        grid=(b, hw // 2, lq // tq),
        in_specs=in_specs,
        out_specs=pl.BlockSpec((1, tq, HEAD_PAD), lambda bi, hp, i: (bi, i, hp)),
        out_shape=jax.ShapeDtypeStruct((b, lq, hw * V_DIM), BF16),
        compiler_params=_cp("parallel", "parallel", "arbitrary"),
        name="attention",
    )(*args)


def _halo_specs(tm, width, nblk):
    r = tm // POOL_HALO
    last = nblk * r - 1
    return [pl.BlockSpec((1, tm, width), lambda bi, i: (bi, i, 0)),
            pl.BlockSpec((1, POOL_HALO, width), lambda bi, i: (bi, jnp.maximum(i * r - 1, 0), 0)),
            pl.BlockSpec((1, POOL_HALO, width), lambda bi, i: (bi, jnp.minimum((i + 1) * r, last), 0))]


def _fill_padded(buf_ref, cur_ref, prev_ref, next_ref, tm):
    i = pl.program_id(1)
    first = i == 0
    last = i == pl.num_programs(1) - 1
    buf_ref[0:POOL_HALO] = jnp.where(first, 0.0, prev_ref[0])
    buf_ref[POOL_HALO:POOL_HALO + tm] = cur_ref[0]
    buf_ref[POOL_HALO + tm:] = jnp.where(last, 0.0, next_ref[0])


def _hyena_pre_kernel(u_ref, up_ref, un_ref, w_ref, b_ref, z_ref, x0_ref, buf_ref, *, tm):
    _fill_padded(buf_ref, u_ref, up_ref, un_ref, tm)
    w = w_ref[...]
    uc = (buf_ref[POOL_HALO - 1:POOL_HALO - 1 + tm] * w[0:1] + buf_ref[POOL_HALO:POOL_HALO + tm] * w[1:2]
          + buf_ref[POOL_HALO + 1:POOL_HALO + 1 + tm] * w[2:3] + b_ref[...])
    x0_ref[0] = uc[:, :HY_WIDTH]
    z_ref[0] = uc[:, HY_WIDTH:2 * HY_WIDTH] * uc[:, 2 * HY_WIDTH:]


def hyena_pre(u, sw, sb):
    b, l, wd = u.shape
    tm = min(ROW_TILE, l)
    nblk = l // tm
    return pl.pallas_call(
        functools.partial(_hyena_pre_kernel, tm=tm),
        grid=(b, nblk),
        in_specs=_halo_specs(tm, wd, nblk) + [pl.BlockSpec((3, wd), lambda bi, i: (0, 0)),
                                              pl.BlockSpec((1, wd), lambda bi, i: (0, 0))],
        out_specs=[pl.BlockSpec((1, tm, HY_WIDTH), lambda bi, i: (bi, i, 0))] * 2,
        out_shape=[jax.ShapeDtypeStruct((b, l, HY_WIDTH), F32)] * 2,
        scratch_shapes=[pltpu.VMEM((tm + 2 * POOL_HALO, wd), F32)],
        compiler_params=_cp("parallel", "parallel"),
        name="hyena_pre",
    )(u, u, u, sw, sb.reshape(1, wd))


FILT_TILE = 1024


def _filter_kernel(feat_ref, tdec_ref, w1_ref, b1_ref, f1_ref, w2_ref, b2_ref, f2_ref, w3_ref, dl_ref,
                   filt_ref, norm_ref, *, l, tr):
    i = pl.program_id(0)
    half = tr // 2
    h = jnp.sin(f1_ref[...] * (_dot(feat_ref[...], w1_ref[...], precision=HIGHEST) + b1_ref[...]))
    h = jnp.sin(f2_ref[...] * (_dot(h, w2_ref[...], precision=HIGHEST) + b2_ref[...]))
    h = _dot(h, w3_ref[...], precision=HIGHEST)
    dl = jnp.abs(dl_ref[...])
    total = jnp.zeros((1, HY_WIDTH), F32)
    for p in range(2):
        rows = slice(p * half, (p + 1) * half)
        hp = h[:, p * 2 * HY_WIDTH:(p + 1) * 2 * HY_WIDTH]
        decay = jnp.exp(-tdec_ref[rows] * dl)
        n = i * tr + p * half + lax.broadcasted_iota(jnp.int32, (half, 1), 0)
        f = jnp.where(n < l, hp[:, :HY_WIDTH], hp[:, HY_WIDTH:]) * decay
        f = jnp.where(n == l, 0.0, f)
        filt_ref[rows] = f
        total = total + jnp.sum(jnp.abs(f), axis=0, keepdims=True)

    @pl.when(i == 0)
    def _():
        norm_ref[...] = jnp.zeros_like(norm_ref)
    norm_ref[...] += total


def filter_tables(l):
    n = jnp.arange(2 * l)
    pos = jnp.where(n < l, n, 2 * l - n).astype(F32)[:, None]
    t = pos / max(l - 1, 1)
    w = 2.0 * math.pi * pos / l
    f = jnp.linspace(1e-4, FILT_BANDS - 1, FILT_BANDS, dtype=F32)[None, :]
    feat = jnp.concatenate([t, jnp.cos(f * w), -jnp.sin(f * w),
                            jnp.zeros((2 * l, LANES - FILT_EMB), F32)], axis=-1)
    half = min(FILT_TILE, 2 * l) // 2
    feat = feat.reshape(-1, 2, half, LANES).transpose(0, 2, 1, 3).reshape(l, 2 * LANES)
    return feat, t


def hyena_filter(l, tables, w1, b1, f1, w2, b2, f2, w3):
    feat, t = tables
    tr = min(FILT_TILE, 2 * l)
    hid = w2.shape[0]
    max_decay = math.log(DECAY_TARGET) / FAST_DECAY_PCT
    min_decay = math.log(DECAY_TARGET) / SLOW_DECAY_PCT
    deltas = jnp.linspace(min_decay, max_decay, HY_WIDTH, dtype=F32)[None, :]
    w1p = jnp.concatenate([w1, jnp.zeros((LANES - FILT_EMB, hid), F32)], 0)
    twice = lambda m: jax.scipy.linalg.block_diag(m, m)
    pair = lambda v: jnp.tile(v.reshape(1, hid), (1, 2))
    full = lambda a: pl.BlockSpec(a.shape, lambda i: (0,) * a.ndim)
    smalls = [twice(w1p), pair(b1), pair(f1), twice(w2), pair(b2), pair(f2), twice(w3), deltas]
    return pl.pallas_call(
        functools.partial(_filter_kernel, l=l, tr=tr),
        grid=(2 * l // tr,),
        in_specs=[pl.BlockSpec((tr // 2, 2 * LANES), lambda i: (i, 0)),
                  pl.BlockSpec((tr, 1), lambda i: (i, 0))] + [full(a) for a in smalls],
        out_specs=[pl.BlockSpec((tr, HY_WIDTH), lambda i: (i, 0)),
                   pl.BlockSpec((1, HY_WIDTH), lambda i: (0, 0))],
        out_shape=[jax.ShapeDtypeStruct((2 * l, HY_WIDTH), F32),
                   jax.ShapeDtypeStruct((1, HY_WIDTH), F32)],
        compiler_params=_cp("arbitrary"),
        name="hyena_filter",
    )(feat, t, *smalls)


def _dft_rows_kernel(f_ref, x_ref, o_ref):
    n1, nb, c = o_ref.shape[2:]
    xs = jnp.concatenate([x_ref[0, :, j, :] for j in range(nb)], axis=-1).astype(BF16)
    res = _dot(f_ref[...], xs)
    for p in range(2):
        for j in range(nb):
            o_ref[0, p, :, j, :] = res[p * n1:(p + 1) * n1, j * c:(j + 1) * c]


def dft_rows(fmat, x):
    b, k, n2, c = x.shape
    n1 = fmat.shape[0] // 2
    nb = min(DFT_COLS // c, n2)
    return pl.pallas_call(
        _dft_rows_kernel,
        grid=(b, n2 // nb),
        in_specs=[pl.BlockSpec((2 * n1, k), lambda bi, j: (0, 0)),
                  pl.BlockSpec((1, k, nb, c), lambda bi, j: (bi, 0, j, 0))],
        out_specs=pl.BlockSpec((1, 2, n1, nb, c), lambda bi, j: (bi, 0, 0, j, 0)),
        out_shape=jax.ShapeDtypeStruct((b, 2, n1, n2, c), F32),
        compiler_params=_cp("parallel", "parallel"),
        name="dft_rows",
    )(fmat, x)


SPEC_KB = 16


def _inner_dft(g, are, aim):
    n2 = are.shape[0]
    p1 = _dot(g, are)
    p2 = _dot(g, aim)
    return p1[:n2] - p2[n2:], p2[:n2] + p1[n2:]


def _spectrum_kernel(g_ref, a_ref, h_ref):
    for j in range(g_ref.shape[0]):
        h_ref[0, j], h_ref[1, j] = _inner_dft(g_ref[j], a_ref[0, 0, j].astype(BF16), a_ref[0, 1, j].astype(BF16))


def filter_spectrum(gs, a):
    _, _, n1, n2, c = a.shape
    kb = min(SPEC_KB, n1)
    return pl.pallas_call(
        _spectrum_kernel,
        grid=(n1 // kb,),
        in_specs=[pl.BlockSpec((kb, 2 * n2, n2), lambda k: (k, 0, 0)),
                  pl.BlockSpec((1, 2, kb, n2, c), lambda k: (0, 0, k, 0, 0))],
        out_specs=pl.BlockSpec((2, kb, n2, c), lambda k: (0, k, 0, 0)),
        out_shape=jax.ShapeDtypeStruct((2, n1, n2, c), F32),
        compiler_params=_cp("parallel"),
        name="filter_spectrum",
    )(gs, a)


def _spectral_kernel(g_ref, gt_ref, h_ref, a_ref, b_ref):
    n2 = a_ref.shape[3]
    for j in range(g_ref.shape[0]):
        xre, xim = _inner_dft(g_ref[j], a_ref[0, 0, j].astype(BF16), a_ref[0, 1, j].astype(BF16))
        hre, him = h_ref[0, j], h_ref[1, j]
        yre = (xre * hre - xim * him).astype(BF16)
        yim = (xre * him + xim * hre).astype(BF16)
        gt = gt_ref[j]
        q1 = _dot(gt, yre)
        q2 = _dot(gt, yim)
        b_ref[0, 0, j] = q1[:n2] + q2[n2:]
        b_ref[0, 1, j] = q2[:n2] - q1[n2:]


def spectral_multiply(gs, gts, hf, a):
    b, _, n1, n2, c = a.shape
    kb = min(SPEC_KB, n1)
    return pl.pallas_call(
        _spectral_kernel,
        grid=(n1 // kb, b),
        in_specs=[pl.BlockSpec((kb, 2 * n2, n2), lambda k, bi: (k, 0, 0)),
                  pl.BlockSpec((kb, 2 * n2, n2), lambda k, bi: (k, 0, 0)),
                  pl.BlockSpec((2, kb, n2, c), lambda k, bi: (0, k, 0, 0)),
                  pl.BlockSpec((1, 2, kb, n2, c), lambda k, bi: (bi, 0, k, 0, 0))],
        out_specs=pl.BlockSpec((1, 2, kb, n2, c), lambda k, bi: (bi, 0, k, 0, 0)),
        out_shape=jax.ShapeDtypeStruct(a.shape, F32),
        compiler_params=_cp("parallel", "arbitrary"),
        name="spectral_multiply",
    )(gs, gts, hf, a)


def _idft_rows_kernel(f_ref, bm_ref, x0_ref, z_ref, inv_ref, bias_ref, o_ref, *, inv_n):
    m, nb, c = o_ref.shape[1:]
    bflat = bm_ref[0].reshape(-1, c).astype(BF16)
    y = _dot(f_ref[...], bflat) * inv_n
    for j in range(nb):
        z = z_ref[0, :, j, :]
        o_ref[0, :, j, :] = x0_ref[0, :, j, :] * (y[j * m:(j + 1) * m] * inv_ref[...] + bias_ref[...] * z)


def idft_rows_gate(fmat, bm, x0, z, inv_norm, bias, n):
    b, _, n1, n2, c = bm.shape
    m = fmat.shape[0]
    nb = min(DFT_COLS // c, n2)
    spread = jnp.kron(fmat, jnp.eye(nb, dtype=fmat.dtype)).reshape(m, nb, 2 * n1 * nb)
    spread = spread.transpose(1, 0, 2).reshape(nb * m, 2 * n1 * nb)
    blk = pl.BlockSpec((1, m, nb, c), lambda bi, j: (bi, 0, j, 0))
    return pl.pallas_call(
        functools.partial(_idft_rows_kernel, inv_n=1.0 / n),
        grid=(b, n2 // nb),
        in_specs=[pl.BlockSpec(spread.shape, lambda bi, j: (0, 0), pipeline_mode=pl.Buffered(1)),
                  pl.BlockSpec((1, 2, n1, nb, c), lambda bi, j: (bi, 0, 0, j, 0)),
                  blk, blk,
                  pl.BlockSpec((1, c), lambda bi, j: (0, 0)),
                  pl.BlockSpec((1, c), lambda bi, j: (0, 0))],
        out_specs=blk,
        out_shape=jax.ShapeDtypeStruct((b, m, n2, c), F32),
        compiler_params=_cp("parallel", "parallel"),
        name="idft_rows_gate",
    )(spread, bm, x0, z, inv_norm, bias)


def dft_tables(l):
    n = 2 * l
    n2 = FFT_N2
    n1 = n // n2
    k1 = np.arange(n1)
    ang1 = 2.0 * np.pi * np.outer(k1, k1) / n1
    f1 = np.concatenate([np.cos(ang1), -np.sin(ang1)], 0)
    f3 = np.concatenate([np.cos(ang1), -np.sin(ang1)], 1)[: n1 // 2]
    m2 = np.arange(n2)
    tw = 2.0 * np.pi * np.outer(k1, m2) / n
    ang2 = 2.0 * np.pi * np.outer(m2, m2) / n2
    twr, twi = jnp.asarray(np.cos(tw), F32)[:, None, :], jnp.asarray(-np.sin(tw), F32)[:, None, :]
    f2r, f2i = jnp.asarray(np.cos(ang2), F32)[None], jnp.asarray(-np.sin(ang2), F32)[None]
    gre, gim = twr * f2r - twi * f2i, twr * f2i + twi * f2r
    gs = jnp.concatenate([gre, gim], 1).astype(BF16)
    gts = jnp.concatenate([gre.transpose(0, 2, 1), gim.transpose(0, 2, 1)], 1).astype(BF16)
    return dict(n=n, n1=n1, n2=n2, f1=jnp.asarray(f1, BF16), f3=jnp.asarray(f3, BF16), gs=gs, gts=gts)


def hyena_long_conv(z, x0, filt2, norm, bias, tabs):
    b, l, c = z.shape
    n, n1, n2 = tabs["n"], tabs["n1"], tabs["n2"]
    hf = filter_spectrum(tabs["gs"], dft_rows(tabs["f1"], filt2.reshape(1, n1, n2, c)))
    a = dft_rows(tabs["f1"][:, : n1 // 2], z.reshape(b, n1 // 2, n2, c))
    bm = spectral_multiply(tabs["gs"], tabs["gts"], hf, a)
    y = idft_rows_gate(tabs["f3"], bm, x0.reshape(b, n1 // 2, n2, c), z.reshape(b, n1 // 2, n2, c),
                       1.0 / norm, bias, n)
    return y.reshape(b, l, c)


def _small_conv_kernel(ff_ref, fi_ref, filt_ref, z_ref, x0_ref, inv_ref, bias_ref, o_ref, *, l):
    ff = ff_ref[...]
    hs = _dot(ff, filt_ref[...].astype(BF16))
    hre, him = hs[:2 * l], hs[2 * l:]
    z = z_ref[0]
    xs = _dot(ff[:, :l], z.astype(BF16))
    xre, xim = xs[:2 * l], xs[2 * l:]
    yre = (xre * hre - xim * him).astype(BF16)
    yim = (xre * him + xim * hre).astype(BF16)
    y = _dot(fi_ref[...], jnp.concatenate([yre, yim], 0)) * (0.5 / l)
    o_ref[0] = (x0_ref[0] * (y * inv_ref[...] + bias_ref[...] * z)).astype(o_ref.dtype)


def small_long_conv(z, x0, filt2, norm, bias):
    b, l, c = z.shape
    k = np.arange(2 * l)
    ang = 2.0 * np.pi * np.outer(k, k) / (2 * l)
    ff = jnp.asarray(np.concatenate([np.cos(ang), -np.sin(ang)], 0), BF16)
    fi = jnp.asarray(np.concatenate([np.cos(ang), -np.sin(ang)], 1)[:l], BF16)
    full = lambda a: pl.BlockSpec(a.shape, lambda bi: (0,) * a.ndim)
    row = pl.BlockSpec((1, l, c), lambda bi: (bi, 0, 0))
    inv = 1.0 / norm
    return pl.pallas_call(
        functools.partial(_small_conv_kernel, l=l),
        grid=(b,),
        in_specs=[full(ff), full(fi), full(filt2), row, row, full(inv), full(bias)],
        out_specs=row,
        out_shape=jax.ShapeDtypeStruct((b, l, c), BF16),
        compiler_params=_cp("parallel"),
        name="small_long_conv",
    )(ff, fi, filt2, z, x0, inv, bias)


def _pool_kernel(p_ref, pp_ref, pn_ref, w_ref, sc_ref, o_ref, buf_ref, *, tm, l):
    _fill_padded(buf_ref, p_ref, pp_ref, pn_ref, tm)
    t = pl.program_id(1) * tm + lax.broadcasted_iota(jnp.int32, (tm, LANES), 0)
    lane = lax.broadcasted_iota(jnp.int32, (tm, LANES), 1)
    first_group = lane < POOL_GROUP
    h0 = POOL_HALO

    def win(col, lo, hi):
        acc = buf_ref[h0 + lo:h0 + lo + tm, col]
        for j in range(lo + 1, hi):
            acc = acc + buf_ref[h0 + j:h0 + j + tm, col]
        return acc

    parts = []
    for half, (wa, wb) in enumerate(((2, 4), (8, 16))):
        col = slice(half * LANES, (half + 1) * LANES)
        sa = win(col, -wa // 2, wa // 2)
        sb = sa + win(col, -wb // 2, -wa // 2) + win(col, wa // 2, wb // 2)
        hw = jnp.where(first_group, wa // 2, wb // 2)
        cnt = jnp.minimum(t + hw, l) - jnp.maximum(t - hw, 0)
        mean = jnp.where(first_group, sa, sb) / cnt.astype(F32)
        parts.append(mean - buf_ref[h0:h0 + tm, col])
    d = jnp.concatenate(parts, axis=-1).astype(BF16)
    o_ref[0] = (_dot(d, w_ref[...]) * sc_ref[...]).astype(o_ref.dtype)


def pool_mix(p, pool_w, pool_scale):
    b, l, wd = p.shape
    tm = min(ROW_TILE, l)
    nblk = l // tm
    wblk = jax.scipy.linalg.block_diag(*[pool_w[g] for g in range(len(POOL_WINDOWS))]).astype(BF16)
    return pl.pallas_call(
        functools.partial(_pool_kernel, tm=tm, l=l),
        grid=(b, nblk),
        in_specs=_halo_specs(tm, wd, nblk) + [pl.BlockSpec((wd, wd), lambda bi, i: (0, 0)),
                                              pl.BlockSpec((1, wd), lambda bi, i: (0, 0))],
        out_specs=pl.BlockSpec((1, tm, wd), lambda bi, i: (bi, i, 0)),
        out_shape=jax.ShapeDtypeStruct((b, l, wd), BF16),
        scratch_shapes=[pltpu.VMEM((tm + 2 * POOL_HALO, wd), F32)],
        compiler_params=_cp("parallel", "parallel"),
        name="pool_mix",
    )(p, p, p, wblk, pool_scale.reshape(1, wd))


def _outproj_kernel(x_ref, hy_ref, pool_ref, att_ref, w_ref, gate_ref, o_ref):
    a, b2 = HY_WIDTH, HY_WIDTH + POOL_WIDTH
    y = (_dot(hy_ref[0].astype(BF16), w_ref[0:a]) + _dot(pool_ref[0], w_ref[a:b2]) + _dot(att_ref[0], w_ref[b2:]))
    o_ref[0] = x_ref[0] + gate_ref[0] * y


def out_projection(x, y_hy, y_pool, y_att, w_out, gate):
    b, l, d = x.shape
    tm = min(ROW_TILE, l)
    row = lambda bi, i: (bi, i, 0)
    return pl.pallas_call(
        _outproj_kernel,
        grid=(b, l // tm),
        in_specs=[pl.BlockSpec((1, tm, d), row),
                  pl.BlockSpec((1, tm, y_hy.shape[2]), row),
                  pl.BlockSpec((1, tm, y_pool.shape[2]), row),
                  pl.BlockSpec((1, tm, y_att.shape[2]), row),
                  pl.BlockSpec(w_out.shape, lambda bi, i: (0, 0)),
                  pl.BlockSpec((1, 1, d), lambda bi, i: (bi, 0, 0))],
        out_specs=pl.BlockSpec((1, tm, d), row),
        out_shape=jax.ShapeDtypeStruct(x.shape, F32),
        compiler_params=_cp("parallel", "parallel"),
        name="out_projection",
    )(x, y_hy, y_pool, y_att, w_out, gate)


FF_CHUNK = 256
MOE_TILE = 1024
MOE_ROWS = (128, 192, 256, 320, 384)


def _swiglu_acc(h, wg_ref, wu_ref, wd_ref, lead):
    ff = wg_ref.shape[-1]
    acc = None
    for f0 in range(0, ff, FF_CHUNK):
        cs = slice(f0, min(f0 + FF_CHUNK, ff))
        g = _dot(h, wg_ref[lead + (slice(None), cs)])
        u = _dot(h, wu_ref[lead + (slice(None), cs)])
        a = (g * jax.nn.sigmoid(g) * u).astype(BF16)
        part = _dot(a, wd_ref[lead + (cs, slice(None))])
        acc = part if acc is None else acc + part
    return acc


def _ffn_kernel(x_ref, g_ref, sh_ref, sc_ref, gate_ref, wg_ref, wu_ref, wd_ref, o_ref):
    x = x_ref[0]
    h = _normmod(x, g_ref[...], sh_ref[0], sc_ref[0]).astype(BF16)
    o_ref[0] = x + gate_ref[0] * _swiglu_acc(h, wg_ref, wu_ref, wd_ref, ())


def ffn_dense(x, g, shift, scale, gate, wg, wu, wd):
    b, l, d = x.shape
    tm = min(ROW_TILE, l)
    row = lambda bi, i: (bi, i, 0)
    vec = lambda bi, i: (bi, 0, 0)
    wspec = lambda w: pl.BlockSpec(w.shape, lambda bi, i: (0, 0), pipeline_mode=pl.Buffered(1))
    return pl.pallas_call(
        _ffn_kernel,
        grid=(b, l // tm),
        in_specs=[pl.BlockSpec((1, tm, d), row),
                  pl.BlockSpec((1, d), lambda bi, i: (0, 0)),
                  pl.BlockSpec((1, 1, d), vec), pl.BlockSpec((1, 1, d), vec), pl.BlockSpec((1, 1, d), vec),
                  wspec(wg), wspec(wu), wspec(wd)],
        out_specs=pl.BlockSpec((1, tm, d), row),
        out_shape=jax.ShapeDtypeStruct(x.shape, F32),
        compiler_params=_cp("parallel", "parallel"),
        name="ffn_dense",
    )(x, g, shift, scale, gate, wg, wu, wd)


def _router_kernel(x_ref, g_ref, sh_ref, sc_ref, wr_ref, tri_ref, comb_ref, rank_ref):
    h = _normmod(x_ref[0], g_ref[...], sh_ref[0], sc_ref[0])
    logits = _dot(h, wr_ref[...], precision=HIGHEST)
    lane = lax.broadcasted_iota(jnp.int32, logits.shape, 1)
    neg = jnp.float32(-jnp.inf)
    lg = jnp.where(lane < N_EXPERTS, logits, neg)
    m1 = jnp.max(lg, axis=-1, keepdims=True)
    i1 = jnp.min(jnp.where(lg == m1, lane, LANES), axis=-1, keepdims=True)
    lg2 = jnp.where(lane == i1, neg, lg)
    m2 = jnp.max(lg2, axis=-1, keepdims=True)
    i2 = jnp.min(jnp.where(lg2 == m2, lane, LANES), axis=-1, keepdims=True)
    e = jnp.exp(m2 - m1)
    g1 = 1.0 / (1.0 + e)
    comb_ref[0] = jnp.where(lane == i1, g1, jnp.where(lane == i2, e * g1, 0.0))
    routed = (lane == i1) | (lane == i2)
    before = _dot(tri_ref[...], jnp.where(routed, 1.0, 0.0).astype(BF16))
    rank_ref[0] = jnp.where(routed, before, -1.0)


def moe_route(x, g, shift, scale, w_router):
    b, l, d = x.shape
    tm = min(MOE_TILE, l)
    wr = jnp.concatenate([w_router, jnp.zeros((d, LANES - N_EXPERTS), F32)], -1)
    tri = jnp.asarray(np.tril(np.ones((tm, tm), np.float32), -1), BF16)
    row = lambda bi, i: (bi, i, 0)
    vec = lambda bi, i: (bi, 0, 0)
    return pl.pallas_call(
        _router_kernel,
        grid=(b, l // tm),
        in_specs=[pl.BlockSpec((1, tm, d), row),
                  pl.BlockSpec((1, d), lambda bi, i: (0, 0)),
                  pl.BlockSpec((1, 1, d), vec), pl.BlockSpec((1, 1, d), vec),
                  pl.BlockSpec((d, LANES), lambda bi, i: (0, 0)),
                  pl.BlockSpec((tm, tm), lambda bi, i: (0, 0))],
        out_specs=[pl.BlockSpec((1, tm, LANES), row)] * 2,
        out_shape=[jax.ShapeDtypeStruct((b, l, LANES), F32)] * 2,
        compiler_params=_cp("parallel", "parallel"),
        name="moe_router",
    )(x, g, shift, scale, wr, tri)


def _moe_kernel(x_ref, g_ref, sh_ref, sc_ref, gate_ref, comb_ref, wg_ref, wu_ref, wd_ref, o_ref, acc_ref):
    e = pl.program_id(2)
    x = x_ref[0]
    h = _normmod(x, g_ref[...], sh_ref[0], sc_ref[0]).astype(BF16)
    comb = comb_ref[0]
    lane = lax.broadcasted_iota(jnp.int32, comb.shape, 1)
    ce = jnp.sum(jnp.where(lane == e, comb, 0.0), axis=-1, keepdims=True)
    y = ce * _swiglu_acc(h, wg_ref, wu_ref, wd_ref, (0, 0))

    @pl.when(e == 0)
    def _():
        acc_ref[...] = y

    @pl.when(e > 0)
    def _():
        acc_ref[...] += y

    @pl.when(e == N_EXPERTS - 1)
    def _():
        o_ref[0] = x + gate_ref[0] * acc_ref[...]


def moe_dense(x, g, shift, scale, gate, comb, wg, wu, wd, li):
    b, l, d = x.shape
    tm = min(ROW_TILE, l)
    ff = wg.shape[-1]
    row = lambda bi, i, e: (bi, i, 0)
    vec = lambda bi, i, e: (bi, 0, 0)
    return pl.pallas_call(
        _moe_kernel,
        grid=(b, l // tm, N_EXPERTS),
        in_specs=[pl.BlockSpec((1, tm, d), row),
                  pl.BlockSpec((1, d), lambda bi, i, e: (0, 0)),
                  pl.BlockSpec((1, 1, d), vec), pl.BlockSpec((1, 1, d), vec), pl.BlockSpec((1, 1, d), vec),
                  pl.BlockSpec((1, tm, LANES), row),
                  pl.BlockSpec((1, 1, d, ff), lambda bi, i, e: (li, e, 0, 0)),
                  pl.BlockSpec((1, 1, d, ff), lambda bi, i, e: (li, e, 0, 0)),
                  pl.BlockSpec((1, 1, ff, d), lambda bi, i, e: (li, e, 0, 0))],
        out_specs=pl.BlockSpec((1, tm, d), row),
        out_shape=jax.ShapeDtypeStruct(x.shape, F32),
        scratch_shapes=[pltpu.VMEM((tm, d), F32)],
        compiler_params=_cp("parallel", "parallel", "arbitrary"),
        name="moe_dense",
    )(x, g, shift, scale, gate, comb, wg, wu, wd)


def _moe_sparse_kernel(cnt_ref, x_ref, g_ref, sh_ref, sc_ref, gate_ref, comb_ref, rank_ref, rankt_ref,
                       wg_ref, wu_ref, wd_ref, *rest, t, sizes):
    fg_ref = rest[0] if len(rest) == 3 else None
    o_ref, h_scr = rest[-2:]
    i, e = pl.program_id(0), pl.program_id(1)
    cnt = cnt_ref[i * N_EXPERTS + e]

    @pl.when(e == 0)
    def _():
        x = x_ref[0]
        h_scr[...] = _normmod(x, g_ref[...], sh_ref[0], sc_ref[0]).astype(BF16)
        o_ref[0] = x

    rrow = rankt_ref[0, pl.ds(e, 1), :]
    lane = lax.broadcasted_iota(jnp.int32, (t, LANES), 1)
    rcol = jnp.sum(jnp.where(lane == e, rank_ref[0], 0.0), axis=-1, keepdims=True)
    ccol = jnp.sum(jnp.where(lane == e, comb_ref[0], 0.0), axis=-1, keepdims=True)

    def block(base, r):
        rid = (base + lax.broadcasted_iota(jnp.int32, (r, 1), 0)).astype(F32)
        onehot = jnp.where(rrow == rid, 1.0, 0.0).astype(BF16)
        xg = _dot(onehot, h_scr[...]).astype(BF16)
        y = _swiglu_acc(xg, wg_ref, wu_ref, wd_ref, (0, 0)).astype(BF16)
        cid = (base + lax.broadcasted_iota(jnp.int32, (1, r), 1)).astype(F32)
        onehot_t = jnp.where(rcol == cid, 1.0, 0.0).astype(BF16)
        o_ref[0] += (gate_ref[0] * ccol) * _dot(onehot_t, y)

    big = sizes[-1]
    nfull = cnt // big

    def full(k, carry):
        block(k * big, big)
        return carry
    lax.fori_loop(0, nfull, full, 0)
    base = nfull * big
    rem = cnt - base
    for lo, r in zip((0,) + sizes[:-1], sizes):
        @pl.when((rem > lo) & (rem <= r))
        def _(r=r):
            block(base, r)

    if fg_ref is not None:
        @pl.when(e == N_EXPERTS - 1)
        def _():
            o = o_ref[0]
            o_ref[0] = o * lax.rsqrt(jnp.mean(o * o, axis=-1, keepdims=True) + EPS) * fg_ref[...]


def moe_sparse(x, g, shift, scale, gate, comb, rank, wg, wu, wd, li, final_g=None):
    b, l, d = x.shape
    t = MOE_TILE
    tpb = l // t
    n_tiles = b * tpb
    ff = wg.shape[-1]
    rk = rank[..., :N_EXPERTS].reshape(n_tiles, t, N_EXPERTS)
    rank_t = rk.transpose(0, 2, 1)
    cnt = jnp.sum(rk >= 0, axis=1).astype(jnp.int32).reshape(n_tiles * N_EXPERTS)
    row = lambda i, e, c: (i // tpb, i % tpb, 0)
    vec = lambda i, e, c: (i // tpb, 0, 0)
    grid_spec = pltpu.PrefetchScalarGridSpec(
        num_scalar_prefetch=1,
        grid=(n_tiles, N_EXPERTS),
        in_specs=[pl.BlockSpec((1, t, d), row, pipeline_mode=pl.Buffered(1)),
                  pl.BlockSpec((1, d), lambda i, e, c: (0, 0)),
                  pl.BlockSpec((1, 1, d), vec), pl.BlockSpec((1, 1, d), vec), pl.BlockSpec((1, 1, d), vec),
                  pl.BlockSpec((1, t, LANES), row, pipeline_mode=pl.Buffered(1)),
                  pl.BlockSpec((1, t, LANES), row, pipeline_mode=pl.Buffered(1)),
                  pl.BlockSpec((1, N_EXPERTS, t), lambda i, e, c: (i, 0, 0)),
                  pl.BlockSpec((1, 1, d, ff), lambda i, e, c: (li, e, 0, 0)),
                  pl.BlockSpec((1, 1, d, ff), lambda i, e, c: (li, e, 0, 0)),
                  pl.BlockSpec((1, 1, ff, d), lambda i, e, c: (li, e, 0, 0))]
        + ([] if final_g is None else [pl.BlockSpec((1, d), lambda i, e, c: (0, 0))]),
        out_specs=pl.BlockSpec((1, t, d), row),
        scratch_shapes=[pltpu.VMEM((t, d), BF16)],
    )
    return pl.pallas_call(
        functools.partial(_moe_sparse_kernel, t=t, sizes=MOE_ROWS),
        grid_spec=grid_spec,
        out_shape=jax.ShapeDtypeStruct(x.shape, F32),
        compiler_params=_cp("parallel", "arbitrary"),
        name="moe_sparse",
    )(cnt, x, g, shift, scale, gate, comb, rank, rank_t, wg, wu, wd, *(() if final_g is None else (final_g,)))


def _final_kernel(x_ref, g_ref, o_ref):
    x = x_ref[0]
    o_ref[0] = x * lax.rsqrt(jnp.mean(x * x, axis=-1, keepdims=True) + EPS) * g_ref[...]


def final_norm(x, g):
    b, l, d = x.shape
    tm = min(2 * ROW_TILE, l)
    return pl.pallas_call(
        _final_kernel,
        grid=(b, l // tm),
        in_specs=[pl.BlockSpec((1, tm, d), lambda bi, i: (bi, i, 0)),
                  pl.BlockSpec((1, d), lambda bi, i: (0, 0))],
        out_specs=pl.BlockSpec((1, tm, d), lambda bi, i: (bi, i, 0)),
        out_shape=jax.ShapeDtypeStruct(x.shape, F32),
        compiler_params=_cp("parallel", "parallel"),
        name="final_norm",
    )(x, g)


def kernel(x, c, ctx, c_ctx, norm1_g, norm2_g, w_ada, b_ada, w_in, hy_short_w, hy_short_b, filt_w1, filt_b1, filt_freq1, filt_w2, filt_b2, filt_freq2, filt_w3, hy_bias, pool_w, pool_scale, q_norm_g, w_uq, kv_norm_g, w_ukv, w_out, ffn_wg, ffn_wu, ffn_wd, moe_router, moe_wg, moe_wu, moe_wd, final_g):
    b, l, d = x.shape
    lc = ctx.shape[1]
    depth = w_ada.shape[0]

    cc = jnp.zeros((8, d), F32).at[:b].set(c).at[b].set(c_ctx)
    mods = ada_vectors(cc, w_ada, b_ada)

    rope_l = rope_tables(l, True)
    rope_c = rope_tables(lc, False)
    ftab_l, ftab_c = filter_tables(l), filter_tables(lc)
    dtabs = dft_tables(l)
    moe_w = (moe_wg.astype(BF16), moe_wu.astype(BF16), moe_wd.astype(BF16))

    xl, xc = x, ctx
    for layer in range(depth):
        last = layer == depth - 1
        mod = mods[layer].reshape(8, 6, 1, d)
        mod_l = [mod[:b, j] for j in range(6)]
        mod_c = [jnp.broadcast_to(mod[b:b + 1, j], (b, 1, d)) for j in range(6)]
        g1 = norm1_g[layer].reshape(1, d)
        g2 = norm2_g[layer].reshape(1, d)
        w_in_l = w_in[layer].astype(BF16)
        w_out_l = w_out[layer].astype(BF16)
        mw = mla_weights(q_norm_g[layer], w_uq[layer], kv_norm_g[layer], w_ukv[layer])
        fw = (filt_w1[layer], filt_b1[layer], filt_freq1[layer], filt_w2[layer], filt_b2[layer],
              filt_freq2[layer], filt_w3[layer])
        bias = hy_bias[layer].reshape(1, HY_WIDTH)

        def channel_mix(xs, md, final=False):
            i = layer // 2
            fg = final_g.reshape(1, d)
            if layer % 2 == 1 and xs.shape[1] % MOE_TILE == 0:
                comb, rank = moe_route(xs, g2, md[3], md[4], moe_router[i])
                return moe_sparse(xs, g2, md[3], md[4], md[5], comb, rank, *moe_w, i, fg if final else None)
            if layer % 2 == 0:
                out = ffn_dense(xs, g2, md[3], md[4], md[5], ffn_wg[i].astype(BF16),
                                ffn_wu[i].astype(BF16), ffn_wd[i].astype(BF16))
            else:
                comb, _ = moe_route(xs, g2, md[3], md[4], moe_router[i])
                out = moe_dense(xs, g2, md[3], md[4], md[5], comb, *moe_w, i)
            return final_norm(out, fg) if final else out

        hy_l, pool_l, q_l, kt_l, v_l = in_projection(xl, g1, mod_l[0], mod_l[1], w_in_l, *rope_l, mw)
        hy_c, pool_c, q_c, kt_c, v_c = in_projection(xc, g1, mod_c[0], mod_c[1], w_in_l, *rope_c, mw)

        att_l = attention(q_l, kt_c, v_c, kt_l, v_l)
        z_l, x0_l = hyena_pre(hy_l, hy_short_w[layer], hy_short_b[layer])
        filt_l, nrm_l = hyena_filter(l, ftab_l, *fw)
        yhy_l = hyena_long_conv(z_l, x0_l, filt_l, nrm_l, bias, dtabs)
        ypool_l = pool_mix(pool_l, pool_w[layer], pool_scale[layer])
        xl = out_projection(xl, yhy_l, ypool_l, att_l, w_out_l, mod_l[2])

        if not last:
            att_c = attention(q_c, kt_c, v_c)
            z_c, x0_c = hyena_pre(hy_c, hy_short_w[layer], hy_short_b[layer])
            filt_c, nrm_c = hyena_filter(lc, ftab_c, *fw)
            yhy_c = small_long_conv(z_c, x0_c, filt_c, nrm_c, bias)
            ypool_c = pool_mix(pool_c, pool_w[layer], pool_scale[layer])
            xc = out_projection(xc, yhy_c, ypool_c, att_c, w_out_l, mod_c[2])
            xc = channel_mix(xc, mod_c)

        xl = channel_mix(xl, mod_l, final=last)
    return xl
```

```python
import functools
import math

import jax
import jax.numpy as jnp
import numpy as np
from jax import lax
from jax.experimental import pallas as pl
from jax.experimental.pallas import tpu as pltpu

F32 = jnp.float32
BF16 = jnp.bfloat16
HIGHEST = lax.Precision.HIGHEST

EPS = 1e-6
GRID_W = 64
HY_WIDTH = 256
POOL_WINDOWS = (2, 4, 8, 16)
POOL_WIDTH = 256
POOL_GROUP = 64
POOL_HALO = 8
MLA_HEADS = 8
QK_NOPE = 64
QK_ROPE = 32
V_DIM = 64
Q_LORA = 384
KV_LORA = 256
HEAD_PAD = 128
SM_SCALE = (QK_NOPE + QK_ROPE) ** -0.5
ROPE_THETA = 10000.0
FILT_EMB = 33
FILT_BANDS = 16
DECAY_TARGET = 1e-2
FAST_DECAY_PCT = 0.3
SLOW_DECAY_PCT = 1.5
N_EXPERTS = 8
LANES = 128
FFT_N2 = 128
VMEM_LIMIT = 56 * 2 ** 20
ROW_TILE = 1024
DFT_COLS = 2048
ADA_COLS = 1536


def _cp(*sem):
    return pltpu.CompilerParams(dimension_semantics=sem, vmem_limit_bytes=VMEM_LIMIT)


def _dot(a, b, **kw):
    return jnp.dot(a, b, preferred_element_type=F32, **kw)


def _normmod(x, g, shift, scale):
    ms = jnp.mean(x * x, axis=-1, keepdims=True)
    return x * lax.rsqrt(ms + EPS) * (g * (1.0 + scale)) + shift


def _ada_kernel(c_ref, w_ref, b_ref, o_ref):
    c = c_ref[...]
    s = c * jax.nn.sigmoid(c)
    o_ref[0] = _dot(s, w_ref[0], precision=HIGHEST) + b_ref[0]


def ada_vectors(cc, w_ada, b_ada):
    depth, d, n = w_ada.shape
    tn = ADA_COLS
    return pl.pallas_call(
        _ada_kernel,
        grid=(depth, n // tn),
        in_specs=[pl.BlockSpec((8, d), lambda l, j: (0, 0)),
                  pl.BlockSpec((1, d, tn), lambda l, j: (l, 0, j)),
                  pl.BlockSpec((1, 1, tn), lambda l, j: (l, 0, j))],
        out_specs=pl.BlockSpec((1, 8, tn), lambda l, j: (l, 0, j)),
        out_shape=jax.ShapeDtypeStruct((depth, 8, n), F32),
        compiler_params=_cp("parallel", "parallel"),
        name="ada_vectors",
    )(cc, w_ada, b_ada.reshape(depth, 1, n))


IN_HY, IN_POOL = 3 * HY_WIDTH, POOL_WIDTH
IN_OFFS = tuple(int(v) for v in np.cumsum((0, IN_HY, IN_POOL, Q_LORA, KV_LORA, QK_ROPE)))


def _inproj_kernel(x_ref, g_ref, sh_ref, sc_ref, w_ref, c_ref, s_ref, wqa_ref, wqb_ref, wk_ref, wv_ref,
                   pa_ref, pb_ref, hy_ref, pool_ref, q_ref, kt_ref, v_ref):
    h = _normmod(x_ref[0], g_ref[...], sh_ref[0], sc_ref[0]).astype(BF16)
    y = _dot(h, w_ref[...])
    o = IN_OFFS
    hy_ref[0] = y[:, o[0]:o[1]]
    pool_ref[0] = y[:, o[1]:o[2]]
    cq, ckv = y[:, o[2]:o[3]], y[:, o[3]:o[4]]
    kpe = y[:, o[4]:o[5]].astype(BF16)
    cqn = (cq * lax.rsqrt(jnp.mean(cq * cq, axis=-1, keepdims=True) + EPS)).astype(BF16)
    ckvn = (ckv * lax.rsqrt(jnp.mean(ckv * ckv, axis=-1, keepdims=True) + EPS)).astype(BF16)
    cs, sn = c_ref[...], s_ref[...]
    qa = _dot(cqn, wqa_ref[...])
    qb = _dot(cqn, wqb_ref[...])
    kn = _dot(ckvn, wk_ref[...])
    vv = _dot(ckvn, wv_ref[...])
    kpa = _dot(kpe, pa_ref[...])
    kpb = _dot(kpe, pb_ref[...])
    kpe_rot = kpa * cs + kpb * sn
    lane = lax.broadcasted_iota(jnp.int32, (1, HEAD_PAD), 1)
    for hd in range(MLA_HEADS):
        sl = slice(hd * HEAD_PAD, (hd + 1) * HEAD_PAD)
        q_ref[0, hd] = (qa[:, sl] * cs + qb[:, sl] * sn).astype(BF16)
        kt_ref[0, hd] = (kn[:, sl] + kpe_rot).T.astype(BF16)
        one_lane = V_DIM if hd % 2 == 0 else 0
        v_ref[0, hd] = jnp.where(lane == one_lane, 1.0, vv[:, sl]).astype(BF16)


def in_projection(x, g, shift, scale, w_in, ctab, stab, wts):
    b, l, d = x.shape
    tm = min(ROW_TILE, l)
    hw = MLA_HEADS
    row = lambda bi, i: (bi, i, 0)
    vec = lambda bi, i: (bi, 0, 0)
    full = lambda a: pl.BlockSpec(a.shape, lambda bi, i: (0,) * a.ndim)
    return pl.pallas_call(
        _inproj_kernel,
        grid=(b, l // tm),
        in_specs=[pl.BlockSpec((1, tm, d), row),
                  pl.BlockSpec((1, d), lambda bi, i: (0, 0)),
                  pl.BlockSpec((1, 1, d), vec),
                  pl.BlockSpec((1, 1, d), vec),
                  full(w_in),
                  pl.BlockSpec((tm, HEAD_PAD), lambda bi, i: (i, 0)),
                  pl.BlockSpec((tm, HEAD_PAD), lambda bi, i: (i, 0))] + [full(w) for w in wts],
        out_specs=[pl.BlockSpec((1, tm, IN_HY), row),
                   pl.BlockSpec((1, tm, IN_POOL), row),
                   pl.BlockSpec((1, hw, tm, HEAD_PAD), lambda bi, i: (bi, 0, i, 0)),
                   pl.BlockSpec((1, hw, HEAD_PAD, tm), lambda bi, i: (bi, 0, 0, i)),
                   pl.BlockSpec((1, hw, tm, HEAD_PAD), lambda bi, i: (bi, 0, i, 0))],
        out_shape=[jax.ShapeDtypeStruct((b, l, IN_HY), F32),
                   jax.ShapeDtypeStruct((b, l, IN_POOL), F32),
                   jax.ShapeDtypeStruct((b, hw, l, HEAD_PAD), BF16),
                   jax.ShapeDtypeStruct((b, hw, HEAD_PAD, l), BF16),
                   jax.ShapeDtypeStruct((b, hw, l, HEAD_PAD), BF16)],
        compiler_params=_cp("parallel", "parallel"),
        name="in_projection",
    )(x, g, shift, scale, w_in, ctab, stab, *wts)


def mla_weights(q_norm_g, w_uq, kv_norm_g, w_ukv):
    hw, dq = MLA_HEADS, QK_NOPE + QK_ROPE
    half = QK_ROPE // 2
    wq = (w_uq * q_norm_g[:, None] * (SM_SCALE * math.log2(math.e))).reshape(Q_LORA, hw, dq)
    pad = jnp.zeros((Q_LORA, hw, HEAD_PAD - dq), F32)
    wqa = jnp.concatenate([wq, pad], -1)
    swap = jnp.concatenate([wq[..., QK_NOPE + half:], wq[..., QK_NOPE:QK_NOPE + half]], -1)
    wqb = jnp.concatenate([jnp.zeros((Q_LORA, hw, QK_NOPE), F32), swap, pad], -1)
    wkv = (w_ukv * kv_norm_g[:, None]).reshape(KV_LORA, hw, QK_NOPE + V_DIM)
    z64 = jnp.zeros((KV_LORA, hw, 64), F32)
    wk = jnp.concatenate([wkv[..., :QK_NOPE], z64], -1)
    v = wkv[..., QK_NOPE:]
    wv = jnp.where((jnp.arange(hw) % 2 == 0)[None, :, None],
                   jnp.concatenate([v, z64], -1), jnp.concatenate([z64, v], -1))
    eye = np.eye(QK_ROPE, dtype=np.float32)
    pa = np.zeros((QK_ROPE, HEAD_PAD), np.float32)
    pa[:, QK_NOPE:QK_NOPE + QK_ROPE] = eye
    pb = np.zeros((QK_ROPE, HEAD_PAD), np.float32)
    pb[:, QK_NOPE:QK_NOPE + QK_ROPE] = np.concatenate([eye[:, half:], eye[:, :half]], 1)
    flat = lambda w: w.reshape(w.shape[0], hw * HEAD_PAD).astype(BF16)
    return (flat(wqa), flat(wqb), flat(wk), flat(wv), jnp.asarray(pa, BF16), jnp.asarray(pb, BF16))


def rope_tables(l, with_rope):
    ctab = np.zeros((l, HEAD_PAD), np.float32)
    stab = np.zeros((l, HEAD_PAD), np.float32)
    ctab[:, :QK_NOPE + QK_ROPE] = 1.0
    if not with_rope:
        return jnp.asarray(ctab), jnp.asarray(stab)
    n_freq = QK_ROPE // 4
    inv_freq = 1.0 / (ROPE_THETA ** (jnp.arange(n_freq, dtype=F32) / n_freq))
    rows = l // GRID_W
    row = jnp.repeat(jnp.arange(rows, dtype=F32), GRID_W)
    col = jnp.tile(jnp.arange(GRID_W, dtype=F32), rows)
    ang = jnp.concatenate([row[:, None] * inv_freq, col[:, None] * inv_freq], axis=-1)
    cos, sin = jnp.cos(ang), jnp.sin(ang)
    one = jnp.ones((l, QK_NOPE), F32)
    zero = jnp.zeros((l, QK_NOPE), F32)
    tail = jnp.zeros((l, HEAD_PAD - QK_NOPE - QK_ROPE), F32)
    return (jnp.concatenate([one, cos, cos, tail], -1), jnp.concatenate([zero, -sin, sin, tail], -1))


ATT_TQ = 1024
ATT_RB = 1024
ATT_CK = 2048


def _row_max(s):
    mp = s[:, 0:LANES]
    for j in range(1, s.shape[1] // LANES):
        mp = jnp.maximum(mp, s[:, j * LANES:(j + 1) * LANES])
    return jnp.max(mp, axis=-1, keepdims=True)


def _attn_kernel(q_ref, kct_ref, vc_ref, *rest, n_chunks, ck, tq):
    if n_chunks:
        kt_ref, v_ref, o_ref = rest
    else:
        (o_ref,) = rest
    rb = min(ATT_RB, tq)
    chains = [(hh, r) for hh in range(2) for r in range(tq // rb)]
    lane = lax.broadcasted_iota(jnp.int32, (1, HEAD_PAD), 1)

    def q_rows(hh, r):
        return q_ref[0, hh, r * rb:(r + 1) * rb, :]

    state = []
    for hh, r in chains:
        s = _dot(q_rows(hh, r), kct_ref[0, hh])
        m = _row_max(s)
        p = jnp.exp2(s - m).astype(BF16)
        state += [m, _dot(p, vc_ref[0, hh])]

    if n_chunks:
        def body(c, state):
            off = pl.multiple_of(c * ck, ck)
            out = []
            for i, (hh, r) in enumerate(chains):
                m, acc = state[2 * i], state[2 * i + 1]
                s = _dot(q_rows(hh, r), kt_ref[0, hh, :, pl.ds(off, ck)])
                m_new = jnp.maximum(m, _row_max(s))
                alpha = jnp.exp2(m - m_new)
                p = jnp.exp2(s - m_new).astype(BF16)
                out += [m_new, acc * alpha + _dot(p, v_ref[0, hh, pl.ds(off, ck), :])]
            return tuple(out)
        state = lax.fori_loop(0, n_chunks, body, tuple(state), unroll=4 if n_chunks % 4 == 0 else 1)

    for r in range(tq // rb):
        outs = []
        for hh in range(2):
            acc = state[2 * chains.index((hh, r)) + 1]
            one_lane = V_DIM if hh == 0 else 0
            denom = jnp.sum(jnp.where(lane == one_lane, acc, 0.0), axis=-1, keepdims=True)
            outs.append(acc / denom)
        o_ref[0, r * rb:(r + 1) * rb, :] = jnp.where(lane < V_DIM, outs[0], outs[1]).astype(o_ref.dtype)


def attention(q, kct, vc, kt=None, v=None):
    b, hw, lq, _ = q.shape
    lc = vc.shape[2]
    tq = min(ATT_TQ, lq)
    ck = ATT_CK if kt is None else min(ATT_CK, kt.shape[3])
    n_chunks = 0 if kt is None else kt.shape[3] // ck
    in_specs = [pl.BlockSpec((1, 2, tq, HEAD_PAD), lambda bi, hp, i: (bi, hp, i, 0)),
                pl.BlockSpec((1, 2, HEAD_PAD, lc), lambda bi, hp, i: (bi, hp, 0, 0)),
                pl.BlockSpec((1, 2, lc, HEAD_PAD), lambda bi, hp, i: (bi, hp, 0, 0))]
    args = [q, kct, vc]
    if n_chunks:
        l = kt.shape[3]
        in_specs += [pl.BlockSpec((1, 2, HEAD_PAD, l), lambda bi, hp, i: (bi, hp, 0, 0)),
                     pl.BlockSpec((1, 2, l, HEAD_PAD), lambda bi, hp, i: (bi, hp, 0, 0))]
        args += [kt, v]
    return pl.pallas_call(
        functools.partial(_attn_kernel, n_chunks=n_chunks, ck=ck, tq=tq),
        grid=(b, hw // 2, lq // tq),
        in_specs=in_specs,
        out_specs=pl.BlockSpec((1, tq, HEAD_PAD), lambda bi, hp, i: (bi, i, hp)),
        out_shape=jax.ShapeDtypeStruct((b, lq, hw * V_DIM), BF16),
        compiler_params=_cp("parallel", "parallel", "arbitrary"),
        name="attention",
    )(*args)


def _halo_specs(tm, width, nblk):
    r = tm // POOL_HALO
    last = nblk * r - 1
    return [pl.BlockSpec((1, tm, width), lambda bi, i: (bi, i, 0)),
            pl.BlockSpec((1, POOL_HALO, width), lambda bi, i: (bi, jnp.maximum(i * r - 1, 0), 0)),
            pl.BlockSpec((1, POOL_HALO, width), lambda bi, i: (bi, jnp.minimum((i + 1) * r, last), 0))]


def _fill_padded(buf_ref, cur_ref, prev_ref, next_ref, tm):
    i = pl.program_id(1)
    first = i == 0
    last = i == pl.num_programs(1) - 1
    buf_ref[0:POOL_HALO] = jnp.where(first, 0.0, prev_ref[0])
    buf_ref[POOL_HALO:POOL_HALO + tm] = cur_ref[0]
    buf_ref[POOL_HALO + tm:] = jnp.where(last, 0.0, next_ref[0])


def _hyena_pre_kernel(u_ref, up_ref, un_ref, w_ref, b_ref, z_ref, x0_ref, buf_ref, *, tm):
    _fill_padded(buf_ref, u_ref, up_ref, un_ref, tm)
    w = w_ref[...]
    uc = (buf_ref[POOL_HALO - 1:POOL_HALO - 1 + tm] * w[0:1] + buf_ref[POOL_HALO:POOL_HALO + tm] * w[1:2]
          + buf_ref[POOL_HALO + 1:POOL_HALO + 1 + tm] * w[2:3] + b_ref[...])
    x0_ref[0] = uc[:, :HY_WIDTH]
    z_ref[0] = uc[:, HY_WIDTH:2 * HY_WIDTH] * uc[:, 2 * HY_WIDTH:]


def hyena_pre(u, sw, sb):
    b, l, wd = u.shape
    tm = min(ROW_TILE, l)
    nblk = l // tm
    return pl.pallas_call(
        functools.partial(_hyena_pre_kernel, tm=tm),
        grid=(b, nblk),
        in_specs=_halo_specs(tm, wd, nblk) + [pl.BlockSpec((3, wd), lambda bi, i: (0, 0)),
                                              pl.BlockSpec((1, wd), lambda bi, i: (0, 0))],
        out_specs=[pl.BlockSpec((1, tm, HY_WIDTH), lambda bi, i: (bi, i, 0))] * 2,
        out_shape=[jax.ShapeDtypeStruct((b, l, HY_WIDTH), F32)] * 2,
        scratch_shapes=[pltpu.VMEM((tm + 2 * POOL_HALO, wd), F32)],
        compiler_params=_cp("parallel", "parallel"),
        name="hyena_pre",
    )(u, u, u, sw, sb.reshape(1, wd))


FILT_TILE = 1024


def _filter_kernel(feat_ref, tdec_ref, w1_ref, b1_ref, f1_ref, w2_ref, b2_ref, f2_ref, w3_ref, dl_ref,
                   filt_ref, norm_ref, *, l, tr):
    i = pl.program_id(0)
    half = tr // 2
    h = jnp.sin(f1_ref[...] * (_dot(feat_ref[...], w1_ref[...], precision=HIGHEST) + b1_ref[...]))
    h = jnp.sin(f2_ref[...] * (_dot(h, w2_ref[...], precision=HIGHEST) + b2_ref[...]))
    h = _dot(h, w3_ref[...], precision=HIGHEST)
    dl = jnp.abs(dl_ref[...])
    total = jnp.zeros((1, HY_WIDTH), F32)
    for p in range(2):
        rows = slice(p * half, (p + 1) * half)
        hp = h[:, p * 2 * HY_WIDTH:(p + 1) * 2 * HY_WIDTH]
        decay = jnp.exp(-tdec_ref[rows] * dl)
        n = i * tr + p * half + lax.broadcasted_iota(jnp.int32, (half, 1), 0)
        f = jnp.where(n < l, hp[:, :HY_WIDTH], hp[:, HY_WIDTH:]) * decay
        f = jnp.where(n == l, 0.0, f)
        filt_ref[rows] = f
        total = total + jnp.sum(jnp.abs(f), axis=0, keepdims=True)

    @pl.when(i == 0)
    def _():
        norm_ref[...] = jnp.zeros_like(norm_ref)
    norm_ref[...] += total


def filter_tables(l):
    n = jnp.arange(2 * l)
    pos = jnp.where(n < l, n, 2 * l - n).astype(F32)[:, None]
    t = pos / max(l - 1, 1)
    w = 2.0 * math.pi * pos / l
    f = jnp.linspace(1e-4, FILT_BANDS - 1, FILT_BANDS, dtype=F32)[None, :]
    feat = jnp.concatenate([t, jnp.cos(f * w), -jnp.sin(f * w),
                            jnp.zeros((2 * l, LANES - FILT_EMB), F32)], axis=-1)
    half = min(FILT_TILE, 2 * l) // 2
    feat = feat.reshape(-1, 2, half, LANES).transpose(0, 2, 1, 3).reshape(l, 2 * LANES)
    return feat, t


def hyena_filter(l, tables, w1, b1, f1, w2, b2, f2, w3):
    feat, t = tables
    tr = min(FILT_TILE, 2 * l)
    hid = w2.shape[0]
    max_decay = math.log(DECAY_TARGET) / FAST_DECAY_PCT
    min_decay = math.log(DECAY_TARGET) / SLOW_DECAY_PCT
    deltas = jnp.linspace(min_decay, max_decay, HY_WIDTH, dtype=F32)[None, :]
    w1p = jnp.concatenate([w1, jnp.zeros((LANES - FILT_EMB, hid), F32)], 0)
    twice = lambda m: jax.scipy.linalg.block_diag(m, m)
    pair = lambda v: jnp.tile(v.reshape(1, hid), (1, 2))
    full = lambda a: pl.BlockSpec(a.shape, lambda i: (0,) * a.ndim)
    smalls = [twice(w1p), pair(b1), pair(f1), twice(w2), pair(b2), pair(f2), twice(w3), deltas]
    return pl.pallas_call(
        functools.partial(_filter_kernel, l=l, tr=tr),
        grid=(2 * l // tr,),
        in_specs=[pl.BlockSpec((tr // 2, 2 * LANES), lambda i: (i, 0)),
                  pl.BlockSpec((tr, 1), lambda i: (i, 0))] + [full(a) for a in smalls],
        out_specs=[pl.BlockSpec((tr, HY_WIDTH), lambda i: (i, 0)),
                   pl.BlockSpec((1, HY_WIDTH), lambda i: (0, 0))],
        out_shape=[jax.ShapeDtypeStruct((2 * l, HY_WIDTH), F32),
                   jax.ShapeDtypeStruct((1, HY_WIDTH), F32)],
        compiler_params=_cp("arbitrary"),
        name="hyena_filter",
    )(feat, t, *smalls)


def _dft_rows_kernel(f_ref, x_ref, o_ref):
    n1, nb, c = o_ref.shape[2:]
    xs = jnp.concatenate([x_ref[0, :, j, :] for j in range(nb)], axis=-1).astype(BF16)
    res = _dot(f_ref[...], xs)
    for p in range(2):
        for j in range(nb):
            o_ref[0, p, :, j, :] = res[p * n1:(p + 1) * n1, j * c:(j + 1) * c]


def dft_rows(fmat, x):
    b, k, n2, c = x.shape
    n1 = fmat.shape[0] // 2
    nb = min(DFT_COLS // c, n2)
    return pl.pallas_call(
        _dft_rows_kernel,
        grid=(b, n2 // nb),
        in_specs=[pl.BlockSpec((2 * n1, k), lambda bi, j: (0, 0)),
                  pl.BlockSpec((1, k, nb, c), lambda bi, j: (bi, 0, j, 0))],
        out_specs=pl.BlockSpec((1, 2, n1, nb, c), lambda bi, j: (bi, 0, 0, j, 0)),
        out_shape=jax.ShapeDtypeStruct((b, 2, n1, n2, c), F32),
        compiler_params=_cp("parallel", "parallel"),
        name="dft_rows",
    )(fmat, x)


SPEC_KB = 16


def _inner_dft(g, are, aim):
    n2 = are.shape[0]
    p1 = _dot(g, are)
    p2 = _dot(g, aim)
    return p1[:n2] - p2[n2:], p2[:n2] + p1[n2:]


def _spectrum_kernel(g_ref, a_ref, h_ref):
    for j in range(g_ref.shape[0]):
        h_ref[0, j], h_ref[1, j] = _inner_dft(g_ref[j], a_ref[0, 0, j].astype(BF16), a_ref[0, 1, j].astype(BF16))


def filter_spectrum(gs, a):
    _, _, n1, n2, c = a.shape
    kb = min(SPEC_KB, n1)
    return pl.pallas_call(
        _spectrum_kernel,
        grid=(n1 // kb,),
        in_specs=[pl.BlockSpec((kb, 2 * n2, n2), lambda k: (k, 0, 0)),
                  pl.BlockSpec((1, 2, kb, n2, c), lambda k: (0, 0, k, 0, 0))],
        out_specs=pl.BlockSpec((2, kb, n2, c), lambda k: (0, k, 0, 0)),
        out_shape=jax.ShapeDtypeStruct((2, n1, n2, c), F32),
        compiler_params=_cp("parallel"),
        name="filter_spectrum",
    )(gs, a)


def _spectral_kernel(g_ref, gt_ref, h_ref, a_ref, b_ref):
    n2 = a_ref.shape[3]
    for j in range(g_ref.shape[0]):
        xre, xim = _inner_dft(g_ref[j], a_ref[0, 0, j].astype(BF16), a_ref[0, 1, j].astype(BF16))
        hre, him = h_ref[0, j], h_ref[1, j]
        yre = (xre * hre - xim * him).astype(BF16)
        yim = (xre * him + xim * hre).astype(BF16)
        gt = gt_ref[j]
        q1 = _dot(gt, yre)
        q2 = _dot(gt, yim)
        b_ref[0, 0, j] = q1[:n2] + q2[n2:]
        b_ref[0, 1, j] = q2[:n2] - q1[n2:]


def spectral_multiply(gs, gts, hf, a):
    b, _, n1, n2, c = a.shape
    kb = min(SPEC_KB, n1)
    return pl.pallas_call(
        _spectral_kernel,
        grid=(n1 // kb, b),
        in_specs=[pl.BlockSpec((kb, 2 * n2, n2), lambda k, bi: (k, 0, 0)),
                  pl.BlockSpec((kb, 2 * n2, n2), lambda k, bi: (k, 0, 0)),
                  pl.BlockSpec((2, kb, n2, c), lambda k, bi: (0, k, 0, 0)),
                  pl.BlockSpec((1, 2, kb, n2, c), lambda k, bi: (bi, 0, k, 0, 0))],
        out_specs=pl.BlockSpec((1, 2, kb, n2, c), lambda k, bi: (bi, 0, k, 0, 0)),
        out_shape=jax.ShapeDtypeStruct(a.shape, F32),
        compiler_params=_cp("parallel", "arbitrary"),
        name="spectral_multiply",
    )(gs, gts, hf, a)


def _idft_rows_kernel(f_ref, bm_ref, x0_ref, z_ref, inv_ref, bias_ref, o_ref, *, inv_n):
    m, nb, c = o_ref.shape[1:]
    bflat = bm_ref[0].reshape(-1, c).astype(BF16)
    y = _dot(f_ref[...], bflat) * inv_n
    for j in range(nb):
        z = z_ref[0, :, j, :]
        o_ref[0, :, j, :] = x0_ref[0, :, j, :] * (y[j * m:(j + 1) * m] * inv_ref[...] + bias_ref[...] * z)


def idft_rows_gate(fmat, bm, x0, z, inv_norm, bias, n):
    b, _, n1, n2, c = bm.shape
    m = fmat.shape[0]
    nb = min(DFT_COLS // c, n2)
    spread = jnp.kron(fmat, jnp.eye(nb, dtype=fmat.dtype)).reshape(m, nb, 2 * n1 * nb)
    spread = spread.transpose(1, 0, 2).reshape(nb * m, 2 * n1 * nb)
    blk = pl.BlockSpec((1, m, nb, c), lambda bi, j: (bi, 0, j, 0))
    return pl.pallas_call(
        functools.partial(_idft_rows_kernel, inv_n=1.0 / n),
        grid=(b, n2 // nb),
        in_specs=[pl.BlockSpec(spread.shape, lambda bi, j: (0, 0), pipeline_mode=pl.Buffered(1)),
                  pl.BlockSpec((1, 2, n1, nb, c), lambda bi, j: (bi, 0, 0, j, 0)),
                  blk, blk,
                  pl.BlockSpec((1, c), lambda bi, j: (0, 0)),
                  pl.BlockSpec((1, c), lambda bi, j: (0, 0))],
        out_specs=blk,
        out_shape=jax.ShapeDtypeStruct((b, m, n2, c), F32),
        compiler_params=_cp("parallel", "parallel"),
        name="idft_rows_gate",
    )(spread, bm, x0, z, inv_norm, bias)


def dft_tables(l):
    n = 2 * l
    n2 = FFT_N2
    n1 = n // n2
    k1 = np.arange(n1)
    ang1 = 2.0 * np.pi * np.outer(k1, k1) / n1
    f1 = np.concatenate([np.cos(ang1), -np.sin(ang1)], 0)
    f3 = np.concatenate([np.cos(ang1), -np.sin(ang1)], 1)[: n1 // 2]
    m2 = np.arange(n2)
    tw = 2.0 * np.pi * np.outer(k1, m2) / n
    ang2 = 2.0 * np.pi * np.outer(m2, m2) / n2
    twr, twi = jnp.asarray(np.cos(tw), F32)[:, None, :], jnp.asarray(-np.sin(tw), F32)[:, None, :]
    f2r, f2i = jnp.asarray(np.cos(ang2), F32)[None], jnp.asarray(-np.sin(ang2), F32)[None]
    gre, gim = twr * f2r - twi * f2i, twr * f2i + twi * f2r
    gs = jnp.concatenate([gre, gim], 1).astype(BF16)
    gts = jnp.concatenate([gre.transpose(0, 2, 1), gim.transpose(0, 2, 1)], 1).astype(BF16)
    return dict(n=n, n1=n1, n2=n2, f1=jnp.asarray(f1, BF16), f3=jnp.asarray(f3, BF16), gs=gs, gts=gts)


def hyena_long_conv(z, x0, filt2, norm, bias, tabs):
    b, l, c = z.shape
    n, n1, n2 = tabs["n"], tabs["n1"], tabs["n2"]
    hf = filter_spectrum(tabs["gs"], dft_rows(tabs["f1"], filt2.reshape(1, n1, n2, c)))
    a = dft_rows(tabs["f1"][:, : n1 // 2], z.reshape(b, n1 // 2, n2, c))
    bm = spectral_multiply(tabs["gs"], tabs["gts"], hf, a)
    y = idft_rows_gate(tabs["f3"], bm, x0.reshape(b, n1 // 2, n2, c), z.reshape(b, n1 // 2, n2, c),
                       1.0 / norm, bias, n)
    return y.reshape(b, l, c)


def _small_conv_kernel(ff_ref, fi_ref, filt_ref, z_ref, x0_ref, inv_ref, bias_ref, o_ref, *, l):
    ff = ff_ref[...]
    hs = _dot(ff, filt_ref[...].astype(BF16))
    hre, him = hs[:2 * l], hs[2 * l:]
    z = z_ref[0]
    xs = _dot(ff[:, :l], z.astype(BF16))
    xre, xim = xs[:2 * l], xs[2 * l:]
    yre = (xre * hre - xim * him).astype(BF16)
    yim = (xre * him + xim * hre).astype(BF16)
    y = _dot(fi_ref[...], jnp.concatenate([yre, yim], 0)) * (0.5 / l)
    o_ref[0] = (x0_ref[0] * (y * inv_ref[...] + bias_ref[...] * z)).astype(o_ref.dtype)


def small_long_conv(z, x0, filt2, norm, bias):
    b, l, c = z.shape
    k = np.arange(2 * l)
    ang = 2.0 * np.pi * np.outer(k, k) / (2 * l)
    ff = jnp.asarray(np.concatenate([np.cos(ang), -np.sin(ang)], 0), BF16)
    fi = jnp.asarray(np.concatenate([np.cos(ang), -np.sin(ang)], 1)[:l], BF16)
    full = lambda a: pl.BlockSpec(a.shape, lambda bi: (0,) * a.ndim)
    row = pl.BlockSpec((1, l, c), lambda bi: (bi, 0, 0))
    inv = 1.0 / norm
    return pl.pallas_call(
        functools.partial(_small_conv_kernel, l=l),
        grid=(b,),
        in_specs=[full(ff), full(fi), full(filt2), row, row, full(inv), full(bias)],
        out_specs=row,
        out_shape=jax.ShapeDtypeStruct((b, l, c), BF16),
        compiler_params=_cp("parallel"),
        name="small_long_conv",
    )(ff, fi, filt2, z, x0, inv, bias)


def _pool_kernel(p_ref, pp_ref, pn_ref, w_ref, sc_ref, o_ref, buf_ref, *, tm, l):
    _fill_padded(buf_ref, p_ref, pp_ref, pn_ref, tm)
    t = pl.program_id(1) * tm + lax.broadcasted_iota(jnp.int32, (tm, LANES), 0)
    lane = lax.broadcasted_iota(jnp.int32, (tm, LANES), 1)
    first_group = lane < POOL_GROUP
    h0 = POOL_HALO

    def win(col, lo, hi):
        acc = buf_ref[h0 + lo:h0 + lo + tm, col]
        for j in range(lo + 1, hi):
            acc = acc + buf_ref[h0 + j:h0 + j + tm, col]
        return acc

    parts = []
    for half, (wa, wb) in enumerate(((2, 4), (8, 16))):
        col = slice(half * LANES, (half + 1) * LANES)
        sa = win(col, -wa // 2, wa // 2)
        sb = sa + win(col, -wb // 2, -wa // 2) + win(col, wa // 2, wb // 2)
        hw = jnp.where(first_group, wa // 2, wb // 2)
        cnt = jnp.minimum(t + hw, l) - jnp.maximum(t - hw, 0)
        mean = jnp.where(first_group, sa, sb) / cnt.astype(F32)
        parts.append(mean - buf_ref[h0:h0 + tm, col])
    d = jnp.concatenate(parts, axis=-1).astype(BF16)
    o_ref[0] = (_dot(d, w_ref[...]) * sc_ref[...]).astype(o_ref.dtype)


def pool_mix(p, pool_w, pool_scale):
    b, l, wd = p.shape
    tm = min(ROW_TILE, l)
    nblk = l // tm
    wblk = jax.scipy.linalg.block_diag(*[pool_w[g] for g in range(len(POOL_WINDOWS))]).astype(BF16)
    return pl.pallas_call(
        functools.partial(_pool_kernel, tm=tm, l=l),
        grid=(b, nblk),
        in_specs=_halo_specs(tm, wd, nblk) + [pl.BlockSpec((wd, wd), lambda bi, i: (0, 0)),
                                              pl.BlockSpec((1, wd), lambda bi, i: (0, 0))],
        out_specs=pl.BlockSpec((1, tm, wd), lambda bi, i: (bi, i, 0)),
        out_shape=jax.ShapeDtypeStruct((b, l, wd), BF16),
        scratch_shapes=[pltpu.VMEM((tm + 2 * POOL_HALO, wd), F32)],
        compiler_params=_cp("parallel", "parallel"),
        name="pool_mix",
    )(p, p, p, wblk, pool_scale.reshape(1, wd))


def _outproj_kernel(x_ref, hy_ref, pool_ref, att_ref, w_ref, gate_ref, o_ref):
    a, b2 = HY_WIDTH, HY_WIDTH + POOL_WIDTH
    y = (_dot(hy_ref[0].astype(BF16), w_ref[0:a]) + _dot(pool_ref[0], w_ref[a:b2]) + _dot(att_ref[0], w_ref[b2:]))
    o_ref[0] = x_ref[0] + gate_ref[0] * y


def out_projection(x, y_hy, y_pool, y_att, w_out, gate):
    b, l, d = x.shape
    tm = min(ROW_TILE, l)
    row = lambda bi, i: (bi, i, 0)
    return pl.pallas_call(
        _outproj_kernel,
        grid=(b, l // tm),
        in_specs=[pl.BlockSpec((1, tm, d), row),
                  pl.BlockSpec((1, tm, y_hy.shape[2]), row),
                  pl.BlockSpec((1, tm, y_pool.shape[2]), row),
                  pl.BlockSpec((1, tm, y_att.shape[2]), row),
                  pl.BlockSpec(w_out.shape, lambda bi, i: (0, 0)),
                  pl.BlockSpec((1, 1, d), lambda bi, i: (bi, 0, 0))],
        out_specs=pl.BlockSpec((1, tm, d), row),
        out_shape=jax.ShapeDtypeStruct(x.shape, F32),
        compiler_params=_cp("parallel", "parallel"),
        name="out_projection",
    )(x, y_hy, y_pool, y_att, w_out, gate)


FF_CHUNK = 256
MOE_TILE = 1024
MOE_ROWS = (128, 192, 256, 320, 384)


def _swiglu_acc(h, wg_ref, wu_ref, wd_ref, lead):
    ff = wg_ref.shape[-1]
    acc = None
    for f0 in range(0, ff, FF_CHUNK):
        cs = slice(f0, min(f0 + FF_CHUNK, ff))
        g = _dot(h, wg_ref[lead + (slice(None), cs)])
        u = _dot(h, wu_ref[lead + (slice(None), cs)])
        a = (g * jax.nn.sigmoid(g) * u).astype(BF16)
        part = _dot(a, wd_ref[lead + (cs, slice(None))])
        acc = part if acc is None else acc + part
    return acc


def _ffn_kernel(x_ref, g_ref, sh_ref, sc_ref, gate_ref, wg_ref, wu_ref, wd_ref, o_ref):
    x = x_ref[0]
    h = _normmod(x, g_ref[...], sh_ref[0], sc_ref[0]).astype(BF16)
    o_ref[0] = x + gate_ref[0] * _swiglu_acc(h, wg_ref, wu_ref, wd_ref, ())


def ffn_dense(x, g, shift, scale, gate, wg, wu, wd):
    b, l, d = x.shape
    tm = min(ROW_TILE, l)
    row = lambda bi, i: (bi, i, 0)
    vec = lambda bi, i: (bi, 0, 0)
    wspec = lambda w: pl.BlockSpec(w.shape, lambda bi, i: (0, 0), pipeline_mode=pl.Buffered(1))
    return pl.pallas_call(
        _ffn_kernel,
        grid=(b, l // tm),
        in_specs=[pl.BlockSpec((1, tm, d), row),
                  pl.BlockSpec((1, d), lambda bi, i: (0, 0)),
                  pl.BlockSpec((1, 1, d), vec), pl.BlockSpec((1, 1, d), vec), pl.BlockSpec((1, 1, d), vec),
                  wspec(wg), wspec(wu), wspec(wd)],
        out_specs=pl.BlockSpec((1, tm, d), row),
        out_shape=jax.ShapeDtypeStruct(x.shape, F32),
        compiler_params=_cp("parallel", "parallel"),
        name="ffn_dense",
    )(x, g, shift, scale, gate, wg, wu, wd)


def _mix_ffn_kernel(x_ref, hy_ref, pool_ref, att_ref, wo_ref, gate1_ref, g_ref, sh_ref, sc_ref, gate2_ref,
                    wg_ref, wu_ref, wd_ref, o_ref):
    a, b2 = HY_WIDTH, HY_WIDTH + POOL_WIDTH
    y = (_dot(hy_ref[0].astype(BF16), wo_ref[0:a]) + _dot(pool_ref[0], wo_ref[a:b2]) + _dot(att_ref[0], wo_ref[b2:]))
    x = x_ref[0] + gate1_ref[0] * y
    h = _normmod(x, g_ref[...], sh_ref[0], sc_ref[0]).astype(BF16)
    o_ref[0] = x + gate2_ref[0] * _swiglu_acc(h, wg_ref, wu_ref, wd_ref, ())


def mix_ffn(x, y_hy, y_pool, y_att, w_out, gate1, g, shift, scale, gate2, wg, wu, wd):
    b, l, d = x.shape
    tm = min(ROW_TILE, l)
    row = lambda bi, i: (bi, i, 0)
    vec = lambda bi, i: (bi, 0, 0)
    wspec = lambda w: pl.BlockSpec(w.shape, lambda bi, i: (0, 0), pipeline_mode=pl.Buffered(1))
    vspec = pl.BlockSpec((1, 1, d), vec)
    return pl.pallas_call(
        _mix_ffn_kernel,
        grid=(b, l // tm),
        in_specs=[pl.BlockSpec((1, tm, d), row),
                  pl.BlockSpec((1, tm, y_hy.shape[2]), row),
                  pl.BlockSpec((1, tm, y_pool.shape[2]), row),
                  pl.BlockSpec((1, tm, y_att.shape[2]), row),
                  wspec(w_out), vspec,
                  pl.BlockSpec((1, d), lambda bi, i: (0, 0)),
                  vspec, vspec, vspec,
                  wspec(wg), wspec(wu), wspec(wd)],
        out_specs=pl.BlockSpec((1, tm, d), row),
        out_shape=jax.ShapeDtypeStruct(x.shape, F32),
        compiler_params=_cp("parallel", "parallel"),
        name="mix_ffn",
    )(x, y_hy, y_pool, y_att, w_out, gate1, g, shift, scale, gate2, wg, wu, wd)


def _router_kernel(x_ref, g_ref, sh_ref, sc_ref, wr_ref, tri_ref, comb_ref, rank_ref):
    h = _normmod(x_ref[0], g_ref[...], sh_ref[0], sc_ref[0])
    logits = _dot(h, wr_ref[...], precision=HIGHEST)
    lane = lax.broadcasted_iota(jnp.int32, logits.shape, 1)
    neg = jnp.float32(-jnp.inf)
    lg = jnp.where(lane < N_EXPERTS, logits, neg)
    m1 = jnp.max(lg, axis=-1, keepdims=True)
    i1 = jnp.min(jnp.where(lg == m1, lane, LANES), axis=-1, keepdims=True)
    lg2 = jnp.where(lane == i1, neg, lg)
    m2 = jnp.max(lg2, axis=-1, keepdims=True)
    i2 = jnp.min(jnp.where(lg2 == m2, lane, LANES), axis=-1, keepdims=True)
    e = jnp.exp(m2 - m1)
    g1 = 1.0 / (1.0 + e)
    comb_ref[0] = jnp.where(lane == i1, g1, jnp.where(lane == i2, e * g1, 0.0))
    routed = (lane == i1) | (lane == i2)
    before = _dot(tri_ref[...], jnp.where(routed, 1.0, 0.0).astype(BF16))
    rank_ref[0] = jnp.where(routed, before, -1.0)


def moe_route(x, g, shift, scale, w_router):
    b, l, d = x.shape
    tm = min(MOE_TILE, l)
    wr = jnp.concatenate([w_router, jnp.zeros((d, LANES - N_EXPERTS), F32)], -1)
    tri = jnp.asarray(np.tril(np.ones((tm, tm), np.float32), -1), BF16)
    row = lambda bi, i: (bi, i, 0)
    vec = lambda bi, i: (bi, 0, 0)
    return pl.pallas_call(
        _router_kernel,
        grid=(b, l // tm),
        in_specs=[pl.BlockSpec((1, tm, d), row),
                  pl.BlockSpec((1, d), lambda bi, i: (0, 0)),
                  pl.BlockSpec((1, 1, d), vec), pl.BlockSpec((1, 1, d), vec),
                  pl.BlockSpec((d, LANES), lambda bi, i: (0, 0)),
                  pl.BlockSpec((tm, tm), lambda bi, i: (0, 0))],
        out_specs=[pl.BlockSpec((1, tm, LANES), row)] * 2,
        out_shape=[jax.ShapeDtypeStruct((b, l, LANES), F32)] * 2,
        compiler_params=_cp("parallel", "parallel"),
        name="moe_router",
    )(x, g, shift, scale, wr, tri)


def _moe_kernel(x_ref, g_ref, sh_ref, sc_ref, gate_ref, comb_ref, wg_ref, wu_ref, wd_ref, o_ref, acc_ref):
    e = pl.program_id(2)
    x = x_ref[0]
    h = _normmod(x, g_ref[...], sh_ref[0], sc_ref[0]).astype(BF16)
    comb = comb_ref[0]
    lane = lax.broadcasted_iota(jnp.int32, comb.shape, 1)
    ce = jnp.sum(jnp.where(lane == e, comb, 0.0), axis=-1, keepdims=True)
    y = ce * _swiglu_acc(h, wg_ref, wu_ref, wd_ref, (0, 0))

    @pl.when(e == 0)
    def _():
        acc_ref[...] = y

    @pl.when(e > 0)
    def _():
        acc_ref[...] += y

    @pl.when(e == N_EXPERTS - 1)
    def _():
        o_ref[0] = x + gate_ref[0] * acc_ref[...]


def moe_dense(x, g, shift, scale, gate, comb, wg, wu, wd, li):
    b, l, d = x.shape
    tm = min(ROW_TILE, l)
    ff = wg.shape[-1]
    row = lambda bi, i, e: (bi, i, 0)
    vec = lambda bi, i, e: (bi, 0, 0)
    return pl.pallas_call(
        _moe_kernel,
        grid=(b, l // tm, N_EXPERTS),
        in_specs=[pl.BlockSpec((1, tm, d), row),
                  pl.BlockSpec((1, d), lambda bi, i, e: (0, 0)),
                  pl.BlockSpec((1, 1, d), vec), pl.BlockSpec((1, 1, d), vec), pl.BlockSpec((1, 1, d), vec),
                  pl.BlockSpec((1, tm, LANES), row),
                  pl.BlockSpec((1, 1, d, ff), lambda bi, i, e: (li, e, 0, 0)),
                  pl.BlockSpec((1, 1, d, ff), lambda bi, i, e: (li, e, 0, 0)),
                  pl.BlockSpec((1, 1, ff, d), lambda bi, i, e: (li, e, 0, 0))],
        out_specs=pl.BlockSpec((1, tm, d), row),
        out_shape=jax.ShapeDtypeStruct(x.shape, F32),
        scratch_shapes=[pltpu.VMEM((tm, d), F32)],
        compiler_params=_cp("parallel", "parallel", "arbitrary"),
        name="moe_dense",
    )(x, g, shift, scale, gate, comb, wg, wu, wd)


def _moe_sparse_kernel(cnt_ref, x_ref, g_ref, sh_ref, sc_ref, gate_ref, comb_ref, rank_ref, rankt_ref,
                       wg_ref, wu_ref, wd_ref, *rest, t, sizes):
    fg_ref = rest[0] if len(rest) == 3 else None
    o_ref, h_scr = rest[-2:]
    i, e = pl.program_id(0), pl.program_id(1)
    cnt = cnt_ref[i * N_EXPERTS + e]

    @pl.when(e == 0)
    def _():
        x = x_ref[0]
        h_scr[...] = _normmod(x, g_ref[...], sh_ref[0], sc_ref[0]).astype(BF16)
        o_ref[0] = x

    rrow = rankt_ref[0, pl.ds(e, 1), :]
    lane = lax.broadcasted_iota(jnp.int32, (t, LANES), 1)
    rcol = jnp.sum(jnp.where(lane == e, rank_ref[0], 0.0), axis=-1, keepdims=True)
    ccol = jnp.sum(jnp.where(lane == e, comb_ref[0], 0.0), axis=-1, keepdims=True)

    def block(base, r):
        rid = (base + lax.broadcasted_iota(jnp.int32, (r, 1), 0)).astype(F32)
        onehot = jnp.where(rrow == rid, 1.0, 0.0).astype(BF16)
        xg = _dot(onehot, h_scr[...]).astype(BF16)
        y = _swiglu_acc(xg, wg_ref, wu_ref, wd_ref, (0, 0)).astype(BF16)
        cid = (base + lax.broadcasted_iota(jnp.int32, (1, r), 1)).astype(F32)
        onehot_t = jnp.where(rcol == cid, 1.0, 0.0).astype(BF16)
        o_ref[0] += (gate_ref[0] * ccol) * _dot(onehot_t, y)

    big = sizes[-1]
    nfull = cnt // big

    def full(k, carry):
        block(k * big, big)
        return carry
    lax.fori_loop(0, nfull, full, 0)
    base = nfull * big
    rem = cnt - base
    for lo, r in zip((0,) + sizes[:-1], sizes):
        @pl.when((rem > lo) & (rem <= r))
        def _(r=r):
            block(base, r)

    if fg_ref is not None:
        @pl.when(e == N_EXPERTS - 1)
        def _():
            o = o_ref[0]
            o_ref[0] = o * lax.rsqrt(jnp.mean(o * o, axis=-1, keepdims=True) + EPS) * fg_ref[...]


def moe_sparse(x, g, shift, scale, gate, comb, rank, wg, wu, wd, li, final_g=None):
    b, l, d = x.shape
    t = MOE_TILE
    tpb = l // t
    n_tiles = b * tpb
    ff = wg.shape[-1]
    rk = rank[..., :N_EXPERTS].reshape(n_tiles, t, N_EXPERTS)
    rank_t = rk.transpose(0, 2, 1)
    cnt = jnp.sum(rk >= 0, axis=1).astype(jnp.int32).reshape(n_tiles * N_EXPERTS)
    row = lambda i, e, c: (i // tpb, i % tpb, 0)
    vec = lambda i, e, c: (i // tpb, 0, 0)
    grid_spec = pltpu.PrefetchScalarGridSpec(
        num_scalar_prefetch=1,
        grid=(n_tiles, N_EXPERTS),
        in_specs=[pl.BlockSpec((1, t, d), row, pipeline_mode=pl.Buffered(1)),
                  pl.BlockSpec((1, d), lambda i, e, c: (0, 0)),
                  pl.BlockSpec((1, 1, d), vec), pl.BlockSpec((1, 1, d), vec), pl.BlockSpec((1, 1, d), vec),
                  pl.BlockSpec((1, t, LANES), row, pipeline_mode=pl.Buffered(1)),
                  pl.BlockSpec((1, t, LANES), row, pipeline_mode=pl.Buffered(1)),
                  pl.BlockSpec((1, N_EXPERTS, t), lambda i, e, c: (i, 0, 0)),
                  pl.BlockSpec((1, 1, d, ff), lambda i, e, c: (li, e, 0, 0)),
                  pl.BlockSpec((1, 1, d, ff), lambda i, e, c: (li, e, 0, 0)),
                  pl.BlockSpec((1, 1, ff, d), lambda i, e, c: (li, e, 0, 0))]
        + ([] if final_g is None else [pl.BlockSpec((1, d), lambda i, e, c: (0, 0))]),
        out_specs=pl.BlockSpec((1, t, d), row),
        scratch_shapes=[pltpu.VMEM((t, d), BF16)],
    )
    return pl.pallas_call(
        functools.partial(_moe_sparse_kernel, t=t, sizes=MOE_ROWS),
        grid_spec=grid_spec,
        out_shape=jax.ShapeDtypeStruct(x.shape, F32),
        compiler_params=_cp("parallel", "arbitrary"),
        name="moe_sparse",
    )(cnt, x, g, shift, scale, gate, comb, rank, rank_t, wg, wu, wd, *(() if final_g is None else (final_g,)))


def _final_kernel(x_ref, g_ref, o_ref):
    x = x_ref[0]
    o_ref[0] = x * lax.rsqrt(jnp.mean(x * x, axis=-1, keepdims=True) + EPS) * g_ref[...]


def final_norm(x, g):
    b, l, d = x.shape
    tm = min(2 * ROW_TILE, l)
    return pl.pallas_call(
        _final_kernel,
        grid=(b, l // tm),
        in_specs=[pl.BlockSpec((1, tm, d), lambda bi, i: (bi, i, 0)),
                  pl.BlockSpec((1, d), lambda bi, i: (0, 0))],
        out_specs=pl.BlockSpec((1, tm, d), lambda bi, i: (bi, i, 0)),
        out_shape=jax.ShapeDtypeStruct(x.shape, F32),
        compiler_params=_cp("parallel", "parallel"),
        name="final_norm",
    )(x, g)


def kernel(x, c, ctx, c_ctx, norm1_g, norm2_g, w_ada, b_ada, w_in, hy_short_w, hy_short_b, filt_w1, filt_b1, filt_freq1, filt_w2, filt_b2, filt_freq2, filt_w3, hy_bias, pool_w, pool_scale, q_norm_g, w_uq, kv_norm_g, w_ukv, w_out, ffn_wg, ffn_wu, ffn_wd, moe_router, moe_wg, moe_wu, moe_wd, final_g):
    b, l, d = x.shape
    lc = ctx.shape[1]
    depth = w_ada.shape[0]

    cc = jnp.zeros((8, d), F32).at[:b].set(c).at[b].set(c_ctx)
    mods = ada_vectors(cc, w_ada, b_ada)

    rope_l = rope_tables(l, True)
    rope_c = rope_tables(lc, False)
    ftab_l, ftab_c = filter_tables(l), filter_tables(lc)
    dtabs = dft_tables(l)
    moe_w = (moe_wg.astype(BF16), moe_wu.astype(BF16), moe_wd.astype(BF16))

    xl, xc = x, ctx
    for layer in range(depth):
        last = layer == depth - 1
        mod = mods[layer].reshape(8, 6, 1, d)
        mod_l = [mod[:b, j] for j in range(6)]
        mod_c = [jnp.broadcast_to(mod[b:b + 1, j], (b, 1, d)) for j in range(6)]
        g1 = norm1_g[layer].reshape(1, d)
        g2 = norm2_g[layer].reshape(1, d)
        w_in_l = w_in[layer].astype(BF16)
        w_out_l = w_out[layer].astype(BF16)
        mw = mla_weights(q_norm_g[layer], w_uq[layer], kv_norm_g[layer], w_ukv[layer])
        fw = (filt_w1[layer], filt_b1[layer], filt_freq1[layer], filt_w2[layer], filt_b2[layer],
              filt_freq2[layer], filt_w3[layer])
        bias = hy_bias[layer].reshape(1, HY_WIDTH)

        def channel_mix(xs, md, final=False):
            i = layer // 2
            fg = final_g.reshape(1, d)
            if layer % 2 == 1 and xs.shape[1] % MOE_TILE == 0:
                comb, rank = moe_route(xs, g2, md[3], md[4], moe_router[i])
                return moe_sparse(xs, g2, md[3], md[4], md[5], comb, rank, *moe_w, i, fg if final else None)
            if layer % 2 == 0:
                out = ffn_dense(xs, g2, md[3], md[4], md[5], ffn_wg[i].astype(BF16),
                                ffn_wu[i].astype(BF16), ffn_wd[i].astype(BF16))
            else:
                comb, _ = moe_route(xs, g2, md[3], md[4], moe_router[i])
                out = moe_dense(xs, g2, md[3], md[4], md[5], comb, *moe_w, i)
            return final_norm(out, fg) if final else out

        hy_l, pool_l, q_l, kt_l, v_l = in_projection(xl, g1, mod_l[0], mod_l[1], w_in_l, *rope_l, mw)
        hy_c, pool_c, q_c, kt_c, v_c = in_projection(xc, g1, mod_c[0], mod_c[1], w_in_l, *rope_c, mw)

        att_l = attention(q_l, kt_c, v_c, kt_l, v_l)
        z_l, x0_l = hyena_pre(hy_l, hy_short_w[layer], hy_short_b[layer])
        filt_l, nrm_l = hyena_filter(l, ftab_l, *fw)
        yhy_l = hyena_long_conv(z_l, x0_l, filt_l, nrm_l, bias, dtabs)
        ypool_l = pool_mix(pool_l, pool_w[layer], pool_scale[layer])
        dense = layer % 2 == 0
        if dense:
            fw3 = tuple(w[layer // 2].astype(BF16) for w in (ffn_wg, ffn_wu, ffn_wd))
            mix = lambda xs, ys, md: mix_ffn(xs, *ys, w_out_l, md[2], g2, md[3], md[4], md[5], *fw3)
        else:
            xl = out_projection(xl, yhy_l, ypool_l, att_l, w_out_l, mod_l[2])

        if not last:
            att_c = attention(q_c, kt_c, v_c)
            z_c, x0_c = hyena_pre(hy_c, hy_short_w[layer], hy_short_b[layer])
            filt_c, nrm_c = hyena_filter(lc, ftab_c, *fw)
            yhy_c = small_long_conv(z_c, x0_c, filt_c, nrm_c, bias)
            ypool_c = pool_mix(pool_c, pool_w[layer], pool_scale[layer])
            if dense:
                xc = mix(xc, (yhy_c, ypool_c, att_c), mod_c)
            else:
                xc = out_projection(xc, yhy_c, ypool_c, att_c, w_out_l, mod_c[2])
                xc = channel_mix(xc, mod_c)

        if dense:
            xl = mix(xl, (yhy_l, ypool_l, att_l), mod_l)
            xl = final_norm(xl, final_g.reshape(1, d)) if last else xl
        else:
            xl = channel_mix(xl, mod_l, final=last)
    return xl
```

```python
import functools
import math

import jax
import jax.numpy as jnp
import numpy as np
from jax import lax
from jax.experimental import pallas as pl
from jax.experimental.pallas import tpu as pltpu

F32 = jnp.float32
BF16 = jnp.bfloat16
HIGHEST = lax.Precision.HIGHEST

EPS = 1e-6
GRID_W = 64
HY_WIDTH = 256
POOL_WINDOWS = (2, 4, 8, 16)
POOL_WIDTH = 256
POOL_GROUP = 64
POOL_HALO = 8
MLA_HEADS = 8
QK_NOPE = 64
QK_ROPE = 32
V_DIM = 64
Q_LORA = 384
KV_LORA = 256
HEAD_PAD = 128
SM_SCALE = (QK_NOPE + QK_ROPE) ** -0.5
ROPE_THETA = 10000.0
FILT_EMB = 33
FILT_BANDS = 16
DECAY_TARGET = 1e-2
FAST_DECAY_PCT = 0.3
SLOW_DECAY_PCT = 1.5
N_EXPERTS = 8
LANES = 128
FFT_N2 = 128
VMEM_LIMIT = 56 * 2 ** 20
ROW_TILE = 1024
DFT_COLS = 2048
ADA_COLS = 1536


def _cp(*sem):
    return pltpu.CompilerParams(dimension_semantics=sem, vmem_limit_bytes=VMEM_LIMIT)


def _dot(a, b, **kw):
    return jnp.dot(a, b, preferred_element_type=F32, **kw)


def _normmod(x, g, shift, scale):
    ms = jnp.mean(x * x, axis=-1, keepdims=True)
    return x * lax.rsqrt(ms + EPS) * (g * (1.0 + scale)) + shift


def _ada_kernel(c_ref, w_ref, b_ref, o_ref):
    c = c_ref[...]
    s = c * jax.nn.sigmoid(c)
    o_ref[0] = _dot(s, w_ref[0], precision=HIGHEST) + b_ref[0]


def ada_vectors(cc, w_ada, b_ada):
    depth, d, n = w_ada.shape
    tn = ADA_COLS
    return pl.pallas_call(
        _ada_kernel,
        grid=(depth, n // tn),
        in_specs=[pl.BlockSpec((8, d), lambda l, j: (0, 0)),
                  pl.BlockSpec((1, d, tn), lambda l, j: (l, 0, j)),
                  pl.BlockSpec((1, 1, tn), lambda l, j: (l, 0, j))],
        out_specs=pl.BlockSpec((1, 8, tn), lambda l, j: (l, 0, j)),
        out_shape=jax.ShapeDtypeStruct((depth, 8, n), F32),
        compiler_params=_cp("parallel", "parallel"),
        name="ada_vectors",
    )(cc, w_ada, b_ada.reshape(depth, 1, n))


IN_HY, IN_POOL = 3 * HY_WIDTH, POOL_WIDTH
IN_OFFS = tuple(int(v) for v in np.cumsum((0, IN_HY, IN_POOL, Q_LORA, KV_LORA, QK_ROPE)))


def _inproj_kernel(x_ref, g_ref, sh_ref, sc_ref, w_ref, c_ref, s_ref, wqa_ref, wqb_ref, wk_ref, wv_ref,
                   pa_ref, pb_ref, hy_ref, pool_ref, q_ref, kt_ref, v_ref):
    h = _normmod(x_ref[0], g_ref[...], sh_ref[0], sc_ref[0]).astype(BF16)
    y = _dot(h, w_ref[...])
    o = IN_OFFS
    hy_ref[0] = y[:, o[0]:o[1]]
    pool_ref[0] = y[:, o[1]:o[2]]
    cq, ckv = y[:, o[2]:o[3]], y[:, o[3]:o[4]]
    kpe = y[:, o[4]:o[5]].astype(BF16)
    cqn = (cq * lax.rsqrt(jnp.mean(cq * cq, axis=-1, keepdims=True) + EPS)).astype(BF16)
    ckvn = (ckv * lax.rsqrt(jnp.mean(ckv * ckv, axis=-1, keepdims=True) + EPS)).astype(BF16)
    cs, sn = c_ref[...], s_ref[...]
    qa = _dot(cqn, wqa_ref[...])
    qb = _dot(cqn, wqb_ref[...])
    kn = _dot(ckvn, wk_ref[...])
    vv = _dot(ckvn, wv_ref[...])
    kpa = _dot(kpe, pa_ref[...])
    kpb = _dot(kpe, pb_ref[...])
    kpe_rot = kpa * cs + kpb * sn
    lane = lax.broadcasted_iota(jnp.int32, (1, HEAD_PAD), 1)
    for hd in range(MLA_HEADS):
        sl = slice(hd * HEAD_PAD, (hd + 1) * HEAD_PAD)
        q_ref[0, hd] = (qa[:, sl] * cs + qb[:, sl] * sn).astype(BF16)
        kt_ref[0, hd] = (kn[:, sl] + kpe_rot).T.astype(BF16)
        one_lane = V_DIM if hd % 2 == 0 else 0
        v_ref[0, hd] = jnp.where(lane == one_lane, 1.0, vv[:, sl]).astype(BF16)


def in_projection(x, g, shift, scale, w_in, ctab, stab, wts):
    b, l, d = x.shape
    tm = min(ROW_TILE, l)
    hw = MLA_HEADS
    row = lambda bi, i: (bi, i, 0)
    vec = lambda bi, i: (bi, 0, 0)
    full = lambda a: pl.BlockSpec(a.shape, lambda bi, i: (0,) * a.ndim)
    return pl.pallas_call(
        _inproj_kernel,
        grid=(b, l // tm),
        in_specs=[pl.BlockSpec((1, tm, d), row),
                  pl.BlockSpec((1, d), lambda bi, i: (0, 0)),
                  pl.BlockSpec((1, 1, d), vec),
                  pl.BlockSpec((1, 1, d), vec),
                  full(w_in),
                  pl.BlockSpec((tm, HEAD_PAD), lambda bi, i: (i, 0)),
                  pl.BlockSpec((tm, HEAD_PAD), lambda bi, i: (i, 0))] + [full(w) for w in wts],
        out_specs=[pl.BlockSpec((1, tm, IN_HY), row),
                   pl.BlockSpec((1, tm, IN_POOL), row),
                   pl.BlockSpec((1, hw, tm, HEAD_PAD), lambda bi, i: (bi, 0, i, 0)),
                   pl.BlockSpec((1, hw, HEAD_PAD, tm), lambda bi, i: (bi, 0, 0, i)),
                   pl.BlockSpec((1, hw, tm, HEAD_PAD), lambda bi, i: (bi, 0, i, 0))],
        out_shape=[jax.ShapeDtypeStruct((b, l, IN_HY), F32),
                   jax.ShapeDtypeStruct((b, l, IN_POOL), F32),
                   jax.ShapeDtypeStruct((b, hw, l, HEAD_PAD), BF16),
                   jax.ShapeDtypeStruct((b, hw, HEAD_PAD, l), BF16),
                   jax.ShapeDtypeStruct((b, hw, l, HEAD_PAD), BF16)],
        compiler_params=_cp("parallel", "parallel"),
        name="in_projection",
    )(x, g, shift, scale, w_in, ctab, stab, *wts)


def mla_weights(q_norm_g, w_uq, kv_norm_g, w_ukv):
    hw, dq = MLA_HEADS, QK_NOPE + QK_ROPE
    half = QK_ROPE // 2
    wq = (w_uq * q_norm_g[:, None] * (SM_SCALE * math.log2(math.e))).reshape(Q_LORA, hw, dq)
    pad = jnp.zeros((Q_LORA, hw, HEAD_PAD - dq), F32)
    wqa = jnp.concatenate([wq, pad], -1)
    swap = jnp.concatenate([wq[..., QK_NOPE + half:], wq[..., QK_NOPE:QK_NOPE + half]], -1)
    wqb = jnp.concatenate([jnp.zeros((Q_LORA, hw, QK_NOPE), F32), swap, pad], -1)
    wkv = (w_ukv * kv_norm_g[:, None]).reshape(KV_LORA, hw, QK_NOPE + V_DIM)
    z64 = jnp.zeros((KV_LORA, hw, 64), F32)
    wk = jnp.concatenate([wkv[..., :QK_NOPE], z64], -1)
    v = wkv[..., QK_NOPE:]
    wv = jnp.where((jnp.arange(hw) % 2 == 0)[None, :, None],
                   jnp.concatenate([v, z64], -1), jnp.concatenate([z64, v], -1))
    eye = np.eye(QK_ROPE, dtype=np.float32)
    pa = np.zeros((QK_ROPE, HEAD_PAD), np.float32)
    pa[:, QK_NOPE:QK_NOPE + QK_ROPE] = eye
    pb = np.zeros((QK_ROPE, HEAD_PAD), np.float32)
    pb[:, QK_NOPE:QK_NOPE + QK_ROPE] = np.concatenate([eye[:, half:], eye[:, :half]], 1)
    flat = lambda w: w.reshape(w.shape[0], hw * HEAD_PAD).astype(BF16)
    return (flat(wqa), flat(wqb), flat(wk), flat(wv), jnp.asarray(pa, BF16), jnp.asarray(pb, BF16))


def rope_tables(l, with_rope):
    ctab = np.zeros((l, HEAD_PAD), np.float32)
    stab = np.zeros((l, HEAD_PAD), np.float32)
    ctab[:, :QK_NOPE + QK_ROPE] = 1.0
    if not with_rope:
        return jnp.asarray(ctab), jnp.asarray(stab)
    n_freq = QK_ROPE // 4
    inv_freq = 1.0 / (ROPE_THETA ** (jnp.arange(n_freq, dtype=F32) / n_freq))
    rows = l // GRID_W
    row = jnp.repeat(jnp.arange(rows, dtype=F32), GRID_W)
    col = jnp.tile(jnp.arange(GRID_W, dtype=F32), rows)
    ang = jnp.concatenate([row[:, None] * inv_freq, col[:, None] * inv_freq], axis=-1)
    cos, sin = jnp.cos(ang), jnp.sin(ang)
    one = jnp.ones((l, QK_NOPE), F32)
    zero = jnp.zeros((l, QK_NOPE), F32)
    tail = jnp.zeros((l, HEAD_PAD - QK_NOPE - QK_ROPE), F32)
    return (jnp.concatenate([one, cos, cos, tail], -1), jnp.concatenate([zero, -sin, sin, tail], -1))


ATT_TQ = 1024
ATT_RB = 1024
ATT_CK = 2048


def _row_max(s):
    mp = s[:, 0:LANES]
    for j in range(1, s.shape[1] // LANES):
        mp = jnp.maximum(mp, s[:, j * LANES:(j + 1) * LANES])
    return jnp.max(mp, axis=-1, keepdims=True)


def _attn_kernel(q_ref, kct_ref, vc_ref, *rest, n_chunks, ck, tq):
    if n_chunks:
        kt_ref, v_ref, o_ref = rest
    else:
        (o_ref,) = rest
    rb = min(ATT_RB, tq)
    chains = [(hh, r) for hh in range(2) for r in range(tq // rb)]
    lane = lax.broadcasted_iota(jnp.int32, (1, HEAD_PAD), 1)

    def q_rows(hh, r):
        return q_ref[0, hh, r * rb:(r + 1) * rb, :]

    state = []
    for hh, r in chains:
        s = _dot(q_rows(hh, r), kct_ref[0, hh])
        m = _row_max(s)
        p = jnp.exp2(s - m).astype(BF16)
        state += [m, _dot(p, vc_ref[0, hh])]

    if n_chunks:
        def body(c, state):
            off = pl.multiple_of(c * ck, ck)
            out = []
            for i, (hh, r) in enumerate(chains):
                m, acc = state[2 * i], state[2 * i + 1]
                s = _dot(q_rows(hh, r), kt_ref[0, hh, :, pl.ds(off, ck)])
                m_new = jnp.maximum(m, _row_max(s))
                alpha = jnp.exp2(m - m_new)
                p = jnp.exp2(s - m_new).astype(BF16)
                out += [m_new, acc * alpha + _dot(p, v_ref[0, hh, pl.ds(off, ck), :])]
            return tuple(out)
        state = lax.fori_loop(0, n_chunks, body, tuple(state), unroll=4 if n_chunks % 4 == 0 else 1)

    for r in range(tq // rb):
        outs = []
        for hh in range(2):
            acc = state[2 * chains.index((hh, r)) + 1]
            one_lane = V_DIM if hh == 0 else 0
            denom = jnp.sum(jnp.where(lane == one_lane, acc, 0.0), axis=-1, keepdims=True)
            outs.append(acc / denom)
        o_ref[0, r * rb:(r + 1) * rb, :] = jnp.where(lane < V_DIM, outs[0], outs[1]).astype(o_ref.dtype)


def attention(q, kct, vc, kt=None, v=None):
    b, hw, lq, _ = q.shape
    lc = vc.shape[2]
    tq = min(ATT_TQ, lq)
    ck = ATT_CK if kt is None else min(ATT_CK, kt.shape[3])
    n_chunks = 0 if kt is None else kt.shape[3] // ck
    in_specs = [pl.BlockSpec((1, 2, tq, HEAD_PAD), lambda bi, hp, i: (bi, hp, i, 0)),
                pl.BlockSpec((1, 2, HEAD_PAD, lc), lambda bi, hp, i: (bi, hp, 0, 0)),
                pl.BlockSpec((1, 2, lc, HEAD_PAD), lambda bi, hp, i: (bi, hp, 0, 0))]
    args = [q, kct, vc]
    if n_chunks:
        l = kt.shape[3]
        in_specs += [pl.BlockSpec((1, 2, HEAD_PAD, l), lambda bi, hp, i: (bi, hp, 0, 0)),
                     pl.BlockSpec((1, 2, l, HEAD_PAD), lambda bi, hp, i: (bi, hp, 0, 0))]
        args += [kt, v]
    return pl.pallas_call(
        functools.partial(_attn_kernel, n_chunks=n_chunks, ck=ck, tq=tq),
        grid=(b, hw // 2, lq // tq),
        in_specs=in_specs,
        out_specs=pl.BlockSpec((1, tq, HEAD_PAD), lambda bi, hp, i: (bi, i, hp)),
        out_shape=jax.ShapeDtypeStruct((b, lq, hw * V_DIM), BF16),
        compiler_params=_cp("parallel", "parallel", "arbitrary"),
        name="attention",
    )(*args)


def _halo_specs(tm, width, nblk):
    r = tm // POOL_HALO
    last = nblk * r - 1
    return [pl.BlockSpec((1, tm, width), lambda bi, i: (bi, i, 0)),
            pl.BlockSpec((1, POOL_HALO, width), lambda bi, i: (bi, jnp.maximum(i * r - 1, 0), 0)),
            pl.BlockSpec((1, POOL_HALO, width), lambda bi, i: (bi, jnp.minimum((i + 1) * r, last), 0))]


def _fill_padded(buf_ref, cur_ref, prev_ref, next_ref, tm):
    i = pl.program_id(1)
    first = i == 0
    last = i == pl.num_programs(1) - 1
    buf_ref[0:POOL_HALO] = jnp.where(first, 0.0, prev_ref[0])
    buf_ref[POOL_HALO:POOL_HALO + tm] = cur_ref[0]
    buf_ref[POOL_HALO + tm:] = jnp.where(last, 0.0, next_ref[0])


def _hyena_pre_kernel(u_ref, up_ref, un_ref, w_ref, b_ref, z_ref, x0_ref, buf_ref, *, tm):
    _fill_padded(buf_ref, u_ref, up_ref, un_ref, tm)
    w = w_ref[...]
    uc = (buf_ref[POOL_HALO - 1:POOL_HALO - 1 + tm] * w[0:1] + buf_ref[POOL_HALO:POOL_HALO + tm] * w[1:2]
          + buf_ref[POOL_HALO + 1:POOL_HALO + 1 + tm] * w[2:3] + b_ref[...])
    x0_ref[0] = uc[:, :HY_WIDTH]
    z_ref[0] = uc[:, HY_WIDTH:2 * HY_WIDTH] * uc[:, 2 * HY_WIDTH:]


def hyena_pre(u, sw, sb):
    b, l, wd = u.shape
    tm = min(ROW_TILE, l)
    nblk = l // tm
    return pl.pallas_call(
        functools.partial(_hyena_pre_kernel, tm=tm),
        grid=(b, nblk),
        in_specs=_halo_specs(tm, wd, nblk) + [pl.BlockSpec((3, wd), lambda bi, i: (0, 0)),
                                              pl.BlockSpec((1, wd), lambda bi, i: (0, 0))],
        out_specs=[pl.BlockSpec((1, tm, HY_WIDTH), lambda bi, i: (bi, i, 0))] * 2,
        out_shape=[jax.ShapeDtypeStruct((b, l, HY_WIDTH), F32)] * 2,
        scratch_shapes=[pltpu.VMEM((tm + 2 * POOL_HALO, wd), F32)],
        compiler_params=_cp("parallel", "parallel"),
        name="hyena_pre",
    )(u, u, u, sw, sb.reshape(1, wd))


FILT_TILE = 1024


def _filter_kernel(feat_ref, tdec_ref, w1_ref, b1_ref, f1_ref, w2_ref, b2_ref, f2_ref, w3_ref, dl_ref,
                   filt_ref, norm_ref, *, l, tr):
    i = pl.program_id(0)
    half = tr // 2
    h = jnp.sin(f1_ref[...] * (_dot(feat_ref[...], w1_ref[...], precision=HIGHEST) + b1_ref[...]))
    h = jnp.sin(f2_ref[...] * (_dot(h, w2_ref[...], precision=HIGHEST) + b2_ref[...]))
    h = _dot(h, w3_ref[...], precision=HIGHEST)
    dl = jnp.abs(dl_ref[...])
    total = jnp.zeros((1, HY_WIDTH), F32)
    for p in range(2):
        rows = slice(p * half, (p + 1) * half)
        hp = h[:, p * 2 * HY_WIDTH:(p + 1) * 2 * HY_WIDTH]
        decay = jnp.exp(-tdec_ref[rows] * dl)
        n = i * tr + p * half + lax.broadcasted_iota(jnp.int32, (half, 1), 0)
        f = jnp.where(n < l, hp[:, :HY_WIDTH], hp[:, HY_WIDTH:]) * decay
        f = jnp.where(n == l, 0.0, f)
        filt_ref[rows] = f
        total = total + jnp.sum(jnp.abs(f), axis=0, keepdims=True)

    @pl.when(i == 0)
    def _():
        norm_ref[...] = jnp.zeros_like(norm_ref)
    norm_ref[...] += total


def filter_tables(l):
    n = jnp.arange(2 * l)
    pos = jnp.where(n < l, n, 2 * l - n).astype(F32)[:, None]
    t = pos / max(l - 1, 1)
    w = 2.0 * math.pi * pos / l
    f = jnp.linspace(1e-4, FILT_BANDS - 1, FILT_BANDS, dtype=F32)[None, :]
    feat = jnp.concatenate([t, jnp.cos(f * w), -jnp.sin(f * w),
                            jnp.zeros((2 * l, LANES - FILT_EMB), F32)], axis=-1)
    half = min(FILT_TILE, 2 * l) // 2
    feat = feat.reshape(-1, 2, half, LANES).transpose(0, 2, 1, 3).reshape(l, 2 * LANES)
    return feat, t


def hyena_filter(l, tables, w1, b1, f1, w2, b2, f2, w3):
    feat, t = tables
    tr = min(FILT_TILE, 2 * l)
    hid = w2.shape[0]
    max_decay = math.log(DECAY_TARGET) / FAST_DECAY_PCT
    min_decay = math.log(DECAY_TARGET) / SLOW_DECAY_PCT
    deltas = jnp.linspace(min_decay, max_decay, HY_WIDTH, dtype=F32)[None, :]
    w1p = jnp.concatenate([w1, jnp.zeros((LANES - FILT_EMB, hid), F32)], 0)
    twice = lambda m: jax.scipy.linalg.block_diag(m, m)
    pair = lambda v: jnp.tile(v.reshape(1, hid), (1, 2))
    full = lambda a: pl.BlockSpec(a.shape, lambda i: (0,) * a.ndim)
    smalls = [twice(w1p), pair(b1), pair(f1), twice(w2), pair(b2), pair(f2), twice(w3), deltas]
    return pl.pallas_call(
        functools.partial(_filter_kernel, l=l, tr=tr),
        grid=(2 * l // tr,),
        in_specs=[pl.BlockSpec((tr // 2, 2 * LANES), lambda i: (i, 0)),
                  pl.BlockSpec((tr, 1), lambda i: (i, 0))] + [full(a) for a in smalls],
        out_specs=[pl.BlockSpec((tr, HY_WIDTH), lambda i: (i, 0)),
                   pl.BlockSpec((1, HY_WIDTH), lambda i: (0, 0))],
        out_shape=[jax.ShapeDtypeStruct((2 * l, HY_WIDTH), F32),
                   jax.ShapeDtypeStruct((1, HY_WIDTH), F32)],
        compiler_params=_cp("arbitrary"),
        name="hyena_filter",
    )(feat, t, *smalls)


def _dft_rows_kernel(f_ref, x_ref, o_ref):
    n1, nb, c = o_ref.shape[2:]
    xs = jnp.concatenate([x_ref[0, :, j, :] for j in range(nb)], axis=-1).astype(BF16)
    res = _dot(f_ref[...], xs)
    for p in range(2):
        for j in range(nb):
            o_ref[0, p, :, j, :] = res[p * n1:(p + 1) * n1, j * c:(j + 1) * c]


def dft_rows(fmat, x):
    b, k, n2, c = x.shape
    n1 = fmat.shape[0] // 2
    nb = min(DFT_COLS // c, n2)
    return pl.pallas_call(
        _dft_rows_kernel,
        grid=(b, n2 // nb),
        in_specs=[pl.BlockSpec((2 * n1, k), lambda bi, j: (0, 0)),
                  pl.BlockSpec((1, k, nb, c), lambda bi, j: (bi, 0, j, 0))],
        out_specs=pl.BlockSpec((1, 2, n1, nb, c), lambda bi, j: (bi, 0, 0, j, 0)),
        out_shape=jax.ShapeDtypeStruct((b, 2, n1, n2, c), F32),
        compiler_params=_cp("parallel", "parallel"),
        name="dft_rows",
    )(fmat, x)


SPEC_KB = 16


def _inner_dft(g, are, aim):
    n2 = are.shape[0]
    p1 = _dot(g, are)
    p2 = _dot(g, aim)
    return p1[:n2] - p2[n2:], p2[:n2] + p1[n2:]


def _spectrum_kernel(g_ref, a_ref, h_ref):
    for j in range(g_ref.shape[0]):
        h_ref[0, j], h_ref[1, j] = _inner_dft(g_ref[j], a_ref[0, 0, j].astype(BF16), a_ref[0, 1, j].astype(BF16))


def filter_spectrum(gs, a):
    _, _, n1, n2, c = a.shape
    kb = min(SPEC_KB, n1)
    return pl.pallas_call(
        _spectrum_kernel,
        grid=(n1 // kb,),
        in_specs=[pl.BlockSpec((kb, 2 * n2, n2), lambda k: (k, 0, 0)),
                  pl.BlockSpec((1, 2, kb, n2, c), lambda k: (0, 0, k, 0, 0))],
        out_specs=pl.BlockSpec((2, kb, n2, c), lambda k: (0, k, 0, 0)),
        out_shape=jax.ShapeDtypeStruct((2, n1, n2, c), F32),
        compiler_params=_cp("parallel"),
        name="filter_spectrum",
    )(gs, a)


def _spectral_kernel(g_ref, gt_ref, h_ref, a_ref, b_ref):
    n2 = a_ref.shape[3]
    for j in range(g_ref.shape[0]):
        xre, xim = _inner_dft(g_ref[j], a_ref[0, 0, j].astype(BF16), a_ref[0, 1, j].astype(BF16))
        hre, him = h_ref[0, j], h_ref[1, j]
        yre = (xre * hre - xim * him).astype(BF16)
        yim = (xre * him + xim * hre).astype(BF16)
        gt = gt_ref[j]
        q1 = _dot(gt, yre)
        q2 = _dot(gt, yim)
        b_ref[0, 0, j] = q1[:n2] + q2[n2:]
        b_ref[0, 1, j] = q2[:n2] - q1[n2:]


def spectral_multiply(gs, gts, hf, a):
    b, _, n1, n2, c = a.shape
    kb = min(SPEC_KB, n1)
    return pl.pallas_call(
        _spectral_kernel,
        grid=(n1 // kb, b),
        in_specs=[pl.BlockSpec((kb, 2 * n2, n2), lambda k, bi: (k, 0, 0)),
                  pl.BlockSpec((kb, 2 * n2, n2), lambda k, bi: (k, 0, 0)),
                  pl.BlockSpec((2, kb, n2, c), lambda k, bi: (0, k, 0, 0)),
                  pl.BlockSpec((1, 2, kb, n2, c), lambda k, bi: (bi, 0, k, 0, 0))],
        out_specs=pl.BlockSpec((1, 2, kb, n2, c), lambda k, bi: (bi, 0, k, 0, 0)),
        out_shape=jax.ShapeDtypeStruct(a.shape, F32),
        compiler_params=_cp("parallel", "arbitrary"),
        name="spectral_multiply",
    )(gs, gts, hf, a)


def _idft_rows_kernel(f_ref, bm_ref, x0_ref, z_ref, inv_ref, bias_ref, o_ref, *, inv_n):
    m, nb, c = o_ref.shape[1:]
    bflat = bm_ref[0].reshape(-1, c).astype(BF16)
    y = _dot(f_ref[...], bflat) * inv_n
    for j in range(nb):
        z = z_ref[0, :, j, :]
        o_ref[0, :, j, :] = x0_ref[0, :, j, :] * (y[j * m:(j + 1) * m] * inv_ref[...] + bias_ref[...] * z)


def idft_rows_gate(fmat, bm, x0, z, inv_norm, bias, n):
    b, _, n1, n2, c = bm.shape
    m = fmat.shape[0]
    nb = min(DFT_COLS // c, n2)
    spread = jnp.kron(fmat, jnp.eye(nb, dtype=fmat.dtype)).reshape(m, nb, 2 * n1 * nb)
    spread = spread.transpose(1, 0, 2).reshape(nb * m, 2 * n1 * nb)
    blk = pl.BlockSpec((1, m, nb, c), lambda bi, j: (bi, 0, j, 0))
    return pl.pallas_call(
        functools.partial(_idft_rows_kernel, inv_n=1.0 / n),
        grid=(b, n2 // nb),
        in_specs=[pl.BlockSpec(spread.shape, lambda bi, j: (0, 0), pipeline_mode=pl.Buffered(1)),
                  pl.BlockSpec((1, 2, n1, nb, c), lambda bi, j: (bi, 0, 0, j, 0)),
                  blk, blk,
                  pl.BlockSpec((1, c), lambda bi, j: (0, 0)),
                  pl.BlockSpec((1, c), lambda bi, j: (0, 0))],
        out_specs=blk,
        out_shape=jax.ShapeDtypeStruct((b, m, n2, c), F32),
        compiler_params=_cp("parallel", "parallel"),
        name="idft_rows_gate",
    )(spread, bm, x0, z, inv_norm, bias)


def dft_tables(l):
    n = 2 * l
    n2 = FFT_N2
    n1 = n // n2
    k1 = np.arange(n1)
    ang1 = 2.0 * np.pi * np.outer(k1, k1) / n1
    f1 = np.concatenate([np.cos(ang1), -np.sin(ang1)], 0)
    f3 = np.concatenate([np.cos(ang1), -np.sin(ang1)], 1)[: n1 // 2]
    m2 = np.arange(n2)
    tw = 2.0 * np.pi * np.outer(k1, m2) / n
    ang2 = 2.0 * np.pi * np.outer(m2, m2) / n2
    twr, twi = jnp.asarray(np.cos(tw), F32)[:, None, :], jnp.asarray(-np.sin(tw), F32)[:, None, :]
    f2r, f2i = jnp.asarray(np.cos(ang2), F32)[None], jnp.asarray(-np.sin(ang2), F32)[None]
    gre, gim = twr * f2r - twi * f2i, twr * f2i + twi * f2r
    gs = jnp.concatenate([gre, gim], 1).astype(BF16)
    gts = jnp.concatenate([gre.transpose(0, 2, 1), gim.transpose(0, 2, 1)], 1).astype(BF16)
    return dict(n=n, n1=n1, n2=n2, f1=jnp.asarray(f1, BF16), f3=jnp.asarray(f3, BF16), gs=gs, gts=gts)


def hyena_long_conv(z, x0, filt2, norm, bias, tabs):
    b, l, c = z.shape
    n, n1, n2 = tabs["n"], tabs["n1"], tabs["n2"]
    hf = filter_spectrum(tabs["gs"], dft_rows(tabs["f1"], filt2.reshape(1, n1, n2, c)))
    a = dft_rows(tabs["f1"][:, : n1 // 2], z.reshape(b, n1 // 2, n2, c))
    bm = spectral_multiply(tabs["gs"], tabs["gts"], hf, a)
    y = idft_rows_gate(tabs["f3"], bm, x0.reshape(b, n1 // 2, n2, c), z.reshape(b, n1 // 2, n2, c),
                       1.0 / norm, bias, n)
    return y.reshape(b, l, c)


def _small_conv_kernel(ff_ref, fi_ref, filt_ref, z_ref, x0_ref, inv_ref, bias_ref, o_ref, *, l):
    ff = ff_ref[...]
    hs = _dot(ff, filt_ref[...].astype(BF16))
    hre, him = hs[:2 * l], hs[2 * l:]
    z = z_ref[0]
    xs = _dot(ff[:, :l], z.astype(BF16))
    xre, xim = xs[:2 * l], xs[2 * l:]
    yre = (xre * hre - xim * him).astype(BF16)
    yim = (xre * him + xim * hre).astype(BF16)
    y = _dot(fi_ref[...], jnp.concatenate([yre, yim], 0)) * (0.5 / l)
    o_ref[0] = (x0_ref[0] * (y * inv_ref[...] + bias_ref[...] * z)).astype(o_ref.dtype)


def small_long_conv(z, x0, filt2, norm, bias):
    b, l, c = z.shape
    k = np.arange(2 * l)
    ang = 2.0 * np.pi * np.outer(k, k) / (2 * l)
    ff = jnp.asarray(np.concatenate([np.cos(ang), -np.sin(ang)], 0), BF16)
    fi = jnp.asarray(np.concatenate([np.cos(ang), -np.sin(ang)], 1)[:l], BF16)
    full = lambda a: pl.BlockSpec(a.shape, lambda bi: (0,) * a.ndim)
    row = pl.BlockSpec((1, l, c), lambda bi: (bi, 0, 0))
    inv = 1.0 / norm
    return pl.pallas_call(
        functools.partial(_small_conv_kernel, l=l),
        grid=(b,),
        in_specs=[full(ff), full(fi), full(filt2), row, row, full(inv), full(bias)],
        out_specs=row,
        out_shape=jax.ShapeDtypeStruct((b, l, c), BF16),
        compiler_params=_cp("parallel"),
        name="small_long_conv",
    )(ff, fi, filt2, z, x0, inv, bias)


def _pool_kernel(p_ref, pp_ref, pn_ref, w_ref, sc_ref, o_ref, buf_ref, *, tm, l):
    _fill_padded(buf_ref, p_ref, pp_ref, pn_ref, tm)
    t = pl.program_id(1) * tm + lax.broadcasted_iota(jnp.int32, (tm, LANES), 0)
    lane = lax.broadcasted_iota(jnp.int32, (tm, LANES), 1)
    first_group = lane < POOL_GROUP
    h0 = POOL_HALO

    def win(col, lo, hi):
        acc = buf_ref[h0 + lo:h0 + lo + tm, col]
        for j in range(lo + 1, hi):
            acc = acc + buf_ref[h0 + j:h0 + j + tm, col]
        return acc

    parts = []
    for half, (wa, wb) in enumerate(((2, 4), (8, 16))):
        col = slice(half * LANES, (half + 1) * LANES)
        sa = win(col, -wa // 2, wa // 2)
        sb = sa + win(col, -wb // 2, -wa // 2) + win(col, wa // 2, wb // 2)
        hw = jnp.where(first_group, wa // 2, wb // 2)
        cnt = jnp.minimum(t + hw, l) - jnp.maximum(t - hw, 0)
        mean = jnp.where(first_group, sa, sb) / cnt.astype(F32)
        parts.append(mean - buf_ref[h0:h0 + tm, col])
    d = jnp.concatenate(parts, axis=-1).astype(BF16)
    o_ref[0] = (_dot(d, w_ref[...]) * sc_ref[...]).astype(o_ref.dtype)


def pool_mix(p, pool_w, pool_scale):
    b, l, wd = p.shape
    tm = min(ROW_TILE, l)
    nblk = l // tm
    wblk = jax.scipy.linalg.block_diag(*[pool_w[g] for g in range(len(POOL_WINDOWS))]).astype(BF16)
    return pl.pallas_call(
        functools.partial(_pool_kernel, tm=tm, l=l),
        grid=(b, nblk),
        in_specs=_halo_specs(tm, wd, nblk) + [pl.BlockSpec((wd, wd), lambda bi, i: (0, 0)),
                                              pl.BlockSpec((1, wd), lambda bi, i: (0, 0))],
        out_specs=pl.BlockSpec((1, tm, wd), lambda bi, i: (bi, i, 0)),
        out_shape=jax.ShapeDtypeStruct((b, l, wd), BF16),
        scratch_shapes=[pltpu.VMEM((tm + 2 * POOL_HALO, wd), F32)],
        compiler_params=_cp("parallel", "parallel"),
        name="pool_mix",
    )(p, p, p, wblk, pool_scale.reshape(1, wd))


def _outproj_kernel(x_ref, hy_ref, pool_ref, att_ref, w_ref, gate_ref, o_ref):
    a, b2 = HY_WIDTH, HY_WIDTH + POOL_WIDTH
    y = (_dot(hy_ref[0].astype(BF16), w_ref[0:a]) + _dot(pool_ref[0], w_ref[a:b2]) + _dot(att_ref[0], w_ref[b2:]))
    o_ref[0] = x_ref[0] + gate_ref[0] * y


def out_projection(x, y_hy, y_pool, y_att, w_out, gate):
    b, l, d = x.shape
    tm = min(ROW_TILE, l)
    row = lambda bi, i: (bi, i, 0)
    return pl.pallas_call(
        _outproj_kernel,
        grid=(b, l // tm),
        in_specs=[pl.BlockSpec((1, tm, d), row),
                  pl.BlockSpec((1, tm, y_hy.shape[2]), row),
                  pl.BlockSpec((1, tm, y_pool.shape[2]), row),
                  pl.BlockSpec((1, tm, y_att.shape[2]), row),
                  pl.BlockSpec(w_out.shape, lambda bi, i: (0, 0)),
                  pl.BlockSpec((1, 1, d), lambda bi, i: (bi, 0, 0))],
        out_specs=pl.BlockSpec((1, tm, d), row),
        out_shape=jax.ShapeDtypeStruct(x.shape, F32),
        compiler_params=_cp("parallel", "parallel"),
        name="out_projection",
    )(x, y_hy, y_pool, y_att, w_out, gate)


FF_CHUNK = 256
MOE_TILE = 1024
MOE_ROWS = (128, 192, 256, 320, 384)


def _swiglu_acc(h, wg_ref, wu_ref, wd_ref, lead):
    ff = wg_ref.shape[-1]
    acc = None
    for f0 in range(0, ff, FF_CHUNK):
        cs = slice(f0, min(f0 + FF_CHUNK, ff))
        g = _dot(h, wg_ref[lead + (slice(None), cs)])
        u = _dot(h, wu_ref[lead + (slice(None), cs)])
        a = (g * jax.nn.sigmoid(g) * u).astype(BF16)
        part = _dot(a, wd_ref[lead + (cs, slice(None))])
        acc = part if acc is None else acc + part
    return acc


def _ffn_kernel(x_ref, g_ref, sh_ref, sc_ref, gate_ref, wg_ref, wu_ref, wd_ref, o_ref):
    x = x_ref[0]
    h = _normmod(x, g_ref[...], sh_ref[0], sc_ref[0]).astype(BF16)
    o_ref[0] = x + gate_ref[0] * _swiglu_acc(h, wg_ref, wu_ref, wd_ref, ())


def ffn_dense(x, g, shift, scale, gate, wg, wu, wd):
    b, l, d = x.shape
    tm = min(ROW_TILE, l)
    row = lambda bi, i: (bi, i, 0)
    vec = lambda bi, i: (bi, 0, 0)
    wspec = lambda w: pl.BlockSpec(w.shape, lambda bi, i: (0, 0), pipeline_mode=pl.Buffered(1))
    return pl.pallas_call(
        _ffn_kernel,
        grid=(b, l // tm),
        in_specs=[pl.BlockSpec((1, tm, d), row),
                  pl.BlockSpec((1, d), lambda bi, i: (0, 0)),
                  pl.BlockSpec((1, 1, d), vec), pl.BlockSpec((1, 1, d), vec), pl.BlockSpec((1, 1, d), vec),
                  wspec(wg), wspec(wu), wspec(wd)],
        out_specs=pl.BlockSpec((1, tm, d), row),
        out_shape=jax.ShapeDtypeStruct(x.shape, F32),
        compiler_params=_cp("parallel", "parallel"),
        name="ffn_dense",
    )(x, g, shift, scale, gate, wg, wu, wd)


def _mix_ffn_kernel(x_ref, hy_ref, pool_ref, att_ref, wo_ref, gate1_ref, g_ref, sh_ref, sc_ref, gate2_ref,
                    wg_ref, wu_ref, wd_ref, o_ref):
    a, b2 = HY_WIDTH, HY_WIDTH + POOL_WIDTH
    y = (_dot(hy_ref[0].astype(BF16), wo_ref[0:a]) + _dot(pool_ref[0], wo_ref[a:b2]) + _dot(att_ref[0], wo_ref[b2:]))
    x = x_ref[0] + gate1_ref[0] * y
    h = _normmod(x, g_ref[...], sh_ref[0], sc_ref[0]).astype(BF16)
    o_ref[0] = x + gate2_ref[0] * _swiglu_acc(h, wg_ref, wu_ref, wd_ref, ())


def mix_ffn(x, y_hy, y_pool, y_att, w_out, gate1, g, shift, scale, gate2, wg, wu, wd):
    b, l, d = x.shape
    tm = min(ROW_TILE, l)
    row = lambda bi, i: (bi, i, 0)
    vec = lambda bi, i: (bi, 0, 0)
    wspec = lambda w: pl.BlockSpec(w.shape, lambda bi, i: (0, 0), pipeline_mode=pl.Buffered(1))
    vspec = pl.BlockSpec((1, 1, d), vec)
    return pl.pallas_call(
        _mix_ffn_kernel,
        grid=(b, l // tm),
        in_specs=[pl.BlockSpec((1, tm, d), row),
                  pl.BlockSpec((1, tm, y_hy.shape[2]), row),
                  pl.BlockSpec((1, tm, y_pool.shape[2]), row),
                  pl.BlockSpec((1, tm, y_att.shape[2]), row),
                  wspec(w_out), vspec,
                  pl.BlockSpec((1, d), lambda bi, i: (0, 0)),
                  vspec, vspec, vspec,
                  wspec(wg), wspec(wu), wspec(wd)],
        out_specs=pl.BlockSpec((1, tm, d), row),
        out_shape=jax.ShapeDtypeStruct(x.shape, F32),
        compiler_params=_cp("parallel", "parallel"),
        name="mix_ffn",
    )(x, y_hy, y_pool, y_att, w_out, gate1, g, shift, scale, gate2, wg, wu, wd)


def _router_kernel(x_ref, g_ref, sh_ref, sc_ref, wr_ref, tri_ref, comb_ref, rank_ref):
    h = _normmod(x_ref[0], g_ref[...], sh_ref[0], sc_ref[0])
    wr = wr_ref[...]
    h_hi, w_hi = h.astype(BF16), wr.astype(BF16)
    h_lo, w_lo = (h - h_hi.astype(F32)).astype(BF16), (wr - w_hi.astype(F32)).astype(BF16)
    logits = _dot(h_hi, w_hi) + (_dot(h_lo, w_hi) + _dot(h_hi, w_lo))
    lane = lax.broadcasted_iota(jnp.int32, logits.shape, 1)
    neg = jnp.float32(-jnp.inf)
    lg = jnp.where(lane < N_EXPERTS, logits, neg)
    m1 = jnp.max(lg, axis=-1, keepdims=True)
    i1 = jnp.min(jnp.where(lg == m1, lane, LANES), axis=-1, keepdims=True)
    lg2 = jnp.where(lane == i1, neg, lg)
    m2 = jnp.max(lg2, axis=-1, keepdims=True)
    i2 = jnp.min(jnp.where(lg2 == m2, lane, LANES), axis=-1, keepdims=True)
    e = jnp.exp(m2 - m1)
    g1 = 1.0 / (1.0 + e)
    comb_ref[0] = jnp.where(lane == i1, g1, jnp.where(lane == i2, e * g1, 0.0))
    routed = (lane == i1) | (lane == i2)
    before = _dot(tri_ref[...], jnp.where(routed, 1.0, 0.0).astype(BF16))
    rank_ref[0] = jnp.where(routed, before, -1.0)


def moe_route(x, g, shift, scale, w_router):
    b, l, d = x.shape
    tm = min(MOE_TILE, l)
    wr = jnp.concatenate([w_router, jnp.zeros((d, LANES - N_EXPERTS), F32)], -1)
    tri = jnp.asarray(np.tril(np.ones((tm, tm), np.float32), -1), BF16)
    row = lambda bi, i: (bi, i, 0)
    vec = lambda bi, i: (bi, 0, 0)
    return pl.pallas_call(
        _router_kernel,
        grid=(b, l // tm),
        in_specs=[pl.BlockSpec((1, tm, d), row),
                  pl.BlockSpec((1, d), lambda bi, i: (0, 0)),
                  pl.BlockSpec((1, 1, d), vec), pl.BlockSpec((1, 1, d), vec),
                  pl.BlockSpec((d, LANES), lambda bi, i: (0, 0)),
                  pl.BlockSpec((tm, tm), lambda bi, i: (0, 0))],
        out_specs=[pl.BlockSpec((1, tm, LANES), row)] * 2,
        out_shape=[jax.ShapeDtypeStruct((b, l, LANES), F32)] * 2,
        compiler_params=_cp("parallel", "parallel"),
        name="moe_router",
    )(x, g, shift, scale, wr, tri)


def _moe_kernel(x_ref, g_ref, sh_ref, sc_ref, gate_ref, comb_ref, wg_ref, wu_ref, wd_ref, o_ref, acc_ref):
    e = pl.program_id(2)
    x = x_ref[0]
    h = _normmod(x, g_ref[...], sh_ref[0], sc_ref[0]).astype(BF16)
    comb = comb_ref[0]
    lane = lax.broadcasted_iota(jnp.int32, comb.shape, 1)
    ce = jnp.sum(jnp.where(lane == e, comb, 0.0), axis=-1, keepdims=True)
    y = ce * _swiglu_acc(h, wg_ref, wu_ref, wd_ref, (0, 0))

    @pl.when(e == 0)
    def _():
        acc_ref[...] = y

    @pl.when(e > 0)
    def _():
        acc_ref[...] += y

    @pl.when(e == N_EXPERTS - 1)
    def _():
        o_ref[0] = x + gate_ref[0] * acc_ref[...]


def moe_dense(x, g, shift, scale, gate, comb, wg, wu, wd, li):
    b, l, d = x.shape
    tm = min(ROW_TILE, l)
    ff = wg.shape[-1]
    row = lambda bi, i, e: (bi, i, 0)
    vec = lambda bi, i, e: (bi, 0, 0)
    return pl.pallas_call(
        _moe_kernel,
        grid=(b, l // tm, N_EXPERTS),
        in_specs=[pl.BlockSpec((1, tm, d), row),
                  pl.BlockSpec((1, d), lambda bi, i, e: (0, 0)),
                  pl.BlockSpec((1, 1, d), vec), pl.BlockSpec((1, 1, d), vec), pl.BlockSpec((1, 1, d), vec),
                  pl.BlockSpec((1, tm, LANES), row),
                  pl.BlockSpec((1, 1, d, ff), lambda bi, i, e: (li, e, 0, 0)),
                  pl.BlockSpec((1, 1, d, ff), lambda bi, i, e: (li, e, 0, 0)),
                  pl.BlockSpec((1, 1, ff, d), lambda bi, i, e: (li, e, 0, 0))],
        out_specs=pl.BlockSpec((1, tm, d), row),
        out_shape=jax.ShapeDtypeStruct(x.shape, F32),
        scratch_shapes=[pltpu.VMEM((tm, d), F32)],
        compiler_params=_cp("parallel", "parallel", "arbitrary"),
        name="moe_dense",
    )(x, g, shift, scale, gate, comb, wg, wu, wd)


def _moe_sparse_kernel(cnt_ref, x_ref, g_ref, sh_ref, sc_ref, gate_ref, comb_ref, rank_ref, rankt_ref,
                       wg_ref, wu_ref, wd_ref, *rest, t, sizes):
    fg_ref = rest[0] if len(rest) == 3 else None
    o_ref, h_scr = rest[-2:]
    i, e = pl.program_id(0), pl.program_id(1)
    cnt = cnt_ref[i * N_EXPERTS + e]

    @pl.when(e == 0)
    def _():
        x = x_ref[0]
        h_scr[...] = _normmod(x, g_ref[...], sh_ref[0], sc_ref[0]).astype(BF16)
        o_ref[0] = x

    rrow = rankt_ref[0, pl.ds(e, 1), :]
    lane = lax.broadcasted_iota(jnp.int32, (t, LANES), 1)
    rcol = jnp.sum(jnp.where(lane == e, rank_ref[0], 0.0), axis=-1, keepdims=True)
    ccol = jnp.sum(jnp.where(lane == e, comb_ref[0], 0.0), axis=-1, keepdims=True)

    def block(base, r):
        rid = (base + lax.broadcasted_iota(jnp.int32, (r, 1), 0)).astype(F32)
        onehot = jnp.where(rrow == rid, 1.0, 0.0).astype(BF16)
        xg = _dot(onehot, h_scr[...]).astype(BF16)
        y = _swiglu_acc(xg, wg_ref, wu_ref, wd_ref, (0, 0)).astype(BF16)
        cid = (base + lax.broadcasted_iota(jnp.int32, (1, r), 1)).astype(F32)
        onehot_t = jnp.where(rcol == cid, 1.0, 0.0).astype(BF16)
        o_ref[0] += (gate_ref[0] * ccol) * _dot(onehot_t, y)

    big = sizes[-1]
    nfull = cnt // big

    def full(k, carry):
        block(k * big, big)
        return carry
    lax.fori_loop(0, nfull, full, 0)
    base = nfull * big
    rem = cnt - base
    for lo, r in zip((0,) + sizes[:-1], sizes):
        @pl.when((rem > lo) & (rem <= r))
        def _(r=r):
            block(base, r)

    if fg_ref is not None:
        @pl.when(e == N_EXPERTS - 1)
        def _():
            o = o_ref[0]
            o_ref[0] = o * lax.rsqrt(jnp.mean(o * o, axis=-1, keepdims=True) + EPS) * fg_ref[...]


def moe_sparse(x, g, shift, scale, gate, comb, rank, wg, wu, wd, li, final_g=None):
    b, l, d = x.shape
    t = MOE_TILE
    tpb = l // t
    n_tiles = b * tpb
    ff = wg.shape[-1]
    rk = rank[..., :N_EXPERTS].reshape(n_tiles, t, N_EXPERTS)
    rank_t = rk.transpose(0, 2, 1)
    cnt = jnp.sum(rk >= 0, axis=1).astype(jnp.int32).reshape(n_tiles * N_EXPERTS)
    row = lambda i, e, c: (i // tpb, i % tpb, 0)
    vec = lambda i, e, c: (i // tpb, 0, 0)
    grid_spec = pltpu.PrefetchScalarGridSpec(
        num_scalar_prefetch=1,
        grid=(n_tiles, N_EXPERTS),
        in_specs=[pl.BlockSpec((1, t, d), row, pipeline_mode=pl.Buffered(1)),
                  pl.BlockSpec((1, d), lambda i, e, c: (0, 0)),
                  pl.BlockSpec((1, 1, d), vec), pl.BlockSpec((1, 1, d), vec), pl.BlockSpec((1, 1, d), vec),
                  pl.BlockSpec((1, t, LANES), row, pipeline_mode=pl.Buffered(1)),
                  pl.BlockSpec((1, t, LANES), row, pipeline_mode=pl.Buffered(1)),
                  pl.BlockSpec((1, N_EXPERTS, t), lambda i, e, c: (i, 0, 0)),
                  pl.BlockSpec((1, 1, d, ff), lambda i, e, c: (li, e, 0, 0)),
                  pl.BlockSpec((1, 1, d, ff), lambda i, e, c: (li, e, 0, 0)),
                  pl.BlockSpec((1, 1, ff, d), lambda i, e, c: (li, e, 0, 0))]
        + ([] if final_g is None else [pl.BlockSpec((1, d), lambda i, e, c: (0, 0))]),
        out_specs=pl.BlockSpec((1, t, d), row),
        scratch_shapes=[pltpu.VMEM((t, d), BF16)],
    )
    return pl.pallas_call(
        functools.partial(_moe_sparse_kernel, t=t, sizes=MOE_ROWS),
        grid_spec=grid_spec,
        out_shape=jax.ShapeDtypeStruct(x.shape, F32),
        compiler_params=_cp("parallel", "arbitrary"),
        name="moe_sparse",
    )(cnt, x, g, shift, scale, gate, comb, rank, rank_t, wg, wu, wd, *(() if final_g is None else (final_g,)))


def _final_kernel(x_ref, g_ref, o_ref):
    x = x_ref[0]
    o_ref[0] = x * lax.rsqrt(jnp.mean(x * x, axis=-1, keepdims=True) + EPS) * g_ref[...]


def final_norm(x, g):
    b, l, d = x.shape
    tm = min(2 * ROW_TILE, l)
    return pl.pallas_call(
        _final_kernel,
        grid=(b, l // tm),
        in_specs=[pl.BlockSpec((1, tm, d), lambda bi, i: (bi, i, 0)),
                  pl.BlockSpec((1, d), lambda bi, i: (0, 0))],
        out_specs=pl.BlockSpec((1, tm, d), lambda bi, i: (bi, i, 0)),
        out_shape=jax.ShapeDtypeStruct(x.shape, F32),
        compiler_params=_cp("parallel", "parallel"),
        name="final_norm",
    )(x, g)


def kernel(x, c, ctx, c_ctx, norm1_g, norm2_g, w_ada, b_ada, w_in, hy_short_w, hy_short_b, filt_w1, filt_b1, filt_freq1, filt_w2, filt_b2, filt_freq2, filt_w3, hy_bias, pool_w, pool_scale, q_norm_g, w_uq, kv_norm_g, w_ukv, w_out, ffn_wg, ffn_wu, ffn_wd, moe_router, moe_wg, moe_wu, moe_wd, final_g):
    b, l, d = x.shape
    lc = ctx.shape[1]
    depth = w_ada.shape[0]

    cc = jnp.zeros((8, d), F32).at[:b].set(c).at[b].set(c_ctx)
    mods = ada_vectors(cc, w_ada, b_ada)

    rope_l = rope_tables(l, True)
    rope_c = rope_tables(lc, False)
    ftab_l, ftab_c = filter_tables(l), filter_tables(lc)
    dtabs = dft_tables(l)
    moe_w = (moe_wg.astype(BF16), moe_wu.astype(BF16), moe_wd.astype(BF16))

    xl, xc = x, ctx
    for layer in range(depth):
        last = layer == depth - 1
        mod = mods[layer].reshape(8, 6, 1, d)
        mod_l = [mod[:b, j] for j in range(6)]
        mod_c = [jnp.broadcast_to(mod[b:b + 1, j], (b, 1, d)) for j in range(6)]
        g1 = norm1_g[layer].reshape(1, d)
        g2 = norm2_g[layer].reshape(1, d)
        w_in_l = w_in[layer].astype(BF16)
        w_out_l = w_out[layer].astype(BF16)
        mw = mla_weights(q_norm_g[layer], w_uq[layer], kv_norm_g[layer], w_ukv[layer])
        fw = (filt_w1[layer], filt_b1[layer], filt_freq1[layer], filt_w2[layer], filt_b2[layer],
              filt_freq2[layer], filt_w3[layer])
        bias = hy_bias[layer].reshape(1, HY_WIDTH)

        def channel_mix(xs, md, final=False):
            i = layer // 2
            fg = final_g.reshape(1, d)
            if layer % 2 == 1 and xs.shape[1] % MOE_TILE == 0:
                comb, rank = moe_route(xs, g2, md[3], md[4], moe_router[i])
                return moe_sparse(xs, g2, md[3], md[4], md[5], comb, rank, *moe_w, i, fg if final else None)
            if layer % 2 == 0:
                out = ffn_dense(xs, g2, md[3], md[4], md[5], ffn_wg[i].astype(BF16),
                                ffn_wu[i].astype(BF16), ffn_wd[i].astype(BF16))
            else:
                comb, _ = moe_route(xs, g2, md[3], md[4], moe_router[i])
                out = moe_dense(xs, g2, md[3], md[4], md[5], comb, *moe_w, i)
            return final_norm(out, fg) if final else out

        hy_l, pool_l, q_l, kt_l, v_l = in_projection(xl, g1, mod_l[0], mod_l[1], w_in_l, *rope_l, mw)
        hy_c, pool_c, q_c, kt_c, v_c = in_projection(xc, g1, mod_c[0], mod_c[1], w_in_l, *rope_c, mw)

        att_l = attention(q_l, kt_c, v_c, kt_l, v_l)
        z_l, x0_l = hyena_pre(hy_l, hy_short_w[layer], hy_short_b[layer])
        filt_l, nrm_l = hyena_filter(l, ftab_l, *fw)
        yhy_l = hyena_long_conv(z_l, x0_l, filt_l, nrm_l, bias, dtabs)
        ypool_l = pool_mix(pool_l, pool_w[layer], pool_scale[layer])
        dense = layer % 2 == 0
        if dense:
            fw3 = tuple(w[layer // 2].astype(BF16) for w in (ffn_wg, ffn_wu, ffn_wd))
            mix = lambda xs, ys, md: mix_ffn(xs, *ys, w_out_l, md[2], g2, md[3], md[4], md[5], *fw3)
        else:
            xl = out_projection(xl, yhy_l, ypool_l, att_l, w_out_l, mod_l[2])

        if not last:
            att_c = attention(q_c, kt_c, v_c)
            z_c, x0_c = hyena_pre(hy_c, hy_short_w[layer], hy_short_b[layer])
            filt_c, nrm_c = hyena_filter(lc, ftab_c, *fw)
            yhy_c = small_long_conv(z_c, x0_c, filt_c, nrm_c, bias)
            ypool_c = pool_mix(pool_c, pool_w[layer], pool_scale[layer])
            if dense:
                xc = mix(xc, (yhy_c, ypool_c, att_c), mod_c)
            else:
                xc = out_projection(xc, yhy_c, ypool_c, att_c, w_out_l, mod_c[2])
                xc = channel_mix(xc, mod_c)

        if dense:
            xl = mix(xl, (yhy_l, ypool_l, att_l), mod_l)
            xl = final_norm(xl, final_g.reshape(1, d)) if last else xl
        else:
            xl = channel_mix(xl, mod_l, final=last)
    return xl
```
